```python
import jax, jax.numpy as jnp
from jax import lax
import numpy as np

D_MODEL = 1024
BATCH = 32
SEQ = 2048
DEPTH = 2

CHUNK = 64
D_MIX = D_MODEL
D_CONV = D_MIX // 4
CONV_WIDTH = 3
D_FOX = D_MIX // 2
FOX_HEAD_DIM = 64
N_FOX_HEADS = D_FOX // FOX_HEAD_DIM
FOX_BLOCK = 128
D_SGU = D_MIX - D_CONV - D_FOX
N_SGU_GROUPS = 4
SGU_GROUP_DIM = D_SGU // N_SGU_GROUPS
SGU_CHUNK = 128
D_FF = 2816
ALPHA = (2 * DEPTH) ** 0.25
BETA = (8 * DEPTH) ** -0.25
LN_EPS = 1e-5
SPLIT_SIZES = (D_CONV, D_CONV, D_CONV, D_FOX, D_FOX, D_FOX, N_FOX_HEADS, D_SGU, D_SGU)
D_IN = 3 * D_CONV + 3 * D_FOX + N_FOX_HEADS + 2 * D_SGU

kernel_name = "hybrid_conv_fox_sgu_deepnorm_macaron"


def layer_norm(x, g, b):
    xf = x.astype(jnp.float32)
    mu = jnp.mean(xf, axis=-1, keepdims=True)
    xc = xf - mu
    var = jnp.mean(xc * xc, axis=-1, keepdims=True)
    y = xc * lax.rsqrt(var + LN_EPS) * g.astype(jnp.float32) + b.astype(jnp.float32)
    return y.astype(x.dtype)


def swiglu(x, w_up, w_down):
    gate, up = jnp.split(x @ w_up, 2, axis=-1)
    return (jax.nn.silu(gate) * up) @ w_down


def short_conv(gate_b, gate_c, h, w_conv):
    z = gate_c * h
    y = lax.conv_general_dilated(
        z, w_conv[:, None, :].astype(z.dtype),
        window_strides=(1,), padding=[(CONV_WIDTH - 1, 0)],
        dimension_numbers=("NWC", "WIO", "NWC"),
        feature_group_count=D_CONV)
    return gate_b * y


def forgetting_attention(q, k, v, f_logit):
    seq = q.shape[1]
    scale = FOX_HEAD_DIM ** -0.5
    log_f = jax.nn.log_sigmoid(f_logit.astype(jnp.float32))
    cum = jnp.transpose(jnp.cumsum(log_f, axis=1), (0, 2, 1))
    qf = q.astype(jnp.float32) * scale
    kf = k.astype(jnp.float32)
    vf = v.astype(jnp.float32)
    outs = []
    for i in range(seq // FOX_BLOCK):
        lo, hi = i * FOX_BLOCK, (i + 1) * FOX_BLOCK
        s = jnp.einsum('bqhd,bkhd->bhqk', qf[:, lo:hi], kf[:, :hi])
        s = s + cum[:, :, lo:hi, None] - cum[:, :, None, :hi]
        mask = jnp.arange(hi)[None, :] <= jnp.arange(lo, hi)[:, None]
        p = jax.nn.softmax(jnp.where(mask, s, -jnp.inf), axis=-1)
        outs.append(jnp.einsum('bhqk,bkhd->bqhd', p, vf[:, :hi]))
    return jnp.concatenate(outs, axis=1).astype(v.dtype)


def spatial_gating(u, v, ln_g, ln_b, w_s, b_s):
    bsz, seq, _ = v.shape
    u = jax.nn.gelu(u)
    v = layer_norm(jax.nn.gelu(v), ln_g, ln_b)
    vg = v.reshape(bsz, seq // SGU_CHUNK, SGU_CHUNK, N_SGU_GROUPS, SGU_GROUP_DIM)
    causal = jnp.tril(jnp.ones((SGU_CHUNK, SGU_CHUNK), dtype=w_s.dtype))
    mixed = jnp.einsum('gts,bnsgc->bntgc', w_s * causal, vg) + jnp.transpose(b_s)[:, :, None]
    return u * mixed.reshape(bsz, seq, D_SGU)


def hybrid_mixer(x, w_in, b_f, w_conv, sgu_ln_g, sgu_ln_b, w_s, b_s, w_out):
    bsz, seq, _ = x.shape
    offsets = []
    acc = 0
    for n in SPLIT_SIZES[:-1]:
        acc += n
        offsets.append(acc)
    proj = x @ w_in
    cb, cc, ch, q, k, v, f_logit, su, sv = jnp.split(proj, offsets, axis=-1)
    y_a = short_conv(cb, cc, ch, w_conv)
    heads = (bsz, seq, N_FOX_HEADS, FOX_HEAD_DIM)
    y_b = forgetting_attention(q.reshape(heads), k.reshape(heads), v.reshape(heads),
                               f_logit + b_f).reshape(bsz, seq, D_FOX)
    y_c = spatial_gating(su, sv, sgu_ln_g, sgu_ln_b, w_s, b_s)
    return jnp.concatenate([y_a, y_b, y_c], axis=-1) @ w_out


def _fwd_setup_inputs(seed: int = 0) -> dict:
    key = jax.random.key(seed)
    ks = jax.random.split(key, 20)
    nrm = lambda k, shape, s: jax.random.normal(k, shape, jnp.float32) * s
    L, D = DEPTH, D_MODEL
    return {
        "x": nrm(ks[0], (BATCH, SEQ, D), 1.0),
        "ln1_g": 1.0 + nrm(ks[1], (L, D), 0.05),
        "ln1_b": nrm(ks[2], (L, D), 0.02),
        "ffn1_w_up": nrm(ks[3], (L, D, 2 * D_FF), D ** -0.5),
        "ffn1_w_down": nrm(ks[4], (L, D_FF, D), BETA * D_FF ** -0.5),
        "mix_w_in": nrm(ks[5], (L, D, D_IN), D ** -0.5),
        "fox_b_f": 4.0 + nrm(ks[6], (L, N_FOX_HEADS), 0.5),
        "conv_w": nrm(ks[7], (L, CONV_WIDTH, D_CONV), CONV_WIDTH ** -0.5),
        "sgu_ln_g": 1.0 + nrm(ks[8], (L, D_SGU), 0.05),
        "sgu_ln_b": nrm(ks[9], (L, D_SGU), 0.02),
        "sgu_w_s": nrm(ks[10], (L, N_SGU_GROUPS, SGU_CHUNK, SGU_CHUNK), SGU_CHUNK ** -0.5),
        "sgu_b_s": 1.0 + nrm(ks[11], (L, N_SGU_GROUPS, SGU_CHUNK), 0.02),
        "mix_w_out": nrm(ks[12], (L, D_MIX, D), BETA * D_MIX ** -0.5),
        "ln2_g": 1.0 + nrm(ks[13], (L, D), 0.05),
        "ln2_b": nrm(ks[14], (L, D), 0.02),
        "ffn2_w_up": nrm(ks[15], (L, D, 2 * D_FF), D ** -0.5),
        "ffn2_w_down": nrm(ks[16], (L, D_FF, D), BETA * D_FF ** -0.5),
        "ln3_g": 1.0 + nrm(ks[17], (L, D), 0.05),
        "ln3_b": nrm(ks[18], (L, D), 0.02),
    }


def _fwd_reference(x, ln1_g, ln1_b, ffn1_w_up, ffn1_w_down, mix_w_in, fox_b_f, conv_w,
              sgu_ln_g, sgu_ln_b, sgu_w_s, sgu_b_s, mix_w_out, ln2_g, ln2_b,
              ffn2_w_up, ffn2_w_down, ln3_g, ln3_b):
    for l in range(DEPTH):
        x = layer_norm(ALPHA * x + 0.5 * swiglu(x, ffn1_w_up[l], ffn1_w_down[l]), ln1_g[l], ln1_b[l])
        mix = hybrid_mixer(x, mix_w_in[l], fox_b_f[l], conv_w[l], sgu_ln_g[l], sgu_ln_b[l],
                           sgu_w_s[l], sgu_b_s[l], mix_w_out[l])
        x = layer_norm(ALPHA * x + mix, ln2_g[l], ln2_b[l])
        x = layer_norm(ALPHA * x + 0.5 * swiglu(x, ffn2_w_up[l], ffn2_w_down[l]), ln3_g[l], ln3_b[l])
    return x


import jax as _jax
import jax.numpy as _jnp

TWIN_FORMAT = 'train_step'
FWD_PARAMS = ['x', 'ln1_g', 'ln1_b', 'ffn1_w_up', 'ffn1_w_down', 'mix_w_in', 'fox_b_f', 'conv_w', 'sgu_ln_g', 'sgu_ln_b', 'sgu_w_s', 'sgu_b_s', 'mix_w_out', 'ln2_g', 'ln2_b', 'ffn2_w_up', 'ffn2_w_down', 'ln3_g', 'ln3_b']
TWIN_WEIGHTS = ['ln1_g', 'ln1_b', 'ffn1_w_up', 'ffn1_w_down', 'mix_w_in', 'fox_b_f', 'conv_w', 'sgu_ln_g', 'sgu_ln_b', 'sgu_w_s', 'sgu_b_s', 'mix_w_out', 'ln2_g', 'ln2_b', 'ffn2_w_up', 'ffn2_w_down', 'ln3_g', 'ln3_b']
TWIN_DIFF_INPUT = 'x'
TWIN_INPUTS = ['x', 'ln1_g', 'ln1_b', 'ffn1_w_up', 'ffn1_w_down', 'mix_w_in', 'fox_b_f', 'conv_w', 'sgu_ln_g', 'sgu_ln_b', 'sgu_w_s', 'sgu_b_s', 'mix_w_out', 'ln2_g', 'ln2_b', 'ffn2_w_up', 'ffn2_w_down', 'ln3_g', 'ln3_b', 'loss_target', 'm_ln1_g', 'm_ln1_b', 'm_ffn1_w_up', 'm_ffn1_w_down', 'm_mix_w_in', 'm_fox_b_f', 'm_conv_w', 'm_sgu_ln_g', 'm_sgu_ln_b', 'm_sgu_w_s', 'm_sgu_b_s', 'm_mix_w_out', 'm_ln2_g', 'm_ln2_b', 'm_ffn2_w_up', 'm_ffn2_w_down', 'm_ln3_g', 'm_ln3_b', 'v_ln1_g', 'v_ln1_b', 'v_ffn1_w_up', 'v_ffn1_w_down', 'v_mix_w_in', 'v_fox_b_f', 'v_conv_w', 'v_sgu_ln_g', 'v_sgu_ln_b', 'v_sgu_w_s', 'v_sgu_b_s', 'v_mix_w_out', 'v_ln2_g', 'v_ln2_b', 'v_ffn2_w_up', 'v_ffn2_w_down', 'v_ln3_g', 'v_ln3_b']
TWIN_OUTPUTS = ['loss', 'grad_x', 'grad_ln1_g', 'grad_ln1_b', 'grad_ffn1_w_up', 'grad_ffn1_w_down', 'grad_mix_w_in', 'grad_fox_b_f', 'grad_conv_w', 'grad_sgu_ln_g', 'grad_sgu_ln_b', 'grad_sgu_w_s', 'grad_sgu_b_s', 'grad_mix_w_out', 'grad_ln2_g', 'grad_ln2_b', 'grad_ffn2_w_up', 'grad_ffn2_w_down', 'grad_ln3_g', 'grad_ln3_b', 'delta_ln1_g', 'delta_ln1_b', 'delta_ffn1_w_up', 'delta_ffn1_w_down', 'delta_mix_w_in', 'delta_fox_b_f', 'delta_conv_w', 'delta_sgu_ln_g', 'delta_sgu_ln_b', 'delta_sgu_w_s', 'delta_sgu_b_s', 'delta_mix_w_out', 'delta_ln2_g', 'delta_ln2_b', 'delta_ffn2_w_up', 'delta_ffn2_w_down', 'delta_ln3_g', 'delta_ln3_b', 'new_m_ln1_g', 'new_m_ln1_b', 'new_m_ffn1_w_up', 'new_m_ffn1_w_down', 'new_m_mix_w_in', 'new_m_fox_b_f', 'new_m_conv_w', 'new_m_sgu_ln_g', 'new_m_sgu_ln_b', 'new_m_sgu_w_s', 'new_m_sgu_b_s', 'new_m_mix_w_out', 'new_m_ln2_g', 'new_m_ln2_b', 'new_m_ffn2_w_up', 'new_m_ffn2_w_down', 'new_m_ln3_g', 'new_m_ln3_b', 'new_v_ln1_g', 'new_v_ln1_b', 'new_v_ffn1_w_up', 'new_v_ffn1_w_down', 'new_v_mix_w_in', 'new_v_fox_b_f', 'new_v_conv_w', 'new_v_sgu_ln_g', 'new_v_sgu_ln_b', 'new_v_sgu_w_s', 'new_v_sgu_b_s', 'new_v_mix_w_out', 'new_v_ln2_g', 'new_v_ln2_b', 'new_v_ffn2_w_up', 'new_v_ffn2_w_down', 'new_v_ln3_g', 'new_v_ln3_b']
TWIN_LEAF_KINDS = {'loss': 'loss', 'grad_x': 'grad_x', 'grad_ln1_g': 'grad_w', 'grad_ln1_b': 'grad_w', 'grad_ffn1_w_up': 'grad_w', 'grad_ffn1_w_down': 'grad_w', 'grad_mix_w_in': 'grad_w', 'grad_fox_b_f': 'grad_w', 'grad_conv_w': 'grad_w', 'grad_sgu_ln_g': 'grad_w', 'grad_sgu_ln_b': 'grad_w', 'grad_sgu_w_s': 'grad_w', 'grad_sgu_b_s': 'grad_w', 'grad_mix_w_out': 'grad_w', 'grad_ln2_g': 'grad_w', 'grad_ln2_b': 'grad_w', 'grad_ffn2_w_up': 'grad_w', 'grad_ffn2_w_down': 'grad_w', 'grad_ln3_g': 'grad_w', 'grad_ln3_b': 'grad_w', 'delta_ln1_g': 'delta_w', 'delta_ln1_b': 'delta_w', 'delta_ffn1_w_up': 'delta_w', 'delta_ffn1_w_down': 'delta_w', 'delta_mix_w_in': 'delta_w', 'delta_fox_b_f': 'delta_w', 'delta_conv_w': 'delta_w', 'delta_sgu_ln_g': 'delta_w', 'delta_sgu_ln_b': 'delta_w', 'delta_sgu_w_s': 'delta_w', 'delta_sgu_b_s': 'delta_w', 'delta_mix_w_out': 'delta_w', 'delta_ln2_g': 'delta_w', 'delta_ln2_b': 'delta_w', 'delta_ffn2_w_up': 'delta_w', 'delta_ffn2_w_down': 'delta_w', 'delta_ln3_g': 'delta_w', 'delta_ln3_b': 'delta_w', 'new_m_ln1_g': 'new_m', 'new_m_ln1_b': 'new_m', 'new_m_ffn1_w_up': 'new_m', 'new_m_ffn1_w_down': 'new_m', 'new_m_mix_w_in': 'new_m', 'new_m_fox_b_f': 'new_m', 'new_m_conv_w': 'new_m', 'new_m_sgu_ln_g': 'new_m', 'new_m_sgu_ln_b': 'new_m', 'new_m_sgu_w_s': 'new_m', 'new_m_sgu_b_s': 'new_m', 'new_m_mix_w_out': 'new_m', 'new_m_ln2_g': 'new_m', 'new_m_ln2_b': 'new_m', 'new_m_ffn2_w_up': 'new_m', 'new_m_ffn2_w_down': 'new_m', 'new_m_ln3_g': 'new_m', 'new_m_ln3_b': 'new_m', 'new_v_ln1_g': 'new_v', 'new_v_ln1_b': 'new_v', 'new_v_ffn1_w_up': 'new_v', 'new_v_ffn1_w_down': 'new_v', 'new_v_mix_w_in': 'new_v', 'new_v_fox_b_f': 'new_v', 'new_v_conv_w': 'new_v', 'new_v_sgu_ln_g': 'new_v', 'new_v_sgu_ln_b': 'new_v', 'new_v_sgu_w_s': 'new_v', 'new_v_sgu_b_s': 'new_v', 'new_v_mix_w_out': 'new_v', 'new_v_ln2_g': 'new_v', 'new_v_ln2_b': 'new_v', 'new_v_ffn2_w_up': 'new_v', 'new_v_ffn2_w_down': 'new_v', 'new_v_ln3_g': 'new_v', 'new_v_ln3_b': 'new_v'}


def _forward(args):
    return _fwd_reference(*[args[k] for k in FWD_PARAMS])


def _output_shape():
    out = _jax.eval_shape(lambda: _forward(_fwd_setup_inputs(0)))
    return out.shape, out.dtype

N_MICROBATCH = 1
ADAM_LR = 0.001
ADAM_B1 = 0.9
ADAM_B2 = 0.999
ADAM_EPS = 1e-08
ADAM_WD = 0.01
ADAM_STEP = 10
PER_EXAMPLE_BATCH_AXIS = {'x': 0, 'loss_target': 0}
SHARED_INPUTS = []
_WEIGHT_DTYPES = {'ln1_g': _jnp.float32, 'ln1_b': _jnp.float32, 'ffn1_w_up': _jnp.float32, 'ffn1_w_down': _jnp.float32, 'mix_w_in': _jnp.float32, 'fox_b_f': _jnp.float32, 'conv_w': _jnp.float32, 'sgu_ln_g': _jnp.float32, 'sgu_ln_b': _jnp.float32, 'sgu_w_s': _jnp.float32, 'sgu_b_s': _jnp.float32, 'mix_w_out': _jnp.float32, 'ln2_g': _jnp.float32, 'ln2_b': _jnp.float32, 'ffn2_w_up': _jnp.float32, 'ffn2_w_down': _jnp.float32, 'ln3_g': _jnp.float32, 'ln3_b': _jnp.float32}
MOMENT_SCALE = {'ln1_g': 5.541732e+00, 'ln1_b': 8.531070e-01, 'ffn1_w_up': 1.730347e-02, 'ffn1_w_down': 5.647102e-02, 'mix_w_in': 5.518039e-02, 'fox_b_f': 2.802314e-01, 'conv_w': 9.919512e-02, 'sgu_ln_g': 4.331034e-02, 'sgu_ln_b': 4.996692e-02, 'sgu_w_s': 2.922382e-02, 'sgu_b_s': 4.348274e-02, 'mix_w_out': 1.285755e-01, 'ln2_g': 5.913801e+00, 'ln2_b': 8.479988e-01, 'ffn2_w_up': 1.671224e-02, 'ffn2_w_down': 5.461322e-02, 'ln3_g': 4.637063e+01, 'ln3_b': 3.339920e+00}


def _to_microbatches(a, axis):
    t = _jnp.moveaxis(a, axis, 0)
    t = t.reshape((N_MICROBATCH, t.shape[0] // N_MICROBATCH) + t.shape[1:])
    return _jnp.moveaxis(t, 1, axis + 1)


def setup_inputs(seed: int = 0) -> dict:
    inp = _fwd_setup_inputs(seed)
    key = _jax.random.fold_in(_jax.random.key(seed), 7919)
    shape, _ = _output_shape()
    out = dict(inp)
    out["loss_target"] = _jax.random.normal(_jax.random.fold_in(key, 0), shape, _jnp.float32)
    for i, name in enumerate(TWIN_WEIGHTS):
        w = inp[name].astype(_jnp.float32)
        if MOMENT_SCALE is None:
            s = _jnp.sqrt(_jnp.mean(_jnp.square(w)) + 1e-30)
        else:
            s = MOMENT_SCALE[name]
        km, kv = _jax.random.split(_jax.random.fold_in(key, i + 1))
        out[name] = w
        out["m_" + name] = s * _jax.random.normal(km, w.shape, _jnp.float32)
        out["v_" + name] = (s * s) * _jax.random.uniform(kv, w.shape, _jnp.float32, 0.5, 1.5)
    if N_MICROBATCH > 1:
        for name, axis in PER_EXAMPLE_BATCH_AXIS.items():
            out[name] = _to_microbatches(out[name], axis)
    return {'x': out['x'], 'ln1_g': out['ln1_g'], 'ln1_b': out['ln1_b'], 'ffn1_w_up': out['ffn1_w_up'], 'ffn1_w_down': out['ffn1_w_down'], 'mix_w_in': out['mix_w_in'], 'fox_b_f': out['fox_b_f'], 'conv_w': out['conv_w'], 'sgu_ln_g': out['sgu_ln_g'], 'sgu_ln_b': out['sgu_ln_b'], 'sgu_w_s': out['sgu_w_s'], 'sgu_b_s': out['sgu_b_s'], 'mix_w_out': out['mix_w_out'], 'ln2_g': out['ln2_g'], 'ln2_b': out['ln2_b'], 'ffn2_w_up': out['ffn2_w_up'], 'ffn2_w_down': out['ffn2_w_down'], 'ln3_g': out['ln3_g'], 'ln3_b': out['ln3_b'], 'loss_target': out['loss_target'], 'm_ln1_g': out['m_ln1_g'], 'm_ln1_b': out['m_ln1_b'], 'm_ffn1_w_up': out['m_ffn1_w_up'], 'm_ffn1_w_down': out['m_ffn1_w_down'], 'm_mix_w_in': out['m_mix_w_in'], 'm_fox_b_f': out['m_fox_b_f'], 'm_conv_w': out['m_conv_w'], 'm_sgu_ln_g': out['m_sgu_ln_g'], 'm_sgu_ln_b': out['m_sgu_ln_b'], 'm_sgu_w_s': out['m_sgu_w_s'], 'm_sgu_b_s': out['m_sgu_b_s'], 'm_mix_w_out': out['m_mix_w_out'], 'm_ln2_g': out['m_ln2_g'], 'm_ln2_b': out['m_ln2_b'], 'm_ffn2_w_up': out['m_ffn2_w_up'], 'm_ffn2_w_down': out['m_ffn2_w_down'], 'm_ln3_g': out['m_ln3_g'], 'm_ln3_b': out['m_ln3_b'], 'v_ln1_g': out['v_ln1_g'], 'v_ln1_b': out['v_ln1_b'], 'v_ffn1_w_up': out['v_ffn1_w_up'], 'v_ffn1_w_down': out['v_ffn1_w_down'], 'v_mix_w_in': out['v_mix_w_in'], 'v_fox_b_f': out['v_fox_b_f'], 'v_conv_w': out['v_conv_w'], 'v_sgu_ln_g': out['v_sgu_ln_g'], 'v_sgu_ln_b': out['v_sgu_ln_b'], 'v_sgu_w_s': out['v_sgu_w_s'], 'v_sgu_b_s': out['v_sgu_b_s'], 'v_mix_w_out': out['v_mix_w_out'], 'v_ln2_g': out['v_ln2_g'], 'v_ln2_b': out['v_ln2_b'], 'v_ffn2_w_up': out['v_ffn2_w_up'], 'v_ffn2_w_down': out['v_ffn2_w_down'], 'v_ln3_g': out['v_ln3_g'], 'v_ln3_b': out['v_ln3_b']}


def _loss(weights, diff, rest, loss_target):
    with _jax.named_scope("forward"):
        args = {**rest, TWIN_DIFF_INPUT: diff, **{k: w.astype(_WEIGHT_DTYPES[k]) for k, w in weights.items()}}
        y = _forward(args)
    with _jax.named_scope("loss_head"):
        err = _jnp.square(y.astype(_jnp.float32) - loss_target)
        return 0.5 * _jnp.sum(_jnp.mean(err, axis=-1)) if err.ndim else 0.5 * err


def _adamw(w, g, m, v):
    m = ADAM_B1 * m + (1.0 - ADAM_B1) * g
    v = ADAM_B2 * v + (1.0 - ADAM_B2) * _jnp.square(g)
    m_hat = m / (1.0 - ADAM_B1 ** ADAM_STEP)
    v_hat = v / (1.0 - ADAM_B2 ** ADAM_STEP)
    delta = -ADAM_LR * (m_hat / (_jnp.sqrt(v_hat) + ADAM_EPS) + ADAM_WD * w)
    return delta, m, v


def reference(x, ln1_g, ln1_b, ffn1_w_up, ffn1_w_down, mix_w_in, fox_b_f, conv_w, sgu_ln_g, sgu_ln_b, sgu_w_s, sgu_b_s, mix_w_out, ln2_g, ln2_b, ffn2_w_up, ffn2_w_down, ln3_g, ln3_b, loss_target, m_ln1_g, m_ln1_b, m_ffn1_w_up, m_ffn1_w_down, m_mix_w_in, m_fox_b_f, m_conv_w, m_sgu_ln_g, m_sgu_ln_b, m_sgu_w_s, m_sgu_b_s, m_mix_w_out, m_ln2_g, m_ln2_b, m_ffn2_w_up, m_ffn2_w_down, m_ln3_g, m_ln3_b, v_ln1_g, v_ln1_b, v_ffn1_w_up, v_ffn1_w_down, v_mix_w_in, v_fox_b_f, v_conv_w, v_sgu_ln_g, v_sgu_ln_b, v_sgu_w_s, v_sgu_b_s, v_mix_w_out, v_ln2_g, v_ln2_b, v_ffn2_w_up, v_ffn2_w_down, v_ln3_g, v_ln3_b):
    given = dict(x=x, ln1_g=ln1_g, ln1_b=ln1_b, ffn1_w_up=ffn1_w_up, ffn1_w_down=ffn1_w_down, mix_w_in=mix_w_in, fox_b_f=fox_b_f, conv_w=conv_w, sgu_ln_g=sgu_ln_g, sgu_ln_b=sgu_ln_b, sgu_w_s=sgu_w_s, sgu_b_s=sgu_b_s, mix_w_out=mix_w_out, ln2_g=ln2_g, ln2_b=ln2_b, ffn2_w_up=ffn2_w_up, ffn2_w_down=ffn2_w_down, ln3_g=ln3_g, ln3_b=ln3_b, loss_target=loss_target, m_ln1_g=m_ln1_g, m_ln1_b=m_ln1_b, m_ffn1_w_up=m_ffn1_w_up, m_ffn1_w_down=m_ffn1_w_down, m_mix_w_in=m_mix_w_in, m_fox_b_f=m_fox_b_f, m_conv_w=m_conv_w, m_sgu_ln_g=m_sgu_ln_g, m_sgu_ln_b=m_sgu_ln_b, m_sgu_w_s=m_sgu_w_s, m_sgu_b_s=m_sgu_b_s, m_mix_w_out=m_mix_w_out, m_ln2_g=m_ln2_g, m_ln2_b=m_ln2_b, m_ffn2_w_up=m_ffn2_w_up, m_ffn2_w_down=m_ffn2_w_down, m_ln3_g=m_ln3_g, m_ln3_b=m_ln3_b, v_ln1_g=v_ln1_g, v_ln1_b=v_ln1_b, v_ffn1_w_up=v_ffn1_w_up, v_ffn1_w_down=v_ffn1_w_down, v_mix_w_in=v_mix_w_in, v_fox_b_f=v_fox_b_f, v_conv_w=v_conv_w, v_sgu_ln_g=v_sgu_ln_g, v_sgu_ln_b=v_sgu_ln_b, v_sgu_w_s=v_sgu_w_s, v_sgu_b_s=v_sgu_b_s, v_mix_w_out=v_mix_w_out, v_ln2_g=v_ln2_g, v_ln2_b=v_ln2_b, v_ffn2_w_up=v_ffn2_w_up, v_ffn2_w_down=v_ffn2_w_down, v_ln3_g=v_ln3_g, v_ln3_b=v_ln3_b)
    weights = {n: given[n] for n in TWIN_WEIGHTS}
    shared = {n: given[n] for n in SHARED_INPUTS}
    per_example = {n: given[n] for n in ['x']}
    grad_fn = _jax.value_and_grad(_loss, argnums=(0, 1))

    def one_microbatch(ex, loss_target):
        ex = dict(ex)
        diff = ex.pop(TWIN_DIFF_INPUT)
        return grad_fn(weights, diff, {**shared, **ex}, loss_target)

    if N_MICROBATCH == 1:
        loss, (grad_w, grad_x) = one_microbatch(per_example, given["loss_target"])
    else:
        def body(carry, xs):
            loss_sum, grad_sum = carry
            l_k, (gw_k, gx_k) = one_microbatch(xs[0], xs[1])
            with _jax.named_scope("update"):
                return (loss_sum + l_k, _jax.tree.map(_jnp.add, grad_sum, gw_k)), gx_k

        init = (_jnp.zeros((), _jnp.float32), _jax.tree.map(_jnp.zeros_like, weights))
        (loss, grad_w), grad_x = _jax.lax.scan(body, init, (per_example, given["loss_target"]))
    with _jax.named_scope("update"):
        delta_w, new_m, new_v = {}, {}, {}
        for n in TWIN_WEIGHTS:
            delta_w[n], new_m[n], new_v[n] = _adamw(weights[n], grad_w[n], given["m_" + n], given["v_" + n])
    return (loss, grad_x, *[grad_w[n] for n in TWIN_WEIGHTS], *[delta_w[n] for n in TWIN_WEIGHTS],
            *[new_m[n] for n in TWIN_WEIGHTS], *[new_v[n] for n in TWIN_WEIGHTS])
```

```python
import functools

import jax
import jax.numpy as jnp
from jax import lax
from jax.experimental import pallas as pl
from jax.experimental.pallas import tpu as pltpu

F32 = jnp.float32
BF16 = jnp.bfloat16
MESH = pl.DeviceIdType.MESH

N_DEV = 8
DEPTH = 2
D_MODEL = 1024
D_FF = 2816
FFN_BLK = 2 * D_FF // N_DEV
N_FFN_CHUNK = D_FF // FFN_BLK
D_CONV = 256
D_FOX = 512
N_HEADS = 8
D_SGU = 256
N_SGU_GROUPS = 4
SGU_CHUNK = 128
D_IN = 3 * D_CONV + 3 * D_FOX + N_HEADS + 2 * D_SGU
D_IN_SHARD = D_IN // N_DEV
COL_CONV, COL_QKV, COL_SGU, COL_F = 0, 768, 2304, 2816
D_IN_PAD = 2944
F_ORIG = 3 * D_CONV + 3 * D_FOX
ALPHA = (2 * DEPTH) ** 0.25
LN_EPS = 1e-5
ATT_SCALE = 0.125
ATT_BLK = 256
NEG = -1e30

ADAM_LR, ADAM_B1, ADAM_B2, ADAM_EPS, ADAM_WD, ADAM_STEP = 0.001, 0.9, 0.999, 1e-08, 0.01, 10

VMEM_BYTES_V7X = 64 * 1024 * 1024
HIGHEST = lax.Precision.HIGHEST


def _params(vmem_mb, sem=None):
    assert vmem_mb * 1024 * 1024 < VMEM_BYTES_V7X
    kw = dict(vmem_limit_bytes=vmem_mb * 1024 * 1024)
    if sem is not None:
        kw["dimension_semantics"] = sem
    return pltpu.CompilerParams(**kw)


def _dot(a, b, precision=None):
    return lax.dot_general(a, b, (((1,), (0,)), ((), ())), preferred_element_type=F32, precision=precision)


def _dot_nt(a, b):
    return lax.dot_general(a, b, (((1,), (1,)), ((), ())), preferred_element_type=F32)


def _dot_tn(a, b):
    return lax.dot_general(a, b, (((0,), (0,)), ((), ())), preferred_element_type=F32)


def _ln_stats(z):
    mu = jnp.mean(z, axis=-1, keepdims=True)
    zc = z - mu
    var = jnp.mean(zc * zc, axis=-1, keepdims=True)
    rstd = lax.rsqrt(var + LN_EPS)
    return zc * rstd, rstd


def _ln_bwd(dy, xhat, rstd, g):
    dxh = dy * g
    m1 = jnp.mean(dxh, axis=-1, keepdims=True)
    m2 = jnp.mean(dxh * xhat, axis=-1, keepdims=True)
    return rstd * (dxh - m1 - xhat * m2)


_GELU_C = 0.7978845608028654


def _gelu(x):
    return 0.5 * x * (1.0 + jnp.tanh(_GELU_C * (x + 0.044715 * x * x * x)))


def _gelu_grad(x):
    t = jnp.tanh(_GELU_C * (x + 0.044715 * x * x * x))
    return 0.5 * (1.0 + t) + 0.5 * x * (1.0 - t * t) * _GELU_C * (1.0 + 3 * 0.044715 * x * x)


def _vspec():
    return pl.BlockSpec(memory_space=pltpu.VMEM)


def _anyspec():
    return pl.BlockSpec(memory_space=pl.ANY)


def _mesh_pos():
    return lax.axis_index("x"), lax.axis_index("y"), lax.axis_index("c")


def _allgather_big(shards):
    n = len(shards)
    depth = shards[0].shape[0]
    items = [(a, l) for a in range(n) for l in range(depth)]

    def body(*refs):
        ins, outs = refs[:n], refs[n:2 * n]
        send_sems, recv_sems, local_sems = refs[2 * n:]
        x, y, c = _mesh_pos()
        me, sibling = (x, y, c), (x, y, 1 - c)
        chips = [(1 - x, y), (x, 1 - y), (1 - x, 1 - y)]

        def blk(pos):
            return 4 * pos[0] + 2 * pos[1] + pos[2]

        def copy(i, k, block, to, from_input=False):
            a, l = items[i]
            dst = outs[a].at[l, blk(block)]
            src = ins[a].at[l] if from_input else dst
            return pltpu.make_async_remote_copy(
                src_ref=src, dst_ref=dst, send_sem=send_sems.at[7 * i + k], recv_sem=recv_sems.at[7 * i + k],
                device_id=to, device_id_type=MESH)

        mine = [pltpu.make_async_copy(ins[a].at[l], outs[a].at[l, blk(me)], local_sems.at[i])
                for i, (a, l) in enumerate(items)]
        for cp in mine:
            cp.start()
        first = []
        for i in range(len(items)):
            first.append(copy(i, 0, me, sibling, True))
            first += [copy(i, 1 + j, me, (*chip, c), True) for j, chip in enumerate(chips)]
        for cp in first:
            cp.start()
        passed = []
        for j, chip in enumerate(chips):
            for i in range(len(items)):
                copy(i, 1 + j, (*chip, c), me).wait_recv()
                cp = copy(i, 4 + j, (*chip, c), sibling)
                cp.start()
                passed.append(cp)
        for i in range(len(items)):
            copy(i, 0, sibling, me).wait_recv()
            for j, chip in enumerate(chips):
                copy(i, 4 + j, (*chip, 1 - c), me).wait_recv()
        for cp in first + passed:
            cp.wait_send()
        for cp in mine:
            cp.wait()

    n_sem = 7 * len(items)
    return pl.pallas_call(
        body, name="allgather_big",
        out_shape=[jax.ShapeDtypeStruct((s.shape[0], N_DEV) + s.shape[1:], s.dtype) for s in shards],
        in_specs=[_anyspec()] * n, out_specs=[_anyspec()] * n,
        scratch_shapes=[pltpu.SemaphoreType.DMA((n_sem,)), pltpu.SemaphoreType.DMA((n_sem,)),
                        pltpu.SemaphoreType.DMA((len(items),))],
    )(*shards)


def _allgather_small(v):
    rows = v.shape[0]

    def body(v_ref, out_ref, send_sems, recv_sems):
        x, y, c = _mesh_pos()
        me = 4 * x + 2 * y + c
        out_ref[me] = v_ref[...]
        rel = [(dx, dy, dc) for dx in (0, 1) for dy in (0, 1) for dc in (0, 1)][1:]
        copies = []
        for k, (dx, dy, dc) in enumerate(rel):
            to = (x ^ dx, y ^ dy, c ^ dc)
            copies.append(pltpu.make_async_remote_copy(
                src_ref=v_ref, dst_ref=out_ref.at[me], send_sem=send_sems.at[k], recv_sem=recv_sems.at[k],
                device_id=to, device_id_type=MESH))
        for cp in copies:
            cp.start()
        for k, (dx, dy, dc) in enumerate(rel):
            src_blk = 4 * (x ^ dx) + 2 * (y ^ dy) + (c ^ dc)
            pltpu.make_async_remote_copy(
                src_ref=v_ref, dst_ref=out_ref.at[src_blk], send_sem=send_sems.at[k], recv_sem=recv_sems.at[k],
                device_id=(x, y, c), device_id_type=MESH).wait_recv()
        for cp in copies:
            cp.wait_send()

    return pl.pallas_call(
        body, name="allgather_small",
        out_shape=jax.ShapeDtypeStruct((N_DEV, rows, 128), v.dtype),
        in_specs=[_vspec()], out_specs=_vspec(),
        scratch_shapes=[pltpu.SemaphoreType.DMA((7,)), pltpu.SemaphoreType.DMA((7,))],
        compiler_params=_params(24),
    )(v)


def _sibling_exchange(grads):
    n = len(grads)

    def body(*refs):
        ins, outs = refs[:n], refs[n:2 * n]
        send_sems, recv_sems = refs[2 * n:]
        x, y, c = _mesh_pos()
        sibling = (x, y, 1 - c)
        copies = []
        for a in range(n):
            for q in range(4):
                copies.append(pltpu.make_async_remote_copy(
                    src_ref=ins[a].at[2 * q + (1 - c)], dst_ref=outs[a].at[q],
                    send_sem=send_sems.at[4 * a + q], recv_sem=recv_sems.at[4 * a + q],
                    device_id=sibling, device_id_type=MESH))
        for cp in copies:
            cp.start()
        for cp in copies:
            cp.wait()

    return pl.pallas_call(
        body, name="rs_sibling_exchange",
        out_shape=[jax.ShapeDtypeStruct((4,) + g.shape[1:], g.dtype) for g in grads],
        in_specs=[_anyspec()] * n, out_specs=[_anyspec()] * n,
        scratch_shapes=[pltpu.SemaphoreType.DMA((4 * n,)), pltpu.SemaphoreType.DMA((4 * n,))],
    )(*grads)


def _chip_exchange(parts):
    n = len(parts)

    def body(*refs):
        ins, outs = refs[:n], refs[n:2 * n]
        send_sems, recv_sems = refs[2 * n:]
        x, y, c = _mesh_pos()
        chips = [(1 - x, y), (x, 1 - y), (1 - x, 1 - y)]
        copies = []
        for a in range(n):
            for j, chip in enumerate(chips):
                copies.append(pltpu.make_async_remote_copy(
                    src_ref=ins[a].at[2 * chip[0] + chip[1]], dst_ref=outs[a].at[j],
                    send_sem=send_sems.at[3 * a + j], recv_sem=recv_sems.at[3 * a + j],
                    device_id=(*chip, c), device_id_type=MESH))
        for cp in copies:
            cp.start()
        for cp in copies:
            cp.wait()

    return pl.pallas_call(
        body, name="rs_chip_exchange",
        out_shape=[jax.ShapeDtypeStruct((3,) + p.shape[1:], p.dtype) for p in parts],
        in_specs=[_anyspec()] * n, out_specs=[_anyspec()] * n,
        scratch_shapes=[pltpu.SemaphoreType.DMA((3 * n,)), pltpu.SemaphoreType.DMA((3 * n,))],
    )(*parts)


def _row_tile(rows, cols, budget_bytes=2 * 1024 * 1024):
    best = 8
    for t in range(8, rows + 1, 8):
        if rows % t == 0 and t * cols * 4 <= budget_bytes:
            best = t
    return best


def _chip_partial(g, recv, core):
    _, rows, cols = g.shape
    tr = _row_tile(rows, cols)

    def body(core_ref, g_ref, r_ref, o32_ref, o16_ref):
        s = g_ref[...] + r_ref[...]
        o32_ref[...] = s
        o16_ref[...] = s.astype(BF16)

    blk = (None, tr, cols)
    return pl.pallas_call(
        body, name="rs_chip_partial",
        grid_spec=pltpu.PrefetchScalarGridSpec(
            num_scalar_prefetch=1, grid=(4, rows // tr),
            in_specs=[pl.BlockSpec(blk, lambda q, i, c: (2 * q + c[0], i, 0)),
                      pl.BlockSpec(blk, lambda q, i, c: (q, i, 0))],
            out_specs=[pl.BlockSpec(blk, lambda q, i, c: (q, i, 0))] * 2),
        out_shape=[jax.ShapeDtypeStruct((4, rows, cols), F32), jax.ShapeDtypeStruct((4, rows, cols), BF16)],
        compiler_params=_params(32),
    )(core, g, recv)


def _adam_math(w, g, m, v):
    m = ADAM_B1 * m + (1.0 - ADAM_B1) * g
    v = ADAM_B2 * v + (1.0 - ADAM_B2) * (g * g)
    m_hat = m / (1.0 - ADAM_B1 ** ADAM_STEP)
    v_hat = v / (1.0 - ADAM_B2 ** ADAM_STEP)
    delta = -ADAM_LR * (m_hat / (jnp.sqrt(v_hat) + ADAM_EPS) + ADAM_WD * w)
    return delta, m, v


def _adamw_shard(part32, recv16, slot, w, m, v):
    rows, cols = w.shape
    tr = _row_tile(rows, cols, 1024 * 1024)

    def body(slot_ref, p_ref, r_ref, w_ref, m_ref, v_ref, g_out, d_out, m_out, v_out):
        g = p_ref[...] + r_ref[0].astype(F32) + r_ref[1].astype(F32) + r_ref[2].astype(F32)
        d, mn, vn = _adam_math(w_ref[...], g, m_ref[...], v_ref[...])
        g_out[...] = g
        d_out[...] = d
        m_out[...] = mn
        v_out[...] = vn

    flat = pl.BlockSpec((tr, cols), lambda i, s: (i, 0))
    return pl.pallas_call(
        body, name="adamw_shard",
        grid_spec=pltpu.PrefetchScalarGridSpec(
            num_scalar_prefetch=1, grid=(rows // tr,),
            in_specs=[pl.BlockSpec((None, tr, cols), lambda i, s: (s[0], i, 0)),
                      pl.BlockSpec((3, tr, cols), lambda i, s: (0, i, 0)), flat, flat, flat],
            out_specs=[flat] * 4),
        out_shape=[jax.ShapeDtypeStruct((rows, cols), F32)] * 4,
        compiler_params=_params(32),
    )(slot, part32, recv16, w, m, v)


def _adamw_small(gathered, w, m, v):
    rows = w.shape[0]

    def body(a_ref, w_ref, m_ref, v_ref, g_out, d_out, m_out, v_out):
        g = a_ref[0]
        for d in range(1, N_DEV):
            g = g + a_ref[d]
        dl, mn, vn = _adam_math(w_ref[...], g, m_ref[...], v_ref[...])
        g_out[...] = g
        d_out[...] = dl
        m_out[...] = mn
        v_out[...] = vn

    return pl.pallas_call(
        body, name="adamw_small",
        in_specs=[_vspec()] * 4, out_specs=[_vspec()] * 4,
        out_shape=[jax.ShapeDtypeStruct((rows, 128), F32)] * 4,
        compiler_params=_params(32),
    )(gathered, w, m, v)


def _load_weights_once(pairs, sems):
    @pl.when((pl.program_id(0) == 0) & (pl.program_id(1) == 0))
    def _():
        cps = [pltpu.make_async_copy(src, dst, sems.at[i]) for i, (src, dst) in enumerate(pairs)]
        for cp in cps:
            cp.start()
        for cp in cps:
            cp.wait()


def _ffn_fwd(x, wup_all, wd_all, layer, ln_g, ln_b, tm=512):
    t_tok = x.shape[0]

    def body(x_ref, g_ref, b_ref, wup_hbm, wd_hbm, xn_ref, xnb_ref, z_ref, gu_ref, wup_v, wd_v, acc, sems):
        j = pl.program_id(1)
        _load_weights_once([(wup_hbm.at[layer], wup_v), (wd_hbm.at[layer], wd_v)], sems)
        xb = x_ref[...].astype(BF16)
        g = _dot(xb, wup_v[j])
        u = _dot(xb, wup_v[N_FFN_CHUNK + j])
        gu_ref[0] = g.astype(BF16)
        gu_ref[1] = u.astype(BF16)
        a = (g * jax.nn.sigmoid(g) * u).astype(BF16)
        y = _dot(a, wd_v[j])

        @pl.when(j == 0)
        def _():
            acc[...] = y

        @pl.when(j > 0)
        def _():
            acc[...] += y

        @pl.when(j == N_FFN_CHUNK - 1)
        def _():
            z = ALPHA * x_ref[...] + 0.5 * acc[...]
            xhat, _ = _ln_stats(z)
            xn = xhat * g_ref[...] + b_ref[...]
            z_ref[...] = z
            xn_ref[...] = xn
            xnb_ref[...] = xn.astype(BF16)

    tok = pl.BlockSpec((tm, D_MODEL), lambda i, j: (i, 0))
    vec = pl.BlockSpec((1, D_MODEL), lambda i, j: (0, 0))
    return pl.pallas_call(
        body, name="ffn_fwd", grid=(t_tok // tm, N_FFN_CHUNK),
        in_specs=[tok, vec, vec, _anyspec(), _anyspec()],
        out_specs=[tok, tok, tok, pl.BlockSpec((2, None, tm, FFN_BLK), lambda i, j: (0, j, i, 0))],
        out_shape=[jax.ShapeDtypeStruct((t_tok, D_MODEL), F32), jax.ShapeDtypeStruct((t_tok, D_MODEL), BF16),
                   jax.ShapeDtypeStruct((t_tok, D_MODEL), F32),
                   jax.ShapeDtypeStruct((2, N_FFN_CHUNK, t_tok, FFN_BLK), BF16)],
        scratch_shapes=[pltpu.VMEM((N_DEV, D_MODEL, FFN_BLK), BF16), pltpu.VMEM((N_FFN_CHUNK, FFN_BLK, D_MODEL), BF16),
                        pltpu.VMEM((tm, D_MODEL), F32), pltpu.SemaphoreType.DMA((2,))],
        compiler_params=_params(56, ("arbitrary", "arbitrary")),
    )(x, ln_g, ln_b, wup_all, wd_all)


def _ffn_bwd(dxn, z, gu, wup_all, wd_all, layer, ln_g, tm=512):
    t_tok = dxn.shape[0]

    def body(dxn_ref, z_ref, gu_ref, g_ref, wup_hbm, wd_hbm,
             dx_ref, dy_ref, a_ref, dgu_ref, dg_ref, db_ref, wup_v, wd_v, dxacc, dyb, sems):
        i, j = pl.program_id(0), pl.program_id(1)
        _load_weights_once([(wup_hbm.at[layer], wup_v), (wd_hbm.at[layer], wd_v)], sems)

        @pl.when(j == 0)
        def _():
            dxn_t = dxn_ref[...]
            xhat, rstd = _ln_stats(z_ref[...])
            pg = jnp.sum(dxn_t * xhat, axis=0, keepdims=True)
            pb = jnp.sum(dxn_t, axis=0, keepdims=True)

            @pl.when(i == 0)
            def _():
                dg_ref[...] = pg
                db_ref[...] = pb

            @pl.when(i > 0)
            def _():
                dg_ref[...] += pg
                db_ref[...] += pb

            dz = _ln_bwd(dxn_t, xhat, rstd, g_ref[...])
            dxacc[...] = ALPHA * dz
            dy = (0.5 * dz).astype(BF16)
            dyb[...] = dy
            dy_ref[...] = dy

        da = _dot_nt(dyb[...], wd_v[j])
        g = gu_ref[0].astype(F32)
        u = gu_ref[1].astype(F32)
        sig = jax.nn.sigmoid(g)
        silu = g * sig
        a_ref[...] = (silu * u).astype(BF16)
        dg = (da * u * (sig * (1.0 + g * (1.0 - sig)))).astype(BF16)
        du = (da * silu).astype(BF16)
        dgu_ref[0] = dg
        dgu_ref[1] = du
        dxacc[...] += _dot_nt(dg, wup_v[j]) + _dot_nt(du, wup_v[N_FFN_CHUNK + j])

        @pl.when(j == N_FFN_CHUNK - 1)
        def _():
            dx_ref[...] = dxacc[...]

    tok = pl.BlockSpec((tm, D_MODEL), lambda i, j: (i, 0))
    vec = pl.BlockSpec((1, D_MODEL), lambda i, j: (0, 0))
    gu_spec = pl.BlockSpec((2, None, tm, FFN_BLK), lambda i, j: (0, j, i, 0))
    return pl.pallas_call(
        body, name="ffn_bwd", grid=(t_tok // tm, N_FFN_CHUNK),
        in_specs=[tok, tok, gu_spec, vec, _anyspec(), _anyspec()],
        out_specs=[tok, tok, pl.BlockSpec((None, tm, FFN_BLK), lambda i, j: (j, i, 0)), gu_spec, vec, vec],
        out_shape=[jax.ShapeDtypeStruct((t_tok, D_MODEL), F32), jax.ShapeDtypeStruct((t_tok, D_MODEL), BF16),
                   jax.ShapeDtypeStruct((N_FFN_CHUNK, t_tok, FFN_BLK), BF16),
                   jax.ShapeDtypeStruct((2, N_FFN_CHUNK, t_tok, FFN_BLK), BF16),
                   jax.ShapeDtypeStruct((1, D_MODEL), F32), jax.ShapeDtypeStruct((1, D_MODEL), F32)],
        scratch_shapes=[pltpu.VMEM((N_DEV, D_MODEL, FFN_BLK), BF16), pltpu.VMEM((N_FFN_CHUNK, FFN_BLK, D_MODEL), BF16),
                        pltpu.VMEM((tm, D_MODEL), F32), pltpu.VMEM((tm, D_MODEL), BF16), pltpu.SemaphoreType.DMA((2,))],
        compiler_params=_params(56, ("arbitrary", "arbitrary")),
    )(dxn, z, gu, ln_g, wup_all, wd_all)


def _matmul_tn(a, b, tk=1024):
    ga, t_tok, m = a.shape
    gb, _, n = b.shape
    groups = max(ga, gb)

    def body(a_ref, b_ref, o_ref):
        p = _dot_tn(a_ref[...].astype(BF16), b_ref[...].astype(BF16))

        @pl.when(pl.program_id(1) == 0)
        def _():
            o_ref[...] = p

        @pl.when(pl.program_id(1) > 0)
        def _():
            o_ref[...] += p

    return pl.pallas_call(
        body, name=f"matmul_tn_{m}x{n}", grid=(groups, t_tok // tk),
        in_specs=[pl.BlockSpec((None, tk, m), (lambda g, t: (g, t, 0)) if ga > 1 else (lambda g, t: (0, t, 0))),
                  pl.BlockSpec((None, tk, n), (lambda g, t: (g, t, 0)) if gb > 1 else (lambda g, t: (0, t, 0)))],
        out_specs=pl.BlockSpec((None, m, n), lambda g, t: (g, 0, 0)),
        out_shape=jax.ShapeDtypeStruct((groups, m, n), F32),
        compiler_params=_params(56, ("arbitrary", "arbitrary")),
    )(a, b)


def _in_proj(x, w_in, tm=512):
    t_tok = x.shape[0]

    def body(x_ref, w_ref, conv_ref, qkv_ref, sgu_ref, f_ref):
        xb = x_ref[...].astype(BF16)
        conv_ref[...] = _dot(xb, w_ref[:, COL_CONV:COL_QKV])
        qkv_ref[...] = _dot(xb, w_ref[:, COL_QKV:COL_SGU]).astype(BF16)
        sgu_ref[...] = _dot(xb, w_ref[:, COL_SGU:COL_F])
        f_ref[...] = _dot(xb, w_ref[:, COL_F:D_IN_PAD])

    def tok(n):
        return pl.BlockSpec((tm, n), lambda i: (i, 0))

    return pl.pallas_call(
        body, name="mix_in_proj", grid=(t_tok // tm,),
        in_specs=[tok(D_MODEL), pl.BlockSpec((D_MODEL, D_IN_PAD), lambda i: (0, 0))],
        out_specs=[tok(768), tok(1536), tok(512), tok(128)],
        out_shape=[jax.ShapeDtypeStruct((t_tok, 768), F32), jax.ShapeDtypeStruct((t_tok, 1536), BF16),
                   jax.ShapeDtypeStruct((t_tok, 512), F32), jax.ShapeDtypeStruct((t_tok, 128), F32)],
        compiler_params=_params(48, ("arbitrary",)),
    )(x, w_in)


def _shift_down(a, k):
    row = lax.broadcasted_iota(jnp.int32, a.shape, 0)
    return jnp.where(row >= k, pltpu.roll(a, k, 0), 0.0)


def _shift_up(a, k):
    rows = a.shape[0]
    row = lax.broadcasted_iota(jnp.int32, a.shape, 0)
    return jnp.where(row < rows - k, pltpu.roll(a, rows - k, 0), 0.0)


def _tril(n):
    return lax.broadcasted_iota(jnp.int32, (n, n), 0) >= lax.broadcasted_iota(jnp.int32, (n, n), 1)


def _sgu_group_of_lane():
    return lax.broadcasted_iota(jnp.int32, (1, D_SGU), 1) // (D_SGU // N_SGU_GROUPS)


def _log_sigmoid(x):
    return jnp.minimum(x, 0.0) - jnp.log1p(jnp.exp(-jnp.abs(x)))


def _mix_mid_fwd(conv, sgu, f, conv_w, b_f, sgu_g, sgu_b, w_s, b_mat, n_seq):
    t_tok = conv.shape[0]
    seq = t_tok // n_seq
    n_chunk = seq // SGU_CHUNK
    per_blk = ATT_BLK // SGU_CHUNK

    def body(conv_ref, sgu_ref, f_ref, cw_ref, bf_ref, lg_ref, lb_ref, ws_ref, bm_ref, ya_ref, yc_ref, cum_ref):
        z = conv_ref[:, 256:512] * conv_ref[:, 512:768]
        y = cw_ref[0:1, :] * _shift_down(z, 2) + cw_ref[1:2, :] * _shift_down(z, 1) + cw_ref[2:3, :] * z
        ya_ref[...] = (conv_ref[:, 0:256] * y).astype(BF16)

        tril = _tril(SGU_CHUNK)
        grp = _sgu_group_of_lane()
        wc = [jnp.where(tril, ws_ref[g], 0.0).astype(BF16) for g in range(N_SGU_GROUPS)]
        tri_f = tril.astype(F32)
        carry = jnp.zeros((1, 128), F32)
        for n in range(n_chunk):
            rows = pl.ds(n * SGU_CHUNK, SGU_CHUNK)
            u = _gelu(sgu_ref[rows, 0:256])
            vhat, _ = _ln_stats(_gelu(sgu_ref[rows, 256:512]))
            vn = (vhat * lg_ref[...] + lb_ref[...]).astype(BF16)
            mixed = bm_ref[...]
            for g in range(N_SGU_GROUPS):
                mixed = mixed + jnp.where(grp == g, _dot(wc[g], vn), 0.0)
            yc_ref[rows, :] = (u * mixed).astype(BF16)

            log_f = _log_sigmoid(f_ref[rows, :] + bf_ref[...])
            cs = _dot(tri_f, log_f, HIGHEST) + carry
            carry = cs[SGU_CHUNK - 1:SGU_CHUNK, :]
            cs_t = cs.T
            lanes = pl.ds((n % per_blk) * SGU_CHUNK, SGU_CHUNK)
            for h in range(N_HEADS):
                cum_ref[h, n // per_blk, :, lanes] = cs_t[h:h + 1, :]

    def seq_blk(n):
        return pl.BlockSpec((seq, n), lambda b: (b, 0))

    def full(shape):
        return pl.BlockSpec(shape, lambda b: (0,) * len(shape))

    return pl.pallas_call(
        body, name="mix_mid_fwd", grid=(n_seq,),
        in_specs=[seq_blk(768), seq_blk(512), seq_blk(128), full((8, 256)), full((1, 128)), full((1, 256)),
                  full((1, 256)), full((4, 128, 128)), full((128, 256))],
        out_specs=[seq_blk(256), seq_blk(256),
                   pl.BlockSpec((N_HEADS, seq // ATT_BLK, 1, ATT_BLK), lambda b: (b, 0, 0, 0))],
        out_shape=[jax.ShapeDtypeStruct((t_tok, 256), BF16), jax.ShapeDtypeStruct((t_tok, 256), BF16),
                   jax.ShapeDtypeStruct((n_seq * N_HEADS, seq // ATT_BLK, 1, ATT_BLK), F32)],
        compiler_params=_params(48, ("arbitrary",)),
    )(conv, sgu, f, conv_w, b_f, sgu_g, sgu_b, w_s, b_mat)


def _head_masks():
    lane = lax.broadcasted_iota(jnp.int32, (1, 128), 1)
    return lane < 64, lane


def _fox_fwd(qkv, cum_t, n_seq):
    t_tok = qkv.shape[0]
    seq = t_tok // n_seq
    nq = seq // ATT_BLK
    blk = ATT_BLK

    def body(q_ref, k_ref, v_ref, c0_ref, c1_ref, o_ref, lse_ref):
        qi = pl.program_id(2)
        first, _ = _head_masks()
        qs = q_ref[...] * ATT_SCALE
        zero = jnp.zeros_like(qs)
        q0 = jnp.where(first, qs, zero)
        q1 = jnp.where(first, zero, qs)
        causal = _tril(blk)

        def step(kb, carry, masked):
            m0, l0, m1, l1, acc = carry
            rows = pl.ds(pl.multiple_of(kb * blk, blk), blk)
            k = k_ref[rows, :]
            v = v_ref[rows, :]

            def one(qh, c_ref, m, l):
                s = _dot_nt(qh, k) - c_ref[kb]
                if masked:
                    s = jnp.where(causal, s, NEG)
                m_new = jnp.maximum(m, jnp.max(s, axis=1, keepdims=True))
                p = jnp.exp(s - m_new)
                scale = jnp.exp(m - m_new)
                l_new = scale * l + jnp.sum(p, axis=1, keepdims=True)
                return m_new, l_new, scale, _dot(p.astype(BF16), v)

            m0, l0, s0, pv0 = one(q0, c0_ref, m0, l0)
            m1, l1, s1, pv1 = one(q1, c1_ref, m1, l1)
            acc = jnp.where(first, acc * s0 + pv0, acc * s1 + pv1)
            return m0, l0, m1, l1, acc

        col = jnp.full((blk, 1), NEG, F32)
        zcol = jnp.zeros((blk, 1), F32)
        carry = (col, zcol, col, zcol, jnp.zeros((blk, 128), F32))
        carry = lax.fori_loop(0, qi, lambda kb, cr: step(kb, cr, False), carry)
        m0, l0, m1, l1, acc = step(qi, carry, True)
        o_ref[...] = (acc / jnp.where(first, l0, l1)).astype(BF16)
        lse_ref[...] = jnp.where(first, m0 + jnp.log(l0), m1 + jnp.log(l1))

    cum_spec0 = pl.BlockSpec((None, nq, 1, blk), lambda b, hp, qi: (b * N_HEADS + 2 * hp, 0, 0, 0))
    cum_spec1 = pl.BlockSpec((None, nq, 1, blk), lambda b, hp, qi: (b * N_HEADS + 2 * hp + 1, 0, 0, 0))
    out_spec = pl.BlockSpec((blk, 128), lambda b, hp, qi: (b * nq + qi, hp))
    return pl.pallas_call(
        body, name="fox_fwd", grid=(n_seq, 4, nq),
        in_specs=[pl.BlockSpec((blk, 128), lambda b, hp, qi: (b * nq + qi, hp)),
                  pl.BlockSpec((seq, 128), lambda b, hp, qi: (b, 4 + hp)),
                  pl.BlockSpec((seq, 128), lambda b, hp, qi: (b, 8 + hp)), cum_spec0, cum_spec1],
        out_specs=[out_spec, out_spec],
        out_shape=[jax.ShapeDtypeStruct((t_tok, D_FOX), BF16), jax.ShapeDtypeStruct((t_tok, D_FOX), F32)],
        compiler_params=_params(32, ("arbitrary", "arbitrary", "arbitrary")),
    )(qkv, qkv, qkv, cum_t, cum_t)


def _fox_bwd(qkv, cum_t, o, lse, d_o, n_seq):
    t_tok = qkv.shape[0]
    seq = t_tok // n_seq
    nk = seq // ATT_BLK
    blk = ATT_BLK

    def body(q_ref, k_ref, v_ref, c0_ref, c1_ref, o_ref, lse_ref, do_ref,
             dq_ref, dk_ref, dv_ref, drow_ref, dc0_ref, dc1_ref):
        kb = pl.program_id(2)
        first, lane = _head_masks()
        k = k_ref[...]
        v = v_ref[...]
        zero = jnp.zeros_like(k)
        k0 = jnp.where(first, k, zero)
        k1 = jnp.where(first, zero, k)
        c0 = c0_ref[...]
        c1 = c1_ref[...]
        causal = _tril(blk)

        @pl.when(kb == 0)
        def _():
            dq_ref[...] = jnp.zeros_like(dq_ref)
            drow_ref[...] = jnp.zeros_like(drow_ref)

        def step(qi, carry, masked):
            dk, dv, dc0, dc1 = carry
            rows = pl.ds(pl.multiple_of(qi * blk, blk), blk)
            qs = q_ref[rows, :] * ATT_SCALE
            q0 = jnp.where(first, qs, zero)
            q1 = jnp.where(first, zero, qs)
            d_o = do_ref[rows, :]
            do0 = jnp.where(first, d_o, zero)
            do1 = jnp.where(first, zero, d_o)
            dd = d_o.astype(F32) * o_ref[rows, :].astype(F32)
            lse_t = lse_ref[rows, :]

            def one(qh, kh, doh, c, head_lanes, lse_lane):
                delta = jnp.sum(jnp.where(head_lanes, dd, 0.0), axis=1, keepdims=True)
                lse_h = jnp.sum(jnp.where(lane == lse_lane, lse_t, 0.0), axis=1, keepdims=True)
                s = _dot_nt(qh, k) - c
                if masked:
                    s = jnp.where(causal, s, NEG)
                p = jnp.exp(s - lse_h)
                ds = p * (_dot_nt(doh, v) - delta)
                pb = p.astype(BF16)
                dsb = ds.astype(BF16)
                return (_dot_tn(dsb, qh), _dot_tn(pb, doh), _dot(dsb, kh) * ATT_SCALE,
                        jnp.sum(ds, axis=0, keepdims=True), jnp.sum(ds, axis=1, keepdims=True))

            dk_a, dv_a, dq_a, dc_a, dr_a = one(q0, k0, do0, c0, first, 0)
            dk_b, dv_b, dq_b, dc_b, dr_b = one(q1, k1, do1, c1, jnp.logical_not(first), 64)
            dq_ref[rows, :] += dq_a + dq_b
            drow_ref[rows, :] += jnp.where(first, dr_a, dr_b)
            return dk + dk_a + dk_b, dv + dv_a + dv_b, dc0 - dc_a, dc1 - dc_b

        zt = jnp.zeros((blk, 128), F32)
        zr = jnp.zeros((1, blk), F32)
        carry = step(kb, (zt, zt, zr, zr), True)
        dk, dv, dc0, dc1 = lax.fori_loop(kb + 1, nk, lambda qi, cr: step(qi, cr, False), carry)
        dk_ref[...] = dk.astype(BF16)
        dv_ref[...] = dv.astype(BF16)
        dc0_ref[...] = dc0
        dc1_ref[...] = dc1

    def seq_spec(col0):
        return pl.BlockSpec((seq, 128), lambda b, hp, kb: (b, col0 + hp))

    def key_spec(col0):
        return pl.BlockSpec((blk, 128), lambda b, hp, kb: (b * nk + kb, col0 + hp))

    def cum_spec(h):
        return pl.BlockSpec((None, None, 1, blk), lambda b, hp, kb: (b * N_HEADS + 2 * hp + h, kb, 0, 0))

    dcum_spec = pl.BlockSpec((None, None, 1, blk), lambda b, hp, kb: (b * 4 + hp, kb, 0, 0))
    dcum_shape = jax.ShapeDtypeStruct((n_seq * 4, nk, 1, blk), F32)
    return pl.pallas_call(
        body, name="fox_bwd", grid=(n_seq, 4, nk),
        in_specs=[seq_spec(0), key_spec(4), key_spec(8), cum_spec(0), cum_spec(1), seq_spec(0), seq_spec(0), seq_spec(0)],
        out_specs=[seq_spec(0), key_spec(0), key_spec(0), seq_spec(0), dcum_spec, dcum_spec],
        out_shape=[jax.ShapeDtypeStruct((t_tok, D_FOX), F32), jax.ShapeDtypeStruct((t_tok, D_FOX), BF16),
                   jax.ShapeDtypeStruct((t_tok, D_FOX), BF16), jax.ShapeDtypeStruct((t_tok, D_FOX), F32),
                   dcum_shape, dcum_shape],
        compiler_params=_params(32, ("arbitrary", "arbitrary", "arbitrary")),
    )(qkv, qkv, qkv, cum_t, cum_t, o, lse, d_o)


def _mix_out_fwd(ya, yb, yc, x, w_out_all, layer, ln_g, ln_b, tm=512):
    t_tok = x.shape[0]

    def body(ya_ref, yb_ref, yc_ref, x_ref, w_ref, g_ref, b_ref, xn_ref, xnb_ref, z_ref):
        mo = _dot(ya_ref[...], w_ref[0:256, :]) + _dot(yb_ref[...], w_ref[256:768, :]) + _dot(yc_ref[...], w_ref[768:1024, :])
        z = ALPHA * x_ref[...] + mo
        xhat, _ = _ln_stats(z)
        xn = xhat * g_ref[...] + b_ref[...]
        z_ref[...] = z
        xn_ref[...] = xn
        xnb_ref[...] = xn.astype(BF16)

    def tok(n):
        return pl.BlockSpec((tm, n), lambda i: (i, 0))

    vec = pl.BlockSpec((1, D_MODEL), lambda i: (0, 0))
    return pl.pallas_call(
        body, name="mix_out_fwd", grid=(t_tok // tm,),
        in_specs=[tok(256), tok(512), tok(256), tok(D_MODEL),
                  pl.BlockSpec((None, D_MODEL, D_MODEL), lambda i: (layer, 0, 0)), vec, vec],
        out_specs=[tok(D_MODEL)] * 3,
        out_shape=[jax.ShapeDtypeStruct((t_tok, D_MODEL), F32), jax.ShapeDtypeStruct((t_tok, D_MODEL), BF16),
                   jax.ShapeDtypeStruct((t_tok, D_MODEL), F32)],
        compiler_params=_params(40, ("arbitrary",)),
    )(ya, yb, yc, x, w_out_all, ln_g, ln_b)


def _mix_out_bwd(dxn, z, w_out_all, layer, ln_g, tm=512):
    t_tok = dxn.shape[0]

    def body(dxn_ref, z_ref, w_ref, g_ref, dz_ref, dzb_ref, dya_ref, dyb_ref, dyc_ref, dg_ref, db_ref):
        i = pl.program_id(0)
        dxn_t = dxn_ref[...]
        xhat, rstd = _ln_stats(z_ref[...])
        pg = jnp.sum(dxn_t * xhat, axis=0, keepdims=True)
        pb = jnp.sum(dxn_t, axis=0, keepdims=True)

        @pl.when(i == 0)
        def _():
            dg_ref[...] = pg
            db_ref[...] = pb

        @pl.when(i > 0)
        def _():
            dg_ref[...] += pg
            db_ref[...] += pb

        dz = _ln_bwd(dxn_t, xhat, rstd, g_ref[...])
        dzb = dz.astype(BF16)
        dz_ref[...] = dz
        dzb_ref[...] = dzb
        dya_ref[...] = _dot_nt(dzb, w_ref[0:256, :])
        dyb_ref[...] = _dot_nt(dzb, w_ref[256:768, :]).astype(BF16)
        dyc_ref[...] = _dot_nt(dzb, w_ref[768:1024, :])

    def tok(n):
        return pl.BlockSpec((tm, n), lambda i: (i, 0))

    vec = pl.BlockSpec((1, D_MODEL), lambda i: (0, 0))
    return pl.pallas_call(
        body, name="mix_out_bwd", grid=(t_tok // tm,),
        in_specs=[tok(D_MODEL), tok(D_MODEL), pl.BlockSpec((None, D_MODEL, D_MODEL), lambda i: (layer, 0, 0)), vec],
        out_specs=[tok(D_MODEL), tok(D_MODEL), tok(256), tok(512), tok(256), vec, vec],
        out_shape=[jax.ShapeDtypeStruct((t_tok, D_MODEL), F32), jax.ShapeDtypeStruct((t_tok, D_MODEL), BF16),
                   jax.ShapeDtypeStruct((t_tok, 256), F32), jax.ShapeDtypeStruct((t_tok, 512), BF16),
                   jax.ShapeDtypeStruct((t_tok, 256), F32),
                   jax.ShapeDtypeStruct((1, D_MODEL), F32), jax.ShapeDtypeStruct((1, D_MODEL), F32)],
        compiler_params=_params(40, ("arbitrary",)),
    )(dxn, z, w_out_all, ln_g)


def _conv_bwd(conv, dya, conv_w, n_seq):
    t_tok = conv.shape[0]
    seq = t_tok // n_seq

    def body(conv_ref, dya_ref, cw_ref, dconv_ref, dcw_ref):
        @pl.when(pl.program_id(0) == 0)
        def _():
            dcw_ref[...] = jnp.zeros_like(dcw_ref)

        z = conv_ref[:, 256:512] * conv_ref[:, 512:768]
        z1 = _shift_down(z, 1)
        z2 = _shift_down(z, 2)
        y = cw_ref[0:1, :] * z2 + cw_ref[1:2, :] * z1 + cw_ref[2:3, :] * z
        dya_t = dya_ref[...]
        dconv_ref[:, 0:256] = (dya_t * y).astype(BF16)
        dy = dya_t * conv_ref[:, 0:256]
        dcw_ref[0:1, :] += jnp.sum(dy * z2, axis=0, keepdims=True)
        dcw_ref[1:2, :] += jnp.sum(dy * z1, axis=0, keepdims=True)
        dcw_ref[2:3, :] += jnp.sum(dy * z, axis=0, keepdims=True)
        dz = cw_ref[2:3, :] * dy + cw_ref[1:2, :] * _shift_up(dy, 1) + cw_ref[0:1, :] * _shift_up(dy, 2)
        dconv_ref[:, 256:512] = (dz * conv_ref[:, 512:768]).astype(BF16)
        dconv_ref[:, 512:768] = (dz * conv_ref[:, 256:512]).astype(BF16)

    def seq_blk(n):
        return pl.BlockSpec((seq, n), lambda b: (b, 0))

    par = pl.BlockSpec((8, 256), lambda b: (0, 0))
    return pl.pallas_call(
        body, name="conv_bwd", grid=(n_seq,),
        in_specs=[seq_blk(768), seq_blk(256), par], out_specs=[seq_blk(768), par],
        out_shape=[jax.ShapeDtypeStruct((t_tok, 768), BF16), jax.ShapeDtypeStruct((8, 256), F32)],
        compiler_params=_params(56, ("arbitrary",)),
    )(conv, dya, conv_w)


def _sgu_gate_bwd(sgu, f, dyc, dcum, drow, b_f, sgu_g, sgu_b, w_s, b_mat, n_seq):
    t_tok = sgu.shape[0]
    seq = t_tok // n_seq
    n_chunk = seq // SGU_CHUNK
    per_blk = ATT_BLK // SGU_CHUNK

    def body(sgu_ref, f_ref, dyc_ref, dcum_ref, drow_ref, bf_ref, lg_ref, lb_ref, ws_ref, bm_ref,
             dsgu_ref, df_ref, dbf_ref, dlg_ref, dlb_ref, dws_ref, dbs_ref, dbm_acc):
        b = pl.program_id(0)

        @pl.when(b == 0)
        def _():
            for r in (dbf_ref, dlg_ref, dlb_ref, dws_ref, dbm_acc):
                r[...] = jnp.zeros_like(r)

        tril = _tril(SGU_CHUNK)
        grp = _sgu_group_of_lane()
        wc = [jnp.where(tril, ws_ref[g], 0.0).astype(BF16) for g in range(N_SGU_GROUPS)]
        for n in range(n_chunk):
            rows = pl.ds(n * SGU_CHUNK, SGU_CHUNK)
            su = sgu_ref[rows, 0:256]
            sv = sgu_ref[rows, 256:512]
            u = _gelu(su)
            vhat, rstd = _ln_stats(_gelu(sv))
            vn = (vhat * lg_ref[...] + lb_ref[...]).astype(BF16)
            mixed = bm_ref[...]
            for g in range(N_SGU_GROUPS):
                mixed = mixed + jnp.where(grp == g, _dot(wc[g], vn), 0.0)
            dyc_t = dyc_ref[rows, :]
            dsgu_ref[rows, 0:256] = (dyc_t * mixed * _gelu_grad(su)).astype(BF16)
            dmixed = dyc_t * u
            dbm_acc[...] += dmixed
            dvn = jnp.zeros((SGU_CHUNK, D_SGU), F32)
            for g in range(N_SGU_GROUPS):
                dm_g = jnp.where(grp == g, dmixed, 0.0).astype(BF16)
                dws_ref[g] += _dot_nt(dm_g, vn)
                dvn = dvn + _dot_tn(wc[g], dm_g)
            dlg_ref[...] += jnp.sum(dvn * vhat, axis=0, keepdims=True)
            dlb_ref[...] += jnp.sum(dvn, axis=0, keepdims=True)
            dsgu_ref[rows, 256:512] = (_ln_bwd(dvn, vhat, rstd, lg_ref[...]) * _gelu_grad(sv)).astype(BF16)

        later = (lax.broadcasted_iota(jnp.int32, (128, 128), 0) <= lax.broadcasted_iota(jnp.int32, (128, 128), 1)).astype(F32)
        pick = (lax.broadcasted_iota(jnp.int32, (D_FOX, 128), 0)
                == 64 * lax.broadcasted_iota(jnp.int32, (D_FOX, 128), 1)).astype(F32)
        head_row = lax.broadcasted_iota(jnp.int32, (8, 128), 0)
        carry = jnp.zeros((1, 128), F32)
        pad = jnp.zeros((120, 128), F32)
        for n in reversed(range(n_chunk)):
            rows = pl.ds(n * SGU_CHUNK, SGU_CHUNK)
            lanes = pl.ds((n % per_blk) * SGU_CHUNK, SGU_CHUNK)
            dc = jnp.zeros((8, 128), F32)
            for h in range(N_HEADS):
                dc = jnp.where(head_row == h, dcum_ref[h, n // per_blk, :, lanes], dc)
            dcum_n = jnp.concatenate([dc, pad], axis=0).T + _dot(drow_ref[rows, :], pick, HIGHEST)
            dlf = _dot(later, dcum_n, HIGHEST) + carry
            carry = carry + jnp.sum(dcum_n, axis=0, keepdims=True)
            df = dlf * jax.nn.sigmoid(-(f_ref[rows, :] + bf_ref[...]))
            df_ref[rows, :] = df.astype(BF16)
            dbf_ref[...] += jnp.sum(df, axis=0, keepdims=True)

        @pl.when(b == n_seq - 1)
        def _():
            for g in range(N_SGU_GROUPS):
                dws_ref[g] = jnp.where(tril, dws_ref[g], 0.0)
            sel = (lax.broadcasted_iota(jnp.int32, (D_SGU, 128), 0) // (D_SGU // N_SGU_GROUPS)
                   == lax.broadcasted_iota(jnp.int32, (D_SGU, 128), 1)).astype(F32)
            dbs_ref[...] = _dot(dbm_acc[...], sel, HIGHEST)

    def seq_blk(n):
        return pl.BlockSpec((seq, n), lambda b: (b, 0))

    def full(shape):
        return pl.BlockSpec(shape, lambda b: (0,) * len(shape))

    param_shapes = [(1, 128), (1, 256), (1, 256), (4, 128, 128), (128, 128)]
    return pl.pallas_call(
        body, name="sgu_gate_bwd", grid=(n_seq,),
        in_specs=[seq_blk(512), seq_blk(128), seq_blk(256),
                  pl.BlockSpec((N_HEADS, seq // ATT_BLK, 1, ATT_BLK), lambda b: (b, 0, 0, 0)), seq_blk(D_FOX),
                  full((1, 128)), full((1, 256)), full((1, 256)), full((4, 128, 128)), full((128, 256))],
        out_specs=[seq_blk(512), seq_blk(128)] + [full(s) for s in param_shapes],
        out_shape=[jax.ShapeDtypeStruct((t_tok, 512), BF16), jax.ShapeDtypeStruct((t_tok, 128), BF16)]
        + [jax.ShapeDtypeStruct(s, F32) for s in param_shapes],
        scratch_shapes=[pltpu.VMEM((128, 256), F32)],
        compiler_params=_params(48, ("arbitrary",)),
    )(sgu, f, dyc, dcum, drow, b_f, sgu_g, sgu_b, w_s, b_mat)


def _mix_in_bwd(dconv, dq, dk, dv, dsgu, df, dz, w_in, tm=512):
    t_tok = dz.shape[0]

    def body(dconv_ref, dq_ref, dk_ref, dv_ref, dsgu_ref, df_ref, dz_ref, w_ref, dx_ref, dp_ref):
        dqb = dq_ref[...].astype(BF16)
        pieces = [(COL_CONV, dconv_ref[...]), (COL_QKV, dqb), (COL_QKV + 512, dk_ref[...]), (COL_QKV + 1024, dv_ref[...]),
                  (COL_SGU, dsgu_ref[...]), (COL_F, df_ref[...])]
        dx = ALPHA * dz_ref[...]
        for col, val in pieces:
            width = val.shape[1]
            dp_ref[:, col:col + width] = val
            dx = dx + _dot_nt(val, w_ref[:, col:col + width])
        dx_ref[...] = dx

    def tok(n):
        return pl.BlockSpec((tm, n), lambda i: (i, 0))

    return pl.pallas_call(
        body, name="mix_in_bwd", grid=(t_tok // tm,),
        in_specs=[tok(768), tok(512), tok(512), tok(512), tok(512), tok(128), tok(D_MODEL),
                  pl.BlockSpec((D_MODEL, D_IN_PAD), lambda i: (0, 0))],
        out_specs=[tok(D_MODEL), tok(D_IN_PAD)],
        out_shape=[jax.ShapeDtypeStruct((t_tok, D_MODEL), F32), jax.ShapeDtypeStruct((t_tok, D_IN_PAD), BF16)],
        compiler_params=_params(48, ("arbitrary",)),
    )(dconv, dq, dk, dv, dsgu, df, dz, w_in)


def _loss_grad(y, target, tm=512):
    t_tok = y.shape[0]

    def body(y_ref, t_ref, dy_ref, loss_ref):
        err = y_ref[...] - t_ref[...]
        dy_ref[...] = err * (1.0 / D_MODEL)
        part = jnp.sum(jnp.sum(err * err, axis=1, keepdims=True), axis=0, keepdims=True) * (0.5 / D_MODEL)

        @pl.when(pl.program_id(0) == 0)
        def _():
            loss_ref[...] = jnp.zeros_like(loss_ref)

        loss_ref[...] += part

    tok = pl.BlockSpec((tm, D_MODEL), lambda i: (i, 0))
    return pl.pallas_call(
        body, name="loss_grad", grid=(t_tok // tm,),
        in_specs=[tok, tok], out_specs=[tok, pl.BlockSpec((1, 128), lambda i: (0, 0))],
        out_shape=[jax.ShapeDtypeStruct((t_tok, D_MODEL), F32), jax.ShapeDtypeStruct((1, 128), F32)],
        compiler_params=_params(32, ("arbitrary",)),
    )(y, target)


def _pad_rows(a, rows):
    return jnp.pad(a, ((0, rows - a.shape[0]), (0, 0)))


def _w_in_padded(w):
    zeros = jnp.zeros(w.shape[:-1] + (D_IN_PAD - COL_F - N_HEADS,), w.dtype)
    return jnp.concatenate([w[..., :F_ORIG], w[..., F_ORIG + N_HEADS:], w[..., F_ORIG:F_ORIG + N_HEADS], zeros], axis=-1)


def _w_in_unpadded(dw):
    return jnp.concatenate([dw[:, :F_ORIG], dw[:, COL_F:COL_F + N_HEADS], dw[:, F_ORIG:COL_F]], axis=-1)


SMALL_ROWS_PER_LAYER = 6 * 8 + 2 + 2 + 512 + 4 + 1 + 6
SMALL_ROWS = 1152


def _pack_small(p):
    rows = []
    for l in range(DEPTH):
        for name in ("ln1_g", "ln1_b", "ln2_g", "ln2_b", "ln3_g", "ln3_b", "sgu_ln_g", "sgu_ln_b"):
            rows.append(p[name][l].reshape(-1, 128))
        rows.append(p["sgu_w_s"][l].reshape(-1, 128))
        rows.append(p["sgu_b_s"][l].reshape(-1, 128))
        rows.append(jnp.pad(p["fox_b_f"][l], (0, 128 - N_HEADS)).reshape(1, 128))
        rows.append(p["conv_w"][l].reshape(-1, 128))
    return _pad_rows(jnp.concatenate(rows, axis=0), SMALL_ROWS)


def _unpack_small(a):
    out = {}
    r = 0

    def take(n):
        nonlocal r
        piece = a[r:r + n]
        r += n
        return piece

    per_layer = []
    for l in range(DEPTH):
        d = {}
        for name in ("ln1_g", "ln1_b", "ln2_g", "ln2_b", "ln3_g", "ln3_b"):
            d[name] = take(8).reshape(D_MODEL)
        for name in ("sgu_ln_g", "sgu_ln_b"):
            d[name] = take(2).reshape(D_SGU)
        d["sgu_w_s"] = take(512).reshape(N_SGU_GROUPS, SGU_CHUNK, SGU_CHUNK)
        d["sgu_b_s"] = take(4).reshape(N_SGU_GROUPS, SGU_CHUNK)
        d["fox_b_f"] = take(1).reshape(128)[:N_HEADS]
        d["conv_w"] = take(6).reshape(3, D_CONV)
        per_layer.append(d)
    for name in per_layer[0]:
        out[name] = jnp.stack([per_layer[l][name] for l in range(DEPTH)])
    return out


SMALL_NAMES = ("ln1_g", "ln1_b", "fox_b_f", "sgu_ln_g", "sgu_ln_b", "sgu_w_s", "sgu_b_s", "ln2_g", "ln2_b", "ln3_g", "ln3_b")
BIG_NAMES = ("ffn1_w_up", "ffn1_w_down", "mix_w_in", "mix_w_out", "ffn2_w_up", "ffn2_w_down")
WEIGHT_ORDER = ("ln1_g", "ln1_b", "ffn1_w_up", "ffn1_w_down", "mix_w_in", "fox_b_f", "conv_w", "sgu_ln_g", "sgu_ln_b",
                "sgu_w_s", "sgu_b_s", "mix_w_out", "ln2_g", "ln2_b", "ffn2_w_up", "ffn2_w_down", "ln3_g", "ln3_b")


def _local_step(x, target, wts, small, n_seq):
    def vec(a):
        return a.reshape(1, -1)

    saved = []
    h = x
    for l in range(DEPTH):
        s = {"x0": h}
        h1, h1b, s["z1"], s["gu1"] = _ffn_fwd(h, wts["ffn1_up"], wts["ffn1_down"], l, vec(small["ln1_g"][l]), vec(small["ln1_b"][l]))
        s["x1"], s["x1b"] = h1, h1b
        conv, qkv, sgu, f = _in_proj(h1, wts["w_in"][l])
        cw = _pad_rows(small["conv_w"][l], 8)
        bf = jnp.pad(small["fox_b_f"][l], (0, 128 - N_HEADS)).reshape(1, 128)
        b_mat = jnp.repeat(small["sgu_b_s"][l].T, D_SGU // N_SGU_GROUPS, axis=1)
        mid_params = (cw, bf, vec(small["sgu_ln_g"][l]), vec(small["sgu_ln_b"][l]), small["sgu_w_s"][l], b_mat)
        ya, yc, cum_t = _mix_mid_fwd(conv, sgu, f, *mid_params, n_seq)
        yb, lse = _fox_fwd(qkv, cum_t, n_seq)
        h2, h2b, s["z2"] = _mix_out_fwd(ya, yb, yc, h1, wts["w_out"], l, vec(small["ln2_g"][l]), vec(small["ln2_b"][l]))
        s.update(conv=conv, qkv=qkv, sgu=sgu, f=f, mid_params=mid_params, ya=ya, yb=yb, yc=yc, cum_t=cum_t, lse=lse,
                 x2=h2, x2b=h2b)
        h3, h3b, s["z3"], s["gu2"] = _ffn_fwd(h2, wts["ffn2_up"], wts["ffn2_down"], l, vec(small["ln3_g"][l]), vec(small["ln3_b"][l]))
        s["x3b"] = h3b
        saved.append(s)
        h = h3

    dh, loss = _loss_grad(h, target)

    big_grads = [None] * DEPTH
    small_grads = [None] * DEPTH
    for l in reversed(range(DEPTH)):
        s = saved[l]
        x0b = saved[l - 1]["x3b"] if l > 0 else x.astype(BF16)
        bg, sg = {}, {}
        dh, dy, a, dgu, sg["ln3_g"], sg["ln3_b"] = _ffn_bwd(dh, s["z3"], s["gu2"], wts["ffn2_up"], wts["ffn2_down"], l, vec(small["ln3_g"][l]))
        bg["ffn2_w_up"] = _matmul_tn(s["x2b"][None], dgu.reshape(N_DEV, -1, FFN_BLK))
        bg["ffn2_w_down"] = _matmul_tn(a, dy[None]).reshape(N_DEV, FFN_BLK // 2, D_MODEL)
        dz, dzb, dya, dyb, dyc, sg["ln2_g"], sg["ln2_b"] = _mix_out_bwd(dh, s["z2"], wts["w_out"], l, vec(small["ln2_g"][l]))
        dwo = [_matmul_tn(y[None], dzb[None])[0] for y in (s["ya"], s["yb"], s["yc"])]
        bg["mix_w_out"] = jnp.concatenate(dwo, axis=0).reshape(N_DEV, D_MODEL // N_DEV, D_MODEL)
        dq, dk, dv, drow, dc0, dc1 = _fox_bwd(s["qkv"], s["cum_t"], s["yb"], s["lse"], dyb, n_seq)
        dcum = jnp.stack([dc0, dc1], axis=1).reshape(s["cum_t"].shape)
        dconv, dcw = _conv_bwd(s["conv"], dya, s["mid_params"][0], n_seq)
        dsgu, df, dbf, dlg, dlb, dws, dbs = _sgu_gate_bwd(s["sgu"], s["f"], dyc, dcum, drow, *s["mid_params"][1:], n_seq)
        sg.update(conv_w=dcw[:3], fox_b_f=dbf[0, :N_HEADS], sgu_ln_g=dlg[0], sgu_ln_b=dlb[0], sgu_w_s=dws,
                  sgu_b_s=dbs[:, :N_SGU_GROUPS].T)
        dh, dp = _mix_in_bwd(dconv, dq, dk, dv, dsgu, df, dz, wts["w_in"][l])
        dwin = _w_in_unpadded(_matmul_tn(s["x1b"][None], dp[None], tk=512)[0])
        bg["mix_w_in"] = jnp.transpose(dwin.reshape(D_MODEL, N_DEV, D_IN_SHARD), (1, 0, 2))
        dh, dy, a, dgu, sg["ln1_g"], sg["ln1_b"] = _ffn_bwd(dh, s["z1"], s["gu1"], wts["ffn1_up"], wts["ffn1_down"], l, vec(small["ln1_g"][l]))
        bg["ffn1_w_up"] = _matmul_tn(x0b[None], dgu.reshape(N_DEV, -1, FFN_BLK))
        bg["ffn1_w_down"] = _matmul_tn(a, dy[None]).reshape(N_DEV, FFN_BLK // 2, D_MODEL)
        for name in ("ln1_g", "ln1_b", "ln2_g", "ln2_b", "ln3_g", "ln3_b"):
            sg[name] = sg[name][0]
        big_grads[l] = bg
        small_grads[l] = sg
    return loss, dh, big_grads, small_grads


def kernel(x, ln1_g, ln1_b, ffn1_w_up, ffn1_w_down, mix_w_in, fox_b_f, conv_w, sgu_ln_g, sgu_ln_b, sgu_w_s, sgu_b_s, mix_w_out, ln2_g, ln2_b, ffn2_w_up, ffn2_w_down, ln3_g, ln3_b, loss_target, m_ln1_g, m_ln1_b, m_ffn1_w_up, m_ffn1_w_down, m_mix_w_in, m_fox_b_f, m_conv_w, m_sgu_ln_g, m_sgu_ln_b, m_sgu_w_s, m_sgu_b_s, m_mix_w_out, m_ln2_g, m_ln2_b, m_ffn2_w_up, m_ffn2_w_down, m_ln3_g, m_ln3_b, v_ln1_g, v_ln1_b, v_ffn1_w_up, v_ffn1_w_down, v_mix_w_in, v_fox_b_f, v_conv_w, v_sgu_ln_g, v_sgu_ln_b, v_sgu_w_s, v_sgu_b_s, v_mix_w_out, v_ln2_g, v_ln2_b, v_ffn2_w_up, v_ffn2_w_down, v_ln3_g, v_ln3_b):
    w = dict(ln1_g=ln1_g, ln1_b=ln1_b, ffn1_w_up=ffn1_w_up, ffn1_w_down=ffn1_w_down, mix_w_in=mix_w_in, fox_b_f=fox_b_f,
             conv_w=conv_w, sgu_ln_g=sgu_ln_g, sgu_ln_b=sgu_ln_b, sgu_w_s=sgu_w_s, sgu_b_s=sgu_b_s, mix_w_out=mix_w_out,
             ln2_g=ln2_g, ln2_b=ln2_b, ffn2_w_up=ffn2_w_up, ffn2_w_down=ffn2_w_down, ln3_g=ln3_g, ln3_b=ln3_b)
    m = dict(ln1_g=m_ln1_g, ln1_b=m_ln1_b, ffn1_w_up=m_ffn1_w_up, ffn1_w_down=m_ffn1_w_down, mix_w_in=m_mix_w_in,
             fox_b_f=m_fox_b_f, conv_w=m_conv_w, sgu_ln_g=m_sgu_ln_g, sgu_ln_b=m_sgu_ln_b, sgu_w_s=m_sgu_w_s,
             sgu_b_s=m_sgu_b_s, mix_w_out=m_mix_w_out, ln2_g=m_ln2_g, ln2_b=m_ln2_b, ffn2_w_up=m_ffn2_w_up,
             ffn2_w_down=m_ffn2_w_down, ln3_g=m_ln3_g, ln3_b=m_ln3_b)
    v = dict(ln1_g=v_ln1_g, ln1_b=v_ln1_b, ffn1_w_up=v_ffn1_w_up, ffn1_w_down=v_ffn1_w_down, mix_w_in=v_mix_w_in,
             fox_b_f=v_fox_b_f, conv_w=v_conv_w, sgu_ln_g=v_sgu_ln_g, sgu_ln_b=v_sgu_ln_b, sgu_w_s=v_sgu_w_s,
             sgu_b_s=v_sgu_b_s, mix_w_out=v_mix_w_out, ln2_g=v_ln2_g, ln2_b=v_ln2_b, ffn2_w_up=v_ffn2_w_up,
             ffn2_w_down=v_ffn2_w_down, ln3_g=v_ln3_g, ln3_b=v_ln3_b)

    mx, my, mc = lax.axis_index("x"), lax.axis_index("y"), lax.axis_index("c")
    me = 4 * mx + 2 * my + mc
    n_seq, seq, _ = x.shape
    t_tok = n_seq * seq

    gathered = _allgather_big([w[name].astype(BF16) for name in BIG_NAMES])
    g = dict(zip(BIG_NAMES, gathered))
    w_in_full = jnp.transpose(g["mix_w_in"], (0, 2, 1, 3)).reshape(DEPTH, D_MODEL, D_IN)
    wts = dict(ffn1_up=g["ffn1_w_up"], ffn2_up=g["ffn2_w_up"],
               ffn1_down=g["ffn1_w_down"].reshape(DEPTH, N_FFN_CHUNK, FFN_BLK, D_MODEL),
               ffn2_down=g["ffn2_w_down"].reshape(DEPTH, N_FFN_CHUNK, FFN_BLK, D_MODEL),
               w_in=_w_in_padded(w_in_full), w_out=g["mix_w_out"].reshape(DEPTH, D_MODEL, D_MODEL))
    cw_rows = _pad_rows(conv_w.reshape(DEPTH * 3, D_CONV // N_DEV), 8)
    cw_all = _allgather_small(jnp.pad(cw_rows, ((0, 0), (0, 128 - D_CONV // N_DEV))))
    conv_w_full = jnp.transpose(cw_all[:, :DEPTH * 3, :D_CONV // N_DEV], (1, 0, 2)).reshape(DEPTH, 3, D_CONV)
    small = {name: w[name] for name in SMALL_NAMES}
    small["conv_w"] = conv_w_full

    loss_dev, grad_x, big_grads, small_grads = _local_step(
        x.reshape(t_tok, D_MODEL), loss_target.reshape(t_tok, D_MODEL), wts, small, n_seq)
    loss = lax.psum(loss_dev[0, 0], ("x", "y", "c"))

    core = mc.reshape(1).astype(jnp.int32)
    slot = (2 * mx + my).reshape(1).astype(jnp.int32)
    items = [(name, l) for l in range(DEPTH) for name in BIG_NAMES]
    blocked = [big_grads[l][name] for name, l in items]
    from_sibling = _sibling_exchange(blocked)
    partials = [_chip_partial(gb, r, core) for gb, r in zip(blocked, from_sibling)]
    from_chips = _chip_exchange([p16 for _, p16 in partials])
    res = {}
    for (name, l), (p32, _), r16 in zip(items, partials, from_chips):
        rows = p32.shape[1]
        res[(name, l)] = _adamw_shard(p32, r16, slot, *[t[name][l].reshape(rows, -1) for t in (w, m, v)])
    out = {}
    for name in BIG_NAMES:
        out[name] = [jnp.stack([res[(name, l)][k] for l in range(DEPTH)]).reshape(w[name].shape) for k in range(4)]

    sg = {name: jnp.stack([small_grads[l][name] for l in range(DEPTH)]) for name in SMALL_NAMES + ("conv_w",)}
    all_small = _allgather_small(_pack_small(sg))

    def widen(a):
        return lax.dynamic_update_slice(jnp.zeros((DEPTH, 3, D_CONV), F32), a, (0, 0, me * (D_CONV // N_DEV)))

    packed = [_pack_small({**{name: t[name] for name in SMALL_NAMES}, "conv_w": widen(t["conv_w"])}) for t in (w, m, v)]
    small_out = [_unpack_small(a) for a in _adamw_small(all_small, *packed)]
    for name in SMALL_NAMES:
        out[name] = [small_out[k][name] for k in range(4)]
    out["conv_w"] = [lax.dynamic_slice(small_out[k]["conv_w"], (0, 0, me * (D_CONV // N_DEV)), (DEPTH, 3, D_CONV // N_DEV))
                     for k in range(4)]

    return (loss, grad_x.reshape(x.shape), *[out[name][0] for name in WEIGHT_ORDER], *[out[name][1] for name in WEIGHT_ORDER],
            *[out[name][2] for name in WEIGHT_ORDER], *[out[name][3] for name in WEIGHT_ORDER])
```

```python
import functools

import jax
import jax.numpy as jnp
from jax import lax
from jax.experimental import pallas as pl
from jax.experimental.pallas import tpu as pltpu

F32 = jnp.float32
BF16 = jnp.bfloat16
MESH = pl.DeviceIdType.MESH

N_DEV = 8
DEPTH = 2
D_MODEL = 1024
D_FF = 2816
FFN_BLK = 2 * D_FF // N_DEV
N_FFN_CHUNK = D_FF // FFN_BLK
D_CONV = 256
D_FOX = 512
N_HEADS = 8
D_SGU = 256
N_SGU_GROUPS = 4
SGU_CHUNK = 128
D_IN = 3 * D_CONV + 3 * D_FOX + N_HEADS + 2 * D_SGU
D_IN_SHARD = D_IN // N_DEV
COL_CONV, COL_QKV, COL_SGU, COL_F = 0, 768, 2304, 2816
D_IN_PAD = 2944
F_ORIG = 3 * D_CONV + 3 * D_FOX
ALPHA = (2 * DEPTH) ** 0.25
LN_EPS = 1e-5
ATT_SCALE = 0.125
ATT_BLK = 512
NEG = -1e30

ADAM_LR, ADAM_B1, ADAM_B2, ADAM_EPS, ADAM_WD, ADAM_STEP = 0.001, 0.9, 0.999, 1e-08, 0.01, 10

VMEM_BYTES_V7X = 64 * 1024 * 1024
HIGHEST = lax.Precision.HIGHEST


def _params(vmem_mb, sem=None):
    assert vmem_mb * 1024 * 1024 < VMEM_BYTES_V7X
    kw = dict(vmem_limit_bytes=vmem_mb * 1024 * 1024)
    if sem is not None:
        kw["dimension_semantics"] = sem
    return pltpu.CompilerParams(**kw)


def _dot(a, b, precision=None):
    return lax.dot_general(a, b, (((1,), (0,)), ((), ())), preferred_element_type=F32, precision=precision)


def _dot_nt(a, b):
    return lax.dot_general(a, b, (((1,), (1,)), ((), ())), preferred_element_type=F32)


def _dot_tn(a, b):
    return lax.dot_general(a, b, (((0,), (0,)), ((), ())), preferred_element_type=F32)


def _ln_stats(z):
    mu = jnp.mean(z, axis=-1, keepdims=True)
    zc = z - mu
    var = jnp.mean(zc * zc, axis=-1, keepdims=True)
    rstd = lax.rsqrt(var + LN_EPS)
    return zc * rstd, rstd


def _ln_bwd(dy, xhat, rstd, g):
    dxh = dy * g
    m1 = jnp.mean(dxh, axis=-1, keepdims=True)
    m2 = jnp.mean(dxh * xhat, axis=-1, keepdims=True)
    return rstd * (dxh - m1 - xhat * m2)


_GELU_C = 0.7978845608028654


def _gelu(x):
    return 0.5 * x * (1.0 + jnp.tanh(_GELU_C * (x + 0.044715 * x * x * x)))


def _gelu_grad(x):
    t = jnp.tanh(_GELU_C * (x + 0.044715 * x * x * x))
    return 0.5 * (1.0 + t) + 0.5 * x * (1.0 - t * t) * _GELU_C * (1.0 + 3 * 0.044715 * x * x)


def _vspec():
    return pl.BlockSpec(memory_space=pltpu.VMEM)


def _anyspec():
    return pl.BlockSpec(memory_space=pl.ANY)


def _mesh_pos():
    return lax.axis_index("x"), lax.axis_index("y"), lax.axis_index("c")


def _allgather_big(shards):
    n = len(shards)
    depth = shards[0].shape[0]
    items = [(a, l) for a in range(n) for l in range(depth)]

    def body(*refs):
        ins, outs = refs[:n], refs[n:2 * n]
        send_sems, recv_sems, local_sems = refs[2 * n:]
        x, y, c = _mesh_pos()
        me, sibling = (x, y, c), (x, y, 1 - c)
        chips = [(1 - x, y), (x, 1 - y), (1 - x, 1 - y)]

        def blk(pos):
            return 4 * pos[0] + 2 * pos[1] + pos[2]

        def copy(i, k, block, to, from_input=False):
            a, l = items[i]
            dst = outs[a].at[l, blk(block)]
            src = ins[a].at[l] if from_input else dst
            return pltpu.make_async_remote_copy(
                src_ref=src, dst_ref=dst, send_sem=send_sems.at[7 * i + k], recv_sem=recv_sems.at[7 * i + k],
                device_id=to, device_id_type=MESH)

        mine = [pltpu.make_async_copy(ins[a].at[l], outs[a].at[l, blk(me)], local_sems.at[i])
                for i, (a, l) in enumerate(items)]
        for cp in mine:
            cp.start()
        first = []
        for i in range(len(items)):
            first.append(copy(i, 0, me, sibling, True))
            first += [copy(i, 1 + j, me, (*chip, c), True) for j, chip in enumerate(chips)]
        for cp in first:
            cp.start()
        passed = []
        for j, chip in enumerate(chips):
            for i in range(len(items)):
                copy(i, 1 + j, (*chip, c), me).wait_recv()
                cp = copy(i, 4 + j, (*chip, c), sibling)
                cp.start()
                passed.append(cp)
        for i in range(len(items)):
            copy(i, 0, sibling, me).wait_recv()
            for j, chip in enumerate(chips):
                copy(i, 4 + j, (*chip, 1 - c), me).wait_recv()
        for cp in first + passed:
            cp.wait_send()
        for cp in mine:
            cp.wait()

    n_sem = 7 * len(items)
    return pl.pallas_call(
        body, name="allgather_big",
        out_shape=[jax.ShapeDtypeStruct((s.shape[0], N_DEV) + s.shape[1:], s.dtype) for s in shards],
        in_specs=[_anyspec()] * n, out_specs=[_anyspec()] * n,
        scratch_shapes=[pltpu.SemaphoreType.DMA((n_sem,)), pltpu.SemaphoreType.DMA((n_sem,)),
                        pltpu.SemaphoreType.DMA((len(items),))],
    )(*shards)


def _allgather_small(v):
    rows = v.shape[0]

    def body(v_ref, out_ref, send_sems, recv_sems):
        x, y, c = _mesh_pos()
        me = 4 * x + 2 * y + c
        out_ref[me] = v_ref[...]
        rel = [(dx, dy, dc) for dx in (0, 1) for dy in (0, 1) for dc in (0, 1)][1:]
        copies = []
        for k, (dx, dy, dc) in enumerate(rel):
            to = (x ^ dx, y ^ dy, c ^ dc)
            copies.append(pltpu.make_async_remote_copy(
                src_ref=v_ref, dst_ref=out_ref.at[me], send_sem=send_sems.at[k], recv_sem=recv_sems.at[k],
                device_id=to, device_id_type=MESH))
        for cp in copies:
            cp.start()
        for k, (dx, dy, dc) in enumerate(rel):
            src_blk = 4 * (x ^ dx) + 2 * (y ^ dy) + (c ^ dc)
            pltpu.make_async_remote_copy(
                src_ref=v_ref, dst_ref=out_ref.at[src_blk], send_sem=send_sems.at[k], recv_sem=recv_sems.at[k],
                device_id=(x, y, c), device_id_type=MESH).wait_recv()
        for cp in copies:
            cp.wait_send()

    return pl.pallas_call(
        body, name="allgather_small",
        out_shape=jax.ShapeDtypeStruct((N_DEV, rows, 128), v.dtype),
        in_specs=[_vspec()], out_specs=_vspec(),
        scratch_shapes=[pltpu.SemaphoreType.DMA((7,)), pltpu.SemaphoreType.DMA((7,))],
        compiler_params=_params(24),
    )(v)


def _sibling_exchange(grads):
    n = len(grads)

    def body(*refs):
        ins, outs = refs[:n], refs[n:2 * n]
        send_sems, recv_sems = refs[2 * n:]
        x, y, c = _mesh_pos()
        sibling = (x, y, 1 - c)
        copies = []
        for a in range(n):
            for q in range(4):
                copies.append(pltpu.make_async_remote_copy(
                    src_ref=ins[a].at[2 * q + (1 - c)], dst_ref=outs[a].at[q],
                    send_sem=send_sems.at[4 * a + q], recv_sem=recv_sems.at[4 * a + q],
                    device_id=sibling, device_id_type=MESH))
        for cp in copies:
            cp.start()
        for cp in copies:
            cp.wait()

    return pl.pallas_call(
        body, name="rs_sibling_exchange",
        out_shape=[jax.ShapeDtypeStruct((4,) + g.shape[1:], g.dtype) for g in grads],
        in_specs=[_anyspec()] * n, out_specs=[_anyspec()] * n,
        scratch_shapes=[pltpu.SemaphoreType.DMA((4 * n,)), pltpu.SemaphoreType.DMA((4 * n,))],
    )(*grads)


def _chip_exchange(parts):
    n = len(parts)

    def body(*refs):
        ins, outs = refs[:n], refs[n:2 * n]
        send_sems, recv_sems = refs[2 * n:]
        x, y, c = _mesh_pos()
        chips = [(1 - x, y), (x, 1 - y), (1 - x, 1 - y)]
        copies = []
        for a in range(n):
            for j, chip in enumerate(chips):
                copies.append(pltpu.make_async_remote_copy(
                    src_ref=ins[a].at[2 * chip[0] + chip[1]], dst_ref=outs[a].at[j],
                    send_sem=send_sems.at[3 * a + j], recv_sem=recv_sems.at[3 * a + j],
                    device_id=(*chip, c), device_id_type=MESH))
        for cp in copies:
            cp.start()
        for cp in copies:
            cp.wait()

    return pl.pallas_call(
        body, name="rs_chip_exchange",
        out_shape=[jax.ShapeDtypeStruct((3,) + p.shape[1:], p.dtype) for p in parts],
        in_specs=[_anyspec()] * n, out_specs=[_anyspec()] * n,
        scratch_shapes=[pltpu.SemaphoreType.DMA((3 * n,)), pltpu.SemaphoreType.DMA((3 * n,))],
    )(*parts)


def _row_tile(rows, cols, budget_bytes=2 * 1024 * 1024):
    best = 8
    for t in range(8, rows + 1, 8):
        if rows % t == 0 and t * cols * 4 <= budget_bytes:
            best = t
    return best


def _chip_partial(g, recv, core):
    _, rows, cols = g.shape
    tr = _row_tile(rows, cols)

    def body(core_ref, g_ref, r_ref, o32_ref, o16_ref):
        s = g_ref[...] + r_ref[...]
        o32_ref[...] = s
        o16_ref[...] = s.astype(BF16)

    blk = (None, tr, cols)
    return pl.pallas_call(
        body, name="rs_chip_partial",
        grid_spec=pltpu.PrefetchScalarGridSpec(
            num_scalar_prefetch=1, grid=(4, rows // tr),
            in_specs=[pl.BlockSpec(blk, lambda q, i, c: (2 * q + c[0], i, 0)),
                      pl.BlockSpec(blk, lambda q, i, c: (q, i, 0))],
            out_specs=[pl.BlockSpec(blk, lambda q, i, c: (q, i, 0))] * 2),
        out_shape=[jax.ShapeDtypeStruct((4, rows, cols), F32), jax.ShapeDtypeStruct((4, rows, cols), BF16)],
        compiler_params=_params(32),
    )(core, g, recv)


def _adam_math(w, g, m, v):
    m = ADAM_B1 * m + (1.0 - ADAM_B1) * g
    v = ADAM_B2 * v + (1.0 - ADAM_B2) * (g * g)
    m_hat = m / (1.0 - ADAM_B1 ** ADAM_STEP)
    v_hat = v / (1.0 - ADAM_B2 ** ADAM_STEP)
    delta = -ADAM_LR * (m_hat / (jnp.sqrt(v_hat) + ADAM_EPS) + ADAM_WD * w)
    return delta, m, v


def _adamw_shard(part32, recv16, slot, w, m, v):
    rows, cols = w.shape
    tr = _row_tile(rows, cols, 1024 * 1024)

    def body(slot_ref, p_ref, r_ref, w_ref, m_ref, v_ref, g_out, d_out, m_out, v_out):
        g = p_ref[...] + r_ref[0].astype(F32) + r_ref[1].astype(F32) + r_ref[2].astype(F32)
        d, mn, vn = _adam_math(w_ref[...], g, m_ref[...], v_ref[...])
        g_out[...] = g
        d_out[...] = d
        m_out[...] = mn
        v_out[...] = vn

    flat = pl.BlockSpec((tr, cols), lambda i, s: (i, 0))
    return pl.pallas_call(
        body, name="adamw_shard",
        grid_spec=pltpu.PrefetchScalarGridSpec(
            num_scalar_prefetch=1, grid=(rows // tr,),
            in_specs=[pl.BlockSpec((None, tr, cols), lambda i, s: (s[0], i, 0)),
                      pl.BlockSpec((3, tr, cols), lambda i, s: (0, i, 0)), flat, flat, flat],
            out_specs=[flat] * 4),
        out_shape=[jax.ShapeDtypeStruct((rows, cols), F32)] * 4,
        compiler_params=_params(32),
    )(slot, part32, recv16, w, m, v)


def _adamw_small(gathered, w, m, v):
    rows = w.shape[0]

    def body(a_ref, w_ref, m_ref, v_ref, g_out, d_out, m_out, v_out):
        g = a_ref[0]
        for d in range(1, N_DEV):
            g = g + a_ref[d]
        dl, mn, vn = _adam_math(w_ref[...], g, m_ref[...], v_ref[...])
        g_out[...] = g
        d_out[...] = dl
        m_out[...] = mn
        v_out[...] = vn

    return pl.pallas_call(
        body, name="adamw_small",
        in_specs=[_vspec()] * 4, out_specs=[_vspec()] * 4,
        out_shape=[jax.ShapeDtypeStruct((rows, 128), F32)] * 4,
        compiler_params=_params(32),
    )(gathered, w, m, v)


def _load_weights_once(pairs, sems):
    @pl.when((pl.program_id(0) == 0) & (pl.program_id(1) == 0))
    def _():
        cps = [pltpu.make_async_copy(src, dst, sems.at[i]) for i, (src, dst) in enumerate(pairs)]
        for cp in cps:
            cp.start()
        for cp in cps:
            cp.wait()


def _ffn_fwd(x, wup_all, wd_all, layer, ln_g, ln_b, tm=512):
    t_tok = x.shape[0]

    def body(x_ref, g_ref, b_ref, wup_hbm, wd_hbm, xn_ref, xnb_ref, z_ref, gu_ref, wup_v, wd_v, acc, sems):
        j = pl.program_id(1)
        _load_weights_once([(wup_hbm.at[layer], wup_v), (wd_hbm.at[layer], wd_v)], sems)
        xb = x_ref[...].astype(BF16)
        g = _dot(xb, wup_v[j])
        u = _dot(xb, wup_v[N_FFN_CHUNK + j])
        gu_ref[0] = g.astype(BF16)
        gu_ref[1] = u.astype(BF16)
        a = (g * jax.nn.sigmoid(g) * u).astype(BF16)
        y = _dot(a, wd_v[j])

        @pl.when(j == 0)
        def _():
            acc[...] = y

        @pl.when(j > 0)
        def _():
            acc[...] += y

        @pl.when(j == N_FFN_CHUNK - 1)
        def _():
            z = ALPHA * x_ref[...] + 0.5 * acc[...]
            xhat, _ = _ln_stats(z)
            xn = xhat * g_ref[...] + b_ref[...]
            z_ref[...] = z
            xn_ref[...] = xn
            xnb_ref[...] = xn.astype(BF16)

    tok = pl.BlockSpec((tm, D_MODEL), lambda i, j: (i, 0))
    vec = pl.BlockSpec((1, D_MODEL), lambda i, j: (0, 0))
    return pl.pallas_call(
        body, name="ffn_fwd", grid=(t_tok // tm, N_FFN_CHUNK),
        in_specs=[tok, vec, vec, _anyspec(), _anyspec()],
        out_specs=[tok, tok, tok, pl.BlockSpec((2, None, tm, FFN_BLK), lambda i, j: (0, j, i, 0))],
        out_shape=[jax.ShapeDtypeStruct((t_tok, D_MODEL), F32), jax.ShapeDtypeStruct((t_tok, D_MODEL), BF16),
                   jax.ShapeDtypeStruct((t_tok, D_MODEL), F32),
                   jax.ShapeDtypeStruct((2, N_FFN_CHUNK, t_tok, FFN_BLK), BF16)],
        scratch_shapes=[pltpu.VMEM((N_DEV, D_MODEL, FFN_BLK), BF16), pltpu.VMEM((N_FFN_CHUNK, FFN_BLK, D_MODEL), BF16),
                        pltpu.VMEM((tm, D_MODEL), F32), pltpu.SemaphoreType.DMA((2,))],
        compiler_params=_params(56, ("arbitrary", "arbitrary")),
    )(x, ln_g, ln_b, wup_all, wd_all)


def _ffn_bwd(dxn, z, gu, wup_all, wd_all, layer, ln_g, tm=512):
    t_tok = dxn.shape[0]

    def body(dxn_ref, z_ref, gu_ref, g_ref, wup_hbm, wd_hbm,
             dx_ref, dy_ref, a_ref, dgu_ref, dg_ref, db_ref, wup_v, wd_v, dxacc, dyb, sems):
        i, j = pl.program_id(0), pl.program_id(1)
        _load_weights_once([(wup_hbm.at[layer], wup_v), (wd_hbm.at[layer], wd_v)], sems)

        @pl.when(j == 0)
        def _():
            dxn_t = dxn_ref[...]
            xhat, rstd = _ln_stats(z_ref[...])
            pg = jnp.sum(dxn_t * xhat, axis=0, keepdims=True)
            pb = jnp.sum(dxn_t, axis=0, keepdims=True)

            @pl.when(i == 0)
            def _():
                dg_ref[...] = pg
                db_ref[...] = pb

            @pl.when(i > 0)
            def _():
                dg_ref[...] += pg
                db_ref[...] += pb

            dz = _ln_bwd(dxn_t, xhat, rstd, g_ref[...])
            dxacc[...] = ALPHA * dz
            dy = (0.5 * dz).astype(BF16)
            dyb[...] = dy
            dy_ref[...] = dy

        da = _dot_nt(dyb[...], wd_v[j])
        g = gu_ref[0].astype(F32)
        u = gu_ref[1].astype(F32)
        sig = jax.nn.sigmoid(g)
        silu = g * sig
        a_ref[...] = (silu * u).astype(BF16)
        dg = (da * u * (sig * (1.0 + g * (1.0 - sig)))).astype(BF16)
        du = (da * silu).astype(BF16)
        dgu_ref[0] = dg
        dgu_ref[1] = du
        dxacc[...] += _dot_nt(dg, wup_v[j]) + _dot_nt(du, wup_v[N_FFN_CHUNK + j])

        @pl.when(j == N_FFN_CHUNK - 1)
        def _():
            dx_ref[...] = dxacc[...]

    tok = pl.BlockSpec((tm, D_MODEL), lambda i, j: (i, 0))
    vec = pl.BlockSpec((1, D_MODEL), lambda i, j: (0, 0))
    gu_spec = pl.BlockSpec((2, None, tm, FFN_BLK), lambda i, j: (0, j, i, 0))
    return pl.pallas_call(
        body, name="ffn_bwd", grid=(t_tok // tm, N_FFN_CHUNK),
        in_specs=[tok, tok, gu_spec, vec, _anyspec(), _anyspec()],
        out_specs=[tok, tok, pl.BlockSpec((None, tm, FFN_BLK), lambda i, j: (j, i, 0)), gu_spec, vec, vec],
        out_shape=[jax.ShapeDtypeStruct((t_tok, D_MODEL), F32), jax.ShapeDtypeStruct((t_tok, D_MODEL), BF16),
                   jax.ShapeDtypeStruct((N_FFN_CHUNK, t_tok, FFN_BLK), BF16),
                   jax.ShapeDtypeStruct((2, N_FFN_CHUNK, t_tok, FFN_BLK), BF16),
                   jax.ShapeDtypeStruct((1, D_MODEL), F32), jax.ShapeDtypeStruct((1, D_MODEL), F32)],
        scratch_shapes=[pltpu.VMEM((N_DEV, D_MODEL, FFN_BLK), BF16), pltpu.VMEM((N_FFN_CHUNK, FFN_BLK, D_MODEL), BF16),
                        pltpu.VMEM((tm, D_MODEL), F32), pltpu.VMEM((tm, D_MODEL), BF16), pltpu.SemaphoreType.DMA((2,))],
        compiler_params=_params(56, ("arbitrary", "arbitrary")),
    )(dxn, z, gu, ln_g, wup_all, wd_all)


def _matmul_tn(a, b, tk=1024):
    ga, t_tok, m = a.shape
    gb, _, n = b.shape
    groups = max(ga, gb)

    def body(a_ref, b_ref, o_ref):
        p = _dot_tn(a_ref[...].astype(BF16), b_ref[...].astype(BF16))

        @pl.when(pl.program_id(1) == 0)
        def _():
            o_ref[...] = p

        @pl.when(pl.program_id(1) > 0)
        def _():
            o_ref[...] += p

    return pl.pallas_call(
        body, name=f"matmul_tn_{m}x{n}", grid=(groups, t_tok // tk),
        in_specs=[pl.BlockSpec((None, tk, m), (lambda g, t: (g, t, 0)) if ga > 1 else (lambda g, t: (0, t, 0))),
                  pl.BlockSpec((None, tk, n), (lambda g, t: (g, t, 0)) if gb > 1 else (lambda g, t: (0, t, 0)))],
        out_specs=pl.BlockSpec((None, m, n), lambda g, t: (g, 0, 0)),
        out_shape=jax.ShapeDtypeStruct((groups, m, n), F32),
        compiler_params=_params(56, ("arbitrary", "arbitrary")),
    )(a, b)


def _in_proj(x, w_in, tm=512):
    t_tok = x.shape[0]

    def body(x_ref, w_ref, conv_ref, qkv_ref, sgu_ref, f_ref):
        xb = x_ref[...].astype(BF16)
        conv_ref[...] = _dot(xb, w_ref[:, COL_CONV:COL_QKV])
        qkv_ref[...] = _dot(xb, w_ref[:, COL_QKV:COL_SGU]).astype(BF16)
        sgu_ref[...] = _dot(xb, w_ref[:, COL_SGU:COL_F])
        f_ref[...] = _dot(xb, w_ref[:, COL_F:D_IN_PAD])

    def tok(n):
        return pl.BlockSpec((tm, n), lambda i: (i, 0))

    return pl.pallas_call(
        body, name="mix_in_proj", grid=(t_tok // tm,),
        in_specs=[tok(D_MODEL), pl.BlockSpec((D_MODEL, D_IN_PAD), lambda i: (0, 0))],
        out_specs=[tok(768), tok(1536), tok(512), tok(128)],
        out_shape=[jax.ShapeDtypeStruct((t_tok, 768), F32), jax.ShapeDtypeStruct((t_tok, 1536), BF16),
                   jax.ShapeDtypeStruct((t_tok, 512), F32), jax.ShapeDtypeStruct((t_tok, 128), F32)],
        compiler_params=_params(48, ("arbitrary",)),
    )(x, w_in)


def _shift_down(a, k):
    row = lax.broadcasted_iota(jnp.int32, a.shape, 0)
    return jnp.where(row >= k, pltpu.roll(a, k, 0), 0.0)


def _shift_up(a, k):
    rows = a.shape[0]
    row = lax.broadcasted_iota(jnp.int32, a.shape, 0)
    return jnp.where(row < rows - k, pltpu.roll(a, rows - k, 0), 0.0)


def _tril(n):
    return lax.broadcasted_iota(jnp.int32, (n, n), 0) >= lax.broadcasted_iota(jnp.int32, (n, n), 1)


def _sgu_group_of_lane():
    return lax.broadcasted_iota(jnp.int32, (1, D_SGU), 1) // (D_SGU // N_SGU_GROUPS)


def _log_sigmoid(x):
    return jnp.minimum(x, 0.0) - jnp.log1p(jnp.exp(-jnp.abs(x)))


def _mix_mid_fwd(conv, sgu, f, conv_w, b_f, sgu_g, sgu_b, w_s, b_mat, n_seq):
    t_tok = conv.shape[0]
    seq = t_tok // n_seq
    n_chunk = seq // SGU_CHUNK
    per_blk = ATT_BLK // SGU_CHUNK

    def body(conv_ref, sgu_ref, f_ref, cw_ref, bf_ref, lg_ref, lb_ref, ws_ref, bm_ref, ya_ref, yc_ref, cum_ref):
        z = conv_ref[:, 256:512] * conv_ref[:, 512:768]
        y = cw_ref[0:1, :] * _shift_down(z, 2) + cw_ref[1:2, :] * _shift_down(z, 1) + cw_ref[2:3, :] * z
        ya_ref[...] = (conv_ref[:, 0:256] * y).astype(BF16)

        tril = _tril(SGU_CHUNK)
        grp = _sgu_group_of_lane()
        wc = [jnp.where(tril, ws_ref[g], 0.0).astype(BF16) for g in range(N_SGU_GROUPS)]
        tri_f = tril.astype(F32)
        carry = jnp.zeros((1, 128), F32)
        for n in range(n_chunk):
            rows = pl.ds(n * SGU_CHUNK, SGU_CHUNK)
            u = _gelu(sgu_ref[rows, 0:256])
            vhat, _ = _ln_stats(_gelu(sgu_ref[rows, 256:512]))
            vn = (vhat * lg_ref[...] + lb_ref[...]).astype(BF16)
            mixed = bm_ref[...]
            for g in range(N_SGU_GROUPS):
                mixed = mixed + jnp.where(grp == g, _dot(wc[g], vn), 0.0)
            yc_ref[rows, :] = (u * mixed).astype(BF16)

            log_f = _log_sigmoid(f_ref[rows, :] + bf_ref[...])
            cs = _dot(tri_f, log_f, HIGHEST) + carry
            carry = cs[SGU_CHUNK - 1:SGU_CHUNK, :]
            cs_t = cs.T
            lanes = pl.ds((n % per_blk) * SGU_CHUNK, SGU_CHUNK)
            for h in range(N_HEADS):
                cum_ref[h, n // per_blk, :, lanes] = cs_t[h:h + 1, :]

    def seq_blk(n):
        return pl.BlockSpec((seq, n), lambda b: (b, 0))

    def full(shape):
        return pl.BlockSpec(shape, lambda b: (0,) * len(shape))

    return pl.pallas_call(
        body, name="mix_mid_fwd", grid=(n_seq,),
        in_specs=[seq_blk(768), seq_blk(512), seq_blk(128), full((8, 256)), full((1, 128)), full((1, 256)),
                  full((1, 256)), full((4, 128, 128)), full((128, 256))],
        out_specs=[seq_blk(256), seq_blk(256),
                   pl.BlockSpec((N_HEADS, seq // ATT_BLK, 1, ATT_BLK), lambda b: (b, 0, 0, 0))],
        out_shape=[jax.ShapeDtypeStruct((t_tok, 256), BF16), jax.ShapeDtypeStruct((t_tok, 256), BF16),
                   jax.ShapeDtypeStruct((n_seq * N_HEADS, seq // ATT_BLK, 1, ATT_BLK), F32)],
        compiler_params=_params(48, ("arbitrary",)),
    )(conv, sgu, f, conv_w, b_f, sgu_g, sgu_b, w_s, b_mat)


def _head_masks():
    lane = lax.broadcasted_iota(jnp.int32, (1, 128), 1)
    return lane < 64, lane


def _fox_fwd(qkv, cum_t, n_seq):
    t_tok = qkv.shape[0]
    seq = t_tok // n_seq
    nq = seq // ATT_BLK
    blk = ATT_BLK

    def body(q_ref, k_ref, v_ref, c0_ref, c1_ref, o_ref, lse_ref):
        qi = pl.program_id(2)
        first, _ = _head_masks()
        qs = q_ref[...] * ATT_SCALE
        zero = jnp.zeros_like(qs)
        q0 = jnp.where(first, qs, zero)
        q1 = jnp.where(first, zero, qs)
        causal = _tril(blk)
        one = jnp.ones((1, 128), BF16)

        def step(kb, carry, masked):
            m0, m1, acc0, acc1 = carry
            rows = pl.ds(pl.multiple_of(kb * blk, blk), blk)
            k = k_ref[rows, :]
            v = v_ref[rows, :]

            def head(qh, c_ref, m, acc, vh):
                s = _dot_nt(qh, k) - c_ref[kb]
                if masked:
                    s = jnp.where(causal, s, NEG)
                m_new = jnp.maximum(m, jnp.max(s, axis=1, keepdims=True))
                p = jnp.exp(s - m_new)
                return m_new, acc * jnp.exp(m - m_new) + _dot(p.astype(BF16), vh)

            m0, acc0 = head(q0, c0_ref, m0, acc0, jnp.where(first, v, one))
            m1, acc1 = head(q1, c1_ref, m1, acc1, jnp.where(first, one, v))
            return m0, m1, acc0, acc1

        col = jnp.full((blk, 1), NEG, F32)
        zacc = jnp.zeros((blk, 128), F32)
        carry = lax.fori_loop(0, qi, lambda kb, cr: step(kb, cr, False), (col, col, zacc, zacc))
        m0, m1, acc0, acc1 = step(qi, carry, True)
        l0 = pltpu.roll(acc0, 64, 1)
        l1 = pltpu.roll(acc1, 64, 1)
        o_ref[...] = jnp.where(first, acc0 / l0, acc1 / l1).astype(BF16)
        lse_ref[...] = jnp.where(first, m0 + jnp.log(l0), m1 + jnp.log(l1))

    cum_spec0 = pl.BlockSpec((None, nq, 1, blk), lambda b, hp, qi: (b * N_HEADS + 2 * hp, 0, 0, 0))
    cum_spec1 = pl.BlockSpec((None, nq, 1, blk), lambda b, hp, qi: (b * N_HEADS + 2 * hp + 1, 0, 0, 0))
    out_spec = pl.BlockSpec((blk, 128), lambda b, hp, qi: (b * nq + qi, hp))
    return pl.pallas_call(
        body, name="fox_fwd", grid=(n_seq, 4, nq),
        in_specs=[pl.BlockSpec((blk, 128), lambda b, hp, qi: (b * nq + qi, hp)),
                  pl.BlockSpec((seq, 128), lambda b, hp, qi: (b, 4 + hp)),
                  pl.BlockSpec((seq, 128), lambda b, hp, qi: (b, 8 + hp)), cum_spec0, cum_spec1],
        out_specs=[out_spec, out_spec],
        out_shape=[jax.ShapeDtypeStruct((t_tok, D_FOX), BF16), jax.ShapeDtypeStruct((t_tok, D_FOX), F32)],
        compiler_params=_params(32, ("arbitrary", "arbitrary", "arbitrary")),
    )(qkv, qkv, qkv, cum_t, cum_t)


def _fox_bwd(qkv, cum_t, o, lse, d_o, n_seq):
    t_tok = qkv.shape[0]
    seq = t_tok // n_seq
    nk = seq // ATT_BLK
    blk = ATT_BLK

    def body(q_ref, k_ref, v_ref, c0_ref, c1_ref, o_ref, lse_ref, do_ref,
             dq_ref, dk_ref, dv_ref, drow_ref, dcol_ref):
        kb = pl.program_id(2)
        first, lane = _head_masks()
        second = jnp.logical_not(first)
        k = k_ref[...]
        v = v_ref[...]
        zero = jnp.zeros_like(k)
        one = jnp.ones((1, 128), BF16)
        ks = k * ATT_SCALE
        causal = _tril(blk)

        @pl.when(kb == 0)
        def _():
            dq_ref[...] = jnp.zeros_like(dq_ref)
            drow_ref[...] = jnp.zeros_like(drow_ref)

        def step(qi, carry, masked):
            rows = pl.ds(pl.multiple_of(qi * blk, blk), blk)
            qs = q_ref[rows, :] * ATT_SCALE
            d_o = do_ref[rows, :]
            dd = d_o.astype(F32) * o_ref[rows, :].astype(F32)
            lse_t = lse_ref[rows, :]

            def head(mine, c, lse_lane, dk, dv):
                qh = jnp.where(mine, qs, zero)
                doh = jnp.where(mine, d_o, zero)
                delta = jnp.sum(jnp.where(mine, dd, 0.0), axis=1, keepdims=True)
                lse_h = jnp.sum(jnp.where(lane == lse_lane, lse_t, 0.0), axis=1, keepdims=True)
                s = _dot_nt(qh, k) - c
                if masked:
                    s = jnp.where(causal, s, NEG)
                p = jnp.exp(s - lse_h)
                ds = (p * (_dot_nt(doh, v) - delta)).astype(BF16)
                dk = dk + _dot_tn(ds, jnp.where(mine, qs, one))
                dv = dv + _dot_tn(p.astype(BF16), doh)
                return dk, dv, _dot(ds, jnp.where(mine, ks, one))

            dk0, dv0, dk1, dv1 = carry
            dk0, dv0, dq0 = head(first, c0_ref[...], 0, dk0, dv0)
            dk1, dv1, dq1 = head(second, c1_ref[...], 64, dk1, dv1)
            dq_ref[rows, :] += jnp.where(first, dq0, dq1)
            drow_ref[rows, :] += jnp.where(first, dq1, dq0)
            return dk0, dv0, dk1, dv1

        zt = jnp.zeros((blk, 128), F32)
        carry = step(kb, (zt, zt, zt, zt), True)
        dk0, dv0, dk1, dv1 = lax.fori_loop(kb + 1, nk, lambda qi, cr: step(qi, cr, False), carry)
        dk_ref[...] = jnp.where(first, dk0, dk1).astype(BF16)
        dcol_ref[...] = jnp.where(first, dk1, dk0)
        dv_ref[...] = (dv0 + dv1).astype(BF16)

    def seq_spec(col0):
        return pl.BlockSpec((seq, 128), lambda b, hp, kb: (b, col0 + hp))

    def key_spec(col0):
        return pl.BlockSpec((blk, 128), lambda b, hp, kb: (b * nk + kb, col0 + hp))

    def cum_spec(h):
        return pl.BlockSpec((None, None, 1, blk), lambda b, hp, kb: (b * N_HEADS + 2 * hp + h, kb, 0, 0))

    return pl.pallas_call(
        body, name="fox_bwd", grid=(n_seq, 4, nk),
        in_specs=[seq_spec(0), key_spec(4), key_spec(8), cum_spec(0), cum_spec(1), seq_spec(0), seq_spec(0), seq_spec(0)],
        out_specs=[seq_spec(0), key_spec(0), key_spec(0), seq_spec(0), key_spec(0)],
        out_shape=[jax.ShapeDtypeStruct((t_tok, D_FOX), F32), jax.ShapeDtypeStruct((t_tok, D_FOX), BF16),
                   jax.ShapeDtypeStruct((t_tok, D_FOX), BF16), jax.ShapeDtypeStruct((t_tok, D_FOX), F32),
                   jax.ShapeDtypeStruct((t_tok, D_FOX), F32)],
        compiler_params=_params(48, ("arbitrary", "arbitrary", "arbitrary")),
    )(qkv, qkv, qkv, cum_t, cum_t, o, lse, d_o)


def _mix_out_fwd(ya, yb, yc, x, w_out_all, layer, ln_g, ln_b, tm=512):
    t_tok = x.shape[0]

    def body(ya_ref, yb_ref, yc_ref, x_ref, w_ref, g_ref, b_ref, xn_ref, xnb_ref, z_ref):
        mo = _dot(ya_ref[...], w_ref[0:256, :]) + _dot(yb_ref[...], w_ref[256:768, :]) + _dot(yc_ref[...], w_ref[768:1024, :])
        z = ALPHA * x_ref[...] + mo
        xhat, _ = _ln_stats(z)
        xn = xhat * g_ref[...] + b_ref[...]
        z_ref[...] = z
        xn_ref[...] = xn
        xnb_ref[...] = xn.astype(BF16)

    def tok(n):
        return pl.BlockSpec((tm, n), lambda i: (i, 0))

    vec = pl.BlockSpec((1, D_MODEL), lambda i: (0, 0))
    return pl.pallas_call(
        body, name="mix_out_fwd", grid=(t_tok // tm,),
        in_specs=[tok(256), tok(512), tok(256), tok(D_MODEL),
                  pl.BlockSpec((None, D_MODEL, D_MODEL), lambda i: (layer, 0, 0)), vec, vec],
        out_specs=[tok(D_MODEL)] * 3,
        out_shape=[jax.ShapeDtypeStruct((t_tok, D_MODEL), F32), jax.ShapeDtypeStruct((t_tok, D_MODEL), BF16),
                   jax.ShapeDtypeStruct((t_tok, D_MODEL), F32)],
        compiler_params=_params(40, ("arbitrary",)),
    )(ya, yb, yc, x, w_out_all, ln_g, ln_b)


def _mix_out_bwd(dxn, z, w_out_all, layer, ln_g, tm=512):
    t_tok = dxn.shape[0]

    def body(dxn_ref, z_ref, w_ref, g_ref, dz_ref, dzb_ref, dya_ref, dyb_ref, dyc_ref, dg_ref, db_ref):
        i = pl.program_id(0)
        dxn_t = dxn_ref[...]
        xhat, rstd = _ln_stats(z_ref[...])
        pg = jnp.sum(dxn_t * xhat, axis=0, keepdims=True)
        pb = jnp.sum(dxn_t, axis=0, keepdims=True)

        @pl.when(i == 0)
        def _():
            dg_ref[...] = pg
            db_ref[...] = pb

        @pl.when(i > 0)
        def _():
            dg_ref[...] += pg
            db_ref[...] += pb

        dz = _ln_bwd(dxn_t, xhat, rstd, g_ref[...])
        dzb = dz.astype(BF16)
        dz_ref[...] = dz
        dzb_ref[...] = dzb
        dya_ref[...] = _dot_nt(dzb, w_ref[0:256, :])
        dyb_ref[...] = _dot_nt(dzb, w_ref[256:768, :]).astype(BF16)
        dyc_ref[...] = _dot_nt(dzb, w_ref[768:1024, :])

    def tok(n):
        return pl.BlockSpec((tm, n), lambda i: (i, 0))

    vec = pl.BlockSpec((1, D_MODEL), lambda i: (0, 0))
    return pl.pallas_call(
        body, name="mix_out_bwd", grid=(t_tok // tm,),
        in_specs=[tok(D_MODEL), tok(D_MODEL), pl.BlockSpec((None, D_MODEL, D_MODEL), lambda i: (layer, 0, 0)), vec],
        out_specs=[tok(D_MODEL), tok(D_MODEL), tok(256), tok(512), tok(256), vec, vec],
        out_shape=[jax.ShapeDtypeStruct((t_tok, D_MODEL), F32), jax.ShapeDtypeStruct((t_tok, D_MODEL), BF16),
                   jax.ShapeDtypeStruct((t_tok, 256), F32), jax.ShapeDtypeStruct((t_tok, 512), BF16),
                   jax.ShapeDtypeStruct((t_tok, 256), F32),
                   jax.ShapeDtypeStruct((1, D_MODEL), F32), jax.ShapeDtypeStruct((1, D_MODEL), F32)],
        compiler_params=_params(40, ("arbitrary",)),
    )(dxn, z, w_out_all, ln_g)


def _conv_bwd(conv, dya, conv_w, n_seq):
    t_tok = conv.shape[0]
    seq = t_tok // n_seq

    def body(conv_ref, dya_ref, cw_ref, dconv_ref, dcw_ref):
        @pl.when(pl.program_id(0) == 0)
        def _():
            dcw_ref[...] = jnp.zeros_like(dcw_ref)

        z = conv_ref[:, 256:512] * conv_ref[:, 512:768]
        z1 = _shift_down(z, 1)
        z2 = _shift_down(z, 2)
        y = cw_ref[0:1, :] * z2 + cw_ref[1:2, :] * z1 + cw_ref[2:3, :] * z
        dya_t = dya_ref[...]
        dconv_ref[:, 0:256] = (dya_t * y).astype(BF16)
        dy = dya_t * conv_ref[:, 0:256]
        dcw_ref[0:1, :] += jnp.sum(dy * z2, axis=0, keepdims=True)
        dcw_ref[1:2, :] += jnp.sum(dy * z1, axis=0, keepdims=True)
        dcw_ref[2:3, :] += jnp.sum(dy * z, axis=0, keepdims=True)
        dz = cw_ref[2:3, :] * dy + cw_ref[1:2, :] * _shift_up(dy, 1) + cw_ref[0:1, :] * _shift_up(dy, 2)
        dconv_ref[:, 256:512] = (dz * conv_ref[:, 512:768]).astype(BF16)
        dconv_ref[:, 512:768] = (dz * conv_ref[:, 256:512]).astype(BF16)

    def seq_blk(n):
        return pl.BlockSpec((seq, n), lambda b: (b, 0))

    par = pl.BlockSpec((8, 256), lambda b: (0, 0))
    return pl.pallas_call(
        body, name="conv_bwd", grid=(n_seq,),
        in_specs=[seq_blk(768), seq_blk(256), par], out_specs=[seq_blk(768), par],
        out_shape=[jax.ShapeDtypeStruct((t_tok, 768), BF16), jax.ShapeDtypeStruct((8, 256), F32)],
        compiler_params=_params(56, ("arbitrary",)),
    )(conv, dya, conv_w)


def _sgu_gate_bwd(sgu, f, dyc, drow, dcol, b_f, sgu_g, sgu_b, w_s, b_mat, n_seq):
    t_tok = sgu.shape[0]
    seq = t_tok // n_seq
    n_chunk = seq // SGU_CHUNK

    def body(sgu_ref, f_ref, dyc_ref, drow_ref, dcol_ref, bf_ref, lg_ref, lb_ref, ws_ref, bm_ref,
             dsgu_ref, df_ref, dbf_ref, dlg_ref, dlb_ref, dws_ref, dbs_ref, dbm_acc):
        b = pl.program_id(0)

        @pl.when(b == 0)
        def _():
            for r in (dbf_ref, dlg_ref, dlb_ref, dws_ref, dbm_acc):
                r[...] = jnp.zeros_like(r)

        tril = _tril(SGU_CHUNK)
        grp = _sgu_group_of_lane()
        wc = [jnp.where(tril, ws_ref[g], 0.0).astype(BF16) for g in range(N_SGU_GROUPS)]
        for n in range(n_chunk):
            rows = pl.ds(n * SGU_CHUNK, SGU_CHUNK)
            su = sgu_ref[rows, 0:256]
            sv = sgu_ref[rows, 256:512]
            u = _gelu(su)
            vhat, rstd = _ln_stats(_gelu(sv))
            vn = (vhat * lg_ref[...] + lb_ref[...]).astype(BF16)
            mixed = bm_ref[...]
            for g in range(N_SGU_GROUPS):
                mixed = mixed + jnp.where(grp == g, _dot(wc[g], vn), 0.0)
            dyc_t = dyc_ref[rows, :]
            dsgu_ref[rows, 0:256] = (dyc_t * mixed * _gelu_grad(su)).astype(BF16)
            dmixed = dyc_t * u
            dbm_acc[...] += dmixed
            dvn = jnp.zeros((SGU_CHUNK, D_SGU), F32)
            for g in range(N_SGU_GROUPS):
                dm_g = jnp.where(grp == g, dmixed, 0.0).astype(BF16)
                dws_ref[g] += _dot_nt(dm_g, vn)
                dvn = dvn + _dot_tn(wc[g], dm_g)
            dlg_ref[...] += jnp.sum(dvn * vhat, axis=0, keepdims=True)
            dlb_ref[...] += jnp.sum(dvn, axis=0, keepdims=True)
            dsgu_ref[rows, 256:512] = (_ln_bwd(dvn, vhat, rstd, lg_ref[...]) * _gelu_grad(sv)).astype(BF16)

        later = (lax.broadcasted_iota(jnp.int32, (128, 128), 0) <= lax.broadcasted_iota(jnp.int32, (128, 128), 1)).astype(F32)
        head = lax.broadcasted_iota(jnp.int32, (D_FOX, 128), 1)
        pick = (lax.broadcasted_iota(jnp.int32, (D_FOX, 128), 0) == 128 * (head // 2) + 64 * (1 - head % 2)).astype(F32)
        carry = jnp.zeros((1, 128), F32)
        for n in reversed(range(n_chunk)):
            rows = pl.ds(n * SGU_CHUNK, SGU_CHUNK)
            dcum_n = _dot(drow_ref[rows, :] - dcol_ref[rows, :], pick, HIGHEST)
            dlf = _dot(later, dcum_n, HIGHEST) + carry
            carry = carry + jnp.sum(dcum_n, axis=0, keepdims=True)
            df = dlf * jax.nn.sigmoid(-(f_ref[rows, :] + bf_ref[...]))
            df_ref[rows, :] = df.astype(BF16)
            dbf_ref[...] += jnp.sum(df, axis=0, keepdims=True)

        @pl.when(b == n_seq - 1)
        def _():
            for g in range(N_SGU_GROUPS):
                dws_ref[g] = jnp.where(tril, dws_ref[g], 0.0)
            sel = (lax.broadcasted_iota(jnp.int32, (D_SGU, 128), 0) // (D_SGU // N_SGU_GROUPS)
                   == lax.broadcasted_iota(jnp.int32, (D_SGU, 128), 1)).astype(F32)
            dbs_ref[...] = _dot(dbm_acc[...], sel, HIGHEST)

    def seq_blk(n):
        return pl.BlockSpec((seq, n), lambda b: (b, 0))

    def full(shape):
        return pl.BlockSpec(shape, lambda b: (0,) * len(shape))

    param_shapes = [(1, 128), (1, 256), (1, 256), (4, 128, 128), (128, 128)]
    return pl.pallas_call(
        body, name="sgu_gate_bwd", grid=(n_seq,),
        in_specs=[seq_blk(512), seq_blk(128), seq_blk(256), seq_blk(D_FOX), seq_blk(D_FOX),
                  full((1, 128)), full((1, 256)), full((1, 256)), full((4, 128, 128)), full((128, 256))],
        out_specs=[seq_blk(512), seq_blk(128)] + [full(s) for s in param_shapes],
        out_shape=[jax.ShapeDtypeStruct((t_tok, 512), BF16), jax.ShapeDtypeStruct((t_tok, 128), BF16)]
        + [jax.ShapeDtypeStruct(s, F32) for s in param_shapes],
        scratch_shapes=[pltpu.VMEM((128, 256), F32)],
        compiler_params=_params(48, ("arbitrary",)),
    )(sgu, f, dyc, drow, dcol, b_f, sgu_g, sgu_b, w_s, b_mat)


def _mix_in_bwd(dconv, dq, dk, dv, dsgu, df, dz, w_in, tm=512):
    t_tok = dz.shape[0]

    def body(dconv_ref, dq_ref, dk_ref, dv_ref, dsgu_ref, df_ref, dz_ref, w_ref, dx_ref, dp_ref):
        dqb = dq_ref[...].astype(BF16)
        pieces = [(COL_CONV, dconv_ref[...]), (COL_QKV, dqb), (COL_QKV + 512, dk_ref[...]), (COL_QKV + 1024, dv_ref[...]),
                  (COL_SGU, dsgu_ref[...]), (COL_F, df_ref[...])]
        dx = ALPHA * dz_ref[...]
        for col, val in pieces:
            width = val.shape[1]
            dp_ref[:, col:col + width] = val
            dx = dx + _dot_nt(val, w_ref[:, col:col + width])
        dx_ref[...] = dx

    def tok(n):
        return pl.BlockSpec((tm, n), lambda i: (i, 0))

    return pl.pallas_call(
        body, name="mix_in_bwd", grid=(t_tok // tm,),
        in_specs=[tok(768), tok(512), tok(512), tok(512), tok(512), tok(128), tok(D_MODEL),
                  pl.BlockSpec((D_MODEL, D_IN_PAD), lambda i: (0, 0))],
        out_specs=[tok(D_MODEL), tok(D_IN_PAD)],
        out_shape=[jax.ShapeDtypeStruct((t_tok, D_MODEL), F32), jax.ShapeDtypeStruct((t_tok, D_IN_PAD), BF16)],
        compiler_params=_params(48, ("arbitrary",)),
    )(dconv, dq, dk, dv, dsgu, df, dz, w_in)


def _loss_grad(y, target, tm=512):
    t_tok = y.shape[0]

    def body(y_ref, t_ref, dy_ref, loss_ref):
        err = y_ref[...] - t_ref[...]
        dy_ref[...] = err * (1.0 / D_MODEL)
        part = jnp.sum(jnp.sum(err * err, axis=1, keepdims=True), axis=0, keepdims=True) * (0.5 / D_MODEL)

        @pl.when(pl.program_id(0) == 0)
        def _():
            loss_ref[...] = jnp.zeros_like(loss_ref)

        loss_ref[...] += part

    tok = pl.BlockSpec((tm, D_MODEL), lambda i: (i, 0))
    return pl.pallas_call(
        body, name="loss_grad", grid=(t_tok // tm,),
        in_specs=[tok, tok], out_specs=[tok, pl.BlockSpec((1, 128), lambda i: (0, 0))],
        out_shape=[jax.ShapeDtypeStruct((t_tok, D_MODEL), F32), jax.ShapeDtypeStruct((1, 128), F32)],
        compiler_params=_params(32, ("arbitrary",)),
    )(y, target)


def _pad_rows(a, rows):
    return jnp.pad(a, ((0, rows - a.shape[0]), (0, 0)))


def _w_in_padded(w):
    zeros = jnp.zeros(w.shape[:-1] + (D_IN_PAD - COL_F - N_HEADS,), w.dtype)
    return jnp.concatenate([w[..., :F_ORIG], w[..., F_ORIG + N_HEADS:], w[..., F_ORIG:F_ORIG + N_HEADS], zeros], axis=-1)


def _w_in_unpadded(dw):
    return jnp.concatenate([dw[:, :F_ORIG], dw[:, COL_F:COL_F + N_HEADS], dw[:, F_ORIG:COL_F]], axis=-1)


SMALL_ROWS_PER_LAYER = 6 * 8 + 2 + 2 + 512 + 4 + 1 + 6
SMALL_ROWS = 1152


def _pack_small(p):
    rows = []
    for l in range(DEPTH):
        for name in ("ln1_g", "ln1_b", "ln2_g", "ln2_b", "ln3_g", "ln3_b", "sgu_ln_g", "sgu_ln_b"):
            rows.append(p[name][l].reshape(-1, 128))
        rows.append(p["sgu_w_s"][l].reshape(-1, 128))
        rows.append(p["sgu_b_s"][l].reshape(-1, 128))
        rows.append(jnp.pad(p["fox_b_f"][l], (0, 128 - N_HEADS)).reshape(1, 128))
        rows.append(p["conv_w"][l].reshape(-1, 128))
    return _pad_rows(jnp.concatenate(rows, axis=0), SMALL_ROWS)


def _unpack_small(a):
    out = {}
    r = 0

    def take(n):
        nonlocal r
        piece = a[r:r + n]
        r += n
        return piece

    per_layer = []
    for l in range(DEPTH):
        d = {}
        for name in ("ln1_g", "ln1_b", "ln2_g", "ln2_b", "ln3_g", "ln3_b"):
            d[name] = take(8).reshape(D_MODEL)
        for name in ("sgu_ln_g", "sgu_ln_b"):
            d[name] = take(2).reshape(D_SGU)
        d["sgu_w_s"] = take(512).reshape(N_SGU_GROUPS, SGU_CHUNK, SGU_CHUNK)
        d["sgu_b_s"] = take(4).reshape(N_SGU_GROUPS, SGU_CHUNK)
        d["fox_b_f"] = take(1).reshape(128)[:N_HEADS]
        d["conv_w"] = take(6).reshape(3, D_CONV)
        per_layer.append(d)
    for name in per_layer[0]:
        out[name] = jnp.stack([per_layer[l][name] for l in range(DEPTH)])
    return out


SMALL_NAMES = ("ln1_g", "ln1_b", "fox_b_f", "sgu_ln_g", "sgu_ln_b", "sgu_w_s", "sgu_b_s", "ln2_g", "ln2_b", "ln3_g", "ln3_b")
BIG_NAMES = ("ffn1_w_up", "ffn1_w_down", "mix_w_in", "mix_w_out", "ffn2_w_up", "ffn2_w_down")
WEIGHT_ORDER = ("ln1_g", "ln1_b", "ffn1_w_up", "ffn1_w_down", "mix_w_in", "fox_b_f", "conv_w", "sgu_ln_g", "sgu_ln_b",
                "sgu_w_s", "sgu_b_s", "mix_w_out", "ln2_g", "ln2_b", "ffn2_w_up", "ffn2_w_down", "ln3_g", "ln3_b")


def _local_step(x, target, wts, small, n_seq):
    def vec(a):
        return a.reshape(1, -1)

    saved = []
    h = x
    for l in range(DEPTH):
        s = {"x0": h}
        h1, h1b, s["z1"], s["gu1"] = _ffn_fwd(h, wts["ffn1_up"], wts["ffn1_down"], l, vec(small["ln1_g"][l]), vec(small["ln1_b"][l]))
        s["x1"], s["x1b"] = h1, h1b
        conv, qkv, sgu, f = _in_proj(h1, wts["w_in"][l])
        cw = _pad_rows(small["conv_w"][l], 8)
        bf = jnp.pad(small["fox_b_f"][l], (0, 128 - N_HEADS)).reshape(1, 128)
        b_mat = jnp.repeat(small["sgu_b_s"][l].T, D_SGU // N_SGU_GROUPS, axis=1)
        mid_params = (cw, bf, vec(small["sgu_ln_g"][l]), vec(small["sgu_ln_b"][l]), small["sgu_w_s"][l], b_mat)
        ya, yc, cum_t = _mix_mid_fwd(conv, sgu, f, *mid_params, n_seq)
        yb, lse = _fox_fwd(qkv, cum_t, n_seq)
        h2, h2b, s["z2"] = _mix_out_fwd(ya, yb, yc, h1, wts["w_out"], l, vec(small["ln2_g"][l]), vec(small["ln2_b"][l]))
        s.update(conv=conv, qkv=qkv, sgu=sgu, f=f, mid_params=mid_params, ya=ya, yb=yb, yc=yc, cum_t=cum_t, lse=lse,
                 x2=h2, x2b=h2b)
        h3, h3b, s["z3"], s["gu2"] = _ffn_fwd(h2, wts["ffn2_up"], wts["ffn2_down"], l, vec(small["ln3_g"][l]), vec(small["ln3_b"][l]))
        s["x3b"] = h3b
        saved.append(s)
        h = h3

    dh, loss = _loss_grad(h, target)

    big_grads = [None] * DEPTH
    small_grads = [None] * DEPTH
    for l in reversed(range(DEPTH)):
        s = saved[l]
        x0b = saved[l - 1]["x3b"] if l > 0 else x.astype(BF16)
        bg, sg = {}, {}
        dh, dy, a, dgu, sg["ln3_g"], sg["ln3_b"] = _ffn_bwd(dh, s["z3"], s["gu2"], wts["ffn2_up"], wts["ffn2_down"], l, vec(small["ln3_g"][l]))
        bg["ffn2_w_up"] = _matmul_tn(s["x2b"][None], dgu.reshape(N_DEV, -1, FFN_BLK))
        bg["ffn2_w_down"] = _matmul_tn(a, dy[None]).reshape(N_DEV, FFN_BLK // 2, D_MODEL)
        dz, dzb, dya, dyb, dyc, sg["ln2_g"], sg["ln2_b"] = _mix_out_bwd(dh, s["z2"], wts["w_out"], l, vec(small["ln2_g"][l]))
        dwo = [_matmul_tn(y[None], dzb[None])[0] for y in (s["ya"], s["yb"], s["yc"])]
        bg["mix_w_out"] = jnp.concatenate(dwo, axis=0).reshape(N_DEV, D_MODEL // N_DEV, D_MODEL)
        dq, dk, dv, drow, dcol = _fox_bwd(s["qkv"], s["cum_t"], s["yb"], s["lse"], dyb, n_seq)
        dconv, dcw = _conv_bwd(s["conv"], dya, s["mid_params"][0], n_seq)
        dsgu, df, dbf, dlg, dlb, dws, dbs = _sgu_gate_bwd(s["sgu"], s["f"], dyc, drow, dcol, *s["mid_params"][1:], n_seq)
        sg.update(conv_w=dcw[:3], fox_b_f=dbf[0, :N_HEADS], sgu_ln_g=dlg[0], sgu_ln_b=dlb[0], sgu_w_s=dws,
                  sgu_b_s=dbs[:, :N_SGU_GROUPS].T)
        dh, dp = _mix_in_bwd(dconv, dq, dk, dv, dsgu, df, dz, wts["w_in"][l])
        dwin = _w_in_unpadded(_matmul_tn(s["x1b"][None], dp[None], tk=512)[0])
        bg["mix_w_in"] = jnp.transpose(dwin.reshape(D_MODEL, N_DEV, D_IN_SHARD), (1, 0, 2))
        dh, dy, a, dgu, sg["ln1_g"], sg["ln1_b"] = _ffn_bwd(dh, s["z1"], s["gu1"], wts["ffn1_up"], wts["ffn1_down"], l, vec(small["ln1_g"][l]))
        bg["ffn1_w_up"] = _matmul_tn(x0b[None], dgu.reshape(N_DEV, -1, FFN_BLK))
        bg["ffn1_w_down"] = _matmul_tn(a, dy[None]).reshape(N_DEV, FFN_BLK // 2, D_MODEL)
        for name in ("ln1_g", "ln1_b", "ln2_g", "ln2_b", "ln3_g", "ln3_b"):
            sg[name] = sg[name][0]
        big_grads[l] = bg
        small_grads[l] = sg
    return loss, dh, big_grads, small_grads


def kernel(x, ln1_g, ln1_b, ffn1_w_up, ffn1_w_down, mix_w_in, fox_b_f, conv_w, sgu_ln_g, sgu_ln_b, sgu_w_s, sgu_b_s, mix_w_out, ln2_g, ln2_b, ffn2_w_up, ffn2_w_down, ln3_g, ln3_b, loss_target, m_ln1_g, m_ln1_b, m_ffn1_w_up, m_ffn1_w_down, m_mix_w_in, m_fox_b_f, m_conv_w, m_sgu_ln_g, m_sgu_ln_b, m_sgu_w_s, m_sgu_b_s, m_mix_w_out, m_ln2_g, m_ln2_b, m_ffn2_w_up, m_ffn2_w_down, m_ln3_g, m_ln3_b, v_ln1_g, v_ln1_b, v_ffn1_w_up, v_ffn1_w_down, v_mix_w_in, v_fox_b_f, v_conv_w, v_sgu_ln_g, v_sgu_ln_b, v_sgu_w_s, v_sgu_b_s, v_mix_w_out, v_ln2_g, v_ln2_b, v_ffn2_w_up, v_ffn2_w_down, v_ln3_g, v_ln3_b):
    w = dict(ln1_g=ln1_g, ln1_b=ln1_b, ffn1_w_up=ffn1_w_up, ffn1_w_down=ffn1_w_down, mix_w_in=mix_w_in, fox_b_f=fox_b_f,
             conv_w=conv_w, sgu_ln_g=sgu_ln_g, sgu_ln_b=sgu_ln_b, sgu_w_s=sgu_w_s, sgu_b_s=sgu_b_s, mix_w_out=mix_w_out,
             ln2_g=ln2_g, ln2_b=ln2_b, ffn2_w_up=ffn2_w_up, ffn2_w_down=ffn2_w_down, ln3_g=ln3_g, ln3_b=ln3_b)
    m = dict(ln1_g=m_ln1_g, ln1_b=m_ln1_b, ffn1_w_up=m_ffn1_w_up, ffn1_w_down=m_ffn1_w_down, mix_w_in=m_mix_w_in,
             fox_b_f=m_fox_b_f, conv_w=m_conv_w, sgu_ln_g=m_sgu_ln_g, sgu_ln_b=m_sgu_ln_b, sgu_w_s=m_sgu_w_s,
             sgu_b_s=m_sgu_b_s, mix_w_out=m_mix_w_out, ln2_g=m_ln2_g, ln2_b=m_ln2_b, ffn2_w_up=m_ffn2_w_up,
             ffn2_w_down=m_ffn2_w_down, ln3_g=m_ln3_g, ln3_b=m_ln3_b)
    v = dict(ln1_g=v_ln1_g, ln1_b=v_ln1_b, ffn1_w_up=v_ffn1_w_up, ffn1_w_down=v_ffn1_w_down, mix_w_in=v_mix_w_in,
             fox_b_f=v_fox_b_f, conv_w=v_conv_w, sgu_ln_g=v_sgu_ln_g, sgu_ln_b=v_sgu_ln_b, sgu_w_s=v_sgu_w_s,
             sgu_b_s=v_sgu_b_s, mix_w_out=v_mix_w_out, ln2_g=v_ln2_g, ln2_b=v_ln2_b, ffn2_w_up=v_ffn2_w_up,
             ffn2_w_down=v_ffn2_w_down, ln3_g=v_ln3_g, ln3_b=v_ln3_b)

    mx, my, mc = lax.axis_index("x"), lax.axis_index("y"), lax.axis_index("c")
    me = 4 * mx + 2 * my + mc
    n_seq, seq, _ = x.shape
    t_tok = n_seq * seq

    gathered = _allgather_big([w[name].astype(BF16) for name in BIG_NAMES])
    g = dict(zip(BIG_NAMES, gathered))
    w_in_full = jnp.transpose(g["mix_w_in"], (0, 2, 1, 3)).reshape(DEPTH, D_MODEL, D_IN)
    wts = dict(ffn1_up=g["ffn1_w_up"], ffn2_up=g["ffn2_w_up"],
               ffn1_down=g["ffn1_w_down"].reshape(DEPTH, N_FFN_CHUNK, FFN_BLK, D_MODEL),
               ffn2_down=g["ffn2_w_down"].reshape(DEPTH, N_FFN_CHUNK, FFN_BLK, D_MODEL),
               w_in=_w_in_padded(w_in_full), w_out=g["mix_w_out"].reshape(DEPTH, D_MODEL, D_MODEL))
    cw_rows = _pad_rows(conv_w.reshape(DEPTH * 3, D_CONV // N_DEV), 8)
    cw_all = _allgather_small(jnp.pad(cw_rows, ((0, 0), (0, 128 - D_CONV // N_DEV))))
    conv_w_full = jnp.transpose(cw_all[:, :DEPTH * 3, :D_CONV // N_DEV], (1, 0, 2)).reshape(DEPTH, 3, D_CONV)
    small = {name: w[name] for name in SMALL_NAMES}
    small["conv_w"] = conv_w_full

    loss_dev, grad_x, big_grads, small_grads = _local_step(
        x.reshape(t_tok, D_MODEL), loss_target.reshape(t_tok, D_MODEL), wts, small, n_seq)
    loss = lax.psum(loss_dev[0, 0], ("x", "y", "c"))

    core = mc.reshape(1).astype(jnp.int32)
    slot = (2 * mx + my).reshape(1).astype(jnp.int32)
    items = [(name, l) for l in range(DEPTH) for name in BIG_NAMES]
    blocked = [big_grads[l][name] for name, l in items]
    from_sibling = _sibling_exchange(blocked)
    partials = [_chip_partial(gb, r, core) for gb, r in zip(blocked, from_sibling)]
    from_chips = _chip_exchange([p16 for _, p16 in partials])
    res = {}
    for (name, l), (p32, _), r16 in zip(items, partials, from_chips):
        rows = p32.shape[1]
        res[(name, l)] = _adamw_shard(p32, r16, slot, *[t[name][l].reshape(rows, -1) for t in (w, m, v)])
    out = {}
    for name in BIG_NAMES:
        out[name] = [jnp.stack([res[(name, l)][k] for l in range(DEPTH)]).reshape(w[name].shape) for k in range(4)]

    sg = {name: jnp.stack([small_grads[l][name] for l in range(DEPTH)]) for name in SMALL_NAMES + ("conv_w",)}
    all_small = _allgather_small(_pack_small(sg))

    def widen(a):
        return lax.dynamic_update_slice(jnp.zeros((DEPTH, 3, D_CONV), F32), a, (0, 0, me * (D_CONV // N_DEV)))

    packed = [_pack_small({**{name: t[name] for name in SMALL_NAMES}, "conv_w": widen(t["conv_w"])}) for t in (w, m, v)]
    small_out = [_unpack_small(a) for a in _adamw_small(all_small, *packed)]
    for name in SMALL_NAMES:
        out[name] = [small_out[k][name] for k in range(4)]
    out["conv_w"] = [lax.dynamic_slice(small_out[k]["conv_w"], (0, 0, me * (D_CONV // N_DEV)), (DEPTH, 3, D_CONV // N_DEV))
                     for k in range(4)]

    return (loss, grad_x.reshape(x.shape), *[out[name][0] for name in WEIGHT_ORDER], *[out[name][1] for name in WEIGHT_ORDER],
            *[out[name][2] for name in WEIGHT_ORDER], *[out[name][3] for name in WEIGHT_ORDER])
```

```python
import functools

import jax
import jax.numpy as jnp
from jax import lax
from jax.experimental import pallas as pl
from jax.experimental.pallas import tpu as pltpu

F32 = jnp.float32
BF16 = jnp.bfloat16
MESH = pl.DeviceIdType.MESH

N_DEV = 8
DEPTH = 2
D_MODEL = 1024
D_FF = 2816
FFN_BLK = 2 * D_FF // N_DEV
N_FFN_CHUNK = D_FF // FFN_BLK
D_CONV = 256
D_FOX = 512
N_HEADS = 8
D_SGU = 256
N_SGU_GROUPS = 4
SGU_CHUNK = 128
D_IN = 3 * D_CONV + 3 * D_FOX + N_HEADS + 2 * D_SGU
D_IN_SHARD = D_IN // N_DEV
COL_CONV, COL_QKV, COL_SGU, COL_F = 0, 768, 2304, 2816
D_IN_PAD = 2944
F_ORIG = 3 * D_CONV + 3 * D_FOX
ALPHA = (2 * DEPTH) ** 0.25
LN_EPS = 1e-5
ATT_SCALE = 0.125
ATT_BLK = 512
NEG = -1e30

ADAM_LR, ADAM_B1, ADAM_B2, ADAM_EPS, ADAM_WD, ADAM_STEP = 0.001, 0.9, 0.999, 1e-08, 0.01, 10

VMEM_BYTES_V7X = 64 * 1024 * 1024
HIGHEST = lax.Precision.HIGHEST


def _params(vmem_mb, sem=None):
    assert vmem_mb * 1024 * 1024 < VMEM_BYTES_V7X
    kw = dict(vmem_limit_bytes=vmem_mb * 1024 * 1024)
    if sem is not None:
        kw["dimension_semantics"] = sem
    return pltpu.CompilerParams(**kw)


def _dot(a, b, precision=None):
    return lax.dot_general(a, b, (((1,), (0,)), ((), ())), preferred_element_type=F32, precision=precision)


def _dot_nt(a, b):
    return lax.dot_general(a, b, (((1,), (1,)), ((), ())), preferred_element_type=F32)


def _dot_tn(a, b):
    return lax.dot_general(a, b, (((0,), (0,)), ((), ())), preferred_element_type=F32)


def _ln_stats(z):
    mu = jnp.mean(z, axis=-1, keepdims=True)
    zc = z - mu
    var = jnp.mean(zc * zc, axis=-1, keepdims=True)
    rstd = lax.rsqrt(var + LN_EPS)
    return zc * rstd, rstd


def _ln_bwd(dy, xhat, rstd, g):
    dxh = dy * g
    m1 = jnp.mean(dxh, axis=-1, keepdims=True)
    m2 = jnp.mean(dxh * xhat, axis=-1, keepdims=True)
    return rstd * (dxh - m1 - xhat * m2)


_GELU_C = 0.7978845608028654


def _gelu(x):
    return 0.5 * x * (1.0 + jnp.tanh(_GELU_C * (x + 0.044715 * x * x * x)))


def _gelu_grad(x):
    t = jnp.tanh(_GELU_C * (x + 0.044715 * x * x * x))
    return 0.5 * (1.0 + t) + 0.5 * x * (1.0 - t * t) * _GELU_C * (1.0 + 3 * 0.044715 * x * x)


def _vspec():
    return pl.BlockSpec(memory_space=pltpu.VMEM)


def _anyspec():
    return pl.BlockSpec(memory_space=pl.ANY)


def _mesh_pos():
    return lax.axis_index("x"), lax.axis_index("y"), lax.axis_index("c")


def _other_chips(x, y):
    return [(1 - x, y), (x, 1 - y), (1 - x, 1 - y)]


_HBM_SPEC = pl.BlockSpec(memory_space=pltpu.HBM)
_SEM_SPEC = pl.BlockSpec(memory_space=pltpu.SEMAPHORE)
_DATAFLOW_EFFECT = pltpu.SideEffectType.DATAFLOW_SIDE_EFFECTING


def _remote_copies(plan, refs, send_sems, recv_sems):
    return [pltpu.make_async_remote_copy(src_ref=src, dst_ref=dst, send_sem=send_sems.at[k], recv_sem=recv_sems.at[k],
                                         device_id=to, device_id_type=MESH)
            for k, (src, dst, to) in enumerate(plan(refs, *_mesh_pos()))]


def _exchange_start(name, plan, n_copies, arrays, after):
    n = len(arrays)

    def body(*refs):
        send_sems, recv_sems, token = refs[n + 1], refs[n + 2], refs[-1]
        for cp in _remote_copies(plan, refs[:n], send_sems, recv_sems):
            cp.start()
        token[...] = jnp.zeros_like(token)

    out = pl.pallas_call(
        body, name=name,
        out_shape=(pltpu.SemaphoreType.DMA((n_copies,)), pltpu.SemaphoreType.DMA((n_copies,)),
                   *[pltpu.HBM(a.shape, a.dtype) for a in arrays], jax.ShapeDtypeStruct((8, 128), F32)),
        in_specs=[_HBM_SPEC] * n + [_anyspec()],
        out_specs=(_SEM_SPEC, _SEM_SPEC, *[_HBM_SPEC] * n, _vspec()),
        input_output_aliases={i: 2 + i for i in range(n)},
        compiler_params=pltpu.CompilerParams(has_side_effects=_DATAFLOW_EFFECT),
    )(*[pltpu.with_memory_space_constraint(a, pltpu.HBM) for a in arrays], after)
    return out[0], out[1], list(out[2:2 + n]), out[-1]


def _exchange_wait(name, plan, n_copies, started, after):
    send_sems, recv_sems, arrays, _ = started
    n = len(arrays)

    def body(*refs):
        for cp in _remote_copies(plan, refs[:n], refs[n], refs[n + 1]):
            cp.wait_send()
            cp.wait_recv()

    out = pl.pallas_call(
        body, name=name,
        out_shape=tuple(pltpu.HBM(a.shape, a.dtype) for a in arrays),
        in_specs=[_HBM_SPEC] * n + [_SEM_SPEC, _SEM_SPEC, _anyspec()], out_specs=tuple([_HBM_SPEC] * n),
        input_output_aliases={i: i for i in range(n)},
        compiler_params=pltpu.CompilerParams(has_side_effects=_DATAFLOW_EFFECT),
    )(*arrays, send_sems, recv_sems, after)
    return list(out)


def _gather_plan(m):
    def plan(refs, x, y, c):
        me = 4 * x + 2 * y + c
        return [(refs[i], refs[m + i].at[me], (*chip, c)) for i in range(m) for chip in _other_chips(x, y)]
    return plan


def _gather_finish(shards, lands):
    m = len(shards)

    def body(*refs):
        ins, outs = refs[:m], refs[2 * m:3 * m]
        send_sems, recv_sems, local_sems = refs[3 * m:]
        x, y, c = _mesh_pos()
        me = 4 * x + 2 * y + c
        sibling = (x, y, 1 - c)
        mine = [pltpu.make_async_copy(ins[i], outs[i].at[me], local_sems.at[i]) for i in range(m)]
        for cp in mine:
            cp.start()
        copies = []
        for i in range(m):
            blocks = [(ins[i], me)] + [(outs[i].at[4 * cx + 2 * cy + c], 4 * cx + 2 * cy + c) for cx, cy in _other_chips(x, y)]
            for k, (src, blk) in enumerate(blocks):
                copies.append(pltpu.make_async_remote_copy(
                    src_ref=src, dst_ref=outs[i].at[blk], send_sem=send_sems.at[4 * i + k], recv_sem=recv_sems.at[4 * i + k],
                    device_id=sibling, device_id_type=MESH))
        for cp in copies:
            cp.start()
        for cp in copies:
            cp.wait_send()
        for i in range(m):
            for k, (cx, cy) in enumerate([(x, y)] + _other_chips(x, y)):
                blk = 4 * cx + 2 * cy + (1 - c)
                pltpu.make_async_remote_copy(
                    src_ref=outs[i].at[blk], dst_ref=outs[i].at[blk], send_sem=send_sems.at[4 * i + k],
                    recv_sem=recv_sems.at[4 * i + k], device_id=sibling, device_id_type=MESH).wait_recv()
        for cp in mine:
            cp.wait()

    return pl.pallas_call(
        body, name="allgather_finish",
        out_shape=[jax.ShapeDtypeStruct(a.shape, a.dtype) for a in lands],
        in_specs=[_anyspec()] * (2 * m), out_specs=[_anyspec()] * m,
        input_output_aliases={m + i: i for i in range(m)},
        scratch_shapes=[pltpu.SemaphoreType.DMA((4 * m,)), pltpu.SemaphoreType.DMA((4 * m,)), pltpu.SemaphoreType.DMA((m,))],
    )(*shards, *lands)


def _allgather_small(v):
    rows = v.shape[0]

    def body(v_ref, out_ref, send_sems, recv_sems):
        x, y, c = _mesh_pos()
        me = 4 * x + 2 * y + c
        out_ref[me] = v_ref[...]
        rel = [(dx, dy, dc) for dx in (0, 1) for dy in (0, 1) for dc in (0, 1)][1:]
        copies = []
        for k, (dx, dy, dc) in enumerate(rel):
            to = (x ^ dx, y ^ dy, c ^ dc)
            copies.append(pltpu.make_async_remote_copy(
                src_ref=v_ref, dst_ref=out_ref.at[me], send_sem=send_sems.at[k], recv_sem=recv_sems.at[k],
                device_id=to, device_id_type=MESH))
        for cp in copies:
            cp.start()
        for k, (dx, dy, dc) in enumerate(rel):
            src_blk = 4 * (x ^ dx) + 2 * (y ^ dy) + (c ^ dc)
            pltpu.make_async_remote_copy(
                src_ref=v_ref, dst_ref=out_ref.at[src_blk], send_sem=send_sems.at[k], recv_sem=recv_sems.at[k],
                device_id=(x, y, c), device_id_type=MESH).wait_recv()
        for cp in copies:
            cp.wait_send()

    return pl.pallas_call(
        body, name="allgather_small",
        out_shape=jax.ShapeDtypeStruct((N_DEV, rows, 128), v.dtype),
        in_specs=[_vspec()], out_specs=_vspec(),
        scratch_shapes=[pltpu.SemaphoreType.DMA((7,)), pltpu.SemaphoreType.DMA((7,))],
        compiler_params=_params(24),
    )(v)


def _sibling_plan(n):
    def plan(refs, x, y, c):
        return [(refs[a].at[2 * q + (1 - c)], refs[n + a].at[q], (x, y, 1 - c)) for a in range(n) for q in range(4)]
    return plan


def _chip_plan(n):
    def plan(refs, x, y, c):
        return [(refs[a].at[2 * cx + cy], refs[n + a].at[j], (cx, cy, c))
                for a in range(n) for j, (cx, cy) in enumerate(_other_chips(x, y))]
    return plan


def _row_tile(rows, cols, budget_bytes=2 * 1024 * 1024):
    best = 8
    for t in range(8, rows + 1, 8):
        if rows % t == 0 and t * cols * 4 <= budget_bytes:
            best = t
    return best


def _chip_partial(g, recv, core):
    _, rows, cols = g.shape
    tr = _row_tile(rows, cols)

    def body(core_ref, g_ref, r_ref, o32_ref, o16_ref):
        s = g_ref[...] + r_ref[...]
        o32_ref[...] = s
        o16_ref[...] = s.astype(BF16)

    blk = (None, tr, cols)
    return pl.pallas_call(
        body, name="rs_chip_partial",
        grid_spec=pltpu.PrefetchScalarGridSpec(
            num_scalar_prefetch=1, grid=(4, rows // tr),
            in_specs=[pl.BlockSpec(blk, lambda q, i, c: (2 * q + c[0], i, 0)),
                      pl.BlockSpec(blk, lambda q, i, c: (q, i, 0))],
            out_specs=[pl.BlockSpec(blk, lambda q, i, c: (q, i, 0))] * 2),
        out_shape=[jax.ShapeDtypeStruct((4, rows, cols), F32), jax.ShapeDtypeStruct((4, rows, cols), BF16)],
        compiler_params=_params(32),
    )(core, g, recv)


def _adam_math(w, g, m, v):
    m = ADAM_B1 * m + (1.0 - ADAM_B1) * g
    v = ADAM_B2 * v + (1.0 - ADAM_B2) * (g * g)
    m_hat = m / (1.0 - ADAM_B1 ** ADAM_STEP)
    v_hat = v / (1.0 - ADAM_B2 ** ADAM_STEP)
    delta = -ADAM_LR * (m_hat / (jnp.sqrt(v_hat) + ADAM_EPS) + ADAM_WD * w)
    return delta, m, v


def _adamw_shard(part32, recv16, slot, w, m, v):
    rows, cols = w.shape
    tr = _row_tile(rows, cols, 1024 * 1024)

    def body(slot_ref, p_ref, r_ref, w_ref, m_ref, v_ref, g_out, d_out, m_out, v_out):
        g = p_ref[...] + r_ref[0].astype(F32) + r_ref[1].astype(F32) + r_ref[2].astype(F32)
        d, mn, vn = _adam_math(w_ref[...], g, m_ref[...], v_ref[...])
        g_out[...] = g
        d_out[...] = d
        m_out[...] = mn
        v_out[...] = vn

    flat = pl.BlockSpec((tr, cols), lambda i, s: (i, 0))
    return pl.pallas_call(
        body, name="adamw_shard",
        grid_spec=pltpu.PrefetchScalarGridSpec(
            num_scalar_prefetch=1, grid=(rows // tr,),
            in_specs=[pl.BlockSpec((None, tr, cols), lambda i, s: (s[0], i, 0)),
                      pl.BlockSpec((3, tr, cols), lambda i, s: (0, i, 0)), flat, flat, flat],
            out_specs=[flat] * 4),
        out_shape=[jax.ShapeDtypeStruct((rows, cols), F32)] * 4,
        compiler_params=_params(32),
    )(slot, part32, recv16, w, m, v)


def _adamw_small(gathered, w, m, v):
    rows = w.shape[0]

    def body(a_ref, w_ref, m_ref, v_ref, g_out, d_out, m_out, v_out):
        g = a_ref[0]
        for d in range(1, N_DEV):
            g = g + a_ref[d]
        dl, mn, vn = _adam_math(w_ref[...], g, m_ref[...], v_ref[...])
        g_out[...] = g
        d_out[...] = dl
        m_out[...] = mn
        v_out[...] = vn

    return pl.pallas_call(
        body, name="adamw_small",
        in_specs=[_vspec()] * 4, out_specs=[_vspec()] * 4,
        out_shape=[jax.ShapeDtypeStruct((rows, 128), F32)] * 4,
        compiler_params=_params(32),
    )(gathered, w, m, v)


def _load_weights_once(pairs, sems):
    @pl.when((pl.program_id(0) == 0) & (pl.program_id(1) == 0))
    def _():
        cps = [pltpu.make_async_copy(src, dst, sems.at[i]) for i, (src, dst) in enumerate(pairs)]
        for cp in cps:
            cp.start()
        for cp in cps:
            cp.wait()


def _ffn_fwd(x, wup, wd, ln_g, ln_b, tm=512):
    t_tok = x.shape[0]

    def body(x_ref, g_ref, b_ref, wup_hbm, wd_hbm, xn_ref, xnb_ref, z_ref, gu_ref, wup_v, wd_v, acc, sems):
        j = pl.program_id(1)
        _load_weights_once([(wup_hbm, wup_v), (wd_hbm, wd_v)], sems)
        xb = x_ref[...].astype(BF16)
        g = _dot(xb, wup_v[j])
        u = _dot(xb, wup_v[N_FFN_CHUNK + j])
        gu_ref[0] = g.astype(BF16)
        gu_ref[1] = u.astype(BF16)
        a = (g * jax.nn.sigmoid(g) * u).astype(BF16)
        y = _dot(a, wd_v[j])

        @pl.when(j == 0)
        def _():
            acc[...] = y

        @pl.when(j > 0)
        def _():
            acc[...] += y

        @pl.when(j == N_FFN_CHUNK - 1)
        def _():
            z = ALPHA * x_ref[...] + 0.5 * acc[...]
            xhat, _ = _ln_stats(z)
            xn = xhat * g_ref[...] + b_ref[...]
            z_ref[...] = z
            xn_ref[...] = xn
            xnb_ref[...] = xn.astype(BF16)

    tok = pl.BlockSpec((tm, D_MODEL), lambda i, j: (i, 0))
    vec = pl.BlockSpec((1, D_MODEL), lambda i, j: (0, 0))
    return pl.pallas_call(
        body, name="ffn_fwd", grid=(t_tok // tm, N_FFN_CHUNK),
        in_specs=[tok, vec, vec, _anyspec(), _anyspec()],
        out_specs=[tok, tok, tok, pl.BlockSpec((2, None, tm, FFN_BLK), lambda i, j: (0, j, i, 0))],
        out_shape=[jax.ShapeDtypeStruct((t_tok, D_MODEL), F32), jax.ShapeDtypeStruct((t_tok, D_MODEL), BF16),
                   jax.ShapeDtypeStruct((t_tok, D_MODEL), F32),
                   jax.ShapeDtypeStruct((2, N_FFN_CHUNK, t_tok, FFN_BLK), BF16)],
        scratch_shapes=[pltpu.VMEM((N_DEV, D_MODEL, FFN_BLK), BF16), pltpu.VMEM((N_FFN_CHUNK, FFN_BLK, D_MODEL), BF16),
                        pltpu.VMEM((tm, D_MODEL), F32), pltpu.SemaphoreType.DMA((2,))],
        compiler_params=_params(56, ("arbitrary", "arbitrary")),
    )(x, ln_g, ln_b, wup, wd)


def _ffn_bwd(dxn, z, gu, wup, wd, ln_g, after, tm=512):
    t_tok = dxn.shape[0]

    def body(dxn_ref, z_ref, gu_ref, g_ref, wup_hbm, wd_hbm, _after,
             dx_ref, dy_ref, a_ref, dgu_ref, dg_ref, db_ref, wup_v, wd_v, dxacc, dyb, sems):
        i, j = pl.program_id(0), pl.program_id(1)
        _load_weights_once([(wup_hbm, wup_v), (wd_hbm, wd_v)], sems)

        @pl.when(j == 0)
        def _():
            dxn_t = dxn_ref[...]
            xhat, rstd = _ln_stats(z_ref[...])
            pg = jnp.sum(dxn_t * xhat, axis=0, keepdims=True)
            pb = jnp.sum(dxn_t, axis=0, keepdims=True)

            @pl.when(i == 0)
            def _():
                dg_ref[...] = pg
                db_ref[...] = pb

            @pl.when(i > 0)
            def _():
                dg_ref[...] += pg
                db_ref[...] += pb

            dz = _ln_bwd(dxn_t, xhat, rstd, g_ref[...])
            dxacc[...] = ALPHA * dz
            dy = (0.5 * dz).astype(BF16)
            dyb[...] = dy
            dy_ref[...] = dy

        da = _dot_nt(dyb[...], wd_v[j])
        g = gu_ref[0].astype(F32)
        u = gu_ref[1].astype(F32)
        sig = jax.nn.sigmoid(g)
        silu = g * sig
        a_ref[...] = (silu * u).astype(BF16)
        dg = (da * u * (sig * (1.0 + g * (1.0 - sig)))).astype(BF16)
        du = (da * silu).astype(BF16)
        dgu_ref[0] = dg
        dgu_ref[1] = du
        dxacc[...] += _dot_nt(dg, wup_v[j]) + _dot_nt(du, wup_v[N_FFN_CHUNK + j])

        @pl.when(j == N_FFN_CHUNK - 1)
        def _():
            dx_ref[...] = dxacc[...]

    tok = pl.BlockSpec((tm, D_MODEL), lambda i, j: (i, 0))
    vec = pl.BlockSpec((1, D_MODEL), lambda i, j: (0, 0))
    gu_spec = pl.BlockSpec((2, None, tm, FFN_BLK), lambda i, j: (0, j, i, 0))
    return pl.pallas_call(
        body, name="ffn_bwd", grid=(t_tok // tm, N_FFN_CHUNK),
        in_specs=[tok, tok, gu_spec, vec, _anyspec(), _anyspec(), _anyspec()],
        out_specs=[tok, tok, pl.BlockSpec((None, tm, FFN_BLK), lambda i, j: (j, i, 0)), gu_spec, vec, vec],
        out_shape=[jax.ShapeDtypeStruct((t_tok, D_MODEL), F32), jax.ShapeDtypeStruct((t_tok, D_MODEL), BF16),
                   jax.ShapeDtypeStruct((N_FFN_CHUNK, t_tok, FFN_BLK), BF16),
                   jax.ShapeDtypeStruct((2, N_FFN_CHUNK, t_tok, FFN_BLK), BF16),
                   jax.ShapeDtypeStruct((1, D_MODEL), F32), jax.ShapeDtypeStruct((1, D_MODEL), F32)],
        scratch_shapes=[pltpu.VMEM((N_DEV, D_MODEL, FFN_BLK), BF16), pltpu.VMEM((N_FFN_CHUNK, FFN_BLK, D_MODEL), BF16),
                        pltpu.VMEM((tm, D_MODEL), F32), pltpu.VMEM((tm, D_MODEL), BF16), pltpu.SemaphoreType.DMA((2,))],
        compiler_params=_params(56, ("arbitrary", "arbitrary")),
    )(dxn, z, gu, ln_g, wup, wd, after)


def _matmul_tn(a, b, after, tk=1024):
    ga, t_tok, m = a.shape
    gb, _, n = b.shape
    groups = max(ga, gb)

    def body(a_ref, b_ref, _after, o_ref):
        p = _dot_tn(a_ref[...].astype(BF16), b_ref[...].astype(BF16))

        @pl.when(pl.program_id(1) == 0)
        def _():
            o_ref[...] = p

        @pl.when(pl.program_id(1) > 0)
        def _():
            o_ref[...] += p

    return pl.pallas_call(
        body, name=f"matmul_tn_{m}x{n}", grid=(groups, t_tok // tk),
        in_specs=[pl.BlockSpec((None, tk, m), (lambda g, t: (g, t, 0)) if ga > 1 else (lambda g, t: (0, t, 0))),
                  pl.BlockSpec((None, tk, n), (lambda g, t: (g, t, 0)) if gb > 1 else (lambda g, t: (0, t, 0))),
                  _anyspec()],
        out_specs=pl.BlockSpec((None, m, n), lambda g, t: (g, 0, 0)),
        out_shape=jax.ShapeDtypeStruct((groups, m, n), F32),
        compiler_params=_params(56, ("arbitrary", "arbitrary")),
    )(a, b, after)


def _in_proj(x, w_in, tm=512):
    t_tok = x.shape[0]

    def body(x_ref, w_ref, conv_ref, qkv_ref, sgu_ref, f_ref):
        xb = x_ref[...].astype(BF16)
        conv_ref[...] = _dot(xb, w_ref[:, COL_CONV:COL_QKV])
        qkv_ref[...] = _dot(xb, w_ref[:, COL_QKV:COL_SGU]).astype(BF16)
        sgu_ref[...] = _dot(xb, w_ref[:, COL_SGU:COL_F])
        f_ref[...] = _dot(xb, w_ref[:, COL_F:D_IN_PAD])

    def tok(n):
        return pl.BlockSpec((tm, n), lambda i: (i, 0))

    return pl.pallas_call(
        body, name="mix_in_proj", grid=(t_tok // tm,),
        in_specs=[tok(D_MODEL), pl.BlockSpec((D_MODEL, D_IN_PAD), lambda i: (0, 0))],
        out_specs=[tok(768), tok(1536), tok(512), tok(128)],
        out_shape=[jax.ShapeDtypeStruct((t_tok, 768), F32), jax.ShapeDtypeStruct((t_tok, 1536), BF16),
                   jax.ShapeDtypeStruct((t_tok, 512), F32), jax.ShapeDtypeStruct((t_tok, 128), F32)],
        compiler_params=_params(48, ("arbitrary",)),
    )(x, w_in)


def _shift_down(a, k):
    row = lax.broadcasted_iota(jnp.int32, a.shape, 0)
    return jnp.where(row >= k, pltpu.roll(a, k, 0), 0.0)


def _shift_up(a, k):
    rows = a.shape[0]
    row = lax.broadcasted_iota(jnp.int32, a.shape, 0)
    return jnp.where(row < rows - k, pltpu.roll(a, rows - k, 0), 0.0)


def _tril(n):
    return lax.broadcasted_iota(jnp.int32, (n, n), 0) >= lax.broadcasted_iota(jnp.int32, (n, n), 1)


def _sgu_group_of_lane():
    return lax.broadcasted_iota(jnp.int32, (1, D_SGU), 1) // (D_SGU // N_SGU_GROUPS)


def _log_sigmoid(x):
    return jnp.minimum(x, 0.0) - jnp.log1p(jnp.exp(-jnp.abs(x)))


def _mix_mid_fwd(conv, sgu, f, conv_w, b_f, sgu_g, sgu_b, w_s, b_mat, n_seq):
    t_tok = conv.shape[0]
    seq = t_tok // n_seq
    n_chunk = seq // SGU_CHUNK
    per_blk = ATT_BLK // SGU_CHUNK

    def body(conv_ref, sgu_ref, f_ref, cw_ref, bf_ref, lg_ref, lb_ref, ws_ref, bm_ref, ya_ref, yc_ref, cum_ref):
        z = conv_ref[:, 256:512] * conv_ref[:, 512:768]
        y = cw_ref[0:1, :] * _shift_down(z, 2) + cw_ref[1:2, :] * _shift_down(z, 1) + cw_ref[2:3, :] * z
        ya_ref[...] = (conv_ref[:, 0:256] * y).astype(BF16)

        tril = _tril(SGU_CHUNK)
        grp = _sgu_group_of_lane()
        wc = [jnp.where(tril, ws_ref[g], 0.0).astype(BF16) for g in range(N_SGU_GROUPS)]
        tri_f = tril.astype(F32)
        carry = jnp.zeros((1, 128), F32)
        for n in range(n_chunk):
            rows = pl.ds(n * SGU_CHUNK, SGU_CHUNK)
            u = _gelu(sgu_ref[rows, 0:256])
            vhat, _ = _ln_stats(_gelu(sgu_ref[rows, 256:512]))
            vn = (vhat * lg_ref[...] + lb_ref[...]).astype(BF16)
            mixed = bm_ref[...]
            for g in range(N_SGU_GROUPS):
                mixed = mixed + jnp.where(grp == g, _dot(wc[g], vn), 0.0)
            yc_ref[rows, :] = (u * mixed).astype(BF16)

            log_f = _log_sigmoid(f_ref[rows, :] + bf_ref[...])
            cs = _dot(tri_f, log_f, HIGHEST) + carry
            carry = cs[SGU_CHUNK - 1:SGU_CHUNK, :]
            cs_t = cs.T
            lanes = pl.ds((n % per_blk) * SGU_CHUNK, SGU_CHUNK)
            for h in range(N_HEADS):
                cum_ref[h, n // per_blk, :, lanes] = cs_t[h:h + 1, :]

    def seq_blk(n):
        return pl.BlockSpec((seq, n), lambda b: (b, 0))

    def full(shape):
        return pl.BlockSpec(shape, lambda b: (0,) * len(shape))

    return pl.pallas_call(
        body, name="mix_mid_fwd", grid=(n_seq,),
        in_specs=[seq_blk(768), seq_blk(512), seq_blk(128), full((8, 256)), full((1, 128)), full((1, 256)),
                  full((1, 256)), full((4, 128, 128)), full((128, 256))],
        out_specs=[seq_blk(256), seq_blk(256),
                   pl.BlockSpec((N_HEADS, seq // ATT_BLK, 1, ATT_BLK), lambda b: (b, 0, 0, 0))],
        out_shape=[jax.ShapeDtypeStruct((t_tok, 256), BF16), jax.ShapeDtypeStruct((t_tok, 256), BF16),
                   jax.ShapeDtypeStruct((n_seq * N_HEADS, seq // ATT_BLK, 1, ATT_BLK), F32)],
        compiler_params=_params(48, ("arbitrary",)),
    )(conv, sgu, f, conv_w, b_f, sgu_g, sgu_b, w_s, b_mat)


def _head_masks():
    lane = lax.broadcasted_iota(jnp.int32, (1, 128), 1)
    return lane < 64, lane


def _fox_fwd(qkv, cum_t, n_seq):
    t_tok = qkv.shape[0]
    seq = t_tok // n_seq
    nq = seq // ATT_BLK
    blk = ATT_BLK

    def body(q_ref, k_ref, v_ref, c0_ref, c1_ref, o_ref, lse_ref):
        qi = pl.program_id(2)
        first, _ = _head_masks()
        qs = q_ref[...] * ATT_SCALE
        zero = jnp.zeros_like(qs)
        q0 = jnp.where(first, qs, zero)
        q1 = jnp.where(first, zero, qs)
        causal = _tril(blk)
        one = jnp.ones((1, 128), BF16)

        def step(kb, carry, masked):
            m0, m1, acc0, acc1 = carry
            rows = pl.ds(pl.multiple_of(kb * blk, blk), blk)
            k = k_ref[rows, :]
            v = v_ref[rows, :]

            def head(qh, c_ref, m, acc, vh):
                s = _dot_nt(qh, k) - c_ref[kb]
                if masked:
                    s = jnp.where(causal, s, NEG)
                m_new = jnp.maximum(m, jnp.max(s, axis=1, keepdims=True))
                p = jnp.exp(s - m_new)
                return m_new, acc * jnp.exp(m - m_new) + _dot(p.astype(BF16), vh)

            m0, acc0 = head(q0, c0_ref, m0, acc0, jnp.where(first, v, one))
            m1, acc1 = head(q1, c1_ref, m1, acc1, jnp.where(first, one, v))
            return m0, m1, acc0, acc1

        col = jnp.full((blk, 1), NEG, F32)
        zacc = jnp.zeros((blk, 128), F32)
        carry = lax.fori_loop(0, qi, lambda kb, cr: step(kb, cr, False), (col, col, zacc, zacc))
        m0, m1, acc0, acc1 = step(qi, carry, True)
        l0 = pltpu.roll(acc0, 64, 1)
        l1 = pltpu.roll(acc1, 64, 1)
        o_ref[...] = jnp.where(first, acc0 / l0, acc1 / l1).astype(BF16)
        lse_ref[...] = jnp.where(first, m0 + jnp.log(l0), m1 + jnp.log(l1))

    cum_spec0 = pl.BlockSpec((None, nq, 1, blk), lambda b, hp, qi: (b * N_HEADS + 2 * hp, 0, 0, 0))
    cum_spec1 = pl.BlockSpec((None, nq, 1, blk), lambda b, hp, qi: (b * N_HEADS + 2 * hp + 1, 0, 0, 0))
    out_spec = pl.BlockSpec((blk, 128), lambda b, hp, qi: (b * nq + qi, hp))
    return pl.pallas_call(
        body, name="fox_fwd", grid=(n_seq, 4, nq),
        in_specs=[pl.BlockSpec((blk, 128), lambda b, hp, qi: (b * nq + qi, hp)),
                  pl.BlockSpec((seq, 128), lambda b, hp, qi: (b, 4 + hp)),
                  pl.BlockSpec((seq, 128), lambda b, hp, qi: (b, 8 + hp)), cum_spec0, cum_spec1],
        out_specs=[out_spec, out_spec],
        out_shape=[jax.ShapeDtypeStruct((t_tok, D_FOX), BF16), jax.ShapeDtypeStruct((t_tok, D_FOX), F32)],
        compiler_params=_params(32, ("arbitrary", "arbitrary", "arbitrary")),
    )(qkv, qkv, qkv, cum_t, cum_t)


def _fox_bwd(qkv, cum_t, o, lse, d_o, n_seq):
    t_tok = qkv.shape[0]
    seq = t_tok // n_seq
    nk = seq // ATT_BLK
    blk = ATT_BLK

    def body(q_ref, k_ref, v_ref, c0_ref, c1_ref, o_ref, lse_ref, do_ref,
             dq_ref, dk_ref, dv_ref, drow_ref, dcol_ref):
        kb = pl.program_id(2)
        first, lane = _head_masks()
        second = jnp.logical_not(first)
        k = k_ref[...]
        v = v_ref[...]
        zero = jnp.zeros_like(k)
        one = jnp.ones((1, 128), BF16)
        ks = k * ATT_SCALE
        causal = _tril(blk)

        @pl.when(kb == 0)
        def _():
            dq_ref[...] = jnp.zeros_like(dq_ref)
            drow_ref[...] = jnp.zeros_like(drow_ref)

        def step(qi, carry, masked):
            rows = pl.ds(pl.multiple_of(qi * blk, blk), blk)
            qs = q_ref[rows, :] * ATT_SCALE
            d_o = do_ref[rows, :]
            dd = d_o.astype(F32) * o_ref[rows, :].astype(F32)
            lse_t = lse_ref[rows, :]

            def head(mine, c, lse_lane, dk, dv):
                qh = jnp.where(mine, qs, zero)
                doh = jnp.where(mine, d_o, zero)
                delta = jnp.sum(jnp.where(mine, dd, 0.0), axis=1, keepdims=True)
                lse_h = jnp.sum(jnp.where(lane == lse_lane, lse_t, 0.0), axis=1, keepdims=True)
                s = _dot_nt(qh, k) - c
                if masked:
                    s = jnp.where(causal, s, NEG)
                p = jnp.exp(s - lse_h)
                ds = (p * (_dot_nt(doh, v) - delta)).astype(BF16)
                dk = dk + _dot_tn(ds, jnp.where(mine, qs, one))
                dv = dv + _dot_tn(p.astype(BF16), doh)
                return dk, dv, _dot(ds, jnp.where(mine, ks, one))

            dk0, dv0, dk1, dv1 = carry
            dk0, dv0, dq0 = head(first, c0_ref[...], 0, dk0, dv0)
            dk1, dv1, dq1 = head(second, c1_ref[...], 64, dk1, dv1)
            dq_ref[rows, :] += jnp.where(first, dq0, dq1)
            drow_ref[rows, :] += jnp.where(first, dq1, dq0)
            return dk0, dv0, dk1, dv1

        zt = jnp.zeros((blk, 128), F32)
        carry = step(kb, (zt, zt, zt, zt), True)
        dk0, dv0, dk1, dv1 = lax.fori_loop(kb + 1, nk, lambda qi, cr: step(qi, cr, False), carry)
        dk_ref[...] = jnp.where(first, dk0, dk1).astype(BF16)
        dcol_ref[...] = jnp.where(first, dk1, dk0)
        dv_ref[...] = (dv0 + dv1).astype(BF16)

    def seq_spec(col0):
        return pl.BlockSpec((seq, 128), lambda b, hp, kb: (b, col0 + hp))

    def key_spec(col0):
        return pl.BlockSpec((blk, 128), lambda b, hp, kb: (b * nk + kb, col0 + hp))

    def cum_spec(h):
        return pl.BlockSpec((None, None, 1, blk), lambda b, hp, kb: (b * N_HEADS + 2 * hp + h, kb, 0, 0))

    return pl.pallas_call(
        body, name="fox_bwd", grid=(n_seq, 4, nk),
        in_specs=[seq_spec(0), key_spec(4), key_spec(8), cum_spec(0), cum_spec(1), seq_spec(0), seq_spec(0), seq_spec(0)],
        out_specs=[seq_spec(0), key_spec(0), key_spec(0), seq_spec(0), key_spec(0)],
        out_shape=[jax.ShapeDtypeStruct((t_tok, D_FOX), F32), jax.ShapeDtypeStruct((t_tok, D_FOX), BF16),
                   jax.ShapeDtypeStruct((t_tok, D_FOX), BF16), jax.ShapeDtypeStruct((t_tok, D_FOX), F32),
                   jax.ShapeDtypeStruct((t_tok, D_FOX), F32)],
        compiler_params=_params(48, ("arbitrary", "arbitrary", "arbitrary")),
    )(qkv, qkv, qkv, cum_t, cum_t, o, lse, d_o)


def _mix_out_fwd(ya, yb, yc, x, w_out, ln_g, ln_b, tm=512):
    t_tok = x.shape[0]

    def body(ya_ref, yb_ref, yc_ref, x_ref, w_ref, g_ref, b_ref, xn_ref, xnb_ref, z_ref):
        mo = _dot(ya_ref[...], w_ref[0:256, :]) + _dot(yb_ref[...], w_ref[256:768, :]) + _dot(yc_ref[...], w_ref[768:1024, :])
        z = ALPHA * x_ref[...] + mo
        xhat, _ = _ln_stats(z)
        xn = xhat * g_ref[...] + b_ref[...]
        z_ref[...] = z
        xn_ref[...] = xn
        xnb_ref[...] = xn.astype(BF16)

    def tok(n):
        return pl.BlockSpec((tm, n), lambda i: (i, 0))

    vec = pl.BlockSpec((1, D_MODEL), lambda i: (0, 0))
    return pl.pallas_call(
        body, name="mix_out_fwd", grid=(t_tok // tm,),
        in_specs=[tok(256), tok(512), tok(256), tok(D_MODEL),
                  pl.BlockSpec((D_MODEL, D_MODEL), lambda i: (0, 0)), vec, vec],
        out_specs=[tok(D_MODEL)] * 3,
        out_shape=[jax.ShapeDtypeStruct((t_tok, D_MODEL), F32), jax.ShapeDtypeStruct((t_tok, D_MODEL), BF16),
                   jax.ShapeDtypeStruct((t_tok, D_MODEL), F32)],
        compiler_params=_params(40, ("arbitrary",)),
    )(ya, yb, yc, x, w_out, ln_g, ln_b)


def _mix_out_bwd(dxn, z, w_out, ln_g, tm=512):
    t_tok = dxn.shape[0]

    def body(dxn_ref, z_ref, w_ref, g_ref, dz_ref, dzb_ref, dya_ref, dyb_ref, dyc_ref, dg_ref, db_ref):
        i = pl.program_id(0)
        dxn_t = dxn_ref[...]
        xhat, rstd = _ln_stats(z_ref[...])
        pg = jnp.sum(dxn_t * xhat, axis=0, keepdims=True)
        pb = jnp.sum(dxn_t, axis=0, keepdims=True)

        @pl.when(i == 0)
        def _():
            dg_ref[...] = pg
            db_ref[...] = pb

        @pl.when(i > 0)
        def _():
            dg_ref[...] += pg
            db_ref[...] += pb

        dz = _ln_bwd(dxn_t, xhat, rstd, g_ref[...])
        dzb = dz.astype(BF16)
        dz_ref[...] = dz
        dzb_ref[...] = dzb
        dya_ref[...] = _dot_nt(dzb, w_ref[0:256, :])
        dyb_ref[...] = _dot_nt(dzb, w_ref[256:768, :]).astype(BF16)
        dyc_ref[...] = _dot_nt(dzb, w_ref[768:1024, :])

    def tok(n):
        return pl.BlockSpec((tm, n), lambda i: (i, 0))

    vec = pl.BlockSpec((1, D_MODEL), lambda i: (0, 0))
    return pl.pallas_call(
        body, name="mix_out_bwd", grid=(t_tok // tm,),
        in_specs=[tok(D_MODEL), tok(D_MODEL), pl.BlockSpec((D_MODEL, D_MODEL), lambda i: (0, 0)), vec],
        out_specs=[tok(D_MODEL), tok(D_MODEL), tok(256), tok(512), tok(256), vec, vec],
        out_shape=[jax.ShapeDtypeStruct((t_tok, D_MODEL), F32), jax.ShapeDtypeStruct((t_tok, D_MODEL), BF16),
                   jax.ShapeDtypeStruct((t_tok, 256), F32), jax.ShapeDtypeStruct((t_tok, 512), BF16),
                   jax.ShapeDtypeStruct((t_tok, 256), F32),
                   jax.ShapeDtypeStruct((1, D_MODEL), F32), jax.ShapeDtypeStruct((1, D_MODEL), F32)],
        compiler_params=_params(40, ("arbitrary",)),
    )(dxn, z, w_out, ln_g)


def _conv_bwd(conv, dya, conv_w, n_seq):
    t_tok = conv.shape[0]
    seq = t_tok // n_seq

    def body(conv_ref, dya_ref, cw_ref, dconv_ref, dcw_ref):
        @pl.when(pl.program_id(0) == 0)
        def _():
            dcw_ref[...] = jnp.zeros_like(dcw_ref)

        z = conv_ref[:, 256:512] * conv_ref[:, 512:768]
        z1 = _shift_down(z, 1)
        z2 = _shift_down(z, 2)
        y = cw_ref[0:1, :] * z2 + cw_ref[1:2, :] * z1 + cw_ref[2:3, :] * z
        dya_t = dya_ref[...]
        dconv_ref[:, 0:256] = (dya_t * y).astype(BF16)
        dy = dya_t * conv_ref[:, 0:256]
        dcw_ref[0:1, :] += jnp.sum(dy * z2, axis=0, keepdims=True)
        dcw_ref[1:2, :] += jnp.sum(dy * z1, axis=0, keepdims=True)
        dcw_ref[2:3, :] += jnp.sum(dy * z, axis=0, keepdims=True)
        dz = cw_ref[2:3, :] * dy + cw_ref[1:2, :] * _shift_up(dy, 1) + cw_ref[0:1, :] * _shift_up(dy, 2)
        dconv_ref[:, 256:512] = (dz * conv_ref[:, 512:768]).astype(BF16)
        dconv_ref[:, 512:768] = (dz * conv_ref[:, 256:512]).astype(BF16)

    def seq_blk(n):
        return pl.BlockSpec((seq, n), lambda b: (b, 0))

    par = pl.BlockSpec((8, 256), lambda b: (0, 0))
    return pl.pallas_call(
        body, name="conv_bwd", grid=(n_seq,),
        in_specs=[seq_blk(768), seq_blk(256), par], out_specs=[seq_blk(768), par],
        out_shape=[jax.ShapeDtypeStruct((t_tok, 768), BF16), jax.ShapeDtypeStruct((8, 256), F32)],
        compiler_params=_params(56, ("arbitrary",)),
    )(conv, dya, conv_w)


def _sgu_gate_bwd(sgu, f, dyc, drow, dcol, b_f, sgu_g, sgu_b, w_s, b_mat, n_seq):
    t_tok = sgu.shape[0]
    seq = t_tok // n_seq
    n_chunk = seq // SGU_CHUNK

    def body(sgu_ref, f_ref, dyc_ref, drow_ref, dcol_ref, bf_ref, lg_ref, lb_ref, ws_ref, bm_ref,
             dsgu_ref, df_ref, dbf_ref, dlg_ref, dlb_ref, dws_ref, dbs_ref, dbm_acc):
        b = pl.program_id(0)

        @pl.when(b == 0)
        def _():
            for r in (dbf_ref, dlg_ref, dlb_ref, dws_ref, dbm_acc):
                r[...] = jnp.zeros_like(r)

        tril = _tril(SGU_CHUNK)
        grp = _sgu_group_of_lane()
        wc = [jnp.where(tril, ws_ref[g], 0.0).astype(BF16) for g in range(N_SGU_GROUPS)]
        for n in range(n_chunk):
            rows = pl.ds(n * SGU_CHUNK, SGU_CHUNK)
            su = sgu_ref[rows, 0:256]
            sv = sgu_ref[rows, 256:512]
            u = _gelu(su)
            vhat, rstd = _ln_stats(_gelu(sv))
            vn = (vhat * lg_ref[...] + lb_ref[...]).astype(BF16)
            mixed = bm_ref[...]
            for g in range(N_SGU_GROUPS):
                mixed = mixed + jnp.where(grp == g, _dot(wc[g], vn), 0.0)
            dyc_t = dyc_ref[rows, :]
            dsgu_ref[rows, 0:256] = (dyc_t * mixed * _gelu_grad(su)).astype(BF16)
            dmixed = dyc_t * u
            dbm_acc[...] += dmixed
            dvn = jnp.zeros((SGU_CHUNK, D_SGU), F32)
            for g in range(N_SGU_GROUPS):
                dm_g = jnp.where(grp == g, dmixed, 0.0).astype(BF16)
                dws_ref[g] += _dot_nt(dm_g, vn)
                dvn = dvn + _dot_tn(wc[g], dm_g)
            dlg_ref[...] += jnp.sum(dvn * vhat, axis=0, keepdims=True)
            dlb_ref[...] += jnp.sum(dvn, axis=0, keepdims=True)
            dsgu_ref[rows, 256:512] = (_ln_bwd(dvn, vhat, rstd, lg_ref[...]) * _gelu_grad(sv)).astype(BF16)

        later = (lax.broadcasted_iota(jnp.int32, (128, 128), 0) <= lax.broadcasted_iota(jnp.int32, (128, 128), 1)).astype(F32)
        head = lax.broadcasted_iota(jnp.int32, (D_FOX, 128), 1)
        pick = (lax.broadcasted_iota(jnp.int32, (D_FOX, 128), 0) == 128 * (head // 2) + 64 * (1 - head % 2)).astype(F32)
        carry = jnp.zeros((1, 128), F32)
        for n in reversed(range(n_chunk)):
            rows = pl.ds(n * SGU_CHUNK, SGU_CHUNK)
            dcum_n = _dot(drow_ref[rows, :] - dcol_ref[rows, :], pick, HIGHEST)
            dlf = _dot(later, dcum_n, HIGHEST) + carry
            carry = carry + jnp.sum(dcum_n, axis=0, keepdims=True)
            df = dlf * jax.nn.sigmoid(-(f_ref[rows, :] + bf_ref[...]))
            df_ref[rows, :] = df.astype(BF16)
            dbf_ref[...] += jnp.sum(df, axis=0, keepdims=True)

        @pl.when(b == n_seq - 1)
        def _():
            for g in range(N_SGU_GROUPS):
                dws_ref[g] = jnp.where(tril, dws_ref[g], 0.0)
            sel = (lax.broadcasted_iota(jnp.int32, (D_SGU, 128), 0) // (D_SGU // N_SGU_GROUPS)
                   == lax.broadcasted_iota(jnp.int32, (D_SGU, 128), 1)).astype(F32)
            dbs_ref[...] = _dot(dbm_acc[...], sel, HIGHEST)

    def seq_blk(n):
        return pl.BlockSpec((seq, n), lambda b: (b, 0))

    def full(shape):
        return pl.BlockSpec(shape, lambda b: (0,) * len(shape))

    param_shapes = [(1, 128), (1, 256), (1, 256), (4, 128, 128), (128, 128)]
    return pl.pallas_call(
        body, name="sgu_gate_bwd", grid=(n_seq,),
        in_specs=[seq_blk(512), seq_blk(128), seq_blk(256), seq_blk(D_FOX), seq_blk(D_FOX),
                  full((1, 128)), full((1, 256)), full((1, 256)), full((4, 128, 128)), full((128, 256))],
        out_specs=[seq_blk(512), seq_blk(128)] + [full(s) for s in param_shapes],
        out_shape=[jax.ShapeDtypeStruct((t_tok, 512), BF16), jax.ShapeDtypeStruct((t_tok, 128), BF16)]
        + [jax.ShapeDtypeStruct(s, F32) for s in param_shapes],
        scratch_shapes=[pltpu.VMEM((128, 256), F32)],
        compiler_params=_params(48, ("arbitrary",)),
    )(sgu, f, dyc, drow, dcol, b_f, sgu_g, sgu_b, w_s, b_mat)


def _mix_in_bwd(dconv, dq, dk, dv, dsgu, df, dz, w_in, tm=512):
    t_tok = dz.shape[0]

    def body(dconv_ref, dq_ref, dk_ref, dv_ref, dsgu_ref, df_ref, dz_ref, w_ref, dx_ref, dp_ref):
        dqb = dq_ref[...].astype(BF16)
        pieces = [(COL_CONV, dconv_ref[...]), (COL_QKV, dqb), (COL_QKV + 512, dk_ref[...]), (COL_QKV + 1024, dv_ref[...]),
                  (COL_SGU, dsgu_ref[...]), (COL_F, df_ref[...])]
        dx = ALPHA * dz_ref[...]
        for col, val in pieces:
            width = val.shape[1]
            dp_ref[:, col:col + width] = val
            dx = dx + _dot_nt(val, w_ref[:, col:col + width])
        dx_ref[...] = dx

    def tok(n):
        return pl.BlockSpec((tm, n), lambda i: (i, 0))

    return pl.pallas_call(
        body, name="mix_in_bwd", grid=(t_tok // tm,),
        in_specs=[tok(768), tok(512), tok(512), tok(512), tok(512), tok(128), tok(D_MODEL),
                  pl.BlockSpec((D_MODEL, D_IN_PAD), lambda i: (0, 0))],
        out_specs=[tok(D_MODEL), tok(D_IN_PAD)],
        out_shape=[jax.ShapeDtypeStruct((t_tok, D_MODEL), F32), jax.ShapeDtypeStruct((t_tok, D_IN_PAD), BF16)],
        compiler_params=_params(48, ("arbitrary",)),
    )(dconv, dq, dk, dv, dsgu, df, dz, w_in)


def _loss_grad(y, target, tm=512):
    t_tok = y.shape[0]

    def body(y_ref, t_ref, dy_ref, loss_ref):
        err = y_ref[...] - t_ref[...]
        dy_ref[...] = err * (1.0 / D_MODEL)
        part = jnp.sum(jnp.sum(err * err, axis=1, keepdims=True), axis=0, keepdims=True) * (0.5 / D_MODEL)

        @pl.when(pl.program_id(0) == 0)
        def _():
            loss_ref[...] = jnp.zeros_like(loss_ref)

        loss_ref[...] += part

    tok = pl.BlockSpec((tm, D_MODEL), lambda i: (i, 0))
    return pl.pallas_call(
        body, name="loss_grad", grid=(t_tok // tm,),
        in_specs=[tok, tok], out_specs=[tok, pl.BlockSpec((1, 128), lambda i: (0, 0))],
        out_shape=[jax.ShapeDtypeStruct((t_tok, D_MODEL), F32), jax.ShapeDtypeStruct((1, 128), F32)],
        compiler_params=_params(32, ("arbitrary",)),
    )(y, target)


def _pad_rows(a, rows):
    return jnp.pad(a, ((0, rows - a.shape[0]), (0, 0)))


def _w_in_padded(w):
    zeros = jnp.zeros(w.shape[:-1] + (D_IN_PAD - COL_F - N_HEADS,), w.dtype)
    return jnp.concatenate([w[..., :F_ORIG], w[..., F_ORIG + N_HEADS:], w[..., F_ORIG:F_ORIG + N_HEADS], zeros], axis=-1)


def _w_in_unpadded(dw):
    return jnp.concatenate([dw[:, :F_ORIG], dw[:, COL_F:COL_F + N_HEADS], dw[:, F_ORIG:COL_F]], axis=-1)


SMALL_ROWS_PER_LAYER = 6 * 8 + 2 + 2 + 512 + 4 + 1 + 6
SMALL_ROWS = 1152


def _pack_small(p):
    rows = []
    for l in range(DEPTH):
        for name in ("ln1_g", "ln1_b", "ln2_g", "ln2_b", "ln3_g", "ln3_b", "sgu_ln_g", "sgu_ln_b"):
            rows.append(p[name][l].reshape(-1, 128))
        rows.append(p["sgu_w_s"][l].reshape(-1, 128))
        rows.append(p["sgu_b_s"][l].reshape(-1, 128))
        rows.append(jnp.pad(p["fox_b_f"][l], (0, 128 - N_HEADS)).reshape(1, 128))
        rows.append(p["conv_w"][l].reshape(-1, 128))
    return _pad_rows(jnp.concatenate(rows, axis=0), SMALL_ROWS)


def _unpack_small(a):
    out = {}
    r = 0

    def take(n):
        nonlocal r
        piece = a[r:r + n]
        r += n
        return piece

    per_layer = []
    for l in range(DEPTH):
        d = {}
        for name in ("ln1_g", "ln1_b", "ln2_g", "ln2_b", "ln3_g", "ln3_b"):
            d[name] = take(8).reshape(D_MODEL)
        for name in ("sgu_ln_g", "sgu_ln_b"):
            d[name] = take(2).reshape(D_SGU)
        d["sgu_w_s"] = take(512).reshape(N_SGU_GROUPS, SGU_CHUNK, SGU_CHUNK)
        d["sgu_b_s"] = take(4).reshape(N_SGU_GROUPS, SGU_CHUNK)
        d["fox_b_f"] = take(1).reshape(128)[:N_HEADS]
        d["conv_w"] = take(6).reshape(3, D_CONV)
        per_layer.append(d)
    for name in per_layer[0]:
        out[name] = jnp.stack([per_layer[l][name] for l in range(DEPTH)])
    return out


SMALL_NAMES = ("ln1_g", "ln1_b", "fox_b_f", "sgu_ln_g", "sgu_ln_b", "sgu_w_s", "sgu_b_s", "ln2_g", "ln2_b", "ln3_g", "ln3_b")
BIG_NAMES = ("ffn1_w_up", "ffn1_w_down", "mix_w_in", "mix_w_out", "ffn2_w_up", "ffn2_w_down")
WEIGHT_ORDER = ("ln1_g", "ln1_b", "ffn1_w_up", "ffn1_w_down", "mix_w_in", "fox_b_f", "conv_w", "sgu_ln_g", "sgu_ln_b",
                "sgu_w_s", "sgu_b_s", "mix_w_out", "ln2_g", "ln2_b", "ffn2_w_up", "ffn2_w_down", "ln3_g", "ln3_b")


class _Overlap:
    def __init__(self, w, after, core):
        self.core = core
        groups = [[("ffn1_w_up", 0), ("ffn1_w_down", 0)],
                  [("mix_w_in", 0), ("mix_w_out", 0), ("ffn2_w_up", 0), ("ffn2_w_down", 0)]]
        groups += [[(name, l) for name in BIG_NAMES] for l in range(1, DEPTH)]
        self.gathers = []
        for gi, group in enumerate(groups):
            shards = [w[name][l].astype(BF16) for name, l in group]
            lands = [lax.empty((N_DEV,) + s.shape, BF16) for s in shards]
            started = _exchange_start(f"allgather_start_{gi}", _gather_plan(len(group)), 3 * len(group), shards + lands, after)
            after = started[3]
            self.gathers.append((group, started))
        self.all_started = after
        self.gathered = {}
        self.scatters = {}
        self.order = []

    def weights(self, layer, part, after):
        gi = layer + 1 if layer > 0 else (0 if part == "ffn1" else 1)
        group, started = self.gathers[gi]
        if gi not in self.gathered:
            m = len(group)
            arrays = _exchange_wait(f"allgather_wait_{gi}", _gather_plan(m), 3 * m, started,
                                    self.all_started if after is None else after)
            self.gathered[gi] = dict(zip(group, _gather_finish(arrays[:m], arrays[m:])))
        g = self.gathered[gi]

        def ffn(n):
            return g[(f"ffn{n}_w_up", layer)], g[(f"ffn{n}_w_down", layer)].reshape(N_FFN_CHUNK, FFN_BLK, D_MODEL)

        if part == "ffn1":
            return ffn(1)
        w_in = jnp.transpose(g[("mix_w_in", layer)], (1, 0, 2)).reshape(D_MODEL, D_IN)
        return (_w_in_padded(w_in), g[("mix_w_out", layer)].reshape(D_MODEL, D_MODEL), *ffn(2))

    def push(self, key, items):
        n = len(items)
        grads = [g for _, _, g in items]
        lands = [lax.empty((4,) + g.shape[1:], F32) for g in grads]
        started = _exchange_start(f"rs_sibling_start_{key[0]}{key[1]}", _sibling_plan(n), 4 * n, grads + lands, self.core)
        self.scatters[key] = dict(items=items, sibling=started)
        self.order.append(key)
        return started[3]

    def advance(self, key, after):
        st = self.scatters[key]
        n = len(st["items"])
        arrays = _exchange_wait(f"rs_sibling_wait_{key[0]}{key[1]}", _sibling_plan(n), 4 * n, st["sibling"], after)
        partials = [_chip_partial(g, r, self.core) for g, r in zip(arrays[:n], arrays[n:])]
        p16 = [p for _, p in partials]
        lands = [lax.empty((3,) + p.shape[1:], BF16) for p in p16]
        started = _exchange_start(f"rs_chip_start_{key[0]}{key[1]}", _chip_plan(n), 3 * n, p16 + lands, self.core)
        st.update(p32=[p for p, _ in partials], chip=started)
        return started[3]

    def finish(self, slot, w, m, v):
        res = {}
        after = self.scatters[self.order[-1]]["chip"][3]
        for key in self.order:
            st = self.scatters[key]
            n = len(st["items"])
            arrays = _exchange_wait(f"rs_chip_wait_{key[0]}{key[1]}", _chip_plan(n), 3 * n, st["chip"], after)
            for (name, l, _), p32, r16 in zip(st["items"], st["p32"], arrays[n:]):
                rows = p32.shape[1]
                res[(name, l)] = _adamw_shard(p32, r16, slot, *[t[name][l].reshape(rows, -1) for t in (w, m, v)])
                after = res[(name, l)][0]
        return res


def _local_step(x, target, comm, small, n_seq):
    def vec(a):
        return a.reshape(1, -1)

    saved = []
    h = x
    for l in range(DEPTH):
        s = {}
        s["up1"], s["down1"] = comm.weights(l, "ffn1", None if l == 0 else h)
        h1, h1b, s["z1"], s["gu1"] = _ffn_fwd(h, s["up1"], s["down1"], vec(small["ln1_g"][l]), vec(small["ln1_b"][l]))
        s["w_in"], s["w_out"], s["up2"], s["down2"] = comm.weights(l, "rest", s["z1"])
        s["x1b"] = h1b
        conv, qkv, sgu, f = _in_proj(h1, s["w_in"])
        cw = _pad_rows(small["conv_w"][l], 8)
        bf = jnp.pad(small["fox_b_f"][l], (0, 128 - N_HEADS)).reshape(1, 128)
        b_mat = jnp.repeat(small["sgu_b_s"][l].T, D_SGU // N_SGU_GROUPS, axis=1)
        mid_params = (cw, bf, vec(small["sgu_ln_g"][l]), vec(small["sgu_ln_b"][l]), small["sgu_w_s"][l], b_mat)
        ya, yc, cum_t = _mix_mid_fwd(conv, sgu, f, *mid_params, n_seq)
        yb, lse = _fox_fwd(qkv, cum_t, n_seq)
        h2, h2b, s["z2"] = _mix_out_fwd(ya, yb, yc, h1, s["w_out"], vec(small["ln2_g"][l]), vec(small["ln2_b"][l]))
        s.update(conv=conv, qkv=qkv, sgu=sgu, f=f, mid_params=mid_params, ya=ya, yb=yb, yc=yc, cum_t=cum_t, lse=lse, x2b=h2b)
        h3, h3b, s["z3"], s["gu2"] = _ffn_fwd(h2, s["up2"], s["down2"], vec(small["ln3_g"][l]), vec(small["ln3_b"][l]))
        s["x3b"] = h3b
        saved.append(s)
        h = h3

    dh, loss = _loss_grad(h, target)

    small_grads = [None] * DEPTH
    token = loss
    pending = None
    for l in reversed(range(DEPTH)):
        s = saved[l]
        x0b = saved[l - 1]["x3b"] if l > 0 else x.astype(BF16)
        sg = {}
        dh, dy, a, dgu, sg["ln3_g"], sg["ln3_b"] = _ffn_bwd(dh, s["z3"], s["gu2"], s["up2"], s["down2"], vec(small["ln3_g"][l]), token)
        if pending is not None:
            token = comm.advance(pending, dh)
        g_up2 = _matmul_tn(s["x2b"][None], dgu.reshape(N_DEV, -1, FFN_BLK), token)
        g_down2 = _matmul_tn(a, dy[None], token).reshape(N_DEV, FFN_BLK // 2, D_MODEL)
        dz, dzb, dya, dyb, dyc, sg["ln2_g"], sg["ln2_b"] = _mix_out_bwd(dh, s["z2"], s["w_out"], vec(small["ln2_g"][l]))
        dwo = [_matmul_tn(y[None], dzb[None], token)[0] for y in (s["ya"], s["yb"], s["yc"])]
        g_out = jnp.concatenate(dwo, axis=0).reshape(N_DEV, D_MODEL // N_DEV, D_MODEL)
        dq, dk, dv, drow, dcol = _fox_bwd(s["qkv"], s["cum_t"], s["yb"], s["lse"], dyb, n_seq)
        dconv, dcw = _conv_bwd(s["conv"], dya, s["mid_params"][0], n_seq)
        dsgu, df, dbf, dlg, dlb, dws, dbs = _sgu_gate_bwd(s["sgu"], s["f"], dyc, drow, dcol, *s["mid_params"][1:], n_seq)
        sg.update(conv_w=dcw[:3], fox_b_f=dbf[0, :N_HEADS], sgu_ln_g=dlg[0], sgu_ln_b=dlb[0], sgu_w_s=dws,
                  sgu_b_s=dbs[:, :N_SGU_GROUPS].T)
        dh, dp = _mix_in_bwd(dconv, dq, dk, dv, dsgu, df, dz, s["w_in"])
        dwin = _w_in_unpadded(_matmul_tn(s["x1b"][None], dp[None], token, tk=512)[0])
        g_in = jnp.transpose(dwin.reshape(D_MODEL, N_DEV, D_IN_SHARD), (1, 0, 2))
        first = [("ffn2_w_up", l, g_up2), ("ffn2_w_down", l, g_down2), ("mix_w_out", l, g_out), ("mix_w_in", l, g_in)]
        if l == 0:
            token = comm.push((l, "a"), first)
            pending, first = (l, "a"), []
        dh, dy, a, dgu, sg["ln1_g"], sg["ln1_b"] = _ffn_bwd(dh, s["z1"], s["gu1"], s["up1"], s["down1"], vec(small["ln1_g"][l]), token)
        if l == 0:
            token = comm.advance(pending, dh)
        g_up1 = _matmul_tn(x0b[None], dgu.reshape(N_DEV, -1, FFN_BLK), token)
        g_down1 = _matmul_tn(a, dy[None], token).reshape(N_DEV, FFN_BLK // 2, D_MODEL)
        key = (l, "b")
        token = comm.push(key, first + [("ffn1_w_up", l, g_up1), ("ffn1_w_down", l, g_down1)])
        pending = key
        if l == 0:
            token = comm.advance(key, token)
        for name in ("ln1_g", "ln1_b", "ln2_g", "ln2_b", "ln3_g", "ln3_b"):
            sg[name] = sg[name][0]
        small_grads[l] = sg
    return loss, dh, small_grads


def kernel(x, ln1_g, ln1_b, ffn1_w_up, ffn1_w_down, mix_w_in, fox_b_f, conv_w, sgu_ln_g, sgu_ln_b, sgu_w_s, sgu_b_s, mix_w_out, ln2_g, ln2_b, ffn2_w_up, ffn2_w_down, ln3_g, ln3_b, loss_target, m_ln1_g, m_ln1_b, m_ffn1_w_up, m_ffn1_w_down, m_mix_w_in, m_fox_b_f, m_conv_w, m_sgu_ln_g, m_sgu_ln_b, m_sgu_w_s, m_sgu_b_s, m_mix_w_out, m_ln2_g, m_ln2_b, m_ffn2_w_up, m_ffn2_w_down, m_ln3_g, m_ln3_b, v_ln1_g, v_ln1_b, v_ffn1_w_up, v_ffn1_w_down, v_mix_w_in, v_fox_b_f, v_conv_w, v_sgu_ln_g, v_sgu_ln_b, v_sgu_w_s, v_sgu_b_s, v_mix_w_out, v_ln2_g, v_ln2_b, v_ffn2_w_up, v_ffn2_w_down, v_ln3_g, v_ln3_b):
    w = dict(ln1_g=ln1_g, ln1_b=ln1_b, ffn1_w_up=ffn1_w_up, ffn1_w_down=ffn1_w_down, mix_w_in=mix_w_in, fox_b_f=fox_b_f,
             conv_w=conv_w, sgu_ln_g=sgu_ln_g, sgu_ln_b=sgu_ln_b, sgu_w_s=sgu_w_s, sgu_b_s=sgu_b_s, mix_w_out=mix_w_out,
             ln2_g=ln2_g, ln2_b=ln2_b, ffn2_w_up=ffn2_w_up, ffn2_w_down=ffn2_w_down, ln3_g=ln3_g, ln3_b=ln3_b)
    m = dict(ln1_g=m_ln1_g, ln1_b=m_ln1_b, ffn1_w_up=m_ffn1_w_up, ffn1_w_down=m_ffn1_w_down, mix_w_in=m_mix_w_in,
             fox_b_f=m_fox_b_f, conv_w=m_conv_w, sgu_ln_g=m_sgu_ln_g, sgu_ln_b=m_sgu_ln_b, sgu_w_s=m_sgu_w_s,
             sgu_b_s=m_sgu_b_s, mix_w_out=m_mix_w_out, ln2_g=m_ln2_g, ln2_b=m_ln2_b, ffn2_w_up=m_ffn2_w_up,
             ffn2_w_down=m_ffn2_w_down, ln3_g=m_ln3_g, ln3_b=m_ln3_b)
    v = dict(ln1_g=v_ln1_g, ln1_b=v_ln1_b, ffn1_w_up=v_ffn1_w_up, ffn1_w_down=v_ffn1_w_down, mix_w_in=v_mix_w_in,
             fox_b_f=v_fox_b_f, conv_w=v_conv_w, sgu_ln_g=v_sgu_ln_g, sgu_ln_b=v_sgu_ln_b, sgu_w_s=v_sgu_w_s,
             sgu_b_s=v_sgu_b_s, mix_w_out=v_mix_w_out, ln2_g=v_ln2_g, ln2_b=v_ln2_b, ffn2_w_up=v_ffn2_w_up,
             ffn2_w_down=v_ffn2_w_down, ln3_g=v_ln3_g, ln3_b=v_ln3_b)

    mx, my, mc = lax.axis_index("x"), lax.axis_index("y"), lax.axis_index("c")
    me = 4 * mx + 2 * my + mc
    n_seq, seq, _ = x.shape
    t_tok = n_seq * seq

    cw_rows = _pad_rows(conv_w.reshape(DEPTH * 3, D_CONV // N_DEV), 8)
    cw_all = _allgather_small(jnp.pad(cw_rows, ((0, 0), (0, 128 - D_CONV // N_DEV))))
    conv_w_full = jnp.transpose(cw_all[:, :DEPTH * 3, :D_CONV // N_DEV], (1, 0, 2)).reshape(DEPTH, 3, D_CONV)
    small = {name: w[name] for name in SMALL_NAMES}
    small["conv_w"] = conv_w_full

    comm = _Overlap(w, cw_all, mc.reshape(1).astype(jnp.int32))
    loss_dev, grad_x, small_grads = _local_step(
        x.reshape(t_tok, D_MODEL), loss_target.reshape(t_tok, D_MODEL), comm, small, n_seq)
    loss = lax.psum(loss_dev[0, 0], ("x", "y", "c"))
    res = comm.finish((2 * mx + my).reshape(1).astype(jnp.int32), w, m, v)
    out = {}
    for name in BIG_NAMES:
        out[name] = [jnp.stack([res[(name, l)][k] for l in range(DEPTH)]).reshape(w[name].shape) for k in range(4)]

    sg = {name: jnp.stack([small_grads[l][name] for l in range(DEPTH)]) for name in SMALL_NAMES + ("conv_w",)}
    all_small = _allgather_small(_pack_small(sg))

    def widen(a):
        return lax.dynamic_update_slice(jnp.zeros((DEPTH, 3, D_CONV), F32), a, (0, 0, me * (D_CONV // N_DEV)))

    packed = [_pack_small({**{name: t[name] for name in SMALL_NAMES}, "conv_w": widen(t["conv_w"])}) for t in (w, m, v)]
    small_out = [_unpack_small(a) for a in _adamw_small(all_small, *packed)]
    for name in SMALL_NAMES:
        out[name] = [small_out[k][name] for k in range(4)]
    out["conv_w"] = [lax.dynamic_slice(small_out[k]["conv_w"], (0, 0, me * (D_CONV // N_DEV)), (DEPTH, 3, D_CONV // N_DEV))
                     for k in range(4)]

    return (loss, grad_x.reshape(x.shape), *[out[name][0] for name in WEIGHT_ORDER], *[out[name][1] for name in WEIGHT_ORDER],
            *[out[name][2] for name in WEIGHT_ORDER], *[out[name][3] for name in WEIGHT_ORDER])
```

```python
import functools

import jax
import jax.numpy as jnp
from jax import lax
from jax.experimental import pallas as pl
from jax.experimental.pallas import tpu as pltpu

F32 = jnp.float32
BF16 = jnp.bfloat16
MESH = pl.DeviceIdType.MESH

N_DEV = 8
DEPTH = 2
D_MODEL = 1024
D_FF = 2816
FFN_BLK = 2 * D_FF // N_DEV
N_FFN_CHUNK = D_FF // FFN_BLK
D_CONV = 256
D_FOX = 512
N_HEADS = 8
D_SGU = 256
N_SGU_GROUPS = 4
SGU_CHUNK = 128
D_IN = 3 * D_CONV + 3 * D_FOX + N_HEADS + 2 * D_SGU
D_IN_SHARD = D_IN // N_DEV
COL_CONV, COL_QKV, COL_SGU, COL_F = 0, 768, 2304, 2816
D_IN_PAD = 2944
F_ORIG = 3 * D_CONV + 3 * D_FOX
ALPHA = (2 * DEPTH) ** 0.25
LN_EPS = 1e-5
ATT_SCALE = 0.125
ATT_BLK = 512
NEG = -1e30

ADAM_LR, ADAM_B1, ADAM_B2, ADAM_EPS, ADAM_WD, ADAM_STEP = 0.001, 0.9, 0.999, 1e-08, 0.01, 10

VMEM_BYTES_V7X = 64 * 1024 * 1024
HIGHEST = lax.Precision.HIGHEST


def _params(vmem_mb, sem=None):
    assert vmem_mb * 1024 * 1024 < VMEM_BYTES_V7X
    kw = dict(vmem_limit_bytes=vmem_mb * 1024 * 1024)
    if sem is not None:
        kw["dimension_semantics"] = sem
    return pltpu.CompilerParams(**kw)


def _dot(a, b, precision=None):
    return lax.dot_general(a, b, (((1,), (0,)), ((), ())), preferred_element_type=F32, precision=precision)


def _dot_nt(a, b):
    return lax.dot_general(a, b, (((1,), (1,)), ((), ())), preferred_element_type=F32)


def _dot_tn(a, b):
    return lax.dot_general(a, b, (((0,), (0,)), ((), ())), preferred_element_type=F32)


def _ln_stats(z):
    mu = jnp.mean(z, axis=-1, keepdims=True)
    zc = z - mu
    var = jnp.mean(zc * zc, axis=-1, keepdims=True)
    rstd = lax.rsqrt(var + LN_EPS)
    return zc * rstd, rstd


def _ln_bwd(dy, xhat, rstd, g):
    dxh = dy * g
    m1 = jnp.mean(dxh, axis=-1, keepdims=True)
    m2 = jnp.mean(dxh * xhat, axis=-1, keepdims=True)
    return rstd * (dxh - m1 - xhat * m2)


_GELU_C = 0.7978845608028654


def _gelu(x):
    return 0.5 * x * (1.0 + jnp.tanh(_GELU_C * (x + 0.044715 * x * x * x)))


def _gelu_grad(x):
    t = jnp.tanh(_GELU_C * (x + 0.044715 * x * x * x))
    return 0.5 * (1.0 + t) + 0.5 * x * (1.0 - t * t) * _GELU_C * (1.0 + 3 * 0.044715 * x * x)


def _vspec():
    return pl.BlockSpec(memory_space=pltpu.VMEM)


def _anyspec():
    return pl.BlockSpec(memory_space=pl.ANY)


def _mesh_pos():
    return lax.axis_index("x"), lax.axis_index("y"), lax.axis_index("c")


def _other_chips(x, y):
    return [(1 - x, y), (x, 1 - y), (1 - x, 1 - y)]


_HBM_SPEC = pl.BlockSpec(memory_space=pltpu.HBM)
_SEM_SPEC = pl.BlockSpec(memory_space=pltpu.SEMAPHORE)
_DATAFLOW_EFFECT = pltpu.SideEffectType.DATAFLOW_SIDE_EFFECTING


def _remote_copies(plan, refs, send_sems, recv_sems):
    return [pltpu.make_async_remote_copy(src_ref=src, dst_ref=dst, send_sem=send_sems.at[k], recv_sem=recv_sems.at[k],
                                         device_id=to, device_id_type=MESH)
            for k, (src, dst, to) in enumerate(plan(refs, *_mesh_pos()))]


def _exchange_start(name, plan, n_copies, arrays, after):
    n = len(arrays)

    def body(*refs):
        send_sems, recv_sems, token = refs[n + 1], refs[n + 2], refs[-1]
        for cp in _remote_copies(plan, refs[:n], send_sems, recv_sems):
            cp.start()
        token[...] = jnp.zeros_like(token)

    out = pl.pallas_call(
        body, name=name,
        out_shape=(pltpu.SemaphoreType.DMA((n_copies,)), pltpu.SemaphoreType.DMA((n_copies,)),
                   *[pltpu.HBM(a.shape, a.dtype) for a in arrays], jax.ShapeDtypeStruct((8, 128), F32)),
        in_specs=[_HBM_SPEC] * n + [_anyspec()],
        out_specs=(_SEM_SPEC, _SEM_SPEC, *[_HBM_SPEC] * n, _vspec()),
        input_output_aliases={i: 2 + i for i in range(n)},
        compiler_params=pltpu.CompilerParams(has_side_effects=_DATAFLOW_EFFECT),
    )(*[pltpu.with_memory_space_constraint(a, pltpu.HBM) for a in arrays], after)
    return out[0], out[1], list(out[2:2 + n]), out[-1]


def _exchange_wait(name, plan, n_copies, started, after):
    send_sems, recv_sems, arrays, _ = started
    n = len(arrays)

    def body(*refs):
        for cp in _remote_copies(plan, refs[:n], refs[n], refs[n + 1]):
            cp.wait_send()
            cp.wait_recv()

    out = pl.pallas_call(
        body, name=name,
        out_shape=tuple(pltpu.HBM(a.shape, a.dtype) for a in arrays),
        in_specs=[_HBM_SPEC] * n + [_SEM_SPEC, _SEM_SPEC, _anyspec()], out_specs=tuple([_HBM_SPEC] * n),
        input_output_aliases={i: i for i in range(n)},
        compiler_params=pltpu.CompilerParams(has_side_effects=_DATAFLOW_EFFECT),
    )(*arrays, send_sems, recv_sems, after)
    return list(out)


def _gather_plan(m):
    def plan(refs, x, y, c):
        me = 4 * x + 2 * y + c
        return [(refs[i], refs[m + i].at[me], (*chip, c)) for i in range(m) for chip in _other_chips(x, y)]
    return plan


def _gather_finish(shards, lands):
    m = len(shards)

    def body(*refs):
        ins, outs = refs[:m], refs[2 * m:3 * m]
        send_sems, recv_sems = refs[3 * m:]
        x, y, c = _mesh_pos()
        me = 4 * x + 2 * y + c
        sibling = (x, y, 1 - c)
        copies = []
        for i in range(m):
            blocks = [(ins[i], me)] + [(outs[i].at[4 * cx + 2 * cy + c], 4 * cx + 2 * cy + c) for cx, cy in _other_chips(x, y)]
            for k, (src, blk) in enumerate(blocks):
                copies.append(pltpu.make_async_remote_copy(
                    src_ref=src, dst_ref=outs[i].at[blk], send_sem=send_sems.at[4 * i + k], recv_sem=recv_sems.at[4 * i + k],
                    device_id=sibling, device_id_type=MESH))
        for cp in copies:
            cp.start()
        for cp in copies:
            cp.wait_send()
        for i in range(m):
            for k, (cx, cy) in enumerate([(x, y)] + _other_chips(x, y)):
                blk = 4 * cx + 2 * cy + (1 - c)
                pltpu.make_async_remote_copy(
                    src_ref=outs[i].at[blk], dst_ref=outs[i].at[blk], send_sem=send_sems.at[4 * i + k],
                    recv_sem=recv_sems.at[4 * i + k], device_id=sibling, device_id_type=MESH).wait_recv()

    return pl.pallas_call(
        body, name="allgather_finish",
        out_shape=[jax.ShapeDtypeStruct(a.shape, a.dtype) for a in lands],
        in_specs=[_anyspec()] * (2 * m), out_specs=[_anyspec()] * m,
        input_output_aliases={m + i: i for i in range(m)},
        scratch_shapes=[pltpu.SemaphoreType.DMA((4 * m,)), pltpu.SemaphoreType.DMA((4 * m,))],
    )(*shards, *lands)


def _allgather_small(v):
    rows = v.shape[0]

    def body(v_ref, out_ref, send_sems, recv_sems):
        x, y, c = _mesh_pos()
        me = 4 * x + 2 * y + c
        out_ref[me] = v_ref[...]
        rel = [(dx, dy, dc) for dx in (0, 1) for dy in (0, 1) for dc in (0, 1)][1:]
        copies = []
        for k, (dx, dy, dc) in enumerate(rel):
            to = (x ^ dx, y ^ dy, c ^ dc)
            copies.append(pltpu.make_async_remote_copy(
                src_ref=v_ref, dst_ref=out_ref.at[me], send_sem=send_sems.at[k], recv_sem=recv_sems.at[k],
                device_id=to, device_id_type=MESH))
        for cp in copies:
            cp.start()
        for k, (dx, dy, dc) in enumerate(rel):
            src_blk = 4 * (x ^ dx) + 2 * (y ^ dy) + (c ^ dc)
            pltpu.make_async_remote_copy(
                src_ref=v_ref, dst_ref=out_ref.at[src_blk], send_sem=send_sems.at[k], recv_sem=recv_sems.at[k],
                device_id=(x, y, c), device_id_type=MESH).wait_recv()
        for cp in copies:
            cp.wait_send()

    return pl.pallas_call(
        body, name="allgather_small",
        out_shape=jax.ShapeDtypeStruct((N_DEV, rows, 128), v.dtype),
        in_specs=[_vspec()], out_specs=_vspec(),
        scratch_shapes=[pltpu.SemaphoreType.DMA((7,)), pltpu.SemaphoreType.DMA((7,))],
        compiler_params=_params(24),
    )(v)


def _sibling_plan(n):
    def plan(refs, x, y, c):
        return [(refs[a].at[2 * q + (1 - c)], refs[n + a].at[q], (x, y, 1 - c)) for a in range(n) for q in range(4)]
    return plan


def _chip_plan(n):
    def plan(refs, x, y, c):
        return [(refs[a].at[2 * cx + cy], refs[n + a].at[j], (cx, cy, c))
                for a in range(n) for j, (cx, cy) in enumerate(_other_chips(x, y))]
    return plan


def _row_tile(rows, cols, budget_bytes=2 * 1024 * 1024):
    best = 8
    for t in range(8, rows + 1, 8):
        if rows % t == 0 and t * cols * 4 <= budget_bytes:
            best = t
    return best


def _chip_partial(g, recv, core):
    _, rows, cols = g.shape
    tr = _row_tile(rows, cols)

    def body(core_ref, g_ref, r_ref, o32_ref, o16_ref):
        s = g_ref[...] + r_ref[...]
        o32_ref[...] = s
        o16_ref[...] = s.astype(BF16)

    blk = (None, tr, cols)
    return pl.pallas_call(
        body, name="rs_chip_partial",
        grid_spec=pltpu.PrefetchScalarGridSpec(
            num_scalar_prefetch=1, grid=(4, rows // tr),
            in_specs=[pl.BlockSpec(blk, lambda q, i, c: (2 * q + c[0], i, 0)),
                      pl.BlockSpec(blk, lambda q, i, c: (q, i, 0))],
            out_specs=[pl.BlockSpec(blk, lambda q, i, c: (q, i, 0))] * 2),
        out_shape=[jax.ShapeDtypeStruct((4, rows, cols), F32), jax.ShapeDtypeStruct((4, rows, cols), BF16)],
        compiler_params=_params(32),
    )(core, g, recv)


def _adam_math(w, g, m, v):
    m = ADAM_B1 * m + (1.0 - ADAM_B1) * g
    v = ADAM_B2 * v + (1.0 - ADAM_B2) * (g * g)
    m_hat = m / (1.0 - ADAM_B1 ** ADAM_STEP)
    v_hat = v / (1.0 - ADAM_B2 ** ADAM_STEP)
    delta = -ADAM_LR * (m_hat / (jnp.sqrt(v_hat) + ADAM_EPS) + ADAM_WD * w)
    return delta, m, v


def _adamw_shard(part32, recv16, slot, w, m, v, layer, earlier):
    depth, rows, cols = w.shape
    tr = _row_tile(rows, cols, 1024 * 1024)
    n_prev = 0 if earlier is None else 4

    def body(slot_ref, p_ref, r_ref, w_ref, m_ref, v_ref, *rest):
        g_out, d_out, m_out, v_out = rest[n_prev:]
        g = p_ref[...] + r_ref[0].astype(F32) + r_ref[1].astype(F32) + r_ref[2].astype(F32)
        d, mn, vn = _adam_math(w_ref[...], g, m_ref[...], v_ref[...])
        g_out[...] = g
        d_out[...] = d
        m_out[...] = mn
        v_out[...] = vn

    mine = pl.BlockSpec((None, tr, cols), lambda i, s: (layer, i, 0))
    return pl.pallas_call(
        body, name="adamw_shard",
        grid_spec=pltpu.PrefetchScalarGridSpec(
            num_scalar_prefetch=1, grid=(rows // tr,),
            in_specs=[pl.BlockSpec((None, tr, cols), lambda i, s: (s[0], i, 0)),
                      pl.BlockSpec((3, tr, cols), lambda i, s: (0, i, 0)), mine, mine, mine] + [_anyspec()] * n_prev,
            out_specs=[mine] * 4),
        out_shape=[jax.ShapeDtypeStruct((depth, rows, cols), F32)] * 4,
        input_output_aliases={6 + k: k for k in range(n_prev)},
        compiler_params=_params(32),
    )(slot, part32, recv16, w, m, v, *([] if earlier is None else earlier))


def _adamw_small(gathered, w, m, v):
    rows = w.shape[0]

    def body(a_ref, w_ref, m_ref, v_ref, g_out, d_out, m_out, v_out):
        g = a_ref[0]
        for d in range(1, N_DEV):
            g = g + a_ref[d]
        dl, mn, vn = _adam_math(w_ref[...], g, m_ref[...], v_ref[...])
        g_out[...] = g
        d_out[...] = dl
        m_out[...] = mn
        v_out[...] = vn

    return pl.pallas_call(
        body, name="adamw_small",
        in_specs=[_vspec()] * 4, out_specs=[_vspec()] * 4,
        out_shape=[jax.ShapeDtypeStruct((rows, 128), F32)] * 4,
        compiler_params=_params(32),
    )(gathered, w, m, v)


def _load_weights_once(pairs, sems):
    @pl.when((pl.program_id(0) == 0) & (pl.program_id(1) == 0))
    def _():
        cps = [pltpu.make_async_copy(src, dst, sems.at[i]) for i, (src, dst) in enumerate(pairs)]
        for cp in cps:
            cp.start()
        for cp in cps:
            cp.wait()


def _ffn_fwd(x, wup, wd, ln_g, ln_b, tm=512):
    t_tok = x.shape[0]

    def body(x_ref, g_ref, b_ref, wup_hbm, wd_hbm, xn_ref, xnb_ref, z_ref, gu_ref, wup_v, wd_v, acc, sems):
        j = pl.program_id(1)
        _load_weights_once([(wup_hbm, wup_v), (wd_hbm, wd_v)], sems)
        xb = x_ref[...].astype(BF16)
        g = _dot(xb, wup_v[j])
        u = _dot(xb, wup_v[N_FFN_CHUNK + j])
        gu_ref[0] = g.astype(BF16)
        gu_ref[1] = u.astype(BF16)
        a = (g * jax.nn.sigmoid(g) * u).astype(BF16)
        y = _dot(a, wd_v[j])

        @pl.when(j == 0)
        def _():
            acc[...] = y

        @pl.when(j > 0)
        def _():
            acc[...] += y

        @pl.when(j == N_FFN_CHUNK - 1)
        def _():
            z = ALPHA * x_ref[...] + 0.5 * acc[...]
            xhat, _ = _ln_stats(z)
            xn = xhat * g_ref[...] + b_ref[...]
            z_ref[...] = z
            xn_ref[...] = xn
            xnb_ref[...] = xn.astype(BF16)

    tok = pl.BlockSpec((tm, D_MODEL), lambda i, j: (i, 0))
    vec = pl.BlockSpec((1, D_MODEL), lambda i, j: (0, 0))
    return pl.pallas_call(
        body, name="ffn_fwd", grid=(t_tok // tm, N_FFN_CHUNK),
        in_specs=[tok, vec, vec, _anyspec(), _anyspec()],
        out_specs=[tok, tok, tok, pl.BlockSpec((2, None, tm, FFN_BLK), lambda i, j: (0, j, i, 0))],
        out_shape=[jax.ShapeDtypeStruct((t_tok, D_MODEL), F32), jax.ShapeDtypeStruct((t_tok, D_MODEL), BF16),
                   jax.ShapeDtypeStruct((t_tok, D_MODEL), F32),
                   jax.ShapeDtypeStruct((2, N_FFN_CHUNK, t_tok, FFN_BLK), BF16)],
        scratch_shapes=[pltpu.VMEM((N_DEV, D_MODEL, FFN_BLK), BF16), pltpu.VMEM((N_FFN_CHUNK, FFN_BLK, D_MODEL), BF16),
                        pltpu.VMEM((tm, D_MODEL), F32), pltpu.SemaphoreType.DMA((2,))],
        compiler_params=_params(56, ("arbitrary", "arbitrary")),
    )(x, ln_g, ln_b, wup, wd)


def _ffn_bwd(dxn, z, gu, wup, wd, ln_g, after, tm=512):
    t_tok = dxn.shape[0]

    def body(dxn_ref, z_ref, gu_ref, g_ref, wup_hbm, wd_hbm, _after,
             dx_ref, dy_ref, a_ref, dgu_ref, dg_ref, db_ref, wup_v, wd_v, dxacc, dyb, sems):
        i, j = pl.program_id(0), pl.program_id(1)
        _load_weights_once([(wup_hbm, wup_v), (wd_hbm, wd_v)], sems)

        @pl.when(j == 0)
        def _():
            dxn_t = dxn_ref[...]
            xhat, rstd = _ln_stats(z_ref[...])
            pg = jnp.sum(dxn_t * xhat, axis=0, keepdims=True)
            pb = jnp.sum(dxn_t, axis=0, keepdims=True)

            @pl.when(i == 0)
            def _():
                dg_ref[...] = pg
                db_ref[...] = pb

            @pl.when(i > 0)
            def _():
                dg_ref[...] += pg
                db_ref[...] += pb

            dz = _ln_bwd(dxn_t, xhat, rstd, g_ref[...])
            dxacc[...] = ALPHA * dz
            dy = (0.5 * dz).astype(BF16)
            dyb[...] = dy
            dy_ref[...] = dy

        da = _dot_nt(dyb[...], wd_v[j])
        g = gu_ref[0].astype(F32)
        u = gu_ref[1].astype(F32)
        sig = jax.nn.sigmoid(g)
        silu = g * sig
        a_ref[...] = (silu * u).astype(BF16)
        dg = (da * u * (sig * (1.0 + g * (1.0 - sig)))).astype(BF16)
        du = (da * silu).astype(BF16)
        dgu_ref[0] = dg
        dgu_ref[1] = du
        dxacc[...] += _dot_nt(dg, wup_v[j]) + _dot_nt(du, wup_v[N_FFN_CHUNK + j])

        @pl.when(j == N_FFN_CHUNK - 1)
        def _():
            dx_ref[...] = dxacc[...]

    tok = pl.BlockSpec((tm, D_MODEL), lambda i, j: (i, 0))
    vec = pl.BlockSpec((1, D_MODEL), lambda i, j: (0, 0))
    gu_spec = pl.BlockSpec((2, None, tm, FFN_BLK), lambda i, j: (0, j, i, 0))
    return pl.pallas_call(
        body, name="ffn_bwd", grid=(t_tok // tm, N_FFN_CHUNK),
        in_specs=[tok, tok, gu_spec, vec, _anyspec(), _anyspec(), _anyspec()],
        out_specs=[tok, tok, pl.BlockSpec((None, tm, FFN_BLK), lambda i, j: (j, i, 0)), gu_spec, vec, vec],
        out_shape=[jax.ShapeDtypeStruct((t_tok, D_MODEL), F32), jax.ShapeDtypeStruct((t_tok, D_MODEL), BF16),
                   jax.ShapeDtypeStruct((N_FFN_CHUNK, t_tok, FFN_BLK), BF16),
                   jax.ShapeDtypeStruct((2, N_FFN_CHUNK, t_tok, FFN_BLK), BF16),
                   jax.ShapeDtypeStruct((1, D_MODEL), F32), jax.ShapeDtypeStruct((1, D_MODEL), F32)],
        scratch_shapes=[pltpu.VMEM((N_DEV, D_MODEL, FFN_BLK), BF16), pltpu.VMEM((N_FFN_CHUNK, FFN_BLK, D_MODEL), BF16),
                        pltpu.VMEM((tm, D_MODEL), F32), pltpu.VMEM((tm, D_MODEL), BF16), pltpu.SemaphoreType.DMA((2,))],
        compiler_params=_params(56, ("arbitrary", "arbitrary")),
    )(dxn, z, gu, ln_g, wup, wd, after)


def _matmul_tn(a, b, after, tk=1024):
    ga, t_tok, m = a.shape
    gb, _, n = b.shape
    groups = max(ga, gb)

    def body(a_ref, b_ref, _after, o_ref):
        p = _dot_tn(a_ref[...].astype(BF16), b_ref[...].astype(BF16))

        @pl.when(pl.program_id(1) == 0)
        def _():
            o_ref[...] = p

        @pl.when(pl.program_id(1) > 0)
        def _():
            o_ref[...] += p

    return pl.pallas_call(
        body, name=f"matmul_tn_{m}x{n}", grid=(groups, t_tok // tk),
        in_specs=[pl.BlockSpec((None, tk, m), (lambda g, t: (g, t, 0)) if ga > 1 else (lambda g, t: (0, t, 0))),
                  pl.BlockSpec((None, tk, n), (lambda g, t: (g, t, 0)) if gb > 1 else (lambda g, t: (0, t, 0))),
                  _anyspec()],
        out_specs=pl.BlockSpec((None, m, n), lambda g, t: (g, 0, 0)),
        out_shape=jax.ShapeDtypeStruct((groups, m, n), F32),
        compiler_params=_params(56, ("arbitrary", "arbitrary")),
    )(a, b, after)


def _in_proj(x, w_in, tm=512):
    t_tok = x.shape[0]

    def body(x_ref, w_ref, conv_ref, qkv_ref, sgu_ref, f_ref):
        xb = x_ref[...].astype(BF16)
        conv_ref[...] = _dot(xb, w_ref[:, COL_CONV:COL_QKV])
        qkv_ref[...] = _dot(xb, w_ref[:, COL_QKV:COL_SGU]).astype(BF16)
        sgu_ref[...] = _dot(xb, w_ref[:, COL_SGU:COL_F])
        f_ref[...] = _dot(xb, w_ref[:, COL_F:D_IN_PAD])

    def tok(n):
        return pl.BlockSpec((tm, n), lambda i: (i, 0))

    return pl.pallas_call(
        body, name="mix_in_proj", grid=(t_tok // tm,),
        in_specs=[tok(D_MODEL), pl.BlockSpec((D_MODEL, D_IN_PAD), lambda i: (0, 0))],
        out_specs=[tok(768), tok(1536), tok(512), tok(128)],
        out_shape=[jax.ShapeDtypeStruct((t_tok, 768), F32), jax.ShapeDtypeStruct((t_tok, 1536), BF16),
                   jax.ShapeDtypeStruct((t_tok, 512), F32), jax.ShapeDtypeStruct((t_tok, 128), F32)],
        compiler_params=_params(48, ("arbitrary",)),
    )(x, w_in)


def _shift_down(a, k):
    row = lax.broadcasted_iota(jnp.int32, a.shape, 0)
    return jnp.where(row >= k, pltpu.roll(a, k, 0), 0.0)


def _shift_up(a, k):
    rows = a.shape[0]
    row = lax.broadcasted_iota(jnp.int32, a.shape, 0)
    return jnp.where(row < rows - k, pltpu.roll(a, rows - k, 0), 0.0)


def _tril(n):
    return lax.broadcasted_iota(jnp.int32, (n, n), 0) >= lax.broadcasted_iota(jnp.int32, (n, n), 1)


def _sgu_group_of_lane():
    return lax.broadcasted_iota(jnp.int32, (1, D_SGU), 1) // (D_SGU // N_SGU_GROUPS)


def _log_sigmoid(x):
    return jnp.minimum(x, 0.0) - jnp.log1p(jnp.exp(-jnp.abs(x)))


def _mix_mid_fwd(conv, sgu, f, conv_w, b_f, sgu_g, sgu_b, w_s, b_mat, n_seq):
    t_tok = conv.shape[0]
    seq = t_tok // n_seq
    n_chunk = seq // SGU_CHUNK
    per_blk = ATT_BLK // SGU_CHUNK

    def body(conv_ref, sgu_ref, f_ref, cw_ref, bf_ref, lg_ref, lb_ref, ws_ref, bm_ref, ya_ref, yc_ref, cum_ref):
        z = conv_ref[:, 256:512] * conv_ref[:, 512:768]
        y = cw_ref[0:1, :] * _shift_down(z, 2) + cw_ref[1:2, :] * _shift_down(z, 1) + cw_ref[2:3, :] * z
        ya_ref[...] = (conv_ref[:, 0:256] * y).astype(BF16)

        tril = _tril(SGU_CHUNK)
        grp = _sgu_group_of_lane()
        wc = [jnp.where(tril, ws_ref[g], 0.0).astype(BF16) for g in range(N_SGU_GROUPS)]
        tri_f = tril.astype(F32)
        carry = jnp.zeros((1, 128), F32)
        for n in range(n_chunk):
            rows = pl.ds(n * SGU_CHUNK, SGU_CHUNK)
            u = _gelu(sgu_ref[rows, 0:256])
            vhat, _ = _ln_stats(_gelu(sgu_ref[rows, 256:512]))
            vn = (vhat * lg_ref[...] + lb_ref[...]).astype(BF16)
            mixed = bm_ref[...]
            for g in range(N_SGU_GROUPS):
                mixed = mixed + jnp.where(grp == g, _dot(wc[g], vn), 0.0)
            yc_ref[rows, :] = (u * mixed).astype(BF16)

            log_f = _log_sigmoid(f_ref[rows, :] + bf_ref[...])
            cs = _dot(tri_f, log_f, HIGHEST) + carry
            carry = cs[SGU_CHUNK - 1:SGU_CHUNK, :]
            cs_t = cs.T
            lanes = pl.ds((n % per_blk) * SGU_CHUNK, SGU_CHUNK)
            for h in range(N_HEADS):
                cum_ref[h, n // per_blk, :, lanes] = cs_t[h:h + 1, :]

    def seq_blk(n):
        return pl.BlockSpec((seq, n), lambda b: (b, 0))

    def full(shape):
        return pl.BlockSpec(shape, lambda b: (0,) * len(shape))

    return pl.pallas_call(
        body, name="mix_mid_fwd", grid=(n_seq,),
        in_specs=[seq_blk(768), seq_blk(512), seq_blk(128), full((8, 256)), full((1, 128)), full((1, 256)),
                  full((1, 256)), full((4, 128, 128)), full((128, 256))],
        out_specs=[seq_blk(256), seq_blk(256),
                   pl.BlockSpec((N_HEADS, seq // ATT_BLK, 1, ATT_BLK), lambda b: (b, 0, 0, 0))],
        out_shape=[jax.ShapeDtypeStruct((t_tok, 256), BF16), jax.ShapeDtypeStruct((t_tok, 256), BF16),
                   jax.ShapeDtypeStruct((n_seq * N_HEADS, seq // ATT_BLK, 1, ATT_BLK), F32)],
        compiler_params=_params(48, ("arbitrary",)),
    )(conv, sgu, f, conv_w, b_f, sgu_g, sgu_b, w_s, b_mat)


def _head_masks():
    lane = lax.broadcasted_iota(jnp.int32, (1, 128), 1)
    return lane < 64, lane


def _fox_fwd(qkv, cum_t, n_seq):
    t_tok = qkv.shape[0]
    seq = t_tok // n_seq
    nq = seq // ATT_BLK
    blk = ATT_BLK

    def body(q_ref, k_ref, v_ref, c0_ref, c1_ref, o_ref, lse_ref):
        qi = pl.program_id(2)
        first, _ = _head_masks()
        qs = q_ref[...] * ATT_SCALE
        zero = jnp.zeros_like(qs)
        q0 = jnp.where(first, qs, zero)
        q1 = jnp.where(first, zero, qs)
        causal = _tril(blk)
        one = jnp.ones((1, 128), BF16)

        def step(kb, carry, masked):
            m0, m1, acc0, acc1 = carry
            rows = pl.ds(pl.multiple_of(kb * blk, blk), blk)
            k = k_ref[rows, :]
            v = v_ref[rows, :]

            def head(qh, c_ref, m, acc, vh):
                s = _dot_nt(qh, k) - c_ref[kb]
                if masked:
                    s = jnp.where(causal, s, NEG)
                m_new = jnp.maximum(m, jnp.max(s, axis=1, keepdims=True))
                p = jnp.exp(s - m_new)
                return m_new, acc * jnp.exp(m - m_new) + _dot(p.astype(BF16), vh)

            m0, acc0 = head(q0, c0_ref, m0, acc0, jnp.where(first, v, one))
            m1, acc1 = head(q1, c1_ref, m1, acc1, jnp.where(first, one, v))
            return m0, m1, acc0, acc1

        col = jnp.full((blk, 1), NEG, F32)
        zacc = jnp.zeros((blk, 128), F32)
        carry = lax.fori_loop(0, qi, lambda kb, cr: step(kb, cr, False), (col, col, zacc, zacc))
        m0, m1, acc0, acc1 = step(qi, carry, True)
        l0 = pltpu.roll(acc0, 64, 1)
        l1 = pltpu.roll(acc1, 64, 1)
        o_ref[...] = jnp.where(first, acc0 / l0, acc1 / l1).astype(BF16)
        lse_ref[...] = jnp.where(first, m0 + jnp.log(l0), m1 + jnp.log(l1))

    cum_spec0 = pl.BlockSpec((None, nq, 1, blk), lambda b, hp, qi: (b * N_HEADS + 2 * hp, 0, 0, 0))
    cum_spec1 = pl.BlockSpec((None, nq, 1, blk), lambda b, hp, qi: (b * N_HEADS + 2 * hp + 1, 0, 0, 0))
    out_spec = pl.BlockSpec((blk, 128), lambda b, hp, qi: (b * nq + qi, hp))
    return pl.pallas_call(
        body, name="fox_fwd", grid=(n_seq, 4, nq),
        in_specs=[pl.BlockSpec((blk, 128), lambda b, hp, qi: (b * nq + qi, hp)),
                  pl.BlockSpec((seq, 128), lambda b, hp, qi: (b, 4 + hp)),
                  pl.BlockSpec((seq, 128), lambda b, hp, qi: (b, 8 + hp)), cum_spec0, cum_spec1],
        out_specs=[out_spec, out_spec],
        out_shape=[jax.ShapeDtypeStruct((t_tok, D_FOX), BF16), jax.ShapeDtypeStruct((t_tok, D_FOX), F32)],
        compiler_params=_params(32, ("arbitrary", "arbitrary", "arbitrary")),
    )(qkv, qkv, qkv, cum_t, cum_t)


def _fox_bwd(qkv, cum_t, o, lse, d_o, n_seq):
    t_tok = qkv.shape[0]
    seq = t_tok // n_seq
    nk = seq // ATT_BLK
    blk = ATT_BLK

    def body(q_ref, k_ref, v_ref, c0_ref, c1_ref, o_ref, lse_ref, do_ref,
             dq_ref, dk_ref, dv_ref, drow_ref, dcol_ref):
        kb = pl.program_id(2)
        first, lane = _head_masks()
        second = jnp.logical_not(first)
        k = k_ref[...]
        v = v_ref[...]
        zero = jnp.zeros_like(k)
        one = jnp.ones((1, 128), BF16)
        ks = k * ATT_SCALE
        causal = _tril(blk)

        @pl.when(kb == 0)
        def _():
            dq_ref[...] = jnp.zeros_like(dq_ref)
            drow_ref[...] = jnp.zeros_like(drow_ref)

        def step(qi, carry, masked):
            rows = pl.ds(pl.multiple_of(qi * blk, blk), blk)
            qs = q_ref[rows, :] * ATT_SCALE
            d_o = do_ref[rows, :]
            dd = d_o.astype(F32) * o_ref[rows, :].astype(F32)
            lse_t = lse_ref[rows, :]

            def head(mine, c, lse_lane, dk, dv):
                qh = jnp.where(mine, qs, zero)
                doh = jnp.where(mine, d_o, zero)
                delta = jnp.sum(jnp.where(mine, dd, 0.0), axis=1, keepdims=True)
                lse_h = jnp.sum(jnp.where(lane == lse_lane, lse_t, 0.0), axis=1, keepdims=True)
                s = _dot_nt(qh, k) - c
                if masked:
                    s = jnp.where(causal, s, NEG)
                p = jnp.exp(s - lse_h)
                ds = (p * (_dot_nt(doh, v) - delta)).astype(BF16)
                dk = dk + _dot_tn(ds, jnp.where(mine, qs, one))
                dv = dv + _dot_tn(p.astype(BF16), doh)
                return dk, dv, _dot(ds, jnp.where(mine, ks, one))

            dk0, dv0, dk1, dv1 = carry
            dk0, dv0, dq0 = head(first, c0_ref[...], 0, dk0, dv0)
            dk1, dv1, dq1 = head(second, c1_ref[...], 64, dk1, dv1)
            dq_ref[rows, :] += jnp.where(first, dq0, dq1)
            drow_ref[rows, :] += jnp.where(first, dq1, dq0)
            return dk0, dv0, dk1, dv1

        zt = jnp.zeros((blk, 128), F32)
        carry = step(kb, (zt, zt, zt, zt), True)
        dk0, dv0, dk1, dv1 = lax.fori_loop(kb + 1, nk, lambda qi, cr: step(qi, cr, False), carry)
        dk_ref[...] = jnp.where(first, dk0, dk1).astype(BF16)
        dcol_ref[...] = jnp.where(first, dk1, dk0)
        dv_ref[...] = (dv0 + dv1).astype(BF16)

    def seq_spec(col0):
        return pl.BlockSpec((seq, 128), lambda b, hp, kb: (b, col0 + hp))

    def key_spec(col0):
        return pl.BlockSpec((blk, 128), lambda b, hp, kb: (b * nk + kb, col0 + hp))

    def cum_spec(h):
        return pl.BlockSpec((None, None, 1, blk), lambda b, hp, kb: (b * N_HEADS + 2 * hp + h, kb, 0, 0))

    return pl.pallas_call(
        body, name="fox_bwd", grid=(n_seq, 4, nk),
        in_specs=[seq_spec(0), key_spec(4), key_spec(8), cum_spec(0), cum_spec(1), seq_spec(0), seq_spec(0), seq_spec(0)],
        out_specs=[seq_spec(0), key_spec(0), key_spec(0), seq_spec(0), key_spec(0)],
        out_shape=[jax.ShapeDtypeStruct((t_tok, D_FOX), F32), jax.ShapeDtypeStruct((t_tok, D_FOX), BF16),
                   jax.ShapeDtypeStruct((t_tok, D_FOX), BF16), jax.ShapeDtypeStruct((t_tok, D_FOX), F32),
                   jax.ShapeDtypeStruct((t_tok, D_FOX), F32)],
        compiler_params=_params(48, ("arbitrary", "arbitrary", "arbitrary")),
    )(qkv, qkv, qkv, cum_t, cum_t, o, lse, d_o)


def _mix_out_fwd(ya, yb, yc, x, w_out, ln_g, ln_b, tm=512):
    t_tok = x.shape[0]

    def body(ya_ref, yb_ref, yc_ref, x_ref, w_ref, g_ref, b_ref, xn_ref, xnb_ref, z_ref):
        mo = _dot(ya_ref[...], w_ref[0:256, :]) + _dot(yb_ref[...], w_ref[256:768, :]) + _dot(yc_ref[...], w_ref[768:1024, :])
        z = ALPHA * x_ref[...] + mo
        xhat, _ = _ln_stats(z)
        xn = xhat * g_ref[...] + b_ref[...]
        z_ref[...] = z
        xn_ref[...] = xn
        xnb_ref[...] = xn.astype(BF16)

    def tok(n):
        return pl.BlockSpec((tm, n), lambda i: (i, 0))

    vec = pl.BlockSpec((1, D_MODEL), lambda i: (0, 0))
    return pl.pallas_call(
        body, name="mix_out_fwd", grid=(t_tok // tm,),
        in_specs=[tok(256), tok(512), tok(256), tok(D_MODEL),
                  pl.BlockSpec((D_MODEL, D_MODEL), lambda i: (0, 0)), vec, vec],
        out_specs=[tok(D_MODEL)] * 3,
        out_shape=[jax.ShapeDtypeStruct((t_tok, D_MODEL), F32), jax.ShapeDtypeStruct((t_tok, D_MODEL), BF16),
                   jax.ShapeDtypeStruct((t_tok, D_MODEL), F32)],
        compiler_params=_params(40, ("arbitrary",)),
    )(ya, yb, yc, x, w_out, ln_g, ln_b)


def _mix_out_bwd(dxn, z, w_out, ln_g, tm=512):
    t_tok = dxn.shape[0]

    def body(dxn_ref, z_ref, w_ref, g_ref, dz_ref, dzb_ref, dya_ref, dyb_ref, dyc_ref, dg_ref, db_ref):
        i = pl.program_id(0)
        dxn_t = dxn_ref[...]
        xhat, rstd = _ln_stats(z_ref[...])
        pg = jnp.sum(dxn_t * xhat, axis=0, keepdims=True)
        pb = jnp.sum(dxn_t, axis=0, keepdims=True)

        @pl.when(i == 0)
        def _():
            dg_ref[...] = pg
            db_ref[...] = pb

        @pl.when(i > 0)
        def _():
            dg_ref[...] += pg
            db_ref[...] += pb

        dz = _ln_bwd(dxn_t, xhat, rstd, g_ref[...])
        dzb = dz.astype(BF16)
        dz_ref[...] = dz
        dzb_ref[...] = dzb
        dya_ref[...] = _dot_nt(dzb, w_ref[0:256, :])
        dyb_ref[...] = _dot_nt(dzb, w_ref[256:768, :]).astype(BF16)
        dyc_ref[...] = _dot_nt(dzb, w_ref[768:1024, :])

    def tok(n):
        return pl.BlockSpec((tm, n), lambda i: (i, 0))

    vec = pl.BlockSpec((1, D_MODEL), lambda i: (0, 0))
    return pl.pallas_call(
        body, name="mix_out_bwd", grid=(t_tok // tm,),
        in_specs=[tok(D_MODEL), tok(D_MODEL), pl.BlockSpec((D_MODEL, D_MODEL), lambda i: (0, 0)), vec],
        out_specs=[tok(D_MODEL), tok(D_MODEL), tok(256), tok(512), tok(256), vec, vec],
        out_shape=[jax.ShapeDtypeStruct((t_tok, D_MODEL), F32), jax.ShapeDtypeStruct((t_tok, D_MODEL), BF16),
                   jax.ShapeDtypeStruct((t_tok, 256), F32), jax.ShapeDtypeStruct((t_tok, 512), BF16),
                   jax.ShapeDtypeStruct((t_tok, 256), F32),
                   jax.ShapeDtypeStruct((1, D_MODEL), F32), jax.ShapeDtypeStruct((1, D_MODEL), F32)],
        compiler_params=_params(40, ("arbitrary",)),
    )(dxn, z, w_out, ln_g)


def _conv_bwd(conv, dya, conv_w, n_seq):
    t_tok = conv.shape[0]
    seq = t_tok // n_seq

    def body(conv_ref, dya_ref, cw_ref, dconv_ref, dcw_ref):
        @pl.when(pl.program_id(0) == 0)
        def _():
            dcw_ref[...] = jnp.zeros_like(dcw_ref)

        z = conv_ref[:, 256:512] * conv_ref[:, 512:768]
        z1 = _shift_down(z, 1)
        z2 = _shift_down(z, 2)
        y = cw_ref[0:1, :] * z2 + cw_ref[1:2, :] * z1 + cw_ref[2:3, :] * z
        dya_t = dya_ref[...]
        dconv_ref[:, 0:256] = (dya_t * y).astype(BF16)
        dy = dya_t * conv_ref[:, 0:256]
        dcw_ref[0:1, :] += jnp.sum(dy * z2, axis=0, keepdims=True)
        dcw_ref[1:2, :] += jnp.sum(dy * z1, axis=0, keepdims=True)
        dcw_ref[2:3, :] += jnp.sum(dy * z, axis=0, keepdims=True)
        dz = cw_ref[2:3, :] * dy + cw_ref[1:2, :] * _shift_up(dy, 1) + cw_ref[0:1, :] * _shift_up(dy, 2)
        dconv_ref[:, 256:512] = (dz * conv_ref[:, 512:768]).astype(BF16)
        dconv_ref[:, 512:768] = (dz * conv_ref[:, 256:512]).astype(BF16)

    def seq_blk(n):
        return pl.BlockSpec((seq, n), lambda b: (b, 0))

    par = pl.BlockSpec((8, 256), lambda b: (0, 0))
    return pl.pallas_call(
        body, name="conv_bwd", grid=(n_seq,),
        in_specs=[seq_blk(768), seq_blk(256), par], out_specs=[seq_blk(768), par],
        out_shape=[jax.ShapeDtypeStruct((t_tok, 768), BF16), jax.ShapeDtypeStruct((8, 256), F32)],
        compiler_params=_params(56, ("arbitrary",)),
    )(conv, dya, conv_w)


def _sgu_gate_bwd(sgu, f, dyc, drow, dcol, b_f, sgu_g, sgu_b, w_s, b_mat, n_seq):
    t_tok = sgu.shape[0]
    seq = t_tok // n_seq
    n_chunk = seq // SGU_CHUNK

    def body(sgu_ref, f_ref, dyc_ref, drow_ref, dcol_ref, bf_ref, lg_ref, lb_ref, ws_ref, bm_ref,
             dsgu_ref, df_ref, dbf_ref, dlg_ref, dlb_ref, dws_ref, dbs_ref, dbm_acc):
        b = pl.program_id(0)

        @pl.when(b == 0)
        def _():
            for r in (dbf_ref, dlg_ref, dlb_ref, dws_ref, dbm_acc):
                r[...] = jnp.zeros_like(r)

        tril = _tril(SGU_CHUNK)
        grp = _sgu_group_of_lane()
        wc = [jnp.where(tril, ws_ref[g], 0.0).astype(BF16) for g in range(N_SGU_GROUPS)]
        for n in range(n_chunk):
            rows = pl.ds(n * SGU_CHUNK, SGU_CHUNK)
            su = sgu_ref[rows, 0:256]
            sv = sgu_ref[rows, 256:512]
            u = _gelu(su)
            vhat, rstd = _ln_stats(_gelu(sv))
            vn = (vhat * lg_ref[...] + lb_ref[...]).astype(BF16)
            mixed = bm_ref[...]
            for g in range(N_SGU_GROUPS):
                mixed = mixed + jnp.where(grp == g, _dot(wc[g], vn), 0.0)
            dyc_t = dyc_ref[rows, :]
            dsgu_ref[rows, 0:256] = (dyc_t * mixed * _gelu_grad(su)).astype(BF16)
            dmixed = dyc_t * u
            dbm_acc[...] += dmixed
            dvn = jnp.zeros((SGU_CHUNK, D_SGU), F32)
            for g in range(N_SGU_GROUPS):
                dm_g = jnp.where(grp == g, dmixed, 0.0).astype(BF16)
                dws_ref[g] += _dot_nt(dm_g, vn)
                dvn = dvn + _dot_tn(wc[g], dm_g)
            dlg_ref[...] += jnp.sum(dvn * vhat, axis=0, keepdims=True)
            dlb_ref[...] += jnp.sum(dvn, axis=0, keepdims=True)
            dsgu_ref[rows, 256:512] = (_ln_bwd(dvn, vhat, rstd, lg_ref[...]) * _gelu_grad(sv)).astype(BF16)

        later = (lax.broadcasted_iota(jnp.int32, (128, 128), 0) <= lax.broadcasted_iota(jnp.int32, (128, 128), 1)).astype(F32)
        head = lax.broadcasted_iota(jnp.int32, (D_FOX, 128), 1)
        pick = (lax.broadcasted_iota(jnp.int32, (D_FOX, 128), 0) == 128 * (head // 2) + 64 * (1 - head % 2)).astype(F32)
        carry = jnp.zeros((1, 128), F32)
        for n in reversed(range(n_chunk)):
            rows = pl.ds(n * SGU_CHUNK, SGU_CHUNK)
            dcum_n = _dot(drow_ref[rows, :] - dcol_ref[rows, :], pick, HIGHEST)
            dlf = _dot(later, dcum_n, HIGHEST) + carry
            carry = carry + jnp.sum(dcum_n, axis=0, keepdims=True)
            df = dlf * jax.nn.sigmoid(-(f_ref[rows, :] + bf_ref[...]))
            df_ref[rows, :] = df.astype(BF16)
            dbf_ref[...] += jnp.sum(df, axis=0, keepdims=True)

        @pl.when(b == n_seq - 1)
        def _():
            for g in range(N_SGU_GROUPS):
                dws_ref[g] = jnp.where(tril, dws_ref[g], 0.0)
            sel = (lax.broadcasted_iota(jnp.int32, (D_SGU, 128), 0) // (D_SGU // N_SGU_GROUPS)
                   == lax.broadcasted_iota(jnp.int32, (D_SGU, 128), 1)).astype(F32)
            dbs_ref[...] = _dot(dbm_acc[...], sel, HIGHEST)

    def seq_blk(n):
        return pl.BlockSpec((seq, n), lambda b: (b, 0))

    def full(shape):
        return pl.BlockSpec(shape, lambda b: (0,) * len(shape))

    param_shapes = [(1, 128), (1, 256), (1, 256), (4, 128, 128), (128, 128)]
    return pl.pallas_call(
        body, name="sgu_gate_bwd", grid=(n_seq,),
        in_specs=[seq_blk(512), seq_blk(128), seq_blk(256), seq_blk(D_FOX), seq_blk(D_FOX),
                  full((1, 128)), full((1, 256)), full((1, 256)), full((4, 128, 128)), full((128, 256))],
        out_specs=[seq_blk(512), seq_blk(128)] + [full(s) for s in param_shapes],
        out_shape=[jax.ShapeDtypeStruct((t_tok, 512), BF16), jax.ShapeDtypeStruct((t_tok, 128), BF16)]
        + [jax.ShapeDtypeStruct(s, F32) for s in param_shapes],
        scratch_shapes=[pltpu.VMEM((128, 256), F32)],
        compiler_params=_params(48, ("arbitrary",)),
    )(sgu, f, dyc, drow, dcol, b_f, sgu_g, sgu_b, w_s, b_mat)


def _mix_in_bwd(dconv, dq, dk, dv, dsgu, df, dz, w_in, tm=512):
    t_tok = dz.shape[0]

    def body(dconv_ref, dq_ref, dk_ref, dv_ref, dsgu_ref, df_ref, dz_ref, w_ref, dx_ref, dp_ref):
        dqb = dq_ref[...].astype(BF16)
        pieces = [(COL_CONV, dconv_ref[...]), (COL_QKV, dqb), (COL_QKV + 512, dk_ref[...]), (COL_QKV + 1024, dv_ref[...]),
                  (COL_SGU, dsgu_ref[...]), (COL_F, df_ref[...])]
        dx = ALPHA * dz_ref[...]
        for col, val in pieces:
            width = val.shape[1]
            dp_ref[:, col:col + width] = val
            dx = dx + _dot_nt(val, w_ref[:, col:col + width])
        dx_ref[...] = dx

    def tok(n):
        return pl.BlockSpec((tm, n), lambda i: (i, 0))

    return pl.pallas_call(
        body, name="mix_in_bwd", grid=(t_tok // tm,),
        in_specs=[tok(768), tok(512), tok(512), tok(512), tok(512), tok(128), tok(D_MODEL),
                  pl.BlockSpec((D_MODEL, D_IN_PAD), lambda i: (0, 0))],
        out_specs=[tok(D_MODEL), tok(D_IN_PAD)],
        out_shape=[jax.ShapeDtypeStruct((t_tok, D_MODEL), F32), jax.ShapeDtypeStruct((t_tok, D_IN_PAD), BF16)],
        compiler_params=_params(48, ("arbitrary",)),
    )(dconv, dq, dk, dv, dsgu, df, dz, w_in)


def _loss_grad(y, target, tm=512):
    t_tok = y.shape[0]

    def body(y_ref, t_ref, dy_ref, loss_ref):
        err = y_ref[...] - t_ref[...]
        dy_ref[...] = err * (1.0 / D_MODEL)
        part = jnp.sum(jnp.sum(err * err, axis=1, keepdims=True), axis=0, keepdims=True) * (0.5 / D_MODEL)

        @pl.when(pl.program_id(0) == 0)
        def _():
            loss_ref[...] = jnp.zeros_like(loss_ref)

        loss_ref[...] += part

    tok = pl.BlockSpec((tm, D_MODEL), lambda i: (i, 0))
    return pl.pallas_call(
        body, name="loss_grad", grid=(t_tok // tm,),
        in_specs=[tok, tok], out_specs=[tok, pl.BlockSpec((1, 128), lambda i: (0, 0))],
        out_shape=[jax.ShapeDtypeStruct((t_tok, D_MODEL), F32), jax.ShapeDtypeStruct((1, 128), F32)],
        compiler_params=_params(32, ("arbitrary",)),
    )(y, target)


def _pad_rows(a, rows):
    return jnp.pad(a, ((0, rows - a.shape[0]), (0, 0)))


F_BLOCK = F_ORIG // D_IN_SHARD
F_AT = F_ORIG - F_BLOCK * D_IN_SHARD
assert (F_ORIG + N_HEADS) // D_IN_SHARD == F_BLOCK


def _w_in_from_blocks(g):
    fb = g[F_BLOCK]
    zeros = jnp.zeros((D_MODEL, D_IN_PAD - COL_F - N_HEADS), g.dtype)
    return jnp.concatenate([g[d] for d in range(F_BLOCK)] + [fb[:, :F_AT], fb[:, F_AT + N_HEADS:]]
                           + [g[d] for d in range(F_BLOCK + 1, N_DEV)] + [fb[:, F_AT:F_AT + N_HEADS], zeros], axis=1)


def _w_in_to_blocks(dw):
    def cols(lo, hi):
        shift = 0 if hi <= F_ORIG else N_HEADS
        return dw[:, lo - shift:hi - shift]

    blocks = []
    for d in range(N_DEV):
        lo, hi = d * D_IN_SHARD, (d + 1) * D_IN_SHARD
        if d == F_BLOCK:
            blocks.append(jnp.concatenate([cols(lo, F_ORIG), dw[:, COL_F:COL_F + N_HEADS], cols(F_ORIG + N_HEADS, hi)], axis=1))
        else:
            blocks.append(cols(lo, hi))
    return jnp.stack(blocks)


SMALL_ROWS = DEPTH * (6 * 8 + 2 * 8 + 512 + 8 + 8 + 8)


def _pack_small(p):
    rows = []
    for l in range(DEPTH):
        for name in ("ln1_g", "ln1_b", "ln2_g", "ln2_b", "ln3_g", "ln3_b"):
            rows.append(p[name][l].reshape(8, 128))
        for name in ("sgu_ln_g", "sgu_ln_b"):
            rows.append(_pad_rows(p[name][l].reshape(2, 128), 8))
        rows.append(p["sgu_w_s"][l].reshape(512, 128))
        rows.append(_pad_rows(p["sgu_b_s"][l], 8))
        rows.append(_pad_rows(jnp.pad(p["fox_b_f"][l], (0, 128 - N_HEADS)).reshape(1, 128), 8))
        rows.append(_pad_rows(p["conv_w"][l].reshape(6, 128), 8))
    return jnp.concatenate(rows, axis=0)


def _unpack_small(a):
    out = {}
    r = 0

    def take(n, valid):
        nonlocal r
        piece = a[r:r + valid]
        r += n
        return piece

    per_layer = []
    for l in range(DEPTH):
        d = {}
        for name in ("ln1_g", "ln1_b", "ln2_g", "ln2_b", "ln3_g", "ln3_b"):
            d[name] = take(8, 8).reshape(D_MODEL)
        for name in ("sgu_ln_g", "sgu_ln_b"):
            d[name] = take(8, 2).reshape(D_SGU)
        d["sgu_w_s"] = take(512, 512).reshape(N_SGU_GROUPS, SGU_CHUNK, SGU_CHUNK)
        d["sgu_b_s"] = take(8, 4).reshape(N_SGU_GROUPS, SGU_CHUNK)
        d["fox_b_f"] = take(8, 1).reshape(128)[:N_HEADS]
        d["conv_w"] = take(8, 6).reshape(3, D_CONV)
        per_layer.append(d)
    for name in per_layer[0]:
        out[name] = jnp.stack([per_layer[l][name] for l in range(DEPTH)])
    return out


SMALL_NAMES = ("ln1_g", "ln1_b", "fox_b_f", "sgu_ln_g", "sgu_ln_b", "sgu_w_s", "sgu_b_s", "ln2_g", "ln2_b", "ln3_g", "ln3_b")
BIG_NAMES = ("ffn1_w_up", "ffn1_w_down", "mix_w_in", "mix_w_out", "ffn2_w_up", "ffn2_w_down")
WEIGHT_ORDER = ("ln1_g", "ln1_b", "ffn1_w_up", "ffn1_w_down", "mix_w_in", "fox_b_f", "conv_w", "sgu_ln_g", "sgu_ln_b",
                "sgu_w_s", "sgu_b_s", "mix_w_out", "ln2_g", "ln2_b", "ffn2_w_up", "ffn2_w_down", "ln3_g", "ln3_b")


class _Overlap:
    def __init__(self, w, after, me, core):
        self.core = core
        groups = [[("ffn1_w_up", 0), ("ffn1_w_down", 0)],
                  [("mix_w_in", 0), ("mix_w_out", 0), ("ffn2_w_up", 0), ("ffn2_w_down", 0)]]
        groups += [[(name, l) for name in BIG_NAMES] for l in range(1, DEPTH)]
        self.gathers = []
        for gi, group in enumerate(groups):
            shards = [w[name][l].astype(BF16) for name, l in group]
            lands = [lax.dynamic_update_slice(lax.empty((N_DEV,) + s.shape, BF16), s[None], (me, 0, 0)) for s in shards]
            started = _exchange_start(f"allgather_start_{gi}", _gather_plan(len(group)), 3 * len(group), shards + lands, after)
            after = started[3]
            self.gathers.append((group, started))
        self.all_started = after
        self.gathered = {}
        self.scatters = {}
        self.order = []

    def weights(self, layer, part, after):
        gi = layer + 1 if layer > 0 else (0 if part == "ffn1" else 1)
        group, started = self.gathers[gi]
        if gi not in self.gathered:
            m = len(group)
            arrays = _exchange_wait(f"allgather_wait_{gi}", _gather_plan(m), 3 * m, started,
                                    self.all_started if after is None else after)
            self.gathered[gi] = dict(zip(group, _gather_finish(arrays[:m], arrays[m:])))
        g = self.gathered[gi]

        def ffn(n):
            return g[(f"ffn{n}_w_up", layer)], g[(f"ffn{n}_w_down", layer)].reshape(N_FFN_CHUNK, FFN_BLK, D_MODEL)

        if part == "ffn1":
            return ffn(1)
        return (_w_in_from_blocks(g[("mix_w_in", layer)]), g[("mix_w_out", layer)].reshape(D_MODEL, D_MODEL), *ffn(2))

    def push(self, key, items):
        n = len(items)
        grads = [g for _, _, g in items]
        lands = [lax.empty((4,) + g.shape[1:], F32) for g in grads]
        started = _exchange_start(f"rs_sibling_start_{key[0]}{key[1]}", _sibling_plan(n), 4 * n, grads + lands, self.core)
        self.scatters[key] = dict(items=items, sibling=started)
        self.order.append(key)
        return started[3]

    def advance(self, key, after):
        st = self.scatters[key]
        n = len(st["items"])
        arrays = _exchange_wait(f"rs_sibling_wait_{key[0]}{key[1]}", _sibling_plan(n), 4 * n, st["sibling"], after)
        partials = [_chip_partial(g, r, self.core) for g, r in zip(arrays[:n], arrays[n:])]
        p16 = [p for _, p in partials]
        lands = [lax.empty((3,) + p.shape[1:], BF16) for p in p16]
        started = _exchange_start(f"rs_chip_start_{key[0]}{key[1]}", _chip_plan(n), 3 * n, p16 + lands, self.core)
        st.update(p32=[p for p, _ in partials], chip=started)
        return started[3]

    def finish(self, slot, w, m, v):
        res = {}
        after = self.scatters[self.order[-1]]["chip"][3]
        for key in self.order:
            st = self.scatters[key]
            n = len(st["items"])
            arrays = _exchange_wait(f"rs_chip_wait_{key[0]}{key[1]}", _chip_plan(n), 3 * n, st["chip"], after)
            for (name, l, _), p32, r16 in zip(st["items"], st["p32"], arrays[n:]):
                res[name] = _adamw_shard(p32, r16, slot, w[name], m[name], v[name], l, res.get(name))
                after = res[name][0]
        return res


def _local_step(x, target, comm, small, n_seq):
    def vec(a):
        return a.reshape(1, -1)

    saved = []
    h = x
    for l in range(DEPTH):
        s = {}
        s["up1"], s["down1"] = comm.weights(l, "ffn1", None if l == 0 else h)
        h1, h1b, s["z1"], s["gu1"] = _ffn_fwd(h, s["up1"], s["down1"], vec(small["ln1_g"][l]), vec(small["ln1_b"][l]))
        s["w_in"], s["w_out"], s["up2"], s["down2"] = comm.weights(l, "rest", s["z1"])
        s["x1b"] = h1b
        conv, qkv, sgu, f = _in_proj(h1, s["w_in"])
        cw = _pad_rows(small["conv_w"][l], 8)
        bf = jnp.pad(small["fox_b_f"][l], (0, 128 - N_HEADS)).reshape(1, 128)
        b_mat = jnp.repeat(small["sgu_b_s"][l].T, D_SGU // N_SGU_GROUPS, axis=1)
        mid_params = (cw, bf, vec(small["sgu_ln_g"][l]), vec(small["sgu_ln_b"][l]), small["sgu_w_s"][l], b_mat)
        ya, yc, cum_t = _mix_mid_fwd(conv, sgu, f, *mid_params, n_seq)
        yb, lse = _fox_fwd(qkv, cum_t, n_seq)
        h2, h2b, s["z2"] = _mix_out_fwd(ya, yb, yc, h1, s["w_out"], vec(small["ln2_g"][l]), vec(small["ln2_b"][l]))
        s.update(conv=conv, qkv=qkv, sgu=sgu, f=f, mid_params=mid_params, ya=ya, yb=yb, yc=yc, cum_t=cum_t, lse=lse, x2b=h2b)
        h3, h3b, s["z3"], s["gu2"] = _ffn_fwd(h2, s["up2"], s["down2"], vec(small["ln3_g"][l]), vec(small["ln3_b"][l]))
        s["x3b"] = h3b
        saved.append(s)
        h = h3

    dh, loss = _loss_grad(h, target)

    small_grads = [None] * DEPTH
    token = loss
    pending = None
    for l in reversed(range(DEPTH)):
        s = saved[l]
        x0b = saved[l - 1]["x3b"] if l > 0 else x.astype(BF16)
        sg = {}
        dh, dy, a, dgu, sg["ln3_g"], sg["ln3_b"] = _ffn_bwd(dh, s["z3"], s["gu2"], s["up2"], s["down2"], vec(small["ln3_g"][l]), token)
        if pending is not None:
            token = comm.advance(pending, dh)
        g_up2 = _matmul_tn(s["x2b"][None], dgu.reshape(N_DEV, -1, FFN_BLK), token)
        g_down2 = _matmul_tn(a, dy[None], token).reshape(N_DEV, FFN_BLK // 2, D_MODEL)
        dz, dzb, dya, dyb, dyc, sg["ln2_g"], sg["ln2_b"] = _mix_out_bwd(dh, s["z2"], s["w_out"], vec(small["ln2_g"][l]))
        dwo = [_matmul_tn(y[None], dzb[None], token)[0] for y in (s["ya"], s["yb"], s["yc"])]
        g_out = jnp.concatenate(dwo, axis=0).reshape(N_DEV, D_MODEL // N_DEV, D_MODEL)
        dq, dk, dv, drow, dcol = _fox_bwd(s["qkv"], s["cum_t"], s["yb"], s["lse"], dyb, n_seq)
        dconv, dcw = _conv_bwd(s["conv"], dya, s["mid_params"][0], n_seq)
        dsgu, df, dbf, dlg, dlb, dws, dbs = _sgu_gate_bwd(s["sgu"], s["f"], dyc, drow, dcol, *s["mid_params"][1:], n_seq)
        sg.update(conv_w=dcw[:3], fox_b_f=dbf[0, :N_HEADS], sgu_ln_g=dlg[0], sgu_ln_b=dlb[0], sgu_w_s=dws,
                  sgu_b_s=dbs[:, :N_SGU_GROUPS].T)
        dh, dp = _mix_in_bwd(dconv, dq, dk, dv, dsgu, df, dz, s["w_in"])
        g_in = _w_in_to_blocks(_matmul_tn(s["x1b"][None], dp[None], token, tk=512)[0])
        first = [("ffn2_w_up", l, g_up2), ("ffn2_w_down", l, g_down2), ("mix_w_out", l, g_out), ("mix_w_in", l, g_in)]
        if l == 0:
            token = comm.push((l, "a"), first)
            pending, first = (l, "a"), []
        dh, dy, a, dgu, sg["ln1_g"], sg["ln1_b"] = _ffn_bwd(dh, s["z1"], s["gu1"], s["up1"], s["down1"], vec(small["ln1_g"][l]), token)
        if l == 0:
            token = comm.advance(pending, dh)
        g_up1 = _matmul_tn(x0b[None], dgu.reshape(N_DEV, -1, FFN_BLK), token)
        g_down1 = _matmul_tn(a, dy[None], token).reshape(N_DEV, FFN_BLK // 2, D_MODEL)
        key = (l, "b")
        token = comm.push(key, first + [("ffn1_w_up", l, g_up1), ("ffn1_w_down", l, g_down1)])
        pending = key
        if l == 0:
            token = comm.advance(key, token)
        for name in ("ln1_g", "ln1_b", "ln2_g", "ln2_b", "ln3_g", "ln3_b"):
            sg[name] = sg[name][0]
        small_grads[l] = sg
    return loss, dh, small_grads


def kernel(x, ln1_g, ln1_b, ffn1_w_up, ffn1_w_down, mix_w_in, fox_b_f, conv_w, sgu_ln_g, sgu_ln_b, sgu_w_s, sgu_b_s, mix_w_out, ln2_g, ln2_b, ffn2_w_up, ffn2_w_down, ln3_g, ln3_b, loss_target, m_ln1_g, m_ln1_b, m_ffn1_w_up, m_ffn1_w_down, m_mix_w_in, m_fox_b_f, m_conv_w, m_sgu_ln_g, m_sgu_ln_b, m_sgu_w_s, m_sgu_b_s, m_mix_w_out, m_ln2_g, m_ln2_b, m_ffn2_w_up, m_ffn2_w_down, m_ln3_g, m_ln3_b, v_ln1_g, v_ln1_b, v_ffn1_w_up, v_ffn1_w_down, v_mix_w_in, v_fox_b_f, v_conv_w, v_sgu_ln_g, v_sgu_ln_b, v_sgu_w_s, v_sgu_b_s, v_mix_w_out, v_ln2_g, v_ln2_b, v_ffn2_w_up, v_ffn2_w_down, v_ln3_g, v_ln3_b):
    w = dict(ln1_g=ln1_g, ln1_b=ln1_b, ffn1_w_up=ffn1_w_up, ffn1_w_down=ffn1_w_down, mix_w_in=mix_w_in, fox_b_f=fox_b_f,
             conv_w=conv_w, sgu_ln_g=sgu_ln_g, sgu_ln_b=sgu_ln_b, sgu_w_s=sgu_w_s, sgu_b_s=sgu_b_s, mix_w_out=mix_w_out,
             ln2_g=ln2_g, ln2_b=ln2_b, ffn2_w_up=ffn2_w_up, ffn2_w_down=ffn2_w_down, ln3_g=ln3_g, ln3_b=ln3_b)
    m = dict(ln1_g=m_ln1_g, ln1_b=m_ln1_b, ffn1_w_up=m_ffn1_w_up, ffn1_w_down=m_ffn1_w_down, mix_w_in=m_mix_w_in,
             fox_b_f=m_fox_b_f, conv_w=m_conv_w, sgu_ln_g=m_sgu_ln_g, sgu_ln_b=m_sgu_ln_b, sgu_w_s=m_sgu_w_s,
             sgu_b_s=m_sgu_b_s, mix_w_out=m_mix_w_out, ln2_g=m_ln2_g, ln2_b=m_ln2_b, ffn2_w_up=m_ffn2_w_up,
             ffn2_w_down=m_ffn2_w_down, ln3_g=m_ln3_g, ln3_b=m_ln3_b)
    v = dict(ln1_g=v_ln1_g, ln1_b=v_ln1_b, ffn1_w_up=v_ffn1_w_up, ffn1_w_down=v_ffn1_w_down, mix_w_in=v_mix_w_in,
             fox_b_f=v_fox_b_f, conv_w=v_conv_w, sgu_ln_g=v_sgu_ln_g, sgu_ln_b=v_sgu_ln_b, sgu_w_s=v_sgu_w_s,
             sgu_b_s=v_sgu_b_s, mix_w_out=v_mix_w_out, ln2_g=v_ln2_g, ln2_b=v_ln2_b, ffn2_w_up=v_ffn2_w_up,
             ffn2_w_down=v_ffn2_w_down, ln3_g=v_ln3_g, ln3_b=v_ln3_b)

    mx, my, mc = lax.axis_index("x"), lax.axis_index("y"), lax.axis_index("c")
    me = 4 * mx + 2 * my + mc
    n_seq, seq, _ = x.shape
    t_tok = n_seq * seq

    cw_rows = _pad_rows(conv_w.reshape(DEPTH * 3, D_CONV // N_DEV), 8)
    cw_all = _allgather_small(jnp.pad(cw_rows, ((0, 0), (0, 128 - D_CONV // N_DEV))))
    conv_w_full = jnp.transpose(cw_all[:, :DEPTH * 3, :D_CONV // N_DEV], (1, 0, 2)).reshape(DEPTH, 3, D_CONV)
    small = {name: w[name] for name in SMALL_NAMES}
    small["conv_w"] = conv_w_full

    comm = _Overlap(w, cw_all, me, mc.reshape(1).astype(jnp.int32))
    loss_dev, grad_x, small_grads = _local_step(
        x.reshape(t_tok, D_MODEL), loss_target.reshape(t_tok, D_MODEL), comm, small, n_seq)
    loss = lax.psum(loss_dev[0, 0], ("x", "y", "c"))
    out = comm.finish((2 * mx + my).reshape(1).astype(jnp.int32), w, m, v)

    sg = {name: jnp.stack([small_grads[l][name] for l in range(DEPTH)]) for name in SMALL_NAMES + ("conv_w",)}
    all_small = _allgather_small(_pack_small(sg))

    def widen(a):
        return lax.dynamic_update_slice(jnp.zeros((DEPTH, 3, D_CONV), F32), a, (0, 0, me * (D_CONV // N_DEV)))

    packed = [_pack_small({**{name: t[name] for name in SMALL_NAMES}, "conv_w": widen(t["conv_w"])}) for t in (w, m, v)]
    small_out = [_unpack_small(a) for a in _adamw_small(all_small, *packed)]
    for name in SMALL_NAMES:
        out[name] = [small_out[k][name] for k in range(4)]
    out["conv_w"] = [lax.dynamic_slice(small_out[k]["conv_w"], (0, 0, me * (D_CONV // N_DEV)), (DEPTH, 3, D_CONV // N_DEV))
                     for k in range(4)]

    return (loss, grad_x.reshape(x.shape), *[out[name][0] for name in WEIGHT_ORDER], *[out[name][1] for name in WEIGHT_ORDER],
            *[out[name][2] for name in WEIGHT_ORDER], *[out[name][3] for name in WEIGHT_ORDER])
```

```python
import functools

import jax
import jax.numpy as jnp
from jax import lax
from jax.experimental import pallas as pl
from jax.experimental.pallas import tpu as pltpu

F32 = jnp.float32
BF16 = jnp.bfloat16
MESH = pl.DeviceIdType.MESH

N_DEV = 8
DEPTH = 2
D_MODEL = 1024
D_FF = 2816
FFN_BLK = 2 * D_FF // N_DEV
N_FFN_CHUNK = D_FF // FFN_BLK
D_CONV = 256
D_FOX = 512
N_HEADS = 8
D_SGU = 256
N_SGU_GROUPS = 4
SGU_CHUNK = 128
D_IN = 3 * D_CONV + 3 * D_FOX + N_HEADS + 2 * D_SGU
D_IN_SHARD = D_IN // N_DEV
COL_CONV, COL_QKV, COL_SGU, COL_F = 0, 768, 2304, 2816
D_IN_PAD = 2944
F_ORIG = 3 * D_CONV + 3 * D_FOX
ALPHA = (2 * DEPTH) ** 0.25
LN_EPS = 1e-5
ATT_SCALE = 0.125
ATT_BLK = 512
NEG = -1e30

ADAM_LR, ADAM_B1, ADAM_B2, ADAM_EPS, ADAM_WD, ADAM_STEP = 0.001, 0.9, 0.999, 1e-08, 0.01, 10

VMEM_BYTES_V7X = 64 * 1024 * 1024
HIGHEST = lax.Precision.HIGHEST


def _params(vmem_mb, sem=None):
    assert vmem_mb * 1024 * 1024 < VMEM_BYTES_V7X
    kw = dict(vmem_limit_bytes=vmem_mb * 1024 * 1024)
    if sem is not None:
        kw["dimension_semantics"] = sem
    return pltpu.CompilerParams(**kw)


def _dot(a, b, precision=None):
    return lax.dot_general(a, b, (((1,), (0,)), ((), ())), preferred_element_type=F32, precision=precision)


def _dot_nt(a, b):
    return lax.dot_general(a, b, (((1,), (1,)), ((), ())), preferred_element_type=F32)


def _dot_tn(a, b):
    return lax.dot_general(a, b, (((0,), (0,)), ((), ())), preferred_element_type=F32)


def _ln_stats(z):
    mu = jnp.mean(z, axis=-1, keepdims=True)
    zc = z - mu
    var = jnp.mean(zc * zc, axis=-1, keepdims=True)
    rstd = lax.rsqrt(var + LN_EPS)
    return zc * rstd, rstd


def _ln_bwd(dy, xhat, rstd, g):
    dxh = dy * g
    m1 = jnp.mean(dxh, axis=-1, keepdims=True)
    m2 = jnp.mean(dxh * xhat, axis=-1, keepdims=True)
    return rstd * (dxh - m1 - xhat * m2)


_GELU_C = 0.7978845608028654


def _gelu(x):
    return 0.5 * x * (1.0 + jnp.tanh(_GELU_C * (x + 0.044715 * x * x * x)))


def _gelu_grad(x):
    t = jnp.tanh(_GELU_C * (x + 0.044715 * x * x * x))
    return 0.5 * (1.0 + t) + 0.5 * x * (1.0 - t * t) * _GELU_C * (1.0 + 3 * 0.044715 * x * x)


def _vspec():
    return pl.BlockSpec(memory_space=pltpu.VMEM)


def _anyspec():
    return pl.BlockSpec(memory_space=pl.ANY)


def _mesh_pos():
    return lax.axis_index("x"), lax.axis_index("y"), lax.axis_index("c")


def _other_chips(x, y):
    return [(1 - x, y), (x, 1 - y), (1 - x, 1 - y)]


_HBM_SPEC = pl.BlockSpec(memory_space=pltpu.HBM)
_SEM_SPEC = pl.BlockSpec(memory_space=pltpu.SEMAPHORE)
_DATAFLOW_EFFECT = pltpu.SideEffectType.DATAFLOW_SIDE_EFFECTING


def _remote_copies(plan, refs, send_sems, recv_sems):
    return [pltpu.make_async_remote_copy(src_ref=src, dst_ref=dst, send_sem=send_sems.at[k], recv_sem=recv_sems.at[k],
                                         device_id=to, device_id_type=MESH)
            for k, (src, dst, to) in enumerate(plan(refs, *_mesh_pos()))]


def _exchange_start(name, plan, n_copies, arrays, after):
    n = len(arrays)

    def body(*refs):
        send_sems, recv_sems, token = refs[n + 1], refs[n + 2], refs[-1]
        for cp in _remote_copies(plan, refs[:n], send_sems, recv_sems):
            cp.start()
        token[...] = jnp.zeros_like(token)

    out = pl.pallas_call(
        body, name=name,
        out_shape=(pltpu.SemaphoreType.DMA((n_copies,)), pltpu.SemaphoreType.DMA((n_copies,)),
                   *[pltpu.HBM(a.shape, a.dtype) for a in arrays], jax.ShapeDtypeStruct((8, 128), F32)),
        in_specs=[_HBM_SPEC] * n + [_anyspec()],
        out_specs=(_SEM_SPEC, _SEM_SPEC, *[_HBM_SPEC] * n, _vspec()),
        input_output_aliases={i: 2 + i for i in range(n)},
        compiler_params=pltpu.CompilerParams(has_side_effects=_DATAFLOW_EFFECT),
    )(*[pltpu.with_memory_space_constraint(a, pltpu.HBM) for a in arrays], after)
    return out[0], out[1], list(out[2:2 + n]), out[-1]


def _exchange_wait(name, plan, n_copies, started, after):
    send_sems, recv_sems, arrays, _ = started
    n = len(arrays)

    def body(*refs):
        for cp in _remote_copies(plan, refs[:n], refs[n], refs[n + 1]):
            cp.wait_send()
            cp.wait_recv()

    out = pl.pallas_call(
        body, name=name,
        out_shape=tuple(pltpu.HBM(a.shape, a.dtype) for a in arrays),
        in_specs=[_HBM_SPEC] * n + [_SEM_SPEC, _SEM_SPEC, _anyspec()], out_specs=tuple([_HBM_SPEC] * n),
        input_output_aliases={i: i for i in range(n)},
        compiler_params=pltpu.CompilerParams(has_side_effects=_DATAFLOW_EFFECT),
    )(*arrays, send_sems, recv_sems, after)
    return list(out)


def _gather_plan(m):
    def plan(refs, x, y, c):
        me = 4 * x + 2 * y + c
        return [(refs[i], refs[m + i].at[me], (*chip, c)) for i in range(m) for chip in _other_chips(x, y)]
    return plan


def _gather_finish(shards, lands):
    m = len(shards)

    def body(*refs):
        ins, outs = refs[:m], refs[2 * m:3 * m]
        send_sems, recv_sems = refs[3 * m:]
        x, y, c = _mesh_pos()
        me = 4 * x + 2 * y + c
        sibling = (x, y, 1 - c)
        copies = []
        for i in range(m):
            blocks = [(ins[i], me)] + [(outs[i].at[4 * cx + 2 * cy + c], 4 * cx + 2 * cy + c) for cx, cy in _other_chips(x, y)]
            for k, (src, blk) in enumerate(blocks):
                copies.append(pltpu.make_async_remote_copy(
                    src_ref=src, dst_ref=outs[i].at[blk], send_sem=send_sems.at[4 * i + k], recv_sem=recv_sems.at[4 * i + k],
                    device_id=sibling, device_id_type=MESH))
        for cp in copies:
            cp.start()
        for cp in copies:
            cp.wait_send()
        for i in range(m):
            for k, (cx, cy) in enumerate([(x, y)] + _other_chips(x, y)):
                blk = 4 * cx + 2 * cy + (1 - c)
                pltpu.make_async_remote_copy(
                    src_ref=outs[i].at[blk], dst_ref=outs[i].at[blk], send_sem=send_sems.at[4 * i + k],
                    recv_sem=recv_sems.at[4 * i + k], device_id=sibling, device_id_type=MESH).wait_recv()

    return pl.pallas_call(
        body, name="allgather_finish",
        out_shape=[jax.ShapeDtypeStruct(a.shape, a.dtype) for a in lands],
        in_specs=[_anyspec()] * (2 * m), out_specs=[_anyspec()] * m,
        input_output_aliases={m + i: i for i in range(m)},
        scratch_shapes=[pltpu.SemaphoreType.DMA((4 * m,)), pltpu.SemaphoreType.DMA((4 * m,))],
    )(*shards, *lands)


def _allgather_small(v):
    rows = v.shape[0]

    def body(v_ref, out_ref, send_sems, recv_sems):
        x, y, c = _mesh_pos()
        me = 4 * x + 2 * y + c
        out_ref[me] = v_ref[...]
        rel = [(dx, dy, dc) for dx in (0, 1) for dy in (0, 1) for dc in (0, 1)][1:]
        copies = []
        for k, (dx, dy, dc) in enumerate(rel):
            to = (x ^ dx, y ^ dy, c ^ dc)
            copies.append(pltpu.make_async_remote_copy(
                src_ref=v_ref, dst_ref=out_ref.at[me], send_sem=send_sems.at[k], recv_sem=recv_sems.at[k],
                device_id=to, device_id_type=MESH))
        for cp in copies:
            cp.start()
        for k, (dx, dy, dc) in enumerate(rel):
            src_blk = 4 * (x ^ dx) + 2 * (y ^ dy) + (c ^ dc)
            pltpu.make_async_remote_copy(
                src_ref=v_ref, dst_ref=out_ref.at[src_blk], send_sem=send_sems.at[k], recv_sem=recv_sems.at[k],
                device_id=(x, y, c), device_id_type=MESH).wait_recv()
        for cp in copies:
            cp.wait_send()

    return pl.pallas_call(
        body, name="allgather_small",
        out_shape=jax.ShapeDtypeStruct((N_DEV, rows, 128), v.dtype),
        in_specs=[_vspec()], out_specs=_vspec(),
        scratch_shapes=[pltpu.SemaphoreType.DMA((7,)), pltpu.SemaphoreType.DMA((7,))],
        compiler_params=_params(24),
    )(v)


def _sibling_plan(n):
    def plan(refs, x, y, c):
        return [(refs[a].at[2 * q + (1 - c)], refs[n + a].at[q], (x, y, 1 - c)) for a in range(n) for q in range(4)]
    return plan


def _chip_plan(n):
    def plan(refs, x, y, c):
        return [(refs[a].at[2 * cx + cy], refs[n + a].at[j], (cx, cy, c))
                for a in range(n) for j, (cx, cy) in enumerate(_other_chips(x, y))]
    return plan


def _row_tile(rows, cols, budget_bytes=2 * 1024 * 1024):
    best = 8
    for t in range(8, rows + 1, 8):
        if rows % t == 0 and t * cols * 4 <= budget_bytes:
            best = t
    return best


def _chip_partial(g, recv, core):
    _, rows, cols = g.shape
    tr = _row_tile(rows, cols)

    def body(core_ref, g_ref, r_ref, o32_ref, o16_ref):
        s = g_ref[...] + r_ref[...]
        o32_ref[...] = s
        o16_ref[...] = s.astype(BF16)

    blk = (None, tr, cols)
    return pl.pallas_call(
        body, name="rs_chip_partial",
        grid_spec=pltpu.PrefetchScalarGridSpec(
            num_scalar_prefetch=1, grid=(4, rows // tr),
            in_specs=[pl.BlockSpec(blk, lambda q, i, c: (2 * q + c[0], i, 0)),
                      pl.BlockSpec(blk, lambda q, i, c: (q, i, 0))],
            out_specs=[pl.BlockSpec(blk, lambda q, i, c: (q, i, 0))] * 2),
        out_shape=[jax.ShapeDtypeStruct((4, rows, cols), F32), jax.ShapeDtypeStruct((4, rows, cols), BF16)],
        compiler_params=_params(32),
    )(core, g, recv)


def _adam_math(w, g, m, v):
    m = ADAM_B1 * m + (1.0 - ADAM_B1) * g
    v = ADAM_B2 * v + (1.0 - ADAM_B2) * (g * g)
    m_hat = m / (1.0 - ADAM_B1 ** ADAM_STEP)
    v_hat = v / (1.0 - ADAM_B2 ** ADAM_STEP)
    delta = -ADAM_LR * (m_hat / (jnp.sqrt(v_hat) + ADAM_EPS) + ADAM_WD * w)
    return delta, m, v


def _adamw_shard(part32, recv16, slot, w, m, v, layer, earlier):
    depth, rows, cols = w.shape
    tr = _row_tile(rows, cols, 1024 * 1024)
    n_prev = 0 if earlier is None else 4

    def body(slot_ref, p_ref, r_ref, w_ref, m_ref, v_ref, *rest):
        g_out, d_out, m_out, v_out = rest[n_prev:]
        g = p_ref[...] + r_ref[0].astype(F32) + r_ref[1].astype(F32) + r_ref[2].astype(F32)
        d, mn, vn = _adam_math(w_ref[...], g, m_ref[...], v_ref[...])
        g_out[...] = g
        d_out[...] = d
        m_out[...] = mn
        v_out[...] = vn

    mine = pl.BlockSpec((None, tr, cols), lambda i, s: (layer, i, 0))
    return pl.pallas_call(
        body, name="adamw_shard",
        grid_spec=pltpu.PrefetchScalarGridSpec(
            num_scalar_prefetch=1, grid=(rows // tr,),
            in_specs=[pl.BlockSpec((None, tr, cols), lambda i, s: (s[0], i, 0)),
                      pl.BlockSpec((3, tr, cols), lambda i, s: (0, i, 0)), mine, mine, mine] + [_anyspec()] * n_prev,
            out_specs=[mine] * 4),
        out_shape=[jax.ShapeDtypeStruct((depth, rows, cols), F32)] * 4,
        input_output_aliases={6 + k: k for k in range(n_prev)},
        compiler_params=_params(32),
    )(slot, part32, recv16, w, m, v, *([] if earlier is None else earlier))


def _adamw_small(gathered, w, m, v):
    rows = w.shape[0]

    def body(a_ref, w_ref, m_ref, v_ref, g_out, d_out, m_out, v_out):
        g = a_ref[0]
        for d in range(1, N_DEV):
            g = g + a_ref[d]
        dl, mn, vn = _adam_math(w_ref[...], g, m_ref[...], v_ref[...])
        g_out[...] = g
        d_out[...] = dl
        m_out[...] = mn
        v_out[...] = vn

    return pl.pallas_call(
        body, name="adamw_small",
        in_specs=[_vspec()] * 4, out_specs=[_vspec()] * 4,
        out_shape=[jax.ShapeDtypeStruct((rows, 128), F32)] * 4,
        compiler_params=_params(32),
    )(gathered, w, m, v)


def _load_weights_once(pairs, sems):
    @pl.when(pl.program_id(0) == 0)
    def _():
        cps = [pltpu.make_async_copy(src, dst, sems.at[i]) for i, (src, dst) in enumerate(pairs)]
        for cp in cps:
            cp.start()
        for cp in cps:
            cp.wait()


def _ffn_fwd(x, wup, wd, ln_g, ln_b, tm=512):
    t_tok = x.shape[0]

    def body(x_ref, g_ref, b_ref, wup_hbm, wd_hbm, xn_ref, xnb_ref, z_ref, gu_ref, xb_ref, wup_v, wd_v, sems):
        _load_weights_once([(wup_hbm, wup_v), (wd_hbm, wd_v)], sems)
        xb = x_ref[...].astype(BF16)
        xb_ref[...] = xb
        y = None
        for j in range(N_FFN_CHUNK):
            g = _dot(xb, wup_v[j])
            u = _dot(xb, wup_v[N_FFN_CHUNK + j])
            gu_ref[0, j] = g.astype(BF16)
            gu_ref[1, j] = u.astype(BF16)
            a = (g * jax.nn.sigmoid(g) * u).astype(BF16)
            part = _dot(a, wd_v[j])
            y = part if y is None else y + part
        z = ALPHA * x_ref[...] + 0.5 * y
        xhat, _ = _ln_stats(z)
        xn = xhat * g_ref[...] + b_ref[...]
        z_ref[...] = z
        xn_ref[...] = xn
        xnb_ref[...] = xn.astype(BF16)

    tok = pl.BlockSpec((tm, D_MODEL), lambda i: (i, 0))
    vec = pl.BlockSpec((1, D_MODEL), lambda i: (0, 0))
    return pl.pallas_call(
        body, name="ffn_fwd", grid=(t_tok // tm,),
        in_specs=[tok, vec, vec, _anyspec(), _anyspec()],
        out_specs=[tok, tok, tok, pl.BlockSpec((2, N_FFN_CHUNK, tm, FFN_BLK), lambda i: (0, 0, i, 0)), tok],
        out_shape=[jax.ShapeDtypeStruct((t_tok, D_MODEL), F32), jax.ShapeDtypeStruct((t_tok, D_MODEL), BF16),
                   jax.ShapeDtypeStruct((t_tok, D_MODEL), F32),
                   jax.ShapeDtypeStruct((2, N_FFN_CHUNK, t_tok, FFN_BLK), BF16),
                   jax.ShapeDtypeStruct((t_tok, D_MODEL), BF16)],
        scratch_shapes=[pltpu.VMEM((N_DEV, D_MODEL, FFN_BLK), BF16), pltpu.VMEM((N_FFN_CHUNK, FFN_BLK, D_MODEL), BF16),
                        pltpu.SemaphoreType.DMA((2,))],
        compiler_params=_params(62, ("arbitrary",)),
    )(x, ln_g, ln_b, wup, wd)


def _ffn_bwd(dxn, z, gu, wup, wd, ln_g, after, tm=256):
    t_tok = dxn.shape[0]

    def body(dxn_ref, z_ref, gu_ref, g_ref, wup_hbm, wd_hbm, _after,
             dx_ref, dy_ref, a_ref, dgu_ref, dg_ref, db_ref, wup_v, wd_v, sems):
        i = pl.program_id(0)
        _load_weights_once([(wup_hbm, wup_v), (wd_hbm, wd_v)], sems)
        dxn_t = dxn_ref[...]
        xhat, rstd = _ln_stats(z_ref[...])
        pg = jnp.sum(dxn_t * xhat, axis=0, keepdims=True)
        pb = jnp.sum(dxn_t, axis=0, keepdims=True)

        @pl.when(i == 0)
        def _():
            dg_ref[...] = pg
            db_ref[...] = pb

        @pl.when(i > 0)
        def _():
            dg_ref[...] += pg
            db_ref[...] += pb

        dz = _ln_bwd(dxn_t, xhat, rstd, g_ref[...])
        dy = (0.5 * dz).astype(BF16)
        dy_ref[...] = dy
        dx = ALPHA * dz
        for j in range(N_FFN_CHUNK):
            da = _dot_nt(dy, wd_v[j])
            g = gu_ref[0, j].astype(F32)
            u = gu_ref[1, j].astype(F32)
            sig = jax.nn.sigmoid(g)
            silu = g * sig
            a_ref[j] = (silu * u).astype(BF16)
            dg = (da * u * (sig * (1.0 + g * (1.0 - sig)))).astype(BF16)
            du = (da * silu).astype(BF16)
            dgu_ref[0, j] = dg
            dgu_ref[1, j] = du
            dx = dx + _dot_nt(dg, wup_v[j]) + _dot_nt(du, wup_v[N_FFN_CHUNK + j])
        dx_ref[...] = dx

    tok = pl.BlockSpec((tm, D_MODEL), lambda i: (i, 0))
    vec = pl.BlockSpec((1, D_MODEL), lambda i: (0, 0))
    gu_spec = pl.BlockSpec((2, N_FFN_CHUNK, tm, FFN_BLK), lambda i: (0, 0, i, 0))
    return pl.pallas_call(
        body, name="ffn_bwd", grid=(t_tok // tm,),
        in_specs=[tok, tok, gu_spec, vec, _anyspec(), _anyspec(), _anyspec()],
        out_specs=[tok, tok, pl.BlockSpec((N_FFN_CHUNK, tm, FFN_BLK), lambda i: (0, i, 0)), gu_spec, vec, vec],
        out_shape=[jax.ShapeDtypeStruct((t_tok, D_MODEL), F32), jax.ShapeDtypeStruct((t_tok, D_MODEL), BF16),
                   jax.ShapeDtypeStruct((N_FFN_CHUNK, t_tok, FFN_BLK), BF16),
                   jax.ShapeDtypeStruct((2, N_FFN_CHUNK, t_tok, FFN_BLK), BF16),
                   jax.ShapeDtypeStruct((1, D_MODEL), F32), jax.ShapeDtypeStruct((1, D_MODEL), F32)],
        scratch_shapes=[pltpu.VMEM((N_DEV, D_MODEL, FFN_BLK), BF16), pltpu.VMEM((N_FFN_CHUNK, FFN_BLK, D_MODEL), BF16),
                        pltpu.SemaphoreType.DMA((2,))],
        compiler_params=_params(60, ("arbitrary",)),
    )(dxn, z, gu, ln_g, wup, wd, after)


def _matmul_tn(a, b, after, tk=1024):
    ga, t_tok, m = a.shape
    gb, _, n = b.shape
    groups = max(ga, gb)

    def body(a_ref, b_ref, _after, o_ref):
        p = _dot_tn(a_ref[...].astype(BF16), b_ref[...].astype(BF16))

        @pl.when(pl.program_id(1) == 0)
        def _():
            o_ref[...] = p

        @pl.when(pl.program_id(1) > 0)
        def _():
            o_ref[...] += p

    return pl.pallas_call(
        body, name=f"matmul_tn_{m}x{n}", grid=(groups, t_tok // tk),
        in_specs=[pl.BlockSpec((None, tk, m), (lambda g, t: (g, t, 0)) if ga > 1 else (lambda g, t: (0, t, 0))),
                  pl.BlockSpec((None, tk, n), (lambda g, t: (g, t, 0)) if gb > 1 else (lambda g, t: (0, t, 0))),
                  _anyspec()],
        out_specs=pl.BlockSpec((None, m, n), lambda g, t: (g, 0, 0)),
        out_shape=jax.ShapeDtypeStruct((groups, m, n), F32),
        compiler_params=_params(56, ("arbitrary", "arbitrary")),
    )(a, b, after)


def _in_proj(x, w_in, tm=512):
    t_tok = x.shape[0]

    def body(x_ref, w_ref, conv_ref, qkv_ref, sgu_ref, f_ref):
        xb = x_ref[...].astype(BF16)
        conv_ref[...] = _dot(xb, w_ref[:, COL_CONV:COL_QKV])
        qkv_ref[...] = _dot(xb, w_ref[:, COL_QKV:COL_SGU]).astype(BF16)
        sgu_ref[...] = _dot(xb, w_ref[:, COL_SGU:COL_F])
        f_ref[...] = _dot(xb, w_ref[:, COL_F:D_IN_PAD])

    def tok(n):
        return pl.BlockSpec((tm, n), lambda i: (i, 0))

    return pl.pallas_call(
        body, name="mix_in_proj", grid=(t_tok // tm,),
        in_specs=[tok(D_MODEL), pl.BlockSpec((D_MODEL, D_IN_PAD), lambda i: (0, 0))],
        out_specs=[tok(768), tok(1536), tok(512), tok(128)],
        out_shape=[jax.ShapeDtypeStruct((t_tok, 768), F32), jax.ShapeDtypeStruct((t_tok, 1536), BF16),
                   jax.ShapeDtypeStruct((t_tok, 512), F32), jax.ShapeDtypeStruct((t_tok, 128), F32)],
        compiler_params=_params(48, ("arbitrary",)),
    )(x, w_in)


def _shift_down(a, k):
    row = lax.broadcasted_iota(jnp.int32, a.shape, 0)
    return jnp.where(row >= k, pltpu.roll(a, k, 0), 0.0)


def _shift_up(a, k):
    rows = a.shape[0]
    row = lax.broadcasted_iota(jnp.int32, a.shape, 0)
    return jnp.where(row < rows - k, pltpu.roll(a, rows - k, 0), 0.0)


def _tril(n):
    return lax.broadcasted_iota(jnp.int32, (n, n), 0) >= lax.broadcasted_iota(jnp.int32, (n, n), 1)


def _sgu_group_of_lane():
    return lax.broadcasted_iota(jnp.int32, (1, D_SGU), 1) // (D_SGU // N_SGU_GROUPS)


def _log_sigmoid(x):
    return jnp.minimum(x, 0.0) - jnp.log1p(jnp.exp(-jnp.abs(x)))


def _mix_mid_fwd(conv, sgu, f, conv_w, b_f, sgu_g, sgu_b, w_s, b_mat, n_seq):
    t_tok = conv.shape[0]
    seq = t_tok // n_seq
    n_chunk = seq // SGU_CHUNK
    per_blk = ATT_BLK // SGU_CHUNK

    def body(conv_ref, sgu_ref, f_ref, cw_ref, bf_ref, lg_ref, lb_ref, ws_ref, bm_ref, ya_ref, yc_ref, cum_ref):
        z = conv_ref[:, 256:512] * conv_ref[:, 512:768]
        y = cw_ref[0:1, :] * _shift_down(z, 2) + cw_ref[1:2, :] * _shift_down(z, 1) + cw_ref[2:3, :] * z
        ya_ref[...] = (conv_ref[:, 0:256] * y).astype(BF16)

        tril = _tril(SGU_CHUNK)
        grp = _sgu_group_of_lane()
        wc = [jnp.where(tril, ws_ref[g], 0.0).astype(BF16) for g in range(N_SGU_GROUPS)]
        tri_f = tril.astype(F32)
        carry = jnp.zeros((1, 128), F32)
        for n in range(n_chunk):
            rows = pl.ds(n * SGU_CHUNK, SGU_CHUNK)
            u = _gelu(sgu_ref[rows, 0:256])
            vhat, _ = _ln_stats(_gelu(sgu_ref[rows, 256:512]))
            vn = (vhat * lg_ref[...] + lb_ref[...]).astype(BF16)
            mixed = bm_ref[...]
            for g in range(N_SGU_GROUPS):
                mixed = mixed + jnp.where(grp == g, _dot(wc[g], vn), 0.0)
            yc_ref[rows, :] = (u * mixed).astype(BF16)

            log_f = _log_sigmoid(f_ref[rows, :] + bf_ref[...])
            cs = _dot(tri_f, log_f, HIGHEST) + carry
            carry = cs[SGU_CHUNK - 1:SGU_CHUNK, :]
            cs_t = cs.T
            lanes = pl.ds((n % per_blk) * SGU_CHUNK, SGU_CHUNK)
            for h in range(N_HEADS):
                cum_ref[h, n // per_blk, :, lanes] = cs_t[h:h + 1, :]

    def seq_blk(n):
        return pl.BlockSpec((seq, n), lambda b: (b, 0))

    def full(shape):
        return pl.BlockSpec(shape, lambda b: (0,) * len(shape))

    return pl.pallas_call(
        body, name="mix_mid_fwd", grid=(n_seq,),
        in_specs=[seq_blk(768), seq_blk(512), seq_blk(128), full((8, 256)), full((1, 128)), full((1, 256)),
                  full((1, 256)), full((4, 128, 128)), full((128, 256))],
        out_specs=[seq_blk(256), seq_blk(256),
                   pl.BlockSpec((N_HEADS, seq // ATT_BLK, 1, ATT_BLK), lambda b: (b, 0, 0, 0))],
        out_shape=[jax.ShapeDtypeStruct((t_tok, 256), BF16), jax.ShapeDtypeStruct((t_tok, 256), BF16),
                   jax.ShapeDtypeStruct((n_seq * N_HEADS, seq // ATT_BLK, 1, ATT_BLK), F32)],
        compiler_params=_params(48, ("arbitrary",)),
    )(conv, sgu, f, conv_w, b_f, sgu_g, sgu_b, w_s, b_mat)


def _head_masks():
    lane = lax.broadcasted_iota(jnp.int32, (1, 128), 1)
    return lane < 64, lane


def _fox_fwd(qkv, cum_t, n_seq):
    t_tok = qkv.shape[0]
    seq = t_tok // n_seq
    nq = seq // ATT_BLK
    blk = ATT_BLK

    def body(q_ref, k_ref, v_ref, c0_ref, c1_ref, o_ref, lse_ref):
        qi = pl.program_id(2)
        first, _ = _head_masks()
        qs = q_ref[...] * ATT_SCALE
        zero = jnp.zeros_like(qs)
        q0 = jnp.where(first, qs, zero)
        q1 = jnp.where(first, zero, qs)
        causal = _tril(blk)
        one = jnp.ones((1, 128), BF16)

        def step(kb, carry, masked):
            m0, m1, acc0, acc1 = carry
            rows = pl.ds(pl.multiple_of(kb * blk, blk), blk)
            k = k_ref[rows, :]
            v = v_ref[rows, :]

            def head(qh, c_ref, m, acc, vh):
                s = _dot_nt(qh, k) - c_ref[kb]
                if masked:
                    s = jnp.where(causal, s, NEG)
                m_new = jnp.maximum(m, jnp.max(s, axis=1, keepdims=True))
                p = jnp.exp(s - m_new)
                return m_new, acc * jnp.exp(m - m_new) + _dot(p.astype(BF16), vh)

            m0, acc0 = head(q0, c0_ref, m0, acc0, jnp.where(first, v, one))
            m1, acc1 = head(q1, c1_ref, m1, acc1, jnp.where(first, one, v))
            return m0, m1, acc0, acc1

        col = jnp.full((blk, 1), NEG, F32)
        zacc = jnp.zeros((blk, 128), F32)
        carry = lax.fori_loop(0, qi, lambda kb, cr: step(kb, cr, False), (col, col, zacc, zacc))
        m0, m1, acc0, acc1 = step(qi, carry, True)
        l0 = pltpu.roll(acc0, 64, 1)
        l1 = pltpu.roll(acc1, 64, 1)
        o_ref[...] = jnp.where(first, acc0 / l0, acc1 / l1).astype(BF16)
        lse_ref[...] = jnp.where(first, m0 + jnp.log(l0), m1 + jnp.log(l1))

    cum_spec0 = pl.BlockSpec((None, nq, 1, blk), lambda b, hp, qi: (b * N_HEADS + 2 * hp, 0, 0, 0))
    cum_spec1 = pl.BlockSpec((None, nq, 1, blk), lambda b, hp, qi: (b * N_HEADS + 2 * hp + 1, 0, 0, 0))
    out_spec = pl.BlockSpec((blk, 128), lambda b, hp, qi: (b * nq + qi, hp))
    return pl.pallas_call(
        body, name="fox_fwd", grid=(n_seq, 4, nq),
        in_specs=[pl.BlockSpec((blk, 128), lambda b, hp, qi: (b * nq + qi, hp)),
                  pl.BlockSpec((seq, 128), lambda b, hp, qi: (b, 4 + hp)),
                  pl.BlockSpec((seq, 128), lambda b, hp, qi: (b, 8 + hp)), cum_spec0, cum_spec1],
        out_specs=[out_spec, out_spec],
        out_shape=[jax.ShapeDtypeStruct((t_tok, D_FOX), BF16), jax.ShapeDtypeStruct((t_tok, D_FOX), F32)],
        compiler_params=_params(32, ("arbitrary", "arbitrary", "arbitrary")),
    )(qkv, qkv, qkv, cum_t, cum_t)


def _fox_bwd(qkv, cum_t, o, lse, d_o, n_seq):
    t_tok = qkv.shape[0]
    seq = t_tok // n_seq
    nk = seq // ATT_BLK
    blk = ATT_BLK

    def body(q_ref, k_ref, v_ref, c0_ref, c1_ref, o_ref, lse_ref, do_ref,
             dq_ref, dk_ref, dv_ref, drow_ref, dcol_ref):
        kb = pl.program_id(2)
        first, lane = _head_masks()
        second = jnp.logical_not(first)
        k = k_ref[...]
        v = v_ref[...]
        zero = jnp.zeros_like(k)
        one = jnp.ones((1, 128), BF16)
        ks = k * ATT_SCALE
        causal = _tril(blk)

        @pl.when(kb == 0)
        def _():
            dq_ref[...] = jnp.zeros_like(dq_ref)
            drow_ref[...] = jnp.zeros_like(drow_ref)

        def step(qi, carry, masked):
            rows = pl.ds(pl.multiple_of(qi * blk, blk), blk)
            qs = q_ref[rows, :] * ATT_SCALE
            d_o = do_ref[rows, :]
            dd = d_o.astype(F32) * o_ref[rows, :].astype(F32)
            lse_t = lse_ref[rows, :]

            def head(mine, c, lse_lane, dk, dv):
                qh = jnp.where(mine, qs, zero)
                doh = jnp.where(mine, d_o, zero)
                delta = jnp.sum(jnp.where(mine, dd, 0.0), axis=1, keepdims=True)
                lse_h = jnp.sum(jnp.where(lane == lse_lane, lse_t, 0.0), axis=1, keepdims=True)
                s = _dot_nt(qh, k) - c
                if masked:
                    s = jnp.where(causal, s, NEG)
                p = jnp.exp(s - lse_h)
                ds = (p * (_dot_nt(doh, v) - delta)).astype(BF16)
                dk = dk + _dot_tn(ds, jnp.where(mine, qs, one))
                dv = dv + _dot_tn(p.astype(BF16), doh)
                return dk, dv, _dot(ds, jnp.where(mine, ks, one))

            dk0, dv0, dk1, dv1 = carry
            dk0, dv0, dq0 = head(first, c0_ref[...], 0, dk0, dv0)
            dk1, dv1, dq1 = head(second, c1_ref[...], 64, dk1, dv1)
            dq_ref[rows, :] += jnp.where(first, dq0, dq1)
            drow_ref[rows, :] += jnp.where(first, dq1, dq0)
            return dk0, dv0, dk1, dv1

        zt = jnp.zeros((blk, 128), F32)
        carry = step(kb, (zt, zt, zt, zt), True)
        dk0, dv0, dk1, dv1 = lax.fori_loop(kb + 1, nk, lambda qi, cr: step(qi, cr, False), carry)
        dk_ref[...] = jnp.where(first, dk0, dk1).astype(BF16)
        dcol_ref[...] = jnp.where(first, dk1, dk0)
        dv_ref[...] = (dv0 + dv1).astype(BF16)

    def seq_spec(col0):
        return pl.BlockSpec((seq, 128), lambda b, hp, kb: (b, col0 + hp))

    def key_spec(col0):
        return pl.BlockSpec((blk, 128), lambda b, hp, kb: (b * nk + kb, col0 + hp))

    def cum_spec(h):
        return pl.BlockSpec((None, None, 1, blk), lambda b, hp, kb: (b * N_HEADS + 2 * hp + h, kb, 0, 0))

    return pl.pallas_call(
        body, name="fox_bwd", grid=(n_seq, 4, nk),
        in_specs=[seq_spec(0), key_spec(4), key_spec(8), cum_spec(0), cum_spec(1), seq_spec(0), seq_spec(0), seq_spec(0)],
        out_specs=[seq_spec(0), key_spec(0), key_spec(0), seq_spec(0), key_spec(0)],
        out_shape=[jax.ShapeDtypeStruct((t_tok, D_FOX), F32), jax.ShapeDtypeStruct((t_tok, D_FOX), BF16),
                   jax.ShapeDtypeStruct((t_tok, D_FOX), BF16), jax.ShapeDtypeStruct((t_tok, D_FOX), F32),
                   jax.ShapeDtypeStruct((t_tok, D_FOX), F32)],
        compiler_params=_params(48, ("arbitrary", "arbitrary", "arbitrary")),
    )(qkv, qkv, qkv, cum_t, cum_t, o, lse, d_o)


def _mix_out_fwd(ya, yb, yc, x, w_out, ln_g, ln_b, tm=512):
    t_tok = x.shape[0]

    def body(ya_ref, yb_ref, yc_ref, x_ref, w_ref, g_ref, b_ref, xn_ref, xnb_ref, z_ref):
        mo = _dot(ya_ref[...], w_ref[0:256, :]) + _dot(yb_ref[...], w_ref[256:768, :]) + _dot(yc_ref[...], w_ref[768:1024, :])
        z = ALPHA * x_ref[...] + mo
        xhat, _ = _ln_stats(z)
        xn = xhat * g_ref[...] + b_ref[...]
        z_ref[...] = z
        xn_ref[...] = xn
        xnb_ref[...] = xn.astype(BF16)

    def tok(n):
        return pl.BlockSpec((tm, n), lambda i: (i, 0))

    vec = pl.BlockSpec((1, D_MODEL), lambda i: (0, 0))
    return pl.pallas_call(
        body, name="mix_out_fwd", grid=(t_tok // tm,),
        in_specs=[tok(256), tok(512), tok(256), tok(D_MODEL),
                  pl.BlockSpec((D_MODEL, D_MODEL), lambda i: (0, 0)), vec, vec],
        out_specs=[tok(D_MODEL)] * 3,
        out_shape=[jax.ShapeDtypeStruct((t_tok, D_MODEL), F32), jax.ShapeDtypeStruct((t_tok, D_MODEL), BF16),
                   jax.ShapeDtypeStruct((t_tok, D_MODEL), F32)],
        compiler_params=_params(40, ("arbitrary",)),
    )(ya, yb, yc, x, w_out, ln_g, ln_b)


def _mix_out_bwd(dxn, z, w_out, ln_g, tm=512):
    t_tok = dxn.shape[0]

    def body(dxn_ref, z_ref, w_ref, g_ref, dz_ref, dzb_ref, dya_ref, dyb_ref, dyc_ref, dg_ref, db_ref):
        i = pl.program_id(0)
        dxn_t = dxn_ref[...]
        xhat, rstd = _ln_stats(z_ref[...])
        pg = jnp.sum(dxn_t * xhat, axis=0, keepdims=True)
        pb = jnp.sum(dxn_t, axis=0, keepdims=True)

        @pl.when(i == 0)
        def _():
            dg_ref[...] = pg
            db_ref[...] = pb

        @pl.when(i > 0)
        def _():
            dg_ref[...] += pg
            db_ref[...] += pb

        dz = _ln_bwd(dxn_t, xhat, rstd, g_ref[...])
        dzb = dz.astype(BF16)
        dz_ref[...] = dz
        dzb_ref[...] = dzb
        dya_ref[...] = _dot_nt(dzb, w_ref[0:256, :])
        dyb_ref[...] = _dot_nt(dzb, w_ref[256:768, :]).astype(BF16)
        dyc_ref[...] = _dot_nt(dzb, w_ref[768:1024, :])

    def tok(n):
        return pl.BlockSpec((tm, n), lambda i: (i, 0))

    vec = pl.BlockSpec((1, D_MODEL), lambda i: (0, 0))
    return pl.pallas_call(
        body, name="mix_out_bwd", grid=(t_tok // tm,),
        in_specs=[tok(D_MODEL), tok(D_MODEL), pl.BlockSpec((D_MODEL, D_MODEL), lambda i: (0, 0)), vec],
        out_specs=[tok(D_MODEL), tok(D_MODEL), tok(256), tok(512), tok(256), vec, vec],
        out_shape=[jax.ShapeDtypeStruct((t_tok, D_MODEL), F32), jax.ShapeDtypeStruct((t_tok, D_MODEL), BF16),
                   jax.ShapeDtypeStruct((t_tok, 256), F32), jax.ShapeDtypeStruct((t_tok, 512), BF16),
                   jax.ShapeDtypeStruct((t_tok, 256), F32),
                   jax.ShapeDtypeStruct((1, D_MODEL), F32), jax.ShapeDtypeStruct((1, D_MODEL), F32)],
        compiler_params=_params(40, ("arbitrary",)),
    )(dxn, z, w_out, ln_g)


def _conv_bwd(conv, dya, conv_w, n_seq):
    t_tok = conv.shape[0]
    seq = t_tok // n_seq

    def body(conv_ref, dya_ref, cw_ref, dconv_ref, dcw_ref):
        @pl.when(pl.program_id(0) == 0)
        def _():
            dcw_ref[...] = jnp.zeros_like(dcw_ref)

        z = conv_ref[:, 256:512] * conv_ref[:, 512:768]
        z1 = _shift_down(z, 1)
        z2 = _shift_down(z, 2)
        y = cw_ref[0:1, :] * z2 + cw_ref[1:2, :] * z1 + cw_ref[2:3, :] * z
        dya_t = dya_ref[...]
        dconv_ref[:, 0:256] = (dya_t * y).astype(BF16)
        dy = dya_t * conv_ref[:, 0:256]
        dcw_ref[0:1, :] += jnp.sum(dy * z2, axis=0, keepdims=True)
        dcw_ref[1:2, :] += jnp.sum(dy * z1, axis=0, keepdims=True)
        dcw_ref[2:3, :] += jnp.sum(dy * z, axis=0, keepdims=True)
        dz = cw_ref[2:3, :] * dy + cw_ref[1:2, :] * _shift_up(dy, 1) + cw_ref[0:1, :] * _shift_up(dy, 2)
        dconv_ref[:, 256:512] = (dz * conv_ref[:, 512:768]).astype(BF16)
        dconv_ref[:, 512:768] = (dz * conv_ref[:, 256:512]).astype(BF16)

    def seq_blk(n):
        return pl.BlockSpec((seq, n), lambda b: (b, 0))

    par = pl.BlockSpec((8, 256), lambda b: (0, 0))
    return pl.pallas_call(
        body, name="conv_bwd", grid=(n_seq,),
        in_specs=[seq_blk(768), seq_blk(256), par], out_specs=[seq_blk(768), par],
        out_shape=[jax.ShapeDtypeStruct((t_tok, 768), BF16), jax.ShapeDtypeStruct((8, 256), F32)],
        compiler_params=_params(56, ("arbitrary",)),
    )(conv, dya, conv_w)


def _sgu_gate_bwd(sgu, f, dyc, drow, dcol, b_f, sgu_g, sgu_b, w_s, b_mat, n_seq):
    t_tok = sgu.shape[0]
    seq = t_tok // n_seq
    n_chunk = seq // SGU_CHUNK

    def body(sgu_ref, f_ref, dyc_ref, drow_ref, dcol_ref, bf_ref, lg_ref, lb_ref, ws_ref, bm_ref,
             dsgu_ref, df_ref, dbf_ref, dlg_ref, dlb_ref, dws_ref, dbs_ref, dbm_acc):
        b = pl.program_id(0)

        @pl.when(b == 0)
        def _():
            for r in (dbf_ref, dlg_ref, dlb_ref, dws_ref, dbm_acc):
                r[...] = jnp.zeros_like(r)

        tril = _tril(SGU_CHUNK)
        grp = _sgu_group_of_lane()
        wc = [jnp.where(tril, ws_ref[g], 0.0).astype(BF16) for g in range(N_SGU_GROUPS)]
        for n in range(n_chunk):
            rows = pl.ds(n * SGU_CHUNK, SGU_CHUNK)
            su = sgu_ref[rows, 0:256]
            sv = sgu_ref[rows, 256:512]
            u = _gelu(su)
            vhat, rstd = _ln_stats(_gelu(sv))
            vn = (vhat * lg_ref[...] + lb_ref[...]).astype(BF16)
            mixed = bm_ref[...]
            for g in range(N_SGU_GROUPS):
                mixed = mixed + jnp.where(grp == g, _dot(wc[g], vn), 0.0)
            dyc_t = dyc_ref[rows, :]
            dsgu_ref[rows, 0:256] = (dyc_t * mixed * _gelu_grad(su)).astype(BF16)
            dmixed = dyc_t * u
            dbm_acc[...] += dmixed
            dvn = jnp.zeros((SGU_CHUNK, D_SGU), F32)
            for g in range(N_SGU_GROUPS):
                dm_g = jnp.where(grp == g, dmixed, 0.0).astype(BF16)
                dws_ref[g] += _dot_nt(dm_g, vn)
                dvn = dvn + _dot_tn(wc[g], dm_g)
            dlg_ref[...] += jnp.sum(dvn * vhat, axis=0, keepdims=True)
            dlb_ref[...] += jnp.sum(dvn, axis=0, keepdims=True)
            dsgu_ref[rows, 256:512] = (_ln_bwd(dvn, vhat, rstd, lg_ref[...]) * _gelu_grad(sv)).astype(BF16)

        later = (lax.broadcasted_iota(jnp.int32, (128, 128), 0) <= lax.broadcasted_iota(jnp.int32, (128, 128), 1)).astype(F32)
        head = lax.broadcasted_iota(jnp.int32, (D_FOX, 128), 1)
        pick = (lax.broadcasted_iota(jnp.int32, (D_FOX, 128), 0) == 128 * (head // 2) + 64 * (1 - head % 2)).astype(F32)
        carry = jnp.zeros((1, 128), F32)
        for n in reversed(range(n_chunk)):
            rows = pl.ds(n * SGU_CHUNK, SGU_CHUNK)
            dcum_n = _dot(drow_ref[rows, :] - dcol_ref[rows, :], pick, HIGHEST)
            dlf = _dot(later, dcum_n, HIGHEST) + carry
            carry = carry + jnp.sum(dcum_n, axis=0, keepdims=True)
            df = dlf * jax.nn.sigmoid(-(f_ref[rows, :] + bf_ref[...]))
            df_ref[rows, :] = df.astype(BF16)
            dbf_ref[...] += jnp.sum(df, axis=0, keepdims=True)

        @pl.when(b == n_seq - 1)
        def _():
            for g in range(N_SGU_GROUPS):
                dws_ref[g] = jnp.where(tril, dws_ref[g], 0.0)
            sel = (lax.broadcasted_iota(jnp.int32, (D_SGU, 128), 0) // (D_SGU // N_SGU_GROUPS)
                   == lax.broadcasted_iota(jnp.int32, (D_SGU, 128), 1)).astype(F32)
            dbs_ref[...] = _dot(dbm_acc[...], sel, HIGHEST)

    def seq_blk(n):
        return pl.BlockSpec((seq, n), lambda b: (b, 0))

    def full(shape):
        return pl.BlockSpec(shape, lambda b: (0,) * len(shape))

    param_shapes = [(1, 128), (1, 256), (1, 256), (4, 128, 128), (128, 128)]
    return pl.pallas_call(
        body, name="sgu_gate_bwd", grid=(n_seq,),
        in_specs=[seq_blk(512), seq_blk(128), seq_blk(256), seq_blk(D_FOX), seq_blk(D_FOX),
                  full((1, 128)), full((1, 256)), full((1, 256)), full((4, 128, 128)), full((128, 256))],
        out_specs=[seq_blk(512), seq_blk(128)] + [full(s) for s in param_shapes],
        out_shape=[jax.ShapeDtypeStruct((t_tok, 512), BF16), jax.ShapeDtypeStruct((t_tok, 128), BF16)]
        + [jax.ShapeDtypeStruct(s, F32) for s in param_shapes],
        scratch_shapes=[pltpu.VMEM((128, 256), F32)],
        compiler_params=_params(48, ("arbitrary",)),
    )(sgu, f, dyc, drow, dcol, b_f, sgu_g, sgu_b, w_s, b_mat)


def _mix_in_bwd(dconv, dq, dk, dv, dsgu, df, dz, w_in, tm=512):
    t_tok = dz.shape[0]

    def body(dconv_ref, dq_ref, dk_ref, dv_ref, dsgu_ref, df_ref, dz_ref, w_ref, dx_ref, dp_ref):
        dqb = dq_ref[...].astype(BF16)
        pieces = [(COL_CONV, dconv_ref[...]), (COL_QKV, dqb), (COL_QKV + 512, dk_ref[...]), (COL_QKV + 1024, dv_ref[...]),
                  (COL_SGU, dsgu_ref[...]), (COL_F, df_ref[...])]
        dx = ALPHA * dz_ref[...]
        for col, val in pieces:
            width = val.shape[1]
            dp_ref[:, col:col + width] = val
            dx = dx + _dot_nt(val, w_ref[:, col:col + width])
        dx_ref[...] = dx

    def tok(n):
        return pl.BlockSpec((tm, n), lambda i: (i, 0))

    return pl.pallas_call(
        body, name="mix_in_bwd", grid=(t_tok // tm,),
        in_specs=[tok(768), tok(512), tok(512), tok(512), tok(512), tok(128), tok(D_MODEL),
                  pl.BlockSpec((D_MODEL, D_IN_PAD), lambda i: (0, 0))],
        out_specs=[tok(D_MODEL), tok(D_IN_PAD)],
        out_shape=[jax.ShapeDtypeStruct((t_tok, D_MODEL), F32), jax.ShapeDtypeStruct((t_tok, D_IN_PAD), BF16)],
        compiler_params=_params(48, ("arbitrary",)),
    )(dconv, dq, dk, dv, dsgu, df, dz, w_in)


def _loss_grad(y, target, tm=512):
    t_tok = y.shape[0]

    def body(y_ref, t_ref, dy_ref, loss_ref):
        err = y_ref[...] - t_ref[...]
        dy_ref[...] = err * (1.0 / D_MODEL)
        part = jnp.sum(jnp.sum(err * err, axis=1, keepdims=True), axis=0, keepdims=True) * (0.5 / D_MODEL)

        @pl.when(pl.program_id(0) == 0)
        def _():
            loss_ref[...] = jnp.zeros_like(loss_ref)

        loss_ref[...] += part

    tok = pl.BlockSpec((tm, D_MODEL), lambda i: (i, 0))
    return pl.pallas_call(
        body, name="loss_grad", grid=(t_tok // tm,),
        in_specs=[tok, tok], out_specs=[tok, pl.BlockSpec((1, 128), lambda i: (0, 0))],
        out_shape=[jax.ShapeDtypeStruct((t_tok, D_MODEL), F32), jax.ShapeDtypeStruct((1, 128), F32)],
        compiler_params=_params(32, ("arbitrary",)),
    )(y, target)


def _pad_rows(a, rows):
    return jnp.pad(a, ((0, rows - a.shape[0]), (0, 0)))


F_BLOCK = F_ORIG // D_IN_SHARD
F_AT = F_ORIG - F_BLOCK * D_IN_SHARD
assert (F_ORIG + N_HEADS) // D_IN_SHARD == F_BLOCK


def _w_in_from_blocks(g):
    fb = g[F_BLOCK]
    zeros = jnp.zeros((D_MODEL, D_IN_PAD - COL_F - N_HEADS), g.dtype)
    return jnp.concatenate([g[d] for d in range(F_BLOCK)] + [fb[:, :F_AT], fb[:, F_AT + N_HEADS:]]
                           + [g[d] for d in range(F_BLOCK + 1, N_DEV)] + [fb[:, F_AT:F_AT + N_HEADS], zeros], axis=1)


def _w_in_to_blocks(dw):
    def cols(lo, hi):
        shift = 0 if hi <= F_ORIG else N_HEADS
        return dw[:, lo - shift:hi - shift]

    blocks = []
    for d in range(N_DEV):
        lo, hi = d * D_IN_SHARD, (d + 1) * D_IN_SHARD
        if d == F_BLOCK:
            blocks.append(jnp.concatenate([cols(lo, F_ORIG), dw[:, COL_F:COL_F + N_HEADS], cols(F_ORIG + N_HEADS, hi)], axis=1))
        else:
            blocks.append(cols(lo, hi))
    return jnp.stack(blocks)


SMALL_ROWS = DEPTH * (6 * 8 + 2 * 8 + 512 + 8 + 8 + 8)


def _pack_small(p):
    rows = []
    for l in range(DEPTH):
        for name in ("ln1_g", "ln1_b", "ln2_g", "ln2_b", "ln3_g", "ln3_b"):
            rows.append(p[name][l].reshape(8, 128))
        for name in ("sgu_ln_g", "sgu_ln_b"):
            rows.append(_pad_rows(p[name][l].reshape(2, 128), 8))
        rows.append(p["sgu_w_s"][l].reshape(512, 128))
        rows.append(_pad_rows(p["sgu_b_s"][l], 8))
        rows.append(_pad_rows(jnp.pad(p["fox_b_f"][l], (0, 128 - N_HEADS)).reshape(1, 128), 8))
        rows.append(_pad_rows(p["conv_w"][l].reshape(6, 128), 8))
    return jnp.concatenate(rows, axis=0)


def _unpack_small(a):
    out = {}
    r = 0

    def take(n, valid):
        nonlocal r
        piece = a[r:r + valid]
        r += n
        return piece

    per_layer = []
    for l in range(DEPTH):
        d = {}
        for name in ("ln1_g", "ln1_b", "ln2_g", "ln2_b", "ln3_g", "ln3_b"):
            d[name] = take(8, 8).reshape(D_MODEL)
        for name in ("sgu_ln_g", "sgu_ln_b"):
            d[name] = take(8, 2).reshape(D_SGU)
        d["sgu_w_s"] = take(512, 512).reshape(N_SGU_GROUPS, SGU_CHUNK, SGU_CHUNK)
        d["sgu_b_s"] = take(8, 4).reshape(N_SGU_GROUPS, SGU_CHUNK)
        d["fox_b_f"] = take(8, 1).reshape(128)[:N_HEADS]
        d["conv_w"] = take(8, 6).reshape(3, D_CONV)
        per_layer.append(d)
    for name in per_layer[0]:
        out[name] = jnp.stack([per_layer[l][name] for l in range(DEPTH)])
    return out


SMALL_NAMES = ("ln1_g", "ln1_b", "fox_b_f", "sgu_ln_g", "sgu_ln_b", "sgu_w_s", "sgu_b_s", "ln2_g", "ln2_b", "ln3_g", "ln3_b")
BIG_NAMES = ("ffn1_w_up", "ffn1_w_down", "mix_w_in", "mix_w_out", "ffn2_w_up", "ffn2_w_down")
WEIGHT_ORDER = ("ln1_g", "ln1_b", "ffn1_w_up", "ffn1_w_down", "mix_w_in", "fox_b_f", "conv_w", "sgu_ln_g", "sgu_ln_b",
                "sgu_w_s", "sgu_b_s", "mix_w_out", "ln2_g", "ln2_b", "ffn2_w_up", "ffn2_w_down", "ln3_g", "ln3_b")


class _Overlap:
    def __init__(self, w, after, me, core):
        self.core = core
        groups = [[("ffn1_w_up", 0), ("ffn1_w_down", 0)],
                  [("mix_w_in", 0), ("mix_w_out", 0), ("ffn2_w_up", 0), ("ffn2_w_down", 0)]]
        groups += [[(name, l) for name in BIG_NAMES] for l in range(1, DEPTH)]
        self.gathers = []
        for gi, group in enumerate(groups):
            shards = [w[name][l].astype(BF16) for name, l in group]
            lands = [lax.dynamic_update_slice(lax.empty((N_DEV,) + s.shape, BF16), s[None], (me, 0, 0)) for s in shards]
            started = _exchange_start(f"allgather_start_{gi}", _gather_plan(len(group)), 3 * len(group), shards + lands, after)
            after = started[3]
            self.gathers.append((group, started))
        self.all_started = after
        self.gathered = {}
        self.scatters = {}
        self.order = []

    def weights(self, layer, part, after):
        gi = layer + 1 if layer > 0 else (0 if part == "ffn1" else 1)
        group, started = self.gathers[gi]
        if gi not in self.gathered:
            m = len(group)
            arrays = _exchange_wait(f"allgather_wait_{gi}", _gather_plan(m), 3 * m, started,
                                    self.all_started if after is None else after)
            self.gathered[gi] = dict(zip(group, _gather_finish(arrays[:m], arrays[m:])))
        g = self.gathered[gi]

        def ffn(n):
            return g[(f"ffn{n}_w_up", layer)], g[(f"ffn{n}_w_down", layer)].reshape(N_FFN_CHUNK, FFN_BLK, D_MODEL)

        if part == "ffn1":
            return ffn(1)
        return (_w_in_from_blocks(g[("mix_w_in", layer)]), g[("mix_w_out", layer)].reshape(D_MODEL, D_MODEL), *ffn(2))

    def push(self, key, items):
        n = len(items)
        grads = [g for _, _, g in items]
        lands = [lax.empty((4,) + g.shape[1:], F32) for g in grads]
        started = _exchange_start(f"rs_sibling_start_{key[0]}{key[1]}", _sibling_plan(n), 4 * n, grads + lands, self.core)
        self.scatters[key] = dict(items=items, sibling=started)
        self.order.append(key)
        return started[3]

    def advance(self, key, after):
        st = self.scatters[key]
        n = len(st["items"])
        arrays = _exchange_wait(f"rs_sibling_wait_{key[0]}{key[1]}", _sibling_plan(n), 4 * n, st["sibling"], after)
        partials = [_chip_partial(g, r, self.core) for g, r in zip(arrays[:n], arrays[n:])]
        p16 = [p for _, p in partials]
        lands = [lax.empty((3,) + p.shape[1:], BF16) for p in p16]
        started = _exchange_start(f"rs_chip_start_{key[0]}{key[1]}", _chip_plan(n), 3 * n, p16 + lands, self.core)
        st.update(p32=[p for p, _ in partials], chip=started)
        return started[3]

    def finish(self, slot, w, m, v):
        res = {}
        after = self.scatters[self.order[-1]]["chip"][3]
        for key in self.order:
            st = self.scatters[key]
            n = len(st["items"])
            arrays = _exchange_wait(f"rs_chip_wait_{key[0]}{key[1]}", _chip_plan(n), 3 * n, st["chip"], after)
            for (name, l, _), p32, r16 in zip(st["items"], st["p32"], arrays[n:]):
                res[name] = _adamw_shard(p32, r16, slot, w[name], m[name], v[name], l, res.get(name))
                after = res[name][0]
        return res


def _local_step(x, target, comm, small, n_seq):
    def vec(a):
        return a.reshape(1, -1)

    saved = []
    h = x
    for l in range(DEPTH):
        s = {}
        s["up1"], s["down1"] = comm.weights(l, "ffn1", None if l == 0 else h)
        h1, h1b, s["z1"], s["gu1"], s["x0b"] = _ffn_fwd(h, s["up1"], s["down1"], vec(small["ln1_g"][l]), vec(small["ln1_b"][l]))
        s["w_in"], s["w_out"], s["up2"], s["down2"] = comm.weights(l, "rest", s["z1"])
        s["x1b"] = h1b
        conv, qkv, sgu, f = _in_proj(h1, s["w_in"])
        cw = _pad_rows(small["conv_w"][l], 8)
        bf = jnp.pad(small["fox_b_f"][l], (0, 128 - N_HEADS)).reshape(1, 128)
        b_mat = jnp.repeat(small["sgu_b_s"][l].T, D_SGU // N_SGU_GROUPS, axis=1)
        mid_params = (cw, bf, vec(small["sgu_ln_g"][l]), vec(small["sgu_ln_b"][l]), small["sgu_w_s"][l], b_mat)
        ya, yc, cum_t = _mix_mid_fwd(conv, sgu, f, *mid_params, n_seq)
        yb, lse = _fox_fwd(qkv, cum_t, n_seq)
        h2, h2b, s["z2"] = _mix_out_fwd(ya, yb, yc, h1, s["w_out"], vec(small["ln2_g"][l]), vec(small["ln2_b"][l]))
        s.update(conv=conv, qkv=qkv, sgu=sgu, f=f, mid_params=mid_params, ya=ya, yb=yb, yc=yc, cum_t=cum_t, lse=lse, x2b=h2b)
        h3, _, s["z3"], s["gu2"], _ = _ffn_fwd(h2, s["up2"], s["down2"], vec(small["ln3_g"][l]), vec(small["ln3_b"][l]))
        saved.append(s)
        h = h3

    dh, loss = _loss_grad(h, target)

    small_grads = [None] * DEPTH
    token = loss
    pending = None
    for l in reversed(range(DEPTH)):
        s = saved[l]
        sg = {}
        dh, dy, a, dgu, sg["ln3_g"], sg["ln3_b"] = _ffn_bwd(dh, s["z3"], s["gu2"], s["up2"], s["down2"], vec(small["ln3_g"][l]), token)
        if pending is not None:
            token = comm.advance(pending, dh)
        g_up2 = _matmul_tn(s["x2b"][None], dgu.reshape(N_DEV, -1, FFN_BLK), token)
        g_down2 = _matmul_tn(a, dy[None], token).reshape(N_DEV, FFN_BLK // 2, D_MODEL)
        dz, dzb, dya, dyb, dyc, sg["ln2_g"], sg["ln2_b"] = _mix_out_bwd(dh, s["z2"], s["w_out"], vec(small["ln2_g"][l]))
        dwo = [_matmul_tn(y[None], dzb[None], token)[0] for y in (s["ya"], s["yb"], s["yc"])]
        g_out = jnp.concatenate(dwo, axis=0).reshape(N_DEV, D_MODEL // N_DEV, D_MODEL)
        dq, dk, dv, drow, dcol = _fox_bwd(s["qkv"], s["cum_t"], s["yb"], s["lse"], dyb, n_seq)
        dconv, dcw = _conv_bwd(s["conv"], dya, s["mid_params"][0], n_seq)
        dsgu, df, dbf, dlg, dlb, dws, dbs = _sgu_gate_bwd(s["sgu"], s["f"], dyc, drow, dcol, *s["mid_params"][1:], n_seq)
        sg.update(conv_w=dcw[:3], fox_b_f=dbf[0, :N_HEADS], sgu_ln_g=dlg[0], sgu_ln_b=dlb[0], sgu_w_s=dws,
                  sgu_b_s=dbs[:, :N_SGU_GROUPS].T)
        dh, dp = _mix_in_bwd(dconv, dq, dk, dv, dsgu, df, dz, s["w_in"])
        g_in = _w_in_to_blocks(_matmul_tn(s["x1b"][None], dp[None], token, tk=512)[0])
        first = [("ffn2_w_up", l, g_up2), ("ffn2_w_down", l, g_down2), ("mix_w_out", l, g_out), ("mix_w_in", l, g_in)]
        if l == 0:
            token = comm.push((l, "a"), first)
            pending, first = (l, "a"), []
        dh, dy, a, dgu, sg["ln1_g"], sg["ln1_b"] = _ffn_bwd(dh, s["z1"], s["gu1"], s["up1"], s["down1"], vec(small["ln1_g"][l]), token)
        if l == 0:
            token = comm.advance(pending, dh)
        g_up1 = _matmul_tn(s["x0b"][None], dgu.reshape(N_DEV, -1, FFN_BLK), token)
        g_down1 = _matmul_tn(a, dy[None], token).reshape(N_DEV, FFN_BLK // 2, D_MODEL)
        key = (l, "b")
        token = comm.push(key, first + [("ffn1_w_up", l, g_up1), ("ffn1_w_down", l, g_down1)])
        pending = key
        if l == 0:
            token = comm.advance(key, token)
        for name in ("ln1_g", "ln1_b", "ln2_g", "ln2_b", "ln3_g", "ln3_b"):
            sg[name] = sg[name][0]
        small_grads[l] = sg
    return loss, dh, small_grads


def kernel(x, ln1_g, ln1_b, ffn1_w_up, ffn1_w_down, mix_w_in, fox_b_f, conv_w, sgu_ln_g, sgu_ln_b, sgu_w_s, sgu_b_s, mix_w_out, ln2_g, ln2_b, ffn2_w_up, ffn2_w_down, ln3_g, ln3_b, loss_target, m_ln1_g, m_ln1_b, m_ffn1_w_up, m_ffn1_w_down, m_mix_w_in, m_fox_b_f, m_conv_w, m_sgu_ln_g, m_sgu_ln_b, m_sgu_w_s, m_sgu_b_s, m_mix_w_out, m_ln2_g, m_ln2_b, m_ffn2_w_up, m_ffn2_w_down, m_ln3_g, m_ln3_b, v_ln1_g, v_ln1_b, v_ffn1_w_up, v_ffn1_w_down, v_mix_w_in, v_fox_b_f, v_conv_w, v_sgu_ln_g, v_sgu_ln_b, v_sgu_w_s, v_sgu_b_s, v_mix_w_out, v_ln2_g, v_ln2_b, v_ffn2_w_up, v_ffn2_w_down, v_ln3_g, v_ln3_b):
    w = dict(ln1_g=ln1_g, ln1_b=ln1_b, ffn1_w_up=ffn1_w_up, ffn1_w_down=ffn1_w_down, mix_w_in=mix_w_in, fox_b_f=fox_b_f,
             conv_w=conv_w, sgu_ln_g=sgu_ln_g, sgu_ln_b=sgu_ln_b, sgu_w_s=sgu_w_s, sgu_b_s=sgu_b_s, mix_w_out=mix_w_out,
             ln2_g=ln2_g, ln2_b=ln2_b, ffn2_w_up=ffn2_w_up, ffn2_w_down=ffn2_w_down, ln3_g=ln3_g, ln3_b=ln3_b)
    m = dict(ln1_g=m_ln1_g, ln1_b=m_ln1_b, ffn1_w_up=m_ffn1_w_up, ffn1_w_down=m_ffn1_w_down, mix_w_in=m_mix_w_in,
             fox_b_f=m_fox_b_f, conv_w=m_conv_w, sgu_ln_g=m_sgu_ln_g, sgu_ln_b=m_sgu_ln_b, sgu_w_s=m_sgu_w_s,
             sgu_b_s=m_sgu_b_s, mix_w_out=m_mix_w_out, ln2_g=m_ln2_g, ln2_b=m_ln2_b, ffn2_w_up=m_ffn2_w_up,
             ffn2_w_down=m_ffn2_w_down, ln3_g=m_ln3_g, ln3_b=m_ln3_b)
    v = dict(ln1_g=v_ln1_g, ln1_b=v_ln1_b, ffn1_w_up=v_ffn1_w_up, ffn1_w_down=v_ffn1_w_down, mix_w_in=v_mix_w_in,
             fox_b_f=v_fox_b_f, conv_w=v_conv_w, sgu_ln_g=v_sgu_ln_g, sgu_ln_b=v_sgu_ln_b, sgu_w_s=v_sgu_w_s,
             sgu_b_s=v_sgu_b_s, mix_w_out=v_mix_w_out, ln2_g=v_ln2_g, ln2_b=v_ln2_b, ffn2_w_up=v_ffn2_w_up,
             ffn2_w_down=v_ffn2_w_down, ln3_g=v_ln3_g, ln3_b=v_ln3_b)

    mx, my, mc = lax.axis_index("x"), lax.axis_index("y"), lax.axis_index("c")
    me = 4 * mx + 2 * my + mc
    n_seq, seq, _ = x.shape
    t_tok = n_seq * seq

    cw_rows = _pad_rows(conv_w.reshape(DEPTH * 3, D_CONV // N_DEV), 8)
    cw_all = _allgather_small(jnp.pad(cw_rows, ((0, 0), (0, 128 - D_CONV // N_DEV))))
    conv_w_full = jnp.transpose(cw_all[:, :DEPTH * 3, :D_CONV // N_DEV], (1, 0, 2)).reshape(DEPTH, 3, D_CONV)
    small = {name: w[name] for name in SMALL_NAMES}
    small["conv_w"] = conv_w_full

    comm = _Overlap(w, cw_all, me, mc.reshape(1).astype(jnp.int32))
    loss_dev, grad_x, small_grads = _local_step(
        x.reshape(t_tok, D_MODEL), loss_target.reshape(t_tok, D_MODEL), comm, small, n_seq)
    loss = lax.psum(loss_dev[0, 0], ("x", "y", "c"))
    out = comm.finish((2 * mx + my).reshape(1).astype(jnp.int32), w, m, v)

    sg = {name: jnp.stack([small_grads[l][name] for l in range(DEPTH)]) for name in SMALL_NAMES + ("conv_w",)}
    all_small = _allgather_small(_pack_small(sg))

    def widen(a):
        return lax.dynamic_update_slice(jnp.zeros((DEPTH, 3, D_CONV), F32), a, (0, 0, me * (D_CONV // N_DEV)))

    packed = [_pack_small({**{name: t[name] for name in SMALL_NAMES}, "conv_w": widen(t["conv_w"])}) for t in (w, m, v)]
    small_out = [_unpack_small(a) for a in _adamw_small(all_small, *packed)]
    for name in SMALL_NAMES:
        out[name] = [small_out[k][name] for k in range(4)]
    out["conv_w"] = [lax.dynamic_slice(small_out[k]["conv_w"], (0, 0, me * (D_CONV // N_DEV)), (DEPTH, 3, D_CONV // N_DEV))
                     for k in range(4)]

    return (loss, grad_x.reshape(x.shape), *[out[name][0] for name in WEIGHT_ORDER], *[out[name][1] for name in WEIGHT_ORDER],
            *[out[name][2] for name in WEIGHT_ORDER], *[out[name][3] for name in WEIGHT_ORDER])
```

```python
import functools

import jax
import jax.numpy as jnp
from jax import lax
from jax.experimental import pallas as pl
from jax.experimental.pallas import tpu as pltpu

F32 = jnp.float32
BF16 = jnp.bfloat16
MESH = pl.DeviceIdType.MESH

N_DEV = 8
DEPTH = 2
D_MODEL = 1024
D_FF = 2816
FFN_BLK = 2 * D_FF // N_DEV
N_FFN_CHUNK = D_FF // FFN_BLK
D_CONV = 256
D_FOX = 512
N_HEADS = 8
D_SGU = 256
N_SGU_GROUPS = 4
SGU_CHUNK = 128
D_IN = 3 * D_CONV + 3 * D_FOX + N_HEADS + 2 * D_SGU
D_IN_SHARD = D_IN // N_DEV
COL_CONV, COL_QKV, COL_SGU, COL_F = 0, 768, 2304, 2816
D_IN_PAD = 2944
F_ORIG = 3 * D_CONV + 3 * D_FOX
ALPHA = (2 * DEPTH) ** 0.25
LN_EPS = 1e-5
ATT_SCALE = 0.125
ATT_BLK = 512
NEG = -1e30

ADAM_LR, ADAM_B1, ADAM_B2, ADAM_EPS, ADAM_WD, ADAM_STEP = 0.001, 0.9, 0.999, 1e-08, 0.01, 10

VMEM_BYTES_V7X = 64 * 1024 * 1024
HIGHEST = lax.Precision.HIGHEST


def _params(vmem_mb, sem=None):
    assert vmem_mb * 1024 * 1024 < VMEM_BYTES_V7X
    kw = dict(vmem_limit_bytes=vmem_mb * 1024 * 1024)
    if sem is not None:
        kw["dimension_semantics"] = sem
    return pltpu.CompilerParams(**kw)


def _dot(a, b, precision=None):
    return lax.dot_general(a, b, (((1,), (0,)), ((), ())), preferred_element_type=F32, precision=precision)


def _dot_nt(a, b):
    return lax.dot_general(a, b, (((1,), (1,)), ((), ())), preferred_element_type=F32)


def _dot_tn(a, b):
    return lax.dot_general(a, b, (((0,), (0,)), ((), ())), preferred_element_type=F32)


def _ln_stats(z):
    mu = jnp.mean(z, axis=-1, keepdims=True)
    zc = z - mu
    var = jnp.mean(zc * zc, axis=-1, keepdims=True)
    rstd = lax.rsqrt(var + LN_EPS)
    return zc * rstd, rstd


def _ln_bwd(dy, xhat, rstd, g):
    dxh = dy * g
    m1 = jnp.mean(dxh, axis=-1, keepdims=True)
    m2 = jnp.mean(dxh * xhat, axis=-1, keepdims=True)
    return rstd * (dxh - m1 - xhat * m2)


_GELU_C = 0.7978845608028654


def _gelu(x):
    return 0.5 * x * (1.0 + jnp.tanh(_GELU_C * (x + 0.044715 * x * x * x)))


def _gelu_grad(x):
    t = jnp.tanh(_GELU_C * (x + 0.044715 * x * x * x))
    return 0.5 * (1.0 + t) + 0.5 * x * (1.0 - t * t) * _GELU_C * (1.0 + 3 * 0.044715 * x * x)


def _hbm(shape, dtype):
    n = 1
    for d in shape:
        n *= d
    if n * jnp.dtype(dtype).itemsize >= 1024 * 1024:
        return pltpu.HBM(tuple(shape), dtype)
    return jax.ShapeDtypeStruct(tuple(shape), dtype)


def _vspec():
    return pl.BlockSpec(memory_space=pltpu.VMEM)


def _anyspec():
    return pl.BlockSpec(memory_space=pl.ANY)


def _mesh_pos():
    return lax.axis_index("x"), lax.axis_index("y"), lax.axis_index("c")


def _other_chips(x, y):
    return [(1 - x, y), (x, 1 - y), (1 - x, 1 - y)]


_HBM_SPEC = pl.BlockSpec(memory_space=pltpu.HBM)
_SEM_SPEC = pl.BlockSpec(memory_space=pltpu.SEMAPHORE)
_DATAFLOW_EFFECT = pltpu.SideEffectType.DATAFLOW_SIDE_EFFECTING


def _remote_copies(plan, refs, send_sems, recv_sems):
    return [pltpu.make_async_remote_copy(src_ref=src, dst_ref=dst, send_sem=send_sems.at[k], recv_sem=recv_sems.at[k],
                                         device_id=to, device_id_type=MESH)
            for k, (src, dst, to) in enumerate(plan(refs, *_mesh_pos()))]


def _exchange_start(name, plan, n_copies, arrays, after):
    n = len(arrays)

    def body(*refs):
        send_sems, recv_sems, token = refs[n + 1], refs[n + 2], refs[-1]
        for cp in _remote_copies(plan, refs[:n], send_sems, recv_sems):
            cp.start()
        token[...] = jnp.zeros_like(token)

    out = pl.pallas_call(
        body, name=name,
        out_shape=(pltpu.SemaphoreType.DMA((n_copies,)), pltpu.SemaphoreType.DMA((n_copies,)),
                   *[pltpu.HBM(a.shape, a.dtype) for a in arrays], _hbm((8, 128), F32)),
        in_specs=[_HBM_SPEC] * n + [_anyspec()],
        out_specs=(_SEM_SPEC, _SEM_SPEC, *[_HBM_SPEC] * n, _vspec()),
        input_output_aliases={i: 2 + i for i in range(n)},
        compiler_params=pltpu.CompilerParams(has_side_effects=_DATAFLOW_EFFECT),
    )(*[pltpu.with_memory_space_constraint(a, pltpu.HBM) for a in arrays], after)
    return out[0], out[1], list(out[2:2 + n]), out[-1]


def _exchange_wait(name, plan, n_copies, started, after):
    send_sems, recv_sems, arrays, _ = started
    n = len(arrays)

    def body(*refs):
        for cp in _remote_copies(plan, refs[:n], refs[n], refs[n + 1]):
            cp.wait_send()
            cp.wait_recv()

    out = pl.pallas_call(
        body, name=name,
        out_shape=tuple(pltpu.HBM(a.shape, a.dtype) for a in arrays),
        in_specs=[_HBM_SPEC] * n + [_SEM_SPEC, _SEM_SPEC, _anyspec()], out_specs=tuple([_HBM_SPEC] * n),
        input_output_aliases={i: i for i in range(n)},
        compiler_params=pltpu.CompilerParams(has_side_effects=_DATAFLOW_EFFECT),
    )(*arrays, send_sems, recv_sems, after)
    return list(out)


def _gather_plan(m):
    def plan(refs, x, y, c):
        me = 4 * x + 2 * y + c
        return [(refs[i], refs[m + i].at[me], (*chip, c)) for i in range(m) for chip in _other_chips(x, y)]
    return plan


def _gather_finish(shards, lands):
    m = len(shards)

    def body(*refs):
        ins, outs = refs[:m], refs[2 * m:3 * m]
        send_sems, recv_sems = refs[3 * m:]
        x, y, c = _mesh_pos()
        me = 4 * x + 2 * y + c
        sibling = (x, y, 1 - c)
        copies = []
        for i in range(m):
            blocks = [(ins[i], me)] + [(outs[i].at[4 * cx + 2 * cy + c], 4 * cx + 2 * cy + c) for cx, cy in _other_chips(x, y)]
            for k, (src, blk) in enumerate(blocks):
                copies.append(pltpu.make_async_remote_copy(
                    src_ref=src, dst_ref=outs[i].at[blk], send_sem=send_sems.at[4 * i + k], recv_sem=recv_sems.at[4 * i + k],
                    device_id=sibling, device_id_type=MESH))
        for cp in copies:
            cp.start()
        for cp in copies:
            cp.wait_send()
        for i in range(m):
            for k, (cx, cy) in enumerate([(x, y)] + _other_chips(x, y)):
                blk = 4 * cx + 2 * cy + (1 - c)
                pltpu.make_async_remote_copy(
                    src_ref=outs[i].at[blk], dst_ref=outs[i].at[blk], send_sem=send_sems.at[4 * i + k],
                    recv_sem=recv_sems.at[4 * i + k], device_id=sibling, device_id_type=MESH).wait_recv()

    return pl.pallas_call(
        body, name="allgather_finish",
        out_shape=[_hbm(a.shape, a.dtype) for a in lands],
        in_specs=[_anyspec()] * (2 * m), out_specs=[_anyspec()] * m,
        input_output_aliases={m + i: i for i in range(m)},
        scratch_shapes=[pltpu.SemaphoreType.DMA((4 * m,)), pltpu.SemaphoreType.DMA((4 * m,))],
    )(*shards, *lands)


def _allgather_small(v):
    rows = v.shape[0]

    def body(v_ref, out_ref, send_sems, recv_sems):
        x, y, c = _mesh_pos()
        me = 4 * x + 2 * y + c
        out_ref[me] = v_ref[...]
        rel = [(dx, dy, dc) for dx in (0, 1) for dy in (0, 1) for dc in (0, 1)][1:]
        copies = []
        for k, (dx, dy, dc) in enumerate(rel):
            to = (x ^ dx, y ^ dy, c ^ dc)
            copies.append(pltpu.make_async_remote_copy(
                src_ref=v_ref, dst_ref=out_ref.at[me], send_sem=send_sems.at[k], recv_sem=recv_sems.at[k],
                device_id=to, device_id_type=MESH))
        for cp in copies:
            cp.start()
        for k, (dx, dy, dc) in enumerate(rel):
            src_blk = 4 * (x ^ dx) + 2 * (y ^ dy) + (c ^ dc)
            pltpu.make_async_remote_copy(
                src_ref=v_ref, dst_ref=out_ref.at[src_blk], send_sem=send_sems.at[k], recv_sem=recv_sems.at[k],
                device_id=(x, y, c), device_id_type=MESH).wait_recv()
        for cp in copies:
            cp.wait_send()

    return pl.pallas_call(
        body, name="allgather_small",
        out_shape=jax.ShapeDtypeStruct((N_DEV, rows, 128), v.dtype),
        in_specs=[_vspec()], out_specs=_vspec(),
        scratch_shapes=[pltpu.SemaphoreType.DMA((7,)), pltpu.SemaphoreType.DMA((7,))],
        compiler_params=_params(24),
    )(v)


def _sibling_plan(n):
    def plan(refs, x, y, c):
        return [(refs[a].at[2 * q + (1 - c)], refs[n + a].at[q], (x, y, 1 - c)) for a in range(n) for q in range(4)]
    return plan


def _chip_plan(n):
    def plan(refs, x, y, c):
        return [(refs[a].at[2 * cx + cy], refs[n + a].at[j], (cx, cy, c))
                for a in range(n) for j, (cx, cy) in enumerate(_other_chips(x, y))]
    return plan


def _row_tile(rows, cols, budget_bytes=2 * 1024 * 1024):
    best = 8
    for t in range(8, rows + 1, 8):
        if rows % t == 0 and t * cols * 4 <= budget_bytes:
            best = t
    return best


def _chip_partial(g, recv, core):
    _, rows, cols = g.shape
    tr = _row_tile(rows, cols)

    def body(core_ref, g_ref, r_ref, o32_ref, o16_ref):
        s = g_ref[...] + r_ref[...]
        o32_ref[...] = s
        o16_ref[...] = s.astype(BF16)

    blk = (None, tr, cols)
    return pl.pallas_call(
        body, name="rs_chip_partial",
        grid_spec=pltpu.PrefetchScalarGridSpec(
            num_scalar_prefetch=1, grid=(4, rows // tr),
            in_specs=[pl.BlockSpec(blk, lambda q, i, c: (2 * q + c[0], i, 0)),
                      pl.BlockSpec(blk, lambda q, i, c: (q, i, 0))],
            out_specs=[pl.BlockSpec(blk, lambda q, i, c: (q, i, 0))] * 2),
        out_shape=[_hbm((4, rows, cols), F32), _hbm((4, rows, cols), BF16)],
        compiler_params=_params(32),
    )(core, g, recv)


def _adam_math(w, g, m, v):
    m = ADAM_B1 * m + (1.0 - ADAM_B1) * g
    v = ADAM_B2 * v + (1.0 - ADAM_B2) * (g * g)
    m_hat = m / (1.0 - ADAM_B1 ** ADAM_STEP)
    v_hat = v / (1.0 - ADAM_B2 ** ADAM_STEP)
    delta = -ADAM_LR * (m_hat / (jnp.sqrt(v_hat) + ADAM_EPS) + ADAM_WD * w)
    return delta, m, v


def _adamw_shard(part32, recv16, slot, w, m, v, layer, earlier):
    depth, rows, cols = w.shape
    tr = _row_tile(rows, cols, 1024 * 1024)
    n_prev = 0 if earlier is None else 4

    def body(slot_ref, p_ref, r_ref, w_ref, m_ref, v_ref, *rest):
        g_out, d_out, m_out, v_out = rest[n_prev:]
        g = p_ref[...] + r_ref[0].astype(F32) + r_ref[1].astype(F32) + r_ref[2].astype(F32)
        d, mn, vn = _adam_math(w_ref[...], g, m_ref[...], v_ref[...])
        g_out[...] = g
        d_out[...] = d
        m_out[...] = mn
        v_out[...] = vn

    mine = pl.BlockSpec((None, tr, cols), lambda i, s: (layer, i, 0))
    return pl.pallas_call(
        body, name="adamw_shard",
        grid_spec=pltpu.PrefetchScalarGridSpec(
            num_scalar_prefetch=1, grid=(rows // tr,),
            in_specs=[pl.BlockSpec((None, tr, cols), lambda i, s: (s[0], i, 0)),
                      pl.BlockSpec((3, tr, cols), lambda i, s: (0, i, 0)), mine, mine, mine] + [_anyspec()] * n_prev,
            out_specs=[mine] * 4),
        out_shape=[_hbm((depth, rows, cols), F32)] * 4,
        input_output_aliases={6 + k: k for k in range(n_prev)},
        compiler_params=_params(32),
    )(slot, part32, recv16, w, m, v, *([] if earlier is None else earlier))


def _adamw_small(gathered, w, m, v):
    rows = w.shape[0]

    def body(a_ref, w_ref, m_ref, v_ref, g_out, d_out, m_out, v_out):
        g = a_ref[0]
        for d in range(1, N_DEV):
            g = g + a_ref[d]
        dl, mn, vn = _adam_math(w_ref[...], g, m_ref[...], v_ref[...])
        g_out[...] = g
        d_out[...] = dl
        m_out[...] = mn
        v_out[...] = vn

    return pl.pallas_call(
        body, name="adamw_small",
        in_specs=[_vspec()] * 4, out_specs=[_vspec()] * 4,
        out_shape=[_hbm((rows, 128), F32)] * 4,
        compiler_params=_params(32),
    )(gathered, w, m, v)


def _load_weights_once(pairs, sems):
    @pl.when(pl.program_id(0) == 0)
    def _():
        cps = [pltpu.make_async_copy(src, dst, sems.at[i]) for i, (src, dst) in enumerate(pairs)]
        for cp in cps:
            cp.start()
        for cp in cps:
            cp.wait()


def _ffn_fwd(x, wup, wd, ln_g, ln_b, tm=512):
    t_tok = x.shape[0]

    def body(x_ref, g_ref, b_ref, wup_hbm, wd_hbm, xn_ref, xnb_ref, z_ref, gu_ref, xb_ref, wup_v, wd_v, sems):
        _load_weights_once([(wup_hbm, wup_v), (wd_hbm, wd_v)], sems)
        xb = x_ref[...].astype(BF16)
        xb_ref[...] = xb
        y = None
        for j in range(N_FFN_CHUNK):
            g = _dot(xb, wup_v[j])
            u = _dot(xb, wup_v[N_FFN_CHUNK + j])
            gu_ref[0, j] = g.astype(BF16)
            gu_ref[1, j] = u.astype(BF16)
            a = (g * jax.nn.sigmoid(g) * u).astype(BF16)
            part = _dot(a, wd_v[j])
            y = part if y is None else y + part
        z = ALPHA * x_ref[...] + 0.5 * y
        xhat, _ = _ln_stats(z)
        xn = xhat * g_ref[...] + b_ref[...]
        z_ref[...] = z
        xn_ref[...] = xn
        xnb_ref[...] = xn.astype(BF16)

    tok = pl.BlockSpec((tm, D_MODEL), lambda i: (i, 0))
    vec = pl.BlockSpec((1, D_MODEL), lambda i: (0, 0))
    return pl.pallas_call(
        body, name="ffn_fwd", grid=(t_tok // tm,),
        in_specs=[tok, vec, vec, _anyspec(), _anyspec()],
        out_specs=[tok, tok, tok, pl.BlockSpec((2, N_FFN_CHUNK, tm, FFN_BLK), lambda i: (0, 0, i, 0)), tok],
        out_shape=[_hbm((t_tok, D_MODEL), F32), _hbm((t_tok, D_MODEL), BF16),
                   _hbm((t_tok, D_MODEL), F32),
                   _hbm((2, N_FFN_CHUNK, t_tok, FFN_BLK), BF16),
                   _hbm((t_tok, D_MODEL), BF16)],
        scratch_shapes=[pltpu.VMEM((N_DEV, D_MODEL, FFN_BLK), BF16), pltpu.VMEM((N_FFN_CHUNK, FFN_BLK, D_MODEL), BF16),
                        pltpu.SemaphoreType.DMA((2,))],
        compiler_params=_params(62, ("arbitrary",)),
    )(x, ln_g, ln_b, wup, wd)


def _ffn_bwd(dxn, z, gu, wup, wd, ln_g, after, tm=256):
    t_tok = dxn.shape[0]

    def body(dxn_ref, z_ref, gu_ref, g_ref, wup_hbm, wd_hbm, _after,
             dx_ref, dy_ref, a_ref, dgu_ref, dg_ref, db_ref, wup_v, wd_v, sems):
        i = pl.program_id(0)
        _load_weights_once([(wup_hbm, wup_v), (wd_hbm, wd_v)], sems)
        dxn_t = dxn_ref[...]
        xhat, rstd = _ln_stats(z_ref[...])
        pg = jnp.sum(dxn_t * xhat, axis=0, keepdims=True)
        pb = jnp.sum(dxn_t, axis=0, keepdims=True)

        @pl.when(i == 0)
        def _():
            dg_ref[...] = pg
            db_ref[...] = pb

        @pl.when(i > 0)
        def _():
            dg_ref[...] += pg
            db_ref[...] += pb

        dz = _ln_bwd(dxn_t, xhat, rstd, g_ref[...])
        dy = (0.5 * dz).astype(BF16)
        dy_ref[...] = dy
        dx = ALPHA * dz
        for j in range(N_FFN_CHUNK):
            da = _dot_nt(dy, wd_v[j])
            g = gu_ref[0, j].astype(F32)
            u = gu_ref[1, j].astype(F32)
            sig = jax.nn.sigmoid(g)
            silu = g * sig
            a_ref[j] = (silu * u).astype(BF16)
            dg = (da * u * (sig * (1.0 + g * (1.0 - sig)))).astype(BF16)
            du = (da * silu).astype(BF16)
            dgu_ref[0, j] = dg
            dgu_ref[1, j] = du
            dx = dx + _dot_nt(dg, wup_v[j]) + _dot_nt(du, wup_v[N_FFN_CHUNK + j])
        dx_ref[...] = dx

    tok = pl.BlockSpec((tm, D_MODEL), lambda i: (i, 0))
    vec = pl.BlockSpec((1, D_MODEL), lambda i: (0, 0))
    gu_spec = pl.BlockSpec((2, N_FFN_CHUNK, tm, FFN_BLK), lambda i: (0, 0, i, 0))
    return pl.pallas_call(
        body, name="ffn_bwd", grid=(t_tok // tm,),
        in_specs=[tok, tok, gu_spec, vec, _anyspec(), _anyspec(), _anyspec()],
        out_specs=[tok, tok, pl.BlockSpec((N_FFN_CHUNK, tm, FFN_BLK), lambda i: (0, i, 0)), gu_spec, vec, vec],
        out_shape=[_hbm((t_tok, D_MODEL), F32), _hbm((t_tok, D_MODEL), BF16),
                   _hbm((N_FFN_CHUNK, t_tok, FFN_BLK), BF16),
                   _hbm((2, N_FFN_CHUNK, t_tok, FFN_BLK), BF16),
                   _hbm((1, D_MODEL), F32), _hbm((1, D_MODEL), F32)],
        scratch_shapes=[pltpu.VMEM((N_DEV, D_MODEL, FFN_BLK), BF16), pltpu.VMEM((N_FFN_CHUNK, FFN_BLK, D_MODEL), BF16),
                        pltpu.SemaphoreType.DMA((2,))],
        compiler_params=_params(60, ("arbitrary",)),
    )(dxn, z, gu, ln_g, wup, wd, after)


def _matmul_tn(a, b, after, tk=4096):
    ga, t_tok, m = a.shape
    gb, _, n = b.shape
    groups = max(ga, gb)
    tk = min(tk, t_tok)

    def body(a_ref, b_ref, _after, o_ref):
        p = _dot_tn(a_ref[...].astype(BF16), b_ref[...].astype(BF16))

        @pl.when(pl.program_id(1) == 0)
        def _():
            o_ref[...] = p

        @pl.when(pl.program_id(1) > 0)
        def _():
            o_ref[...] += p

    return pl.pallas_call(
        body, name=f"matmul_tn_{m}x{n}", grid=(groups, t_tok // tk),
        in_specs=[pl.BlockSpec((None, tk, m), (lambda g, t: (g, t, 0)) if ga > 1 else (lambda g, t: (0, t, 0))),
                  pl.BlockSpec((None, tk, n), (lambda g, t: (g, t, 0)) if gb > 1 else (lambda g, t: (0, t, 0))),
                  _anyspec()],
        out_specs=pl.BlockSpec((None, m, n), lambda g, t: (g, 0, 0)),
        out_shape=_hbm((groups, m, n), F32),
        compiler_params=_params(56, ("arbitrary", "arbitrary")),
    )(a, b, after)


def _in_proj(x, w_in, tm=512):
    t_tok = x.shape[0]

    def body(x_ref, w_ref, conv_ref, qkv_ref, sgu_ref, f_ref):
        xb = x_ref[...].astype(BF16)
        conv_ref[...] = _dot(xb, w_ref[:, COL_CONV:COL_QKV])
        qkv_ref[...] = _dot(xb, w_ref[:, COL_QKV:COL_SGU]).astype(BF16)
        sgu_ref[...] = _dot(xb, w_ref[:, COL_SGU:COL_F])
        f_ref[...] = _dot(xb, w_ref[:, COL_F:D_IN_PAD])

    def tok(n):
        return pl.BlockSpec((tm, n), lambda i: (i, 0))

    return pl.pallas_call(
        body, name="mix_in_proj", grid=(t_tok // tm,),
        in_specs=[tok(D_MODEL), pl.BlockSpec((D_MODEL, D_IN_PAD), lambda i: (0, 0))],
        out_specs=[tok(768), tok(1536), tok(512), tok(128)],
        out_shape=[_hbm((t_tok, 768), F32), _hbm((t_tok, 1536), BF16),
                   _hbm((t_tok, 512), F32), _hbm((t_tok, 128), F32)],
        compiler_params=_params(48, ("arbitrary",)),
    )(x, w_in)


def _shift_down(a, k):
    row = lax.broadcasted_iota(jnp.int32, a.shape, 0)
    return jnp.where(row >= k, pltpu.roll(a, k, 0), 0.0)


def _shift_up(a, k):
    rows = a.shape[0]
    row = lax.broadcasted_iota(jnp.int32, a.shape, 0)
    return jnp.where(row < rows - k, pltpu.roll(a, rows - k, 0), 0.0)


def _tril(n):
    return lax.broadcasted_iota(jnp.int32, (n, n), 0) >= lax.broadcasted_iota(jnp.int32, (n, n), 1)


def _sgu_group_of_lane():
    return lax.broadcasted_iota(jnp.int32, (1, D_SGU), 1) // (D_SGU // N_SGU_GROUPS)


def _log_sigmoid(x):
    return jnp.minimum(x, 0.0) - jnp.log1p(jnp.exp(-jnp.abs(x)))


def _mix_mid_fwd(conv, sgu, f, conv_w, b_f, sgu_g, sgu_b, w_s, b_mat, n_seq):
    t_tok = conv.shape[0]
    seq = t_tok // n_seq
    n_chunk = seq // SGU_CHUNK
    per_blk = ATT_BLK // SGU_CHUNK

    def body(conv_ref, sgu_ref, f_ref, cw_ref, bf_ref, lg_ref, lb_ref, ws_ref, bm_ref, ya_ref, yc_ref, cum_ref):
        z = conv_ref[:, 256:512] * conv_ref[:, 512:768]
        y = cw_ref[0:1, :] * _shift_down(z, 2) + cw_ref[1:2, :] * _shift_down(z, 1) + cw_ref[2:3, :] * z
        ya_ref[...] = (conv_ref[:, 0:256] * y).astype(BF16)

        tril = _tril(SGU_CHUNK)
        grp = _sgu_group_of_lane()
        wc = [jnp.where(tril, ws_ref[g], 0.0).astype(BF16) for g in range(N_SGU_GROUPS)]
        tri_f = tril.astype(F32)
        carry = jnp.zeros((1, 128), F32)
        for n in range(n_chunk):
            rows = pl.ds(n * SGU_CHUNK, SGU_CHUNK)
            u = _gelu(sgu_ref[rows, 0:256])
            vhat, _ = _ln_stats(_gelu(sgu_ref[rows, 256:512]))
            vn = (vhat * lg_ref[...] + lb_ref[...]).astype(BF16)
            mixed = bm_ref[...]
            for g in range(N_SGU_GROUPS):
                mixed = mixed + jnp.where(grp == g, _dot(wc[g], vn), 0.0)
            yc_ref[rows, :] = (u * mixed).astype(BF16)

            log_f = _log_sigmoid(f_ref[rows, :] + bf_ref[...])
            cs = _dot(tri_f, log_f, HIGHEST) + carry
            carry = cs[SGU_CHUNK - 1:SGU_CHUNK, :]
            cs_t = cs.T
            lanes = pl.ds((n % per_blk) * SGU_CHUNK, SGU_CHUNK)
            for h in range(N_HEADS):
                cum_ref[h, n // per_blk, :, lanes] = cs_t[h:h + 1, :]

    def seq_blk(n):
        return pl.BlockSpec((seq, n), lambda b: (b, 0))

    def full(shape):
        return pl.BlockSpec(shape, lambda b: (0,) * len(shape))

    return pl.pallas_call(
        body, name="mix_mid_fwd", grid=(n_seq,),
        in_specs=[seq_blk(768), seq_blk(512), seq_blk(128), full((8, 256)), full((1, 128)), full((1, 256)),
                  full((1, 256)), full((4, 128, 128)), full((128, 256))],
        out_specs=[seq_blk(256), seq_blk(256),
                   pl.BlockSpec((N_HEADS, seq // ATT_BLK, 1, ATT_BLK), lambda b: (b, 0, 0, 0))],
        out_shape=[_hbm((t_tok, 256), BF16), _hbm((t_tok, 256), BF16),
                   _hbm((n_seq * N_HEADS, seq // ATT_BLK, 1, ATT_BLK), F32)],
        compiler_params=_params(48, ("arbitrary",)),
    )(conv, sgu, f, conv_w, b_f, sgu_g, sgu_b, w_s, b_mat)


def _head_masks():
    lane = lax.broadcasted_iota(jnp.int32, (1, 128), 1)
    return lane < 64, lane


def _fox_fwd(qkv, cum_t, n_seq):
    t_tok = qkv.shape[0]
    seq = t_tok // n_seq
    nq = seq // ATT_BLK
    blk = ATT_BLK

    def body(q_ref, k_ref, v_ref, c0_ref, c1_ref, o_ref, lse_ref):
        qi = pl.program_id(2)
        first, _ = _head_masks()
        qs = q_ref[...] * ATT_SCALE
        zero = jnp.zeros_like(qs)
        q0 = jnp.where(first, qs, zero)
        q1 = jnp.where(first, zero, qs)
        causal = _tril(blk)
        one = jnp.ones((1, 128), BF16)

        def step(kb, carry, masked):
            m0, m1, acc0, acc1 = carry
            rows = pl.ds(pl.multiple_of(kb * blk, blk), blk)
            k = k_ref[rows, :]
            v = v_ref[rows, :]

            def head(qh, c_ref, m, acc, vh):
                s = _dot_nt(qh, k) - c_ref[kb]
                if masked:
                    s = jnp.where(causal, s, NEG)
                m_new = jnp.maximum(m, jnp.max(s, axis=1, keepdims=True))
                p = jnp.exp(s - m_new)
                return m_new, acc * jnp.exp(m - m_new) + _dot(p.astype(BF16), vh)

            m0, acc0 = head(q0, c0_ref, m0, acc0, jnp.where(first, v, one))
            m1, acc1 = head(q1, c1_ref, m1, acc1, jnp.where(first, one, v))
            return m0, m1, acc0, acc1

        col = jnp.full((blk, 1), NEG, F32)
        zacc = jnp.zeros((blk, 128), F32)
        carry = lax.fori_loop(0, qi, lambda kb, cr: step(kb, cr, False), (col, col, zacc, zacc))
        m0, m1, acc0, acc1 = step(qi, carry, True)
        l0 = pltpu.roll(acc0, 64, 1)
        l1 = pltpu.roll(acc1, 64, 1)
        o_ref[...] = jnp.where(first, acc0 / l0, acc1 / l1).astype(BF16)
        lse_ref[...] = jnp.where(first, m0 + jnp.log(l0), m1 + jnp.log(l1))

    cum_spec0 = pl.BlockSpec((None, nq, 1, blk), lambda b, hp, qi: (b * N_HEADS + 2 * hp, 0, 0, 0))
    cum_spec1 = pl.BlockSpec((None, nq, 1, blk), lambda b, hp, qi: (b * N_HEADS + 2 * hp + 1, 0, 0, 0))
    out_spec = pl.BlockSpec((blk, 128), lambda b, hp, qi: (b * nq + qi, hp))
    return pl.pallas_call(
        body, name="fox_fwd", grid=(n_seq, 4, nq),
        in_specs=[pl.BlockSpec((blk, 128), lambda b, hp, qi: (b * nq + qi, hp)),
                  pl.BlockSpec((seq, 128), lambda b, hp, qi: (b, 4 + hp)),
                  pl.BlockSpec((seq, 128), lambda b, hp, qi: (b, 8 + hp)), cum_spec0, cum_spec1],
        out_specs=[out_spec, out_spec],
        out_shape=[_hbm((t_tok, D_FOX), BF16), _hbm((t_tok, D_FOX), F32)],
        compiler_params=_params(32, ("arbitrary", "arbitrary", "arbitrary")),
    )(qkv, qkv, qkv, cum_t, cum_t)


def _fox_bwd(qkv, cum_t, o, lse, d_o, n_seq):
    t_tok = qkv.shape[0]
    seq = t_tok // n_seq
    nk = seq // ATT_BLK
    blk = ATT_BLK

    def body(q_ref, k_ref, v_ref, c0_ref, c1_ref, o_ref, lse_ref, do_ref,
             dq_ref, dk_ref, dv_ref, drow_ref, dcol_ref):
        kb = pl.program_id(2)
        first, lane = _head_masks()
        second = jnp.logical_not(first)
        k = k_ref[...]
        v = v_ref[...]
        zero = jnp.zeros_like(k)
        one = jnp.ones((1, 128), BF16)
        ks = k * ATT_SCALE
        causal = _tril(blk)

        @pl.when(kb == 0)
        def _():
            dq_ref[...] = jnp.zeros_like(dq_ref)
            drow_ref[...] = jnp.zeros_like(drow_ref)

        def step(qi, carry, masked):
            rows = pl.ds(pl.multiple_of(qi * blk, blk), blk)
            qs = q_ref[rows, :] * ATT_SCALE
            d_o = do_ref[rows, :]
            dd = d_o.astype(F32) * o_ref[rows, :].astype(F32)
            lse_t = lse_ref[rows, :]

            def head(mine, c, lse_lane, dk, dv):
                qh = jnp.where(mine, qs, zero)
                doh = jnp.where(mine, d_o, zero)
                delta = jnp.sum(jnp.where(mine, dd, 0.0), axis=1, keepdims=True)
                lse_h = jnp.sum(jnp.where(lane == lse_lane, lse_t, 0.0), axis=1, keepdims=True)
                s = _dot_nt(qh, k) - c
                if masked:
                    s = jnp.where(causal, s, NEG)
                p = jnp.exp(s - lse_h)
                ds = (p * (_dot_nt(doh, v) - delta)).astype(BF16)
                dk = dk + _dot_tn(ds, jnp.where(mine, qs, one))
                dv = dv + _dot_tn(p.astype(BF16), doh)
                return dk, dv, _dot(ds, jnp.where(mine, ks, one))

            dk0, dv0, dk1, dv1 = carry
            dk0, dv0, dq0 = head(first, c0_ref[...], 0, dk0, dv0)
            dk1, dv1, dq1 = head(second, c1_ref[...], 64, dk1, dv1)
            dq_ref[rows, :] += jnp.where(first, dq0, dq1)
            drow_ref[rows, :] += jnp.where(first, dq1, dq0)
            return dk0, dv0, dk1, dv1

        zt = jnp.zeros((blk, 128), F32)
        carry = step(kb, (zt, zt, zt, zt), True)
        dk0, dv0, dk1, dv1 = lax.fori_loop(kb + 1, nk, lambda qi, cr: step(qi, cr, False), carry)
        dk_ref[...] = jnp.where(first, dk0, dk1).astype(BF16)
        dcol_ref[...] = jnp.where(first, dk1, dk0)
        dv_ref[...] = (dv0 + dv1).astype(BF16)

    def seq_spec(col0):
        return pl.BlockSpec((seq, 128), lambda b, hp, kb: (b, col0 + hp))

    def key_spec(col0):
        return pl.BlockSpec((blk, 128), lambda b, hp, kb: (b * nk + kb, col0 + hp))

    def cum_spec(h):
        return pl.BlockSpec((None, None, 1, blk), lambda b, hp, kb: (b * N_HEADS + 2 * hp + h, kb, 0, 0))

    return pl.pallas_call(
        body, name="fox_bwd", grid=(n_seq, 4, nk),
        in_specs=[seq_spec(0), key_spec(4), key_spec(8), cum_spec(0), cum_spec(1), seq_spec(0), seq_spec(0), seq_spec(0)],
        out_specs=[seq_spec(0), key_spec(0), key_spec(0), seq_spec(0), key_spec(0)],
        out_shape=[_hbm((t_tok, D_FOX), F32), _hbm((t_tok, D_FOX), BF16),
                   _hbm((t_tok, D_FOX), BF16), _hbm((t_tok, D_FOX), F32),
                   _hbm((t_tok, D_FOX), F32)],
        compiler_params=_params(48, ("arbitrary", "arbitrary", "arbitrary")),
    )(qkv, qkv, qkv, cum_t, cum_t, o, lse, d_o)


def _mix_out_fwd(ya, yb, yc, x, w_out, ln_g, ln_b, tm=512):
    t_tok = x.shape[0]

    def body(ya_ref, yb_ref, yc_ref, x_ref, w_ref, g_ref, b_ref, xn_ref, xnb_ref, z_ref):
        mo = _dot(ya_ref[...], w_ref[0:256, :]) + _dot(yb_ref[...], w_ref[256:768, :]) + _dot(yc_ref[...], w_ref[768:1024, :])
        z = ALPHA * x_ref[...] + mo
        xhat, _ = _ln_stats(z)
        xn = xhat * g_ref[...] + b_ref[...]
        z_ref[...] = z
        xn_ref[...] = xn
        xnb_ref[...] = xn.astype(BF16)

    def tok(n):
        return pl.BlockSpec((tm, n), lambda i: (i, 0))

    vec = pl.BlockSpec((1, D_MODEL), lambda i: (0, 0))
    return pl.pallas_call(
        body, name="mix_out_fwd", grid=(t_tok // tm,),
        in_specs=[tok(256), tok(512), tok(256), tok(D_MODEL),
                  pl.BlockSpec((D_MODEL, D_MODEL), lambda i: (0, 0)), vec, vec],
        out_specs=[tok(D_MODEL)] * 3,
        out_shape=[_hbm((t_tok, D_MODEL), F32), _hbm((t_tok, D_MODEL), BF16),
                   _hbm((t_tok, D_MODEL), F32)],
        compiler_params=_params(40, ("arbitrary",)),
    )(ya, yb, yc, x, w_out, ln_g, ln_b)


def _mix_out_bwd(dxn, z, w_out, ln_g, tm=512):
    t_tok = dxn.shape[0]

    def body(dxn_ref, z_ref, w_ref, g_ref, dz_ref, dzb_ref, dya_ref, dyb_ref, dyc_ref, dg_ref, db_ref):
        i = pl.program_id(0)
        dxn_t = dxn_ref[...]
        xhat, rstd = _ln_stats(z_ref[...])
        pg = jnp.sum(dxn_t * xhat, axis=0, keepdims=True)
        pb = jnp.sum(dxn_t, axis=0, keepdims=True)

        @pl.when(i == 0)
        def _():
            dg_ref[...] = pg
            db_ref[...] = pb

        @pl.when(i > 0)
        def _():
            dg_ref[...] += pg
            db_ref[...] += pb

        dz = _ln_bwd(dxn_t, xhat, rstd, g_ref[...])
        dzb = dz.astype(BF16)
        dz_ref[...] = dz
        dzb_ref[...] = dzb
        dya_ref[...] = _dot_nt(dzb, w_ref[0:256, :])
        dyb_ref[...] = _dot_nt(dzb, w_ref[256:768, :]).astype(BF16)
        dyc_ref[...] = _dot_nt(dzb, w_ref[768:1024, :])

    def tok(n):
        return pl.BlockSpec((tm, n), lambda i: (i, 0))

    vec = pl.BlockSpec((1, D_MODEL), lambda i: (0, 0))
    return pl.pallas_call(
        body, name="mix_out_bwd", grid=(t_tok // tm,),
        in_specs=[tok(D_MODEL), tok(D_MODEL), pl.BlockSpec((D_MODEL, D_MODEL), lambda i: (0, 0)), vec],
        out_specs=[tok(D_MODEL), tok(D_MODEL), tok(256), tok(512), tok(256), vec, vec],
        out_shape=[_hbm((t_tok, D_MODEL), F32), _hbm((t_tok, D_MODEL), BF16),
                   _hbm((t_tok, 256), F32), _hbm((t_tok, 512), BF16),
                   _hbm((t_tok, 256), F32),
                   _hbm((1, D_MODEL), F32), _hbm((1, D_MODEL), F32)],
        compiler_params=_params(40, ("arbitrary",)),
    )(dxn, z, w_out, ln_g)


def _conv_bwd(conv, dya, conv_w, n_seq):
    t_tok = conv.shape[0]
    seq = t_tok // n_seq

    def body(conv_ref, dya_ref, cw_ref, dconv_ref, dcw_ref):
        @pl.when(pl.program_id(0) == 0)
        def _():
            dcw_ref[...] = jnp.zeros_like(dcw_ref)

        z = conv_ref[:, 256:512] * conv_ref[:, 512:768]
        z1 = _shift_down(z, 1)
        z2 = _shift_down(z, 2)
        y = cw_ref[0:1, :] * z2 + cw_ref[1:2, :] * z1 + cw_ref[2:3, :] * z
        dya_t = dya_ref[...]
        dconv_ref[:, 0:256] = (dya_t * y).astype(BF16)
        dy = dya_t * conv_ref[:, 0:256]
        dcw_ref[0:1, :] += jnp.sum(dy * z2, axis=0, keepdims=True)
        dcw_ref[1:2, :] += jnp.sum(dy * z1, axis=0, keepdims=True)
        dcw_ref[2:3, :] += jnp.sum(dy * z, axis=0, keepdims=True)
        dz = cw_ref[2:3, :] * dy + cw_ref[1:2, :] * _shift_up(dy, 1) + cw_ref[0:1, :] * _shift_up(dy, 2)
        dconv_ref[:, 256:512] = (dz * conv_ref[:, 512:768]).astype(BF16)
        dconv_ref[:, 512:768] = (dz * conv_ref[:, 256:512]).astype(BF16)

    def seq_blk(n):
        return pl.BlockSpec((seq, n), lambda b: (b, 0))

    par = pl.BlockSpec((8, 256), lambda b: (0, 0))
    return pl.pallas_call(
        body, name="conv_bwd", grid=(n_seq,),
        in_specs=[seq_blk(768), seq_blk(256), par], out_specs=[seq_blk(768), par],
        out_shape=[_hbm((t_tok, 768), BF16), _hbm((8, 256), F32)],
        compiler_params=_params(56, ("arbitrary",)),
    )(conv, dya, conv_w)


def _sgu_gate_bwd(sgu, f, dyc, drow, dcol, b_f, sgu_g, sgu_b, w_s, b_mat, n_seq):
    t_tok = sgu.shape[0]
    seq = t_tok // n_seq
    n_chunk = seq // SGU_CHUNK

    def body(sgu_ref, f_ref, dyc_ref, drow_ref, dcol_ref, bf_ref, lg_ref, lb_ref, ws_ref, bm_ref,
             dsgu_ref, df_ref, dbf_ref, dlg_ref, dlb_ref, dws_ref, dbs_ref, dbm_acc):
        b = pl.program_id(0)

        @pl.when(b == 0)
        def _():
            for r in (dbf_ref, dlg_ref, dlb_ref, dws_ref, dbm_acc):
                r[...] = jnp.zeros_like(r)

        tril = _tril(SGU_CHUNK)
        grp = _sgu_group_of_lane()
        wc = [jnp.where(tril, ws_ref[g], 0.0).astype(BF16) for g in range(N_SGU_GROUPS)]
        for n in range(n_chunk):
            rows = pl.ds(n * SGU_CHUNK, SGU_CHUNK)
            su = sgu_ref[rows, 0:256]
            sv = sgu_ref[rows, 256:512]
            u = _gelu(su)
            vhat, rstd = _ln_stats(_gelu(sv))
            vn = (vhat * lg_ref[...] + lb_ref[...]).astype(BF16)
            mixed = bm_ref[...]
            for g in range(N_SGU_GROUPS):
                mixed = mixed + jnp.where(grp == g, _dot(wc[g], vn), 0.0)
            dyc_t = dyc_ref[rows, :]
            dsgu_ref[rows, 0:256] = (dyc_t * mixed * _gelu_grad(su)).astype(BF16)
            dmixed = dyc_t * u
            dbm_acc[...] += dmixed
            dvn = jnp.zeros((SGU_CHUNK, D_SGU), F32)
            for g in range(N_SGU_GROUPS):
                dm_g = jnp.where(grp == g, dmixed, 0.0).astype(BF16)
                dws_ref[g] += _dot_nt(dm_g, vn)
                dvn = dvn + _dot_tn(wc[g], dm_g)
            dlg_ref[...] += jnp.sum(dvn * vhat, axis=0, keepdims=True)
            dlb_ref[...] += jnp.sum(dvn, axis=0, keepdims=True)
            dsgu_ref[rows, 256:512] = (_ln_bwd(dvn, vhat, rstd, lg_ref[...]) * _gelu_grad(sv)).astype(BF16)

        later = (lax.broadcasted_iota(jnp.int32, (128, 128), 0) <= lax.broadcasted_iota(jnp.int32, (128, 128), 1)).astype(F32)
        head = lax.broadcasted_iota(jnp.int32, (D_FOX, 128), 1)
        pick = (lax.broadcasted_iota(jnp.int32, (D_FOX, 128), 0) == 128 * (head // 2) + 64 * (1 - head % 2)).astype(F32)
        carry = jnp.zeros((1, 128), F32)
        for n in reversed(range(n_chunk)):
            rows = pl.ds(n * SGU_CHUNK, SGU_CHUNK)
            dcum_n = _dot(drow_ref[rows, :] - dcol_ref[rows, :], pick, HIGHEST)
            dlf = _dot(later, dcum_n, HIGHEST) + carry
            carry = carry + jnp.sum(dcum_n, axis=0, keepdims=True)
            df = dlf * jax.nn.sigmoid(-(f_ref[rows, :] + bf_ref[...]))
            df_ref[rows, :] = df.astype(BF16)
            dbf_ref[...] += jnp.sum(df, axis=0, keepdims=True)

        @pl.when(b == n_seq - 1)
        def _():
            for g in range(N_SGU_GROUPS):
                dws_ref[g] = jnp.where(tril, dws_ref[g], 0.0)
            sel = (lax.broadcasted_iota(jnp.int32, (D_SGU, 128), 0) // (D_SGU // N_SGU_GROUPS)
                   == lax.broadcasted_iota(jnp.int32, (D_SGU, 128), 1)).astype(F32)
            dbs_ref[...] = _dot(dbm_acc[...], sel, HIGHEST)

    def seq_blk(n):
        return pl.BlockSpec((seq, n), lambda b: (b, 0))

    def full(shape):
        return pl.BlockSpec(shape, lambda b: (0,) * len(shape))

    param_shapes = [(1, 128), (1, 256), (1, 256), (4, 128, 128), (128, 128)]
    return pl.pallas_call(
        body, name="sgu_gate_bwd", grid=(n_seq,),
        in_specs=[seq_blk(512), seq_blk(128), seq_blk(256), seq_blk(D_FOX), seq_blk(D_FOX),
                  full((1, 128)), full((1, 256)), full((1, 256)), full((4, 128, 128)), full((128, 256))],
        out_specs=[seq_blk(512), seq_blk(128)] + [full(s) for s in param_shapes],
        out_shape=[_hbm((t_tok, 512), BF16), _hbm((t_tok, 128), BF16)]
        + [_hbm(s, F32) for s in param_shapes],
        scratch_shapes=[pltpu.VMEM((128, 256), F32)],
        compiler_params=_params(48, ("arbitrary",)),
    )(sgu, f, dyc, drow, dcol, b_f, sgu_g, sgu_b, w_s, b_mat)


def _mix_in_bwd(dconv, dq, dk, dv, dsgu, df, dz, w_in, tm=512):
    t_tok = dz.shape[0]

    def body(dconv_ref, dq_ref, dk_ref, dv_ref, dsgu_ref, df_ref, dz_ref, w_ref, dx_ref, dp_ref):
        dqb = dq_ref[...].astype(BF16)
        pieces = [(COL_CONV, dconv_ref[...]), (COL_QKV, dqb), (COL_QKV + 512, dk_ref[...]), (COL_QKV + 1024, dv_ref[...]),
                  (COL_SGU, dsgu_ref[...]), (COL_F, df_ref[...])]
        dx = ALPHA * dz_ref[...]
        for col, val in pieces:
            width = val.shape[1]
            dp_ref[:, col:col + width] = val
            dx = dx + _dot_nt(val, w_ref[:, col:col + width])
        dx_ref[...] = dx

    def tok(n):
        return pl.BlockSpec((tm, n), lambda i: (i, 0))

    return pl.pallas_call(
        body, name="mix_in_bwd", grid=(t_tok // tm,),
        in_specs=[tok(768), tok(512), tok(512), tok(512), tok(512), tok(128), tok(D_MODEL),
                  pl.BlockSpec((D_MODEL, D_IN_PAD), lambda i: (0, 0))],
        out_specs=[tok(D_MODEL), tok(D_IN_PAD)],
        out_shape=[_hbm((t_tok, D_MODEL), F32), _hbm((t_tok, D_IN_PAD), BF16)],
        compiler_params=_params(48, ("arbitrary",)),
    )(dconv, dq, dk, dv, dsgu, df, dz, w_in)


def _loss_grad(y, target, tm=512):
    t_tok = y.shape[0]

    def body(y_ref, t_ref, dy_ref, loss_ref):
        err = y_ref[...] - t_ref[...]
        dy_ref[...] = err * (1.0 / D_MODEL)
        part = jnp.sum(jnp.sum(err * err, axis=1, keepdims=True), axis=0, keepdims=True) * (0.5 / D_MODEL)

        @pl.when(pl.program_id(0) == 0)
        def _():
            loss_ref[...] = jnp.zeros_like(loss_ref)

        loss_ref[...] += part

    tok = pl.BlockSpec((tm, D_MODEL), lambda i: (i, 0))
    return pl.pallas_call(
        body, name="loss_grad", grid=(t_tok // tm,),
        in_specs=[tok, tok], out_specs=[tok, pl.BlockSpec((1, 128), lambda i: (0, 0))],
        out_shape=[_hbm((t_tok, D_MODEL), F32), _hbm((1, 128), F32)],
        compiler_params=_params(32, ("arbitrary",)),
    )(y, target)


def _pad_rows(a, rows):
    return jnp.pad(a, ((0, rows - a.shape[0]), (0, 0)))


F_BLOCK = F_ORIG // D_IN_SHARD
F_AT = F_ORIG - F_BLOCK * D_IN_SHARD
assert (F_ORIG + N_HEADS) // D_IN_SHARD == F_BLOCK


def _w_in_from_blocks(g):
    fb = g[F_BLOCK]
    zeros = jnp.zeros((D_MODEL, D_IN_PAD - COL_F - N_HEADS), g.dtype)
    return jnp.concatenate([g[d] for d in range(F_BLOCK)] + [fb[:, :F_AT], fb[:, F_AT + N_HEADS:]]
                           + [g[d] for d in range(F_BLOCK + 1, N_DEV)] + [fb[:, F_AT:F_AT + N_HEADS], zeros], axis=1)


def _w_in_to_blocks(dw):
    def cols(lo, hi):
        shift = 0 if hi <= F_ORIG else N_HEADS
        return dw[:, lo - shift:hi - shift]

    blocks = []
    for d in range(N_DEV):
        lo, hi = d * D_IN_SHARD, (d + 1) * D_IN_SHARD
        if d == F_BLOCK:
            blocks.append(jnp.concatenate([cols(lo, F_ORIG), dw[:, COL_F:COL_F + N_HEADS], cols(F_ORIG + N_HEADS, hi)], axis=1))
        else:
            blocks.append(cols(lo, hi))
    return jnp.stack(blocks)


SMALL_ROWS = DEPTH * (6 * 8 + 2 * 8 + 512 + 8 + 8 + 8)


def _pack_small(p):
    rows = []
    for l in range(DEPTH):
        for name in ("ln1_g", "ln1_b", "ln2_g", "ln2_b", "ln3_g", "ln3_b"):
            rows.append(p[name][l].reshape(8, 128))
        for name in ("sgu_ln_g", "sgu_ln_b"):
            rows.append(_pad_rows(p[name][l].reshape(2, 128), 8))
        rows.append(p["sgu_w_s"][l].reshape(512, 128))
        rows.append(_pad_rows(p["sgu_b_s"][l], 8))
        rows.append(_pad_rows(jnp.pad(p["fox_b_f"][l], (0, 128 - N_HEADS)).reshape(1, 128), 8))
        rows.append(_pad_rows(p["conv_w"][l].reshape(6, 128), 8))
    return jnp.concatenate(rows, axis=0)


def _unpack_small(a):
    out = {}
    r = 0

    def take(n, valid):
        nonlocal r
        piece = a[r:r + valid]
        r += n
        return piece

    per_layer = []
    for l in range(DEPTH):
        d = {}
        for name in ("ln1_g", "ln1_b", "ln2_g", "ln2_b", "ln3_g", "ln3_b"):
            d[name] = take(8, 8).reshape(D_MODEL)
        for name in ("sgu_ln_g", "sgu_ln_b"):
            d[name] = take(8, 2).reshape(D_SGU)
        d["sgu_w_s"] = take(512, 512).reshape(N_SGU_GROUPS, SGU_CHUNK, SGU_CHUNK)
        d["sgu_b_s"] = take(8, 4).reshape(N_SGU_GROUPS, SGU_CHUNK)
        d["fox_b_f"] = take(8, 1).reshape(128)[:N_HEADS]
        d["conv_w"] = take(8, 6).reshape(3, D_CONV)
        per_layer.append(d)
    for name in per_layer[0]:
        out[name] = jnp.stack([per_layer[l][name] for l in range(DEPTH)])
    return out


SMALL_NAMES = ("ln1_g", "ln1_b", "fox_b_f", "sgu_ln_g", "sgu_ln_b", "sgu_w_s", "sgu_b_s", "ln2_g", "ln2_b", "ln3_g", "ln3_b")
BIG_NAMES = ("ffn1_w_up", "ffn1_w_down", "mix_w_in", "mix_w_out", "ffn2_w_up", "ffn2_w_down")
WEIGHT_ORDER = ("ln1_g", "ln1_b", "ffn1_w_up", "ffn1_w_down", "mix_w_in", "fox_b_f", "conv_w", "sgu_ln_g", "sgu_ln_b",
                "sgu_w_s", "sgu_b_s", "mix_w_out", "ln2_g", "ln2_b", "ffn2_w_up", "ffn2_w_down", "ln3_g", "ln3_b")


class _Overlap:
    def __init__(self, w, after, me, core):
        self.core = core
        groups = [[("ffn1_w_up", 0), ("ffn1_w_down", 0)],
                  [("mix_w_in", 0), ("mix_w_out", 0), ("ffn2_w_up", 0), ("ffn2_w_down", 0)]]
        groups += [[(name, l) for name in BIG_NAMES] for l in range(1, DEPTH)]
        self.gathers = []
        for gi, group in enumerate(groups):
            shards = [w[name][l].astype(BF16) for name, l in group]
            lands = [lax.dynamic_update_slice(lax.empty((N_DEV,) + s.shape, BF16), s[None], (me, 0, 0)) for s in shards]
            started = _exchange_start(f"allgather_start_{gi}", _gather_plan(len(group)), 3 * len(group), shards + lands, after)
            after = started[3]
            self.gathers.append((group, started))
        self.all_started = after
        self.gathered = {}
        self.scatters = {}
        self.order = []

    def weights(self, layer, part, after):
        gi = layer + 1 if layer > 0 else (0 if part == "ffn1" else 1)
        group, started = self.gathers[gi]
        if gi not in self.gathered:
            m = len(group)
            arrays = _exchange_wait(f"allgather_wait_{gi}", _gather_plan(m), 3 * m, started,
                                    self.all_started if after is None else after)
            self.gathered[gi] = dict(zip(group, _gather_finish(arrays[:m], arrays[m:])))
        g = self.gathered[gi]

        def ffn(n):
            return g[(f"ffn{n}_w_up", layer)], g[(f"ffn{n}_w_down", layer)].reshape(N_FFN_CHUNK, FFN_BLK, D_MODEL)

        if part == "ffn1":
            return ffn(1)
        return (_w_in_from_blocks(g[("mix_w_in", layer)]), g[("mix_w_out", layer)].reshape(D_MODEL, D_MODEL), *ffn(2))

    def push(self, key, items):
        n = len(items)
        grads = [g for _, _, g in items]
        lands = [lax.empty((4,) + g.shape[1:], F32) for g in grads]
        started = _exchange_start(f"rs_sibling_start_{key[0]}{key[1]}", _sibling_plan(n), 4 * n, grads + lands, self.core)
        self.scatters[key] = dict(items=items, sibling=started)
        self.order.append(key)
        return started[3]

    def advance(self, key, after):
        st = self.scatters[key]
        n = len(st["items"])
        arrays = _exchange_wait(f"rs_sibling_wait_{key[0]}{key[1]}", _sibling_plan(n), 4 * n, st["sibling"], after)
        partials = [_chip_partial(g, r, self.core) for g, r in zip(arrays[:n], arrays[n:])]
        p16 = [p for _, p in partials]
        lands = [lax.empty((3,) + p.shape[1:], BF16) for p in p16]
        started = _exchange_start(f"rs_chip_start_{key[0]}{key[1]}", _chip_plan(n), 3 * n, p16 + lands, self.core)
        st.update(p32=[p for p, _ in partials], chip=started)
        return started[3]

    def finish(self, slot, w, m, v):
        res = {}
        after = self.scatters[self.order[-1]]["chip"][3]
        for key in self.order:
            st = self.scatters[key]
            n = len(st["items"])
            arrays = _exchange_wait(f"rs_chip_wait_{key[0]}{key[1]}", _chip_plan(n), 3 * n, st["chip"], after)
            for (name, l, _), p32, r16 in zip(st["items"], st["p32"], arrays[n:]):
                res[name] = _adamw_shard(p32, r16, slot, w[name], m[name], v[name], l, res.get(name))
                after = res[name][0]
        return res


def _local_step(x, target, comm, small, n_seq):
    def vec(a):
        return a.reshape(1, -1)

    saved = []
    h = x
    for l in range(DEPTH):
        s = {}
        s["up1"], s["down1"] = comm.weights(l, "ffn1", None if l == 0 else h)
        h1, h1b, s["z1"], s["gu1"], s["x0b"] = _ffn_fwd(h, s["up1"], s["down1"], vec(small["ln1_g"][l]), vec(small["ln1_b"][l]))
        s["w_in"], s["w_out"], s["up2"], s["down2"] = comm.weights(l, "rest", s["z1"])
        s["x1b"] = h1b
        conv, qkv, sgu, f = _in_proj(h1, s["w_in"])
        cw = _pad_rows(small["conv_w"][l], 8)
        bf = jnp.pad(small["fox_b_f"][l], (0, 128 - N_HEADS)).reshape(1, 128)
        b_mat = jnp.repeat(small["sgu_b_s"][l].T, D_SGU // N_SGU_GROUPS, axis=1)
        mid_params = (cw, bf, vec(small["sgu_ln_g"][l]), vec(small["sgu_ln_b"][l]), small["sgu_w_s"][l], b_mat)
        ya, yc, cum_t = _mix_mid_fwd(conv, sgu, f, *mid_params, n_seq)
        yb, lse = _fox_fwd(qkv, cum_t, n_seq)
        h2, h2b, s["z2"] = _mix_out_fwd(ya, yb, yc, h1, s["w_out"], vec(small["ln2_g"][l]), vec(small["ln2_b"][l]))
        s.update(conv=conv, qkv=qkv, sgu=sgu, f=f, mid_params=mid_params, ya=ya, yb=yb, yc=yc, cum_t=cum_t, lse=lse, x2b=h2b)
        h3, _, s["z3"], s["gu2"], _ = _ffn_fwd(h2, s["up2"], s["down2"], vec(small["ln3_g"][l]), vec(small["ln3_b"][l]))
        saved.append(s)
        h = h3

    dh, loss = _loss_grad(h, target)

    small_grads = [None] * DEPTH
    token = loss
    pending = None
    for l in reversed(range(DEPTH)):
        s = saved[l]
        sg = {}
        dh, dy, a, dgu, sg["ln3_g"], sg["ln3_b"] = _ffn_bwd(dh, s["z3"], s["gu2"], s["up2"], s["down2"], vec(small["ln3_g"][l]), token)
        if pending is not None:
            token = comm.advance(pending, dh)
        g_up2 = _matmul_tn(s["x2b"][None], dgu.reshape(N_DEV, -1, FFN_BLK), token)
        g_down2 = _matmul_tn(a, dy[None], token).reshape(N_DEV, FFN_BLK // 2, D_MODEL)
        dz, dzb, dya, dyb, dyc, sg["ln2_g"], sg["ln2_b"] = _mix_out_bwd(dh, s["z2"], s["w_out"], vec(small["ln2_g"][l]))
        dwo = [_matmul_tn(y[None], dzb[None], token)[0] for y in (s["ya"], s["yb"], s["yc"])]
        g_out = jnp.concatenate(dwo, axis=0).reshape(N_DEV, D_MODEL // N_DEV, D_MODEL)
        dq, dk, dv, drow, dcol = _fox_bwd(s["qkv"], s["cum_t"], s["yb"], s["lse"], dyb, n_seq)
        dconv, dcw = _conv_bwd(s["conv"], dya, s["mid_params"][0], n_seq)
        dsgu, df, dbf, dlg, dlb, dws, dbs = _sgu_gate_bwd(s["sgu"], s["f"], dyc, drow, dcol, *s["mid_params"][1:], n_seq)
        sg.update(conv_w=dcw[:3], fox_b_f=dbf[0, :N_HEADS], sgu_ln_g=dlg[0], sgu_ln_b=dlb[0], sgu_w_s=dws,
                  sgu_b_s=dbs[:, :N_SGU_GROUPS].T)
        dh, dp = _mix_in_bwd(dconv, dq, dk, dv, dsgu, df, dz, s["w_in"])
        g_in = _w_in_to_blocks(_matmul_tn(s["x1b"][None], dp[None], token, tk=1024)[0])
        first = [("ffn2_w_up", l, g_up2), ("ffn2_w_down", l, g_down2), ("mix_w_out", l, g_out), ("mix_w_in", l, g_in)]
        if l == 0:
            token = comm.push((l, "a"), first)
            pending, first = (l, "a"), []
        dh, dy, a, dgu, sg["ln1_g"], sg["ln1_b"] = _ffn_bwd(dh, s["z1"], s["gu1"], s["up1"], s["down1"], vec(small["ln1_g"][l]), token)
        if l == 0:
            token = comm.advance(pending, dh)
        g_up1 = _matmul_tn(s["x0b"][None], dgu.reshape(N_DEV, -1, FFN_BLK), token)
        g_down1 = _matmul_tn(a, dy[None], token).reshape(N_DEV, FFN_BLK // 2, D_MODEL)
        key = (l, "b")
        token = comm.push(key, first + [("ffn1_w_up", l, g_up1), ("ffn1_w_down", l, g_down1)])
        pending = key
        if l == 0:
            token = comm.advance(key, token)
        for name in ("ln1_g", "ln1_b", "ln2_g", "ln2_b", "ln3_g", "ln3_b"):
            sg[name] = sg[name][0]
        small_grads[l] = sg
    return loss, dh, small_grads


def kernel(x, ln1_g, ln1_b, ffn1_w_up, ffn1_w_down, mix_w_in, fox_b_f, conv_w, sgu_ln_g, sgu_ln_b, sgu_w_s, sgu_b_s, mix_w_out, ln2_g, ln2_b, ffn2_w_up, ffn2_w_down, ln3_g, ln3_b, loss_target, m_ln1_g, m_ln1_b, m_ffn1_w_up, m_ffn1_w_down, m_mix_w_in, m_fox_b_f, m_conv_w, m_sgu_ln_g, m_sgu_ln_b, m_sgu_w_s, m_sgu_b_s, m_mix_w_out, m_ln2_g, m_ln2_b, m_ffn2_w_up, m_ffn2_w_down, m_ln3_g, m_ln3_b, v_ln1_g, v_ln1_b, v_ffn1_w_up, v_ffn1_w_down, v_mix_w_in, v_fox_b_f, v_conv_w, v_sgu_ln_g, v_sgu_ln_b, v_sgu_w_s, v_sgu_b_s, v_mix_w_out, v_ln2_g, v_ln2_b, v_ffn2_w_up, v_ffn2_w_down, v_ln3_g, v_ln3_b):
    w = dict(ln1_g=ln1_g, ln1_b=ln1_b, ffn1_w_up=ffn1_w_up, ffn1_w_down=ffn1_w_down, mix_w_in=mix_w_in, fox_b_f=fox_b_f,
             conv_w=conv_w, sgu_ln_g=sgu_ln_g, sgu_ln_b=sgu_ln_b, sgu_w_s=sgu_w_s, sgu_b_s=sgu_b_s, mix_w_out=mix_w_out,
             ln2_g=ln2_g, ln2_b=ln2_b, ffn2_w_up=ffn2_w_up, ffn2_w_down=ffn2_w_down, ln3_g=ln3_g, ln3_b=ln3_b)
    m = dict(ln1_g=m_ln1_g, ln1_b=m_ln1_b, ffn1_w_up=m_ffn1_w_up, ffn1_w_down=m_ffn1_w_down, mix_w_in=m_mix_w_in,
             fox_b_f=m_fox_b_f, conv_w=m_conv_w, sgu_ln_g=m_sgu_ln_g, sgu_ln_b=m_sgu_ln_b, sgu_w_s=m_sgu_w_s,
             sgu_b_s=m_sgu_b_s, mix_w_out=m_mix_w_out, ln2_g=m_ln2_g, ln2_b=m_ln2_b, ffn2_w_up=m_ffn2_w_up,
             ffn2_w_down=m_ffn2_w_down, ln3_g=m_ln3_g, ln3_b=m_ln3_b)
    v = dict(ln1_g=v_ln1_g, ln1_b=v_ln1_b, ffn1_w_up=v_ffn1_w_up, ffn1_w_down=v_ffn1_w_down, mix_w_in=v_mix_w_in,
             fox_b_f=v_fox_b_f, conv_w=v_conv_w, sgu_ln_g=v_sgu_ln_g, sgu_ln_b=v_sgu_ln_b, sgu_w_s=v_sgu_w_s,
             sgu_b_s=v_sgu_b_s, mix_w_out=v_mix_w_out, ln2_g=v_ln2_g, ln2_b=v_ln2_b, ffn2_w_up=v_ffn2_w_up,
             ffn2_w_down=v_ffn2_w_down, ln3_g=v_ln3_g, ln3_b=v_ln3_b)

    mx, my, mc = lax.axis_index("x"), lax.axis_index("y"), lax.axis_index("c")
    me = 4 * mx + 2 * my + mc
    n_seq, seq, _ = x.shape
    t_tok = n_seq * seq

    cw_rows = _pad_rows(conv_w.reshape(DEPTH * 3, D_CONV // N_DEV), 8)
    cw_all = _allgather_small(jnp.pad(cw_rows, ((0, 0), (0, 128 - D_CONV // N_DEV))))
    conv_w_full = jnp.transpose(cw_all[:, :DEPTH * 3, :D_CONV // N_DEV], (1, 0, 2)).reshape(DEPTH, 3, D_CONV)
    small = {name: w[name] for name in SMALL_NAMES}
    small["conv_w"] = conv_w_full

    comm = _Overlap(w, cw_all, me, mc.reshape(1).astype(jnp.int32))
    loss_dev, grad_x, small_grads = _local_step(
        x.reshape(t_tok, D_MODEL), loss_target.reshape(t_tok, D_MODEL), comm, small, n_seq)
    loss = lax.psum(loss_dev[0, 0], ("x", "y", "c"))
    out = comm.finish((2 * mx + my).reshape(1).astype(jnp.int32), w, m, v)

    sg = {name: jnp.stack([small_grads[l][name] for l in range(DEPTH)]) for name in SMALL_NAMES + ("conv_w",)}
    all_small = _allgather_small(_pack_small(sg))

    def widen(a):
        return lax.dynamic_update_slice(jnp.zeros((DEPTH, 3, D_CONV), F32), a, (0, 0, me * (D_CONV // N_DEV)))

    packed = [_pack_small({**{name: t[name] for name in SMALL_NAMES}, "conv_w": widen(t["conv_w"])}) for t in (w, m, v)]
    small_out = [_unpack_small(a) for a in _adamw_small(all_small, *packed)]
    for name in SMALL_NAMES:
        out[name] = [small_out[k][name] for k in range(4)]
    out["conv_w"] = [lax.dynamic_slice(small_out[k]["conv_w"], (0, 0, me * (D_CONV // N_DEV)), (DEPTH, 3, D_CONV // N_DEV))
                     for k in range(4)]

    return (loss, grad_x.reshape(x.shape), *[out[name][0] for name in WEIGHT_ORDER], *[out[name][1] for name in WEIGHT_ORDER],
            *[out[name][2] for name in WEIGHT_ORDER], *[out[name][3] for name in WEIGHT_ORDER])
```

```python
import functools

import jax
import jax.numpy as jnp
from jax import lax
from jax.experimental import pallas as pl
from jax.experimental.pallas import tpu as pltpu

F32 = jnp.float32
BF16 = jnp.bfloat16
MESH = pl.DeviceIdType.MESH

N_DEV = 8
DEPTH = 2
D_MODEL = 1024
D_FF = 2816
FFN_BLK = 2 * D_FF // N_DEV
N_FFN_CHUNK = D_FF // FFN_BLK
D_CONV = 256
D_FOX = 512
N_HEADS = 8
D_SGU = 256
N_SGU_GROUPS = 4
SGU_CHUNK = 128
D_IN = 3 * D_CONV + 3 * D_FOX + N_HEADS + 2 * D_SGU
D_IN_SHARD = D_IN // N_DEV
COL_CONV, COL_QKV, COL_SGU, COL_F = 0, 768, 2304, 2816
D_IN_PAD = 2944
F_ORIG = 3 * D_CONV + 3 * D_FOX
ALPHA = (2 * DEPTH) ** 0.25
LN_EPS = 1e-5
ATT_SCALE = 0.125
ATT_BLK = 512
NEG = -1e30

ADAM_LR, ADAM_B1, ADAM_B2, ADAM_EPS, ADAM_WD, ADAM_STEP = 0.001, 0.9, 0.999, 1e-08, 0.01, 10

VMEM_BYTES_V7X = 64 * 1024 * 1024
HIGHEST = lax.Precision.HIGHEST


def _params(vmem_mb, sem=None):
    assert vmem_mb * 1024 * 1024 < VMEM_BYTES_V7X
    kw = dict(vmem_limit_bytes=vmem_mb * 1024 * 1024)
    if sem is not None:
        kw["dimension_semantics"] = sem
    return pltpu.CompilerParams(**kw)


def _dot(a, b, precision=None):
    return lax.dot_general(a, b, (((1,), (0,)), ((), ())), preferred_element_type=F32, precision=precision)


def _dot_nt(a, b):
    return lax.dot_general(a, b, (((1,), (1,)), ((), ())), preferred_element_type=F32)


def _dot_tn(a, b):
    return lax.dot_general(a, b, (((0,), (0,)), ((), ())), preferred_element_type=F32)


def _ln_stats(z):
    mu = jnp.mean(z, axis=-1, keepdims=True)
    zc = z - mu
    var = jnp.mean(zc * zc, axis=-1, keepdims=True)
    rstd = lax.rsqrt(var + LN_EPS)
    return zc * rstd, rstd


def _ln_bwd(dy, xhat, rstd, g):
    dxh = dy * g
    m1 = jnp.mean(dxh, axis=-1, keepdims=True)
    m2 = jnp.mean(dxh * xhat, axis=-1, keepdims=True)
    return rstd * (dxh - m1 - xhat * m2)


_GELU_C = 0.7978845608028654


def _gelu(x):
    return 0.5 * x * (1.0 + jnp.tanh(_GELU_C * (x + 0.044715 * x * x * x)))


def _gelu_grad(x):
    t = jnp.tanh(_GELU_C * (x + 0.044715 * x * x * x))
    return 0.5 * (1.0 + t) + 0.5 * x * (1.0 - t * t) * _GELU_C * (1.0 + 3 * 0.044715 * x * x)


def _hbm(shape, dtype):
    n = 1
    for d in shape:
        n *= d
    if n * jnp.dtype(dtype).itemsize >= 1024 * 1024:
        return pltpu.HBM(tuple(shape), dtype)
    return jax.ShapeDtypeStruct(tuple(shape), dtype)


def _vspec():
    return pl.BlockSpec(memory_space=pltpu.VMEM)


def _anyspec():
    return pl.BlockSpec(memory_space=pl.ANY)


def _mesh_pos():
    return lax.axis_index("x"), lax.axis_index("y"), lax.axis_index("c")


def _other_chips(x, y):
    return [(1 - x, y), (x, 1 - y), (1 - x, 1 - y)]


_HBM_SPEC = pl.BlockSpec(memory_space=pltpu.HBM)
_SEM_SPEC = pl.BlockSpec(memory_space=pltpu.SEMAPHORE)
_DATAFLOW_EFFECT = pltpu.SideEffectType.DATAFLOW_SIDE_EFFECTING


def _remote_copies(plan, refs, send_sems, recv_sems):
    return [pltpu.make_async_remote_copy(src_ref=src, dst_ref=dst, send_sem=send_sems.at[k], recv_sem=recv_sems.at[k],
                                         device_id=to, device_id_type=MESH)
            for k, (src, dst, to) in enumerate(plan(refs, *_mesh_pos()))]


def _exchange_start(name, plan, n_copies, arrays, after):
    n = len(arrays)

    def body(*refs):
        send_sems, recv_sems, token = refs[n + 1], refs[n + 2], refs[-1]
        for cp in _remote_copies(plan, refs[:n], send_sems, recv_sems):
            cp.start()
        token[...] = jnp.zeros_like(token)

    out = pl.pallas_call(
        body, name=name,
        out_shape=(pltpu.SemaphoreType.DMA((n_copies,)), pltpu.SemaphoreType.DMA((n_copies,)),
                   *[pltpu.HBM(a.shape, a.dtype) for a in arrays], _hbm((8, 128), F32)),
        in_specs=[_HBM_SPEC] * n + [_anyspec()],
        out_specs=(_SEM_SPEC, _SEM_SPEC, *[_HBM_SPEC] * n, _vspec()),
        input_output_aliases={i: 2 + i for i in range(n)},
        compiler_params=pltpu.CompilerParams(has_side_effects=_DATAFLOW_EFFECT),
    )(*[pltpu.with_memory_space_constraint(a, pltpu.HBM) for a in arrays], after)
    return out[0], out[1], list(out[2:2 + n]), out[-1]


def _exchange_wait(name, plan, n_copies, started, after):
    send_sems, recv_sems, arrays, _ = started
    n = len(arrays)

    def body(*refs):
        for cp in _remote_copies(plan, refs[:n], refs[n], refs[n + 1]):
            cp.wait_send()
            cp.wait_recv()

    out = pl.pallas_call(
        body, name=name,
        out_shape=tuple(pltpu.HBM(a.shape, a.dtype) for a in arrays),
        in_specs=[_HBM_SPEC] * n + [_SEM_SPEC, _SEM_SPEC, _anyspec()], out_specs=tuple([_HBM_SPEC] * n),
        input_output_aliases={i: i for i in range(n)},
        compiler_params=pltpu.CompilerParams(has_side_effects=_DATAFLOW_EFFECT),
    )(*arrays, send_sems, recv_sems, after)
    return list(out)


def _gather_plan(m):
    def plan(refs, x, y, c):
        me = 4 * x + 2 * y + c
        return [(refs[i], refs[m + i].at[me], (*chip, c)) for i in range(m) for chip in _other_chips(x, y)]
    return plan


def _pass_on_plan(m):
    def plan(refs, x, y, c):
        out = []
        for i in range(m):
            out.append((refs[i], refs[m + i].at[4 * x + 2 * y + c], (x, y, 1 - c)))
            for cx, cy in _other_chips(x, y):
                block = refs[m + i].at[4 * cx + 2 * cy + c]
                out.append((block, block, (x, y, 1 - c)))
        return out
    return plan


def _peers_plan():
    def plan(refs, x, y, c):
        rel = [(dx, dy, dc) for dx in (0, 1) for dy in (0, 1) for dc in (0, 1)][1:]
        return [(refs[0], refs[1].at[4 * x + 2 * y + c], (x ^ dx, y ^ dy, c ^ dc)) for dx, dy, dc in rel]
    return plan


def _allgather_small(v):
    rows = v.shape[0]

    def body(v_ref, out_ref, send_sems, recv_sems):
        x, y, c = _mesh_pos()
        me = 4 * x + 2 * y + c
        out_ref[me] = v_ref[...]
        rel = [(dx, dy, dc) for dx in (0, 1) for dy in (0, 1) for dc in (0, 1)][1:]
        copies = []
        for k, (dx, dy, dc) in enumerate(rel):
            to = (x ^ dx, y ^ dy, c ^ dc)
            copies.append(pltpu.make_async_remote_copy(
                src_ref=v_ref, dst_ref=out_ref.at[me], send_sem=send_sems.at[k], recv_sem=recv_sems.at[k],
                device_id=to, device_id_type=MESH))
        for cp in copies:
            cp.start()
        for k, (dx, dy, dc) in enumerate(rel):
            src_blk = 4 * (x ^ dx) + 2 * (y ^ dy) + (c ^ dc)
            pltpu.make_async_remote_copy(
                src_ref=v_ref, dst_ref=out_ref.at[src_blk], send_sem=send_sems.at[k], recv_sem=recv_sems.at[k],
                device_id=(x, y, c), device_id_type=MESH).wait_recv()
        for cp in copies:
            cp.wait_send()

    return pl.pallas_call(
        body, name="allgather_small",
        out_shape=jax.ShapeDtypeStruct((N_DEV, rows, 128), v.dtype),
        in_specs=[_vspec()], out_specs=_vspec(),
        scratch_shapes=[pltpu.SemaphoreType.DMA((7,)), pltpu.SemaphoreType.DMA((7,))],
        compiler_params=_params(24),
    )(v)


def _sibling_plan(n):
    def plan(refs, x, y, c):
        return [(refs[a].at[2 * q + (1 - c)], refs[n + a].at[q], (x, y, 1 - c)) for a in range(n) for q in range(4)]
    return plan


def _chip_plan(n):
    def plan(refs, x, y, c):
        return [(refs[a].at[2 * cx + cy], refs[n + a].at[j], (cx, cy, c))
                for a in range(n) for j, (cx, cy) in enumerate(_other_chips(x, y))]
    return plan


def _row_tile(rows, cols, budget_bytes=2 * 1024 * 1024):
    best = 8
    for t in range(8, rows + 1, 8):
        if rows % t == 0 and t * cols * 4 <= budget_bytes:
            best = t
    return best


def _chip_partial(g, recv, where):
    _, rows, cols = g.shape
    tr = _row_tile(rows, cols)

    def body(where_ref, g_ref, r_ref, own_ref, o16_ref):
        s = g_ref[...] + r_ref[...]
        o16_ref[...] = s.astype(BF16)

        @pl.when(pl.program_id(1) == where_ref[1])
        def _():
            own_ref[...] = s

    blk = (None, tr, cols)
    return pl.pallas_call(
        body, name="rs_chip_partial",
        grid_spec=pltpu.PrefetchScalarGridSpec(
            num_scalar_prefetch=1, grid=(rows // tr, 4),
            in_specs=[pl.BlockSpec(blk, lambda i, q, w: (2 * q + w[0], i, 0)),
                      pl.BlockSpec(blk, lambda i, q, w: (q, i, 0))],
            out_specs=[pl.BlockSpec((tr, cols), lambda i, q, w: (i, 0)), pl.BlockSpec(blk, lambda i, q, w: (q, i, 0))]),
        out_shape=[_hbm((rows, cols), F32), _hbm((4, rows, cols), BF16)],
        compiler_params=_params(32),
    )(where, g, recv)


def _adam_math(w, g, m, v):
    m = ADAM_B1 * m + (1.0 - ADAM_B1) * g
    v = ADAM_B2 * v + (1.0 - ADAM_B2) * (g * g)
    m_hat = m / (1.0 - ADAM_B1 ** ADAM_STEP)
    v_hat = v / (1.0 - ADAM_B2 ** ADAM_STEP)
    delta = -ADAM_LR * (m_hat / (jnp.sqrt(v_hat) + ADAM_EPS) + ADAM_WD * w)
    return delta, m, v


def _adamw_shard(own32, recv16, w, m, v, layer, earlier):
    depth, rows, cols = w.shape
    tr = _row_tile(rows, cols, 1024 * 1024)
    n_prev = 0 if earlier is None else 4

    def body(p_ref, r_ref, w_ref, m_ref, v_ref, *rest):
        g_out, d_out, m_out, v_out = rest[n_prev:]
        g = p_ref[...] + r_ref[0].astype(F32) + r_ref[1].astype(F32) + r_ref[2].astype(F32)
        d, mn, vn = _adam_math(w_ref[...], g, m_ref[...], v_ref[...])
        g_out[...] = g
        d_out[...] = d
        m_out[...] = mn
        v_out[...] = vn

    mine = pl.BlockSpec((None, tr, cols), lambda i: (layer, i, 0))
    return pl.pallas_call(
        body, name="adamw_shard", grid=(rows // tr,),
        in_specs=[pl.BlockSpec((tr, cols), lambda i: (i, 0)), pl.BlockSpec((3, tr, cols), lambda i: (0, i, 0)),
                  mine, mine, mine] + [_anyspec()] * n_prev,
        out_specs=[mine] * 4,
        out_shape=[_hbm((depth, rows, cols), F32)] * 4,
        input_output_aliases={5 + k: k for k in range(n_prev)},
        compiler_params=_params(32),
    )(own32, recv16, w, m, v, *([] if earlier is None else earlier))


def _adamw_small(gathered, w, m, v):
    rows = w.shape[0]

    def body(a_ref, w_ref, m_ref, v_ref, g_out, d_out, m_out, v_out):
        g = a_ref[0]
        for d in range(1, N_DEV):
            g = g + a_ref[d]
        dl, mn, vn = _adam_math(w_ref[...], g, m_ref[...], v_ref[...])
        g_out[...] = g
        d_out[...] = dl
        m_out[...] = mn
        v_out[...] = vn

    return pl.pallas_call(
        body, name="adamw_small",
        in_specs=[_vspec()] * 4, out_specs=[_vspec()] * 4,
        out_shape=[_hbm((rows, 128), F32)] * 4,
        compiler_params=_params(32),
    )(gathered, w, m, v)


def _load_weights_once(pairs, sems):
    @pl.when(pl.program_id(0) == 0)
    def _():
        cps = [pltpu.make_async_copy(src, dst, sems.at[i]) for i, (src, dst) in enumerate(pairs)]
        for cp in cps:
            cp.start()
        for cp in cps:
            cp.wait()


def _ffn_fwd(x, wup, wd, ln_g, ln_b, after, tm=512):
    t_tok = x.shape[0]

    def body(x_ref, g_ref, b_ref, wup_hbm, wd_hbm, _after, xn_ref, xnb_ref, z_ref, gu_ref, xb_ref, wup_v, wd_v, sems):
        _load_weights_once([(wup_hbm, wup_v), (wd_hbm, wd_v)], sems)
        xb = x_ref[...].astype(BF16)
        xb_ref[...] = xb
        y = None
        for j in range(N_FFN_CHUNK):
            g = _dot(xb, wup_v[j])
            u = _dot(xb, wup_v[N_FFN_CHUNK + j])
            gu_ref[0, j] = g.astype(BF16)
            gu_ref[1, j] = u.astype(BF16)
            a = (g * jax.nn.sigmoid(g) * u).astype(BF16)
            part = _dot(a, wd_v[j])
            y = part if y is None else y + part
        z = ALPHA * x_ref[...] + 0.5 * y
        xhat, _ = _ln_stats(z)
        xn = xhat * g_ref[...] + b_ref[...]
        z_ref[...] = z
        xn_ref[...] = xn
        xnb_ref[...] = xn.astype(BF16)

    tok = pl.BlockSpec((tm, D_MODEL), lambda i: (i, 0))
    vec = pl.BlockSpec((1, D_MODEL), lambda i: (0, 0))
    return pl.pallas_call(
        body, name="ffn_fwd", grid=(t_tok // tm,),
        in_specs=[tok, vec, vec, _anyspec(), _anyspec(), _anyspec()],
        out_specs=[tok, tok, tok, pl.BlockSpec((2, N_FFN_CHUNK, tm, FFN_BLK), lambda i: (0, 0, i, 0)), tok],
        out_shape=[_hbm((t_tok, D_MODEL), F32), _hbm((t_tok, D_MODEL), BF16),
                   _hbm((t_tok, D_MODEL), F32),
                   _hbm((2, N_FFN_CHUNK, t_tok, FFN_BLK), BF16),
                   _hbm((t_tok, D_MODEL), BF16)],
        scratch_shapes=[pltpu.VMEM((N_DEV, D_MODEL, FFN_BLK), BF16), pltpu.VMEM((N_FFN_CHUNK, FFN_BLK, D_MODEL), BF16),
                        pltpu.SemaphoreType.DMA((2,))],
        compiler_params=_params(62, ("arbitrary",)),
    )(x, ln_g, ln_b, wup, wd, after)


def _ffn_bwd(dxn, z, gu, wup, wd, ln_g, after, tm=256):
    t_tok = dxn.shape[0]

    def body(dxn_ref, z_ref, gu_ref, g_ref, wup_hbm, wd_hbm, _after,
             dx_ref, dy_ref, a_ref, dgu_ref, dg_ref, db_ref, wup_v, wd_v, sems):
        i = pl.program_id(0)
        _load_weights_once([(wup_hbm, wup_v), (wd_hbm, wd_v)], sems)
        dxn_t = dxn_ref[...]
        xhat, rstd = _ln_stats(z_ref[...])
        pg = jnp.sum(dxn_t * xhat, axis=0, keepdims=True)
        pb = jnp.sum(dxn_t, axis=0, keepdims=True)

        @pl.when(i == 0)
        def _():
            dg_ref[...] = pg
            db_ref[...] = pb

        @pl.when(i > 0)
        def _():
            dg_ref[...] += pg
            db_ref[...] += pb

        dz = _ln_bwd(dxn_t, xhat, rstd, g_ref[...])
        dy = (0.5 * dz).astype(BF16)
        dy_ref[...] = dy
        dx = ALPHA * dz
        for j in range(N_FFN_CHUNK):
            da = _dot_nt(dy, wd_v[j])
            g = gu_ref[0, j].astype(F32)
            u = gu_ref[1, j].astype(F32)
            sig = jax.nn.sigmoid(g)
            silu = g * sig
            a_ref[j] = (silu * u).astype(BF16)
            dg = (da * u * (sig * (1.0 + g * (1.0 - sig)))).astype(BF16)
            du = (da * silu).astype(BF16)
            dgu_ref[0, j] = dg
            dgu_ref[1, j] = du
            dx = dx + _dot_nt(dg, wup_v[j]) + _dot_nt(du, wup_v[N_FFN_CHUNK + j])
        dx_ref[...] = dx

    tok = pl.BlockSpec((tm, D_MODEL), lambda i: (i, 0))
    vec = pl.BlockSpec((1, D_MODEL), lambda i: (0, 0))
    gu_spec = pl.BlockSpec((2, N_FFN_CHUNK, tm, FFN_BLK), lambda i: (0, 0, i, 0))
    return pl.pallas_call(
        body, name="ffn_bwd", grid=(t_tok // tm,),
        in_specs=[tok, tok, gu_spec, vec, _anyspec(), _anyspec(), _anyspec()],
        out_specs=[tok, tok, pl.BlockSpec((N_FFN_CHUNK, tm, FFN_BLK), lambda i: (0, i, 0)), gu_spec, vec, vec],
        out_shape=[_hbm((t_tok, D_MODEL), F32), _hbm((t_tok, D_MODEL), BF16),
                   _hbm((N_FFN_CHUNK, t_tok, FFN_BLK), BF16),
                   _hbm((2, N_FFN_CHUNK, t_tok, FFN_BLK), BF16),
                   _hbm((1, D_MODEL), F32), _hbm((1, D_MODEL), F32)],
        scratch_shapes=[pltpu.VMEM((N_DEV, D_MODEL, FFN_BLK), BF16), pltpu.VMEM((N_FFN_CHUNK, FFN_BLK, D_MODEL), BF16),
                        pltpu.SemaphoreType.DMA((2,))],
        compiler_params=_params(60, ("arbitrary",)),
    )(dxn, z, gu, ln_g, wup, wd, after)


def _matmul_tn(a, b, after, tk=4096):
    ga, t_tok, m = a.shape
    gb, _, n = b.shape
    groups = max(ga, gb)
    tk = min(tk, t_tok)

    def body(a_ref, b_ref, _after, o_ref):
        p = _dot_tn(a_ref[...].astype(BF16), b_ref[...].astype(BF16))

        @pl.when(pl.program_id(1) == 0)
        def _():
            o_ref[...] = p

        @pl.when(pl.program_id(1) > 0)
        def _():
            o_ref[...] += p

    return pl.pallas_call(
        body, name=f"matmul_tn_{m}x{n}", grid=(groups, t_tok // tk),
        in_specs=[pl.BlockSpec((None, tk, m), (lambda g, t: (g, t, 0)) if ga > 1 else (lambda g, t: (0, t, 0))),
                  pl.BlockSpec((None, tk, n), (lambda g, t: (g, t, 0)) if gb > 1 else (lambda g, t: (0, t, 0))),
                  _anyspec()],
        out_specs=pl.BlockSpec((None, m, n), lambda g, t: (g, 0, 0)),
        out_shape=_hbm((groups, m, n), F32),
        compiler_params=_params(56, ("arbitrary", "arbitrary")),
    )(a, b, after)


def _in_proj(x, w_in, tm=512):
    t_tok = x.shape[0]

    def body(x_ref, w_ref, conv_ref, qkv_ref, sgu_ref, f_ref):
        xb = x_ref[...].astype(BF16)
        conv_ref[...] = _dot(xb, w_ref[:, COL_CONV:COL_QKV])
        qkv_ref[...] = _dot(xb, w_ref[:, COL_QKV:COL_SGU]).astype(BF16)
        sgu_ref[...] = _dot(xb, w_ref[:, COL_SGU:COL_F])
        f_ref[...] = _dot(xb, w_ref[:, COL_F:D_IN_PAD])

    def tok(n):
        return pl.BlockSpec((tm, n), lambda i: (i, 0))

    return pl.pallas_call(
        body, name="mix_in_proj", grid=(t_tok // tm,),
        in_specs=[tok(D_MODEL), pl.BlockSpec((D_MODEL, D_IN_PAD), lambda i: (0, 0))],
        out_specs=[tok(768), tok(1536), tok(512), tok(128)],
        out_shape=[_hbm((t_tok, 768), F32), _hbm((t_tok, 1536), BF16),
                   _hbm((t_tok, 512), F32), _hbm((t_tok, 128), F32)],
        compiler_params=_params(48, ("arbitrary",)),
    )(x, w_in)


def _shift_down(a, k):
    row = lax.broadcasted_iota(jnp.int32, a.shape, 0)
    return jnp.where(row >= k, pltpu.roll(a, k, 0), 0.0)


def _shift_up(a, k):
    rows = a.shape[0]
    row = lax.broadcasted_iota(jnp.int32, a.shape, 0)
    return jnp.where(row < rows - k, pltpu.roll(a, rows - k, 0), 0.0)


def _tril(n):
    return lax.broadcasted_iota(jnp.int32, (n, n), 0) >= lax.broadcasted_iota(jnp.int32, (n, n), 1)


def _sgu_group_of_lane():
    return lax.broadcasted_iota(jnp.int32, (1, D_SGU), 1) // (D_SGU // N_SGU_GROUPS)


def _log_sigmoid(x):
    return jnp.minimum(x, 0.0) - jnp.log1p(jnp.exp(-jnp.abs(x)))


def _mix_mid_fwd(conv, sgu, f, conv_w, b_f, sgu_g, sgu_b, w_s, b_mat, n_seq):
    t_tok = conv.shape[0]
    seq = t_tok // n_seq
    n_chunk = seq // SGU_CHUNK
    per_blk = ATT_BLK // SGU_CHUNK

    def body(conv_ref, sgu_ref, f_ref, cw_ref, bf_ref, lg_ref, lb_ref, ws_ref, bm_ref, ya_ref, yc_ref, cum_ref):
        z = conv_ref[:, 256:512] * conv_ref[:, 512:768]
        y = cw_ref[0:1, :] * _shift_down(z, 2) + cw_ref[1:2, :] * _shift_down(z, 1) + cw_ref[2:3, :] * z
        ya_ref[...] = (conv_ref[:, 0:256] * y).astype(BF16)

        tril = _tril(SGU_CHUNK)
        grp = _sgu_group_of_lane()
        wc = [jnp.where(tril, ws_ref[g], 0.0).astype(BF16) for g in range(N_SGU_GROUPS)]
        tri_f = tril.astype(F32)
        carry = jnp.zeros((1, 128), F32)
        for n in range(n_chunk):
            rows = pl.ds(n * SGU_CHUNK, SGU_CHUNK)
            u = _gelu(sgu_ref[rows, 0:256])
            vhat, _ = _ln_stats(_gelu(sgu_ref[rows, 256:512]))
            vn = (vhat * lg_ref[...] + lb_ref[...]).astype(BF16)
            mixed = bm_ref[...]
            for g in range(N_SGU_GROUPS):
                mixed = mixed + jnp.where(grp == g, _dot(wc[g], vn), 0.0)
            yc_ref[rows, :] = (u * mixed).astype(BF16)

            log_f = _log_sigmoid(f_ref[rows, :] + bf_ref[...])
            cs = _dot(tri_f, log_f, HIGHEST) + carry
            carry = cs[SGU_CHUNK - 1:SGU_CHUNK, :]
            cs_t = cs.T
            lanes = pl.ds((n % per_blk) * SGU_CHUNK, SGU_CHUNK)
            for h in range(N_HEADS):
                cum_ref[h, n // per_blk, :, lanes] = cs_t[h:h + 1, :]

    def seq_blk(n):
        return pl.BlockSpec((seq, n), lambda b: (b, 0))

    def full(shape):
        return pl.BlockSpec(shape, lambda b: (0,) * len(shape))

    return pl.pallas_call(
        body, name="mix_mid_fwd", grid=(n_seq,),
        in_specs=[seq_blk(768), seq_blk(512), seq_blk(128), full((8, 256)), full((1, 128)), full((1, 256)),
                  full((1, 256)), full((4, 128, 128)), full((128, 256))],
        out_specs=[seq_blk(256), seq_blk(256),
                   pl.BlockSpec((N_HEADS, seq // ATT_BLK, 1, ATT_BLK), lambda b: (b, 0, 0, 0))],
        out_shape=[_hbm((t_tok, 256), BF16), _hbm((t_tok, 256), BF16),
                   _hbm((n_seq * N_HEADS, seq // ATT_BLK, 1, ATT_BLK), F32)],
        compiler_params=_params(48, ("arbitrary",)),
    )(conv, sgu, f, conv_w, b_f, sgu_g, sgu_b, w_s, b_mat)


def _head_masks():
    lane = lax.broadcasted_iota(jnp.int32, (1, 128), 1)
    return lane < 64, lane


def _fox_fwd(qkv, cum_t, n_seq):
    t_tok = qkv.shape[0]
    seq = t_tok // n_seq
    nq = seq // ATT_BLK
    blk = ATT_BLK

    def body(q_ref, k_ref, v_ref, c0_ref, c1_ref, o_ref, lse_ref):
        qi = pl.program_id(2)
        first, _ = _head_masks()
        qs = q_ref[...] * ATT_SCALE
        zero = jnp.zeros_like(qs)
        q0 = jnp.where(first, qs, zero)
        q1 = jnp.where(first, zero, qs)
        causal = _tril(blk)
        one = jnp.ones((1, 128), BF16)

        def step(kb, carry, masked):
            m0, m1, acc0, acc1 = carry
            rows = pl.ds(pl.multiple_of(kb * blk, blk), blk)
            k = k_ref[rows, :]
            v = v_ref[rows, :]

            def head(qh, c_ref, m, acc, vh):
                s = _dot_nt(qh, k) - c_ref[kb]
                if masked:
                    s = jnp.where(causal, s, NEG)
                m_new = jnp.maximum(m, jnp.max(s, axis=1, keepdims=True))
                p = jnp.exp(s - m_new)
                return m_new, acc * jnp.exp(m - m_new) + _dot(p.astype(BF16), vh)

            m0, acc0 = head(q0, c0_ref, m0, acc0, jnp.where(first, v, one))
            m1, acc1 = head(q1, c1_ref, m1, acc1, jnp.where(first, one, v))
            return m0, m1, acc0, acc1

        col = jnp.full((blk, 1), NEG, F32)
        zacc = jnp.zeros((blk, 128), F32)
        carry = lax.fori_loop(0, qi, lambda kb, cr: step(kb, cr, False), (col, col, zacc, zacc))
        m0, m1, acc0, acc1 = step(qi, carry, True)
        l0 = pltpu.roll(acc0, 64, 1)
        l1 = pltpu.roll(acc1, 64, 1)
        o_ref[...] = jnp.where(first, acc0 / l0, acc1 / l1).astype(BF16)
        lse_ref[...] = jnp.where(first, m0 + jnp.log(l0), m1 + jnp.log(l1))

    cum_spec0 = pl.BlockSpec((None, nq, 1, blk), lambda b, hp, qi: (b * N_HEADS + 2 * hp, 0, 0, 0))
    cum_spec1 = pl.BlockSpec((None, nq, 1, blk), lambda b, hp, qi: (b * N_HEADS + 2 * hp + 1, 0, 0, 0))
    out_spec = pl.BlockSpec((blk, 128), lambda b, hp, qi: (b * nq + qi, hp))
    return pl.pallas_call(
        body, name="fox_fwd", grid=(n_seq, 4, nq),
        in_specs=[pl.BlockSpec((blk, 128), lambda b, hp, qi: (b * nq + qi, hp)),
                  pl.BlockSpec((seq, 128), lambda b, hp, qi: (b, 4 + hp)),
                  pl.BlockSpec((seq, 128), lambda b, hp, qi: (b, 8 + hp)), cum_spec0, cum_spec1],
        out_specs=[out_spec, out_spec],
        out_shape=[_hbm((t_tok, D_FOX), BF16), _hbm((t_tok, D_FOX), F32)],
        compiler_params=_params(32, ("arbitrary", "arbitrary", "arbitrary")),
    )(qkv, qkv, qkv, cum_t, cum_t)


def _fox_bwd(qkv, cum_t, o, lse, d_o, n_seq):
    t_tok = qkv.shape[0]
    seq = t_tok // n_seq
    nk = seq // ATT_BLK
    blk = ATT_BLK

    def body(q_ref, k_ref, v_ref, c0_ref, c1_ref, o_ref, lse_ref, do_ref,
             dq_ref, dk_ref, dv_ref, drow_ref, dcol_ref):
        kb = pl.program_id(2)
        first, lane = _head_masks()
        second = jnp.logical_not(first)
        k = k_ref[...]
        v = v_ref[...]
        zero = jnp.zeros_like(k)
        one = jnp.ones((1, 128), BF16)
        ks = k * ATT_SCALE
        causal = _tril(blk)

        @pl.when(kb == 0)
        def _():
            dq_ref[...] = jnp.zeros_like(dq_ref)
            drow_ref[...] = jnp.zeros_like(drow_ref)

        def step(qi, carry, masked):
            rows = pl.ds(pl.multiple_of(qi * blk, blk), blk)
            qs = q_ref[rows, :] * ATT_SCALE
            d_o = do_ref[rows, :]
            dd = d_o.astype(F32) * o_ref[rows, :].astype(F32)
            lse_t = lse_ref[rows, :]

            def head(mine, c, lse_lane, dk, dv):
                qh = jnp.where(mine, qs, zero)
                doh = jnp.where(mine, d_o, zero)
                delta = jnp.sum(jnp.where(mine, dd, 0.0), axis=1, keepdims=True)
                lse_h = jnp.sum(jnp.where(lane == lse_lane, lse_t, 0.0), axis=1, keepdims=True)
                s = _dot_nt(qh, k) - c
                if masked:
                    s = jnp.where(causal, s, NEG)
                p = jnp.exp(s - lse_h)
                ds = (p * (_dot_nt(doh, v) - delta)).astype(BF16)
                dk = dk + _dot_tn(ds, jnp.where(mine, qs, one))
                dv = dv + _dot_tn(p.astype(BF16), doh)
                return dk, dv, _dot(ds, jnp.where(mine, ks, one))

            dk0, dv0, dk1, dv1 = carry
            dk0, dv0, dq0 = head(first, c0_ref[...], 0, dk0, dv0)
            dk1, dv1, dq1 = head(second, c1_ref[...], 64, dk1, dv1)
            dq_ref[rows, :] += jnp.where(first, dq0, dq1)
            drow_ref[rows, :] += jnp.where(first, dq1, dq0)
            return dk0, dv0, dk1, dv1

        zt = jnp.zeros((blk, 128), F32)
        carry = step(kb, (zt, zt, zt, zt), True)
        dk0, dv0, dk1, dv1 = lax.fori_loop(kb + 1, nk, lambda qi, cr: step(qi, cr, False), carry)
        dk_ref[...] = jnp.where(first, dk0, dk1).astype(BF16)
        dcol_ref[...] = jnp.where(first, dk1, dk0)
        dv_ref[...] = (dv0 + dv1).astype(BF16)

    def seq_spec(col0):
        return pl.BlockSpec((seq, 128), lambda b, hp, kb: (b, col0 + hp))

    def key_spec(col0):
        return pl.BlockSpec((blk, 128), lambda b, hp, kb: (b * nk + kb, col0 + hp))

    def cum_spec(h):
        return pl.BlockSpec((None, None, 1, blk), lambda b, hp, kb: (b * N_HEADS + 2 * hp + h, kb, 0, 0))

    return pl.pallas_call(
        body, name="fox_bwd", grid=(n_seq, 4, nk),
        in_specs=[seq_spec(0), key_spec(4), key_spec(8), cum_spec(0), cum_spec(1), seq_spec(0), seq_spec(0), seq_spec(0)],
        out_specs=[seq_spec(0), key_spec(0), key_spec(0), seq_spec(0), key_spec(0)],
        out_shape=[_hbm((t_tok, D_FOX), F32), _hbm((t_tok, D_FOX), BF16),
                   _hbm((t_tok, D_FOX), BF16), _hbm((t_tok, D_FOX), F32),
                   _hbm((t_tok, D_FOX), F32)],
        compiler_params=_params(48, ("arbitrary", "arbitrary", "arbitrary")),
    )(qkv, qkv, qkv, cum_t, cum_t, o, lse, d_o)


def _mix_out_fwd(ya, yb, yc, x, w_out, ln_g, ln_b, tm=512):
    t_tok = x.shape[0]

    def body(ya_ref, yb_ref, yc_ref, x_ref, w_ref, g_ref, b_ref, xn_ref, xnb_ref, z_ref):
        mo = _dot(ya_ref[...], w_ref[0:256, :]) + _dot(yb_ref[...], w_ref[256:768, :]) + _dot(yc_ref[...], w_ref[768:1024, :])
        z = ALPHA * x_ref[...] + mo
        xhat, _ = _ln_stats(z)
        xn = xhat * g_ref[...] + b_ref[...]
        z_ref[...] = z
        xn_ref[...] = xn
        xnb_ref[...] = xn.astype(BF16)

    def tok(n):
        return pl.BlockSpec((tm, n), lambda i: (i, 0))

    vec = pl.BlockSpec((1, D_MODEL), lambda i: (0, 0))
    return pl.pallas_call(
        body, name="mix_out_fwd", grid=(t_tok // tm,),
        in_specs=[tok(256), tok(512), tok(256), tok(D_MODEL),
                  pl.BlockSpec((D_MODEL, D_MODEL), lambda i: (0, 0)), vec, vec],
        out_specs=[tok(D_MODEL)] * 3,
        out_shape=[_hbm((t_tok, D_MODEL), F32), _hbm((t_tok, D_MODEL), BF16),
                   _hbm((t_tok, D_MODEL), F32)],
        compiler_params=_params(40, ("arbitrary",)),
    )(ya, yb, yc, x, w_out, ln_g, ln_b)


def _mix_out_bwd(dxn, z, w_out, ln_g, tm=512):
    t_tok = dxn.shape[0]

    def body(dxn_ref, z_ref, w_ref, g_ref, dz_ref, dzb_ref, dya_ref, dyb_ref, dyc_ref, dg_ref, db_ref):
        i = pl.program_id(0)
        dxn_t = dxn_ref[...]
        xhat, rstd = _ln_stats(z_ref[...])
        pg = jnp.sum(dxn_t * xhat, axis=0, keepdims=True)
        pb = jnp.sum(dxn_t, axis=0, keepdims=True)

        @pl.when(i == 0)
        def _():
            dg_ref[...] = pg
            db_ref[...] = pb

        @pl.when(i > 0)
        def _():
            dg_ref[...] += pg
            db_ref[...] += pb

        dz = _ln_bwd(dxn_t, xhat, rstd, g_ref[...])
        dzb = dz.astype(BF16)
        dz_ref[...] = dz
        dzb_ref[...] = dzb
        dya_ref[...] = _dot_nt(dzb, w_ref[0:256, :])
        dyb_ref[...] = _dot_nt(dzb, w_ref[256:768, :]).astype(BF16)
        dyc_ref[...] = _dot_nt(dzb, w_ref[768:1024, :])

    def tok(n):
        return pl.BlockSpec((tm, n), lambda i: (i, 0))

    vec = pl.BlockSpec((1, D_MODEL), lambda i: (0, 0))
    return pl.pallas_call(
        body, name="mix_out_bwd", grid=(t_tok // tm,),
        in_specs=[tok(D_MODEL), tok(D_MODEL), pl.BlockSpec((D_MODEL, D_MODEL), lambda i: (0, 0)), vec],
        out_specs=[tok(D_MODEL), tok(D_MODEL), tok(256), tok(512), tok(256), vec, vec],
        out_shape=[_hbm((t_tok, D_MODEL), F32), _hbm((t_tok, D_MODEL), BF16),
                   _hbm((t_tok, 256), F32), _hbm((t_tok, 512), BF16),
                   _hbm((t_tok, 256), F32),
                   _hbm((1, D_MODEL), F32), _hbm((1, D_MODEL), F32)],
        compiler_params=_params(40, ("arbitrary",)),
    )(dxn, z, w_out, ln_g)


def _conv_bwd(conv, dya, conv_w, n_seq):
    t_tok = conv.shape[0]
    seq = t_tok // n_seq

    def body(conv_ref, dya_ref, cw_ref, dconv_ref, dcw_ref):
        @pl.when(pl.program_id(0) == 0)
        def _():
            dcw_ref[...] = jnp.zeros_like(dcw_ref)

        z = conv_ref[:, 256:512] * conv_ref[:, 512:768]
        z1 = _shift_down(z, 1)
        z2 = _shift_down(z, 2)
        y = cw_ref[0:1, :] * z2 + cw_ref[1:2, :] * z1 + cw_ref[2:3, :] * z
        dya_t = dya_ref[...]
        dconv_ref[:, 0:256] = (dya_t * y).astype(BF16)
        dy = dya_t * conv_ref[:, 0:256]
        dcw_ref[0:1, :] += jnp.sum(dy * z2, axis=0, keepdims=True)
        dcw_ref[1:2, :] += jnp.sum(dy * z1, axis=0, keepdims=True)
        dcw_ref[2:3, :] += jnp.sum(dy * z, axis=0, keepdims=True)
        dz = cw_ref[2:3, :] * dy + cw_ref[1:2, :] * _shift_up(dy, 1) + cw_ref[0:1, :] * _shift_up(dy, 2)
        dconv_ref[:, 256:512] = (dz * conv_ref[:, 512:768]).astype(BF16)
        dconv_ref[:, 512:768] = (dz * conv_ref[:, 256:512]).astype(BF16)

    def seq_blk(n):
        return pl.BlockSpec((seq, n), lambda b: (b, 0))

    par = pl.BlockSpec((8, 256), lambda b: (0, 0))
    return pl.pallas_call(
        body, name="conv_bwd", grid=(n_seq,),
        in_specs=[seq_blk(768), seq_blk(256), par], out_specs=[seq_blk(768), par],
        out_shape=[_hbm((t_tok, 768), BF16), _hbm((8, 256), F32)],
        compiler_params=_params(56, ("arbitrary",)),
    )(conv, dya, conv_w)


def _sgu_gate_bwd(sgu, f, dyc, drow, dcol, b_f, sgu_g, sgu_b, w_s, b_mat, n_seq):
    t_tok = sgu.shape[0]
    seq = t_tok // n_seq
    n_chunk = seq // SGU_CHUNK

    def body(sgu_ref, f_ref, dyc_ref, drow_ref, dcol_ref, bf_ref, lg_ref, lb_ref, ws_ref, bm_ref,
             dsgu_ref, df_ref, dbf_ref, dlg_ref, dlb_ref, dws_ref, dbs_ref, dbm_acc):
        b = pl.program_id(0)

        @pl.when(b == 0)
        def _():
            for r in (dbf_ref, dlg_ref, dlb_ref, dws_ref, dbm_acc):
                r[...] = jnp.zeros_like(r)

        tril = _tril(SGU_CHUNK)
        grp = _sgu_group_of_lane()
        wc = [jnp.where(tril, ws_ref[g], 0.0).astype(BF16) for g in range(N_SGU_GROUPS)]
        for n in range(n_chunk):
            rows = pl.ds(n * SGU_CHUNK, SGU_CHUNK)
            su = sgu_ref[rows, 0:256]
            sv = sgu_ref[rows, 256:512]
            u = _gelu(su)
            vhat, rstd = _ln_stats(_gelu(sv))
            vn = (vhat * lg_ref[...] + lb_ref[...]).astype(BF16)
            mixed = bm_ref[...]
            for g in range(N_SGU_GROUPS):
                mixed = mixed + jnp.where(grp == g, _dot(wc[g], vn), 0.0)
            dyc_t = dyc_ref[rows, :]
            dsgu_ref[rows, 0:256] = (dyc_t * mixed * _gelu_grad(su)).astype(BF16)
            dmixed = dyc_t * u
            dbm_acc[...] += dmixed
            dvn = jnp.zeros((SGU_CHUNK, D_SGU), F32)
            for g in range(N_SGU_GROUPS):
                dm_g = jnp.where(grp == g, dmixed, 0.0).astype(BF16)
                dws_ref[g] += _dot_nt(dm_g, vn)
                dvn = dvn + _dot_tn(wc[g], dm_g)
            dlg_ref[...] += jnp.sum(dvn * vhat, axis=0, keepdims=True)
            dlb_ref[...] += jnp.sum(dvn, axis=0, keepdims=True)
            dsgu_ref[rows, 256:512] = (_ln_bwd(dvn, vhat, rstd, lg_ref[...]) * _gelu_grad(sv)).astype(BF16)

        later = (lax.broadcasted_iota(jnp.int32, (128, 128), 0) <= lax.broadcasted_iota(jnp.int32, (128, 128), 1)).astype(F32)
        head = lax.broadcasted_iota(jnp.int32, (D_FOX, 128), 1)
        pick = (lax.broadcasted_iota(jnp.int32, (D_FOX, 128), 0) == 128 * (head // 2) + 64 * (1 - head % 2)).astype(F32)
        carry = jnp.zeros((1, 128), F32)
        for n in reversed(range(n_chunk)):
            rows = pl.ds(n * SGU_CHUNK, SGU_CHUNK)
            dcum_n = _dot(drow_ref[rows, :] - dcol_ref[rows, :], pick, HIGHEST)
            dlf = _dot(later, dcum_n, HIGHEST) + carry
            carry = carry + jnp.sum(dcum_n, axis=0, keepdims=True)
            df = dlf * jax.nn.sigmoid(-(f_ref[rows, :] + bf_ref[...]))
            df_ref[rows, :] = df.astype(BF16)
            dbf_ref[...] += jnp.sum(df, axis=0, keepdims=True)

        @pl.when(b == n_seq - 1)
        def _():
            for g in range(N_SGU_GROUPS):
                dws_ref[g] = jnp.where(tril, dws_ref[g], 0.0)
            sel = (lax.broadcasted_iota(jnp.int32, (D_SGU, 128), 0) // (D_SGU // N_SGU_GROUPS)
                   == lax.broadcasted_iota(jnp.int32, (D_SGU, 128), 1)).astype(F32)
            dbs_ref[...] = _dot(dbm_acc[...], sel, HIGHEST)

    def seq_blk(n):
        return pl.BlockSpec((seq, n), lambda b: (b, 0))

    def full(shape):
        return pl.BlockSpec(shape, lambda b: (0,) * len(shape))

    param_shapes = [(1, 128), (1, 256), (1, 256), (4, 128, 128), (128, 128)]
    return pl.pallas_call(
        body, name="sgu_gate_bwd", grid=(n_seq,),
        in_specs=[seq_blk(512), seq_blk(128), seq_blk(256), seq_blk(D_FOX), seq_blk(D_FOX),
                  full((1, 128)), full((1, 256)), full((1, 256)), full((4, 128, 128)), full((128, 256))],
        out_specs=[seq_blk(512), seq_blk(128)] + [full(s) for s in param_shapes],
        out_shape=[_hbm((t_tok, 512), BF16), _hbm((t_tok, 128), BF16)]
        + [_hbm(s, F32) for s in param_shapes],
        scratch_shapes=[pltpu.VMEM((128, 256), F32)],
        compiler_params=_params(48, ("arbitrary",)),
    )(sgu, f, dyc, drow, dcol, b_f, sgu_g, sgu_b, w_s, b_mat)


def _mix_in_bwd(dconv, dq, dk, dv, dsgu, df, dz, w_in, tm=512):
    t_tok = dz.shape[0]

    def body(dconv_ref, dq_ref, dk_ref, dv_ref, dsgu_ref, df_ref, dz_ref, w_ref, dx_ref, dp_ref):
        dqb = dq_ref[...].astype(BF16)
        pieces = [(COL_CONV, dconv_ref[...]), (COL_QKV, dqb), (COL_QKV + 512, dk_ref[...]), (COL_QKV + 1024, dv_ref[...]),
                  (COL_SGU, dsgu_ref[...]), (COL_F, df_ref[...])]
        dx = ALPHA * dz_ref[...]
        for col, val in pieces:
            width = val.shape[1]
            dp_ref[:, col:col + width] = val
            dx = dx + _dot_nt(val, w_ref[:, col:col + width])
        dx_ref[...] = dx

    def tok(n):
        return pl.BlockSpec((tm, n), lambda i: (i, 0))

    return pl.pallas_call(
        body, name="mix_in_bwd", grid=(t_tok // tm,),
        in_specs=[tok(768), tok(512), tok(512), tok(512), tok(512), tok(128), tok(D_MODEL),
                  pl.BlockSpec((D_MODEL, D_IN_PAD), lambda i: (0, 0))],
        out_specs=[tok(D_MODEL), tok(D_IN_PAD)],
        out_shape=[_hbm((t_tok, D_MODEL), F32), _hbm((t_tok, D_IN_PAD), BF16)],
        compiler_params=_params(48, ("arbitrary",)),
    )(dconv, dq, dk, dv, dsgu, df, dz, w_in)


def _loss_grad(y, target, tm=512):
    t_tok = y.shape[0]

    def body(y_ref, t_ref, dy_ref, loss_ref):
        err = y_ref[...] - t_ref[...]
        dy_ref[...] = err * (1.0 / D_MODEL)
        part = jnp.sum(jnp.sum(err * err, axis=1, keepdims=True), axis=0, keepdims=True) * (0.5 / D_MODEL)

        @pl.when(pl.program_id(0) == 0)
        def _():
            loss_ref[...] = jnp.zeros_like(loss_ref)

        loss_ref[...] += part

    tok = pl.BlockSpec((tm, D_MODEL), lambda i: (i, 0))
    return pl.pallas_call(
        body, name="loss_grad", grid=(t_tok // tm,),
        in_specs=[tok, tok], out_specs=[tok, pl.BlockSpec((1, 128), lambda i: (0, 0))],
        out_shape=[_hbm((t_tok, D_MODEL), F32), _hbm((1, 128), F32)],
        compiler_params=_params(32, ("arbitrary",)),
    )(y, target)


def _pad_rows(a, rows):
    return jnp.pad(a, ((0, rows - a.shape[0]), (0, 0)))


F_BLOCK = F_ORIG // D_IN_SHARD
F_AT = F_ORIG - F_BLOCK * D_IN_SHARD
assert (F_ORIG + N_HEADS) // D_IN_SHARD == F_BLOCK


def _w_in_from_blocks(g):
    fb = g[F_BLOCK]
    zeros = jnp.zeros((D_MODEL, D_IN_PAD - COL_F - N_HEADS), g.dtype)
    return jnp.concatenate([g[d] for d in range(F_BLOCK)] + [fb[:, :F_AT], fb[:, F_AT + N_HEADS:]]
                           + [g[d] for d in range(F_BLOCK + 1, N_DEV)] + [fb[:, F_AT:F_AT + N_HEADS], zeros], axis=1)


def _w_in_to_blocks(dw):
    def cols(lo, hi):
        shift = 0 if hi <= F_ORIG else N_HEADS
        return dw[:, lo - shift:hi - shift]

    blocks = []
    for d in range(N_DEV):
        lo, hi = d * D_IN_SHARD, (d + 1) * D_IN_SHARD
        if d == F_BLOCK:
            blocks.append(jnp.concatenate([cols(lo, F_ORIG), dw[:, COL_F:COL_F + N_HEADS], cols(F_ORIG + N_HEADS, hi)], axis=1))
        else:
            blocks.append(cols(lo, hi))
    return jnp.stack(blocks)


LN1_ROWS = 2 * 8
REST_ROWS = 4 * 8 + 2 * 8 + 512 + 8 + 8 + 8


def _pack_rest(p):
    rows = [p[name].reshape(8, 128) for name in ("ln2_g", "ln2_b", "ln3_g", "ln3_b")]
    rows += [_pad_rows(p[name].reshape(2, 128), 8) for name in ("sgu_ln_g", "sgu_ln_b")]
    rows += [p["sgu_w_s"].reshape(512, 128), _pad_rows(p["sgu_b_s"], 8),
             _pad_rows(jnp.pad(p["fox_b_f"], (0, 128 - N_HEADS)).reshape(1, 128), 8), _pad_rows(p["conv_w"].reshape(6, 128), 8)]
    return jnp.concatenate(rows, axis=0)


def _pack_layer(p):
    return jnp.concatenate([p["ln1_g"].reshape(8, 128), p["ln1_b"].reshape(8, 128), _pack_rest(p)], axis=0)


def _unpack_layer(a):
    r = 0

    def take(n, valid):
        nonlocal r
        piece = a[r:r + valid]
        r += n
        return piece

    d = {}
    for name in ("ln1_g", "ln1_b", "ln2_g", "ln2_b", "ln3_g", "ln3_b"):
        d[name] = take(8, 8).reshape(D_MODEL)
    for name in ("sgu_ln_g", "sgu_ln_b"):
        d[name] = take(8, 2).reshape(D_SGU)
    d["sgu_w_s"] = take(512, 512).reshape(N_SGU_GROUPS, SGU_CHUNK, SGU_CHUNK)
    d["sgu_b_s"] = take(8, 4).reshape(N_SGU_GROUPS, SGU_CHUNK)
    d["fox_b_f"] = take(8, 1).reshape(128)[:N_HEADS]
    d["conv_w"] = take(8, 6).reshape(3, D_CONV)
    return d


SMALL_NAMES = ("ln1_g", "ln1_b", "fox_b_f", "sgu_ln_g", "sgu_ln_b", "sgu_w_s", "sgu_b_s", "ln2_g", "ln2_b", "ln3_g", "ln3_b")
BIG_NAMES = ("ffn1_w_up", "ffn1_w_down", "mix_w_in", "mix_w_out", "ffn2_w_up", "ffn2_w_down")
WEIGHT_ORDER = ("ln1_g", "ln1_b", "ffn1_w_up", "ffn1_w_down", "mix_w_in", "fox_b_f", "conv_w", "sgu_ln_g", "sgu_ln_b",
                "sgu_w_s", "sgu_b_s", "mix_w_out", "ln2_g", "ln2_b", "ffn2_w_up", "ffn2_w_down", "ln3_g", "ln3_b")


class _Overlap:
    def __init__(self, w, after, me, where):
        self.me, self.where = me, where
        groups = [[("ffn1_w_up", 0), ("ffn1_w_down", 0)],
                  [("mix_w_in", 0), ("mix_w_out", 0), ("ffn2_w_up", 0), ("ffn2_w_down", 0)]]
        groups += [[(name, l) for name in BIG_NAMES] for l in range(1, DEPTH)]
        self.gathers = []
        for gi, group in enumerate(groups):
            shards = [w[name][l].astype(BF16) for name, l in group]
            lands = [lax.dynamic_update_slice(lax.empty((N_DEV,) + s.shape, BF16), s[None], (me, 0, 0)) for s in shards]
            started = _exchange_start(f"allgather_start_{gi}", _gather_plan(len(group)), 3 * len(group), shards + lands, after)
            after = started[3]
            self.gathers.append(dict(group=group, chips=started))
        self.all_started = self.last = after
        self.scatters = {}
        self.order = []
        self.small = []

    def _start(self, name, plan, n_copies, arrays):
        started = _exchange_start(name, plan, n_copies, arrays, self.last)
        self.last = started[3]
        return started

    def _group_of(self, layer, part):
        return layer + 1 if layer > 0 else (0 if part == "ffn1" else 1)

    def pass_on(self, layer, part, after):
        st = self.gathers[self._group_of(layer, part)]
        if "sibling" not in st:
            gi, m = self._group_of(layer, part), len(st["group"])
            arrays = _exchange_wait(f"allgather_wait_{gi}", _gather_plan(m), 3 * m, st["chips"], after)
            st["sibling"] = self._start(f"allgather_pass_start_{gi}", _pass_on_plan(m), 4 * m, arrays)
        return st["sibling"][3]

    def weights(self, layer, part, after):
        gi = self._group_of(layer, part)
        st = self.gathers[gi]
        if "full" not in st:
            after = self.all_started if after is None else after
            self.pass_on(layer, part, after)
            m = len(st["group"])
            arrays = _exchange_wait(f"allgather_pass_wait_{gi}", _pass_on_plan(m), 4 * m, st["sibling"], after)
            st["full"] = dict(zip(st["group"], arrays[m:]))
        g = st["full"]

        def ffn(n):
            return g[(f"ffn{n}_w_up", layer)], g[(f"ffn{n}_w_down", layer)].reshape(N_FFN_CHUNK, FFN_BLK, D_MODEL)

        if part == "ffn1":
            return ffn(1)
        return (_w_in_from_blocks(g[("mix_w_in", layer)]), g[("mix_w_out", layer)].reshape(D_MODEL, D_MODEL), *ffn(2))

    def push(self, key, items):
        n = len(items)
        grads = [g for _, _, g in items]
        lands = [lax.empty((4,) + g.shape[1:], F32) for g in grads]
        started = self._start(f"rs_sibling_start_{key[0]}{key[1]}", _sibling_plan(n), 4 * n, grads + lands)
        self.scatters[key] = dict(items=items, sibling=started)
        self.order.append(key)
        return started[3]

    def advance(self, key, after):
        st = self.scatters[key]
        n = len(st["items"])
        arrays = _exchange_wait(f"rs_sibling_wait_{key[0]}{key[1]}", _sibling_plan(n), 4 * n, st["sibling"], after)
        partials = [_chip_partial(g, r, self.where) for g, r in zip(arrays[:n], arrays[n:])]
        p16 = [p for _, p in partials]
        lands = [lax.empty((3,) + p.shape[1:], BF16) for p in p16]
        started = self._start(f"rs_chip_start_{key[0]}{key[1]}", _chip_plan(n), 3 * n, p16 + lands)
        st.update(own32=[p for p, _ in partials], chip=started)
        return started[3]

    def push_small(self, rows):
        k = len(self.small)
        land = lax.dynamic_update_slice(lax.empty((N_DEV,) + rows.shape, F32), rows[None], (self.me, 0, 0))
        started = self._start(f"small_start_{k}", _peers_plan(), N_DEV - 1, [rows, land])
        self.small.append(started)
        return started[3]

    def finish(self, w, m, v):
        res = {}
        after = self.scatters[self.order[-1]]["chip"][3]
        for key in self.order:
            st = self.scatters[key]
            n = len(st["items"])
            arrays = _exchange_wait(f"rs_chip_wait_{key[0]}{key[1]}", _chip_plan(n), 3 * n, st["chip"], after)
            for (name, l, _), own32, r16 in zip(st["items"], st["own32"], arrays[n:]):
                res[name] = _adamw_shard(own32, r16, w[name], m[name], v[name], l, res.get(name))
                after = res[name][0]
        pieces = [_exchange_wait(f"small_wait_{k}", _peers_plan(), N_DEV - 1, started, after)[1]
                  for k, started in enumerate(self.small)]
        return res, pieces


def _local_step(x, target, comm, small, n_seq):
    def vec(a):
        return a.reshape(1, -1)

    saved = []
    h = x
    for l in range(DEPTH):
        s = {}
        s["up1"], s["down1"] = comm.weights(l, "ffn1", None if l == 0 else h)
        h1, h1b, s["z1"], s["gu1"], s["x0b"] = _ffn_fwd(h, s["up1"], s["down1"], vec(small["ln1_g"][l]), vec(small["ln1_b"][l]), h)
        s["w_in"], s["w_out"], s["up2"], s["down2"] = comm.weights(l, "rest", s["z1"])
        s["x1b"] = h1b
        conv, qkv, sgu, f = _in_proj(h1, s["w_in"])
        cw = _pad_rows(small["conv_w"][l], 8)
        bf = jnp.pad(small["fox_b_f"][l], (0, 128 - N_HEADS)).reshape(1, 128)
        b_mat = jnp.repeat(small["sgu_b_s"][l].T, D_SGU // N_SGU_GROUPS, axis=1)
        mid_params = (cw, bf, vec(small["sgu_ln_g"][l]), vec(small["sgu_ln_b"][l]), small["sgu_w_s"][l], b_mat)
        ya, yc, cum_t = _mix_mid_fwd(conv, sgu, f, *mid_params, n_seq)
        yb, lse = _fox_fwd(qkv, cum_t, n_seq)
        h2, h2b, s["z2"] = _mix_out_fwd(ya, yb, yc, h1, s["w_out"], vec(small["ln2_g"][l]), vec(small["ln2_b"][l]))
        s.update(conv=conv, qkv=qkv, sgu=sgu, f=f, mid_params=mid_params, ya=ya, yb=yb, yc=yc, cum_t=cum_t, lse=lse, x2b=h2b)
        token = comm.pass_on(l + 1, "ffn1", s["z2"]) if l + 1 < DEPTH else h2
        h3, _, s["z3"], s["gu2"], _ = _ffn_fwd(h2, s["up2"], s["down2"], vec(small["ln3_g"][l]), vec(small["ln3_b"][l]), token)
        saved.append(s)
        h = h3

    dh, loss = _loss_grad(h, target)

    late_rows = None
    token = loss
    pending = None
    for l in reversed(range(DEPTH)):
        s = saved[l]
        sg = {}
        dh, dy, a, dgu, sg["ln3_g"], sg["ln3_b"] = _ffn_bwd(dh, s["z3"], s["gu2"], s["up2"], s["down2"], vec(small["ln3_g"][l]), token)
        if pending is not None:
            token = comm.advance(pending, dh)
        g_up2 = _matmul_tn(s["x2b"][None], dgu.reshape(N_DEV, -1, FFN_BLK), token)
        g_down2 = _matmul_tn(a, dy[None], token).reshape(N_DEV, FFN_BLK // 2, D_MODEL)
        dz, dzb, dya, dyb, dyc, sg["ln2_g"], sg["ln2_b"] = _mix_out_bwd(dh, s["z2"], s["w_out"], vec(small["ln2_g"][l]))
        dwo = [_matmul_tn(y[None], dzb[None], token)[0] for y in (s["ya"], s["yb"], s["yc"])]
        g_out = jnp.concatenate(dwo, axis=0).reshape(N_DEV, D_MODEL // N_DEV, D_MODEL)
        dq, dk, dv, drow, dcol = _fox_bwd(s["qkv"], s["cum_t"], s["yb"], s["lse"], dyb, n_seq)
        dconv, dcw = _conv_bwd(s["conv"], dya, s["mid_params"][0], n_seq)
        dsgu, df, dbf, dlg, dlb, dws, dbs = _sgu_gate_bwd(s["sgu"], s["f"], dyc, drow, dcol, *s["mid_params"][1:], n_seq)
        sg.update(conv_w=dcw[:3], fox_b_f=dbf[0, :N_HEADS], sgu_ln_g=dlg[0], sgu_ln_b=dlb[0], sgu_w_s=dws,
                  sgu_b_s=dbs[:, :N_SGU_GROUPS].T)
        dh, dp = _mix_in_bwd(dconv, dq, dk, dv, dsgu, df, dz, s["w_in"])
        g_in = _w_in_to_blocks(_matmul_tn(s["x1b"][None], dp[None], token, tk=1024)[0])
        first = [("ffn2_w_up", l, g_up2), ("ffn2_w_down", l, g_down2), ("mix_w_out", l, g_out), ("mix_w_in", l, g_in)]
        for name in ("ln2_g", "ln2_b", "ln3_g", "ln3_b"):
            sg[name] = sg[name][0]
        if l == 0:
            comm.push((l, "a"), first)
            token = comm.push_small(_pack_rest(sg))
            pending, first = (l, "a"), []
        dh, dy, a, dgu, dg1, db1 = _ffn_bwd(dh, s["z1"], s["gu1"], s["up1"], s["down1"], vec(small["ln1_g"][l]), token)
        if l == 0:
            token = comm.advance(pending, dh)
        g_up1 = _matmul_tn(s["x0b"][None], dgu.reshape(N_DEV, -1, FFN_BLK), token)
        g_down1 = _matmul_tn(a, dy[None], token).reshape(N_DEV, FFN_BLK // 2, D_MODEL)
        key = (l, "b")
        token = comm.push(key, first + [("ffn1_w_up", l, g_up1), ("ffn1_w_down", l, g_down1)])
        pending = key
        ln1_rows = jnp.concatenate([dg1.reshape(8, 128), db1.reshape(8, 128)], axis=0)
        if l == 0:
            token = comm.advance(key, token)
            late_rows = ln1_rows
        else:
            token = comm.push_small(jnp.concatenate([ln1_rows, _pack_rest(sg)], axis=0))
    return loss, dh, late_rows


def kernel(x, ln1_g, ln1_b, ffn1_w_up, ffn1_w_down, mix_w_in, fox_b_f, conv_w, sgu_ln_g, sgu_ln_b, sgu_w_s, sgu_b_s, mix_w_out, ln2_g, ln2_b, ffn2_w_up, ffn2_w_down, ln3_g, ln3_b, loss_target, m_ln1_g, m_ln1_b, m_ffn1_w_up, m_ffn1_w_down, m_mix_w_in, m_fox_b_f, m_conv_w, m_sgu_ln_g, m_sgu_ln_b, m_sgu_w_s, m_sgu_b_s, m_mix_w_out, m_ln2_g, m_ln2_b, m_ffn2_w_up, m_ffn2_w_down, m_ln3_g, m_ln3_b, v_ln1_g, v_ln1_b, v_ffn1_w_up, v_ffn1_w_down, v_mix_w_in, v_fox_b_f, v_conv_w, v_sgu_ln_g, v_sgu_ln_b, v_sgu_w_s, v_sgu_b_s, v_mix_w_out, v_ln2_g, v_ln2_b, v_ffn2_w_up, v_ffn2_w_down, v_ln3_g, v_ln3_b):
    w = dict(ln1_g=ln1_g, ln1_b=ln1_b, ffn1_w_up=ffn1_w_up, ffn1_w_down=ffn1_w_down, mix_w_in=mix_w_in, fox_b_f=fox_b_f,
             conv_w=conv_w, sgu_ln_g=sgu_ln_g, sgu_ln_b=sgu_ln_b, sgu_w_s=sgu_w_s, sgu_b_s=sgu_b_s, mix_w_out=mix_w_out,
             ln2_g=ln2_g, ln2_b=ln2_b, ffn2_w_up=ffn2_w_up, ffn2_w_down=ffn2_w_down, ln3_g=ln3_g, ln3_b=ln3_b)
    m = dict(ln1_g=m_ln1_g, ln1_b=m_ln1_b, ffn1_w_up=m_ffn1_w_up, ffn1_w_down=m_ffn1_w_down, mix_w_in=m_mix_w_in,
             fox_b_f=m_fox_b_f, conv_w=m_conv_w, sgu_ln_g=m_sgu_ln_g, sgu_ln_b=m_sgu_ln_b, sgu_w_s=m_sgu_w_s,
             sgu_b_s=m_sgu_b_s, mix_w_out=m_mix_w_out, ln2_g=m_ln2_g, ln2_b=m_ln2_b, ffn2_w_up=m_ffn2_w_up,
             ffn2_w_down=m_ffn2_w_down, ln3_g=m_ln3_g, ln3_b=m_ln3_b)
    v = dict(ln1_g=v_ln1_g, ln1_b=v_ln1_b, ffn1_w_up=v_ffn1_w_up, ffn1_w_down=v_ffn1_w_down, mix_w_in=v_mix_w_in,
             fox_b_f=v_fox_b_f, conv_w=v_conv_w, sgu_ln_g=v_sgu_ln_g, sgu_ln_b=v_sgu_ln_b, sgu_w_s=v_sgu_w_s,
             sgu_b_s=v_sgu_b_s, mix_w_out=v_mix_w_out, ln2_g=v_ln2_g, ln2_b=v_ln2_b, ffn2_w_up=v_ffn2_w_up,
             ffn2_w_down=v_ffn2_w_down, ln3_g=v_ln3_g, ln3_b=v_ln3_b)

    mx, my, mc = lax.axis_index("x"), lax.axis_index("y"), lax.axis_index("c")
    me = 4 * mx + 2 * my + mc
    n_seq, seq, _ = x.shape
    t_tok = n_seq * seq

    cw_rows = _pad_rows(conv_w.reshape(DEPTH * 3, D_CONV // N_DEV), 8)
    cw_all = _allgather_small(jnp.pad(cw_rows, ((0, 0), (0, 128 - D_CONV // N_DEV))))
    conv_w_full = jnp.transpose(cw_all[:, :DEPTH * 3, :D_CONV // N_DEV], (1, 0, 2)).reshape(DEPTH, 3, D_CONV)
    small = {name: w[name] for name in SMALL_NAMES}
    small["conv_w"] = conv_w_full

    comm = _Overlap(w, cw_all, me, jnp.stack([mc, 2 * mx + my]).astype(jnp.int32))
    loss_dev, grad_x, late_rows = _local_step(
        x.reshape(t_tok, D_MODEL), loss_target.reshape(t_tok, D_MODEL), comm, small, n_seq)
    loss = lax.psum(loss_dev[0, 0], ("x", "y", "c"))
    out, pieces = comm.finish(w, m, v)

    pieces.append(_allgather_small(late_rows))
    spans = [(l, 0, LN1_ROWS + REST_ROWS) for l in reversed(range(1, DEPTH))] + [(0, LN1_ROWS, LN1_ROWS + REST_ROWS), (0, 0, LN1_ROWS)]

    def widen(a):
        return lax.dynamic_update_slice(jnp.zeros((3, D_CONV), F32), a, (0, me * (D_CONV // N_DEV)))

    packed = [[_pack_layer({**{name: t[name][l] for name in SMALL_NAMES}, "conv_w": widen(t["conv_w"][l])}) for l in range(DEPTH)]
              for t in (w, m, v)]
    rows_out = {}
    for (l, lo, hi), gathered_piece in zip(spans, pieces):
        rows_out[(l, lo)] = _adamw_small(gathered_piece, *[packed[t][l][lo:hi] for t in range(3)])
    per_layer = []
    for l in range(DEPTH):
        parts = sorted(lo for (ll, lo) in rows_out if ll == l)
        per_layer.append([_unpack_layer(jnp.concatenate([rows_out[(l, lo)][k] for lo in parts], axis=0)) for k in range(4)])
    for name in SMALL_NAMES:
        out[name] = [jnp.stack([per_layer[l][k][name] for l in range(DEPTH)]) for k in range(4)]
    lo_col = me * (D_CONV // N_DEV)
    out["conv_w"] = [jnp.stack([lax.dynamic_slice(per_layer[l][k]["conv_w"], (0, lo_col), (3, D_CONV // N_DEV)) for l in range(DEPTH)])
                     for k in range(4)]

    return (loss, grad_x.reshape(x.shape), *[out[name][0] for name in WEIGHT_ORDER], *[out[name][1] for name in WEIGHT_ORDER],
            *[out[name][2] for name in WEIGHT_ORDER], *[out[name][3] for name in WEIGHT_ORDER])
```

```python
import functools

import jax
import jax.numpy as jnp
from jax import lax
from jax.experimental import pallas as pl
from jax.experimental.pallas import tpu as pltpu

F32 = jnp.float32
BF16 = jnp.bfloat16
MESH = pl.DeviceIdType.MESH

N_DEV = 8
DEPTH = 2
D_MODEL = 1024
D_FF = 2816
FFN_BLK = 2 * D_FF // N_DEV
N_FFN_CHUNK = D_FF // FFN_BLK
D_CONV = 256
D_FOX = 512
N_HEADS = 8
D_SGU = 256
N_SGU_GROUPS = 4
SGU_CHUNK = 128
D_IN = 3 * D_CONV + 3 * D_FOX + N_HEADS + 2 * D_SGU
D_IN_SHARD = D_IN // N_DEV
COL_CONV, COL_QKV, COL_SGU, COL_F = 0, 768, 2304, 2816
D_IN_PAD = 2944
F_ORIG = 3 * D_CONV + 3 * D_FOX
ALPHA = (2 * DEPTH) ** 0.25
LN_EPS = 1e-5
ATT_SCALE = 0.125
ATT_BLK = 512
NEG = -1e30

ADAM_LR, ADAM_B1, ADAM_B2, ADAM_EPS, ADAM_WD, ADAM_STEP = 0.001, 0.9, 0.999, 1e-08, 0.01, 10

VMEM_BYTES_V7X = 64 * 1024 * 1024
HIGHEST = lax.Precision.HIGHEST


def _params(vmem_mb, sem=None):
    assert vmem_mb * 1024 * 1024 < VMEM_BYTES_V7X
    kw = dict(vmem_limit_bytes=vmem_mb * 1024 * 1024)
    if sem is not None:
        kw["dimension_semantics"] = sem
    return pltpu.CompilerParams(**kw)


def _dot(a, b, precision=None):
    return lax.dot_general(a, b, (((1,), (0,)), ((), ())), preferred_element_type=F32, precision=precision)


def _dot_nt(a, b):
    return lax.dot_general(a, b, (((1,), (1,)), ((), ())), preferred_element_type=F32)


def _dot_tn(a, b):
    return lax.dot_general(a, b, (((0,), (0,)), ((), ())), preferred_element_type=F32)


def _ln_stats(z):
    mu = jnp.mean(z, axis=-1, keepdims=True)
    zc = z - mu
    var = jnp.mean(zc * zc, axis=-1, keepdims=True)
    rstd = lax.rsqrt(var + LN_EPS)
    return zc * rstd, rstd


def _ln_bwd(dy, xhat, rstd, g):
    dxh = dy * g
    m1 = jnp.mean(dxh, axis=-1, keepdims=True)
    m2 = jnp.mean(dxh * xhat, axis=-1, keepdims=True)
    return rstd * (dxh - m1 - xhat * m2)


_GELU_C = 0.7978845608028654


def _gelu(x):
    return 0.5 * x * (1.0 + jnp.tanh(_GELU_C * (x + 0.044715 * x * x * x)))


def _gelu_grad(x):
    t = jnp.tanh(_GELU_C * (x + 0.044715 * x * x * x))
    return 0.5 * (1.0 + t) + 0.5 * x * (1.0 - t * t) * _GELU_C * (1.0 + 3 * 0.044715 * x * x)


def _hbm(shape, dtype):
    n = 1
    for d in shape:
        n *= d
    if n * jnp.dtype(dtype).itemsize >= 1024 * 1024:
        return pltpu.HBM(tuple(shape), dtype)
    return jax.ShapeDtypeStruct(tuple(shape), dtype)


def _vspec():
    return pl.BlockSpec(memory_space=pltpu.VMEM)


def _anyspec():
    return pl.BlockSpec(memory_space=pl.ANY)


def _mesh_pos():
    return lax.axis_index("x"), lax.axis_index("y"), lax.axis_index("c")


def _other_chips(x, y):
    return [(1 - x, y), (x, 1 - y), (1 - x, 1 - y)]


_HBM_SPEC = pl.BlockSpec(memory_space=pltpu.HBM)
_SEM_SPEC = pl.BlockSpec(memory_space=pltpu.SEMAPHORE)
_DATAFLOW_EFFECT = pltpu.SideEffectType.DATAFLOW_SIDE_EFFECTING


def _remote_copies(plan, refs, send_sems, recv_sems):
    return [pltpu.make_async_remote_copy(src_ref=src, dst_ref=dst, send_sem=send_sems.at[k], recv_sem=recv_sems.at[k],
                                         device_id=to, device_id_type=MESH)
            for k, (src, dst, to) in enumerate(plan(refs, *_mesh_pos()))]


def _exchange_start(name, plan, n_copies, arrays, after):
    n = len(arrays)

    def body(*refs):
        send_sems, recv_sems, token = refs[n + 1], refs[n + 2], refs[-1]
        for cp in _remote_copies(plan, refs[:n], send_sems, recv_sems):
            cp.start()
        token[...] = jnp.zeros_like(token)

    out = pl.pallas_call(
        body, name=name,
        out_shape=(pltpu.SemaphoreType.DMA((n_copies,)), pltpu.SemaphoreType.DMA((n_copies,)),
                   *[pltpu.HBM(a.shape, a.dtype) for a in arrays], _hbm((8, 128), F32)),
        in_specs=[_HBM_SPEC] * n + [_anyspec()],
        out_specs=(_SEM_SPEC, _SEM_SPEC, *[_HBM_SPEC] * n, _vspec()),
        input_output_aliases={i: 2 + i for i in range(n)},
        compiler_params=pltpu.CompilerParams(has_side_effects=_DATAFLOW_EFFECT),
    )(*[pltpu.with_memory_space_constraint(a, pltpu.HBM) for a in arrays], after)
    return out[0], out[1], list(out[2:2 + n]), out[-1]


def _exchange_wait(name, plan, n_copies, started, after):
    send_sems, recv_sems, arrays, _ = started
    n = len(arrays)

    def body(*refs):
        for cp in _remote_copies(plan, refs[:n], refs[n], refs[n + 1]):
            cp.wait_send()
            cp.wait_recv()

    out = pl.pallas_call(
        body, name=name,
        out_shape=tuple(pltpu.HBM(a.shape, a.dtype) for a in arrays),
        in_specs=[_HBM_SPEC] * n + [_SEM_SPEC, _SEM_SPEC, _anyspec()], out_specs=tuple([_HBM_SPEC] * n),
        input_output_aliases={i: i for i in range(n)},
        compiler_params=pltpu.CompilerParams(has_side_effects=_DATAFLOW_EFFECT),
    )(*arrays, send_sems, recv_sems, after)
    return list(out)


def _gather_plan(m):
    def plan(refs, x, y, c):
        me = 4 * x + 2 * y + c
        return [(refs[i], refs[m + i].at[me], (*chip, c)) for i in range(m) for chip in _other_chips(x, y)]
    return plan


def _pass_on_plan(m):
    def plan(refs, x, y, c):
        out = []
        for i in range(m):
            out.append((refs[i], refs[m + i].at[4 * x + 2 * y + c], (x, y, 1 - c)))
            for cx, cy in _other_chips(x, y):
                block = refs[m + i].at[4 * cx + 2 * cy + c]
                out.append((block, block, (x, y, 1 - c)))
        return out
    return plan


def _peers_plan():
    def plan(refs, x, y, c):
        rel = [(dx, dy, dc) for dx in (0, 1) for dy in (0, 1) for dc in (0, 1)][1:]
        return [(refs[0], refs[1].at[4 * x + 2 * y + c], (x ^ dx, y ^ dy, c ^ dc)) for dx, dy, dc in rel]
    return plan


def _allgather_small(v):
    rows = v.shape[0]

    def body(v_ref, out_ref, send_sems, recv_sems):
        x, y, c = _mesh_pos()
        me = 4 * x + 2 * y + c
        out_ref[me] = v_ref[...]
        rel = [(dx, dy, dc) for dx in (0, 1) for dy in (0, 1) for dc in (0, 1)][1:]
        copies = []
        for k, (dx, dy, dc) in enumerate(rel):
            to = (x ^ dx, y ^ dy, c ^ dc)
            copies.append(pltpu.make_async_remote_copy(
                src_ref=v_ref, dst_ref=out_ref.at[me], send_sem=send_sems.at[k], recv_sem=recv_sems.at[k],
                device_id=to, device_id_type=MESH))
        for cp in copies:
            cp.start()
        for k, (dx, dy, dc) in enumerate(rel):
            src_blk = 4 * (x ^ dx) + 2 * (y ^ dy) + (c ^ dc)
            pltpu.make_async_remote_copy(
                src_ref=v_ref, dst_ref=out_ref.at[src_blk], send_sem=send_sems.at[k], recv_sem=recv_sems.at[k],
                device_id=(x, y, c), device_id_type=MESH).wait_recv()
        for cp in copies:
            cp.wait_send()

    return pl.pallas_call(
        body, name="allgather_small",
        out_shape=jax.ShapeDtypeStruct((N_DEV, rows, 128), v.dtype),
        in_specs=[_vspec()], out_specs=_vspec(),
        scratch_shapes=[pltpu.SemaphoreType.DMA((7,)), pltpu.SemaphoreType.DMA((7,))],
        compiler_params=_params(24),
    )(v)


def _sibling_plan(n):
    def plan(refs, x, y, c):
        return [(refs[a].at[2 * q + (1 - c)], refs[n + a].at[q], (x, y, 1 - c)) for a in range(n) for q in range(4)]
    return plan


def _chip_plan(n):
    def plan(refs, x, y, c):
        return [(refs[a].at[2 * cx + cy], refs[n + a].at[j], (cx, cy, c))
                for a in range(n) for j, (cx, cy) in enumerate(_other_chips(x, y))]
    return plan


def _row_tile(rows, cols, budget_bytes=2 * 1024 * 1024):
    best = 8
    for t in range(8, rows + 1, 8):
        if rows % t == 0 and t * cols * 4 <= budget_bytes:
            best = t
    return best


def _chip_partial(g, recv, where):
    _, rows, cols = g.shape
    tr = _row_tile(rows, cols)

    def body(where_ref, g_ref, r_ref, own_ref, o16_ref):
        s = g_ref[...] + r_ref[...]
        o16_ref[...] = s.astype(BF16)

        @pl.when(pl.program_id(1) == where_ref[1])
        def _():
            own_ref[...] = s

    blk = (None, tr, cols)
    return pl.pallas_call(
        body, name="rs_chip_partial",
        grid_spec=pltpu.PrefetchScalarGridSpec(
            num_scalar_prefetch=1, grid=(rows // tr, 4),
            in_specs=[pl.BlockSpec(blk, lambda i, q, w: (2 * q + w[0], i, 0)),
                      pl.BlockSpec(blk, lambda i, q, w: (q, i, 0))],
            out_specs=[pl.BlockSpec((tr, cols), lambda i, q, w: (i, 0)), pl.BlockSpec(blk, lambda i, q, w: (q, i, 0))]),
        out_shape=[_hbm((rows, cols), F32), _hbm((4, rows, cols), BF16)],
        compiler_params=_params(32),
    )(where, g, recv)


def _adam_math(w, g, m, v):
    m = ADAM_B1 * m + (1.0 - ADAM_B1) * g
    v = ADAM_B2 * v + (1.0 - ADAM_B2) * (g * g)
    m_hat = m / (1.0 - ADAM_B1 ** ADAM_STEP)
    v_hat = v / (1.0 - ADAM_B2 ** ADAM_STEP)
    delta = -ADAM_LR * (m_hat / (jnp.sqrt(v_hat) + ADAM_EPS) + ADAM_WD * w)
    return delta, m, v


def _adamw_shard(own32, recv16, w, m, v, layer, earlier):
    depth, rows, cols = w.shape
    tr = _row_tile(rows, cols, 1024 * 1024)
    n_prev = 0 if earlier is None else 4

    def body(p_ref, r_ref, w_ref, m_ref, v_ref, *rest):
        g_out, d_out, m_out, v_out = rest[n_prev:]
        g = p_ref[...] + r_ref[0].astype(F32) + r_ref[1].astype(F32) + r_ref[2].astype(F32)
        d, mn, vn = _adam_math(w_ref[...], g, m_ref[...], v_ref[...])
        g_out[...] = g
        d_out[...] = d
        m_out[...] = mn
        v_out[...] = vn

    mine = pl.BlockSpec((None, tr, cols), lambda i: (layer, i, 0))
    return pl.pallas_call(
        body, name="adamw_shard", grid=(rows // tr,),
        in_specs=[pl.BlockSpec((tr, cols), lambda i: (i, 0)), pl.BlockSpec((3, tr, cols), lambda i: (0, i, 0)),
                  mine, mine, mine] + [_anyspec()] * n_prev,
        out_specs=[mine] * 4,
        out_shape=[_hbm((depth, rows, cols), F32)] * 4,
        input_output_aliases={5 + k: k for k in range(n_prev)},
        compiler_params=_params(32),
    )(own32, recv16, *[pltpu.with_memory_space_constraint(t, pltpu.HBM) for t in (w, m, v)],
      *([] if earlier is None else earlier))


def _adamw_small(gathered, w, m, v):
    rows = w.shape[0]

    def body(a_ref, w_ref, m_ref, v_ref, g_out, d_out, m_out, v_out):
        g = a_ref[0]
        for d in range(1, N_DEV):
            g = g + a_ref[d]
        dl, mn, vn = _adam_math(w_ref[...], g, m_ref[...], v_ref[...])
        g_out[...] = g
        d_out[...] = dl
        m_out[...] = mn
        v_out[...] = vn

    return pl.pallas_call(
        body, name="adamw_small",
        in_specs=[_vspec()] * 4, out_specs=[_vspec()] * 4,
        out_shape=[_hbm((rows, 128), F32)] * 4,
        compiler_params=_params(32),
    )(gathered, w, m, v)


def _load_weights_once(pairs, sems):
    @pl.when(pl.program_id(0) == 0)
    def _():
        cps = [pltpu.make_async_copy(src, dst, sems.at[i]) for i, (src, dst) in enumerate(pairs)]
        for cp in cps:
            cp.start()
        for cp in cps:
            cp.wait()


def _ffn_fwd(x, wup, wd, ln_g, ln_b, after, tm=512):
    t_tok = x.shape[0]

    def body(x_ref, g_ref, b_ref, wup_hbm, wd_hbm, _after, xn_ref, xnb_ref, z_ref, gu_ref, xb_ref, wup_v, wd_v, sems):
        _load_weights_once([(wup_hbm, wup_v), (wd_hbm, wd_v)], sems)
        xb = x_ref[...].astype(BF16)
        xb_ref[...] = xb
        y = None
        for j in range(N_FFN_CHUNK):
            g = _dot_nt(xb, wup_v[j])
            u = _dot_nt(xb, wup_v[N_FFN_CHUNK + j])
            gu_ref[0, j] = g.astype(BF16)
            gu_ref[1, j] = u.astype(BF16)
            a = (g * jax.nn.sigmoid(g) * u).astype(BF16)
            part = _dot(a, wd_v[j])
            y = part if y is None else y + part
        z = ALPHA * x_ref[...] + 0.5 * y
        xhat, _ = _ln_stats(z)
        xn = xhat * g_ref[...] + b_ref[...]
        z_ref[...] = z
        xn_ref[...] = xn
        xnb_ref[...] = xn.astype(BF16)

    tok = pl.BlockSpec((tm, D_MODEL), lambda i: (i, 0))
    vec = pl.BlockSpec((1, D_MODEL), lambda i: (0, 0))
    return pl.pallas_call(
        body, name="ffn_fwd", grid=(t_tok // tm,),
        in_specs=[tok, vec, vec, _anyspec(), _anyspec(), _anyspec()],
        out_specs=[tok, tok, tok, pl.BlockSpec((2, N_FFN_CHUNK, tm, FFN_BLK), lambda i: (0, 0, i, 0)), tok],
        out_shape=[_hbm((t_tok, D_MODEL), F32), _hbm((t_tok, D_MODEL), BF16),
                   _hbm((t_tok, D_MODEL), F32),
                   _hbm((2, N_FFN_CHUNK, t_tok, FFN_BLK), BF16),
                   _hbm((t_tok, D_MODEL), BF16)],
        scratch_shapes=[pltpu.VMEM((N_DEV, FFN_BLK, D_MODEL), BF16), pltpu.VMEM((N_FFN_CHUNK, FFN_BLK, D_MODEL), BF16),
                        pltpu.SemaphoreType.DMA((2,))],
        compiler_params=_params(62, ("arbitrary",)),
    )(x, ln_g, ln_b, wup, wd, after)


def _ffn_bwd(dxn, z, gu, wup, wd, ln_g, after, tm=256):
    t_tok = dxn.shape[0]

    def body(dxn_ref, z_ref, gu_ref, g_ref, wup_hbm, wd_hbm, _after,
             dx_ref, dy_ref, a_ref, dgu_ref, dg_ref, db_ref, wup_v, wd_v, sems):
        i = pl.program_id(0)
        _load_weights_once([(wup_hbm, wup_v), (wd_hbm, wd_v)], sems)
        dxn_t = dxn_ref[...]
        xhat, rstd = _ln_stats(z_ref[...])
        pg = jnp.sum(dxn_t * xhat, axis=0, keepdims=True)
        pb = jnp.sum(dxn_t, axis=0, keepdims=True)

        @pl.when(i == 0)
        def _():
            dg_ref[...] = pg
            db_ref[...] = pb

        @pl.when(i > 0)
        def _():
            dg_ref[...] += pg
            db_ref[...] += pb

        dz = _ln_bwd(dxn_t, xhat, rstd, g_ref[...])
        dy = (0.5 * dz).astype(BF16)
        dy_ref[...] = dy
        dx = ALPHA * dz
        for j in range(N_FFN_CHUNK):
            da = _dot_nt(dy, wd_v[j])
            g = gu_ref[0, j].astype(F32)
            u = gu_ref[1, j].astype(F32)
            sig = jax.nn.sigmoid(g)
            silu = g * sig
            a_ref[j] = (silu * u).astype(BF16)
            dg = (da * u * (sig * (1.0 + g * (1.0 - sig)))).astype(BF16)
            du = (da * silu).astype(BF16)
            dgu_ref[0, j] = dg
            dgu_ref[1, j] = du
            dx = dx + _dot(dg, wup_v[j]) + _dot(du, wup_v[N_FFN_CHUNK + j])
        dx_ref[...] = dx

    tok = pl.BlockSpec((tm, D_MODEL), lambda i: (i, 0))
    vec = pl.BlockSpec((1, D_MODEL), lambda i: (0, 0))
    gu_spec = pl.BlockSpec((2, N_FFN_CHUNK, tm, FFN_BLK), lambda i: (0, 0, i, 0))
    return pl.pallas_call(
        body, name="ffn_bwd", grid=(t_tok // tm,),
        in_specs=[tok, tok, gu_spec, vec, _anyspec(), _anyspec(), _anyspec()],
        out_specs=[tok, tok, pl.BlockSpec((N_FFN_CHUNK, tm, FFN_BLK), lambda i: (0, i, 0)), gu_spec, vec, vec],
        out_shape=[_hbm((t_tok, D_MODEL), F32), _hbm((t_tok, D_MODEL), BF16),
                   _hbm((N_FFN_CHUNK, t_tok, FFN_BLK), BF16),
                   _hbm((2, N_FFN_CHUNK, t_tok, FFN_BLK), BF16),
                   _hbm((1, D_MODEL), F32), _hbm((1, D_MODEL), F32)],
        scratch_shapes=[pltpu.VMEM((N_DEV, FFN_BLK, D_MODEL), BF16), pltpu.VMEM((N_FFN_CHUNK, FFN_BLK, D_MODEL), BF16),
                        pltpu.SemaphoreType.DMA((2,))],
        compiler_params=_params(60, ("arbitrary",)),
    )(dxn, z, gu, ln_g, wup, wd, after)


def _matmul_tn(a, b, after, tk=4096):
    ga, t_tok, m = a.shape
    gb, _, n = b.shape
    groups = max(ga, gb)
    tk = min(tk, t_tok)

    def body(a_ref, b_ref, _after, o_ref):
        p = _dot_tn(a_ref[...].astype(BF16), b_ref[...].astype(BF16))

        @pl.when(pl.program_id(1) == 0)
        def _():
            o_ref[...] = p

        @pl.when(pl.program_id(1) > 0)
        def _():
            o_ref[...] += p

    return pl.pallas_call(
        body, name=f"matmul_tn_{m}x{n}", grid=(groups, t_tok // tk),
        in_specs=[pl.BlockSpec((None, tk, m), (lambda g, t: (g, t, 0)) if ga > 1 else (lambda g, t: (0, t, 0))),
                  pl.BlockSpec((None, tk, n), (lambda g, t: (g, t, 0)) if gb > 1 else (lambda g, t: (0, t, 0))),
                  _anyspec()],
        out_specs=pl.BlockSpec((None, m, n), lambda g, t: (g, 0, 0)),
        out_shape=_hbm((groups, m, n), F32),
        compiler_params=_params(56, ("arbitrary", "arbitrary")),
    )(a, b, after)


def _in_proj(x, w_in, tm=512):
    t_tok = x.shape[0]

    def body(x_ref, w_ref, conv_ref, qkv_ref, sgu_ref, f_ref):
        xb = x_ref[...].astype(BF16)
        conv_ref[...] = _dot(xb, w_ref[:, COL_CONV:COL_QKV])
        qkv_ref[...] = _dot(xb, w_ref[:, COL_QKV:COL_SGU]).astype(BF16)
        sgu_ref[...] = _dot(xb, w_ref[:, COL_SGU:COL_F])
        f_ref[...] = _dot(xb, w_ref[:, COL_F:D_IN_PAD])

    def tok(n):
        return pl.BlockSpec((tm, n), lambda i: (i, 0))

    return pl.pallas_call(
        body, name="mix_in_proj", grid=(t_tok // tm,),
        in_specs=[tok(D_MODEL), pl.BlockSpec((D_MODEL, D_IN_PAD), lambda i: (0, 0))],
        out_specs=[tok(768), tok(1536), tok(512), tok(128)],
        out_shape=[_hbm((t_tok, 768), F32), _hbm((t_tok, 1536), BF16),
                   _hbm((t_tok, 512), F32), _hbm((t_tok, 128), F32)],
        compiler_params=_params(48, ("arbitrary",)),
    )(x, w_in)


def _shift_down(a, k):
    row = lax.broadcasted_iota(jnp.int32, a.shape, 0)
    return jnp.where(row >= k, pltpu.roll(a, k, 0), 0.0)


def _shift_up(a, k):
    rows = a.shape[0]
    row = lax.broadcasted_iota(jnp.int32, a.shape, 0)
    return jnp.where(row < rows - k, pltpu.roll(a, rows - k, 0), 0.0)


def _tril(n):
    return lax.broadcasted_iota(jnp.int32, (n, n), 0) >= lax.broadcasted_iota(jnp.int32, (n, n), 1)


def _sgu_group_of_lane():
    return lax.broadcasted_iota(jnp.int32, (1, D_SGU), 1) // (D_SGU // N_SGU_GROUPS)


def _log_sigmoid(x):
    return jnp.minimum(x, 0.0) - jnp.log1p(jnp.exp(-jnp.abs(x)))


def _mix_mid_fwd(conv, sgu, f, conv_w, b_f, sgu_g, sgu_b, w_s, b_mat, n_seq):
    t_tok = conv.shape[0]
    seq = t_tok // n_seq
    n_chunk = seq // SGU_CHUNK
    per_blk = ATT_BLK // SGU_CHUNK

    def body(conv_ref, sgu_ref, f_ref, cw_ref, bf_ref, lg_ref, lb_ref, ws_ref, bm_ref, ya_ref, yc_ref, cum_ref):
        z = conv_ref[:, 256:512] * conv_ref[:, 512:768]
        y = cw_ref[0:1, :] * _shift_down(z, 2) + cw_ref[1:2, :] * _shift_down(z, 1) + cw_ref[2:3, :] * z
        ya_ref[...] = (conv_ref[:, 0:256] * y).astype(BF16)

        tril = _tril(SGU_CHUNK)
        grp = _sgu_group_of_lane()
        wc = [jnp.where(tril, ws_ref[g], 0.0).astype(BF16) for g in range(N_SGU_GROUPS)]
        tri_f = tril.astype(F32)
        carry = jnp.zeros((1, 128), F32)
        for n in range(n_chunk):
            rows = pl.ds(n * SGU_CHUNK, SGU_CHUNK)
            u = _gelu(sgu_ref[rows, 0:256])
            vhat, _ = _ln_stats(_gelu(sgu_ref[rows, 256:512]))
            vn = (vhat * lg_ref[...] + lb_ref[...]).astype(BF16)
            mixed = bm_ref[...]
            for g in range(N_SGU_GROUPS):
                mixed = mixed + jnp.where(grp == g, _dot(wc[g], vn), 0.0)
            yc_ref[rows, :] = (u * mixed).astype(BF16)

            log_f = _log_sigmoid(f_ref[rows, :] + bf_ref[...])
            cs = _dot(tri_f, log_f, HIGHEST) + carry
            carry = cs[SGU_CHUNK - 1:SGU_CHUNK, :]
            cs_t = cs.T
            lanes = pl.ds((n % per_blk) * SGU_CHUNK, SGU_CHUNK)
            for h in range(N_HEADS):
                cum_ref[h, n // per_blk, :, lanes] = cs_t[h:h + 1, :]

    def seq_blk(n):
        return pl.BlockSpec((seq, n), lambda b: (b, 0))

    def full(shape):
        return pl.BlockSpec(shape, lambda b: (0,) * len(shape))

    return pl.pallas_call(
        body, name="mix_mid_fwd", grid=(n_seq,),
        in_specs=[seq_blk(768), seq_blk(512), seq_blk(128), full((8, 256)), full((1, 128)), full((1, 256)),
                  full((1, 256)), full((4, 128, 128)), full((128, 256))],
        out_specs=[seq_blk(256), seq_blk(256),
                   pl.BlockSpec((N_HEADS, seq // ATT_BLK, 1, ATT_BLK), lambda b: (b, 0, 0, 0))],
        out_shape=[_hbm((t_tok, 256), BF16), _hbm((t_tok, 256), BF16),
                   _hbm((n_seq * N_HEADS, seq // ATT_BLK, 1, ATT_BLK), F32)],
        compiler_params=_params(48, ("arbitrary",)),
    )(conv, sgu, f, conv_w, b_f, sgu_g, sgu_b, w_s, b_mat)


def _head_masks():
    lane = lax.broadcasted_iota(jnp.int32, (1, 128), 1)
    return lane < 64, lane


def _fox_fwd(qkv, cum_t, n_seq):
    t_tok = qkv.shape[0]
    seq = t_tok // n_seq
    nq = seq // ATT_BLK
    blk = ATT_BLK

    def body(q_ref, k_ref, v_ref, c0_ref, c1_ref, o_ref, lse_ref):
        qi = pl.program_id(2)
        first, _ = _head_masks()
        qs = q_ref[...] * ATT_SCALE
        zero = jnp.zeros_like(qs)
        q0 = jnp.where(first, qs, zero)
        q1 = jnp.where(first, zero, qs)
        causal = _tril(blk)
        one = jnp.ones((1, 128), BF16)

        def step(kb, carry, masked):
            m0, m1, acc0, acc1 = carry
            rows = pl.ds(pl.multiple_of(kb * blk, blk), blk)
            k = k_ref[rows, :]
            v = v_ref[rows, :]

            def head(qh, c_ref, m, acc, vh):
                s = _dot_nt(qh, k) - c_ref[kb]
                if masked:
                    s = jnp.where(causal, s, NEG)
                m_new = jnp.maximum(m, jnp.max(s, axis=1, keepdims=True))
                p = jnp.exp(s - m_new)
                return m_new, acc * jnp.exp(m - m_new) + _dot(p.astype(BF16), vh)

            m0, acc0 = head(q0, c0_ref, m0, acc0, jnp.where(first, v, one))
            m1, acc1 = head(q1, c1_ref, m1, acc1, jnp.where(first, one, v))
            return m0, m1, acc0, acc1

        col = jnp.full((blk, 1), NEG, F32)
        zacc = jnp.zeros((blk, 128), F32)
        carry = lax.fori_loop(0, qi, lambda kb, cr: step(kb, cr, False), (col, col, zacc, zacc))
        m0, m1, acc0, acc1 = step(qi, carry, True)
        l0 = pltpu.roll(acc0, 64, 1)
        l1 = pltpu.roll(acc1, 64, 1)
        o_ref[...] = jnp.where(first, acc0 / l0, acc1 / l1).astype(BF16)
        lse_ref[...] = jnp.where(first, m0 + jnp.log(l0), m1 + jnp.log(l1))

    cum_spec0 = pl.BlockSpec((None, nq, 1, blk), lambda b, hp, qi: (b * N_HEADS + 2 * hp, 0, 0, 0))
    cum_spec1 = pl.BlockSpec((None, nq, 1, blk), lambda b, hp, qi: (b * N_HEADS + 2 * hp + 1, 0, 0, 0))
    out_spec = pl.BlockSpec((blk, 128), lambda b, hp, qi: (b * nq + qi, hp))
    return pl.pallas_call(
        body, name="fox_fwd", grid=(n_seq, 4, nq),
        in_specs=[pl.BlockSpec((blk, 128), lambda b, hp, qi: (b * nq + qi, hp)),
                  pl.BlockSpec((seq, 128), lambda b, hp, qi: (b, 4 + hp)),
                  pl.BlockSpec((seq, 128), lambda b, hp, qi: (b, 8 + hp)), cum_spec0, cum_spec1],
        out_specs=[out_spec, out_spec],
        out_shape=[_hbm((t_tok, D_FOX), BF16), _hbm((t_tok, D_FOX), F32)],
        compiler_params=_params(32, ("arbitrary", "arbitrary", "arbitrary")),
    )(qkv, qkv, qkv, cum_t, cum_t)


def _fox_bwd(qkv, cum_t, o, lse, d_o, n_seq):
    t_tok = qkv.shape[0]
    seq = t_tok // n_seq
    nk = seq // ATT_BLK
    blk = ATT_BLK

    def body(q_ref, k_ref, v_ref, c0_ref, c1_ref, o_ref, lse_ref, do_ref,
             dq_ref, dk_ref, dv_ref, drow_ref, dcol_ref):
        kb = pl.program_id(2)
        first, lane = _head_masks()
        second = jnp.logical_not(first)
        k = k_ref[...]
        v = v_ref[...]
        zero = jnp.zeros_like(k)
        one = jnp.ones((1, 128), BF16)
        ks = k * ATT_SCALE
        causal = _tril(blk)

        @pl.when(kb == 0)
        def _():
            dq_ref[...] = jnp.zeros_like(dq_ref)
            drow_ref[...] = jnp.zeros_like(drow_ref)

        def step(qi, carry, masked):
            rows = pl.ds(pl.multiple_of(qi * blk, blk), blk)
            qs = q_ref[rows, :] * ATT_SCALE
            d_o = do_ref[rows, :]
            dd = d_o.astype(F32) * o_ref[rows, :].astype(F32)
            lse_t = lse_ref[rows, :]

            def head(mine, c, lse_lane, dk, dv):
                qh = jnp.where(mine, qs, zero)
                doh = jnp.where(mine, d_o, zero)
                delta = jnp.sum(jnp.where(mine, dd, 0.0), axis=1, keepdims=True)
                lse_h = jnp.sum(jnp.where(lane == lse_lane, lse_t, 0.0), axis=1, keepdims=True)
                s = _dot_nt(qh, k) - c
                if masked:
                    s = jnp.where(causal, s, NEG)
                p = jnp.exp(s - lse_h)
                ds = (p * (_dot_nt(doh, v) - delta)).astype(BF16)
                dk = dk + _dot_tn(ds, jnp.where(mine, qs, one))
                dv = dv + _dot_tn(p.astype(BF16), doh)
                return dk, dv, _dot(ds, jnp.where(mine, ks, one))

            dk0, dv0, dk1, dv1 = carry
            dk0, dv0, dq0 = head(first, c0_ref[...], 0, dk0, dv0)
            dk1, dv1, dq1 = head(second, c1_ref[...], 64, dk1, dv1)
            dq_ref[rows, :] += jnp.where(first, dq0, dq1)
            drow_ref[rows, :] += jnp.where(first, dq1, dq0)
            return dk0, dv0, dk1, dv1

        zt = jnp.zeros((blk, 128), F32)
        carry = step(kb, (zt, zt, zt, zt), True)
        dk0, dv0, dk1, dv1 = lax.fori_loop(kb + 1, nk, lambda qi, cr: step(qi, cr, False), carry)
        dk_ref[...] = jnp.where(first, dk0, dk1).astype(BF16)
        dcol_ref[...] = jnp.where(first, dk1, dk0)
        dv_ref[...] = (dv0 + dv1).astype(BF16)

    def seq_spec(col0):
        return pl.BlockSpec((seq, 128), lambda b, hp, kb: (b, col0 + hp))

    def key_spec(col0):
        return pl.BlockSpec((blk, 128), lambda b, hp, kb: (b * nk + kb, col0 + hp))

    def cum_spec(h):
        return pl.BlockSpec((None, None, 1, blk), lambda b, hp, kb: (b * N_HEADS + 2 * hp + h, kb, 0, 0))

    return pl.pallas_call(
        body, name="fox_bwd", grid=(n_seq, 4, nk),
        in_specs=[seq_spec(0), key_spec(4), key_spec(8), cum_spec(0), cum_spec(1), seq_spec(0), seq_spec(0), seq_spec(0)],
        out_specs=[seq_spec(0), key_spec(0), key_spec(0), seq_spec(0), key_spec(0)],
        out_shape=[_hbm((t_tok, D_FOX), F32), _hbm((t_tok, D_FOX), BF16),
                   _hbm((t_tok, D_FOX), BF16), _hbm((t_tok, D_FOX), F32),
                   _hbm((t_tok, D_FOX), F32)],
        compiler_params=_params(48, ("arbitrary", "arbitrary", "arbitrary")),
    )(qkv, qkv, qkv, cum_t, cum_t, o, lse, d_o)


def _mix_out_fwd(ya, yb, yc, x, w_out, ln_g, ln_b, tm=512):
    t_tok = x.shape[0]

    def body(ya_ref, yb_ref, yc_ref, x_ref, w_ref, g_ref, b_ref, xn_ref, xnb_ref, z_ref):
        mo = _dot(ya_ref[...], w_ref[0:256, :]) + _dot(yb_ref[...], w_ref[256:768, :]) + _dot(yc_ref[...], w_ref[768:1024, :])
        z = ALPHA * x_ref[...] + mo
        xhat, _ = _ln_stats(z)
        xn = xhat * g_ref[...] + b_ref[...]
        z_ref[...] = z
        xn_ref[...] = xn
        xnb_ref[...] = xn.astype(BF16)

    def tok(n):
        return pl.BlockSpec((tm, n), lambda i: (i, 0))

    vec = pl.BlockSpec((1, D_MODEL), lambda i: (0, 0))
    return pl.pallas_call(
        body, name="mix_out_fwd", grid=(t_tok // tm,),
        in_specs=[tok(256), tok(512), tok(256), tok(D_MODEL),
                  pl.BlockSpec((D_MODEL, D_MODEL), lambda i: (0, 0)), vec, vec],
        out_specs=[tok(D_MODEL)] * 3,
        out_shape=[_hbm((t_tok, D_MODEL), F32), _hbm((t_tok, D_MODEL), BF16),
                   _hbm((t_tok, D_MODEL), F32)],
        compiler_params=_params(40, ("arbitrary",)),
    )(ya, yb, yc, x, w_out, ln_g, ln_b)


def _mix_out_bwd(dxn, z, w_out, ln_g, tm=512):
    t_tok = dxn.shape[0]

    def body(dxn_ref, z_ref, w_ref, g_ref, dz_ref, dzb_ref, dya_ref, dyb_ref, dyc_ref, dg_ref, db_ref):
        i = pl.program_id(0)
        dxn_t = dxn_ref[...]
        xhat, rstd = _ln_stats(z_ref[...])
        pg = jnp.sum(dxn_t * xhat, axis=0, keepdims=True)
        pb = jnp.sum(dxn_t, axis=0, keepdims=True)

        @pl.when(i == 0)
        def _():
            dg_ref[...] = pg
            db_ref[...] = pb

        @pl.when(i > 0)
        def _():
            dg_ref[...] += pg
            db_ref[...] += pb

        dz = _ln_bwd(dxn_t, xhat, rstd, g_ref[...])
        dzb = dz.astype(BF16)
        dz_ref[...] = dz
        dzb_ref[...] = dzb
        dya_ref[...] = _dot_nt(dzb, w_ref[0:256, :])
        dyb_ref[...] = _dot_nt(dzb, w_ref[256:768, :]).astype(BF16)
        dyc_ref[...] = _dot_nt(dzb, w_ref[768:1024, :])

    def tok(n):
        return pl.BlockSpec((tm, n), lambda i: (i, 0))

    vec = pl.BlockSpec((1, D_MODEL), lambda i: (0, 0))
    return pl.pallas_call(
        body, name="mix_out_bwd", grid=(t_tok // tm,),
        in_specs=[tok(D_MODEL), tok(D_MODEL), pl.BlockSpec((D_MODEL, D_MODEL), lambda i: (0, 0)), vec],
        out_specs=[tok(D_MODEL), tok(D_MODEL), tok(256), tok(512), tok(256), vec, vec],
        out_shape=[_hbm((t_tok, D_MODEL), F32), _hbm((t_tok, D_MODEL), BF16),
                   _hbm((t_tok, 256), F32), _hbm((t_tok, 512), BF16),
                   _hbm((t_tok, 256), F32),
                   _hbm((1, D_MODEL), F32), _hbm((1, D_MODEL), F32)],
        compiler_params=_params(40, ("arbitrary",)),
    )(dxn, z, w_out, ln_g)


def _conv_bwd(conv, dya, conv_w, n_seq):
    t_tok = conv.shape[0]
    seq = t_tok // n_seq

    def body(conv_ref, dya_ref, cw_ref, dconv_ref, dcw_ref):
        @pl.when(pl.program_id(0) == 0)
        def _():
            dcw_ref[...] = jnp.zeros_like(dcw_ref)

        z = conv_ref[:, 256:512] * conv_ref[:, 512:768]
        z1 = _shift_down(z, 1)
        z2 = _shift_down(z, 2)
        y = cw_ref[0:1, :] * z2 + cw_ref[1:2, :] * z1 + cw_ref[2:3, :] * z
        dya_t = dya_ref[...]
        dconv_ref[:, 0:256] = (dya_t * y).astype(BF16)
        dy = dya_t * conv_ref[:, 0:256]
        dcw_ref[0:1, :] += jnp.sum(dy * z2, axis=0, keepdims=True)
        dcw_ref[1:2, :] += jnp.sum(dy * z1, axis=0, keepdims=True)
        dcw_ref[2:3, :] += jnp.sum(dy * z, axis=0, keepdims=True)
        dz = cw_ref[2:3, :] * dy + cw_ref[1:2, :] * _shift_up(dy, 1) + cw_ref[0:1, :] * _shift_up(dy, 2)
        dconv_ref[:, 256:512] = (dz * conv_ref[:, 512:768]).astype(BF16)
        dconv_ref[:, 512:768] = (dz * conv_ref[:, 256:512]).astype(BF16)

    def seq_blk(n):
        return pl.BlockSpec((seq, n), lambda b: (b, 0))

    par = pl.BlockSpec((8, 256), lambda b: (0, 0))
    return pl.pallas_call(
        body, name="conv_bwd", grid=(n_seq,),
        in_specs=[seq_blk(768), seq_blk(256), par], out_specs=[seq_blk(768), par],
        out_shape=[_hbm((t_tok, 768), BF16), _hbm((8, 256), F32)],
        compiler_params=_params(56, ("arbitrary",)),
    )(conv, dya, conv_w)


def _sgu_gate_bwd(sgu, f, dyc, drow, dcol, b_f, sgu_g, sgu_b, w_s, b_mat, n_seq):
    t_tok = sgu.shape[0]
    seq = t_tok // n_seq
    n_chunk = seq // SGU_CHUNK

    def body(sgu_ref, f_ref, dyc_ref, drow_ref, dcol_ref, bf_ref, lg_ref, lb_ref, ws_ref, bm_ref,
             dsgu_ref, df_ref, dbf_ref, dlg_ref, dlb_ref, dws_ref, dbs_ref, dbm_acc):
        b = pl.program_id(0)

        @pl.when(b == 0)
        def _():
            for r in (dbf_ref, dlg_ref, dlb_ref, dws_ref, dbm_acc):
                r[...] = jnp.zeros_like(r)

        tril = _tril(SGU_CHUNK)
        grp = _sgu_group_of_lane()
        wc = [jnp.where(tril, ws_ref[g], 0.0).astype(BF16) for g in range(N_SGU_GROUPS)]
        for n in range(n_chunk):
            rows = pl.ds(n * SGU_CHUNK, SGU_CHUNK)
            su = sgu_ref[rows, 0:256]
            sv = sgu_ref[rows, 256:512]
            u = _gelu(su)
            vhat, rstd = _ln_stats(_gelu(sv))
            vn = (vhat * lg_ref[...] + lb_ref[...]).astype(BF16)
            mixed = bm_ref[...]
            for g in range(N_SGU_GROUPS):
                mixed = mixed + jnp.where(grp == g, _dot(wc[g], vn), 0.0)
            dyc_t = dyc_ref[rows, :]
            dsgu_ref[rows, 0:256] = (dyc_t * mixed * _gelu_grad(su)).astype(BF16)
            dmixed = dyc_t * u
            dbm_acc[...] += dmixed
            dvn = jnp.zeros((SGU_CHUNK, D_SGU), F32)
            for g in range(N_SGU_GROUPS):
                dm_g = jnp.where(grp == g, dmixed, 0.0).astype(BF16)
                dws_ref[g] += _dot_nt(dm_g, vn)
                dvn = dvn + _dot_tn(wc[g], dm_g)
            dlg_ref[...] += jnp.sum(dvn * vhat, axis=0, keepdims=True)
            dlb_ref[...] += jnp.sum(dvn, axis=0, keepdims=True)
            dsgu_ref[rows, 256:512] = (_ln_bwd(dvn, vhat, rstd, lg_ref[...]) * _gelu_grad(sv)).astype(BF16)

        later = (lax.broadcasted_iota(jnp.int32, (128, 128), 0) <= lax.broadcasted_iota(jnp.int32, (128, 128), 1)).astype(F32)
        head = lax.broadcasted_iota(jnp.int32, (D_FOX, 128), 1)
        pick = (lax.broadcasted_iota(jnp.int32, (D_FOX, 128), 0) == 128 * (head // 2) + 64 * (1 - head % 2)).astype(F32)
        carry = jnp.zeros((1, 128), F32)
        for n in reversed(range(n_chunk)):
            rows = pl.ds(n * SGU_CHUNK, SGU_CHUNK)
            dcum_n = _dot(drow_ref[rows, :] - dcol_ref[rows, :], pick, HIGHEST)
            dlf = _dot(later, dcum_n, HIGHEST) + carry
            carry = carry + jnp.sum(dcum_n, axis=0, keepdims=True)
            df = dlf * jax.nn.sigmoid(-(f_ref[rows, :] + bf_ref[...]))
            df_ref[rows, :] = df.astype(BF16)
            dbf_ref[...] += jnp.sum(df, axis=0, keepdims=True)

        @pl.when(b == n_seq - 1)
        def _():
            for g in range(N_SGU_GROUPS):
                dws_ref[g] = jnp.where(tril, dws_ref[g], 0.0)
            sel = (lax.broadcasted_iota(jnp.int32, (D_SGU, 128), 0) // (D_SGU // N_SGU_GROUPS)
                   == lax.broadcasted_iota(jnp.int32, (D_SGU, 128), 1)).astype(F32)
            dbs_ref[...] = _dot(dbm_acc[...], sel, HIGHEST)

    def seq_blk(n):
        return pl.BlockSpec((seq, n), lambda b: (b, 0))

    def full(shape):
        return pl.BlockSpec(shape, lambda b: (0,) * len(shape))

    param_shapes = [(1, 128), (1, 256), (1, 256), (4, 128, 128), (128, 128)]
    return pl.pallas_call(
        body, name="sgu_gate_bwd", grid=(n_seq,),
        in_specs=[seq_blk(512), seq_blk(128), seq_blk(256), seq_blk(D_FOX), seq_blk(D_FOX),
                  full((1, 128)), full((1, 256)), full((1, 256)), full((4, 128, 128)), full((128, 256))],
        out_specs=[seq_blk(512), seq_blk(128)] + [full(s) for s in param_shapes],
        out_shape=[_hbm((t_tok, 512), BF16), _hbm((t_tok, 128), BF16)]
        + [_hbm(s, F32) for s in param_shapes],
        scratch_shapes=[pltpu.VMEM((128, 256), F32)],
        compiler_params=_params(48, ("arbitrary",)),
    )(sgu, f, dyc, drow, dcol, b_f, sgu_g, sgu_b, w_s, b_mat)


def _mix_in_bwd(dconv, dq, dk, dv, dsgu, df, dz, w_in, tm=512):
    t_tok = dz.shape[0]

    def body(dconv_ref, dq_ref, dk_ref, dv_ref, dsgu_ref, df_ref, dz_ref, w_ref, dx_ref, dp_ref):
        dqb = dq_ref[...].astype(BF16)
        pieces = [(COL_CONV, dconv_ref[...]), (COL_QKV, dqb), (COL_QKV + 512, dk_ref[...]), (COL_QKV + 1024, dv_ref[...]),
                  (COL_SGU, dsgu_ref[...]), (COL_F, df_ref[...])]
        dx = ALPHA * dz_ref[...]
        for col, val in pieces:
            width = val.shape[1]
            dp_ref[:, col:col + width] = val
            dx = dx + _dot_nt(val, w_ref[:, col:col + width])
        dx_ref[...] = dx

    def tok(n):
        return pl.BlockSpec((tm, n), lambda i: (i, 0))

    return pl.pallas_call(
        body, name="mix_in_bwd", grid=(t_tok // tm,),
        in_specs=[tok(768), tok(512), tok(512), tok(512), tok(512), tok(128), tok(D_MODEL),
                  pl.BlockSpec((D_MODEL, D_IN_PAD), lambda i: (0, 0))],
        out_specs=[tok(D_MODEL), tok(D_IN_PAD)],
        out_shape=[_hbm((t_tok, D_MODEL), F32), _hbm((t_tok, D_IN_PAD), BF16)],
        compiler_params=_params(48, ("arbitrary",)),
    )(dconv, dq, dk, dv, dsgu, df, dz, w_in)


def _loss_grad(y, target, tm=512):
    t_tok = y.shape[0]

    def body(y_ref, t_ref, dy_ref, loss_ref):
        err = y_ref[...] - t_ref[...]
        dy_ref[...] = err * (1.0 / D_MODEL)
        part = jnp.sum(jnp.sum(err * err, axis=1, keepdims=True), axis=0, keepdims=True) * (0.5 / D_MODEL)

        @pl.when(pl.program_id(0) == 0)
        def _():
            loss_ref[...] = jnp.zeros_like(loss_ref)

        loss_ref[...] += part

    tok = pl.BlockSpec((tm, D_MODEL), lambda i: (i, 0))
    return pl.pallas_call(
        body, name="loss_grad", grid=(t_tok // tm,),
        in_specs=[tok, tok], out_specs=[tok, pl.BlockSpec((1, 128), lambda i: (0, 0))],
        out_shape=[_hbm((t_tok, D_MODEL), F32), _hbm((1, 128), F32)],
        compiler_params=_params(32, ("arbitrary",)),
    )(y, target)


def _pad_rows(a, rows):
    return jnp.pad(a, ((0, rows - a.shape[0]), (0, 0)))


F_BLOCK = F_ORIG // D_IN_SHARD
F_AT = F_ORIG - F_BLOCK * D_IN_SHARD
assert (F_ORIG + N_HEADS) // D_IN_SHARD == F_BLOCK


def _w_in_from_blocks(g):
    fb = g[F_BLOCK]
    zeros = jnp.zeros((D_MODEL, D_IN_PAD - COL_F - N_HEADS), g.dtype)
    return jnp.concatenate([g[d] for d in range(F_BLOCK)] + [fb[:, :F_AT], fb[:, F_AT + N_HEADS:]]
                           + [g[d] for d in range(F_BLOCK + 1, N_DEV)] + [fb[:, F_AT:F_AT + N_HEADS], zeros], axis=1)


def _w_in_to_blocks(dw):
    def cols(lo, hi):
        shift = 0 if hi <= F_ORIG else N_HEADS
        return dw[:, lo - shift:hi - shift]

    blocks = []
    for d in range(N_DEV):
        lo, hi = d * D_IN_SHARD, (d + 1) * D_IN_SHARD
        if d == F_BLOCK:
            blocks.append(jnp.concatenate([cols(lo, F_ORIG), dw[:, COL_F:COL_F + N_HEADS], cols(F_ORIG + N_HEADS, hi)], axis=1))
        else:
            blocks.append(cols(lo, hi))
    return jnp.stack(blocks)


LN1_ROWS = 2 * 8
REST_ROWS = 4 * 8 + 2 * 8 + 512 + 8 + 8 + 8


def _pack_rest(p):
    rows = [p[name].reshape(8, 128) for name in ("ln2_g", "ln2_b", "ln3_g", "ln3_b")]
    rows += [_pad_rows(p[name].reshape(2, 128), 8) for name in ("sgu_ln_g", "sgu_ln_b")]
    rows += [p["sgu_w_s"].reshape(512, 128), _pad_rows(p["sgu_b_s"], 8),
             _pad_rows(jnp.pad(p["fox_b_f"], (0, 128 - N_HEADS)).reshape(1, 128), 8), _pad_rows(p["conv_w"].reshape(6, 128), 8)]
    return jnp.concatenate(rows, axis=0)


def _pack_layer(p):
    return jnp.concatenate([p["ln1_g"].reshape(8, 128), p["ln1_b"].reshape(8, 128), _pack_rest(p)], axis=0)


def _unpack_layer(a):
    r = 0

    def take(n, valid):
        nonlocal r
        piece = a[r:r + valid]
        r += n
        return piece

    d = {}
    for name in ("ln1_g", "ln1_b", "ln2_g", "ln2_b", "ln3_g", "ln3_b"):
        d[name] = take(8, 8).reshape(D_MODEL)
    for name in ("sgu_ln_g", "sgu_ln_b"):
        d[name] = take(8, 2).reshape(D_SGU)
    d["sgu_w_s"] = take(512, 512).reshape(N_SGU_GROUPS, SGU_CHUNK, SGU_CHUNK)
    d["sgu_b_s"] = take(8, 4).reshape(N_SGU_GROUPS, SGU_CHUNK)
    d["fox_b_f"] = take(8, 1).reshape(128)[:N_HEADS]
    d["conv_w"] = take(8, 6).reshape(3, D_CONV)
    return d


SMALL_NAMES = ("ln1_g", "ln1_b", "fox_b_f", "sgu_ln_g", "sgu_ln_b", "sgu_w_s", "sgu_b_s", "ln2_g", "ln2_b", "ln3_g", "ln3_b")
BIG_NAMES = ("ffn1_w_up", "ffn1_w_down", "mix_w_in", "mix_w_out", "ffn2_w_up", "ffn2_w_down")
UP_NAMES = ("ffn1_w_up", "ffn2_w_up")
WEIGHT_ORDER = ("ln1_g", "ln1_b", "ffn1_w_up", "ffn1_w_down", "mix_w_in", "fox_b_f", "conv_w", "sgu_ln_g", "sgu_ln_b",
                "sgu_w_s", "sgu_b_s", "mix_w_out", "ln2_g", "ln2_b", "ffn2_w_up", "ffn2_w_down", "ln3_g", "ln3_b")


class _Overlap:
    def __init__(self, w, after, me, where):
        self.me, self.where = me, where
        groups = [[("ffn1_w_up", 0), ("ffn1_w_down", 0)],
                  [("mix_w_in", 0), ("mix_w_out", 0), ("ffn2_w_up", 0), ("ffn2_w_down", 0)]]
        groups += [[(name, l) for name in BIG_NAMES] for l in range(1, DEPTH)]
        self.gathers = []
        for gi, group in enumerate(groups):
            shards = [w[name][l].astype(BF16) for name, l in group]
            lands = [lax.dynamic_update_slice(lax.empty((N_DEV,) + s.shape, BF16), s[None], (me, 0, 0)) for s in shards]
            started = _exchange_start(f"allgather_start_{gi}", _gather_plan(len(group)), 3 * len(group), shards + lands, after)
            after = started[3]
            self.gathers.append(dict(group=group, chips=started))
        self.all_started = self.last = after
        self.scatters = {}
        self.order = []
        self.small = []

    def _start(self, name, plan, n_copies, arrays):
        started = _exchange_start(name, plan, n_copies, arrays, self.last)
        self.last = started[3]
        return started

    def _group_of(self, layer, part):
        return layer + 1 if layer > 0 else (0 if part == "ffn1" else 1)

    def pass_on(self, layer, part, after):
        st = self.gathers[self._group_of(layer, part)]
        if "sibling" not in st:
            gi, m = self._group_of(layer, part), len(st["group"])
            arrays = _exchange_wait(f"allgather_wait_{gi}", _gather_plan(m), 3 * m, st["chips"], after)
            st["sibling"] = self._start(f"allgather_pass_start_{gi}", _pass_on_plan(m), 4 * m, arrays)
        return st["sibling"][3]

    def weights(self, layer, part, after):
        gi = self._group_of(layer, part)
        st = self.gathers[gi]
        if "full" not in st:
            after = self.all_started if after is None else after
            self.pass_on(layer, part, after)
            m = len(st["group"])
            arrays = _exchange_wait(f"allgather_pass_wait_{gi}", _pass_on_plan(m), 4 * m, st["sibling"], after)
            st["full"] = dict(zip(st["group"], arrays[m:]))
        g = st["full"]

        def ffn(n):
            return g[(f"ffn{n}_w_up", layer)], g[(f"ffn{n}_w_down", layer)].reshape(N_FFN_CHUNK, FFN_BLK, D_MODEL)

        if part == "ffn1":
            return ffn(1)
        return (_w_in_from_blocks(g[("mix_w_in", layer)]), g[("mix_w_out", layer)].reshape(D_MODEL, D_MODEL), *ffn(2))

    def push(self, key, items):
        n = len(items)
        grads = [g for _, _, g in items]
        lands = [lax.empty((4,) + g.shape[1:], F32) for g in grads]
        started = self._start(f"rs_sibling_start_{key[0]}{key[1]}", _sibling_plan(n), 4 * n, grads + lands)
        self.scatters[key] = dict(items=items, sibling=started)
        self.order.append(key)
        return started[3]

    def advance(self, key, after):
        st = self.scatters[key]
        n = len(st["items"])
        arrays = _exchange_wait(f"rs_sibling_wait_{key[0]}{key[1]}", _sibling_plan(n), 4 * n, st["sibling"], after)
        partials = [_chip_partial(g, r, self.where) for g, r in zip(arrays[:n], arrays[n:])]
        p16 = [p for _, p in partials]
        lands = [lax.empty((3,) + p.shape[1:], BF16) for p in p16]
        started = self._start(f"rs_chip_start_{key[0]}{key[1]}", _chip_plan(n), 3 * n, p16 + lands)
        st.update(own32=[p for p, _ in partials], chip=started)
        return started[3]

    def push_small(self, rows):
        k = len(self.small)
        land = lax.dynamic_update_slice(lax.empty((N_DEV,) + rows.shape, F32), rows[None], (self.me, 0, 0))
        started = self._start(f"small_start_{k}", _peers_plan(), N_DEV - 1, [rows, land])
        self.small.append(started)
        return started[3]

    def finish(self, w, m, v):
        res = {}
        after = self.scatters[self.order[-1]]["chip"][3]
        for key in self.order:
            st = self.scatters[key]
            n = len(st["items"])
            arrays = _exchange_wait(f"rs_chip_wait_{key[0]}{key[1]}", _chip_plan(n), 3 * n, st["chip"], after)
            for (name, l, _), own32, r16 in zip(st["items"], st["own32"], arrays[n:]):
                res[name] = _adamw_shard(own32, r16, w[name], m[name], v[name], l, res.get(name))
                after = res[name][0]
        pieces = [_exchange_wait(f"small_wait_{k}", _peers_plan(), N_DEV - 1, started, after)[1]
                  for k, started in enumerate(self.small)]
        return res, pieces


def _local_step(x, target, comm, small, n_seq):
    def vec(a):
        return a.reshape(1, -1)

    saved = []
    h = x
    for l in range(DEPTH):
        s = {}
        s["up1"], s["down1"] = comm.weights(l, "ffn1", None if l == 0 else h)
        h1, h1b, s["z1"], s["gu1"], s["x0b"] = _ffn_fwd(h, s["up1"], s["down1"], vec(small["ln1_g"][l]), vec(small["ln1_b"][l]), h)
        s["w_in"], s["w_out"], s["up2"], s["down2"] = comm.weights(l, "rest", s["z1"])
        s["x1b"] = h1b
        conv, qkv, sgu, f = _in_proj(h1, s["w_in"])
        cw = _pad_rows(small["conv_w"][l], 8)
        bf = jnp.pad(small["fox_b_f"][l], (0, 128 - N_HEADS)).reshape(1, 128)
        b_mat = jnp.repeat(small["sgu_b_s"][l].T, D_SGU // N_SGU_GROUPS, axis=1)
        mid_params = (cw, bf, vec(small["sgu_ln_g"][l]), vec(small["sgu_ln_b"][l]), small["sgu_w_s"][l], b_mat)
        ya, yc, cum_t = _mix_mid_fwd(conv, sgu, f, *mid_params, n_seq)
        yb, lse = _fox_fwd(qkv, cum_t, n_seq)
        h2, h2b, s["z2"] = _mix_out_fwd(ya, yb, yc, h1, s["w_out"], vec(small["ln2_g"][l]), vec(small["ln2_b"][l]))
        s.update(conv=conv, qkv=qkv, sgu=sgu, f=f, mid_params=mid_params, ya=ya, yb=yb, yc=yc, cum_t=cum_t, lse=lse, x2b=h2b)
        token = comm.pass_on(l + 1, "ffn1", s["z2"]) if l + 1 < DEPTH else h2
        h3, _, s["z3"], s["gu2"], _ = _ffn_fwd(h2, s["up2"], s["down2"], vec(small["ln3_g"][l]), vec(small["ln3_b"][l]), token)
        saved.append(s)
        h = h3

    dh, loss = _loss_grad(h, target)

    late_rows = None
    token = loss
    pending = None
    for l in reversed(range(DEPTH)):
        s = saved[l]
        sg = {}
        dh, dy, a, dgu, sg["ln3_g"], sg["ln3_b"] = _ffn_bwd(dh, s["z3"], s["gu2"], s["up2"], s["down2"], vec(small["ln3_g"][l]), token)
        if pending is not None:
            token = comm.advance(pending, dh)
        g_up2 = _matmul_tn(dgu.reshape(N_DEV, -1, FFN_BLK), s["x2b"][None], token)
        g_down2 = _matmul_tn(a, dy[None], token).reshape(N_DEV, FFN_BLK // 2, D_MODEL)
        dz, dzb, dya, dyb, dyc, sg["ln2_g"], sg["ln2_b"] = _mix_out_bwd(dh, s["z2"], s["w_out"], vec(small["ln2_g"][l]))
        dwo = [_matmul_tn(y[None], dzb[None], token)[0] for y in (s["ya"], s["yb"], s["yc"])]
        g_out = jnp.concatenate(dwo, axis=0).reshape(N_DEV, D_MODEL // N_DEV, D_MODEL)
        dq, dk, dv, drow, dcol = _fox_bwd(s["qkv"], s["cum_t"], s["yb"], s["lse"], dyb, n_seq)
        dconv, dcw = _conv_bwd(s["conv"], dya, s["mid_params"][0], n_seq)
        dsgu, df, dbf, dlg, dlb, dws, dbs = _sgu_gate_bwd(s["sgu"], s["f"], dyc, drow, dcol, *s["mid_params"][1:], n_seq)
        sg.update(conv_w=dcw[:3], fox_b_f=dbf[0, :N_HEADS], sgu_ln_g=dlg[0], sgu_ln_b=dlb[0], sgu_w_s=dws,
                  sgu_b_s=dbs[:, :N_SGU_GROUPS].T)
        dh, dp = _mix_in_bwd(dconv, dq, dk, dv, dsgu, df, dz, s["w_in"])
        g_in = _w_in_to_blocks(_matmul_tn(s["x1b"][None], dp[None], token, tk=1024)[0])
        first = [("ffn2_w_up", l, g_up2), ("ffn2_w_down", l, g_down2), ("mix_w_out", l, g_out), ("mix_w_in", l, g_in)]
        for name in ("ln2_g", "ln2_b", "ln3_g", "ln3_b"):
            sg[name] = sg[name][0]
        if l == 0:
            comm.push((l, "a"), first)
            token = comm.push_small(_pack_rest(sg))
            pending, first = (l, "a"), []
        dh, dy, a, dgu, dg1, db1 = _ffn_bwd(dh, s["z1"], s["gu1"], s["up1"], s["down1"], vec(small["ln1_g"][l]), token)
        if l == 0:
            token = comm.advance(pending, dh)
        g_up1 = _matmul_tn(dgu.reshape(N_DEV, -1, FFN_BLK), s["x0b"][None], token)
        ln1_rows = jnp.concatenate([dg1.reshape(8, 128), db1.reshape(8, 128)], axis=0)
        if l == 0:
            token = comm.push((l, "b"), [("ffn1_w_up", l, g_up1)])
            g_down1 = _matmul_tn(a, dy[None], token).reshape(N_DEV, FFN_BLK // 2, D_MODEL)
            token = comm.advance((l, "b"), g_down1)
            token = comm.push((l, "c"), [("ffn1_w_down", l, g_down1)])
            token = comm.advance((l, "c"), token)
            late_rows = ln1_rows
        else:
            g_down1 = _matmul_tn(a, dy[None], token).reshape(N_DEV, FFN_BLK // 2, D_MODEL)
            pending = (l, "b")
            comm.push(pending, first + [("ffn1_w_up", l, g_up1), ("ffn1_w_down", l, g_down1)])
            token = comm.push_small(jnp.concatenate([ln1_rows, _pack_rest(sg)], axis=0))
    return loss, dh, late_rows


def kernel(x, ln1_g, ln1_b, ffn1_w_up, ffn1_w_down, mix_w_in, fox_b_f, conv_w, sgu_ln_g, sgu_ln_b, sgu_w_s, sgu_b_s, mix_w_out, ln2_g, ln2_b, ffn2_w_up, ffn2_w_down, ln3_g, ln3_b, loss_target, m_ln1_g, m_ln1_b, m_ffn1_w_up, m_ffn1_w_down, m_mix_w_in, m_fox_b_f, m_conv_w, m_sgu_ln_g, m_sgu_ln_b, m_sgu_w_s, m_sgu_b_s, m_mix_w_out, m_ln2_g, m_ln2_b, m_ffn2_w_up, m_ffn2_w_down, m_ln3_g, m_ln3_b, v_ln1_g, v_ln1_b, v_ffn1_w_up, v_ffn1_w_down, v_mix_w_in, v_fox_b_f, v_conv_w, v_sgu_ln_g, v_sgu_ln_b, v_sgu_w_s, v_sgu_b_s, v_mix_w_out, v_ln2_g, v_ln2_b, v_ffn2_w_up, v_ffn2_w_down, v_ln3_g, v_ln3_b):
    w = dict(ln1_g=ln1_g, ln1_b=ln1_b, ffn1_w_up=ffn1_w_up, ffn1_w_down=ffn1_w_down, mix_w_in=mix_w_in, fox_b_f=fox_b_f,
             conv_w=conv_w, sgu_ln_g=sgu_ln_g, sgu_ln_b=sgu_ln_b, sgu_w_s=sgu_w_s, sgu_b_s=sgu_b_s, mix_w_out=mix_w_out,
             ln2_g=ln2_g, ln2_b=ln2_b, ffn2_w_up=ffn2_w_up, ffn2_w_down=ffn2_w_down, ln3_g=ln3_g, ln3_b=ln3_b)
    m = dict(ln1_g=m_ln1_g, ln1_b=m_ln1_b, ffn1_w_up=m_ffn1_w_up, ffn1_w_down=m_ffn1_w_down, mix_w_in=m_mix_w_in,
             fox_b_f=m_fox_b_f, conv_w=m_conv_w, sgu_ln_g=m_sgu_ln_g, sgu_ln_b=m_sgu_ln_b, sgu_w_s=m_sgu_w_s,
             sgu_b_s=m_sgu_b_s, mix_w_out=m_mix_w_out, ln2_g=m_ln2_g, ln2_b=m_ln2_b, ffn2_w_up=m_ffn2_w_up,
             ffn2_w_down=m_ffn2_w_down, ln3_g=m_ln3_g, ln3_b=m_ln3_b)
    v = dict(ln1_g=v_ln1_g, ln1_b=v_ln1_b, ffn1_w_up=v_ffn1_w_up, ffn1_w_down=v_ffn1_w_down, mix_w_in=v_mix_w_in,
             fox_b_f=v_fox_b_f, conv_w=v_conv_w, sgu_ln_g=v_sgu_ln_g, sgu_ln_b=v_sgu_ln_b, sgu_w_s=v_sgu_w_s,
             sgu_b_s=v_sgu_b_s, mix_w_out=v_mix_w_out, ln2_g=v_ln2_g, ln2_b=v_ln2_b, ffn2_w_up=v_ffn2_w_up,
             ffn2_w_down=v_ffn2_w_down, ln3_g=v_ln3_g, ln3_b=v_ln3_b)

    mx, my, mc = lax.axis_index("x"), lax.axis_index("y"), lax.axis_index("c")
    me = 4 * mx + 2 * my + mc
    n_seq, seq, _ = x.shape
    t_tok = n_seq * seq
    for name in UP_NAMES:
        for t in (w, m, v):
            t[name] = jnp.transpose(t[name], (0, 2, 1))

    cw_rows = _pad_rows(conv_w.reshape(DEPTH * 3, D_CONV // N_DEV), 8)
    cw_all = _allgather_small(jnp.pad(cw_rows, ((0, 0), (0, 128 - D_CONV // N_DEV))))
    conv_w_full = jnp.transpose(cw_all[:, :DEPTH * 3, :D_CONV // N_DEV], (1, 0, 2)).reshape(DEPTH, 3, D_CONV)
    small = {name: w[name] for name in SMALL_NAMES}
    small["conv_w"] = conv_w_full

    comm = _Overlap(w, cw_all, me, jnp.stack([mc, 2 * mx + my]).astype(jnp.int32))
    loss_dev, grad_x, late_rows = _local_step(
        x.reshape(t_tok, D_MODEL), loss_target.reshape(t_tok, D_MODEL), comm, small, n_seq)
    loss = lax.psum(loss_dev[0, 0], ("x", "y", "c"))
    out, pieces = comm.finish(w, m, v)
    for name in UP_NAMES:
        out[name] = [jnp.transpose(a, (0, 2, 1)) for a in out[name]]

    pieces.append(_allgather_small(late_rows))
    spans = [(l, 0, LN1_ROWS + REST_ROWS) for l in reversed(range(1, DEPTH))] + [(0, LN1_ROWS, LN1_ROWS + REST_ROWS), (0, 0, LN1_ROWS)]

    def widen(a):
        return lax.dynamic_update_slice(jnp.zeros((3, D_CONV), F32), a, (0, me * (D_CONV // N_DEV)))

    packed = [[_pack_layer({**{name: t[name][l] for name in SMALL_NAMES}, "conv_w": widen(t["conv_w"][l])}) for l in range(DEPTH)]
              for t in (w, m, v)]
    rows_out = {}
    for (l, lo, hi), gathered_piece in zip(spans, pieces):
        rows_out[(l, lo)] = _adamw_small(gathered_piece, *[packed[t][l][lo:hi] for t in range(3)])
    per_layer = []
    for l in range(DEPTH):
        parts = sorted(lo for (ll, lo) in rows_out if ll == l)
        per_layer.append([_unpack_layer(jnp.concatenate([rows_out[(l, lo)][k] for lo in parts], axis=0)) for k in range(4)])
    for name in SMALL_NAMES:
        out[name] = [jnp.stack([per_layer[l][k][name] for l in range(DEPTH)]) for k in range(4)]
    lo_col = me * (D_CONV // N_DEV)
    out["conv_w"] = [jnp.stack([lax.dynamic_slice(per_layer[l][k]["conv_w"], (0, lo_col), (3, D_CONV // N_DEV)) for l in range(DEPTH)])
                     for k in range(4)]

    return (loss, grad_x.reshape(x.shape), *[out[name][0] for name in WEIGHT_ORDER], *[out[name][1] for name in WEIGHT_ORDER],
            *[out[name][2] for name in WEIGHT_ORDER], *[out[name][3] for name in WEIGHT_ORDER])
```

```python
import functools

import jax
import jax.numpy as jnp
from jax import lax
from jax.experimental import pallas as pl
from jax.experimental.pallas import tpu as pltpu

F32 = jnp.float32
BF16 = jnp.bfloat16
MESH = pl.DeviceIdType.MESH

N_DEV = 8
DEPTH = 2
D_MODEL = 1024
D_FF = 2816
FFN_BLK = 2 * D_FF // N_DEV
N_FFN_CHUNK = D_FF // FFN_BLK
D_CONV = 256
D_FOX = 512
N_HEADS = 8
D_SGU = 256
N_SGU_GROUPS = 4
SGU_CHUNK = 128
D_IN = 3 * D_CONV + 3 * D_FOX + N_HEADS + 2 * D_SGU
D_IN_SHARD = D_IN // N_DEV
COL_CONV, COL_QKV, COL_SGU, COL_F = 0, 768, 2304, 2816
D_IN_PAD = 2944
F_ORIG = 3 * D_CONV + 3 * D_FOX
ALPHA = (2 * DEPTH) ** 0.25
LN_EPS = 1e-5
ATT_SCALE = 0.125
ATT_BLK = 512
NEG = -1e30

ADAM_LR, ADAM_B1, ADAM_B2, ADAM_EPS, ADAM_WD, ADAM_STEP = 0.001, 0.9, 0.999, 1e-08, 0.01, 10

VMEM_BYTES_V7X = 64 * 1024 * 1024
HIGHEST = lax.Precision.HIGHEST


def _params(vmem_mb, sem=None):
    assert vmem_mb * 1024 * 1024 < VMEM_BYTES_V7X
    kw = dict(vmem_limit_bytes=vmem_mb * 1024 * 1024)
    if sem is not None:
        kw["dimension_semantics"] = sem
    return pltpu.CompilerParams(**kw)


def _dot(a, b, precision=None):
    return lax.dot_general(a, b, (((1,), (0,)), ((), ())), preferred_element_type=F32, precision=precision)


def _dot_nt(a, b):
    return lax.dot_general(a, b, (((1,), (1,)), ((), ())), preferred_element_type=F32)


def _dot_tn(a, b):
    return lax.dot_general(a, b, (((0,), (0,)), ((), ())), preferred_element_type=F32)


def _ln_stats(z):
    mu = jnp.mean(z, axis=-1, keepdims=True)
    zc = z - mu
    var = jnp.mean(zc * zc, axis=-1, keepdims=True)
    rstd = lax.rsqrt(var + LN_EPS)
    return zc * rstd, rstd


def _ln_bwd(dy, xhat, rstd, g):
    dxh = dy * g
    m1 = jnp.mean(dxh, axis=-1, keepdims=True)
    m2 = jnp.mean(dxh * xhat, axis=-1, keepdims=True)
    return rstd * (dxh - m1 - xhat * m2)


_GELU_C = 0.7978845608028654


def _gelu(x):
    return 0.5 * x * (1.0 + jnp.tanh(_GELU_C * (x + 0.044715 * x * x * x)))


def _gelu_grad(x):
    t = jnp.tanh(_GELU_C * (x + 0.044715 * x * x * x))
    return 0.5 * (1.0 + t) + 0.5 * x * (1.0 - t * t) * _GELU_C * (1.0 + 3 * 0.044715 * x * x)


def _hbm(shape, dtype):
    n = 1
    for d in shape:
        n *= d
    if n * jnp.dtype(dtype).itemsize >= 1024 * 1024:
        return pltpu.HBM(tuple(shape), dtype)
    return jax.ShapeDtypeStruct(tuple(shape), dtype)


def _vspec():
    return pl.BlockSpec(memory_space=pltpu.VMEM)


def _anyspec():
    return pl.BlockSpec(memory_space=pl.ANY)


def _mesh_pos():
    return lax.axis_index("x"), lax.axis_index("y"), lax.axis_index("c")


def _other_chips(x, y):
    return [(1 - x, y), (x, 1 - y), (1 - x, 1 - y)]


_HBM_SPEC = pl.BlockSpec(memory_space=pltpu.HBM)
_SEM_SPEC = pl.BlockSpec(memory_space=pltpu.SEMAPHORE)
_DATAFLOW_EFFECT = pltpu.SideEffectType.DATAFLOW_SIDE_EFFECTING


def _remote_copies(plan, refs, send_sems, recv_sems):
    return [pltpu.make_async_remote_copy(src_ref=src, dst_ref=dst, send_sem=send_sems.at[k], recv_sem=recv_sems.at[k],
                                         device_id=to, device_id_type=MESH)
            for k, (src, dst, to) in enumerate(plan(refs, *_mesh_pos()))]


def _exchange_start(name, plan, n_copies, arrays, after):
    n = len(arrays)

    def body(*refs):
        send_sems, recv_sems, token = refs[n + 1], refs[n + 2], refs[-1]
        for cp in _remote_copies(plan, refs[:n], send_sems, recv_sems):
            cp.start()
        token[...] = jnp.zeros_like(token)

    out = pl.pallas_call(
        body, name=name,
        out_shape=(pltpu.SemaphoreType.DMA((n_copies,)), pltpu.SemaphoreType.DMA((n_copies,)),
                   *[pltpu.HBM(a.shape, a.dtype) for a in arrays], _hbm((8, 128), F32)),
        in_specs=[_HBM_SPEC] * n + [_anyspec()],
        out_specs=(_SEM_SPEC, _SEM_SPEC, *[_HBM_SPEC] * n, _vspec()),
        input_output_aliases={i: 2 + i for i in range(n)},
        compiler_params=pltpu.CompilerParams(has_side_effects=_DATAFLOW_EFFECT),
    )(*[pltpu.with_memory_space_constraint(a, pltpu.HBM) for a in arrays], after)
    return out[0], out[1], list(out[2:2 + n]), out[-1]


def _exchange_wait(name, plan, n_copies, started, after):
    send_sems, recv_sems, arrays, _ = started
    n = len(arrays)

    def body(*refs):
        for cp in _remote_copies(plan, refs[:n], refs[n], refs[n + 1]):
            cp.wait_send()
            cp.wait_recv()

    out = pl.pallas_call(
        body, name=name,
        out_shape=tuple(pltpu.HBM(a.shape, a.dtype) for a in arrays),
        in_specs=[_HBM_SPEC] * n + [_SEM_SPEC, _SEM_SPEC, _anyspec()], out_specs=tuple([_HBM_SPEC] * n),
        input_output_aliases={i: i for i in range(n)},
        compiler_params=pltpu.CompilerParams(has_side_effects=_DATAFLOW_EFFECT),
    )(*arrays, send_sems, recv_sems, after)
    return list(out)


def _gather_plan(m):
    def plan(refs, x, y, c):
        me = 4 * x + 2 * y + c
        return [(refs[i], refs[m + i].at[me], (*chip, c)) for i in range(m) for chip in _other_chips(x, y)]
    return plan


def _pass_on_plan(m):
    def plan(refs, x, y, c):
        out = []
        for i in range(m):
            out.append((refs[i], refs[m + i].at[4 * x + 2 * y + c], (x, y, 1 - c)))
            for cx, cy in _other_chips(x, y):
                block = refs[m + i].at[4 * cx + 2 * cy + c]
                out.append((block, block, (x, y, 1 - c)))
        return out
    return plan


def _peers_plan():
    def plan(refs, x, y, c):
        rel = [(dx, dy, dc) for dx in (0, 1) for dy in (0, 1) for dc in (0, 1)][1:]
        return [(refs[0], refs[1].at[4 * x + 2 * y + c], (x ^ dx, y ^ dy, c ^ dc)) for dx, dy, dc in rel]
    return plan


def _allgather_small(v):
    rows = v.shape[0]

    def body(v_ref, out_ref, send_sems, recv_sems):
        x, y, c = _mesh_pos()
        me = 4 * x + 2 * y + c
        out_ref[me] = v_ref[...]
        rel = [(dx, dy, dc) for dx in (0, 1) for dy in (0, 1) for dc in (0, 1)][1:]
        copies = []
        for k, (dx, dy, dc) in enumerate(rel):
            to = (x ^ dx, y ^ dy, c ^ dc)
            copies.append(pltpu.make_async_remote_copy(
                src_ref=v_ref, dst_ref=out_ref.at[me], send_sem=send_sems.at[k], recv_sem=recv_sems.at[k],
                device_id=to, device_id_type=MESH))
        for cp in copies:
            cp.start()
        for k, (dx, dy, dc) in enumerate(rel):
            src_blk = 4 * (x ^ dx) + 2 * (y ^ dy) + (c ^ dc)
            pltpu.make_async_remote_copy(
                src_ref=v_ref, dst_ref=out_ref.at[src_blk], send_sem=send_sems.at[k], recv_sem=recv_sems.at[k],
                device_id=(x, y, c), device_id_type=MESH).wait_recv()
        for cp in copies:
            cp.wait_send()

    return pl.pallas_call(
        body, name="allgather_small",
        out_shape=jax.ShapeDtypeStruct((N_DEV, rows, 128), v.dtype),
        in_specs=[_vspec()], out_specs=_vspec(),
        scratch_shapes=[pltpu.SemaphoreType.DMA((7,)), pltpu.SemaphoreType.DMA((7,))],
        compiler_params=_params(24),
    )(v)


def _sibling_plan(n):
    def plan(refs, x, y, c):
        return [(refs[a].at[2 * q + (1 - c)], refs[n + a].at[q], (x, y, 1 - c)) for a in range(n) for q in range(4)]
    return plan


def _chip_plan(n):
    def plan(refs, x, y, c):
        return [(refs[a].at[2 * cx + cy], refs[n + a].at[j], (cx, cy, c))
                for a in range(n) for j, (cx, cy) in enumerate(_other_chips(x, y))]
    return plan


def _row_tile(rows, cols, budget_bytes=2 * 1024 * 1024):
    best = 8
    for t in range(8, rows + 1, 8):
        if rows % t == 0 and t * cols * 4 <= budget_bytes:
            best = t
    return best


def _chip_partial(g, recv, where):
    _, rows, cols = g.shape
    tr = _row_tile(rows, cols)

    def body(where_ref, g_ref, r_ref, own_ref, o16_ref):
        s = g_ref[...] + r_ref[...]
        o16_ref[...] = s.astype(BF16)

        @pl.when(pl.program_id(1) == where_ref[1])
        def _():
            own_ref[...] = s

    blk = (None, tr, cols)
    return pl.pallas_call(
        body, name="rs_chip_partial",
        grid_spec=pltpu.PrefetchScalarGridSpec(
            num_scalar_prefetch=1, grid=(rows // tr, 4),
            in_specs=[pl.BlockSpec(blk, lambda i, q, w: (2 * q + w[0], i, 0)),
                      pl.BlockSpec(blk, lambda i, q, w: (q, i, 0))],
            out_specs=[pl.BlockSpec((tr, cols), lambda i, q, w: (i, 0)), pl.BlockSpec(blk, lambda i, q, w: (q, i, 0))]),
        out_shape=[_hbm((rows, cols), F32), _hbm((4, rows, cols), BF16)],
        compiler_params=_params(32),
    )(where, g, recv)


def _adam_math(w, g, m, v):
    m = ADAM_B1 * m + (1.0 - ADAM_B1) * g
    v = ADAM_B2 * v + (1.0 - ADAM_B2) * (g * g)
    m_hat = m / (1.0 - ADAM_B1 ** ADAM_STEP)
    v_hat = v / (1.0 - ADAM_B2 ** ADAM_STEP)
    delta = -ADAM_LR * (m_hat / (jnp.sqrt(v_hat) + ADAM_EPS) + ADAM_WD * w)
    return delta, m, v


def _adamw_shard(own32, recv16, w, m, v, layer, earlier):
    depth, rows, cols = w.shape
    tr = _row_tile(rows, cols, 1024 * 1024)
    n_prev = 0 if earlier is None else 4

    def body(p_ref, r_ref, w_ref, m_ref, v_ref, *rest):
        g_out, d_out, m_out, v_out = rest[n_prev:]
        g = p_ref[...] + r_ref[0].astype(F32) + r_ref[1].astype(F32) + r_ref[2].astype(F32)
        d, mn, vn = _adam_math(w_ref[...], g, m_ref[...], v_ref[...])
        g_out[...] = g
        d_out[...] = d
        m_out[...] = mn
        v_out[...] = vn

    mine = pl.BlockSpec((None, tr, cols), lambda i: (layer, i, 0))
    return pl.pallas_call(
        body, name="adamw_shard", grid=(rows // tr,),
        in_specs=[pl.BlockSpec((tr, cols), lambda i: (i, 0)), pl.BlockSpec((3, tr, cols), lambda i: (0, i, 0)),
                  mine, mine, mine] + [_anyspec()] * n_prev,
        out_specs=[mine] * 4,
        out_shape=[_hbm((depth, rows, cols), F32)] * 4,
        input_output_aliases={5 + k: k for k in range(n_prev)},
        compiler_params=_params(32),
    )(own32, recv16, *[pltpu.with_memory_space_constraint(t, pltpu.HBM) for t in (w, m, v)],
      *([] if earlier is None else earlier))


def _adamw_small(gathered, w, m, v):
    rows = w.shape[0]

    def body(a_ref, w_ref, m_ref, v_ref, g_out, d_out, m_out, v_out):
        g = a_ref[0]
        for d in range(1, N_DEV):
            g = g + a_ref[d]
        dl, mn, vn = _adam_math(w_ref[...], g, m_ref[...], v_ref[...])
        g_out[...] = g
        d_out[...] = dl
        m_out[...] = mn
        v_out[...] = vn

    return pl.pallas_call(
        body, name="adamw_small",
        in_specs=[_vspec()] * 4, out_specs=[_vspec()] * 4,
        out_shape=[_hbm((rows, 128), F32)] * 4,
        compiler_params=_params(32),
    )(gathered, w, m, v)


def _load_weights_once(pairs, sems):
    @pl.when(pl.program_id(0) == 0)
    def _():
        cps = [pltpu.make_async_copy(src, dst, sems.at[i]) for i, (src, dst) in enumerate(pairs)]
        for cp in cps:
            cp.start()
        for cp in cps:
            cp.wait()


def _ffn_fwd(x, wup, wd, ln_g, ln_b, after, target=None, tm=512):
    t_tok = x.shape[0]
    last = target is not None

    def body(x_ref, g_ref, b_ref, wup_hbm, wd_hbm, _after, *rest):
        if last:
            t_ref, dxn_ref, loss_ref, z_ref, gu_ref, wup_v, wd_v, sems = rest
        else:
            xn_ref, xnb_ref, z_ref, gu_ref, xb_ref, wup_v, wd_v, sems = rest
        _load_weights_once([(wup_hbm, wup_v), (wd_hbm, wd_v)], sems)
        xb = x_ref[...].astype(BF16)
        if not last:
            xb_ref[...] = xb
        y = None
        for j in range(N_FFN_CHUNK):
            g = _dot_nt(xb, wup_v[j])
            u = _dot_nt(xb, wup_v[N_FFN_CHUNK + j])
            gu_ref[0, j] = g.astype(BF16)
            gu_ref[1, j] = u.astype(BF16)
            a = (g * jax.nn.sigmoid(g) * u).astype(BF16)
            part = _dot(a, wd_v[j])
            y = part if y is None else y + part
        z = ALPHA * x_ref[...] + 0.5 * y
        xhat, _ = _ln_stats(z)
        xn = xhat * g_ref[...] + b_ref[...]
        z_ref[...] = z
        if last:
            err = xn - t_ref[...]
            dxn_ref[...] = err * (1.0 / D_MODEL)
            part = jnp.sum(jnp.sum(err * err, axis=1, keepdims=True), axis=0, keepdims=True) * (0.5 / D_MODEL)

            @pl.when(pl.program_id(0) == 0)
            def _():
                loss_ref[...] = jnp.zeros_like(loss_ref)

            loss_ref[...] += part
        else:
            xn_ref[...] = xn
            xnb_ref[...] = xn.astype(BF16)

    tok = pl.BlockSpec((tm, D_MODEL), lambda i: (i, 0))
    vec = pl.BlockSpec((1, D_MODEL), lambda i: (0, 0))
    gu_spec = pl.BlockSpec((2, N_FFN_CHUNK, tm, FFN_BLK), lambda i: (0, 0, i, 0))
    gu_shape = _hbm((2, N_FFN_CHUNK, t_tok, FFN_BLK), BF16)
    f32_tok, bf16_tok = _hbm((t_tok, D_MODEL), F32), _hbm((t_tok, D_MODEL), BF16)
    if last:
        extra_in, extra_spec = [target], [tok]
        out_specs = [tok, pl.BlockSpec((1, 128), lambda i: (0, 0)), tok, gu_spec]
        out_shape = [f32_tok, _hbm((1, 128), F32), f32_tok, gu_shape]
    else:
        extra_in, extra_spec = [], []
        out_specs = [tok, tok, tok, gu_spec, tok]
        out_shape = [f32_tok, bf16_tok, f32_tok, gu_shape, bf16_tok]
    return pl.pallas_call(
        body, name="ffn_fwd_loss" if last else "ffn_fwd", grid=(t_tok // tm,),
        in_specs=[tok, vec, vec, _anyspec(), _anyspec(), _anyspec()] + extra_spec,
        out_specs=out_specs, out_shape=out_shape,
        scratch_shapes=[pltpu.VMEM((N_DEV, FFN_BLK, D_MODEL), BF16), pltpu.VMEM((N_FFN_CHUNK, FFN_BLK, D_MODEL), BF16),
                        pltpu.SemaphoreType.DMA((2,))],
        compiler_params=_params(62, ("arbitrary",)),
    )(x, ln_g, ln_b, wup, wd, after, *extra_in)


def _ffn_bwd(dxn, z, gu, wup, wd, ln_g, after, tm=256):
    t_tok = dxn.shape[0]

    def body(dxn_ref, z_ref, gu_ref, g_ref, wup_hbm, wd_hbm, _after,
             dx_ref, dy_ref, a_ref, dgu_ref, dg_ref, db_ref, wup_v, wd_v, sems):
        i = pl.program_id(0)
        _load_weights_once([(wup_hbm, wup_v), (wd_hbm, wd_v)], sems)
        dxn_t = dxn_ref[...]
        xhat, rstd = _ln_stats(z_ref[...])
        pg = jnp.sum(dxn_t * xhat, axis=0, keepdims=True)
        pb = jnp.sum(dxn_t, axis=0, keepdims=True)

        @pl.when(i == 0)
        def _():
            dg_ref[...] = pg
            db_ref[...] = pb

        @pl.when(i > 0)
        def _():
            dg_ref[...] += pg
            db_ref[...] += pb

        dz = _ln_bwd(dxn_t, xhat, rstd, g_ref[...])
        dy = (0.5 * dz).astype(BF16)
        dy_ref[...] = dy
        dx = ALPHA * dz
        for j in range(N_FFN_CHUNK):
            da = _dot_nt(dy, wd_v[j])
            g = gu_ref[0, j].astype(F32)
            u = gu_ref[1, j].astype(F32)
            sig = jax.nn.sigmoid(g)
            silu = g * sig
            a_ref[j] = (silu * u).astype(BF16)
            dg = (da * u * (sig * (1.0 + g * (1.0 - sig)))).astype(BF16)
            du = (da * silu).astype(BF16)
            dgu_ref[0, j] = dg
            dgu_ref[1, j] = du
            dx = dx + _dot(dg, wup_v[j]) + _dot(du, wup_v[N_FFN_CHUNK + j])
        dx_ref[...] = dx

    tok = pl.BlockSpec((tm, D_MODEL), lambda i: (i, 0))
    vec = pl.BlockSpec((1, D_MODEL), lambda i: (0, 0))
    gu_spec = pl.BlockSpec((2, N_FFN_CHUNK, tm, FFN_BLK), lambda i: (0, 0, i, 0))
    return pl.pallas_call(
        body, name="ffn_bwd", grid=(t_tok // tm,),
        in_specs=[tok, tok, gu_spec, vec, _anyspec(), _anyspec(), _anyspec()],
        out_specs=[tok, tok, pl.BlockSpec((N_FFN_CHUNK, tm, FFN_BLK), lambda i: (0, i, 0)), gu_spec, vec, vec],
        out_shape=[_hbm((t_tok, D_MODEL), F32), _hbm((t_tok, D_MODEL), BF16),
                   _hbm((N_FFN_CHUNK, t_tok, FFN_BLK), BF16),
                   _hbm((2, N_FFN_CHUNK, t_tok, FFN_BLK), BF16),
                   _hbm((1, D_MODEL), F32), _hbm((1, D_MODEL), F32)],
        scratch_shapes=[pltpu.VMEM((N_DEV, FFN_BLK, D_MODEL), BF16), pltpu.VMEM((N_FFN_CHUNK, FFN_BLK, D_MODEL), BF16),
                        pltpu.SemaphoreType.DMA((2,))],
        compiler_params=_params(60, ("arbitrary",)),
    )(dxn, z, gu, ln_g, wup, wd, after)


def _matmul_tn(a, b, after, tk=4096):
    ga, t_tok, m = a.shape
    gb, _, n = b.shape
    groups = max(ga, gb)
    tk = min(tk, t_tok)

    def body(a_ref, b_ref, _after, o_ref):
        p = _dot_tn(a_ref[...].astype(BF16), b_ref[...].astype(BF16))

        @pl.when(pl.program_id(1) == 0)
        def _():
            o_ref[...] = p

        @pl.when(pl.program_id(1) > 0)
        def _():
            o_ref[...] += p

    return pl.pallas_call(
        body, name=f"matmul_tn_{m}x{n}", grid=(groups, t_tok // tk),
        in_specs=[pl.BlockSpec((None, tk, m), (lambda g, t: (g, t, 0)) if ga > 1 else (lambda g, t: (0, t, 0))),
                  pl.BlockSpec((None, tk, n), (lambda g, t: (g, t, 0)) if gb > 1 else (lambda g, t: (0, t, 0))),
                  _anyspec()],
        out_specs=pl.BlockSpec((None, m, n), lambda g, t: (g, 0, 0)),
        out_shape=_hbm((groups, m, n), F32),
        compiler_params=_params(56, ("arbitrary", "arbitrary")),
    )(a, b, after)


def _in_proj(x, w_in, tm=512):
    t_tok = x.shape[0]

    def body(x_ref, w_ref, conv_ref, qkv_ref, sgu_ref, f_ref):
        xb = x_ref[...].astype(BF16)
        conv_ref[...] = _dot(xb, w_ref[:, COL_CONV:COL_QKV])
        qkv_ref[...] = _dot(xb, w_ref[:, COL_QKV:COL_SGU]).astype(BF16)
        sgu_ref[...] = _dot(xb, w_ref[:, COL_SGU:COL_F])
        f_ref[...] = _dot(xb, w_ref[:, COL_F:D_IN_PAD])

    def tok(n):
        return pl.BlockSpec((tm, n), lambda i: (i, 0))

    return pl.pallas_call(
        body, name="mix_in_proj", grid=(t_tok // tm,),
        in_specs=[tok(D_MODEL), pl.BlockSpec((D_MODEL, D_IN_PAD), lambda i: (0, 0))],
        out_specs=[tok(768), tok(1536), tok(512), tok(128)],
        out_shape=[_hbm((t_tok, 768), F32), _hbm((t_tok, 1536), BF16),
                   _hbm((t_tok, 512), F32), _hbm((t_tok, 128), F32)],
        compiler_params=_params(48, ("arbitrary",)),
    )(x, w_in)


def _shift_down(a, k):
    row = lax.broadcasted_iota(jnp.int32, a.shape, 0)
    return jnp.where(row >= k, pltpu.roll(a, k, 0), 0.0)


def _shift_up(a, k):
    rows = a.shape[0]
    row = lax.broadcasted_iota(jnp.int32, a.shape, 0)
    return jnp.where(row < rows - k, pltpu.roll(a, rows - k, 0), 0.0)


def _tril(n):
    return lax.broadcasted_iota(jnp.int32, (n, n), 0) >= lax.broadcasted_iota(jnp.int32, (n, n), 1)


def _sgu_group_of_lane():
    return lax.broadcasted_iota(jnp.int32, (1, D_SGU), 1) // (D_SGU // N_SGU_GROUPS)


def _log_sigmoid(x):
    return jnp.minimum(x, 0.0) - jnp.log1p(jnp.exp(-jnp.abs(x)))


def _mix_mid_fwd(conv, sgu, f, conv_w, b_f, sgu_g, sgu_b, w_s, b_mat, n_seq):
    t_tok = conv.shape[0]
    seq = t_tok // n_seq
    n_chunk = seq // SGU_CHUNK
    per_blk = ATT_BLK // SGU_CHUNK

    def body(conv_ref, sgu_ref, f_ref, cw_ref, bf_ref, lg_ref, lb_ref, ws_ref, bm_ref, cat_ref, cum_ref):
        z = conv_ref[:, 256:512] * conv_ref[:, 512:768]
        y = cw_ref[0:1, :] * _shift_down(z, 2) + cw_ref[1:2, :] * _shift_down(z, 1) + cw_ref[2:3, :] * z
        cat_ref[:, 0:D_CONV] = (conv_ref[:, 0:256] * y).astype(BF16)
        cat_ref[:, D_CONV:D_CONV + D_FOX] = jnp.zeros((seq, D_FOX), BF16)

        tril = _tril(SGU_CHUNK)
        grp = _sgu_group_of_lane()
        wc = [jnp.where(tril, ws_ref[g], 0.0).astype(BF16) for g in range(N_SGU_GROUPS)]
        tri_f = tril.astype(F32)
        carry = jnp.zeros((1, 128), F32)
        for n in range(n_chunk):
            rows = pl.ds(n * SGU_CHUNK, SGU_CHUNK)
            u = _gelu(sgu_ref[rows, 0:256])
            vhat, _ = _ln_stats(_gelu(sgu_ref[rows, 256:512]))
            vn = (vhat * lg_ref[...] + lb_ref[...]).astype(BF16)
            mixed = bm_ref[...]
            for g in range(N_SGU_GROUPS):
                mixed = mixed + jnp.where(grp == g, _dot(wc[g], vn), 0.0)
            cat_ref[rows, D_CONV + D_FOX:D_MODEL] = (u * mixed).astype(BF16)

            log_f = _log_sigmoid(f_ref[rows, :] + bf_ref[...])
            cs = _dot(tri_f, log_f, HIGHEST) + carry
            carry = cs[SGU_CHUNK - 1:SGU_CHUNK, :]
            cs_t = cs.T
            lanes = pl.ds((n % per_blk) * SGU_CHUNK, SGU_CHUNK)
            for h in range(N_HEADS):
                cum_ref[h, n // per_blk, :, lanes] = cs_t[h:h + 1, :]

    def seq_blk(n):
        return pl.BlockSpec((seq, n), lambda b: (b, 0))

    def full(shape):
        return pl.BlockSpec(shape, lambda b: (0,) * len(shape))

    return pl.pallas_call(
        body, name="mix_mid_fwd", grid=(n_seq,),
        in_specs=[seq_blk(768), seq_blk(512), seq_blk(128), full((8, 256)), full((1, 128)), full((1, 256)),
                  full((1, 256)), full((4, 128, 128)), full((128, 256))],
        out_specs=[seq_blk(D_MODEL), pl.BlockSpec((N_HEADS, seq // ATT_BLK, 1, ATT_BLK), lambda b: (b, 0, 0, 0))],
        out_shape=[_hbm((t_tok, D_MODEL), BF16), _hbm((n_seq * N_HEADS, seq // ATT_BLK, 1, ATT_BLK), F32)],
        compiler_params=_params(48, ("arbitrary",)),
    )(conv, sgu, f, conv_w, b_f, sgu_g, sgu_b, w_s, b_mat)


def _head_masks():
    lane = lax.broadcasted_iota(jnp.int32, (1, 128), 1)
    return lane < 64, lane


def _fox_fwd(qkv, cum_t, cat, n_seq):
    t_tok = qkv.shape[0]
    seq = t_tok // n_seq
    nq = seq // ATT_BLK
    blk = ATT_BLK

    def body(q_ref, k_ref, v_ref, c0_ref, c1_ref, _cat, o_ref, lse_ref):
        qi = pl.program_id(2)
        first, _ = _head_masks()
        qs = q_ref[...] * ATT_SCALE
        zero = jnp.zeros_like(qs)
        q0 = jnp.where(first, qs, zero)
        q1 = jnp.where(first, zero, qs)
        causal = _tril(blk)
        one = jnp.ones((1, 128), BF16)

        def step(kb, carry, masked):
            m0, m1, acc0, acc1 = carry
            rows = pl.ds(pl.multiple_of(kb * blk, blk), blk)
            k = k_ref[rows, :]
            v = v_ref[rows, :]

            def head(qh, c_ref, m, acc, vh):
                s = _dot_nt(qh, k) - c_ref[kb]
                if masked:
                    s = jnp.where(causal, s, NEG)
                m_new = jnp.maximum(m, jnp.max(s, axis=1, keepdims=True))
                p = jnp.exp(s - m_new)
                return m_new, acc * jnp.exp(m - m_new) + _dot(p.astype(BF16), vh)

            m0, acc0 = head(q0, c0_ref, m0, acc0, jnp.where(first, v, one))
            m1, acc1 = head(q1, c1_ref, m1, acc1, jnp.where(first, one, v))
            return m0, m1, acc0, acc1

        col = jnp.full((blk, 1), NEG, F32)
        zacc = jnp.zeros((blk, 128), F32)
        carry = lax.fori_loop(0, qi, lambda kb, cr: step(kb, cr, False), (col, col, zacc, zacc))
        m0, m1, acc0, acc1 = step(qi, carry, True)
        l0 = pltpu.roll(acc0, 64, 1)
        l1 = pltpu.roll(acc1, 64, 1)
        o_ref[...] = jnp.where(first, acc0 / l0, acc1 / l1).astype(BF16)
        lse_ref[...] = jnp.where(first, m0 + jnp.log(l0), m1 + jnp.log(l1))

    cum_spec0 = pl.BlockSpec((None, nq, 1, blk), lambda b, hp, qi: (b * N_HEADS + 2 * hp, 0, 0, 0))
    cum_spec1 = pl.BlockSpec((None, nq, 1, blk), lambda b, hp, qi: (b * N_HEADS + 2 * hp + 1, 0, 0, 0))
    first_col = D_CONV // 128
    return pl.pallas_call(
        body, name="fox_fwd", grid=(n_seq, 4, nq),
        in_specs=[pl.BlockSpec((blk, 128), lambda b, hp, qi: (b * nq + qi, hp)),
                  pl.BlockSpec((seq, 128), lambda b, hp, qi: (b, 4 + hp)),
                  pl.BlockSpec((seq, 128), lambda b, hp, qi: (b, 8 + hp)), cum_spec0, cum_spec1, _anyspec()],
        out_specs=[pl.BlockSpec((blk, 128), lambda b, hp, qi: (b * nq + qi, first_col + hp)),
                   pl.BlockSpec((blk, 128), lambda b, hp, qi: (b * nq + qi, hp))],
        out_shape=[_hbm(cat.shape, BF16), _hbm((t_tok, D_FOX), F32)],
        input_output_aliases={5: 0},
        compiler_params=_params(32, ("arbitrary", "arbitrary", "arbitrary")),
    )(qkv, qkv, qkv, cum_t, cum_t, cat)


def _fox_bwd(qkv, cum_t, cat, lse, d_o, n_seq):
    t_tok = qkv.shape[0]
    seq = t_tok // n_seq
    nk = seq // ATT_BLK
    blk = ATT_BLK

    def body(q_ref, k_ref, v_ref, c0_ref, c1_ref, o_ref, lse_ref, do_ref,
             dq_ref, dk_ref, dv_ref, drow_ref, dcol_ref):
        kb = pl.program_id(2)
        first, lane = _head_masks()
        second = jnp.logical_not(first)
        k = k_ref[...]
        v = v_ref[...]
        zero = jnp.zeros_like(k)
        one = jnp.ones((1, 128), BF16)
        ks = k * ATT_SCALE
        causal = _tril(blk)

        @pl.when(kb == 0)
        def _():
            dq_ref[...] = jnp.zeros_like(dq_ref)
            drow_ref[...] = jnp.zeros_like(drow_ref)

        def step(qi, carry, masked):
            rows = pl.ds(pl.multiple_of(qi * blk, blk), blk)
            qs = q_ref[rows, :] * ATT_SCALE
            d_o = do_ref[rows, :]
            dd = d_o.astype(F32) * o_ref[rows, :].astype(F32)
            lse_t = lse_ref[rows, :]

            def head(mine, c, lse_lane, dk, dv):
                qh = jnp.where(mine, qs, zero)
                doh = jnp.where(mine, d_o, zero)
                delta = jnp.sum(jnp.where(mine, dd, 0.0), axis=1, keepdims=True)
                lse_h = jnp.sum(jnp.where(lane == lse_lane, lse_t, 0.0), axis=1, keepdims=True)
                s = _dot_nt(qh, k) - c
                if masked:
                    s = jnp.where(causal, s, NEG)
                p = jnp.exp(s - lse_h)
                ds = (p * (_dot_nt(doh, v) - delta)).astype(BF16)
                dk = dk + _dot_tn(ds, jnp.where(mine, qs, one))
                dv = dv + _dot_tn(p.astype(BF16), doh)
                return dk, dv, _dot(ds, jnp.where(mine, ks, one))

            dk0, dv0, dk1, dv1 = carry
            dk0, dv0, dq0 = head(first, c0_ref[...], 0, dk0, dv0)
            dk1, dv1, dq1 = head(second, c1_ref[...], 64, dk1, dv1)
            dq_ref[rows, :] += jnp.where(first, dq0, dq1)
            drow_ref[rows, :] += jnp.where(first, dq1, dq0)
            return dk0, dv0, dk1, dv1

        zt = jnp.zeros((blk, 128), F32)
        carry = step(kb, (zt, zt, zt, zt), True)
        dk0, dv0, dk1, dv1 = lax.fori_loop(kb + 1, nk, lambda qi, cr: step(qi, cr, False), carry)
        dk_ref[...] = jnp.where(first, dk0, dk1).astype(BF16)
        dcol_ref[...] = jnp.where(first, dk1, dk0)
        dv_ref[...] = (dv0 + dv1).astype(BF16)

    def seq_spec(col0):
        return pl.BlockSpec((seq, 128), lambda b, hp, kb: (b, col0 + hp))

    def key_spec(col0):
        return pl.BlockSpec((blk, 128), lambda b, hp, kb: (b * nk + kb, col0 + hp))

    def cum_spec(h):
        return pl.BlockSpec((None, None, 1, blk), lambda b, hp, kb: (b * N_HEADS + 2 * hp + h, kb, 0, 0))

    return pl.pallas_call(
        body, name="fox_bwd", grid=(n_seq, 4, nk),
        in_specs=[seq_spec(0), key_spec(4), key_spec(8), cum_spec(0), cum_spec(1), seq_spec(D_CONV // 128), seq_spec(0), seq_spec(0)],
        out_specs=[seq_spec(0), key_spec(0), key_spec(0), seq_spec(0), key_spec(0)],
        out_shape=[_hbm((t_tok, D_FOX), F32), _hbm((t_tok, D_FOX), BF16),
                   _hbm((t_tok, D_FOX), BF16), _hbm((t_tok, D_FOX), F32),
                   _hbm((t_tok, D_FOX), F32)],
        compiler_params=_params(48, ("arbitrary", "arbitrary", "arbitrary")),
    )(qkv, qkv, qkv, cum_t, cum_t, cat, lse, d_o)


def _mix_out_fwd(cat, x, w_out, ln_g, ln_b, tm=512):
    t_tok = x.shape[0]

    def body(cat_ref, x_ref, w_ref, g_ref, b_ref, xn_ref, xnb_ref, z_ref):
        z = ALPHA * x_ref[...] + _dot(cat_ref[...], w_ref[...])
        xhat, _ = _ln_stats(z)
        xn = xhat * g_ref[...] + b_ref[...]
        z_ref[...] = z
        xn_ref[...] = xn
        xnb_ref[...] = xn.astype(BF16)

    def tok(n):
        return pl.BlockSpec((tm, n), lambda i: (i, 0))

    vec = pl.BlockSpec((1, D_MODEL), lambda i: (0, 0))
    return pl.pallas_call(
        body, name="mix_out_fwd", grid=(t_tok // tm,),
        in_specs=[tok(D_MODEL), tok(D_MODEL), pl.BlockSpec((D_MODEL, D_MODEL), lambda i: (0, 0)), vec, vec],
        out_specs=[tok(D_MODEL)] * 3,
        out_shape=[_hbm((t_tok, D_MODEL), F32), _hbm((t_tok, D_MODEL), BF16),
                   _hbm((t_tok, D_MODEL), F32)],
        compiler_params=_params(40, ("arbitrary",)),
    )(cat, x, w_out, ln_g, ln_b)


def _mix_out_bwd(dxn, z, w_out, ln_g, tm=512):
    t_tok = dxn.shape[0]

    def body(dxn_ref, z_ref, w_ref, g_ref, dz_ref, dzb_ref, dya_ref, dyb_ref, dyc_ref, dg_ref, db_ref):
        i = pl.program_id(0)
        dxn_t = dxn_ref[...]
        xhat, rstd = _ln_stats(z_ref[...])
        pg = jnp.sum(dxn_t * xhat, axis=0, keepdims=True)
        pb = jnp.sum(dxn_t, axis=0, keepdims=True)

        @pl.when(i == 0)
        def _():
            dg_ref[...] = pg
            db_ref[...] = pb

        @pl.when(i > 0)
        def _():
            dg_ref[...] += pg
            db_ref[...] += pb

        dz = _ln_bwd(dxn_t, xhat, rstd, g_ref[...])
        dzb = dz.astype(BF16)
        dz_ref[...] = dz
        dzb_ref[...] = dzb
        dya_ref[...] = _dot_nt(dzb, w_ref[0:256, :])
        dyb_ref[...] = _dot_nt(dzb, w_ref[256:768, :]).astype(BF16)
        dyc_ref[...] = _dot_nt(dzb, w_ref[768:1024, :])

    def tok(n):
        return pl.BlockSpec((tm, n), lambda i: (i, 0))

    vec = pl.BlockSpec((1, D_MODEL), lambda i: (0, 0))
    return pl.pallas_call(
        body, name="mix_out_bwd", grid=(t_tok // tm,),
        in_specs=[tok(D_MODEL), tok(D_MODEL), pl.BlockSpec((D_MODEL, D_MODEL), lambda i: (0, 0)), vec],
        out_specs=[tok(D_MODEL), tok(D_MODEL), tok(256), tok(512), tok(256), vec, vec],
        out_shape=[_hbm((t_tok, D_MODEL), F32), _hbm((t_tok, D_MODEL), BF16),
                   _hbm((t_tok, 256), F32), _hbm((t_tok, 512), BF16),
                   _hbm((t_tok, 256), F32),
                   _hbm((1, D_MODEL), F32), _hbm((1, D_MODEL), F32)],
        compiler_params=_params(40, ("arbitrary",)),
    )(dxn, z, w_out, ln_g)


def _conv_bwd(conv, dya, conv_w, n_seq):
    t_tok = conv.shape[0]
    seq = t_tok // n_seq

    def body(conv_ref, dya_ref, cw_ref, dconv_ref, dcw_ref):
        @pl.when(pl.program_id(0) == 0)
        def _():
            dcw_ref[...] = jnp.zeros_like(dcw_ref)

        z = conv_ref[:, 256:512] * conv_ref[:, 512:768]
        z1 = _shift_down(z, 1)
        z2 = _shift_down(z, 2)
        y = cw_ref[0:1, :] * z2 + cw_ref[1:2, :] * z1 + cw_ref[2:3, :] * z
        dya_t = dya_ref[...]
        dconv_ref[:, 0:256] = (dya_t * y).astype(BF16)
        dy = dya_t * conv_ref[:, 0:256]
        dcw_ref[0:1, :] += jnp.sum(dy * z2, axis=0, keepdims=True)
        dcw_ref[1:2, :] += jnp.sum(dy * z1, axis=0, keepdims=True)
        dcw_ref[2:3, :] += jnp.sum(dy * z, axis=0, keepdims=True)
        dz = cw_ref[2:3, :] * dy + cw_ref[1:2, :] * _shift_up(dy, 1) + cw_ref[0:1, :] * _shift_up(dy, 2)
        dconv_ref[:, 256:512] = (dz * conv_ref[:, 512:768]).astype(BF16)
        dconv_ref[:, 512:768] = (dz * conv_ref[:, 256:512]).astype(BF16)

    def seq_blk(n):
        return pl.BlockSpec((seq, n), lambda b: (b, 0))

    par = pl.BlockSpec((8, 256), lambda b: (0, 0))
    return pl.pallas_call(
        body, name="conv_bwd", grid=(n_seq,),
        in_specs=[seq_blk(768), seq_blk(256), par], out_specs=[seq_blk(768), par],
        out_shape=[_hbm((t_tok, 768), BF16), _hbm((8, 256), F32)],
        compiler_params=_params(56, ("arbitrary",)),
    )(conv, dya, conv_w)


def _sgu_gate_bwd(sgu, f, dyc, drow, dcol, b_f, sgu_g, sgu_b, w_s, b_mat, n_seq):
    t_tok = sgu.shape[0]
    seq = t_tok // n_seq
    n_chunk = seq // SGU_CHUNK

    def body(sgu_ref, f_ref, dyc_ref, drow_ref, dcol_ref, bf_ref, lg_ref, lb_ref, ws_ref, bm_ref,
             dsgu_ref, df_ref, dbf_ref, dlg_ref, dlb_ref, dws_ref, dbs_ref, dbm_acc):
        b = pl.program_id(0)

        @pl.when(b == 0)
        def _():
            for r in (dbf_ref, dlg_ref, dlb_ref, dws_ref, dbm_acc):
                r[...] = jnp.zeros_like(r)

        tril = _tril(SGU_CHUNK)
        grp = _sgu_group_of_lane()
        wc = [jnp.where(tril, ws_ref[g], 0.0).astype(BF16) for g in range(N_SGU_GROUPS)]
        for n in range(n_chunk):
            rows = pl.ds(n * SGU_CHUNK, SGU_CHUNK)
            su = sgu_ref[rows, 0:256]
            sv = sgu_ref[rows, 256:512]
            u = _gelu(su)
            vhat, rstd = _ln_stats(_gelu(sv))
            vn = (vhat * lg_ref[...] + lb_ref[...]).astype(BF16)
            mixed = bm_ref[...]
            for g in range(N_SGU_GROUPS):
                mixed = mixed + jnp.where(grp == g, _dot(wc[g], vn), 0.0)
            dyc_t = dyc_ref[rows, :]
            dsgu_ref[rows, 0:256] = (dyc_t * mixed * _gelu_grad(su)).astype(BF16)
            dmixed = dyc_t * u
            dbm_acc[...] += dmixed
            dvn = jnp.zeros((SGU_CHUNK, D_SGU), F32)
            for g in range(N_SGU_GROUPS):
                dm_g = jnp.where(grp == g, dmixed, 0.0).astype(BF16)
                dws_ref[g] += _dot_nt(dm_g, vn)
                dvn = dvn + _dot_tn(wc[g], dm_g)
            dlg_ref[...] += jnp.sum(dvn * vhat, axis=0, keepdims=True)
            dlb_ref[...] += jnp.sum(dvn, axis=0, keepdims=True)
            dsgu_ref[rows, 256:512] = (_ln_bwd(dvn, vhat, rstd, lg_ref[...]) * _gelu_grad(sv)).astype(BF16)

        later = (lax.broadcasted_iota(jnp.int32, (128, 128), 0) <= lax.broadcasted_iota(jnp.int32, (128, 128), 1)).astype(F32)
        head = lax.broadcasted_iota(jnp.int32, (D_FOX, 128), 1)
        pick = (lax.broadcasted_iota(jnp.int32, (D_FOX, 128), 0) == 128 * (head // 2) + 64 * (1 - head % 2)).astype(F32)
        carry = jnp.zeros((1, 128), F32)
        for n in reversed(range(n_chunk)):
            rows = pl.ds(n * SGU_CHUNK, SGU_CHUNK)
            dcum_n = _dot(drow_ref[rows, :] - dcol_ref[rows, :], pick, HIGHEST)
            dlf = _dot(later, dcum_n, HIGHEST) + carry
            carry = carry + jnp.sum(dcum_n, axis=0, keepdims=True)
            df = dlf * jax.nn.sigmoid(-(f_ref[rows, :] + bf_ref[...]))
            df_ref[rows, :] = df.astype(BF16)
            dbf_ref[...] += jnp.sum(df, axis=0, keepdims=True)

        @pl.when(b == n_seq - 1)
        def _():
            for g in range(N_SGU_GROUPS):
                dws_ref[g] = jnp.where(tril, dws_ref[g], 0.0)
            sel = (lax.broadcasted_iota(jnp.int32, (D_SGU, 128), 0) // (D_SGU // N_SGU_GROUPS)
                   == lax.broadcasted_iota(jnp.int32, (D_SGU, 128), 1)).astype(F32)
            dbs_ref[...] = _dot(dbm_acc[...], sel, HIGHEST)

    def seq_blk(n):
        return pl.BlockSpec((seq, n), lambda b: (b, 0))

    def full(shape):
        return pl.BlockSpec(shape, lambda b: (0,) * len(shape))

    param_shapes = [(1, 128), (1, 256), (1, 256), (4, 128, 128), (128, 128)]
    return pl.pallas_call(
        body, name="sgu_gate_bwd", grid=(n_seq,),
        in_specs=[seq_blk(512), seq_blk(128), seq_blk(256), seq_blk(D_FOX), seq_blk(D_FOX),
                  full((1, 128)), full((1, 256)), full((1, 256)), full((4, 128, 128)), full((128, 256))],
        out_specs=[seq_blk(512), seq_blk(128)] + [full(s) for s in param_shapes],
        out_shape=[_hbm((t_tok, 512), BF16), _hbm((t_tok, 128), BF16)]
        + [_hbm(s, F32) for s in param_shapes],
        scratch_shapes=[pltpu.VMEM((128, 256), F32)],
        compiler_params=_params(48, ("arbitrary",)),
    )(sgu, f, dyc, drow, dcol, b_f, sgu_g, sgu_b, w_s, b_mat)


def _mix_in_bwd(dconv, dq, dk, dv, dsgu, df, dz, w_in, tm=512):
    t_tok = dz.shape[0]

    def body(dconv_ref, dq_ref, dk_ref, dv_ref, dsgu_ref, df_ref, dz_ref, w_ref, dx_ref, dp_ref):
        dqb = dq_ref[...].astype(BF16)
        pieces = [(COL_CONV, dconv_ref[...]), (COL_QKV, dqb), (COL_QKV + 512, dk_ref[...]), (COL_QKV + 1024, dv_ref[...]),
                  (COL_SGU, dsgu_ref[...]), (COL_F, df_ref[...])]
        dx = ALPHA * dz_ref[...]
        for col, val in pieces:
            width = val.shape[1]
            dp_ref[:, col:col + width] = val
            dx = dx + _dot_nt(val, w_ref[:, col:col + width])
        dx_ref[...] = dx

    def tok(n):
        return pl.BlockSpec((tm, n), lambda i: (i, 0))

    return pl.pallas_call(
        body, name="mix_in_bwd", grid=(t_tok // tm,),
        in_specs=[tok(768), tok(512), tok(512), tok(512), tok(512), tok(128), tok(D_MODEL),
                  pl.BlockSpec((D_MODEL, D_IN_PAD), lambda i: (0, 0))],
        out_specs=[tok(D_MODEL), tok(D_IN_PAD)],
        out_shape=[_hbm((t_tok, D_MODEL), F32), _hbm((t_tok, D_IN_PAD), BF16)],
        compiler_params=_params(48, ("arbitrary",)),
    )(dconv, dq, dk, dv, dsgu, df, dz, w_in)


def _pad_rows(a, rows):
    return jnp.pad(a, ((0, rows - a.shape[0]), (0, 0)))


F_BLOCK = F_ORIG // D_IN_SHARD
F_AT = F_ORIG - F_BLOCK * D_IN_SHARD
assert (F_ORIG + N_HEADS) // D_IN_SHARD == F_BLOCK


def _w_in_from_blocks(g):
    fb = g[F_BLOCK]
    zeros = jnp.zeros((D_MODEL, D_IN_PAD - COL_F - N_HEADS), g.dtype)
    return jnp.concatenate([g[d] for d in range(F_BLOCK)] + [fb[:, :F_AT], fb[:, F_AT + N_HEADS:]]
                           + [g[d] for d in range(F_BLOCK + 1, N_DEV)] + [fb[:, F_AT:F_AT + N_HEADS], zeros], axis=1)


def _w_in_to_blocks(dw):
    def cols(lo, hi):
        shift = 0 if hi <= F_ORIG else N_HEADS
        return dw[:, lo - shift:hi - shift]

    blocks = []
    for d in range(N_DEV):
        lo, hi = d * D_IN_SHARD, (d + 1) * D_IN_SHARD
        if d == F_BLOCK:
            blocks.append(jnp.concatenate([cols(lo, F_ORIG), dw[:, COL_F:COL_F + N_HEADS], cols(F_ORIG + N_HEADS, hi)], axis=1))
        else:
            blocks.append(cols(lo, hi))
    return jnp.stack(blocks)


LN1_ROWS = 2 * 8
REST_ROWS = 4 * 8 + 2 * 8 + 512 + 8 + 8 + 8


def _pack_rest(p):
    rows = [p[name].reshape(8, 128) for name in ("ln2_g", "ln2_b", "ln3_g", "ln3_b")]
    rows += [_pad_rows(p[name].reshape(2, 128), 8) for name in ("sgu_ln_g", "sgu_ln_b")]
    rows += [p["sgu_w_s"].reshape(512, 128), _pad_rows(p["sgu_b_s"], 8),
             _pad_rows(jnp.pad(p["fox_b_f"], (0, 128 - N_HEADS)).reshape(1, 128), 8), _pad_rows(p["conv_w"].reshape(6, 128), 8)]
    return jnp.concatenate(rows, axis=0)


def _pack_layer(p):
    return jnp.concatenate([p["ln1_g"].reshape(8, 128), p["ln1_b"].reshape(8, 128), _pack_rest(p)], axis=0)


def _unpack_layer(a):
    r = 0

    def take(n, valid):
        nonlocal r
        piece = a[r:r + valid]
        r += n
        return piece

    d = {}
    for name in ("ln1_g", "ln1_b", "ln2_g", "ln2_b", "ln3_g", "ln3_b"):
        d[name] = take(8, 8).reshape(D_MODEL)
    for name in ("sgu_ln_g", "sgu_ln_b"):
        d[name] = take(8, 2).reshape(D_SGU)
    d["sgu_w_s"] = take(512, 512).reshape(N_SGU_GROUPS, SGU_CHUNK, SGU_CHUNK)
    d["sgu_b_s"] = take(8, 4).reshape(N_SGU_GROUPS, SGU_CHUNK)
    d["fox_b_f"] = take(8, 1).reshape(128)[:N_HEADS]
    d["conv_w"] = take(8, 6).reshape(3, D_CONV)
    return d


SMALL_NAMES = ("ln1_g", "ln1_b", "fox_b_f", "sgu_ln_g", "sgu_ln_b", "sgu_w_s", "sgu_b_s", "ln2_g", "ln2_b", "ln3_g", "ln3_b")
BIG_NAMES = ("ffn1_w_up", "ffn1_w_down", "mix_w_in", "mix_w_out", "ffn2_w_up", "ffn2_w_down")
UP_NAMES = ("ffn1_w_up", "ffn2_w_up")
WEIGHT_ORDER = ("ln1_g", "ln1_b", "ffn1_w_up", "ffn1_w_down", "mix_w_in", "fox_b_f", "conv_w", "sgu_ln_g", "sgu_ln_b",
                "sgu_w_s", "sgu_b_s", "mix_w_out", "ln2_g", "ln2_b", "ffn2_w_up", "ffn2_w_down", "ln3_g", "ln3_b")


class _Overlap:
    def __init__(self, w, after, me, where):
        self.me, self.where = me, where
        groups = [[("ffn1_w_up", 0), ("ffn1_w_down", 0)],
                  [("mix_w_in", 0), ("mix_w_out", 0), ("ffn2_w_up", 0), ("ffn2_w_down", 0)]]
        groups += [[(name, l) for name in BIG_NAMES] for l in range(1, DEPTH)]
        self.gathers = []
        for gi, group in enumerate(groups):
            shards = [w[name][l].astype(BF16) for name, l in group]
            lands = [lax.dynamic_update_slice(lax.empty((N_DEV,) + s.shape, BF16), s[None], (me, 0, 0)) for s in shards]
            started = _exchange_start(f"allgather_start_{gi}", _gather_plan(len(group)), 3 * len(group), shards + lands, after)
            after = started[3]
            self.gathers.append(dict(group=group, chips=started))
        self.all_started = self.last = after
        self.scatters = {}
        self.order = []
        self.small = []

    def _start(self, name, plan, n_copies, arrays):
        started = _exchange_start(name, plan, n_copies, arrays, self.last)
        self.last = started[3]
        return started

    def _group_of(self, layer, part):
        return layer + 1 if layer > 0 else (0 if part == "ffn1" else 1)

    def pass_on(self, layer, part, after):
        st = self.gathers[self._group_of(layer, part)]
        if "sibling" not in st:
            gi, m = self._group_of(layer, part), len(st["group"])
            arrays = _exchange_wait(f"allgather_wait_{gi}", _gather_plan(m), 3 * m, st["chips"], after)
            st["sibling"] = self._start(f"allgather_pass_start_{gi}", _pass_on_plan(m), 4 * m, arrays)
        return st["sibling"][3]

    def weights(self, layer, part, after):
        gi = self._group_of(layer, part)
        st = self.gathers[gi]
        if "full" not in st:
            after = self.all_started if after is None else after
            self.pass_on(layer, part, after)
            m = len(st["group"])
            arrays = _exchange_wait(f"allgather_pass_wait_{gi}", _pass_on_plan(m), 4 * m, st["sibling"], after)
            st["full"] = dict(zip(st["group"], arrays[m:]))
        g = st["full"]

        def ffn(n):
            return g[(f"ffn{n}_w_up", layer)], g[(f"ffn{n}_w_down", layer)].reshape(N_FFN_CHUNK, FFN_BLK, D_MODEL)

        if part == "ffn1":
            return ffn(1)
        return (_w_in_from_blocks(g[("mix_w_in", layer)]), g[("mix_w_out", layer)].reshape(D_MODEL, D_MODEL), *ffn(2))

    def push(self, key, items):
        n = len(items)
        grads = [g for _, _, g in items]
        lands = [lax.empty((4,) + g.shape[1:], F32) for g in grads]
        started = self._start(f"rs_sibling_start_{key[0]}{key[1]}", _sibling_plan(n), 4 * n, grads + lands)
        self.scatters[key] = dict(items=items, sibling=started)
        self.order.append(key)
        return started[3]

    def advance(self, key, after):
        st = self.scatters[key]
        n = len(st["items"])
        arrays = _exchange_wait(f"rs_sibling_wait_{key[0]}{key[1]}", _sibling_plan(n), 4 * n, st["sibling"], after)
        partials = [_chip_partial(g, r, self.where) for g, r in zip(arrays[:n], arrays[n:])]
        p16 = [p for _, p in partials]
        lands = [lax.empty((3,) + p.shape[1:], BF16) for p in p16]
        started = self._start(f"rs_chip_start_{key[0]}{key[1]}", _chip_plan(n), 3 * n, p16 + lands)
        st.update(own32=[p for p, _ in partials], chip=started)
        return started[3]

    def push_small(self, rows):
        k = len(self.small)
        land = lax.dynamic_update_slice(lax.empty((N_DEV,) + rows.shape, F32), rows[None], (self.me, 0, 0))
        started = self._start(f"small_start_{k}", _peers_plan(), N_DEV - 1, [rows, land])
        self.small.append(started)
        return started[3]

    def finish(self, w, m, v):
        res = {}
        after = self.scatters[self.order[-1]]["chip"][3]
        for key in self.order:
            st = self.scatters[key]
            n = len(st["items"])
            arrays = _exchange_wait(f"rs_chip_wait_{key[0]}{key[1]}", _chip_plan(n), 3 * n, st["chip"], after)
            for (name, l, _), own32, r16 in zip(st["items"], st["own32"], arrays[n:]):
                res[name] = _adamw_shard(own32, r16, w[name], m[name], v[name], l, res.get(name))
                after = res[name][0]
        pieces = [_exchange_wait(f"small_wait_{k}", _peers_plan(), N_DEV - 1, started, after)[1]
                  for k, started in enumerate(self.small)]
        return res, pieces


def _local_step(x, target, comm, small, n_seq):
    def vec(a):
        return a.reshape(1, -1)

    saved = []
    h = x
    for l in range(DEPTH):
        s = {}
        s["up1"], s["down1"] = comm.weights(l, "ffn1", None if l == 0 else h)
        h1, h1b, s["z1"], s["gu1"], s["x0b"] = _ffn_fwd(h, s["up1"], s["down1"], vec(small["ln1_g"][l]), vec(small["ln1_b"][l]), h)
        s["w_in"], s["w_out"], s["up2"], s["down2"] = comm.weights(l, "rest", s["z1"])
        s["x1b"] = h1b
        conv, qkv, sgu, f = _in_proj(h1, s["w_in"])
        cw = _pad_rows(small["conv_w"][l], 8)
        bf = jnp.pad(small["fox_b_f"][l], (0, 128 - N_HEADS)).reshape(1, 128)
        b_mat = jnp.repeat(small["sgu_b_s"][l].T, D_SGU // N_SGU_GROUPS, axis=1)
        mid_params = (cw, bf, vec(small["sgu_ln_g"][l]), vec(small["sgu_ln_b"][l]), small["sgu_w_s"][l], b_mat)
        cat, cum_t = _mix_mid_fwd(conv, sgu, f, *mid_params, n_seq)
        cat, lse = _fox_fwd(qkv, cum_t, cat, n_seq)
        h2, h2b, s["z2"] = _mix_out_fwd(cat, h1, s["w_out"], vec(small["ln2_g"][l]), vec(small["ln2_b"][l]))
        s.update(conv=conv, qkv=qkv, sgu=sgu, f=f, mid_params=mid_params, cat=cat, cum_t=cum_t, lse=lse, x2b=h2b)
        token = comm.pass_on(l + 1, "ffn1", s["z2"]) if l + 1 < DEPTH else h2
        ln3 = (vec(small["ln3_g"][l]), vec(small["ln3_b"][l]))
        if l + 1 < DEPTH:
            h, _, s["z3"], s["gu2"], _ = _ffn_fwd(h2, s["up2"], s["down2"], *ln3, token)
        else:
            dh, loss, s["z3"], s["gu2"] = _ffn_fwd(h2, s["up2"], s["down2"], *ln3, token, target)
        saved.append(s)

    late_rows = None
    token = loss
    pending = None
    for l in reversed(range(DEPTH)):
        s = saved[l]
        sg = {}
        dh, dy, a, dgu, sg["ln3_g"], sg["ln3_b"] = _ffn_bwd(dh, s["z3"], s["gu2"], s["up2"], s["down2"], vec(small["ln3_g"][l]), token)
        if pending is not None:
            token = comm.advance(pending, dh)
        g_up2 = _matmul_tn(dgu.reshape(N_DEV, -1, FFN_BLK), s["x2b"][None], token)
        g_down2 = _matmul_tn(a, dy[None], token).reshape(N_DEV, FFN_BLK // 2, D_MODEL)
        dz, dzb, dya, dyb, dyc, sg["ln2_g"], sg["ln2_b"] = _mix_out_bwd(dh, s["z2"], s["w_out"], vec(small["ln2_g"][l]))
        g_out = _matmul_tn(s["cat"][None], dzb[None], token).reshape(N_DEV, D_MODEL // N_DEV, D_MODEL)
        dq, dk, dv, drow, dcol = _fox_bwd(s["qkv"], s["cum_t"], s["cat"], s["lse"], dyb, n_seq)
        dconv, dcw = _conv_bwd(s["conv"], dya, s["mid_params"][0], n_seq)
        dsgu, df, dbf, dlg, dlb, dws, dbs = _sgu_gate_bwd(s["sgu"], s["f"], dyc, drow, dcol, *s["mid_params"][1:], n_seq)
        sg.update(conv_w=dcw[:3], fox_b_f=dbf[0, :N_HEADS], sgu_ln_g=dlg[0], sgu_ln_b=dlb[0], sgu_w_s=dws,
                  sgu_b_s=dbs[:, :N_SGU_GROUPS].T)
        dh, dp = _mix_in_bwd(dconv, dq, dk, dv, dsgu, df, dz, s["w_in"])
        g_in = _w_in_to_blocks(_matmul_tn(s["x1b"][None], dp[None], token, tk=1024)[0])
        first = [("ffn2_w_up", l, g_up2), ("ffn2_w_down", l, g_down2), ("mix_w_out", l, g_out), ("mix_w_in", l, g_in)]
        for name in ("ln2_g", "ln2_b", "ln3_g", "ln3_b"):
            sg[name] = sg[name][0]
        if l == 0:
            comm.push((l, "a"), first)
            token = comm.push_small(_pack_rest(sg))
            pending, first = (l, "a"), []
        dh, dy, a, dgu, dg1, db1 = _ffn_bwd(dh, s["z1"], s["gu1"], s["up1"], s["down1"], vec(small["ln1_g"][l]), token)
        if l == 0:
            token = comm.advance(pending, dh)
        g_up1 = _matmul_tn(dgu.reshape(N_DEV, -1, FFN_BLK), s["x0b"][None], token)
        ln1_rows = jnp.concatenate([dg1.reshape(8, 128), db1.reshape(8, 128)], axis=0)
        if l == 0:
            token = comm.push((l, "b"), [("ffn1_w_up", l, g_up1)])
            g_down1 = _matmul_tn(a, dy[None], token).reshape(N_DEV, FFN_BLK // 2, D_MODEL)
            token = comm.advance((l, "b"), g_down1)
            token = comm.push((l, "c"), [("ffn1_w_down", l, g_down1)])
            token = comm.advance((l, "c"), token)
            late_rows = ln1_rows
        else:
            g_down1 = _matmul_tn(a, dy[None], token).reshape(N_DEV, FFN_BLK // 2, D_MODEL)
            pending = (l, "b")
            comm.push(pending, first + [("ffn1_w_up", l, g_up1), ("ffn1_w_down", l, g_down1)])
            token = comm.push_small(jnp.concatenate([ln1_rows, _pack_rest(sg)], axis=0))
    return loss, dh, late_rows


def kernel(x, ln1_g, ln1_b, ffn1_w_up, ffn1_w_down, mix_w_in, fox_b_f, conv_w, sgu_ln_g, sgu_ln_b, sgu_w_s, sgu_b_s, mix_w_out, ln2_g, ln2_b, ffn2_w_up, ffn2_w_down, ln3_g, ln3_b, loss_target, m_ln1_g, m_ln1_b, m_ffn1_w_up, m_ffn1_w_down, m_mix_w_in, m_fox_b_f, m_conv_w, m_sgu_ln_g, m_sgu_ln_b, m_sgu_w_s, m_sgu_b_s, m_mix_w_out, m_ln2_g, m_ln2_b, m_ffn2_w_up, m_ffn2_w_down, m_ln3_g, m_ln3_b, v_ln1_g, v_ln1_b, v_ffn1_w_up, v_ffn1_w_down, v_mix_w_in, v_fox_b_f, v_conv_w, v_sgu_ln_g, v_sgu_ln_b, v_sgu_w_s, v_sgu_b_s, v_mix_w_out, v_ln2_g, v_ln2_b, v_ffn2_w_up, v_ffn2_w_down, v_ln3_g, v_ln3_b):
    w = dict(ln1_g=ln1_g, ln1_b=ln1_b, ffn1_w_up=ffn1_w_up, ffn1_w_down=ffn1_w_down, mix_w_in=mix_w_in, fox_b_f=fox_b_f,
             conv_w=conv_w, sgu_ln_g=sgu_ln_g, sgu_ln_b=sgu_ln_b, sgu_w_s=sgu_w_s, sgu_b_s=sgu_b_s, mix_w_out=mix_w_out,
             ln2_g=ln2_g, ln2_b=ln2_b, ffn2_w_up=ffn2_w_up, ffn2_w_down=ffn2_w_down, ln3_g=ln3_g, ln3_b=ln3_b)
    m = dict(ln1_g=m_ln1_g, ln1_b=m_ln1_b, ffn1_w_up=m_ffn1_w_up, ffn1_w_down=m_ffn1_w_down, mix_w_in=m_mix_w_in,
             fox_b_f=m_fox_b_f, conv_w=m_conv_w, sgu_ln_g=m_sgu_ln_g, sgu_ln_b=m_sgu_ln_b, sgu_w_s=m_sgu_w_s,
             sgu_b_s=m_sgu_b_s, mix_w_out=m_mix_w_out, ln2_g=m_ln2_g, ln2_b=m_ln2_b, ffn2_w_up=m_ffn2_w_up,
             ffn2_w_down=m_ffn2_w_down, ln3_g=m_ln3_g, ln3_b=m_ln3_b)
    v = dict(ln1_g=v_ln1_g, ln1_b=v_ln1_b, ffn1_w_up=v_ffn1_w_up, ffn1_w_down=v_ffn1_w_down, mix_w_in=v_mix_w_in,
             fox_b_f=v_fox_b_f, conv_w=v_conv_w, sgu_ln_g=v_sgu_ln_g, sgu_ln_b=v_sgu_ln_b, sgu_w_s=v_sgu_w_s,
             sgu_b_s=v_sgu_b_s, mix_w_out=v_mix_w_out, ln2_g=v_ln2_g, ln2_b=v_ln2_b, ffn2_w_up=v_ffn2_w_up,
             ffn2_w_down=v_ffn2_w_down, ln3_g=v_ln3_g, ln3_b=v_ln3_b)

    mx, my, mc = lax.axis_index("x"), lax.axis_index("y"), lax.axis_index("c")
    me = 4 * mx + 2 * my + mc
    n_seq, seq, _ = x.shape
    t_tok = n_seq * seq
    for name in UP_NAMES:
        for t in (w, m, v):
            t[name] = jnp.transpose(t[name], (0, 2, 1))

    cw_rows = _pad_rows(conv_w.reshape(DEPTH * 3, D_CONV // N_DEV), 8)
    cw_all = _allgather_small(jnp.pad(cw_rows, ((0, 0), (0, 128 - D_CONV // N_DEV))))
    conv_w_full = jnp.transpose(cw_all[:, :DEPTH * 3, :D_CONV // N_DEV], (1, 0, 2)).reshape(DEPTH, 3, D_CONV)
    small = {name: w[name] for name in SMALL_NAMES}
    small["conv_w"] = conv_w_full

    comm = _Overlap(w, cw_all, me, jnp.stack([mc, 2 * mx + my]).astype(jnp.int32))
    loss_dev, grad_x, late_rows = _local_step(
        x.reshape(t_tok, D_MODEL), loss_target.reshape(t_tok, D_MODEL), comm, small, n_seq)
    loss = lax.psum(loss_dev[0, 0], ("x", "y", "c"))
    out, pieces = comm.finish(w, m, v)
    for name in UP_NAMES:
        out[name] = [jnp.transpose(a, (0, 2, 1)) for a in out[name]]

    pieces.append(_allgather_small(late_rows))
    spans = [(l, 0, LN1_ROWS + REST_ROWS) for l in reversed(range(1, DEPTH))] + [(0, LN1_ROWS, LN1_ROWS + REST_ROWS), (0, 0, LN1_ROWS)]

    def widen(a):
        return lax.dynamic_update_slice(jnp.zeros((3, D_CONV), F32), a, (0, me * (D_CONV // N_DEV)))

    packed = [[_pack_layer({**{name: t[name][l] for name in SMALL_NAMES}, "conv_w": widen(t["conv_w"][l])}) for l in range(DEPTH)]
              for t in (w, m, v)]
    rows_out = {}
    for (l, lo, hi), gathered_piece in zip(spans, pieces):
        rows_out[(l, lo)] = _adamw_small(gathered_piece, *[packed[t][l][lo:hi] for t in range(3)])
    per_layer = []
    for l in range(DEPTH):
        parts = sorted(lo for (ll, lo) in rows_out if ll == l)
        per_layer.append([_unpack_layer(jnp.concatenate([rows_out[(l, lo)][k] for lo in parts], axis=0)) for k in range(4)])
    for name in SMALL_NAMES:
        out[name] = [jnp.stack([per_layer[l][k][name] for l in range(DEPTH)]) for k in range(4)]
    lo_col = me * (D_CONV // N_DEV)
    out["conv_w"] = [jnp.stack([lax.dynamic_slice(per_layer[l][k]["conv_w"], (0, lo_col), (3, D_CONV // N_DEV)) for l in range(DEPTH)])
                     for k in range(4)]

    return (loss, grad_x.reshape(x.shape), *[out[name][0] for name in WEIGHT_ORDER], *[out[name][1] for name in WEIGHT_ORDER],
            *[out[name][2] for name in WEIGHT_ORDER], *[out[name][3] for name in WEIGHT_ORDER])
```

```python
import functools

import jax
import jax.numpy as jnp
from jax import lax
from jax.experimental import pallas as pl
from jax.experimental.pallas import tpu as pltpu

F32 = jnp.float32
BF16 = jnp.bfloat16
MESH = pl.DeviceIdType.MESH

N_DEV = 8
DEPTH = 2
D_MODEL = 1024
D_FF = 2816
FFN_BLK = 2 * D_FF // N_DEV
N_FFN_CHUNK = D_FF // FFN_BLK
D_CONV = 256
D_FOX = 512
N_HEADS = 8
D_SGU = 256
N_SGU_GROUPS = 4
SGU_CHUNK = 128
D_IN = 3 * D_CONV + 3 * D_FOX + N_HEADS + 2 * D_SGU
D_IN_SHARD = D_IN // N_DEV
COL_CONV, COL_QKV, COL_SGU, COL_F = 0, 768, 2304, 2816
D_IN_PAD = 2944
F_ORIG = 3 * D_CONV + 3 * D_FOX
ALPHA = (2 * DEPTH) ** 0.25
LN_EPS = 1e-5
ATT_SCALE = 0.125
ATT_BLK = 512
ATT_PAIRS = 2
NEG = -1e30

ADAM_LR, ADAM_B1, ADAM_B2, ADAM_EPS, ADAM_WD, ADAM_STEP = 0.001, 0.9, 0.999, 1e-08, 0.01, 10

VMEM_BYTES_V7X = 64 * 1024 * 1024
HIGHEST = lax.Precision.HIGHEST


def _params(vmem_mb, sem=None):
    assert vmem_mb * 1024 * 1024 < VMEM_BYTES_V7X
    kw = dict(vmem_limit_bytes=vmem_mb * 1024 * 1024)
    if sem is not None:
        kw["dimension_semantics"] = sem
    return pltpu.CompilerParams(**kw)


def _dot(a, b, precision=None):
    return lax.dot_general(a, b, (((1,), (0,)), ((), ())), preferred_element_type=F32, precision=precision)


def _dot_nt(a, b):
    return lax.dot_general(a, b, (((1,), (1,)), ((), ())), preferred_element_type=F32)


def _dot_tn(a, b):
    return lax.dot_general(a, b, (((0,), (0,)), ((), ())), preferred_element_type=F32)


def _ln_stats(z):
    mu = jnp.mean(z, axis=-1, keepdims=True)
    zc = z - mu
    var = jnp.mean(zc * zc, axis=-1, keepdims=True)
    rstd = lax.rsqrt(var + LN_EPS)
    return zc * rstd, rstd


def _ln_bwd(dy, xhat, rstd, g):
    dxh = dy * g
    m1 = jnp.mean(dxh, axis=-1, keepdims=True)
    m2 = jnp.mean(dxh * xhat, axis=-1, keepdims=True)
    return rstd * (dxh - m1 - xhat * m2)


_GELU_C = 0.7978845608028654


def _gelu(x):
    return 0.5 * x * (1.0 + jnp.tanh(_GELU_C * (x + 0.044715 * x * x * x)))


def _gelu_grad(x):
    t = jnp.tanh(_GELU_C * (x + 0.044715 * x * x * x))
    return 0.5 * (1.0 + t) + 0.5 * x * (1.0 - t * t) * _GELU_C * (1.0 + 3 * 0.044715 * x * x)


def _hbm(shape, dtype):
    n = 1
    for d in shape:
        n *= d
    if n * jnp.dtype(dtype).itemsize >= 1024 * 1024:
        return pltpu.HBM(tuple(shape), dtype)
    return jax.ShapeDtypeStruct(tuple(shape), dtype)


def _vspec():
    return pl.BlockSpec(memory_space=pltpu.VMEM)


def _anyspec():
    return pl.BlockSpec(memory_space=pl.ANY)


def _mesh_pos():
    return lax.axis_index("x"), lax.axis_index("y"), lax.axis_index("c")


def _other_chips(x, y):
    return [(1 - x, y), (x, 1 - y), (1 - x, 1 - y)]


_HBM_SPEC = pl.BlockSpec(memory_space=pltpu.HBM)
_SEM_SPEC = pl.BlockSpec(memory_space=pltpu.SEMAPHORE)
_DATAFLOW_EFFECT = pltpu.SideEffectType.DATAFLOW_SIDE_EFFECTING


def _remote_copies(plan, refs, send_sems, recv_sems):
    return [pltpu.make_async_remote_copy(src_ref=src, dst_ref=dst, send_sem=send_sems.at[k], recv_sem=recv_sems.at[k],
                                         device_id=to, device_id_type=MESH)
            for k, (src, dst, to) in enumerate(plan(refs, *_mesh_pos()))]


def _exchange_start(name, plan, n_copies, arrays, after):
    n = len(arrays)

    def body(*refs):
        send_sems, recv_sems, token = refs[n + 1], refs[n + 2], refs[-1]
        for cp in _remote_copies(plan, refs[:n], send_sems, recv_sems):
            cp.start()
        token[...] = jnp.zeros_like(token)

    out = pl.pallas_call(
        body, name=name,
        out_shape=(pltpu.SemaphoreType.DMA((n_copies,)), pltpu.SemaphoreType.DMA((n_copies,)),
                   *[pltpu.HBM(a.shape, a.dtype) for a in arrays], _hbm((8, 128), F32)),
        in_specs=[_HBM_SPEC] * n + [_anyspec()],
        out_specs=(_SEM_SPEC, _SEM_SPEC, *[_HBM_SPEC] * n, _vspec()),
        input_output_aliases={i: 2 + i for i in range(n)},
        compiler_params=pltpu.CompilerParams(has_side_effects=_DATAFLOW_EFFECT),
    )(*[pltpu.with_memory_space_constraint(a, pltpu.HBM) for a in arrays], after)
    return out[0], out[1], list(out[2:2 + n]), out[-1]


def _exchange_wait(name, plan, n_copies, started, after):
    send_sems, recv_sems, arrays, _ = started
    n = len(arrays)

    def body(*refs):
        for cp in _remote_copies(plan, refs[:n], refs[n], refs[n + 1]):
            cp.wait_send()
            cp.wait_recv()

    out = pl.pallas_call(
        body, name=name,
        out_shape=tuple(pltpu.HBM(a.shape, a.dtype) for a in arrays),
        in_specs=[_HBM_SPEC] * n + [_SEM_SPEC, _SEM_SPEC, _anyspec()], out_specs=tuple([_HBM_SPEC] * n),
        input_output_aliases={i: i for i in range(n)},
        compiler_params=pltpu.CompilerParams(has_side_effects=_DATAFLOW_EFFECT),
    )(*arrays, send_sems, recv_sems, after)
    return list(out)


def _gather_plan(m):
    def plan(refs, x, y, c):
        me = 4 * x + 2 * y + c
        return [(refs[i], refs[m + i].at[me], (*chip, c)) for i in range(m) for chip in _other_chips(x, y)]
    return plan


def _pass_on_plan(m):
    def plan(refs, x, y, c):
        out = []
        for i in range(m):
            out.append((refs[i], refs[m + i].at[4 * x + 2 * y + c], (x, y, 1 - c)))
            for cx, cy in _other_chips(x, y):
                block = refs[m + i].at[4 * cx + 2 * cy + c]
                out.append((block, block, (x, y, 1 - c)))
        return out
    return plan


def _peers_plan():
    def plan(refs, x, y, c):
        rel = [(dx, dy, dc) for dx in (0, 1) for dy in (0, 1) for dc in (0, 1)][1:]
        return [(refs[0], refs[1].at[4 * x + 2 * y + c], (x ^ dx, y ^ dy, c ^ dc)) for dx, dy, dc in rel]
    return plan


def _allgather_small(v):
    rows = v.shape[0]

    def body(v_ref, out_ref, send_sems, recv_sems):
        x, y, c = _mesh_pos()
        me = 4 * x + 2 * y + c
        out_ref[me] = v_ref[...]
        rel = [(dx, dy, dc) for dx in (0, 1) for dy in (0, 1) for dc in (0, 1)][1:]
        copies = []
        for k, (dx, dy, dc) in enumerate(rel):
            to = (x ^ dx, y ^ dy, c ^ dc)
            copies.append(pltpu.make_async_remote_copy(
                src_ref=v_ref, dst_ref=out_ref.at[me], send_sem=send_sems.at[k], recv_sem=recv_sems.at[k],
                device_id=to, device_id_type=MESH))
        for cp in copies:
            cp.start()
        for k, (dx, dy, dc) in enumerate(rel):
            src_blk = 4 * (x ^ dx) + 2 * (y ^ dy) + (c ^ dc)
            pltpu.make_async_remote_copy(
                src_ref=v_ref, dst_ref=out_ref.at[src_blk], send_sem=send_sems.at[k], recv_sem=recv_sems.at[k],
                device_id=(x, y, c), device_id_type=MESH).wait_recv()
        for cp in copies:
            cp.wait_send()

    return pl.pallas_call(
        body, name="allgather_small",
        out_shape=jax.ShapeDtypeStruct((N_DEV, rows, 128), v.dtype),
        in_specs=[_vspec()], out_specs=_vspec(),
        scratch_shapes=[pltpu.SemaphoreType.DMA((7,)), pltpu.SemaphoreType.DMA((7,))],
        compiler_params=_params(24),
    )(v)


def _sibling_plan(n):
    def plan(refs, x, y, c):
        return [(refs[a].at[2 * q + (1 - c)], refs[n + a].at[q], (x, y, 1 - c)) for a in range(n) for q in range(4)]
    return plan


def _chip_plan(n):
    def plan(refs, x, y, c):
        return [(refs[a].at[2 * cx + cy], refs[n + a].at[j], (cx, cy, c))
                for a in range(n) for j, (cx, cy) in enumerate(_other_chips(x, y))]
    return plan


def _row_tile(rows, cols, budget_bytes=2 * 1024 * 1024):
    best = 8
    for t in range(8, rows + 1, 8):
        if rows % t == 0 and t * cols * 4 <= budget_bytes:
            best = t
    return best


def _chip_partial(g, recv, where):
    _, rows, cols = g.shape
    tr = _row_tile(rows, cols)

    def body(where_ref, g_ref, r_ref, own_ref, o16_ref):
        s = g_ref[...] + r_ref[...]
        o16_ref[...] = s.astype(BF16)

        @pl.when(pl.program_id(1) == where_ref[1])
        def _():
            own_ref[...] = s

    blk = (None, tr, cols)
    return pl.pallas_call(
        body, name="rs_chip_partial",
        grid_spec=pltpu.PrefetchScalarGridSpec(
            num_scalar_prefetch=1, grid=(rows // tr, 4),
            in_specs=[pl.BlockSpec(blk, lambda i, q, w: (2 * q + w[0], i, 0)),
                      pl.BlockSpec(blk, lambda i, q, w: (q, i, 0))],
            out_specs=[pl.BlockSpec((tr, cols), lambda i, q, w: (i, 0)), pl.BlockSpec(blk, lambda i, q, w: (q, i, 0))]),
        out_shape=[_hbm((rows, cols), F32), _hbm((4, rows, cols), BF16)],
        compiler_params=_params(32),
    )(where, g, recv)


def _adam_math(w, g, m, v):
    m = ADAM_B1 * m + (1.0 - ADAM_B1) * g
    v = ADAM_B2 * v + (1.0 - ADAM_B2) * (g * g)
    m_hat = m / (1.0 - ADAM_B1 ** ADAM_STEP)
    v_hat = v / (1.0 - ADAM_B2 ** ADAM_STEP)
    delta = -ADAM_LR * (m_hat / (jnp.sqrt(v_hat) + ADAM_EPS) + ADAM_WD * w)
    return delta, m, v


def _adamw_shard(own32, recv16, w, m, v, layer, earlier):
    depth, rows, cols = w.shape
    tr = _row_tile(rows, cols, 1024 * 1024)
    n_prev = 0 if earlier is None else 4

    def body(p_ref, r_ref, w_ref, m_ref, v_ref, *rest):
        g_out, d_out, m_out, v_out = rest[n_prev:]
        g = p_ref[...] + r_ref[0].astype(F32) + r_ref[1].astype(F32) + r_ref[2].astype(F32)
        d, mn, vn = _adam_math(w_ref[...], g, m_ref[...], v_ref[...])
        g_out[...] = g
        d_out[...] = d
        m_out[...] = mn
        v_out[...] = vn

    mine = pl.BlockSpec((None, tr, cols), lambda i: (layer, i, 0))
    return pl.pallas_call(
        body, name="adamw_shard", grid=(rows // tr,),
        in_specs=[pl.BlockSpec((tr, cols), lambda i: (i, 0)), pl.BlockSpec((3, tr, cols), lambda i: (0, i, 0)),
                  mine, mine, mine] + [_anyspec()] * n_prev,
        out_specs=[mine] * 4,
        out_shape=[_hbm((depth, rows, cols), F32)] * 4,
        input_output_aliases={5 + k: k for k in range(n_prev)},
        compiler_params=_params(32),
    )(own32, recv16, *[pltpu.with_memory_space_constraint(t, pltpu.HBM) for t in (w, m, v)],
      *([] if earlier is None else earlier))


def _adamw_small(gathered, w, m, v):
    rows = w.shape[0]

    def body(a_ref, w_ref, m_ref, v_ref, g_out, d_out, m_out, v_out):
        g = a_ref[0]
        for d in range(1, N_DEV):
            g = g + a_ref[d]
        dl, mn, vn = _adam_math(w_ref[...], g, m_ref[...], v_ref[...])
        g_out[...] = g
        d_out[...] = dl
        m_out[...] = mn
        v_out[...] = vn

    return pl.pallas_call(
        body, name="adamw_small",
        in_specs=[_vspec()] * 4, out_specs=[_vspec()] * 4,
        out_shape=[_hbm((rows, 128), F32)] * 4,
        compiler_params=_params(32),
    )(gathered, w, m, v)


def _load_weights_once(pairs, sems):
    @pl.when(pl.program_id(0) == 0)
    def _():
        cps = [pltpu.make_async_copy(src, dst, sems.at[i]) for i, (src, dst) in enumerate(pairs)]
        for cp in cps:
            cp.start()
        for cp in cps:
            cp.wait()


def _ffn_fwd(x, wup, wd, ln_g, ln_b, after, target=None, tm=512):
    t_tok = x.shape[0]
    last = target is not None

    def body(x_ref, g_ref, b_ref, wup_hbm, wd_hbm, _after, *rest):
        if last:
            t_ref, dxn_ref, loss_ref, z_ref, gu_ref, wup_v, wd_v, sems = rest
        else:
            xn_ref, xnb_ref, z_ref, gu_ref, xb_ref, wup_v, wd_v, sems = rest
        _load_weights_once([(wup_hbm, wup_v), (wd_hbm, wd_v)], sems)
        xb = x_ref[...].astype(BF16)
        if not last:
            xb_ref[...] = xb
        y = None
        for j in range(N_FFN_CHUNK):
            g = _dot_nt(xb, wup_v[j])
            u = _dot_nt(xb, wup_v[N_FFN_CHUNK + j])
            gu_ref[0, j] = g.astype(BF16)
            gu_ref[1, j] = u.astype(BF16)
            a = (g * jax.nn.sigmoid(g) * u).astype(BF16)
            part = _dot(a, wd_v[j])
            y = part if y is None else y + part
        z = ALPHA * x_ref[...] + 0.5 * y
        xhat, _ = _ln_stats(z)
        xn = xhat * g_ref[...] + b_ref[...]
        z_ref[...] = z
        if last:
            err = xn - t_ref[...]
            dxn_ref[...] = err * (1.0 / D_MODEL)
            part = jnp.sum(jnp.sum(err * err, axis=1, keepdims=True), axis=0, keepdims=True) * (0.5 / D_MODEL)

            @pl.when(pl.program_id(0) == 0)
            def _():
                loss_ref[...] = jnp.zeros_like(loss_ref)

            loss_ref[...] += part
        else:
            xn_ref[...] = xn
            xnb_ref[...] = xn.astype(BF16)

    tok = pl.BlockSpec((tm, D_MODEL), lambda i: (i, 0))
    vec = pl.BlockSpec((1, D_MODEL), lambda i: (0, 0))
    gu_spec = pl.BlockSpec((2, N_FFN_CHUNK, tm, FFN_BLK), lambda i: (0, 0, i, 0))
    gu_shape = _hbm((2, N_FFN_CHUNK, t_tok, FFN_BLK), BF16)
    f32_tok, bf16_tok = _hbm((t_tok, D_MODEL), F32), _hbm((t_tok, D_MODEL), BF16)
    if last:
        extra_in, extra_spec = [target], [tok]
        out_specs = [tok, pl.BlockSpec((1, 128), lambda i: (0, 0)), tok, gu_spec]
        out_shape = [f32_tok, _hbm((1, 128), F32), f32_tok, gu_shape]
    else:
        extra_in, extra_spec = [], []
        out_specs = [tok, tok, tok, gu_spec, tok]
        out_shape = [f32_tok, bf16_tok, f32_tok, gu_shape, bf16_tok]
    return pl.pallas_call(
        body, name="ffn_fwd_loss" if last else "ffn_fwd", grid=(t_tok // tm,),
        in_specs=[tok, vec, vec, _anyspec(), _anyspec(), _anyspec()] + extra_spec,
        out_specs=out_specs, out_shape=out_shape,
        scratch_shapes=[pltpu.VMEM((N_DEV, FFN_BLK, D_MODEL), BF16), pltpu.VMEM((N_FFN_CHUNK, FFN_BLK, D_MODEL), BF16),
                        pltpu.SemaphoreType.DMA((2,))],
        compiler_params=_params(62, ("arbitrary",)),
    )(x, ln_g, ln_b, wup, wd, after, *extra_in)


def _ffn_bwd(dxn, z, gu, wup, wd, ln_g, after, tm=256):
    t_tok = dxn.shape[0]

    def body(dxn_ref, z_ref, gu_ref, g_ref, wup_hbm, wd_hbm, _after,
             dx_ref, dy_ref, a_ref, dgu_ref, dg_ref, db_ref, wup_v, wd_v, sems):
        i = pl.program_id(0)
        _load_weights_once([(wup_hbm, wup_v), (wd_hbm, wd_v)], sems)
        dxn_t = dxn_ref[...]
        xhat, rstd = _ln_stats(z_ref[...])
        pg = jnp.sum(dxn_t * xhat, axis=0, keepdims=True)
        pb = jnp.sum(dxn_t, axis=0, keepdims=True)

        @pl.when(i == 0)
        def _():
            dg_ref[...] = pg
            db_ref[...] = pb

        @pl.when(i > 0)
        def _():
            dg_ref[...] += pg
            db_ref[...] += pb

        dz = _ln_bwd(dxn_t, xhat, rstd, g_ref[...])
        dy = (0.5 * dz).astype(BF16)
        dy_ref[...] = dy
        dx = ALPHA * dz
        for j in range(N_FFN_CHUNK):
            da = _dot_nt(dy, wd_v[j])
            g = gu_ref[0, j].astype(F32)
            u = gu_ref[1, j].astype(F32)
            sig = jax.nn.sigmoid(g)
            silu = g * sig
            a_ref[j] = (silu * u).astype(BF16)
            dg = (da * u * (sig * (1.0 + g * (1.0 - sig)))).astype(BF16)
            du = (da * silu).astype(BF16)
            dgu_ref[0, j] = dg
            dgu_ref[1, j] = du
            dx = dx + _dot(dg, wup_v[j]) + _dot(du, wup_v[N_FFN_CHUNK + j])
        dx_ref[...] = dx

    tok = pl.BlockSpec((tm, D_MODEL), lambda i: (i, 0))
    vec = pl.BlockSpec((1, D_MODEL), lambda i: (0, 0))
    gu_spec = pl.BlockSpec((2, N_FFN_CHUNK, tm, FFN_BLK), lambda i: (0, 0, i, 0))
    return pl.pallas_call(
        body, name="ffn_bwd", grid=(t_tok // tm,),
        in_specs=[tok, tok, gu_spec, vec, _anyspec(), _anyspec(), _anyspec()],
        out_specs=[tok, tok, pl.BlockSpec((N_FFN_CHUNK, tm, FFN_BLK), lambda i: (0, i, 0)), gu_spec, vec, vec],
        out_shape=[_hbm((t_tok, D_MODEL), F32), _hbm((t_tok, D_MODEL), BF16),
                   _hbm((N_FFN_CHUNK, t_tok, FFN_BLK), BF16),
                   _hbm((2, N_FFN_CHUNK, t_tok, FFN_BLK), BF16),
                   _hbm((1, D_MODEL), F32), _hbm((1, D_MODEL), F32)],
        scratch_shapes=[pltpu.VMEM((N_DEV, FFN_BLK, D_MODEL), BF16), pltpu.VMEM((N_FFN_CHUNK, FFN_BLK, D_MODEL), BF16),
                        pltpu.SemaphoreType.DMA((2,))],
        compiler_params=_params(60, ("arbitrary",)),
    )(dxn, z, gu, ln_g, wup, wd, after)


def _matmul_tn(a, b, after, tk=4096):
    ga, t_tok, m = a.shape
    gb, _, n = b.shape
    groups = max(ga, gb)
    tk = min(tk, t_tok)

    def body(a_ref, b_ref, _after, o_ref):
        p = _dot_tn(a_ref[...].astype(BF16), b_ref[...].astype(BF16))

        @pl.when(pl.program_id(1) == 0)
        def _():
            o_ref[...] = p

        @pl.when(pl.program_id(1) > 0)
        def _():
            o_ref[...] += p

    return pl.pallas_call(
        body, name=f"matmul_tn_{m}x{n}", grid=(groups, t_tok // tk),
        in_specs=[pl.BlockSpec((None, tk, m), (lambda g, t: (g, t, 0)) if ga > 1 else (lambda g, t: (0, t, 0))),
                  pl.BlockSpec((None, tk, n), (lambda g, t: (g, t, 0)) if gb > 1 else (lambda g, t: (0, t, 0))),
                  _anyspec()],
        out_specs=pl.BlockSpec((None, m, n), lambda g, t: (g, 0, 0)),
        out_shape=_hbm((groups, m, n), F32),
        compiler_params=_params(56, ("arbitrary", "arbitrary")),
    )(a, b, after)


def _in_proj(x, w_in, tm=512):
    t_tok = x.shape[0]

    def body(x_ref, w_ref, conv_ref, qkv_ref, sgu_ref, f_ref):
        xb = x_ref[...].astype(BF16)
        conv_ref[...] = _dot(xb, w_ref[:, COL_CONV:COL_QKV])
        qkv_ref[...] = _dot(xb, w_ref[:, COL_QKV:COL_SGU]).astype(BF16)
        sgu_ref[...] = _dot(xb, w_ref[:, COL_SGU:COL_F])
        f_ref[...] = _dot(xb, w_ref[:, COL_F:D_IN_PAD])

    def tok(n):
        return pl.BlockSpec((tm, n), lambda i: (i, 0))

    return pl.pallas_call(
        body, name="mix_in_proj", grid=(t_tok // tm,),
        in_specs=[tok(D_MODEL), pl.BlockSpec((D_MODEL, D_IN_PAD), lambda i: (0, 0))],
        out_specs=[tok(768), tok(1536), tok(512), tok(128)],
        out_shape=[_hbm((t_tok, 768), F32), _hbm((t_tok, 1536), BF16),
                   _hbm((t_tok, 512), F32), _hbm((t_tok, 128), F32)],
        compiler_params=_params(48, ("arbitrary",)),
    )(x, w_in)


def _shift_down(a, k):
    row = lax.broadcasted_iota(jnp.int32, a.shape, 0)
    return jnp.where(row >= k, pltpu.roll(a, k, 0), 0.0)


def _shift_up(a, k):
    rows = a.shape[0]
    row = lax.broadcasted_iota(jnp.int32, a.shape, 0)
    return jnp.where(row < rows - k, pltpu.roll(a, rows - k, 0), 0.0)


def _tril(n):
    return lax.broadcasted_iota(jnp.int32, (n, n), 0) >= lax.broadcasted_iota(jnp.int32, (n, n), 1)


def _sgu_group_of_lane():
    return lax.broadcasted_iota(jnp.int32, (1, D_SGU), 1) // (D_SGU // N_SGU_GROUPS)


def _log_sigmoid(x):
    return jnp.minimum(x, 0.0) - jnp.log1p(jnp.exp(-jnp.abs(x)))


def _mix_mid_fwd(conv, sgu, f, conv_w, b_f, sgu_g, sgu_b, w_s, b_mat, n_seq):
    t_tok = conv.shape[0]
    seq = t_tok // n_seq
    n_chunk = seq // SGU_CHUNK
    per_blk = ATT_BLK // SGU_CHUNK

    def body(conv_ref, sgu_ref, f_ref, cw_ref, bf_ref, lg_ref, lb_ref, ws_ref, bm_ref, cat_ref, cum_ref):
        z = conv_ref[:, 256:512] * conv_ref[:, 512:768]
        y = cw_ref[0:1, :] * _shift_down(z, 2) + cw_ref[1:2, :] * _shift_down(z, 1) + cw_ref[2:3, :] * z
        cat_ref[:, 0:D_CONV] = (conv_ref[:, 0:256] * y).astype(BF16)
        cat_ref[:, D_CONV:D_CONV + D_FOX] = jnp.zeros((seq, D_FOX), BF16)

        tril = _tril(SGU_CHUNK)
        grp = _sgu_group_of_lane()
        wc = [jnp.where(tril, ws_ref[g], 0.0).astype(BF16) for g in range(N_SGU_GROUPS)]
        tri_f = tril.astype(F32)
        carry = jnp.zeros((1, 128), F32)
        for n in range(n_chunk):
            rows = pl.ds(n * SGU_CHUNK, SGU_CHUNK)
            u = _gelu(sgu_ref[rows, 0:256])
            vhat, _ = _ln_stats(_gelu(sgu_ref[rows, 256:512]))
            vn = (vhat * lg_ref[...] + lb_ref[...]).astype(BF16)
            mixed = bm_ref[...]
            for g in range(N_SGU_GROUPS):
                mixed = mixed + jnp.where(grp == g, _dot(wc[g], vn), 0.0)
            cat_ref[rows, D_CONV + D_FOX:D_MODEL] = (u * mixed).astype(BF16)

            log_f = _log_sigmoid(f_ref[rows, :] + bf_ref[...])
            cs = _dot(tri_f, log_f, HIGHEST) + carry
            carry = cs[SGU_CHUNK - 1:SGU_CHUNK, :]
            cs_t = cs.T
            lanes = pl.ds((n % per_blk) * SGU_CHUNK, SGU_CHUNK)
            for h in range(N_HEADS):
                cum_ref[h, n // per_blk, :, lanes] = cs_t[h:h + 1, :]

    def seq_blk(n):
        return pl.BlockSpec((seq, n), lambda b: (b, 0))

    def full(shape):
        return pl.BlockSpec(shape, lambda b: (0,) * len(shape))

    return pl.pallas_call(
        body, name="mix_mid_fwd", grid=(n_seq,),
        in_specs=[seq_blk(768), seq_blk(512), seq_blk(128), full((8, 256)), full((1, 128)), full((1, 256)),
                  full((1, 256)), full((4, 128, 128)), full((128, 256))],
        out_specs=[seq_blk(D_MODEL), pl.BlockSpec((N_HEADS, seq // ATT_BLK, 1, ATT_BLK), lambda b: (b, 0, 0, 0))],
        out_shape=[_hbm((t_tok, D_MODEL), BF16), _hbm((n_seq * N_HEADS, seq // ATT_BLK, 1, ATT_BLK), F32)],
        compiler_params=_params(48, ("arbitrary",)),
    )(conv, sgu, f, conv_w, b_f, sgu_g, sgu_b, w_s, b_mat)


def _head_masks():
    lane = lax.broadcasted_iota(jnp.int32, (1, 128), 1)
    return lane < 64, lane


def _fox_fwd(qkv, cum_t, cat, n_seq):
    t_tok = qkv.shape[0]
    seq = t_tok // n_seq
    nq = seq // ATT_BLK
    blk = ATT_BLK

    def body(q_ref, k_ref, v_ref, c_ref, _cat, o_ref, lse_ref):
        qi = pl.program_id(2)
        first, _ = _head_masks()
        causal = _tril(blk)
        one = jnp.ones((1, 128), BF16)
        qh = []
        for hp in range(ATT_PAIRS):
            qs = q_ref[:, 128 * hp:128 * hp + 128] * ATT_SCALE
            zero = jnp.zeros_like(qs)
            qh += [jnp.where(first, qs, zero), jnp.where(first, zero, qs)]

        def step(kb, carry, masked):
            ms, accs = carry
            rows = pl.ds(pl.multiple_of(kb * blk, blk), blk)
            new_m, new_acc = [], []
            for hp in range(ATT_PAIRS):
                k = k_ref[rows, 128 * hp:128 * hp + 128]
                v = v_ref[rows, 128 * hp:128 * hp + 128]
                for h in range(2):
                    i = 2 * hp + h
                    s = _dot_nt(qh[i], k) - c_ref[i, kb]
                    if masked:
                        s = jnp.where(causal, s, NEG)
                    m_new = jnp.maximum(ms[i], jnp.max(s, axis=1, keepdims=True))
                    p = jnp.exp(s - m_new)
                    vh = jnp.where(first, v, one) if h == 0 else jnp.where(first, one, v)
                    new_acc.append(accs[i] * jnp.exp(ms[i] - m_new) + _dot(p.astype(BF16), vh))
                    new_m.append(m_new)
            return tuple(new_m), tuple(new_acc)

        n_heads = 2 * ATT_PAIRS
        col = jnp.full((blk, 1), NEG, F32)
        zacc = jnp.zeros((blk, 128), F32)
        carry = lax.fori_loop(0, qi, lambda kb, cr: step(kb, cr, False), ((col,) * n_heads, (zacc,) * n_heads))
        ms, accs = step(qi, carry, True)
        for hp in range(ATT_PAIRS):
            acc0, acc1 = accs[2 * hp], accs[2 * hp + 1]
            l0 = pltpu.roll(acc0, 64, 1)
            l1 = pltpu.roll(acc1, 64, 1)
            o_ref[:, 128 * hp:128 * hp + 128] = jnp.where(first, acc0 / l0, acc1 / l1).astype(BF16)
            lse_ref[:, 128 * hp:128 * hp + 128] = jnp.where(first, ms[2 * hp] + jnp.log(l0), ms[2 * hp + 1] + jnp.log(l1))

    wide = 128 * ATT_PAIRS
    n_grp = D_FOX // wide
    first_col = D_CONV // wide
    return pl.pallas_call(
        body, name="fox_fwd", grid=(n_seq, n_grp, nq),
        in_specs=[pl.BlockSpec((blk, wide), lambda b, g, qi: (b * nq + qi, g)),
                  pl.BlockSpec((seq, wide), lambda b, g, qi: (b, n_grp + g)),
                  pl.BlockSpec((seq, wide), lambda b, g, qi: (b, 2 * n_grp + g)),
                  pl.BlockSpec((2 * ATT_PAIRS, nq, 1, blk), lambda b, g, qi: (b * n_grp + g, 0, 0, 0)), _anyspec()],
        out_specs=[pl.BlockSpec((blk, wide), lambda b, g, qi: (b * nq + qi, first_col + g)),
                   pl.BlockSpec((blk, wide), lambda b, g, qi: (b * nq + qi, g))],
        out_shape=[_hbm(cat.shape, BF16), _hbm((t_tok, D_FOX), F32)],
        input_output_aliases={4: 0},
        compiler_params=_params(48, ("arbitrary", "arbitrary", "arbitrary")),
    )(qkv, qkv, qkv, cum_t, cat)


def _fox_bwd(qkv, cum_t, cat, lse, d_o, n_seq):
    t_tok = qkv.shape[0]
    seq = t_tok // n_seq
    nk = seq // ATT_BLK
    blk = ATT_BLK

    def body(q_ref, k_ref, v_ref, c_ref, o_ref, lse_ref, do_ref, dq_ref, dk_ref, dv_ref, drow_ref, dcol_ref):
        kb = pl.program_id(2)
        first, lane = _head_masks()
        second = jnp.logical_not(first)
        one = jnp.ones((1, 128), BF16)
        causal = _tril(blk)

        @pl.when(kb == 0)
        def _():
            dq_ref[...] = jnp.zeros_like(dq_ref)
            drow_ref[...] = jnp.zeros_like(drow_ref)

        def step(qi, carry, masked):
            rows = pl.ds(pl.multiple_of(qi * blk, blk), blk)
            dks, dvs = carry
            new_dk, new_dv = [], []
            for hp in range(ATT_PAIRS):
                cols = slice(128 * hp, 128 * hp + 128)
                k = k_ref[:, cols]
                v = v_ref[:, cols]
                ks = k * ATT_SCALE
                zero = jnp.zeros_like(k)
                qs = q_ref[rows, cols] * ATT_SCALE
                d_o = do_ref[rows, cols]
                dd = d_o.astype(F32) * o_ref[rows, cols].astype(F32)
                lse_t = lse_ref[rows, cols]
                dq = []
                for h, mine in enumerate((first, second)):
                    i = 2 * hp + h
                    qh = jnp.where(mine, qs, zero)
                    doh = jnp.where(mine, d_o, zero)
                    delta = jnp.sum(jnp.where(mine, dd, 0.0), axis=1, keepdims=True)
                    lse_h = jnp.sum(jnp.where(lane == 64 * h, lse_t, 0.0), axis=1, keepdims=True)
                    s = _dot_nt(qh, k) - c_ref[i]
                    if masked:
                        s = jnp.where(causal, s, NEG)
                    p = jnp.exp(s - lse_h)
                    ds = (p * (_dot_nt(doh, v) - delta)).astype(BF16)
                    new_dk.append(dks[i] + _dot_tn(ds, jnp.where(mine, qs, one)))
                    new_dv.append(dvs[i] + _dot_tn(p.astype(BF16), doh))
                    dq.append(_dot(ds, jnp.where(mine, ks, one)))
                dq_ref[rows, cols] += jnp.where(first, dq[0], dq[1])
                drow_ref[rows, cols] += jnp.where(first, dq[1], dq[0])
            return tuple(new_dk), tuple(new_dv)

        zt = (jnp.zeros((blk, 128), F32),) * (2 * ATT_PAIRS)
        carry = step(kb, (zt, zt), True)
        dks, dvs = lax.fori_loop(kb + 1, nk, lambda qi, cr: step(qi, cr, False), carry)
        for hp in range(ATT_PAIRS):
            cols = slice(128 * hp, 128 * hp + 128)
            dk_ref[:, cols] = jnp.where(first, dks[2 * hp], dks[2 * hp + 1]).astype(BF16)
            dcol_ref[:, cols] = jnp.where(first, dks[2 * hp + 1], dks[2 * hp])
            dv_ref[:, cols] = (dvs[2 * hp] + dvs[2 * hp + 1]).astype(BF16)

    wide = 128 * ATT_PAIRS
    n_grp = D_FOX // wide

    def seq_spec(col0):
        return pl.BlockSpec((seq, wide), lambda b, g, kb: (b, col0 + g))

    def key_spec(col0):
        return pl.BlockSpec((blk, wide), lambda b, g, kb: (b * nk + kb, col0 + g))

    return pl.pallas_call(
        body, name="fox_bwd", grid=(n_seq, n_grp, nk),
        in_specs=[seq_spec(0), key_spec(n_grp), key_spec(2 * n_grp),
                  pl.BlockSpec((2 * ATT_PAIRS, None, 1, blk), lambda b, g, kb: (b * n_grp + g, kb, 0, 0)),
                  seq_spec(D_CONV // wide), seq_spec(0), seq_spec(0)],
        out_specs=[seq_spec(0), key_spec(0), key_spec(0), seq_spec(0), key_spec(0)],
        out_shape=[_hbm((t_tok, D_FOX), F32), _hbm((t_tok, D_FOX), BF16),
                   _hbm((t_tok, D_FOX), BF16), _hbm((t_tok, D_FOX), F32),
                   _hbm((t_tok, D_FOX), F32)],
        compiler_params=_params(56, ("arbitrary", "arbitrary", "arbitrary")),
    )(qkv, qkv, qkv, cum_t, cat, lse, d_o)


def _mix_out_fwd(cat, x, w_out, ln_g, ln_b, tm=512):
    t_tok = x.shape[0]

    def body(cat_ref, x_ref, w_ref, g_ref, b_ref, xn_ref, xnb_ref, z_ref):
        z = ALPHA * x_ref[...] + _dot(cat_ref[...], w_ref[...])
        xhat, _ = _ln_stats(z)
        xn = xhat * g_ref[...] + b_ref[...]
        z_ref[...] = z
        xn_ref[...] = xn
        xnb_ref[...] = xn.astype(BF16)

    def tok(n):
        return pl.BlockSpec((tm, n), lambda i: (i, 0))

    vec = pl.BlockSpec((1, D_MODEL), lambda i: (0, 0))
    return pl.pallas_call(
        body, name="mix_out_fwd", grid=(t_tok // tm,),
        in_specs=[tok(D_MODEL), tok(D_MODEL), pl.BlockSpec((D_MODEL, D_MODEL), lambda i: (0, 0)), vec, vec],
        out_specs=[tok(D_MODEL)] * 3,
        out_shape=[_hbm((t_tok, D_MODEL), F32), _hbm((t_tok, D_MODEL), BF16),
                   _hbm((t_tok, D_MODEL), F32)],
        compiler_params=_params(40, ("arbitrary",)),
    )(cat, x, w_out, ln_g, ln_b)


def _mix_out_bwd(dxn, z, w_out, ln_g, tm=512):
    t_tok = dxn.shape[0]

    def body(dxn_ref, z_ref, w_ref, g_ref, dz_ref, dzb_ref, dya_ref, dyb_ref, dyc_ref, dg_ref, db_ref):
        i = pl.program_id(0)
        dxn_t = dxn_ref[...]
        xhat, rstd = _ln_stats(z_ref[...])
        pg = jnp.sum(dxn_t * xhat, axis=0, keepdims=True)
        pb = jnp.sum(dxn_t, axis=0, keepdims=True)

        @pl.when(i == 0)
        def _():
            dg_ref[...] = pg
            db_ref[...] = pb

        @pl.when(i > 0)
        def _():
            dg_ref[...] += pg
            db_ref[...] += pb

        dz = _ln_bwd(dxn_t, xhat, rstd, g_ref[...])
        dzb = dz.astype(BF16)
        dz_ref[...] = dz
        dzb_ref[...] = dzb
        dya_ref[...] = _dot_nt(dzb, w_ref[0:256, :])
        dyb_ref[...] = _dot_nt(dzb, w_ref[256:768, :]).astype(BF16)
        dyc_ref[...] = _dot_nt(dzb, w_ref[768:1024, :])

    def tok(n):
        return pl.BlockSpec((tm, n), lambda i: (i, 0))

    vec = pl.BlockSpec((1, D_MODEL), lambda i: (0, 0))
    return pl.pallas_call(
        body, name="mix_out_bwd", grid=(t_tok // tm,),
        in_specs=[tok(D_MODEL), tok(D_MODEL), pl.BlockSpec((D_MODEL, D_MODEL), lambda i: (0, 0)), vec],
        out_specs=[tok(D_MODEL), tok(D_MODEL), tok(256), tok(512), tok(256), vec, vec],
        out_shape=[_hbm((t_tok, D_MODEL), F32), _hbm((t_tok, D_MODEL), BF16),
                   _hbm((t_tok, 256), F32), _hbm((t_tok, 512), BF16),
                   _hbm((t_tok, 256), F32),
                   _hbm((1, D_MODEL), F32), _hbm((1, D_MODEL), F32)],
        compiler_params=_params(40, ("arbitrary",)),
    )(dxn, z, w_out, ln_g)


def _conv_bwd(conv, dya, conv_w, n_seq):
    t_tok = conv.shape[0]
    seq = t_tok // n_seq

    def body(conv_ref, dya_ref, cw_ref, dconv_ref, dcw_ref):
        @pl.when(pl.program_id(0) == 0)
        def _():
            dcw_ref[...] = jnp.zeros_like(dcw_ref)

        z = conv_ref[:, 256:512] * conv_ref[:, 512:768]
        z1 = _shift_down(z, 1)
        z2 = _shift_down(z, 2)
        y = cw_ref[0:1, :] * z2 + cw_ref[1:2, :] * z1 + cw_ref[2:3, :] * z
        dya_t = dya_ref[...]
        dconv_ref[:, 0:256] = (dya_t * y).astype(BF16)
        dy = dya_t * conv_ref[:, 0:256]
        dcw_ref[0:1, :] += jnp.sum(dy * z2, axis=0, keepdims=True)
        dcw_ref[1:2, :] += jnp.sum(dy * z1, axis=0, keepdims=True)
        dcw_ref[2:3, :] += jnp.sum(dy * z, axis=0, keepdims=True)
        dz = cw_ref[2:3, :] * dy + cw_ref[1:2, :] * _shift_up(dy, 1) + cw_ref[0:1, :] * _shift_up(dy, 2)
        dconv_ref[:, 256:512] = (dz * conv_ref[:, 512:768]).astype(BF16)
        dconv_ref[:, 512:768] = (dz * conv_ref[:, 256:512]).astype(BF16)

    def seq_blk(n):
        return pl.BlockSpec((seq, n), lambda b: (b, 0))

    par = pl.BlockSpec((8, 256), lambda b: (0, 0))
    return pl.pallas_call(
        body, name="conv_bwd", grid=(n_seq,),
        in_specs=[seq_blk(768), seq_blk(256), par], out_specs=[seq_blk(768), par],
        out_shape=[_hbm((t_tok, 768), BF16), _hbm((8, 256), F32)],
        compiler_params=_params(56, ("arbitrary",)),
    )(conv, dya, conv_w)


def _sgu_gate_bwd(sgu, f, dyc, drow, dcol, b_f, sgu_g, sgu_b, w_s, b_mat, n_seq):
    t_tok = sgu.shape[0]
    seq = t_tok // n_seq
    n_chunk = seq // SGU_CHUNK

    def body(sgu_ref, f_ref, dyc_ref, drow_ref, dcol_ref, bf_ref, lg_ref, lb_ref, ws_ref, bm_ref,
             dsgu_ref, df_ref, dbf_ref, dlg_ref, dlb_ref, dws_ref, dbs_ref, dbm_acc):
        b = pl.program_id(0)

        @pl.when(b == 0)
        def _():
            for r in (dbf_ref, dlg_ref, dlb_ref, dws_ref, dbm_acc):
                r[...] = jnp.zeros_like(r)

        tril = _tril(SGU_CHUNK)
        grp = _sgu_group_of_lane()
        wc = [jnp.where(tril, ws_ref[g], 0.0).astype(BF16) for g in range(N_SGU_GROUPS)]
        for n in range(n_chunk):
            rows = pl.ds(n * SGU_CHUNK, SGU_CHUNK)
            su = sgu_ref[rows, 0:256]
            sv = sgu_ref[rows, 256:512]
            u = _gelu(su)
            vhat, rstd = _ln_stats(_gelu(sv))
            vn = (vhat * lg_ref[...] + lb_ref[...]).astype(BF16)
            mixed = bm_ref[...]
            for g in range(N_SGU_GROUPS):
                mixed = mixed + jnp.where(grp == g, _dot(wc[g], vn), 0.0)
            dyc_t = dyc_ref[rows, :]
            dsgu_ref[rows, 0:256] = (dyc_t * mixed * _gelu_grad(su)).astype(BF16)
            dmixed = dyc_t * u
            dbm_acc[...] += dmixed
            dvn = jnp.zeros((SGU_CHUNK, D_SGU), F32)
            for g in range(N_SGU_GROUPS):
                dm_g = jnp.where(grp == g, dmixed, 0.0).astype(BF16)
                dws_ref[g] += _dot_nt(dm_g, vn)
                dvn = dvn + _dot_tn(wc[g], dm_g)
            dlg_ref[...] += jnp.sum(dvn * vhat, axis=0, keepdims=True)
            dlb_ref[...] += jnp.sum(dvn, axis=0, keepdims=True)
            dsgu_ref[rows, 256:512] = (_ln_bwd(dvn, vhat, rstd, lg_ref[...]) * _gelu_grad(sv)).astype(BF16)

        later = (lax.broadcasted_iota(jnp.int32, (128, 128), 0) <= lax.broadcasted_iota(jnp.int32, (128, 128), 1)).astype(F32)
        head = lax.broadcasted_iota(jnp.int32, (D_FOX, 128), 1)
        pick = (lax.broadcasted_iota(jnp.int32, (D_FOX, 128), 0) == 128 * (head // 2) + 64 * (1 - head % 2)).astype(F32)
        carry = jnp.zeros((1, 128), F32)
        for n in reversed(range(n_chunk)):
            rows = pl.ds(n * SGU_CHUNK, SGU_CHUNK)
            dcum_n = _dot(drow_ref[rows, :] - dcol_ref[rows, :], pick, HIGHEST)
            dlf = _dot(later, dcum_n, HIGHEST) + carry
            carry = carry + jnp.sum(dcum_n, axis=0, keepdims=True)
            df = dlf * jax.nn.sigmoid(-(f_ref[rows, :] + bf_ref[...]))
            df_ref[rows, :] = df.astype(BF16)
            dbf_ref[...] += jnp.sum(df, axis=0, keepdims=True)

        @pl.when(b == n_seq - 1)
        def _():
            for g in range(N_SGU_GROUPS):
                dws_ref[g] = jnp.where(tril, dws_ref[g], 0.0)
            sel = (lax.broadcasted_iota(jnp.int32, (D_SGU, 128), 0) // (D_SGU // N_SGU_GROUPS)
                   == lax.broadcasted_iota(jnp.int32, (D_SGU, 128), 1)).astype(F32)
            dbs_ref[...] = _dot(dbm_acc[...], sel, HIGHEST)

    def seq_blk(n):
        return pl.BlockSpec((seq, n), lambda b: (b, 0))

    def full(shape):
        return pl.BlockSpec(shape, lambda b: (0,) * len(shape))

    param_shapes = [(1, 128), (1, 256), (1, 256), (4, 128, 128), (128, 128)]
    return pl.pallas_call(
        body, name="sgu_gate_bwd", grid=(n_seq,),
        in_specs=[seq_blk(512), seq_blk(128), seq_blk(256), seq_blk(D_FOX), seq_blk(D_FOX),
                  full((1, 128)), full((1, 256)), full((1, 256)), full((4, 128, 128)), full((128, 256))],
        out_specs=[seq_blk(512), seq_blk(128)] + [full(s) for s in param_shapes],
        out_shape=[_hbm((t_tok, 512), BF16), _hbm((t_tok, 128), BF16)]
        + [_hbm(s, F32) for s in param_shapes],
        scratch_shapes=[pltpu.VMEM((128, 256), F32)],
        compiler_params=_params(48, ("arbitrary",)),
    )(sgu, f, dyc, drow, dcol, b_f, sgu_g, sgu_b, w_s, b_mat)


def _mix_in_bwd(dconv, dq, dk, dv, dsgu, df, dz, w_in, tm=512):
    t_tok = dz.shape[0]

    def body(dconv_ref, dq_ref, dk_ref, dv_ref, dsgu_ref, df_ref, dz_ref, w_ref, dx_ref, dp_ref):
        dqb = dq_ref[...].astype(BF16)
        pieces = [(COL_CONV, dconv_ref[...]), (COL_QKV, dqb), (COL_QKV + 512, dk_ref[...]), (COL_QKV + 1024, dv_ref[...]),
                  (COL_SGU, dsgu_ref[...]), (COL_F, df_ref[...])]
        dx = ALPHA * dz_ref[...]
        for col, val in pieces:
            width = val.shape[1]
            dp_ref[:, col:col + width] = val
            dx = dx + _dot_nt(val, w_ref[:, col:col + width])
        dx_ref[...] = dx

    def tok(n):
        return pl.BlockSpec((tm, n), lambda i: (i, 0))

    return pl.pallas_call(
        body, name="mix_in_bwd", grid=(t_tok // tm,),
        in_specs=[tok(768), tok(512), tok(512), tok(512), tok(512), tok(128), tok(D_MODEL),
                  pl.BlockSpec((D_MODEL, D_IN_PAD), lambda i: (0, 0))],
        out_specs=[tok(D_MODEL), tok(D_IN_PAD)],
        out_shape=[_hbm((t_tok, D_MODEL), F32), _hbm((t_tok, D_IN_PAD), BF16)],
        compiler_params=_params(48, ("arbitrary",)),
    )(dconv, dq, dk, dv, dsgu, df, dz, w_in)


def _pad_rows(a, rows):
    return jnp.pad(a, ((0, rows - a.shape[0]), (0, 0)))


F_BLOCK = F_ORIG // D_IN_SHARD
F_AT = F_ORIG - F_BLOCK * D_IN_SHARD
assert (F_ORIG + N_HEADS) // D_IN_SHARD == F_BLOCK


def _w_in_from_blocks(g):
    fb = g[F_BLOCK]
    zeros = jnp.zeros((D_MODEL, D_IN_PAD - COL_F - N_HEADS), g.dtype)
    return jnp.concatenate([g[d] for d in range(F_BLOCK)] + [fb[:, :F_AT], fb[:, F_AT + N_HEADS:]]
                           + [g[d] for d in range(F_BLOCK + 1, N_DEV)] + [fb[:, F_AT:F_AT + N_HEADS], zeros], axis=1)


def _w_in_to_blocks(dw):
    def cols(lo, hi):
        shift = 0 if hi <= F_ORIG else N_HEADS
        return dw[:, lo - shift:hi - shift]

    blocks = []
    for d in range(N_DEV):
        lo, hi = d * D_IN_SHARD, (d + 1) * D_IN_SHARD
        if d == F_BLOCK:
            blocks.append(jnp.concatenate([cols(lo, F_ORIG), dw[:, COL_F:COL_F + N_HEADS], cols(F_ORIG + N_HEADS, hi)], axis=1))
        else:
            blocks.append(cols(lo, hi))
    return jnp.stack(blocks)


LN1_ROWS = 2 * 8
REST_ROWS = 4 * 8 + 2 * 8 + 512 + 8 + 8 + 8


def _pack_rest(p):
    rows = [p[name].reshape(8, 128) for name in ("ln2_g", "ln2_b", "ln3_g", "ln3_b")]
    rows += [_pad_rows(p[name].reshape(2, 128), 8) for name in ("sgu_ln_g", "sgu_ln_b")]
    rows += [p["sgu_w_s"].reshape(512, 128), _pad_rows(p["sgu_b_s"], 8),
             _pad_rows(jnp.pad(p["fox_b_f"], (0, 128 - N_HEADS)).reshape(1, 128), 8), _pad_rows(p["conv_w"].reshape(6, 128), 8)]
    return jnp.concatenate(rows, axis=0)


def _pack_layer(p):
    return jnp.concatenate([p["ln1_g"].reshape(8, 128), p["ln1_b"].reshape(8, 128), _pack_rest(p)], axis=0)


def _unpack_layer(a):
    r = 0

    def take(n, valid):
        nonlocal r
        piece = a[r:r + valid]
        r += n
        return piece

    d = {}
    for name in ("ln1_g", "ln1_b", "ln2_g", "ln2_b", "ln3_g", "ln3_b"):
        d[name] = take(8, 8).reshape(D_MODEL)
    for name in ("sgu_ln_g", "sgu_ln_b"):
        d[name] = take(8, 2).reshape(D_SGU)
    d["sgu_w_s"] = take(512, 512).reshape(N_SGU_GROUPS, SGU_CHUNK, SGU_CHUNK)
    d["sgu_b_s"] = take(8, 4).reshape(N_SGU_GROUPS, SGU_CHUNK)
    d["fox_b_f"] = take(8, 1).reshape(128)[:N_HEADS]
    d["conv_w"] = take(8, 6).reshape(3, D_CONV)
    return d


SMALL_NAMES = ("ln1_g", "ln1_b", "fox_b_f", "sgu_ln_g", "sgu_ln_b", "sgu_w_s", "sgu_b_s", "ln2_g", "ln2_b", "ln3_g", "ln3_b")
BIG_NAMES = ("ffn1_w_up", "ffn1_w_down", "mix_w_in", "mix_w_out", "ffn2_w_up", "ffn2_w_down")
UP_NAMES = ("ffn1_w_up", "ffn2_w_up")
WEIGHT_ORDER = ("ln1_g", "ln1_b", "ffn1_w_up", "ffn1_w_down", "mix_w_in", "fox_b_f", "conv_w", "sgu_ln_g", "sgu_ln_b",
                "sgu_w_s", "sgu_b_s", "mix_w_out", "ln2_g", "ln2_b", "ffn2_w_up", "ffn2_w_down", "ln3_g", "ln3_b")


class _Overlap:
    def __init__(self, w, after, me, where):
        self.me, self.where = me, where
        groups = [[("ffn1_w_up", 0), ("ffn1_w_down", 0)],
                  [("mix_w_in", 0), ("mix_w_out", 0), ("ffn2_w_up", 0), ("ffn2_w_down", 0)]]
        groups += [[(name, l) for name in BIG_NAMES] for l in range(1, DEPTH)]
        self.gathers = []
        for gi, group in enumerate(groups):
            shards = [w[name][l].astype(BF16) for name, l in group]
            lands = [lax.dynamic_update_slice(lax.empty((N_DEV,) + s.shape, BF16), s[None], (me, 0, 0)) for s in shards]
            started = _exchange_start(f"allgather_start_{gi}", _gather_plan(len(group)), 3 * len(group), shards + lands, after)
            after = started[3]
            self.gathers.append(dict(group=group, chips=started))
        self.all_started = self.last = after
        self.scatters = {}
        self.order = []
        self.small = []

    def _start(self, name, plan, n_copies, arrays):
        started = _exchange_start(name, plan, n_copies, arrays, self.last)
        self.last = started[3]
        return started

    def _group_of(self, layer, part):
        return layer + 1 if layer > 0 else (0 if part == "ffn1" else 1)

    def pass_on(self, layer, part, after):
        st = self.gathers[self._group_of(layer, part)]
        if "sibling" not in st:
            gi, m = self._group_of(layer, part), len(st["group"])
            arrays = _exchange_wait(f"allgather_wait_{gi}", _gather_plan(m), 3 * m, st["chips"], after)
            st["sibling"] = self._start(f"allgather_pass_start_{gi}", _pass_on_plan(m), 4 * m, arrays)
        return st["sibling"][3]

    def weights(self, layer, part, after):
        gi = self._group_of(layer, part)
        st = self.gathers[gi]
        if "full" not in st:
            after = self.all_started if after is None else after
            self.pass_on(layer, part, after)
            m = len(st["group"])
            arrays = _exchange_wait(f"allgather_pass_wait_{gi}", _pass_on_plan(m), 4 * m, st["sibling"], after)
            st["full"] = dict(zip(st["group"], arrays[m:]))
        g = st["full"]

        def ffn(n):
            return g[(f"ffn{n}_w_up", layer)], g[(f"ffn{n}_w_down", layer)].reshape(N_FFN_CHUNK, FFN_BLK, D_MODEL)

        if part == "ffn1":
            return ffn(1)
        return (_w_in_from_blocks(g[("mix_w_in", layer)]), g[("mix_w_out", layer)].reshape(D_MODEL, D_MODEL), *ffn(2))

    def push(self, key, items):
        n = len(items)
        grads = [g for _, _, g in items]
        lands = [lax.empty((4,) + g.shape[1:], F32) for g in grads]
        started = self._start(f"rs_sibling_start_{key[0]}{key[1]}", _sibling_plan(n), 4 * n, grads + lands)
        self.scatters[key] = dict(items=items, sibling=started)
        self.order.append(key)
        return started[3]

    def advance(self, key, after):
        st = self.scatters[key]
        n = len(st["items"])
        arrays = _exchange_wait(f"rs_sibling_wait_{key[0]}{key[1]}", _sibling_plan(n), 4 * n, st["sibling"], after)
        partials = [_chip_partial(g, r, self.where) for g, r in zip(arrays[:n], arrays[n:])]
        p16 = [p for _, p in partials]
        lands = [lax.empty((3,) + p.shape[1:], BF16) for p in p16]
        started = self._start(f"rs_chip_start_{key[0]}{key[1]}", _chip_plan(n), 3 * n, p16 + lands)
        st.update(own32=[p for p, _ in partials], chip=started)
        return started[3]

    def push_small(self, rows):
        k = len(self.small)
        land = lax.dynamic_update_slice(lax.empty((N_DEV,) + rows.shape, F32), rows[None], (self.me, 0, 0))
        started = self._start(f"small_start_{k}", _peers_plan(), N_DEV - 1, [rows, land])
        self.small.append(started)
        return started[3]

    def finish(self, w, m, v):
        res = {}
        after = self.scatters[self.order[-1]]["chip"][3]
        for key in self.order:
            st = self.scatters[key]
            n = len(st["items"])
            arrays = _exchange_wait(f"rs_chip_wait_{key[0]}{key[1]}", _chip_plan(n), 3 * n, st["chip"], after)
            for (name, l, _), own32, r16 in zip(st["items"], st["own32"], arrays[n:]):
                res[name] = _adamw_shard(own32, r16, w[name], m[name], v[name], l, res.get(name))
                after = res[name][0]
        pieces = [_exchange_wait(f"small_wait_{k}", _peers_plan(), N_DEV - 1, started, after)[1]
                  for k, started in enumerate(self.small)]
        return res, pieces


def _local_step(x, target, comm, small, n_seq):
    def vec(a):
        return a.reshape(1, -1)

    saved = []
    h = x
    for l in range(DEPTH):
        s = {}
        s["up1"], s["down1"] = comm.weights(l, "ffn1", None if l == 0 else h)
        h1, h1b, s["z1"], s["gu1"], s["x0b"] = _ffn_fwd(h, s["up1"], s["down1"], vec(small["ln1_g"][l]), vec(small["ln1_b"][l]), h)
        s["w_in"], s["w_out"], s["up2"], s["down2"] = comm.weights(l, "rest", s["z1"])
        s["x1b"] = h1b
        conv, qkv, sgu, f = _in_proj(h1, s["w_in"])
        cw = _pad_rows(small["conv_w"][l], 8)
        bf = jnp.pad(small["fox_b_f"][l], (0, 128 - N_HEADS)).reshape(1, 128)
        b_mat = jnp.repeat(small["sgu_b_s"][l].T, D_SGU // N_SGU_GROUPS, axis=1)
        mid_params = (cw, bf, vec(small["sgu_ln_g"][l]), vec(small["sgu_ln_b"][l]), small["sgu_w_s"][l], b_mat)
        cat, cum_t = _mix_mid_fwd(conv, sgu, f, *mid_params, n_seq)
        cat, lse = _fox_fwd(qkv, cum_t, cat, n_seq)
        h2, h2b, s["z2"] = _mix_out_fwd(cat, h1, s["w_out"], vec(small["ln2_g"][l]), vec(small["ln2_b"][l]))
        s.update(conv=conv, qkv=qkv, sgu=sgu, f=f, mid_params=mid_params, cat=cat, cum_t=cum_t, lse=lse, x2b=h2b)
        token = comm.pass_on(l + 1, "ffn1", s["z2"]) if l + 1 < DEPTH else h2
        ln3 = (vec(small["ln3_g"][l]), vec(small["ln3_b"][l]))
        if l + 1 < DEPTH:
            h, _, s["z3"], s["gu2"], _ = _ffn_fwd(h2, s["up2"], s["down2"], *ln3, token)
        else:
            dh, loss, s["z3"], s["gu2"] = _ffn_fwd(h2, s["up2"], s["down2"], *ln3, token, target)
        saved.append(s)

    late_rows = None
    token = loss
    pending = None
    for l in reversed(range(DEPTH)):
        s = saved[l]
        sg = {}
        dh, dy, a, dgu, sg["ln3_g"], sg["ln3_b"] = _ffn_bwd(dh, s["z3"], s["gu2"], s["up2"], s["down2"], vec(small["ln3_g"][l]), token)
        if pending is not None:
            token = comm.advance(pending, dh)
        g_up2 = _matmul_tn(dgu.reshape(N_DEV, -1, FFN_BLK), s["x2b"][None], token)
        g_down2 = _matmul_tn(a, dy[None], token).reshape(N_DEV, FFN_BLK // 2, D_MODEL)
        dz, dzb, dya, dyb, dyc, sg["ln2_g"], sg["ln2_b"] = _mix_out_bwd(dh, s["z2"], s["w_out"], vec(small["ln2_g"][l]))
        g_out = _matmul_tn(s["cat"][None], dzb[None], token).reshape(N_DEV, D_MODEL // N_DEV, D_MODEL)
        dq, dk, dv, drow, dcol = _fox_bwd(s["qkv"], s["cum_t"], s["cat"], s["lse"], dyb, n_seq)
        dconv, dcw = _conv_bwd(s["conv"], dya, s["mid_params"][0], n_seq)
        dsgu, df, dbf, dlg, dlb, dws, dbs = _sgu_gate_bwd(s["sgu"], s["f"], dyc, drow, dcol, *s["mid_params"][1:], n_seq)
        sg.update(conv_w=dcw[:3], fox_b_f=dbf[0, :N_HEADS], sgu_ln_g=dlg[0], sgu_ln_b=dlb[0], sgu_w_s=dws,
                  sgu_b_s=dbs[:, :N_SGU_GROUPS].T)
        dh, dp = _mix_in_bwd(dconv, dq, dk, dv, dsgu, df, dz, s["w_in"])
        g_in = _w_in_to_blocks(_matmul_tn(s["x1b"][None], dp[None], token, tk=1024)[0])
        first = [("ffn2_w_up", l, g_up2), ("ffn2_w_down", l, g_down2), ("mix_w_out", l, g_out), ("mix_w_in", l, g_in)]
        for name in ("ln2_g", "ln2_b", "ln3_g", "ln3_b"):
            sg[name] = sg[name][0]
        if l == 0:
            comm.push((l, "a"), first)
            token = comm.push_small(_pack_rest(sg))
            pending, first = (l, "a"), []
        dh, dy, a, dgu, dg1, db1 = _ffn_bwd(dh, s["z1"], s["gu1"], s["up1"], s["down1"], vec(small["ln1_g"][l]), token)
        if l == 0:
            token = comm.advance(pending, dh)
        g_up1 = _matmul_tn(dgu.reshape(N_DEV, -1, FFN_BLK), s["x0b"][None], token)
        ln1_rows = jnp.concatenate([dg1.reshape(8, 128), db1.reshape(8, 128)], axis=0)
        if l == 0:
            token = comm.push((l, "b"), [("ffn1_w_up", l, g_up1)])
            g_down1 = _matmul_tn(a, dy[None], token).reshape(N_DEV, FFN_BLK // 2, D_MODEL)
            token = comm.advance((l, "b"), g_down1)
            token = comm.push((l, "c"), [("ffn1_w_down", l, g_down1)])
            token = comm.advance((l, "c"), token)
            late_rows = ln1_rows
        else:
            g_down1 = _matmul_tn(a, dy[None], token).reshape(N_DEV, FFN_BLK // 2, D_MODEL)
            pending = (l, "b")
            comm.push(pending, first + [("ffn1_w_up", l, g_up1), ("ffn1_w_down", l, g_down1)])
            token = comm.push_small(jnp.concatenate([ln1_rows, _pack_rest(sg)], axis=0))
    return loss, dh, late_rows


def kernel(x, ln1_g, ln1_b, ffn1_w_up, ffn1_w_down, mix_w_in, fox_b_f, conv_w, sgu_ln_g, sgu_ln_b, sgu_w_s, sgu_b_s, mix_w_out, ln2_g, ln2_b, ffn2_w_up, ffn2_w_down, ln3_g, ln3_b, loss_target, m_ln1_g, m_ln1_b, m_ffn1_w_up, m_ffn1_w_down, m_mix_w_in, m_fox_b_f, m_conv_w, m_sgu_ln_g, m_sgu_ln_b, m_sgu_w_s, m_sgu_b_s, m_mix_w_out, m_ln2_g, m_ln2_b, m_ffn2_w_up, m_ffn2_w_down, m_ln3_g, m_ln3_b, v_ln1_g, v_ln1_b, v_ffn1_w_up, v_ffn1_w_down, v_mix_w_in, v_fox_b_f, v_conv_w, v_sgu_ln_g, v_sgu_ln_b, v_sgu_w_s, v_sgu_b_s, v_mix_w_out, v_ln2_g, v_ln2_b, v_ffn2_w_up, v_ffn2_w_down, v_ln3_g, v_ln3_b):
    w = dict(ln1_g=ln1_g, ln1_b=ln1_b, ffn1_w_up=ffn1_w_up, ffn1_w_down=ffn1_w_down, mix_w_in=mix_w_in, fox_b_f=fox_b_f,
             conv_w=conv_w, sgu_ln_g=sgu_ln_g, sgu_ln_b=sgu_ln_b, sgu_w_s=sgu_w_s, sgu_b_s=sgu_b_s, mix_w_out=mix_w_out,
             ln2_g=ln2_g, ln2_b=ln2_b, ffn2_w_up=ffn2_w_up, ffn2_w_down=ffn2_w_down, ln3_g=ln3_g, ln3_b=ln3_b)
    m = dict(ln1_g=m_ln1_g, ln1_b=m_ln1_b, ffn1_w_up=m_ffn1_w_up, ffn1_w_down=m_ffn1_w_down, mix_w_in=m_mix_w_in,
             fox_b_f=m_fox_b_f, conv_w=m_conv_w, sgu_ln_g=m_sgu_ln_g, sgu_ln_b=m_sgu_ln_b, sgu_w_s=m_sgu_w_s,
             sgu_b_s=m_sgu_b_s, mix_w_out=m_mix_w_out, ln2_g=m_ln2_g, ln2_b=m_ln2_b, ffn2_w_up=m_ffn2_w_up,
             ffn2_w_down=m_ffn2_w_down, ln3_g=m_ln3_g, ln3_b=m_ln3_b)
    v = dict(ln1_g=v_ln1_g, ln1_b=v_ln1_b, ffn1_w_up=v_ffn1_w_up, ffn1_w_down=v_ffn1_w_down, mix_w_in=v_mix_w_in,
             fox_b_f=v_fox_b_f, conv_w=v_conv_w, sgu_ln_g=v_sgu_ln_g, sgu_ln_b=v_sgu_ln_b, sgu_w_s=v_sgu_w_s,
             sgu_b_s=v_sgu_b_s, mix_w_out=v_mix_w_out, ln2_g=v_ln2_g, ln2_b=v_ln2_b, ffn2_w_up=v_ffn2_w_up,
             ffn2_w_down=v_ffn2_w_down, ln3_g=v_ln3_g, ln3_b=v_ln3_b)

    mx, my, mc = lax.axis_index("x"), lax.axis_index("y"), lax.axis_index("c")
    me = 4 * mx + 2 * my + mc
    n_seq, seq, _ = x.shape
    t_tok = n_seq * seq
    for name in UP_NAMES:
        for t in (w, m, v):
            t[name] = jnp.transpose(t[name], (0, 2, 1))

    comm = _Overlap(w, x, me, jnp.stack([mc, 2 * mx + my]).astype(jnp.int32))
    cw_rows = _pad_rows(conv_w.reshape(DEPTH * 3, D_CONV // N_DEV), 8)
    cw_all = _allgather_small(jnp.pad(cw_rows, ((0, 0), (0, 128 - D_CONV // N_DEV))) + comm.all_started)
    conv_w_full = jnp.transpose(cw_all[:, :DEPTH * 3, :D_CONV // N_DEV], (1, 0, 2)).reshape(DEPTH, 3, D_CONV)
    small = {name: w[name] for name in SMALL_NAMES}
    small["conv_w"] = conv_w_full

    loss_dev, grad_x, late_rows = _local_step(
        x.reshape(t_tok, D_MODEL), loss_target.reshape(t_tok, D_MODEL), comm, small, n_seq)
    loss = lax.psum(loss_dev[0, 0], ("x", "y", "c"))
    out, pieces = comm.finish(w, m, v)
    for name in UP_NAMES:
        out[name] = [jnp.transpose(a, (0, 2, 1)) for a in out[name]]

    pieces.append(_allgather_small(late_rows))
    spans = [(l, 0, LN1_ROWS + REST_ROWS) for l in reversed(range(1, DEPTH))] + [(0, LN1_ROWS, LN1_ROWS + REST_ROWS), (0, 0, LN1_ROWS)]

    def widen(a):
        return lax.dynamic_update_slice(jnp.zeros((3, D_CONV), F32), a, (0, me * (D_CONV // N_DEV)))

    packed = [[_pack_layer({**{name: t[name][l] for name in SMALL_NAMES}, "conv_w": widen(t["conv_w"][l])}) for l in range(DEPTH)]
              for t in (w, m, v)]
    rows_out = {}
    for (l, lo, hi), gathered_piece in zip(spans, pieces):
        rows_out[(l, lo)] = _adamw_small(gathered_piece, *[packed[t][l][lo:hi] for t in range(3)])
    per_layer = []
    for l in range(DEPTH):
        parts = sorted(lo for (ll, lo) in rows_out if ll == l)
        per_layer.append([_unpack_layer(jnp.concatenate([rows_out[(l, lo)][k] for lo in parts], axis=0)) for k in range(4)])
    for name in SMALL_NAMES:
        out[name] = [jnp.stack([per_layer[l][k][name] for l in range(DEPTH)]) for k in range(4)]
    lo_col = me * (D_CONV // N_DEV)
    out["conv_w"] = [jnp.stack([lax.dynamic_slice(per_layer[l][k]["conv_w"], (0, lo_col), (3, D_CONV // N_DEV)) for l in range(DEPTH)])
                     for k in range(4)]

    return (loss, grad_x.reshape(x.shape), *[out[name][0] for name in WEIGHT_ORDER], *[out[name][1] for name in WEIGHT_ORDER],
            *[out[name][2] for name in WEIGHT_ORDER], *[out[name][3] for name in WEIGHT_ORDER])
```

```python
import functools

import jax
import jax.numpy as jnp
from jax import lax
from jax.experimental import pallas as pl
from jax.experimental.pallas import tpu as pltpu

F32 = jnp.float32
BF16 = jnp.bfloat16
MESH = pl.DeviceIdType.MESH

N_DEV = 8
DEPTH = 2
D_MODEL = 1024
D_FF = 2816
FFN_BLK = 2 * D_FF // N_DEV
N_FFN_CHUNK = D_FF // FFN_BLK
D_CONV = 256
D_FOX = 512
N_HEADS = 8
D_SGU = 256
N_SGU_GROUPS = 4
SGU_CHUNK = 128
D_IN = 3 * D_CONV + 3 * D_FOX + N_HEADS + 2 * D_SGU
D_IN_SHARD = D_IN // N_DEV
COL_CONV, COL_QKV, COL_SGU, COL_F = 0, 768, 2304, 2816
D_IN_PAD = 2944
F_ORIG = 3 * D_CONV + 3 * D_FOX
ALPHA = (2 * DEPTH) ** 0.25
LN_EPS = 1e-5
ATT_SCALE = 0.125
ATT_BLK = 512
ATT_PAIRS = 2
NEG = -1e30

ADAM_LR, ADAM_B1, ADAM_B2, ADAM_EPS, ADAM_WD, ADAM_STEP = 0.001, 0.9, 0.999, 1e-08, 0.01, 10

VMEM_BYTES_V7X = 64 * 1024 * 1024
HIGHEST = lax.Precision.HIGHEST


def _params(vmem_mb, sem=None):
    assert vmem_mb * 1024 * 1024 < VMEM_BYTES_V7X
    kw = dict(vmem_limit_bytes=vmem_mb * 1024 * 1024)
    if sem is not None:
        kw["dimension_semantics"] = sem
    return pltpu.CompilerParams(**kw)


def _dot(a, b, precision=None):
    return lax.dot_general(a, b, (((1,), (0,)), ((), ())), preferred_element_type=F32, precision=precision)


def _dot_nt(a, b):
    return lax.dot_general(a, b, (((1,), (1,)), ((), ())), preferred_element_type=F32)


def _dot_tn(a, b):
    return lax.dot_general(a, b, (((0,), (0,)), ((), ())), preferred_element_type=F32)


def _ln_stats(z):
    mu = jnp.mean(z, axis=-1, keepdims=True)
    zc = z - mu
    var = jnp.mean(zc * zc, axis=-1, keepdims=True)
    rstd = lax.rsqrt(var + LN_EPS)
    return zc * rstd, rstd


def _ln_bwd(dy, xhat, rstd, g):
    dxh = dy * g
    m1 = jnp.mean(dxh, axis=-1, keepdims=True)
    m2 = jnp.mean(dxh * xhat, axis=-1, keepdims=True)
    return rstd * (dxh - m1 - xhat * m2)


_GELU_C = 0.7978845608028654


def _gelu(x):
    return 0.5 * x * (1.0 + jnp.tanh(_GELU_C * (x + 0.044715 * x * x * x)))


def _gelu_grad(x):
    t = jnp.tanh(_GELU_C * (x + 0.044715 * x * x * x))
    return 0.5 * (1.0 + t) + 0.5 * x * (1.0 - t * t) * _GELU_C * (1.0 + 3 * 0.044715 * x * x)


def _hbm(shape, dtype):
    n = 1
    for d in shape:
        n *= d
    if n * jnp.dtype(dtype).itemsize >= 1024 * 1024:
        return pltpu.HBM(tuple(shape), dtype)
    return jax.ShapeDtypeStruct(tuple(shape), dtype)


def _vspec():
    return pl.BlockSpec(memory_space=pltpu.VMEM)


def _anyspec():
    return pl.BlockSpec(memory_space=pl.ANY)


def _mesh_pos():
    return lax.axis_index("x"), lax.axis_index("y"), lax.axis_index("c")


def _other_chips(x, y):
    return [(1 - x, y), (x, 1 - y), (1 - x, 1 - y)]


_HBM_SPEC = pl.BlockSpec(memory_space=pltpu.HBM)
_SEM_SPEC = pl.BlockSpec(memory_space=pltpu.SEMAPHORE)
_DATAFLOW_EFFECT = pltpu.SideEffectType.DATAFLOW_SIDE_EFFECTING


def _remote_copies(plan, refs, send_sems, recv_sems):
    return [pltpu.make_async_remote_copy(src_ref=src, dst_ref=dst, send_sem=send_sems.at[k], recv_sem=recv_sems.at[k],
                                         device_id=to, device_id_type=MESH)
            for k, (src, dst, to) in enumerate(plan(refs, *_mesh_pos()))]


def _exchange_start(name, plan, n_copies, arrays, after):
    n = len(arrays)

    def body(*refs):
        send_sems, recv_sems, token = refs[n + 1], refs[n + 2], refs[-1]
        for cp in _remote_copies(plan, refs[:n], send_sems, recv_sems):
            cp.start()
        token[...] = jnp.zeros_like(token)

    out = pl.pallas_call(
        body, name=name,
        out_shape=(pltpu.SemaphoreType.DMA((n_copies,)), pltpu.SemaphoreType.DMA((n_copies,)),
                   *[pltpu.HBM(a.shape, a.dtype) for a in arrays], _hbm((8, 128), F32)),
        in_specs=[_HBM_SPEC] * n + [_anyspec()],
        out_specs=(_SEM_SPEC, _SEM_SPEC, *[_HBM_SPEC] * n, _vspec()),
        input_output_aliases={i: 2 + i for i in range(n)},
        compiler_params=pltpu.CompilerParams(has_side_effects=_DATAFLOW_EFFECT),
    )(*[pltpu.with_memory_space_constraint(a, pltpu.HBM) for a in arrays], after)
    return out[0], out[1], list(out[2:2 + n]), out[-1]


def _exchange_wait(name, plan, n_copies, started, after):
    send_sems, recv_sems, arrays, _ = started
    n = len(arrays)

    def body(*refs):
        for cp in _remote_copies(plan, refs[:n], refs[n], refs[n + 1]):
            cp.wait_send()
            cp.wait_recv()

    out = pl.pallas_call(
        body, name=name,
        out_shape=tuple(pltpu.HBM(a.shape, a.dtype) for a in arrays),
        in_specs=[_HBM_SPEC] * n + [_SEM_SPEC, _SEM_SPEC, _anyspec()], out_specs=tuple([_HBM_SPEC] * n),
        input_output_aliases={i: i for i in range(n)},
        compiler_params=pltpu.CompilerParams(has_side_effects=_DATAFLOW_EFFECT),
    )(*arrays, send_sems, recv_sems, after)
    return list(out)


def _gather_plan(m):
    def plan(refs, x, y, c):
        me = 4 * x + 2 * y + c
        return [(refs[i], refs[m + i].at[me], (*chip, c)) for i in range(m) for chip in _other_chips(x, y)]
    return plan


def _pass_on_plan(m):
    def plan(refs, x, y, c):
        out = []
        for i in range(m):
            out.append((refs[i], refs[m + i].at[4 * x + 2 * y + c], (x, y, 1 - c)))
            for cx, cy in _other_chips(x, y):
                block = refs[m + i].at[4 * cx + 2 * cy + c]
                out.append((block, block, (x, y, 1 - c)))
        return out
    return plan


def _peers_plan():
    def plan(refs, x, y, c):
        rel = [(dx, dy, dc) for dx in (0, 1) for dy in (0, 1) for dc in (0, 1)][1:]
        return [(refs[0], refs[1].at[4 * x + 2 * y + c], (x ^ dx, y ^ dy, c ^ dc)) for dx, dy, dc in rel]
    return plan


def _allgather_small(v):
    rows = v.shape[0]

    def body(v_ref, out_ref, send_sems, recv_sems):
        x, y, c = _mesh_pos()
        me = 4 * x + 2 * y + c
        out_ref[me] = v_ref[...]
        rel = [(dx, dy, dc) for dx in (0, 1) for dy in (0, 1) for dc in (0, 1)][1:]
        copies = []
        for k, (dx, dy, dc) in enumerate(rel):
            to = (x ^ dx, y ^ dy, c ^ dc)
            copies.append(pltpu.make_async_remote_copy(
                src_ref=v_ref, dst_ref=out_ref.at[me], send_sem=send_sems.at[k], recv_sem=recv_sems.at[k],
                device_id=to, device_id_type=MESH))
        for cp in copies:
            cp.start()
        for k, (dx, dy, dc) in enumerate(rel):
            src_blk = 4 * (x ^ dx) + 2 * (y ^ dy) + (c ^ dc)
            pltpu.make_async_remote_copy(
                src_ref=v_ref, dst_ref=out_ref.at[src_blk], send_sem=send_sems.at[k], recv_sem=recv_sems.at[k],
                device_id=(x, y, c), device_id_type=MESH).wait_recv()
        for cp in copies:
            cp.wait_send()

    return pl.pallas_call(
        body, name="allgather_small",
        out_shape=jax.ShapeDtypeStruct((N_DEV, rows, 128), v.dtype),
        in_specs=[_vspec()], out_specs=_vspec(),
        scratch_shapes=[pltpu.SemaphoreType.DMA((7,)), pltpu.SemaphoreType.DMA((7,))],
        compiler_params=_params(24),
    )(v)


def _sibling_plan(n):
    def plan(refs, x, y, c):
        return [(refs[a].at[2 * q + (1 - c)], refs[n + a].at[q], (x, y, 1 - c)) for a in range(n) for q in range(4)]
    return plan


def _chip_plan(n):
    def plan(refs, x, y, c):
        return [(refs[a].at[2 * cx + cy], refs[n + a].at[j], (cx, cy, c))
                for a in range(n) for j, (cx, cy) in enumerate(_other_chips(x, y))]
    return plan


def _row_tile(rows, cols, budget_bytes=2 * 1024 * 1024):
    best = 8
    for t in range(8, rows + 1, 8):
        if rows % t == 0 and t * cols * 4 <= budget_bytes:
            best = t
    return best


def _chip_partial(g, recv, where):
    _, rows, cols = g.shape
    tr = _row_tile(rows, cols)

    def body(where_ref, g_ref, r_ref, own_ref, o16_ref):
        s = g_ref[...] + r_ref[...]
        o16_ref[...] = s.astype(BF16)

        @pl.when(pl.program_id(1) == where_ref[1])
        def _():
            own_ref[...] = s

    blk = (None, tr, cols)
    return pl.pallas_call(
        body, name="rs_chip_partial",
        grid_spec=pltpu.PrefetchScalarGridSpec(
            num_scalar_prefetch=1, grid=(rows // tr, 4),
            in_specs=[pl.BlockSpec(blk, lambda i, q, w: (2 * q + w[0], i, 0)),
                      pl.BlockSpec(blk, lambda i, q, w: (q, i, 0))],
            out_specs=[pl.BlockSpec((tr, cols), lambda i, q, w: (i, 0)), pl.BlockSpec(blk, lambda i, q, w: (q, i, 0))]),
        out_shape=[_hbm((rows, cols), F32), _hbm((4, rows, cols), BF16)],
        compiler_params=_params(32),
    )(where, g, recv)


def _adam_math(w, g, m, v):
    m = ADAM_B1 * m + (1.0 - ADAM_B1) * g
    v = ADAM_B2 * v + (1.0 - ADAM_B2) * (g * g)
    m_hat = m / (1.0 - ADAM_B1 ** ADAM_STEP)
    v_hat = v / (1.0 - ADAM_B2 ** ADAM_STEP)
    delta = -ADAM_LR * (m_hat / (jnp.sqrt(v_hat) + ADAM_EPS) + ADAM_WD * w)
    return delta, m, v


def _adamw_shard(own32, recv16, w, m, v, layer, earlier):
    depth, rows, cols = w.shape
    tr = _row_tile(rows, cols, 1024 * 1024)
    n_prev = 0 if earlier is None else 4

    def body(p_ref, r_ref, w_ref, m_ref, v_ref, *rest):
        g_out, d_out, m_out, v_out = rest[n_prev:]
        g = p_ref[...] + r_ref[0].astype(F32) + r_ref[1].astype(F32) + r_ref[2].astype(F32)
        d, mn, vn = _adam_math(w_ref[...], g, m_ref[...], v_ref[...])
        g_out[...] = g
        d_out[...] = d
        m_out[...] = mn
        v_out[...] = vn

    mine = pl.BlockSpec((None, tr, cols), lambda i: (layer, i, 0))
    return pl.pallas_call(
        body, name="adamw_shard", grid=(rows // tr,),
        in_specs=[pl.BlockSpec((tr, cols), lambda i: (i, 0)), pl.BlockSpec((3, tr, cols), lambda i: (0, i, 0)),
                  mine, mine, mine] + [_anyspec()] * n_prev,
        out_specs=[mine] * 4,
        out_shape=[_hbm((depth, rows, cols), F32)] * 4,
        input_output_aliases={5 + k: k for k in range(n_prev)},
        compiler_params=_params(32),
    )(own32, recv16, *[pltpu.with_memory_space_constraint(t, pltpu.HBM) for t in (w, m, v)],
      *([] if earlier is None else earlier))


def _adamw_small(gathered, w, m, v):
    rows = w.shape[0]

    def body(a_ref, w_ref, m_ref, v_ref, g_out, d_out, m_out, v_out):
        g = a_ref[0]
        for d in range(1, N_DEV):
            g = g + a_ref[d]
        dl, mn, vn = _adam_math(w_ref[...], g, m_ref[...], v_ref[...])
        g_out[...] = g
        d_out[...] = dl
        m_out[...] = mn
        v_out[...] = vn

    return pl.pallas_call(
        body, name="adamw_small",
        in_specs=[_vspec()] * 4, out_specs=[_vspec()] * 4,
        out_shape=[_hbm((rows, 128), F32)] * 4,
        compiler_params=_params(32),
    )(gathered, w, m, v)


def _load_weights_once(pairs, sems):
    @pl.when(pl.program_id(0) == 0)
    def _():
        cps = [pltpu.make_async_copy(src, dst, sems.at[i]) for i, (src, dst) in enumerate(pairs)]
        for cp in cps:
            cp.start()
        for cp in cps:
            cp.wait()


def _ffn_fwd(x, wup, wd, ln_g, ln_b, after, target=None, tm=512):
    t_tok = x.shape[0]
    last = target is not None

    def body(x_ref, g_ref, b_ref, wup_hbm, wd_hbm, _after, *rest):
        if last:
            t_ref, dxn_ref, loss_ref, z_ref, gu_ref, wup_v, wd_v, sems = rest
        else:
            xn_ref, xnb_ref, z_ref, gu_ref, xb_ref, wup_v, wd_v, sems = rest
        _load_weights_once([(wup_hbm, wup_v), (wd_hbm, wd_v)], sems)
        xb = x_ref[...].astype(BF16)
        if not last:
            xb_ref[...] = xb
        y = None
        for j in range(N_FFN_CHUNK):
            g = _dot_nt(xb, wup_v[j])
            u = _dot_nt(xb, wup_v[N_FFN_CHUNK + j])
            gu_ref[0, j] = g.astype(BF16)
            gu_ref[1, j] = u.astype(BF16)
            a = (g * jax.nn.sigmoid(g) * u).astype(BF16)
            part = _dot(a, wd_v[j])
            y = part if y is None else y + part
        z = ALPHA * x_ref[...] + 0.5 * y
        xhat, _ = _ln_stats(z)
        xn = xhat * g_ref[...] + b_ref[...]
        z_ref[...] = z
        if last:
            err = xn - t_ref[...]
            dxn_ref[...] = err * (1.0 / D_MODEL)
            part = jnp.sum(jnp.sum(err * err, axis=1, keepdims=True), axis=0, keepdims=True) * (0.5 / D_MODEL)

            @pl.when(pl.program_id(0) == 0)
            def _():
                loss_ref[...] = jnp.zeros_like(loss_ref)

            loss_ref[...] += part
        else:
            xn_ref[...] = xn
            xnb_ref[...] = xn.astype(BF16)

    tok = pl.BlockSpec((tm, D_MODEL), lambda i: (i, 0))
    vec = pl.BlockSpec((1, D_MODEL), lambda i: (0, 0))
    gu_spec = pl.BlockSpec((2, N_FFN_CHUNK, tm, FFN_BLK), lambda i: (0, 0, i, 0))
    gu_shape = _hbm((2, N_FFN_CHUNK, t_tok, FFN_BLK), BF16)
    f32_tok, bf16_tok = _hbm((t_tok, D_MODEL), F32), _hbm((t_tok, D_MODEL), BF16)
    if last:
        extra_in, extra_spec = [target], [tok]
        out_specs = [tok, pl.BlockSpec((1, 128), lambda i: (0, 0)), tok, gu_spec]
        out_shape = [f32_tok, _hbm((1, 128), F32), f32_tok, gu_shape]
    else:
        extra_in, extra_spec = [], []
        out_specs = [tok, tok, tok, gu_spec, tok]
        out_shape = [f32_tok, bf16_tok, f32_tok, gu_shape, bf16_tok]
    return pl.pallas_call(
        body, name="ffn_fwd_loss" if last else "ffn_fwd", grid=(t_tok // tm,),
        in_specs=[tok, vec, vec, _anyspec(), _anyspec(), _anyspec()] + extra_spec,
        out_specs=out_specs, out_shape=out_shape,
        scratch_shapes=[pltpu.VMEM((N_DEV, FFN_BLK, D_MODEL), BF16), pltpu.VMEM((N_FFN_CHUNK, FFN_BLK, D_MODEL), BF16),
                        pltpu.SemaphoreType.DMA((2,))],
        compiler_params=_params(62, ("arbitrary",)),
    )(x, ln_g, ln_b, wup, wd, after, *extra_in)


def _ffn_bwd(dxn, z, gu, wup, wd, ln_g, after, tm=256):
    t_tok = dxn.shape[0]

    def body(dxn_ref, z_ref, gu_ref, g_ref, wup_hbm, wd_hbm, _after,
             dx_ref, dy_ref, a_ref, dgu_ref, dg_ref, db_ref, wup_v, wd_v, sems):
        i = pl.program_id(0)
        _load_weights_once([(wup_hbm, wup_v), (wd_hbm, wd_v)], sems)
        dxn_t = dxn_ref[...]
        xhat, rstd = _ln_stats(z_ref[...])
        pg = jnp.sum(dxn_t * xhat, axis=0, keepdims=True)
        pb = jnp.sum(dxn_t, axis=0, keepdims=True)

        @pl.when(i == 0)
        def _():
            dg_ref[...] = pg
            db_ref[...] = pb

        @pl.when(i > 0)
        def _():
            dg_ref[...] += pg
            db_ref[...] += pb

        dz = _ln_bwd(dxn_t, xhat, rstd, g_ref[...])
        dy = (0.5 * dz).astype(BF16)
        dy_ref[...] = dy
        dx = ALPHA * dz
        for j in range(N_FFN_CHUNK):
            da = _dot_nt(dy, wd_v[j])
            g = gu_ref[0, j].astype(F32)
            u = gu_ref[1, j].astype(F32)
            sig = jax.nn.sigmoid(g)
            silu = g * sig
            a_ref[j] = (silu * u).astype(BF16)
            dg = (da * u * (sig * (1.0 + g * (1.0 - sig)))).astype(BF16)
            du = (da * silu).astype(BF16)
            dgu_ref[0, j] = dg
            dgu_ref[1, j] = du
            dx = dx + _dot(dg, wup_v[j]) + _dot(du, wup_v[N_FFN_CHUNK + j])
        dx_ref[...] = dx

    tok = pl.BlockSpec((tm, D_MODEL), lambda i: (i, 0))
    vec = pl.BlockSpec((1, D_MODEL), lambda i: (0, 0))
    gu_spec = pl.BlockSpec((2, N_FFN_CHUNK, tm, FFN_BLK), lambda i: (0, 0, i, 0))
    return pl.pallas_call(
        body, name="ffn_bwd", grid=(t_tok // tm,),
        in_specs=[tok, tok, gu_spec, vec, _anyspec(), _anyspec(), _anyspec()],
        out_specs=[tok, tok, pl.BlockSpec((N_FFN_CHUNK, tm, FFN_BLK), lambda i: (0, i, 0)), gu_spec, vec, vec],
        out_shape=[_hbm((t_tok, D_MODEL), F32), _hbm((t_tok, D_MODEL), BF16),
                   _hbm((N_FFN_CHUNK, t_tok, FFN_BLK), BF16),
                   _hbm((2, N_FFN_CHUNK, t_tok, FFN_BLK), BF16),
                   _hbm((1, D_MODEL), F32), _hbm((1, D_MODEL), F32)],
        scratch_shapes=[pltpu.VMEM((N_DEV, FFN_BLK, D_MODEL), BF16), pltpu.VMEM((N_FFN_CHUNK, FFN_BLK, D_MODEL), BF16),
                        pltpu.SemaphoreType.DMA((2,))],
        compiler_params=_params(60, ("arbitrary",)),
    )(dxn, z, gu, ln_g, wup, wd, after)


def _matmul_tn(a, b, after, tk=4096):
    ga, t_tok, m = a.shape
    gb, _, n = b.shape
    groups = max(ga, gb)
    tk = min(tk, t_tok)

    def body(a_ref, b_ref, _after, o_ref):
        p = _dot_tn(a_ref[...].astype(BF16), b_ref[...].astype(BF16))

        @pl.when(pl.program_id(1) == 0)
        def _():
            o_ref[...] = p

        @pl.when(pl.program_id(1) > 0)
        def _():
            o_ref[...] += p

    return pl.pallas_call(
        body, name=f"matmul_tn_{m}x{n}", grid=(groups, t_tok // tk),
        in_specs=[pl.BlockSpec((None, tk, m), (lambda g, t: (g, t, 0)) if ga > 1 else (lambda g, t: (0, t, 0))),
                  pl.BlockSpec((None, tk, n), (lambda g, t: (g, t, 0)) if gb > 1 else (lambda g, t: (0, t, 0))),
                  _anyspec()],
        out_specs=pl.BlockSpec((None, m, n), lambda g, t: (g, 0, 0)),
        out_shape=_hbm((groups, m, n), F32),
        compiler_params=_params(56, ("arbitrary", "arbitrary")),
    )(a, b, after)


def _in_proj(x, w_in, tm=512):
    t_tok = x.shape[0]

    def body(x_ref, w_ref, conv_ref, qkv_ref, sgu_ref, f_ref):
        xb = x_ref[...].astype(BF16)
        conv_ref[...] = _dot(xb, w_ref[:, COL_CONV:COL_QKV])
        qkv_ref[...] = _dot(xb, w_ref[:, COL_QKV:COL_SGU]).astype(BF16)
        sgu_ref[...] = _dot(xb, w_ref[:, COL_SGU:COL_F])
        f_ref[...] = _dot(xb, w_ref[:, COL_F:D_IN_PAD])

    def tok(n):
        return pl.BlockSpec((tm, n), lambda i: (i, 0))

    return pl.pallas_call(
        body, name="mix_in_proj", grid=(t_tok // tm,),
        in_specs=[tok(D_MODEL), pl.BlockSpec((D_MODEL, D_IN_PAD), lambda i: (0, 0))],
        out_specs=[tok(768), tok(1536), tok(512), tok(128)],
        out_shape=[_hbm((t_tok, 768), F32), _hbm((t_tok, 1536), BF16),
                   _hbm((t_tok, 512), F32), _hbm((t_tok, 128), F32)],
        compiler_params=_params(48, ("arbitrary",)),
    )(x, w_in)


def _shift_down(a, k):
    row = lax.broadcasted_iota(jnp.int32, a.shape, 0)
    return jnp.where(row >= k, pltpu.roll(a, k, 0), 0.0)


def _shift_up(a, k):
    rows = a.shape[0]
    row = lax.broadcasted_iota(jnp.int32, a.shape, 0)
    return jnp.where(row < rows - k, pltpu.roll(a, rows - k, 0), 0.0)


def _tril(n):
    return lax.broadcasted_iota(jnp.int32, (n, n), 0) >= lax.broadcasted_iota(jnp.int32, (n, n), 1)


def _sgu_group_of_lane():
    return lax.broadcasted_iota(jnp.int32, (1, D_SGU), 1) // (D_SGU // N_SGU_GROUPS)


def _log_sigmoid(x):
    return jnp.minimum(x, 0.0) - jnp.log1p(jnp.exp(-jnp.abs(x)))


def _mix_mid_fwd(conv, sgu, f, conv_w, b_f, sgu_g, sgu_b, w_s, b_mat, n_seq):
    t_tok = conv.shape[0]
    seq = t_tok // n_seq
    n_chunk = seq // SGU_CHUNK
    per_blk = ATT_BLK // SGU_CHUNK

    def body(conv_ref, sgu_ref, f_ref, cw_ref, bf_ref, lg_ref, lb_ref, ws_ref, bm_ref, cat_ref, cum_ref):
        z = conv_ref[:, 256:512] * conv_ref[:, 512:768]
        y = cw_ref[0:1, :] * _shift_down(z, 2) + cw_ref[1:2, :] * _shift_down(z, 1) + cw_ref[2:3, :] * z
        cat_ref[:, 0:D_CONV] = (conv_ref[:, 0:256] * y).astype(BF16)
        cat_ref[:, D_CONV:D_CONV + D_FOX] = jnp.zeros((seq, D_FOX), BF16)

        tril = _tril(SGU_CHUNK)
        grp = _sgu_group_of_lane()
        wc = [jnp.where(tril, ws_ref[g], 0.0).astype(BF16) for g in range(N_SGU_GROUPS)]
        tri_f = tril.astype(F32)
        carry = jnp.zeros((1, 128), F32)
        for n in range(n_chunk):
            rows = pl.ds(n * SGU_CHUNK, SGU_CHUNK)
            u = _gelu(sgu_ref[rows, 0:256])
            vhat, _ = _ln_stats(_gelu(sgu_ref[rows, 256:512]))
            vn = (vhat * lg_ref[...] + lb_ref[...]).astype(BF16)
            mixed = bm_ref[...]
            for g in range(N_SGU_GROUPS):
                mixed = mixed + jnp.where(grp == g, _dot(wc[g], vn), 0.0)
            cat_ref[rows, D_CONV + D_FOX:D_MODEL] = (u * mixed).astype(BF16)

            log_f = _log_sigmoid(f_ref[rows, :] + bf_ref[...])
            cs = _dot(tri_f, log_f, HIGHEST) + carry
            carry = cs[SGU_CHUNK - 1:SGU_CHUNK, :]
            cs_t = cs.T
            lanes = pl.ds((n % per_blk) * SGU_CHUNK, SGU_CHUNK)
            for h in range(N_HEADS):
                cum_ref[h, n // per_blk, :, lanes] = cs_t[h:h + 1, :]

    def seq_blk(n):
        return pl.BlockSpec((seq, n), lambda b: (b, 0))

    def full(shape):
        return pl.BlockSpec(shape, lambda b: (0,) * len(shape))

    return pl.pallas_call(
        body, name="mix_mid_fwd", grid=(n_seq,),
        in_specs=[seq_blk(768), seq_blk(512), seq_blk(128), full((8, 256)), full((1, 128)), full((1, 256)),
                  full((1, 256)), full((4, 128, 128)), full((128, 256))],
        out_specs=[seq_blk(D_MODEL), pl.BlockSpec((N_HEADS, seq // ATT_BLK, 1, ATT_BLK), lambda b: (b, 0, 0, 0))],
        out_shape=[_hbm((t_tok, D_MODEL), BF16), _hbm((n_seq * N_HEADS, seq // ATT_BLK, 1, ATT_BLK), F32)],
        compiler_params=_params(48, ("arbitrary",)),
    )(conv, sgu, f, conv_w, b_f, sgu_g, sgu_b, w_s, b_mat)


def _head_masks():
    lane = lax.broadcasted_iota(jnp.int32, (1, 128), 1)
    return lane < 64, lane


def _fox_fwd(qkv, cum_t, cat, n_seq):
    t_tok = qkv.shape[0]
    seq = t_tok // n_seq
    nq = seq // ATT_BLK
    blk = ATT_BLK

    def body(q_ref, k_ref, v_ref, c_ref, _cat, o_ref, lse_ref):
        qi = pl.program_id(2)
        first, _ = _head_masks()
        causal = _tril(blk)
        one = jnp.ones((1, 128), BF16)
        qh = []
        for hp in range(ATT_PAIRS):
            qs = q_ref[:, 128 * hp:128 * hp + 128] * ATT_SCALE
            zero = jnp.zeros_like(qs)
            qh += [jnp.where(first, qs, zero), jnp.where(first, zero, qs)]

        def step(kb, carry, masked):
            ms, accs = carry
            rows = pl.ds(pl.multiple_of(kb * blk, blk), blk)
            new_m, new_acc = [], []
            for hp in range(ATT_PAIRS):
                k = k_ref[rows, 128 * hp:128 * hp + 128]
                v = v_ref[rows, 128 * hp:128 * hp + 128]
                for h in range(2):
                    i = 2 * hp + h
                    s = _dot_nt(qh[i], k) - c_ref[i, kb]
                    if masked:
                        s = jnp.where(causal, s, NEG)
                    m_new = jnp.maximum(ms[i], jnp.max(s, axis=1, keepdims=True))
                    p = jnp.exp(s - m_new)
                    vh = jnp.where(first, v, one) if h == 0 else jnp.where(first, one, v)
                    new_acc.append(accs[i] * jnp.exp(ms[i] - m_new) + _dot(p.astype(BF16), vh))
                    new_m.append(m_new)
            return tuple(new_m), tuple(new_acc)

        n_heads = 2 * ATT_PAIRS
        col = jnp.full((blk, 1), NEG, F32)
        zacc = jnp.zeros((blk, 128), F32)
        carry = lax.fori_loop(0, qi, lambda kb, cr: step(kb, cr, False), ((col,) * n_heads, (zacc,) * n_heads))
        ms, accs = step(qi, carry, True)
        for hp in range(ATT_PAIRS):
            acc0, acc1 = accs[2 * hp], accs[2 * hp + 1]
            l0 = pltpu.roll(acc0, 64, 1)
            l1 = pltpu.roll(acc1, 64, 1)
            o_ref[:, 128 * hp:128 * hp + 128] = jnp.where(first, acc0 / l0, acc1 / l1).astype(BF16)
            lse_ref[:, 128 * hp:128 * hp + 128] = jnp.where(first, ms[2 * hp] + jnp.log(l0), ms[2 * hp + 1] + jnp.log(l1))

    wide = 128 * ATT_PAIRS
    n_grp = D_FOX // wide
    first_col = D_CONV // wide
    return pl.pallas_call(
        body, name="fox_fwd", grid=(n_seq, n_grp, nq),
        in_specs=[pl.BlockSpec((blk, wide), lambda b, g, qi: (b * nq + qi, g)),
                  pl.BlockSpec((seq, wide), lambda b, g, qi: (b, n_grp + g)),
                  pl.BlockSpec((seq, wide), lambda b, g, qi: (b, 2 * n_grp + g)),
                  pl.BlockSpec((2 * ATT_PAIRS, nq, 1, blk), lambda b, g, qi: (b * n_grp + g, 0, 0, 0)), _anyspec()],
        out_specs=[pl.BlockSpec((blk, wide), lambda b, g, qi: (b * nq + qi, first_col + g)),
                   pl.BlockSpec((blk, wide), lambda b, g, qi: (b * nq + qi, g))],
        out_shape=[_hbm(cat.shape, BF16), _hbm((t_tok, D_FOX), F32)],
        input_output_aliases={4: 0},
        compiler_params=_params(48, ("arbitrary", "arbitrary", "arbitrary")),
    )(qkv, qkv, qkv, cum_t, cat)


def _fox_bwd(qkv, cum_t, cat, lse, d_o, n_seq):
    t_tok = qkv.shape[0]
    seq = t_tok // n_seq
    nk = seq // ATT_BLK
    blk = ATT_BLK

    def body(q_ref, k_ref, v_ref, c_ref, o_ref, lse_ref, do_ref, dq_ref, dk_ref, dv_ref, drow_ref, dcol_ref):
        kb = pl.program_id(2)
        first, lane = _head_masks()
        second = jnp.logical_not(first)
        one = jnp.ones((1, 128), BF16)
        causal = _tril(blk)

        @pl.when(kb == 0)
        def _():
            dq_ref[...] = jnp.zeros_like(dq_ref)
            drow_ref[...] = jnp.zeros_like(drow_ref)

        def step(qi, carry, masked):
            rows = pl.ds(pl.multiple_of(qi * blk, blk), blk)
            dks, dvs = carry
            new_dk, new_dv = [], []
            for hp in range(ATT_PAIRS):
                cols = slice(128 * hp, 128 * hp + 128)
                k = k_ref[:, cols]
                v = v_ref[:, cols]
                ks = k * ATT_SCALE
                zero = jnp.zeros_like(k)
                qs = q_ref[rows, cols] * ATT_SCALE
                d_o = do_ref[rows, cols]
                dd = d_o.astype(F32) * o_ref[rows, cols].astype(F32)
                lse_t = lse_ref[rows, cols]
                dq = []
                for h, mine in enumerate((first, second)):
                    i = 2 * hp + h
                    qh = jnp.where(mine, qs, zero)
                    doh = jnp.where(mine, d_o, zero)
                    delta = jnp.sum(jnp.where(mine, dd, 0.0), axis=1, keepdims=True)
                    lse_h = jnp.sum(jnp.where(lane == 64 * h, lse_t, 0.0), axis=1, keepdims=True)
                    s = _dot_nt(qh, k) - c_ref[i]
                    if masked:
                        s = jnp.where(causal, s, NEG)
                    p = jnp.exp(s - lse_h)
                    ds = (p * (_dot_nt(doh, v) - delta)).astype(BF16)
                    new_dk.append(dks[i] + _dot_tn(ds, jnp.where(mine, qs, one)))
                    new_dv.append(dvs[i] + _dot_tn(p.astype(BF16), doh))
                    dq.append(_dot(ds, jnp.where(mine, ks, one)))
                dq_ref[rows, cols] += jnp.where(first, dq[0], dq[1])
                drow_ref[rows, cols] += jnp.where(first, dq[1], dq[0])
            return tuple(new_dk), tuple(new_dv)

        zt = (jnp.zeros((blk, 128), F32),) * (2 * ATT_PAIRS)
        carry = step(kb, (zt, zt), True)
        dks, dvs = lax.fori_loop(kb + 1, nk, lambda qi, cr: step(qi, cr, False), carry)
        for hp in range(ATT_PAIRS):
            cols = slice(128 * hp, 128 * hp + 128)
            dk_ref[:, cols] = jnp.where(first, dks[2 * hp], dks[2 * hp + 1]).astype(BF16)
            dcol_ref[:, cols] = jnp.where(first, dks[2 * hp + 1], dks[2 * hp])
            dv_ref[:, cols] = (dvs[2 * hp] + dvs[2 * hp + 1]).astype(BF16)

    wide = 128 * ATT_PAIRS
    n_grp = D_FOX // wide

    def seq_spec(col0):
        return pl.BlockSpec((seq, wide), lambda b, g, kb: (b, col0 + g))

    def key_spec(col0):
        return pl.BlockSpec((blk, wide), lambda b, g, kb: (b * nk + kb, col0 + g))

    return pl.pallas_call(
        body, name="fox_bwd", grid=(n_seq, n_grp, nk),
        in_specs=[seq_spec(0), key_spec(n_grp), key_spec(2 * n_grp),
                  pl.BlockSpec((2 * ATT_PAIRS, None, 1, blk), lambda b, g, kb: (b * n_grp + g, kb, 0, 0)),
                  seq_spec(D_CONV // wide), seq_spec(0), seq_spec(0)],
        out_specs=[seq_spec(0), key_spec(0), key_spec(0), seq_spec(0), key_spec(0)],
        out_shape=[_hbm((t_tok, D_FOX), F32), _hbm((t_tok, D_FOX), BF16),
                   _hbm((t_tok, D_FOX), BF16), _hbm((t_tok, D_FOX), F32),
                   _hbm((t_tok, D_FOX), F32)],
        compiler_params=_params(56, ("arbitrary", "arbitrary", "arbitrary")),
    )(qkv, qkv, qkv, cum_t, cat, lse, d_o)


def _mix_out_fwd(cat, x, w_out, ln_g, ln_b, tm=512):
    t_tok = x.shape[0]

    def body(cat_ref, x_ref, w_ref, g_ref, b_ref, xn_ref, xnb_ref, z_ref):
        z = ALPHA * x_ref[...] + _dot(cat_ref[...], w_ref[...])
        xhat, _ = _ln_stats(z)
        xn = xhat * g_ref[...] + b_ref[...]
        z_ref[...] = z
        xn_ref[...] = xn
        xnb_ref[...] = xn.astype(BF16)

    def tok(n):
        return pl.BlockSpec((tm, n), lambda i: (i, 0))

    vec = pl.BlockSpec((1, D_MODEL), lambda i: (0, 0))
    return pl.pallas_call(
        body, name="mix_out_fwd", grid=(t_tok // tm,),
        in_specs=[tok(D_MODEL), tok(D_MODEL), pl.BlockSpec((D_MODEL, D_MODEL), lambda i: (0, 0)), vec, vec],
        out_specs=[tok(D_MODEL)] * 3,
        out_shape=[_hbm((t_tok, D_MODEL), F32), _hbm((t_tok, D_MODEL), BF16),
                   _hbm((t_tok, D_MODEL), F32)],
        compiler_params=_params(40, ("arbitrary",)),
    )(cat, x, w_out, ln_g, ln_b)


def _mix_out_bwd(dxn, z, w_out, ln_g, tm=512):
    t_tok = dxn.shape[0]

    def body(dxn_ref, z_ref, w_ref, g_ref, dz_ref, dzb_ref, dya_ref, dyb_ref, dyc_ref, dg_ref, db_ref):
        i = pl.program_id(0)
        dxn_t = dxn_ref[...]
        xhat, rstd = _ln_stats(z_ref[...])
        pg = jnp.sum(dxn_t * xhat, axis=0, keepdims=True)
        pb = jnp.sum(dxn_t, axis=0, keepdims=True)

        @pl.when(i == 0)
        def _():
            dg_ref[...] = pg
            db_ref[...] = pb

        @pl.when(i > 0)
        def _():
            dg_ref[...] += pg
            db_ref[...] += pb

        dz = _ln_bwd(dxn_t, xhat, rstd, g_ref[...])
        dzb = dz.astype(BF16)
        dz_ref[...] = dz
        dzb_ref[...] = dzb
        dya_ref[...] = _dot_nt(dzb, w_ref[0:256, :])
        dyb_ref[...] = _dot_nt(dzb, w_ref[256:768, :]).astype(BF16)
        dyc_ref[...] = _dot_nt(dzb, w_ref[768:1024, :])

    def tok(n):
        return pl.BlockSpec((tm, n), lambda i: (i, 0))

    vec = pl.BlockSpec((1, D_MODEL), lambda i: (0, 0))
    return pl.pallas_call(
        body, name="mix_out_bwd", grid=(t_tok // tm,),
        in_specs=[tok(D_MODEL), tok(D_MODEL), pl.BlockSpec((D_MODEL, D_MODEL), lambda i: (0, 0)), vec],
        out_specs=[tok(D_MODEL), tok(D_MODEL), tok(256), tok(512), tok(256), vec, vec],
        out_shape=[_hbm((t_tok, D_MODEL), F32), _hbm((t_tok, D_MODEL), BF16),
                   _hbm((t_tok, 256), F32), _hbm((t_tok, 512), BF16),
                   _hbm((t_tok, 256), F32),
                   _hbm((1, D_MODEL), F32), _hbm((1, D_MODEL), F32)],
        compiler_params=_params(40, ("arbitrary",)),
    )(dxn, z, w_out, ln_g)


def _conv_bwd(conv, dya, conv_w, n_seq):
    t_tok = conv.shape[0]
    seq = t_tok // n_seq

    def body(conv_ref, dya_ref, cw_ref, dconv_ref, dcw_ref):
        @pl.when(pl.program_id(0) == 0)
        def _():
            dcw_ref[...] = jnp.zeros_like(dcw_ref)

        z = conv_ref[:, 256:512] * conv_ref[:, 512:768]
        z1 = _shift_down(z, 1)
        z2 = _shift_down(z, 2)
        y = cw_ref[0:1, :] * z2 + cw_ref[1:2, :] * z1 + cw_ref[2:3, :] * z
        dya_t = dya_ref[...]
        dconv_ref[:, 0:256] = (dya_t * y).astype(BF16)
        dy = dya_t * conv_ref[:, 0:256]
        dcw_ref[0:1, :] += jnp.sum(dy * z2, axis=0, keepdims=True)
        dcw_ref[1:2, :] += jnp.sum(dy * z1, axis=0, keepdims=True)
        dcw_ref[2:3, :] += jnp.sum(dy * z, axis=0, keepdims=True)
        dz = cw_ref[2:3, :] * dy + cw_ref[1:2, :] * _shift_up(dy, 1) + cw_ref[0:1, :] * _shift_up(dy, 2)
        dconv_ref[:, 256:512] = (dz * conv_ref[:, 512:768]).astype(BF16)
        dconv_ref[:, 512:768] = (dz * conv_ref[:, 256:512]).astype(BF16)

    def seq_blk(n):
        return pl.BlockSpec((seq, n), lambda b: (b, 0))

    par = pl.BlockSpec((8, 256), lambda b: (0, 0))
    return pl.pallas_call(
        body, name="conv_bwd", grid=(n_seq,),
        in_specs=[seq_blk(768), seq_blk(256), par], out_specs=[seq_blk(768), par],
        out_shape=[_hbm((t_tok, 768), BF16), _hbm((8, 256), F32)],
        compiler_params=_params(56, ("arbitrary",)),
    )(conv, dya, conv_w)


def _sgu_gate_bwd(sgu, f, dyc, drow, dcol, b_f, sgu_g, sgu_b, w_s, b_mat, n_seq):
    t_tok = sgu.shape[0]
    seq = t_tok // n_seq
    n_chunk = seq // SGU_CHUNK

    def body(sgu_ref, f_ref, dyc_ref, drow_ref, dcol_ref, bf_ref, lg_ref, lb_ref, ws_ref, bm_ref,
             dsgu_ref, df_ref, dbf_ref, dlg_ref, dlb_ref, dws_ref, dbs_ref, dbm_acc):
        b = pl.program_id(0)

        @pl.when(b == 0)
        def _():
            for r in (dbf_ref, dlg_ref, dlb_ref, dws_ref, dbm_acc):
                r[...] = jnp.zeros_like(r)

        tril = _tril(SGU_CHUNK)
        grp = _sgu_group_of_lane()
        wc = [jnp.where(tril, ws_ref[g], 0.0).astype(BF16) for g in range(N_SGU_GROUPS)]
        for n in range(n_chunk):
            rows = pl.ds(n * SGU_CHUNK, SGU_CHUNK)
            su = sgu_ref[rows, 0:256]
            sv = sgu_ref[rows, 256:512]
            u = _gelu(su)
            vhat, rstd = _ln_stats(_gelu(sv))
            vn = (vhat * lg_ref[...] + lb_ref[...]).astype(BF16)
            mixed = bm_ref[...]
            for g in range(N_SGU_GROUPS):
                mixed = mixed + jnp.where(grp == g, _dot(wc[g], vn), 0.0)
            dyc_t = dyc_ref[rows, :]
            dsgu_ref[rows, 0:256] = (dyc_t * mixed * _gelu_grad(su)).astype(BF16)
            dmixed = dyc_t * u
            dbm_acc[...] += dmixed
            dvn = jnp.zeros((SGU_CHUNK, D_SGU), F32)
            for g in range(N_SGU_GROUPS):
                dm_g = jnp.where(grp == g, dmixed, 0.0).astype(BF16)
                dws_ref[g] += _dot_nt(dm_g, vn)
                dvn = dvn + _dot_tn(wc[g], dm_g)
            dlg_ref[...] += jnp.sum(dvn * vhat, axis=0, keepdims=True)
            dlb_ref[...] += jnp.sum(dvn, axis=0, keepdims=True)
            dsgu_ref[rows, 256:512] = (_ln_bwd(dvn, vhat, rstd, lg_ref[...]) * _gelu_grad(sv)).astype(BF16)

        later = (lax.broadcasted_iota(jnp.int32, (128, 128), 0) <= lax.broadcasted_iota(jnp.int32, (128, 128), 1)).astype(F32)
        head = lax.broadcasted_iota(jnp.int32, (D_FOX, 128), 1)
        pick = (lax.broadcasted_iota(jnp.int32, (D_FOX, 128), 0) == 128 * (head // 2) + 64 * (1 - head % 2)).astype(F32)
        carry = jnp.zeros((1, 128), F32)
        for n in reversed(range(n_chunk)):
            rows = pl.ds(n * SGU_CHUNK, SGU_CHUNK)
            dcum_n = _dot(drow_ref[rows, :] - dcol_ref[rows, :], pick, HIGHEST)
            dlf = _dot(later, dcum_n, HIGHEST) + carry
            carry = carry + jnp.sum(dcum_n, axis=0, keepdims=True)
            df = dlf * jax.nn.sigmoid(-(f_ref[rows, :] + bf_ref[...]))
            df_ref[rows, :] = df.astype(BF16)
            dbf_ref[...] += jnp.sum(df, axis=0, keepdims=True)

        @pl.when(b == n_seq - 1)
        def _():
            for g in range(N_SGU_GROUPS):
                dws_ref[g] = jnp.where(tril, dws_ref[g], 0.0)
            sel = (lax.broadcasted_iota(jnp.int32, (D_SGU, 128), 0) // (D_SGU // N_SGU_GROUPS)
                   == lax.broadcasted_iota(jnp.int32, (D_SGU, 128), 1)).astype(F32)
            dbs_ref[...] = _dot(dbm_acc[...], sel, HIGHEST)

    def seq_blk(n):
        return pl.BlockSpec((seq, n), lambda b: (b, 0))

    def full(shape):
        return pl.BlockSpec(shape, lambda b: (0,) * len(shape))

    param_shapes = [(1, 128), (1, 256), (1, 256), (4, 128, 128), (128, 128)]
    return pl.pallas_call(
        body, name="sgu_gate_bwd", grid=(n_seq,),
        in_specs=[seq_blk(512), seq_blk(128), seq_blk(256), seq_blk(D_FOX), seq_blk(D_FOX),
                  full((1, 128)), full((1, 256)), full((1, 256)), full((4, 128, 128)), full((128, 256))],
        out_specs=[seq_blk(512), seq_blk(128)] + [full(s) for s in param_shapes],
        out_shape=[_hbm((t_tok, 512), BF16), _hbm((t_tok, 128), BF16)]
        + [_hbm(s, F32) for s in param_shapes],
        scratch_shapes=[pltpu.VMEM((128, 256), F32)],
        compiler_params=_params(48, ("arbitrary",)),
    )(sgu, f, dyc, drow, dcol, b_f, sgu_g, sgu_b, w_s, b_mat)


def _mix_in_bwd(dconv, dq, dk, dv, dsgu, df, dz, w_in, tm=512):
    t_tok = dz.shape[0]

    def body(dconv_ref, dq_ref, dk_ref, dv_ref, dsgu_ref, df_ref, dz_ref, w_ref, dx_ref, dp_ref):
        dqb = dq_ref[...].astype(BF16)
        pieces = [(COL_CONV, dconv_ref[...]), (COL_QKV, dqb), (COL_QKV + 512, dk_ref[...]), (COL_QKV + 1024, dv_ref[...]),
                  (COL_SGU, dsgu_ref[...]), (COL_F, df_ref[...])]
        dx = ALPHA * dz_ref[...]
        for col, val in pieces:
            width = val.shape[1]
            dp_ref[:, col:col + width] = val
            dx = dx + _dot_nt(val, w_ref[:, col:col + width])
        dx_ref[...] = dx

    def tok(n):
        return pl.BlockSpec((tm, n), lambda i: (i, 0))

    return pl.pallas_call(
        body, name="mix_in_bwd", grid=(t_tok // tm,),
        in_specs=[tok(768), tok(512), tok(512), tok(512), tok(512), tok(128), tok(D_MODEL),
                  pl.BlockSpec((D_MODEL, D_IN_PAD), lambda i: (0, 0))],
        out_specs=[tok(D_MODEL), tok(D_IN_PAD)],
        out_shape=[_hbm((t_tok, D_MODEL), F32), _hbm((t_tok, D_IN_PAD), BF16)],
        compiler_params=_params(48, ("arbitrary",)),
    )(dconv, dq, dk, dv, dsgu, df, dz, w_in)


def _pad_rows(a, rows):
    return jnp.pad(a, ((0, rows - a.shape[0]), (0, 0)))


F_BLOCK = F_ORIG // D_IN_SHARD
F_AT = F_ORIG - F_BLOCK * D_IN_SHARD
assert (F_ORIG + N_HEADS) // D_IN_SHARD == F_BLOCK


def _w_in_from_blocks(g):
    fb = g[F_BLOCK]
    zeros = jnp.zeros((D_MODEL, D_IN_PAD - COL_F - N_HEADS), g.dtype)
    return jnp.concatenate([g[d] for d in range(F_BLOCK)] + [fb[:, :F_AT], fb[:, F_AT + N_HEADS:]]
                           + [g[d] for d in range(F_BLOCK + 1, N_DEV)] + [fb[:, F_AT:F_AT + N_HEADS], zeros], axis=1)


def _w_in_to_blocks(dw):
    def cols(lo, hi):
        shift = 0 if hi <= F_ORIG else N_HEADS
        return dw[:, lo - shift:hi - shift]

    blocks = []
    for d in range(N_DEV):
        lo, hi = d * D_IN_SHARD, (d + 1) * D_IN_SHARD
        if d == F_BLOCK:
            blocks.append(jnp.concatenate([cols(lo, F_ORIG), dw[:, COL_F:COL_F + N_HEADS], cols(F_ORIG + N_HEADS, hi)], axis=1))
        else:
            blocks.append(cols(lo, hi))
    return jnp.stack(blocks)


LN1_ROWS = 2 * 8
REST_ROWS = 4 * 8 + 2 * 8 + 512 + 8 + 8 + 8


def _pack_rest(p):
    rows = [p[name].reshape(8, 128) for name in ("ln2_g", "ln2_b", "ln3_g", "ln3_b")]
    rows += [_pad_rows(p[name].reshape(2, 128), 8) for name in ("sgu_ln_g", "sgu_ln_b")]
    rows += [p["sgu_w_s"].reshape(512, 128), _pad_rows(p["sgu_b_s"], 8),
             _pad_rows(jnp.pad(p["fox_b_f"], (0, 128 - N_HEADS)).reshape(1, 128), 8), _pad_rows(p["conv_w"].reshape(6, 128), 8)]
    return jnp.concatenate(rows, axis=0)


def _pack_layer(p):
    return jnp.concatenate([p["ln1_g"].reshape(8, 128), p["ln1_b"].reshape(8, 128), _pack_rest(p)], axis=0)


def _unpack_layer(a):
    r = 0

    def take(n, valid):
        nonlocal r
        piece = a[r:r + valid]
        r += n
        return piece

    d = {}
    for name in ("ln1_g", "ln1_b", "ln2_g", "ln2_b", "ln3_g", "ln3_b"):
        d[name] = take(8, 8).reshape(D_MODEL)
    for name in ("sgu_ln_g", "sgu_ln_b"):
        d[name] = take(8, 2).reshape(D_SGU)
    d["sgu_w_s"] = take(512, 512).reshape(N_SGU_GROUPS, SGU_CHUNK, SGU_CHUNK)
    d["sgu_b_s"] = take(8, 4).reshape(N_SGU_GROUPS, SGU_CHUNK)
    d["fox_b_f"] = take(8, 1).reshape(128)[:N_HEADS]
    d["conv_w"] = take(8, 6).reshape(3, D_CONV)
    return d


SMALL_NAMES = ("ln1_g", "ln1_b", "fox_b_f", "sgu_ln_g", "sgu_ln_b", "sgu_w_s", "sgu_b_s", "ln2_g", "ln2_b", "ln3_g", "ln3_b")
BIG_NAMES = ("ffn1_w_up", "ffn1_w_down", "mix_w_in", "mix_w_out", "ffn2_w_up", "ffn2_w_down")
UP_NAMES = ("ffn1_w_up", "ffn2_w_up")
WEIGHT_ORDER = ("ln1_g", "ln1_b", "ffn1_w_up", "ffn1_w_down", "mix_w_in", "fox_b_f", "conv_w", "sgu_ln_g", "sgu_ln_b",
                "sgu_w_s", "sgu_b_s", "mix_w_out", "ln2_g", "ln2_b", "ffn2_w_up", "ffn2_w_down", "ln3_g", "ln3_b")


class _Overlap:
    def __init__(self, w, after, me, where):
        self.me, self.where = me, where
        groups = [[("ffn1_w_up", 0), ("ffn1_w_down", 0)],
                  [("mix_w_in", 0), ("mix_w_out", 0), ("ffn2_w_up", 0), ("ffn2_w_down", 0)]]
        groups += [[(name, l) for name in BIG_NAMES] for l in range(1, DEPTH)]
        self.gathers = []
        for gi, group in enumerate(groups):
            shards = [w[name][l].astype(BF16) for name, l in group]
            lands = [lax.dynamic_update_slice(lax.empty((N_DEV,) + s.shape, BF16), s[None], (me, 0, 0)) for s in shards]
            started = _exchange_start(f"allgather_start_{gi}", _gather_plan(len(group)), 3 * len(group), shards + lands, after)
            after = started[3]
            self.gathers.append(dict(group=group, chips=started))
        self.all_started = self.last = after
        self.scatters = {}
        self.order = []
        self.small = []
        width = D_CONV // N_DEV
        rows = jnp.pad(_pad_rows(w["conv_w"].reshape(DEPTH * 3, width), 8), ((0, 0), (0, 128 - width)))
        land = lax.dynamic_update_slice(lax.empty((N_DEV,) + rows.shape, F32), rows[None], (me, 0, 0))
        self.conv_started = self._start("conv_w_start", _peers_plan(), N_DEV - 1, [rows, land])
        self.conv_full = None
        self.all_started = self.last

    def conv_w(self, after):
        if self.conv_full is None:
            width = D_CONV // N_DEV
            gathered = _exchange_wait("conv_w_wait", _peers_plan(), N_DEV - 1, self.conv_started, after)[1]
            self.conv_full = jnp.transpose(gathered[:, :DEPTH * 3, :width], (1, 0, 2)).reshape(DEPTH, 3, D_CONV)
        return self.conv_full

    def _start(self, name, plan, n_copies, arrays):
        started = _exchange_start(name, plan, n_copies, arrays, self.last)
        self.last = started[3]
        return started

    def _group_of(self, layer, part):
        return layer + 1 if layer > 0 else (0 if part == "ffn1" else 1)

    def pass_on(self, layer, part, after):
        st = self.gathers[self._group_of(layer, part)]
        if "sibling" not in st:
            gi, m = self._group_of(layer, part), len(st["group"])
            arrays = _exchange_wait(f"allgather_wait_{gi}", _gather_plan(m), 3 * m, st["chips"], after)
            st["sibling"] = self._start(f"allgather_pass_start_{gi}", _pass_on_plan(m), 4 * m, arrays)
        return st["sibling"][3]

    def weights(self, layer, part, after):
        gi = self._group_of(layer, part)
        st = self.gathers[gi]
        if "full" not in st:
            after = self.all_started if after is None else after
            self.pass_on(layer, part, after)
            m = len(st["group"])
            arrays = _exchange_wait(f"allgather_pass_wait_{gi}", _pass_on_plan(m), 4 * m, st["sibling"], after)
            st["full"] = dict(zip(st["group"], arrays[m:]))
        g = st["full"]

        def ffn(n):
            return g[(f"ffn{n}_w_up", layer)], g[(f"ffn{n}_w_down", layer)].reshape(N_FFN_CHUNK, FFN_BLK, D_MODEL)

        if part == "ffn1":
            return ffn(1)
        return (_w_in_from_blocks(g[("mix_w_in", layer)]), g[("mix_w_out", layer)].reshape(D_MODEL, D_MODEL), *ffn(2))

    def push(self, key, items):
        n = len(items)
        grads = [g for _, _, g in items]
        lands = [lax.empty((4,) + g.shape[1:], F32) for g in grads]
        started = self._start(f"rs_sibling_start_{key[0]}{key[1]}", _sibling_plan(n), 4 * n, grads + lands)
        self.scatters[key] = dict(items=items, sibling=started)
        self.order.append(key)
        return started[3]

    def advance(self, key, after):
        st = self.scatters[key]
        n = len(st["items"])
        arrays = _exchange_wait(f"rs_sibling_wait_{key[0]}{key[1]}", _sibling_plan(n), 4 * n, st["sibling"], after)
        partials = [_chip_partial(g, r, self.where) for g, r in zip(arrays[:n], arrays[n:])]
        p16 = [p for _, p in partials]
        lands = [lax.empty((3,) + p.shape[1:], BF16) for p in p16]
        started = self._start(f"rs_chip_start_{key[0]}{key[1]}", _chip_plan(n), 3 * n, p16 + lands)
        st.update(own32=[p for p, _ in partials], chip=started)
        return started[3]

    def push_small(self, rows):
        k = len(self.small)
        land = lax.dynamic_update_slice(lax.empty((N_DEV,) + rows.shape, F32), rows[None], (self.me, 0, 0))
        started = self._start(f"small_start_{k}", _peers_plan(), N_DEV - 1, [rows, land])
        self.small.append(started)
        return started[3]

    def finish(self, w, m, v):
        res = {}
        after = self.scatters[self.order[-1]]["chip"][3]
        for key in self.order:
            st = self.scatters[key]
            n = len(st["items"])
            arrays = _exchange_wait(f"rs_chip_wait_{key[0]}{key[1]}", _chip_plan(n), 3 * n, st["chip"], after)
            for (name, l, _), own32, r16 in zip(st["items"], st["own32"], arrays[n:]):
                res[name] = _adamw_shard(own32, r16, w[name], m[name], v[name], l, res.get(name))
                after = res[name][0]
        pieces = [_exchange_wait(f"small_wait_{k}", _peers_plan(), N_DEV - 1, started, after)[1]
                  for k, started in enumerate(self.small)]
        return res, pieces


def _local_step(x, target, comm, small, n_seq):
    def vec(a):
        return a.reshape(1, -1)

    saved = []
    h = x
    for l in range(DEPTH):
        s = {}
        s["up1"], s["down1"] = comm.weights(l, "ffn1", None if l == 0 else h)
        h1, h1b, s["z1"], s["gu1"], s["x0b"] = _ffn_fwd(h, s["up1"], s["down1"], vec(small["ln1_g"][l]), vec(small["ln1_b"][l]), h)
        s["w_in"], s["w_out"], s["up2"], s["down2"] = comm.weights(l, "rest", s["z1"])
        s["x1b"] = h1b
        conv, qkv, sgu, f = _in_proj(h1, s["w_in"])
        cw = _pad_rows(comm.conv_w(h1)[l], 8)
        bf = jnp.pad(small["fox_b_f"][l], (0, 128 - N_HEADS)).reshape(1, 128)
        b_mat = jnp.repeat(small["sgu_b_s"][l].T, D_SGU // N_SGU_GROUPS, axis=1)
        mid_params = (cw, bf, vec(small["sgu_ln_g"][l]), vec(small["sgu_ln_b"][l]), small["sgu_w_s"][l], b_mat)
        cat, cum_t = _mix_mid_fwd(conv, sgu, f, *mid_params, n_seq)
        cat, lse = _fox_fwd(qkv, cum_t, cat, n_seq)
        h2, h2b, s["z2"] = _mix_out_fwd(cat, h1, s["w_out"], vec(small["ln2_g"][l]), vec(small["ln2_b"][l]))
        s.update(conv=conv, qkv=qkv, sgu=sgu, f=f, mid_params=mid_params, cat=cat, cum_t=cum_t, lse=lse, x2b=h2b)
        token = comm.pass_on(l + 1, "ffn1", s["z2"]) if l + 1 < DEPTH else h2
        ln3 = (vec(small["ln3_g"][l]), vec(small["ln3_b"][l]))
        if l + 1 < DEPTH:
            h, _, s["z3"], s["gu2"], _ = _ffn_fwd(h2, s["up2"], s["down2"], *ln3, token)
        else:
            dh, loss, s["z3"], s["gu2"] = _ffn_fwd(h2, s["up2"], s["down2"], *ln3, token, target)
        saved.append(s)

    late_rows = None
    token = loss
    pending = None
    for l in reversed(range(DEPTH)):
        s = saved[l]
        sg = {}
        dh, dy, a, dgu, sg["ln3_g"], sg["ln3_b"] = _ffn_bwd(dh, s["z3"], s["gu2"], s["up2"], s["down2"], vec(small["ln3_g"][l]), token)
        if pending is not None:
            token = comm.advance(pending, dh)
        g_up2 = _matmul_tn(dgu.reshape(N_DEV, -1, FFN_BLK), s["x2b"][None], token)
        g_down2 = _matmul_tn(a, dy[None], token).reshape(N_DEV, FFN_BLK // 2, D_MODEL)
        dz, dzb, dya, dyb, dyc, sg["ln2_g"], sg["ln2_b"] = _mix_out_bwd(dh, s["z2"], s["w_out"], vec(small["ln2_g"][l]))
        g_out = _matmul_tn(s["cat"][None], dzb[None], token).reshape(N_DEV, D_MODEL // N_DEV, D_MODEL)
        dq, dk, dv, drow, dcol = _fox_bwd(s["qkv"], s["cum_t"], s["cat"], s["lse"], dyb, n_seq)
        dconv, dcw = _conv_bwd(s["conv"], dya, s["mid_params"][0], n_seq)
        dsgu, df, dbf, dlg, dlb, dws, dbs = _sgu_gate_bwd(s["sgu"], s["f"], dyc, drow, dcol, *s["mid_params"][1:], n_seq)
        sg.update(conv_w=dcw[:3], fox_b_f=dbf[0, :N_HEADS], sgu_ln_g=dlg[0], sgu_ln_b=dlb[0], sgu_w_s=dws,
                  sgu_b_s=dbs[:, :N_SGU_GROUPS].T)
        dh, dp = _mix_in_bwd(dconv, dq, dk, dv, dsgu, df, dz, s["w_in"])
        g_in = _w_in_to_blocks(_matmul_tn(s["x1b"][None], dp[None], token, tk=1024)[0])
        first = [("ffn2_w_up", l, g_up2), ("ffn2_w_down", l, g_down2), ("mix_w_out", l, g_out), ("mix_w_in", l, g_in)]
        for name in ("ln2_g", "ln2_b", "ln3_g", "ln3_b"):
            sg[name] = sg[name][0]
        if l == 0:
            comm.push((l, "a"), first)
            token = comm.push_small(_pack_rest(sg))
            pending, first = (l, "a"), []
        dh, dy, a, dgu, dg1, db1 = _ffn_bwd(dh, s["z1"], s["gu1"], s["up1"], s["down1"], vec(small["ln1_g"][l]), token)
        if l == 0:
            token = comm.advance(pending, dh)
        g_up1 = _matmul_tn(dgu.reshape(N_DEV, -1, FFN_BLK), s["x0b"][None], token)
        ln1_rows = jnp.concatenate([dg1.reshape(8, 128), db1.reshape(8, 128)], axis=0)
        if l == 0:
            token = comm.push((l, "b"), [("ffn1_w_up", l, g_up1)])
            g_down1 = _matmul_tn(a, dy[None], token).reshape(N_DEV, FFN_BLK // 2, D_MODEL)
            token = comm.advance((l, "b"), g_down1)
            token = comm.push((l, "c"), [("ffn1_w_down", l, g_down1)])
            token = comm.advance((l, "c"), token)
            late_rows = ln1_rows
        else:
            g_down1 = _matmul_tn(a, dy[None], token).reshape(N_DEV, FFN_BLK // 2, D_MODEL)
            pending = (l, "b")
            comm.push(pending, first + [("ffn1_w_up", l, g_up1), ("ffn1_w_down", l, g_down1)])
            token = comm.push_small(jnp.concatenate([ln1_rows, _pack_rest(sg)], axis=0))
    return loss, dh, late_rows


def kernel(x, ln1_g, ln1_b, ffn1_w_up, ffn1_w_down, mix_w_in, fox_b_f, conv_w, sgu_ln_g, sgu_ln_b, sgu_w_s, sgu_b_s, mix_w_out, ln2_g, ln2_b, ffn2_w_up, ffn2_w_down, ln3_g, ln3_b, loss_target, m_ln1_g, m_ln1_b, m_ffn1_w_up, m_ffn1_w_down, m_mix_w_in, m_fox_b_f, m_conv_w, m_sgu_ln_g, m_sgu_ln_b, m_sgu_w_s, m_sgu_b_s, m_mix_w_out, m_ln2_g, m_ln2_b, m_ffn2_w_up, m_ffn2_w_down, m_ln3_g, m_ln3_b, v_ln1_g, v_ln1_b, v_ffn1_w_up, v_ffn1_w_down, v_mix_w_in, v_fox_b_f, v_conv_w, v_sgu_ln_g, v_sgu_ln_b, v_sgu_w_s, v_sgu_b_s, v_mix_w_out, v_ln2_g, v_ln2_b, v_ffn2_w_up, v_ffn2_w_down, v_ln3_g, v_ln3_b):
    w = dict(ln1_g=ln1_g, ln1_b=ln1_b, ffn1_w_up=ffn1_w_up, ffn1_w_down=ffn1_w_down, mix_w_in=mix_w_in, fox_b_f=fox_b_f,
             conv_w=conv_w, sgu_ln_g=sgu_ln_g, sgu_ln_b=sgu_ln_b, sgu_w_s=sgu_w_s, sgu_b_s=sgu_b_s, mix_w_out=mix_w_out,
             ln2_g=ln2_g, ln2_b=ln2_b, ffn2_w_up=ffn2_w_up, ffn2_w_down=ffn2_w_down, ln3_g=ln3_g, ln3_b=ln3_b)
    m = dict(ln1_g=m_ln1_g, ln1_b=m_ln1_b, ffn1_w_up=m_ffn1_w_up, ffn1_w_down=m_ffn1_w_down, mix_w_in=m_mix_w_in,
             fox_b_f=m_fox_b_f, conv_w=m_conv_w, sgu_ln_g=m_sgu_ln_g, sgu_ln_b=m_sgu_ln_b, sgu_w_s=m_sgu_w_s,
             sgu_b_s=m_sgu_b_s, mix_w_out=m_mix_w_out, ln2_g=m_ln2_g, ln2_b=m_ln2_b, ffn2_w_up=m_ffn2_w_up,
             ffn2_w_down=m_ffn2_w_down, ln3_g=m_ln3_g, ln3_b=m_ln3_b)
    v = dict(ln1_g=v_ln1_g, ln1_b=v_ln1_b, ffn1_w_up=v_ffn1_w_up, ffn1_w_down=v_ffn1_w_down, mix_w_in=v_mix_w_in,
             fox_b_f=v_fox_b_f, conv_w=v_conv_w, sgu_ln_g=v_sgu_ln_g, sgu_ln_b=v_sgu_ln_b, sgu_w_s=v_sgu_w_s,
             sgu_b_s=v_sgu_b_s, mix_w_out=v_mix_w_out, ln2_g=v_ln2_g, ln2_b=v_ln2_b, ffn2_w_up=v_ffn2_w_up,
             ffn2_w_down=v_ffn2_w_down, ln3_g=v_ln3_g, ln3_b=v_ln3_b)

    mx, my, mc = lax.axis_index("x"), lax.axis_index("y"), lax.axis_index("c")
    me = 4 * mx + 2 * my + mc
    n_seq, seq, _ = x.shape
    t_tok = n_seq * seq
    for name in UP_NAMES:
        for t in (w, m, v):
            t[name] = jnp.transpose(t[name], (0, 2, 1))

    comm = _Overlap(w, x, me, jnp.stack([mc, 2 * mx + my]).astype(jnp.int32))
    small = {name: w[name] for name in SMALL_NAMES}

    loss_dev, grad_x, late_rows = _local_step(
        x.reshape(t_tok, D_MODEL), loss_target.reshape(t_tok, D_MODEL), comm, small, n_seq)
    loss = lax.psum(loss_dev[0, 0], ("x", "y", "c"))
    out, pieces = comm.finish(w, m, v)
    for name in UP_NAMES:
        out[name] = [jnp.transpose(a, (0, 2, 1)) for a in out[name]]

    pieces.append(_allgather_small(late_rows))
    spans = [(l, 0, LN1_ROWS + REST_ROWS) for l in reversed(range(1, DEPTH))] + [(0, LN1_ROWS, LN1_ROWS + REST_ROWS), (0, 0, LN1_ROWS)]

    def widen(a):
        return lax.dynamic_update_slice(jnp.zeros((3, D_CONV), F32), a, (0, me * (D_CONV // N_DEV)))

    packed = [[_pack_layer({**{name: t[name][l] for name in SMALL_NAMES}, "conv_w": widen(t["conv_w"][l])}) for l in range(DEPTH)]
              for t in (w, m, v)]
    rows_out = {}
    for (l, lo, hi), gathered_piece in zip(spans, pieces):
        rows_out[(l, lo)] = _adamw_small(gathered_piece, *[packed[t][l][lo:hi] for t in range(3)])
    per_layer = []
    for l in range(DEPTH):
        parts = sorted(lo for (ll, lo) in rows_out if ll == l)
        per_layer.append([_unpack_layer(jnp.concatenate([rows_out[(l, lo)][k] for lo in parts], axis=0)) for k in range(4)])
    for name in SMALL_NAMES:
        out[name] = [jnp.stack([per_layer[l][k][name] for l in range(DEPTH)]) for k in range(4)]
    lo_col = me * (D_CONV // N_DEV)
    out["conv_w"] = [jnp.stack([lax.dynamic_slice(per_layer[l][k]["conv_w"], (0, lo_col), (3, D_CONV // N_DEV)) for l in range(DEPTH)])
                     for k in range(4)]

    return (loss, grad_x.reshape(x.shape), *[out[name][0] for name in WEIGHT_ORDER], *[out[name][1] for name in WEIGHT_ORDER],
            *[out[name][2] for name in WEIGHT_ORDER], *[out[name][3] for name in WEIGHT_ORDER])
```

```python
import functools

import jax
import jax.numpy as jnp
from jax import lax
from jax.experimental import pallas as pl
from jax.experimental.pallas import tpu as pltpu

F32 = jnp.float32
BF16 = jnp.bfloat16
MESH = pl.DeviceIdType.MESH

N_DEV = 8
DEPTH = 2
D_MODEL = 1024
D_FF = 2816
FFN_BLK = 2 * D_FF // N_DEV
N_FFN_CHUNK = D_FF // FFN_BLK
D_CONV = 256
D_FOX = 512
N_HEADS = 8
D_SGU = 256
N_SGU_GROUPS = 4
SGU_CHUNK = 128
D_IN = 3 * D_CONV + 3 * D_FOX + N_HEADS + 2 * D_SGU
D_IN_SHARD = D_IN // N_DEV
COL_CONV, COL_QKV, COL_SGU, COL_F = 0, 768, 2304, 2816
D_IN_PAD = 2944
F_ORIG = 3 * D_CONV + 3 * D_FOX
ALPHA = (2 * DEPTH) ** 0.25
LN_EPS = 1e-5
ATT_SCALE = 0.125
ATT_BLK = 512
ATT_PAIRS = 2
NEG = -1e30

ADAM_LR, ADAM_B1, ADAM_B2, ADAM_EPS, ADAM_WD, ADAM_STEP = 0.001, 0.9, 0.999, 1e-08, 0.01, 10

VMEM_BYTES_V7X = 64 * 1024 * 1024
HIGHEST = lax.Precision.HIGHEST


def _params(vmem_mb, sem=None):
    assert vmem_mb * 1024 * 1024 < VMEM_BYTES_V7X
    kw = dict(vmem_limit_bytes=vmem_mb * 1024 * 1024)
    if sem is not None:
        kw["dimension_semantics"] = sem
    return pltpu.CompilerParams(**kw)


def _dot(a, b, precision=None):
    return lax.dot_general(a, b, (((1,), (0,)), ((), ())), preferred_element_type=F32, precision=precision)


def _dot_nt(a, b):
    return lax.dot_general(a, b, (((1,), (1,)), ((), ())), preferred_element_type=F32)


def _dot_tn(a, b):
    return lax.dot_general(a, b, (((0,), (0,)), ((), ())), preferred_element_type=F32)


def _ln_stats(z):
    mu = jnp.mean(z, axis=-1, keepdims=True)
    zc = z - mu
    var = jnp.mean(zc * zc, axis=-1, keepdims=True)
    rstd = lax.rsqrt(var + LN_EPS)
    return zc * rstd, rstd


def _ln_bwd(dy, xhat, rstd, g):
    dxh = dy * g
    m1 = jnp.mean(dxh, axis=-1, keepdims=True)
    m2 = jnp.mean(dxh * xhat, axis=-1, keepdims=True)
    return rstd * (dxh - m1 - xhat * m2)


_GELU_C = 0.7978845608028654


def _gelu(x):
    return 0.5 * x * (1.0 + jnp.tanh(_GELU_C * (x + 0.044715 * x * x * x)))


def _gelu_grad(x):
    t = jnp.tanh(_GELU_C * (x + 0.044715 * x * x * x))
    return 0.5 * (1.0 + t) + 0.5 * x * (1.0 - t * t) * _GELU_C * (1.0 + 3 * 0.044715 * x * x)


def _hbm(shape, dtype):
    n = 1
    for d in shape:
        n *= d
    if n * jnp.dtype(dtype).itemsize >= 1024 * 1024:
        return pltpu.HBM(tuple(shape), dtype)
    return jax.ShapeDtypeStruct(tuple(shape), dtype)


def _vspec():
    return pl.BlockSpec(memory_space=pltpu.VMEM)


def _anyspec():
    return pl.BlockSpec(memory_space=pl.ANY)


def _mesh_pos():
    return lax.axis_index("x"), lax.axis_index("y"), lax.axis_index("c")


def _other_chips(x, y):
    return [(1 - x, y), (x, 1 - y), (1 - x, 1 - y)]


_HBM_SPEC = pl.BlockSpec(memory_space=pltpu.HBM)
_SEM_SPEC = pl.BlockSpec(memory_space=pltpu.SEMAPHORE)
_DATAFLOW_EFFECT = pltpu.SideEffectType.DATAFLOW_SIDE_EFFECTING


def _remote_copies(plan, refs, send_sems, recv_sems):
    return [pltpu.make_async_remote_copy(src_ref=src, dst_ref=dst, send_sem=send_sems.at[k], recv_sem=recv_sems.at[k],
                                         device_id=to, device_id_type=MESH)
            for k, (src, dst, to) in enumerate(plan(refs, *_mesh_pos()))]


def _exchange_start(name, plan, n_copies, arrays, after):
    n = len(arrays)

    def body(*refs):
        send_sems, recv_sems, token = refs[n + 1], refs[n + 2], refs[-1]
        for cp in _remote_copies(plan, refs[:n], send_sems, recv_sems):
            cp.start()
        token[...] = jnp.zeros_like(token)

    out = pl.pallas_call(
        body, name=name,
        out_shape=(pltpu.SemaphoreType.DMA((n_copies,)), pltpu.SemaphoreType.DMA((n_copies,)),
                   *[pltpu.HBM(a.shape, a.dtype) for a in arrays], _hbm((8, 128), F32)),
        in_specs=[_HBM_SPEC] * n + [_anyspec()],
        out_specs=(_SEM_SPEC, _SEM_SPEC, *[_HBM_SPEC] * n, _vspec()),
        input_output_aliases={i: 2 + i for i in range(n)},
        compiler_params=pltpu.CompilerParams(has_side_effects=_DATAFLOW_EFFECT),
    )(*[pltpu.with_memory_space_constraint(a, pltpu.HBM) for a in arrays], after)
    return out[0], out[1], list(out[2:2 + n]), out[-1]


def _exchange_wait(name, plan, n_copies, started, after):
    send_sems, recv_sems, arrays, _ = started
    n = len(arrays)

    def body(*refs):
        for cp in _remote_copies(plan, refs[:n], refs[n], refs[n + 1]):
            cp.wait_send()
            cp.wait_recv()

    out = pl.pallas_call(
        body, name=name,
        out_shape=tuple(pltpu.HBM(a.shape, a.dtype) for a in arrays),
        in_specs=[_HBM_SPEC] * n + [_SEM_SPEC, _SEM_SPEC, _anyspec()], out_specs=tuple([_HBM_SPEC] * n),
        input_output_aliases={i: i for i in range(n)},
        compiler_params=pltpu.CompilerParams(has_side_effects=_DATAFLOW_EFFECT),
    )(*arrays, send_sems, recv_sems, after)
    return list(out)


def _gather_plan(m):
    def plan(refs, x, y, c):
        me = 4 * x + 2 * y + c
        return [(refs[i], refs[m + i].at[me], (*chip, c)) for i in range(m) for chip in _other_chips(x, y)]
    return plan


def _pass_on_plan(m):
    def plan(refs, x, y, c):
        out = []
        for i in range(m):
            out.append((refs[i], refs[m + i].at[4 * x + 2 * y + c], (x, y, 1 - c)))
            for cx, cy in _other_chips(x, y):
                block = refs[m + i].at[4 * cx + 2 * cy + c]
                out.append((block, block, (x, y, 1 - c)))
        return out
    return plan


def _peers_plan():
    def plan(refs, x, y, c):
        rel = [(dx, dy, dc) for dx in (0, 1) for dy in (0, 1) for dc in (0, 1)][1:]
        return [(refs[0], refs[1].at[4 * x + 2 * y + c], (x ^ dx, y ^ dy, c ^ dc)) for dx, dy, dc in rel]
    return plan


def _allgather_small(v):
    rows = v.shape[0]

    def body(v_ref, out_ref, send_sems, recv_sems):
        x, y, c = _mesh_pos()
        me = 4 * x + 2 * y + c
        out_ref[me] = v_ref[...]
        rel = [(dx, dy, dc) for dx in (0, 1) for dy in (0, 1) for dc in (0, 1)][1:]
        copies = []
        for k, (dx, dy, dc) in enumerate(rel):
            to = (x ^ dx, y ^ dy, c ^ dc)
            copies.append(pltpu.make_async_remote_copy(
                src_ref=v_ref, dst_ref=out_ref.at[me], send_sem=send_sems.at[k], recv_sem=recv_sems.at[k],
                device_id=to, device_id_type=MESH))
        for cp in copies:
            cp.start()
        for k, (dx, dy, dc) in enumerate(rel):
            src_blk = 4 * (x ^ dx) + 2 * (y ^ dy) + (c ^ dc)
            pltpu.make_async_remote_copy(
                src_ref=v_ref, dst_ref=out_ref.at[src_blk], send_sem=send_sems.at[k], recv_sem=recv_sems.at[k],
                device_id=(x, y, c), device_id_type=MESH).wait_recv()
        for cp in copies:
            cp.wait_send()

    return pl.pallas_call(
        body, name="allgather_small",
        out_shape=jax.ShapeDtypeStruct((N_DEV, rows, 128), v.dtype),
        in_specs=[_vspec()], out_specs=_vspec(),
        scratch_shapes=[pltpu.SemaphoreType.DMA((7,)), pltpu.SemaphoreType.DMA((7,))],
        compiler_params=_params(24),
    )(v)


def _sibling_plan(n):
    def plan(refs, x, y, c):
        return [(refs[a].at[2 * q + (1 - c)], refs[n + a].at[q], (x, y, 1 - c)) for a in range(n) for q in range(4)]
    return plan


def _chip_plan(n):
    def plan(refs, x, y, c):
        return [(refs[a].at[2 * cx + cy], refs[n + a].at[j], (cx, cy, c))
                for a in range(n) for j, (cx, cy) in enumerate(_other_chips(x, y))]
    return plan


def _row_tile(rows, cols, budget_bytes=2 * 1024 * 1024):
    best = 8
    for t in range(8, rows + 1, 8):
        if rows % t == 0 and t * cols * 4 <= budget_bytes:
            best = t
    return best


def _chip_partial(g, recv, where):
    _, rows, cols = g.shape
    tr = _row_tile(rows, cols)

    def body(where_ref, g_ref, r_ref, own_ref, o16_ref):
        s = g_ref[...] + r_ref[...]
        o16_ref[...] = s.astype(BF16)

        @pl.when(pl.program_id(1) == where_ref[1])
        def _():
            own_ref[...] = s

    blk = (None, tr, cols)
    return pl.pallas_call(
        body, name="rs_chip_partial",
        grid_spec=pltpu.PrefetchScalarGridSpec(
            num_scalar_prefetch=1, grid=(rows // tr, 4),
            in_specs=[pl.BlockSpec(blk, lambda i, q, w: (2 * q + w[0], i, 0)),
                      pl.BlockSpec(blk, lambda i, q, w: (q, i, 0))],
            out_specs=[pl.BlockSpec((tr, cols), lambda i, q, w: (i, 0)), pl.BlockSpec(blk, lambda i, q, w: (q, i, 0))]),
        out_shape=[_hbm((rows, cols), F32), _hbm((4, rows, cols), BF16)],
        compiler_params=_params(32),
    )(where, g, recv)


def _adam_math(w, g, m, v):
    m = ADAM_B1 * m + (1.0 - ADAM_B1) * g
    v = ADAM_B2 * v + (1.0 - ADAM_B2) * (g * g)
    m_hat = m / (1.0 - ADAM_B1 ** ADAM_STEP)
    v_hat = v / (1.0 - ADAM_B2 ** ADAM_STEP)
    delta = -ADAM_LR * (m_hat / (jnp.sqrt(v_hat) + ADAM_EPS) + ADAM_WD * w)
    return delta, m, v


def _adamw_shard(own32, recv16, w, m, v, layer, earlier):
    depth, rows, cols = w.shape
    tr = _row_tile(rows, cols, 1024 * 1024)
    n_prev = 0 if earlier is None else 4

    def body(p_ref, r_ref, w_ref, m_ref, v_ref, *rest):
        g_out, d_out, m_out, v_out = rest[n_prev:]
        g = p_ref[...] + r_ref[0].astype(F32) + r_ref[1].astype(F32) + r_ref[2].astype(F32)
        d, mn, vn = _adam_math(w_ref[...], g, m_ref[...], v_ref[...])
        g_out[...] = g
        d_out[...] = d
        m_out[...] = mn
        v_out[...] = vn

    mine = pl.BlockSpec((None, tr, cols), lambda i: (layer, i, 0))
    return pl.pallas_call(
        body, name="adamw_shard", grid=(rows // tr,),
        in_specs=[pl.BlockSpec((tr, cols), lambda i: (i, 0)), pl.BlockSpec((3, tr, cols), lambda i: (0, i, 0)),
                  mine, mine, mine] + [_anyspec()] * n_prev,
        out_specs=[mine] * 4,
        out_shape=[_hbm((depth, rows, cols), F32)] * 4,
        input_output_aliases={5 + k: k for k in range(n_prev)},
        compiler_params=_params(32),
    )(own32, recv16, *[pltpu.with_memory_space_constraint(t, pltpu.HBM) for t in (w, m, v)],
      *([] if earlier is None else earlier))


def _adamw_small(gathered, w, m, v):
    rows = w.shape[0]

    def body(a_ref, w_ref, m_ref, v_ref, g_out, d_out, m_out, v_out):
        g = a_ref[0]
        for d in range(1, N_DEV):
            g = g + a_ref[d]
        dl, mn, vn = _adam_math(w_ref[...], g, m_ref[...], v_ref[...])
        g_out[...] = g
        d_out[...] = dl
        m_out[...] = mn
        v_out[...] = vn

    return pl.pallas_call(
        body, name="adamw_small",
        in_specs=[_vspec()] * 4, out_specs=[_vspec()] * 4,
        out_shape=[_hbm((rows, 128), F32)] * 4,
        compiler_params=_params(32),
    )(gathered, w, m, v)


def _load_weights_once(pairs, sems):
    @pl.when(pl.program_id(0) == 0)
    def _():
        cps = [pltpu.make_async_copy(src, dst, sems.at[i]) for i, (src, dst) in enumerate(pairs)]
        for cp in cps:
            cp.start()
        for cp in cps:
            cp.wait()


def _ffn_fwd(x, wup, wd, ln_g, ln_b, after, target=None, tm=512):
    t_tok = x.shape[0]
    last = target is not None

    def body(x_ref, g_ref, b_ref, wup_hbm, wd_hbm, _after, *rest):
        if last:
            t_ref, dxn_ref, loss_ref, z_ref, gu_ref, wup_v, wd_v, sems = rest
        else:
            xn_ref, xnb_ref, z_ref, gu_ref, xb_ref, wup_v, wd_v, sems = rest
        _load_weights_once([(wup_hbm, wup_v), (wd_hbm, wd_v)], sems)
        xb = x_ref[...].astype(BF16)
        if not last:
            xb_ref[...] = xb
        y = None
        for j in range(N_FFN_CHUNK):
            g = _dot_nt(xb, wup_v[j])
            u = _dot_nt(xb, wup_v[N_FFN_CHUNK + j])
            gu_ref[0, j] = g.astype(BF16)
            gu_ref[1, j] = u.astype(BF16)
            a = (g * jax.nn.sigmoid(g) * u).astype(BF16)
            part = _dot(a, wd_v[j])
            y = part if y is None else y + part
        z = ALPHA * x_ref[...] + 0.5 * y
        xhat, _ = _ln_stats(z)
        xn = xhat * g_ref[...] + b_ref[...]
        z_ref[...] = z
        if last:
            err = xn - t_ref[...]
            dxn_ref[...] = err * (1.0 / D_MODEL)
            part = jnp.sum(jnp.sum(err * err, axis=1, keepdims=True), axis=0, keepdims=True) * (0.5 / D_MODEL)

            @pl.when(pl.program_id(0) == 0)
            def _():
                loss_ref[...] = jnp.zeros_like(loss_ref)

            loss_ref[...] += part
        else:
            xn_ref[...] = xn
            xnb_ref[...] = xn.astype(BF16)

    tok = pl.BlockSpec((tm, D_MODEL), lambda i: (i, 0))
    vec = pl.BlockSpec((1, D_MODEL), lambda i: (0, 0))
    gu_spec = pl.BlockSpec((2, N_FFN_CHUNK, tm, FFN_BLK), lambda i: (0, 0, i, 0))
    gu_shape = _hbm((2, N_FFN_CHUNK, t_tok, FFN_BLK), BF16)
    f32_tok, bf16_tok = _hbm((t_tok, D_MODEL), F32), _hbm((t_tok, D_MODEL), BF16)
    if last:
        extra_in, extra_spec = [target], [tok]
        out_specs = [tok, pl.BlockSpec((1, 128), lambda i: (0, 0)), tok, gu_spec]
        out_shape = [f32_tok, _hbm((1, 128), F32), f32_tok, gu_shape]
    else:
        extra_in, extra_spec = [], []
        out_specs = [tok, tok, tok, gu_spec, tok]
        out_shape = [f32_tok, bf16_tok, f32_tok, gu_shape, bf16_tok]
    return pl.pallas_call(
        body, name="ffn_fwd_loss" if last else "ffn_fwd", grid=(t_tok // tm,),
        in_specs=[tok, vec, vec, _anyspec(), _anyspec(), _anyspec()] + extra_spec,
        out_specs=out_specs, out_shape=out_shape,
        scratch_shapes=[pltpu.VMEM((N_DEV, FFN_BLK, D_MODEL), BF16), pltpu.VMEM((N_FFN_CHUNK, FFN_BLK, D_MODEL), BF16),
                        pltpu.SemaphoreType.DMA((2,))],
        compiler_params=_params(62, ("arbitrary",)),
    )(x, ln_g, ln_b, wup, wd, after, *extra_in)


def _ffn_bwd(dxn, z, gu, wup, wd, ln_g, after, tm=256):
    t_tok = dxn.shape[0]

    def body(dxn_ref, z_ref, gu_ref, g_ref, wup_hbm, wd_hbm, _after,
             dx_ref, dy_ref, a_ref, dgu_ref, dg_ref, db_ref, wup_v, wd_v, sems):
        i = pl.program_id(0)
        _load_weights_once([(wup_hbm, wup_v), (wd_hbm, wd_v)], sems)
        dxn_t = dxn_ref[...]
        xhat, rstd = _ln_stats(z_ref[...])
        pg = jnp.sum(dxn_t * xhat, axis=0, keepdims=True)
        pb = jnp.sum(dxn_t, axis=0, keepdims=True)

        @pl.when(i == 0)
        def _():
            dg_ref[...] = pg
            db_ref[...] = pb

        @pl.when(i > 0)
        def _():
            dg_ref[...] += pg
            db_ref[...] += pb

        dz = _ln_bwd(dxn_t, xhat, rstd, g_ref[...])
        dy = (0.5 * dz).astype(BF16)
        dy_ref[...] = dy
        dx = ALPHA * dz
        for j in range(N_FFN_CHUNK):
            da = _dot_nt(dy, wd_v[j])
            g = gu_ref[0, j].astype(F32)
            u = gu_ref[1, j].astype(F32)
            sig = jax.nn.sigmoid(g)
            silu = g * sig
            a_ref[j] = (silu * u).astype(BF16)
            dg = (da * u * (sig * (1.0 + g * (1.0 - sig)))).astype(BF16)
            du = (da * silu).astype(BF16)
            dgu_ref[0, j] = dg
            dgu_ref[1, j] = du
            dx = dx + _dot(dg, wup_v[j]) + _dot(du, wup_v[N_FFN_CHUNK + j])
        dx_ref[...] = dx

    tok = pl.BlockSpec((tm, D_MODEL), lambda i: (i, 0))
    vec = pl.BlockSpec((1, D_MODEL), lambda i: (0, 0))
    gu_spec = pl.BlockSpec((2, N_FFN_CHUNK, tm, FFN_BLK), lambda i: (0, 0, i, 0))
    return pl.pallas_call(
        body, name="ffn_bwd", grid=(t_tok // tm,),
        in_specs=[tok, tok, gu_spec, vec, _anyspec(), _anyspec(), _anyspec()],
        out_specs=[tok, tok, pl.BlockSpec((N_FFN_CHUNK, tm, FFN_BLK), lambda i: (0, i, 0)), gu_spec, vec, vec],
        out_shape=[_hbm((t_tok, D_MODEL), F32), _hbm((t_tok, D_MODEL), BF16),
                   _hbm((N_FFN_CHUNK, t_tok, FFN_BLK), BF16),
                   _hbm((2, N_FFN_CHUNK, t_tok, FFN_BLK), BF16),
                   _hbm((1, D_MODEL), F32), _hbm((1, D_MODEL), F32)],
        scratch_shapes=[pltpu.VMEM((N_DEV, FFN_BLK, D_MODEL), BF16), pltpu.VMEM((N_FFN_CHUNK, FFN_BLK, D_MODEL), BF16),
                        pltpu.SemaphoreType.DMA((2,))],
        compiler_params=_params(60, ("arbitrary",)),
    )(dxn, z, gu, ln_g, wup, wd, after)


def _matmul_tn(a, b, after, tk=4096):
    ga, t_tok, m = a.shape
    gb, _, n = b.shape
    groups = max(ga, gb)
    tk = min(tk, t_tok)

    def body(a_ref, b_ref, _after, o_ref):
        p = _dot_tn(a_ref[...].astype(BF16), b_ref[...].astype(BF16))

        @pl.when(pl.program_id(1) == 0)
        def _():
            o_ref[...] = p

        @pl.when(pl.program_id(1) > 0)
        def _():
            o_ref[...] += p

    return pl.pallas_call(
        body, name=f"matmul_tn_{m}x{n}", grid=(groups, t_tok // tk),
        in_specs=[pl.BlockSpec((None, tk, m), (lambda g, t: (g, t, 0)) if ga > 1 else (lambda g, t: (0, t, 0))),
                  pl.BlockSpec((None, tk, n), (lambda g, t: (g, t, 0)) if gb > 1 else (lambda g, t: (0, t, 0))),
                  _anyspec()],
        out_specs=pl.BlockSpec((None, m, n), lambda g, t: (g, 0, 0)),
        out_shape=_hbm((groups, m, n), F32),
        compiler_params=_params(56, ("arbitrary", "arbitrary")),
    )(a, b, after)


def _in_proj(x, w_in, tm=512):
    t_tok = x.shape[0]

    def body(x_ref, w_ref, conv_ref, qkv_ref, sgu_ref, f_ref):
        xb = x_ref[...].astype(BF16)
        conv_ref[...] = _dot(xb, w_ref[:, COL_CONV:COL_QKV])
        qkv_ref[...] = _dot(xb, w_ref[:, COL_QKV:COL_SGU]).astype(BF16)
        sgu_ref[...] = _dot(xb, w_ref[:, COL_SGU:COL_F])
        f_ref[...] = _dot(xb, w_ref[:, COL_F:D_IN_PAD])

    def tok(n):
        return pl.BlockSpec((tm, n), lambda i: (i, 0))

    return pl.pallas_call(
        body, name="mix_in_proj", grid=(t_tok // tm,),
        in_specs=[tok(D_MODEL), pl.BlockSpec((D_MODEL, D_IN_PAD), lambda i: (0, 0))],
        out_specs=[tok(768), tok(1536), tok(512), tok(128)],
        out_shape=[_hbm((t_tok, 768), F32), _hbm((t_tok, 1536), BF16),
                   _hbm((t_tok, 512), F32), _hbm((t_tok, 128), F32)],
        compiler_params=_params(48, ("arbitrary",)),
    )(x, w_in)


def _shift_down(a, k):
    row = lax.broadcasted_iota(jnp.int32, a.shape, 0)
    return jnp.where(row >= k, pltpu.roll(a, k, 0), 0.0)


def _shift_up(a, k):
    rows = a.shape[0]
    row = lax.broadcasted_iota(jnp.int32, a.shape, 0)
    return jnp.where(row < rows - k, pltpu.roll(a, rows - k, 0), 0.0)


def _tril(n):
    return lax.broadcasted_iota(jnp.int32, (n, n), 0) >= lax.broadcasted_iota(jnp.int32, (n, n), 1)


def _sgu_group_of_lane():
    return lax.broadcasted_iota(jnp.int32, (1, D_SGU), 1) // (D_SGU // N_SGU_GROUPS)


def _log_sigmoid(x):
    return jnp.minimum(x, 0.0) - jnp.log1p(jnp.exp(-jnp.abs(x)))


def _mix_mid_fwd(conv, sgu, f, conv_w, b_f, sgu_g, sgu_b, w_s, b_mat, n_seq):
    t_tok = conv.shape[0]
    seq = t_tok // n_seq
    n_chunk = seq // SGU_CHUNK
    per_blk = ATT_BLK // SGU_CHUNK

    def body(conv_ref, sgu_ref, f_ref, cw_ref, bf_ref, lg_ref, lb_ref, ws_ref, bm_ref, cat_ref, cum_ref):
        z = conv_ref[:, 256:512] * conv_ref[:, 512:768]
        y = cw_ref[0:1, :] * _shift_down(z, 2) + cw_ref[1:2, :] * _shift_down(z, 1) + cw_ref[2:3, :] * z
        cat_ref[:, 0:D_CONV] = (conv_ref[:, 0:256] * y).astype(BF16)
        cat_ref[:, D_CONV:D_CONV + D_FOX] = jnp.zeros((seq, D_FOX), BF16)

        tril = _tril(SGU_CHUNK)
        grp = _sgu_group_of_lane()
        wc = [jnp.where(tril, ws_ref[g], 0.0).astype(BF16) for g in range(N_SGU_GROUPS)]
        tri_f = tril.astype(F32)
        carry = jnp.zeros((1, 128), F32)
        for n in range(n_chunk):
            rows = pl.ds(n * SGU_CHUNK, SGU_CHUNK)
            u = _gelu(sgu_ref[rows, 0:256])
            vhat, _ = _ln_stats(_gelu(sgu_ref[rows, 256:512]))
            vn = (vhat * lg_ref[...] + lb_ref[...]).astype(BF16)
            mixed = bm_ref[...]
            for g in range(N_SGU_GROUPS):
                mixed = mixed + jnp.where(grp == g, _dot(wc[g], vn), 0.0)
            cat_ref[rows, D_CONV + D_FOX:D_MODEL] = (u * mixed).astype(BF16)

            log_f = _log_sigmoid(f_ref[rows, :] + bf_ref[...])
            cs = _dot(tri_f, log_f, HIGHEST) + carry
            carry = cs[SGU_CHUNK - 1:SGU_CHUNK, :]
            cs_t = cs.T
            lanes = pl.ds((n % per_blk) * SGU_CHUNK, SGU_CHUNK)
            for h in range(N_HEADS):
                cum_ref[h, n // per_blk, :, lanes] = cs_t[h:h + 1, :]

    def seq_blk(n):
        return pl.BlockSpec((seq, n), lambda b: (b, 0))

    def full(shape):
        return pl.BlockSpec(shape, lambda b: (0,) * len(shape))

    return pl.pallas_call(
        body, name="mix_mid_fwd", grid=(n_seq,),
        in_specs=[seq_blk(768), seq_blk(512), seq_blk(128), full((8, 256)), full((1, 128)), full((1, 256)),
                  full((1, 256)), full((4, 128, 128)), full((128, 256))],
        out_specs=[seq_blk(D_MODEL), pl.BlockSpec((N_HEADS, seq // ATT_BLK, 1, ATT_BLK), lambda b: (b, 0, 0, 0))],
        out_shape=[_hbm((t_tok, D_MODEL), BF16), _hbm((n_seq * N_HEADS, seq // ATT_BLK, 1, ATT_BLK), F32)],
        compiler_params=_params(48, ("arbitrary",)),
    )(conv, sgu, f, conv_w, b_f, sgu_g, sgu_b, w_s, b_mat)


def _head_masks():
    lane = lax.broadcasted_iota(jnp.int32, (1, 128), 1)
    return lane < 64, lane


def _fox_fwd(qkv, cum_t, cat, n_seq):
    t_tok = qkv.shape[0]
    seq = t_tok // n_seq
    nq = seq // ATT_BLK
    blk = ATT_BLK

    def body(q_ref, k_ref, v_ref, c_ref, _cat, o_ref, lse_ref):
        qi = pl.program_id(2)
        first, _ = _head_masks()
        causal = _tril(blk)
        one = jnp.ones((1, 128), BF16)
        qh = []
        for hp in range(ATT_PAIRS):
            qs = q_ref[:, 128 * hp:128 * hp + 128] * ATT_SCALE
            zero = jnp.zeros_like(qs)
            qh += [jnp.where(first, qs, zero), jnp.where(first, zero, qs)]

        def step(kb, carry, masked):
            ms, accs = carry
            rows = pl.ds(pl.multiple_of(kb * blk, blk), blk)
            new_m, new_acc = [], []
            for hp in range(ATT_PAIRS):
                k = k_ref[rows, 128 * hp:128 * hp + 128]
                v = v_ref[rows, 128 * hp:128 * hp + 128]
                for h in range(2):
                    i = 2 * hp + h
                    s = _dot_nt(qh[i], k) - c_ref[i, kb]
                    if masked:
                        s = jnp.where(causal, s, NEG)
                    m_new = jnp.maximum(ms[i], jnp.max(s, axis=1, keepdims=True))
                    p = jnp.exp(s - m_new)
                    vh = jnp.where(first, v, one) if h == 0 else jnp.where(first, one, v)
                    new_acc.append(accs[i] * jnp.exp(ms[i] - m_new) + _dot(p.astype(BF16), vh))
                    new_m.append(m_new)
            return tuple(new_m), tuple(new_acc)

        n_heads = 2 * ATT_PAIRS
        col = jnp.full((blk, 1), NEG, F32)
        zacc = jnp.zeros((blk, 128), F32)
        carry = lax.fori_loop(0, qi, lambda kb, cr: step(kb, cr, False), ((col,) * n_heads, (zacc,) * n_heads))
        ms, accs = step(qi, carry, True)
        for hp in range(ATT_PAIRS):
            acc0, acc1 = accs[2 * hp], accs[2 * hp + 1]
            l0 = pltpu.roll(acc0, 64, 1)
            l1 = pltpu.roll(acc1, 64, 1)
            o_ref[:, 128 * hp:128 * hp + 128] = jnp.where(first, acc0 / l0, acc1 / l1).astype(BF16)
            lse_ref[:, 128 * hp:128 * hp + 128] = jnp.where(first, ms[2 * hp] + jnp.log(l0), ms[2 * hp + 1] + jnp.log(l1))

    wide = 128 * ATT_PAIRS
    n_grp = D_FOX // wide
    first_col = D_CONV // wide
    return pl.pallas_call(
        body, name="fox_fwd", grid=(n_seq, n_grp, nq),
        in_specs=[pl.BlockSpec((blk, wide), lambda b, g, qi: (b * nq + qi, g)),
                  pl.BlockSpec((seq, wide), lambda b, g, qi: (b, n_grp + g)),
                  pl.BlockSpec((seq, wide), lambda b, g, qi: (b, 2 * n_grp + g)),
                  pl.BlockSpec((2 * ATT_PAIRS, nq, 1, blk), lambda b, g, qi: (b * n_grp + g, 0, 0, 0)), _anyspec()],
        out_specs=[pl.BlockSpec((blk, wide), lambda b, g, qi: (b * nq + qi, first_col + g)),
                   pl.BlockSpec((blk, wide), lambda b, g, qi: (b * nq + qi, g))],
        out_shape=[_hbm(cat.shape, BF16), _hbm((t_tok, D_FOX), F32)],
        input_output_aliases={4: 0},
        compiler_params=_params(48, ("arbitrary", "arbitrary", "arbitrary")),
    )(qkv, qkv, qkv, cum_t, cat)


def _fox_bwd(qkv, cum_t, cat, lse, d_o, n_seq):
    t_tok = qkv.shape[0]
    seq = t_tok // n_seq
    nk = seq // ATT_BLK
    blk = ATT_BLK

    def body(q_ref, k_ref, v_ref, c_ref, o_ref, lse_ref, do_ref, dq_ref, dk_ref, dv_ref, drow_ref, dcol_ref):
        kb = pl.program_id(2)
        first, lane = _head_masks()
        second = jnp.logical_not(first)
        one = jnp.ones((1, 128), BF16)
        causal = _tril(blk)

        @pl.when(kb == 0)
        def _():
            dq_ref[...] = jnp.zeros_like(dq_ref)
            drow_ref[...] = jnp.zeros_like(drow_ref)

        def step(qi, carry, masked):
            rows = pl.ds(pl.multiple_of(qi * blk, blk), blk)
            dks, dvs = carry
            new_dk, new_dv = [], []
            for hp in range(ATT_PAIRS):
                cols = slice(128 * hp, 128 * hp + 128)
                k = k_ref[:, cols]
                v = v_ref[:, cols]
                ks = k * ATT_SCALE
                zero = jnp.zeros_like(k)
                qs = q_ref[rows, cols] * ATT_SCALE
                d_o = do_ref[rows, cols]
                dd = d_o.astype(F32) * o_ref[rows, cols].astype(F32)
                lse_t = lse_ref[rows, cols]
                dq = []
                for h, mine in enumerate((first, second)):
                    i = 2 * hp + h
                    qh = jnp.where(mine, qs, zero)
                    doh = jnp.where(mine, d_o, zero)
                    delta = jnp.sum(jnp.where(mine, dd, 0.0), axis=1, keepdims=True)
                    lse_h = jnp.sum(jnp.where(lane == 64 * h, lse_t, 0.0), axis=1, keepdims=True)
                    s = _dot_nt(qh, k) - c_ref[i]
                    if masked:
                        s = jnp.where(causal, s, NEG)
                    p = jnp.exp(s - lse_h)
                    ds = (p * (_dot_nt(doh, v) - delta)).astype(BF16)
                    new_dk.append(dks[i] + _dot_tn(ds, jnp.where(mine, qs, one)))
                    new_dv.append(dvs[i] + _dot_tn(p.astype(BF16), doh))
                    dq.append(_dot(ds, jnp.where(mine, ks, one)))
                dq_ref[rows, cols] += jnp.where(first, dq[0], dq[1])
                drow_ref[rows, cols] += jnp.where(first, dq[1], dq[0])
            return tuple(new_dk), tuple(new_dv)

        zt = (jnp.zeros((blk, 128), F32),) * (2 * ATT_PAIRS)
        carry = step(kb, (zt, zt), True)
        dks, dvs = lax.fori_loop(kb + 1, nk, lambda qi, cr: step(qi, cr, False), carry)
        for hp in range(ATT_PAIRS):
            cols = slice(128 * hp, 128 * hp + 128)
            dk_ref[:, cols] = jnp.where(first, dks[2 * hp], dks[2 * hp + 1]).astype(BF16)
            dcol_ref[:, cols] = jnp.where(first, dks[2 * hp + 1], dks[2 * hp])
            dv_ref[:, cols] = (dvs[2 * hp] + dvs[2 * hp + 1]).astype(BF16)

    wide = 128 * ATT_PAIRS
    n_grp = D_FOX // wide

    def seq_spec(col0):
        return pl.BlockSpec((seq, wide), lambda b, g, kb: (b, col0 + g))

    def key_spec(col0):
        return pl.BlockSpec((blk, wide), lambda b, g, kb: (b * nk + kb, col0 + g))

    return pl.pallas_call(
        body, name="fox_bwd", grid=(n_seq, n_grp, nk),
        in_specs=[seq_spec(0), key_spec(n_grp), key_spec(2 * n_grp),
                  pl.BlockSpec((2 * ATT_PAIRS, None, 1, blk), lambda b, g, kb: (b * n_grp + g, kb, 0, 0)),
                  seq_spec(D_CONV // wide), seq_spec(0), seq_spec(0)],
        out_specs=[seq_spec(0), key_spec(0), key_spec(0), seq_spec(0), key_spec(0)],
        out_shape=[_hbm((t_tok, D_FOX), F32), _hbm((t_tok, D_FOX), BF16),
                   _hbm((t_tok, D_FOX), BF16), _hbm((t_tok, D_FOX), F32),
                   _hbm((t_tok, D_FOX), F32)],
        compiler_params=_params(56, ("arbitrary", "arbitrary", "arbitrary")),
    )(qkv, qkv, qkv, cum_t, cat, lse, d_o)


def _mix_out_fwd(cat, x, w_out, ln_g, ln_b, tm=512):
    t_tok = x.shape[0]

    def body(cat_ref, x_ref, w_ref, g_ref, b_ref, xn_ref, xnb_ref, z_ref):
        z = ALPHA * x_ref[...] + _dot(cat_ref[...], w_ref[...])
        xhat, _ = _ln_stats(z)
        xn = xhat * g_ref[...] + b_ref[...]
        z_ref[...] = z
        xn_ref[...] = xn
        xnb_ref[...] = xn.astype(BF16)

    def tok(n):
        return pl.BlockSpec((tm, n), lambda i: (i, 0))

    vec = pl.BlockSpec((1, D_MODEL), lambda i: (0, 0))
    return pl.pallas_call(
        body, name="mix_out_fwd", grid=(t_tok // tm,),
        in_specs=[tok(D_MODEL), tok(D_MODEL), pl.BlockSpec((D_MODEL, D_MODEL), lambda i: (0, 0)), vec, vec],
        out_specs=[tok(D_MODEL)] * 3,
        out_shape=[_hbm((t_tok, D_MODEL), F32), _hbm((t_tok, D_MODEL), BF16),
                   _hbm((t_tok, D_MODEL), F32)],
        compiler_params=_params(40, ("arbitrary",)),
    )(cat, x, w_out, ln_g, ln_b)


def _mix_out_bwd(dxn, z, w_out, ln_g, tm=512):
    t_tok = dxn.shape[0]

    def body(dxn_ref, z_ref, w_ref, g_ref, dz_ref, dzb_ref, dya_ref, dyb_ref, dyc_ref, dg_ref, db_ref):
        i = pl.program_id(0)
        dxn_t = dxn_ref[...]
        xhat, rstd = _ln_stats(z_ref[...])
        pg = jnp.sum(dxn_t * xhat, axis=0, keepdims=True)
        pb = jnp.sum(dxn_t, axis=0, keepdims=True)

        @pl.when(i == 0)
        def _():
            dg_ref[...] = pg
            db_ref[...] = pb

        @pl.when(i > 0)
        def _():
            dg_ref[...] += pg
            db_ref[...] += pb

        dz = _ln_bwd(dxn_t, xhat, rstd, g_ref[...])
        dzb = dz.astype(BF16)
        dz_ref[...] = dz
        dzb_ref[...] = dzb
        dya_ref[...] = _dot_nt(dzb, w_ref[0:256, :])
        dyb_ref[...] = _dot_nt(dzb, w_ref[256:768, :]).astype(BF16)
        dyc_ref[...] = _dot_nt(dzb, w_ref[768:1024, :])

    def tok(n):
        return pl.BlockSpec((tm, n), lambda i: (i, 0))

    vec = pl.BlockSpec((1, D_MODEL), lambda i: (0, 0))
    return pl.pallas_call(
        body, name="mix_out_bwd", grid=(t_tok // tm,),
        in_specs=[tok(D_MODEL), tok(D_MODEL), pl.BlockSpec((D_MODEL, D_MODEL), lambda i: (0, 0)), vec],
        out_specs=[tok(D_MODEL), tok(D_MODEL), tok(256), tok(512), tok(256), vec, vec],
        out_shape=[_hbm((t_tok, D_MODEL), F32), _hbm((t_tok, D_MODEL), BF16),
                   _hbm((t_tok, 256), F32), _hbm((t_tok, 512), BF16),
                   _hbm((t_tok, 256), F32),
                   _hbm((1, D_MODEL), F32), _hbm((1, D_MODEL), F32)],
        compiler_params=_params(40, ("arbitrary",)),
    )(dxn, z, w_out, ln_g)


def _conv_bwd(conv, dya, conv_w, n_seq):
    t_tok = conv.shape[0]
    seq = t_tok // n_seq

    def body(conv_ref, dya_ref, cw_ref, dconv_ref, dcw_ref):
        @pl.when(pl.program_id(0) == 0)
        def _():
            dcw_ref[...] = jnp.zeros_like(dcw_ref)

        z = conv_ref[:, 256:512] * conv_ref[:, 512:768]
        z1 = _shift_down(z, 1)
        z2 = _shift_down(z, 2)
        y = cw_ref[0:1, :] * z2 + cw_ref[1:2, :] * z1 + cw_ref[2:3, :] * z
        dya_t = dya_ref[...]
        dconv_ref[:, 0:256] = (dya_t * y).astype(BF16)
        dy = dya_t * conv_ref[:, 0:256]
        dcw_ref[0:1, :] += jnp.sum(dy * z2, axis=0, keepdims=True)
        dcw_ref[1:2, :] += jnp.sum(dy * z1, axis=0, keepdims=True)
        dcw_ref[2:3, :] += jnp.sum(dy * z, axis=0, keepdims=True)
        dz = cw_ref[2:3, :] * dy + cw_ref[1:2, :] * _shift_up(dy, 1) + cw_ref[0:1, :] * _shift_up(dy, 2)
        dconv_ref[:, 256:512] = (dz * conv_ref[:, 512:768]).astype(BF16)
        dconv_ref[:, 512:768] = (dz * conv_ref[:, 256:512]).astype(BF16)

    def seq_blk(n):
        return pl.BlockSpec((seq, n), lambda b: (b, 0))

    par = pl.BlockSpec((8, 256), lambda b: (0, 0))
    return pl.pallas_call(
        body, name="conv_bwd", grid=(n_seq,),
        in_specs=[seq_blk(768), seq_blk(256), par], out_specs=[seq_blk(768), par],
        out_shape=[_hbm((t_tok, 768), BF16), _hbm((8, 256), F32)],
        compiler_params=_params(56, ("arbitrary",)),
    )(conv, dya, conv_w)


def _sgu_gate_bwd(sgu, f, dyc, drow, dcol, b_f, sgu_g, sgu_b, w_s, b_mat, n_seq):
    t_tok = sgu.shape[0]
    seq = t_tok // n_seq
    n_chunk = seq // SGU_CHUNK

    def body(sgu_ref, f_ref, dyc_ref, drow_ref, dcol_ref, bf_ref, lg_ref, lb_ref, ws_ref, bm_ref,
             dsgu_ref, df_ref, dbf_ref, dlg_ref, dlb_ref, dws_ref, dbs_ref, dbm_acc):
        b = pl.program_id(0)

        @pl.when(b == 0)
        def _():
            for r in (dbf_ref, dlg_ref, dlb_ref, dws_ref, dbm_acc):
                r[...] = jnp.zeros_like(r)

        tril = _tril(SGU_CHUNK)
        grp = _sgu_group_of_lane()
        wc = [jnp.where(tril, ws_ref[g], 0.0).astype(BF16) for g in range(N_SGU_GROUPS)]
        for n in range(n_chunk):
            rows = pl.ds(n * SGU_CHUNK, SGU_CHUNK)
            su = sgu_ref[rows, 0:256]
            sv = sgu_ref[rows, 256:512]
            u = _gelu(su)
            vhat, rstd = _ln_stats(_gelu(sv))
            vn = (vhat * lg_ref[...] + lb_ref[...]).astype(BF16)
            mixed = bm_ref[...]
            for g in range(N_SGU_GROUPS):
                mixed = mixed + jnp.where(grp == g, _dot(wc[g], vn), 0.0)
            dyc_t = dyc_ref[rows, :]
            dsgu_ref[rows, 0:256] = (dyc_t * mixed * _gelu_grad(su)).astype(BF16)
            dmixed = dyc_t * u
            dbm_acc[...] += dmixed
            dvn = jnp.zeros((SGU_CHUNK, D_SGU), F32)
            for g in range(N_SGU_GROUPS):
                dm_g = jnp.where(grp == g, dmixed, 0.0).astype(BF16)
                dws_ref[g] += _dot_nt(dm_g, vn)
                dvn = dvn + _dot_tn(wc[g], dm_g)
            dlg_ref[...] += jnp.sum(dvn * vhat, axis=0, keepdims=True)
            dlb_ref[...] += jnp.sum(dvn, axis=0, keepdims=True)
            dsgu_ref[rows, 256:512] = (_ln_bwd(dvn, vhat, rstd, lg_ref[...]) * _gelu_grad(sv)).astype(BF16)

        later = (lax.broadcasted_iota(jnp.int32, (128, 128), 0) <= lax.broadcasted_iota(jnp.int32, (128, 128), 1)).astype(F32)
        head = lax.broadcasted_iota(jnp.int32, (D_FOX, 128), 1)
        pick = (lax.broadcasted_iota(jnp.int32, (D_FOX, 128), 0) == 128 * (head // 2) + 64 * (1 - head % 2)).astype(F32)
        carry = jnp.zeros((1, 128), F32)
        for n in reversed(range(n_chunk)):
            rows = pl.ds(n * SGU_CHUNK, SGU_CHUNK)
            dcum_n = _dot(drow_ref[rows, :] - dcol_ref[rows, :], pick, HIGHEST)
            dlf = _dot(later, dcum_n, HIGHEST) + carry
            carry = carry + jnp.sum(dcum_n, axis=0, keepdims=True)
            df = dlf * jax.nn.sigmoid(-(f_ref[rows, :] + bf_ref[...]))
            df_ref[rows, :] = df.astype(BF16)
            dbf_ref[...] += jnp.sum(df, axis=0, keepdims=True)

        @pl.when(b == n_seq - 1)
        def _():
            for g in range(N_SGU_GROUPS):
                dws_ref[g] = jnp.where(tril, dws_ref[g], 0.0)
            sel = (lax.broadcasted_iota(jnp.int32, (D_SGU, 128), 0) // (D_SGU // N_SGU_GROUPS)
                   == lax.broadcasted_iota(jnp.int32, (D_SGU, 128), 1)).astype(F32)
            dbs_ref[...] = _dot(dbm_acc[...], sel, HIGHEST)

    def seq_blk(n):
        return pl.BlockSpec((seq, n), lambda b: (b, 0))

    def full(shape):
        return pl.BlockSpec(shape, lambda b: (0,) * len(shape))

    param_shapes = [(1, 128), (1, 256), (1, 256), (4, 128, 128), (128, 128)]
    return pl.pallas_call(
        body, name="sgu_gate_bwd", grid=(n_seq,),
        in_specs=[seq_blk(512), seq_blk(128), seq_blk(256), seq_blk(D_FOX), seq_blk(D_FOX),
                  full((1, 128)), full((1, 256)), full((1, 256)), full((4, 128, 128)), full((128, 256))],
        out_specs=[seq_blk(512), seq_blk(128)] + [full(s) for s in param_shapes],
        out_shape=[_hbm((t_tok, 512), BF16), _hbm((t_tok, 128), BF16)]
        + [_hbm(s, F32) for s in param_shapes],
        scratch_shapes=[pltpu.VMEM((128, 256), F32)],
        compiler_params=_params(48, ("arbitrary",)),
    )(sgu, f, dyc, drow, dcol, b_f, sgu_g, sgu_b, w_s, b_mat)


def _mix_in_bwd(dconv, dq, dk, dv, dsgu, df, dz, w_in, tm=512):
    t_tok = dz.shape[0]

    def body(dconv_ref, dq_ref, dk_ref, dv_ref, dsgu_ref, df_ref, dz_ref, w_ref, dx_ref, dp_ref):
        dqb = dq_ref[...].astype(BF16)
        pieces = [(COL_CONV, dconv_ref[...]), (COL_QKV, dqb), (COL_QKV + 512, dk_ref[...]), (COL_QKV + 1024, dv_ref[...]),
                  (COL_SGU, dsgu_ref[...]), (COL_F, df_ref[...])]
        dx = ALPHA * dz_ref[...]
        for col, val in pieces:
            width = val.shape[1]
            dp_ref[:, col:col + width] = val
            dx = dx + _dot_nt(val, w_ref[:, col:col + width])
        dx_ref[...] = dx

    def tok(n):
        return pl.BlockSpec((tm, n), lambda i: (i, 0))

    return pl.pallas_call(
        body, name="mix_in_bwd", grid=(t_tok // tm,),
        in_specs=[tok(768), tok(512), tok(512), tok(512), tok(512), tok(128), tok(D_MODEL),
                  pl.BlockSpec((D_MODEL, D_IN_PAD), lambda i: (0, 0))],
        out_specs=[tok(D_MODEL), tok(D_IN_PAD)],
        out_shape=[_hbm((t_tok, D_MODEL), F32), _hbm((t_tok, D_IN_PAD), BF16)],
        compiler_params=_params(48, ("arbitrary",)),
    )(dconv, dq, dk, dv, dsgu, df, dz, w_in)


def _pad_rows(a, rows):
    return jnp.pad(a, ((0, rows - a.shape[0]), (0, 0)))


F_BLOCK = F_ORIG // D_IN_SHARD
F_AT = F_ORIG - F_BLOCK * D_IN_SHARD
assert (F_ORIG + N_HEADS) // D_IN_SHARD == F_BLOCK


def _w_in_from_blocks(g):
    fb = g[F_BLOCK]
    zeros = jnp.zeros((D_MODEL, D_IN_PAD - COL_F - N_HEADS), g.dtype)
    return jnp.concatenate([g[d] for d in range(F_BLOCK)] + [fb[:, :F_AT], fb[:, F_AT + N_HEADS:]]
                           + [g[d] for d in range(F_BLOCK + 1, N_DEV)] + [fb[:, F_AT:F_AT + N_HEADS], zeros], axis=1)


def _w_in_to_blocks(dw):
    def cols(lo, hi):
        shift = 0 if hi <= F_ORIG else N_HEADS
        return dw[:, lo - shift:hi - shift]

    blocks = []
    for d in range(N_DEV):
        lo, hi = d * D_IN_SHARD, (d + 1) * D_IN_SHARD
        if d == F_BLOCK:
            blocks.append(jnp.concatenate([cols(lo, F_ORIG), dw[:, COL_F:COL_F + N_HEADS], cols(F_ORIG + N_HEADS, hi)], axis=1))
        else:
            blocks.append(cols(lo, hi))
    return jnp.stack(blocks)


LN1_ROWS = 2 * 8
REST_ROWS = 4 * 8 + 2 * 8 + 512 + 8 + 8 + 8


def _pack_rest(p):
    rows = [p[name].reshape(8, 128) for name in ("ln2_g", "ln2_b", "ln3_g", "ln3_b")]
    rows += [_pad_rows(p[name].reshape(2, 128), 8) for name in ("sgu_ln_g", "sgu_ln_b")]
    rows += [p["sgu_w_s"].reshape(512, 128), _pad_rows(p["sgu_b_s"], 8),
             _pad_rows(jnp.pad(p["fox_b_f"], (0, 128 - N_HEADS)).reshape(1, 128), 8), _pad_rows(p["conv_w"].reshape(6, 128), 8)]
    return jnp.concatenate(rows, axis=0)


def _pack_layer(p):
    return jnp.concatenate([p["ln1_g"].reshape(8, 128), p["ln1_b"].reshape(8, 128), _pack_rest(p)], axis=0)


def _unpack_layer(a):
    r = 0

    def take(n, valid):
        nonlocal r
        piece = a[r:r + valid]
        r += n
        return piece

    d = {}
    for name in ("ln1_g", "ln1_b", "ln2_g", "ln2_b", "ln3_g", "ln3_b"):
        d[name] = take(8, 8).reshape(D_MODEL)
    for name in ("sgu_ln_g", "sgu_ln_b"):
        d[name] = take(8, 2).reshape(D_SGU)
    d["sgu_w_s"] = take(512, 512).reshape(N_SGU_GROUPS, SGU_CHUNK, SGU_CHUNK)
    d["sgu_b_s"] = take(8, 4).reshape(N_SGU_GROUPS, SGU_CHUNK)
    d["fox_b_f"] = take(8, 1).reshape(128)[:N_HEADS]
    d["conv_w"] = take(8, 6).reshape(3, D_CONV)
    return d


SMALL_NAMES = ("ln1_g", "ln1_b", "fox_b_f", "sgu_ln_g", "sgu_ln_b", "sgu_w_s", "sgu_b_s", "ln2_g", "ln2_b", "ln3_g", "ln3_b")
BIG_NAMES = ("ffn1_w_up", "ffn1_w_down", "mix_w_in", "mix_w_out", "ffn2_w_up", "ffn2_w_down")
UP_NAMES = ("ffn1_w_up", "ffn2_w_up")
WEIGHT_ORDER = ("ln1_g", "ln1_b", "ffn1_w_up", "ffn1_w_down", "mix_w_in", "fox_b_f", "conv_w", "sgu_ln_g", "sgu_ln_b",
                "sgu_w_s", "sgu_b_s", "mix_w_out", "ln2_g", "ln2_b", "ffn2_w_up", "ffn2_w_down", "ln3_g", "ln3_b")


class _Overlap:
    def __init__(self, w, after, me, where):
        self.me, self.where = me, where
        self.last = after
        width = D_CONV // N_DEV
        rows = jnp.pad(_pad_rows(w["conv_w"].reshape(DEPTH * 3, width), 8), ((0, 0), (0, 128 - width)))
        land = lax.dynamic_update_slice(lax.empty((N_DEV,) + rows.shape, F32), rows[None], (me, 0, 0))
        self.conv_started = self._start("conv_w_start", _peers_plan(), N_DEV - 1, [rows, land])
        self.conv_full = None
        groups = [[("ffn1_w_up", 0), ("ffn1_w_down", 0)],
                  [("mix_w_in", 0), ("mix_w_out", 0), ("ffn2_w_up", 0), ("ffn2_w_down", 0)]]
        groups += [[(name, l) for name in BIG_NAMES] for l in range(1, DEPTH)]
        self.gathers = []
        for gi, group in enumerate(groups):
            shards = [w[name][l].astype(BF16) for name, l in group]
            lands = [lax.dynamic_update_slice(lax.empty((N_DEV,) + s.shape, BF16), s[None], (me, 0, 0)) for s in shards]
            started = self._start(f"allgather_start_{gi}", _gather_plan(len(group)), 3 * len(group), shards + lands)
            self.gathers.append(dict(group=group, chips=started))
        self.all_started = self.last
        self.scatters = {}
        self.order = []
        self.small = []

    def conv_w(self, after):
        if self.conv_full is None:
            width = D_CONV // N_DEV
            gathered = _exchange_wait("conv_w_wait", _peers_plan(), N_DEV - 1, self.conv_started, after)[1]
            self.conv_full = jnp.transpose(gathered[:, :DEPTH * 3, :width], (1, 0, 2)).reshape(DEPTH, 3, D_CONV)
        return self.conv_full

    def _start(self, name, plan, n_copies, arrays):
        started = _exchange_start(name, plan, n_copies, arrays, self.last)
        self.last = started[3]
        return started

    def _group_of(self, layer, part):
        return layer + 1 if layer > 0 else (0 if part == "ffn1" else 1)

    def pass_on(self, layer, part, after):
        st = self.gathers[self._group_of(layer, part)]
        if "sibling" not in st:
            gi, m = self._group_of(layer, part), len(st["group"])
            arrays = _exchange_wait(f"allgather_wait_{gi}", _gather_plan(m), 3 * m, st["chips"], after)
            st["sibling"] = self._start(f"allgather_pass_start_{gi}", _pass_on_plan(m), 4 * m, arrays)
        return st["sibling"][3]

    def weights(self, layer, part, after):
        gi = self._group_of(layer, part)
        st = self.gathers[gi]
        if "full" not in st:
            after = self.all_started if after is None else after
            self.pass_on(layer, part, after)
            m = len(st["group"])
            arrays = _exchange_wait(f"allgather_pass_wait_{gi}", _pass_on_plan(m), 4 * m, st["sibling"], after)
            st["full"] = dict(zip(st["group"], arrays[m:]))
        g = st["full"]

        def ffn(n):
            return g[(f"ffn{n}_w_up", layer)], g[(f"ffn{n}_w_down", layer)].reshape(N_FFN_CHUNK, FFN_BLK, D_MODEL)

        if part == "ffn1":
            return ffn(1)
        return (_w_in_from_blocks(g[("mix_w_in", layer)]), g[("mix_w_out", layer)].reshape(D_MODEL, D_MODEL), *ffn(2))

    def push(self, key, items):
        n = len(items)
        grads = [g for _, _, g in items]
        lands = [lax.empty((4,) + g.shape[1:], F32) for g in grads]
        started = self._start(f"rs_sibling_start_{key[0]}{key[1]}", _sibling_plan(n), 4 * n, grads + lands)
        self.scatters[key] = dict(items=items, sibling=started)
        self.order.append(key)
        return started[3]

    def advance(self, key, after):
        st = self.scatters[key]
        n = len(st["items"])
        arrays = _exchange_wait(f"rs_sibling_wait_{key[0]}{key[1]}", _sibling_plan(n), 4 * n, st["sibling"], after)
        partials = [_chip_partial(g, r, self.where) for g, r in zip(arrays[:n], arrays[n:])]
        p16 = [p for _, p in partials]
        lands = [lax.empty((3,) + p.shape[1:], BF16) for p in p16]
        started = self._start(f"rs_chip_start_{key[0]}{key[1]}", _chip_plan(n), 3 * n, p16 + lands)
        st.update(own32=[p for p, _ in partials], chip=started)
        return started[3]

    def push_small(self, rows):
        k = len(self.small)
        land = lax.dynamic_update_slice(lax.empty((N_DEV,) + rows.shape, F32), rows[None], (self.me, 0, 0))
        started = self._start(f"small_start_{k}", _peers_plan(), N_DEV - 1, [rows, land])
        self.small.append(started)
        return started[3]

    def finish(self, w, m, v):
        res = {}
        after = self.scatters[self.order[-1]]["chip"][3]
        for key in self.order:
            st = self.scatters[key]
            n = len(st["items"])
            arrays = _exchange_wait(f"rs_chip_wait_{key[0]}{key[1]}", _chip_plan(n), 3 * n, st["chip"], after)
            for (name, l, _), own32, r16 in zip(st["items"], st["own32"], arrays[n:]):
                res[name] = _adamw_shard(own32, r16, w[name], m[name], v[name], l, res.get(name))
                after = res[name][0]
        pieces = [_exchange_wait(f"small_wait_{k}", _peers_plan(), N_DEV - 1, started, after)[1]
                  for k, started in enumerate(self.small)]
        return res, pieces


def _local_step(x, target, comm, small, n_seq):
    def vec(a):
        return a.reshape(1, -1)

    saved = []
    h = x
    for l in range(DEPTH):
        s = {}
        s["up1"], s["down1"] = comm.weights(l, "ffn1", None if l == 0 else h)
        h1, h1b, s["z1"], s["gu1"], s["x0b"] = _ffn_fwd(h, s["up1"], s["down1"], vec(small["ln1_g"][l]), vec(small["ln1_b"][l]), h)
        s["w_in"], s["w_out"], s["up2"], s["down2"] = comm.weights(l, "rest", s["z1"])
        s["x1b"] = h1b
        conv, qkv, sgu, f = _in_proj(h1, s["w_in"])
        cw = _pad_rows(comm.conv_w(h1)[l], 8)
        bf = jnp.pad(small["fox_b_f"][l], (0, 128 - N_HEADS)).reshape(1, 128)
        b_mat = jnp.repeat(small["sgu_b_s"][l].T, D_SGU // N_SGU_GROUPS, axis=1)
        mid_params = (cw, bf, vec(small["sgu_ln_g"][l]), vec(small["sgu_ln_b"][l]), small["sgu_w_s"][l], b_mat)
        cat, cum_t = _mix_mid_fwd(conv, sgu, f, *mid_params, n_seq)
        cat, lse = _fox_fwd(qkv, cum_t, cat, n_seq)
        h2, h2b, s["z2"] = _mix_out_fwd(cat, h1, s["w_out"], vec(small["ln2_g"][l]), vec(small["ln2_b"][l]))
        s.update(conv=conv, qkv=qkv, sgu=sgu, f=f, mid_params=mid_params, cat=cat, cum_t=cum_t, lse=lse, x2b=h2b)
        token = comm.pass_on(l + 1, "ffn1", s["z2"]) if l + 1 < DEPTH else h2
        ln3 = (vec(small["ln3_g"][l]), vec(small["ln3_b"][l]))
        if l + 1 < DEPTH:
            h, _, s["z3"], s["gu2"], _ = _ffn_fwd(h2, s["up2"], s["down2"], *ln3, token)
        else:
            dh, loss, s["z3"], s["gu2"] = _ffn_fwd(h2, s["up2"], s["down2"], *ln3, token, target)
        saved.append(s)

    late_rows = None
    token = loss
    pending = None
    for l in reversed(range(DEPTH)):
        s = saved[l]
        sg = {}
        dh, dy, a, dgu, sg["ln3_g"], sg["ln3_b"] = _ffn_bwd(dh, s["z3"], s["gu2"], s["up2"], s["down2"], vec(small["ln3_g"][l]), token)
        if pending is not None:
            token = comm.advance(pending, dh)
        g_up2 = _matmul_tn(dgu.reshape(N_DEV, -1, FFN_BLK), s["x2b"][None], token)
        g_down2 = _matmul_tn(a, dy[None], token).reshape(N_DEV, FFN_BLK // 2, D_MODEL)
        dz, dzb, dya, dyb, dyc, sg["ln2_g"], sg["ln2_b"] = _mix_out_bwd(dh, s["z2"], s["w_out"], vec(small["ln2_g"][l]))
        g_out = _matmul_tn(s["cat"][None], dzb[None], token).reshape(N_DEV, D_MODEL // N_DEV, D_MODEL)
        dq, dk, dv, drow, dcol = _fox_bwd(s["qkv"], s["cum_t"], s["cat"], s["lse"], dyb, n_seq)
        dconv, dcw = _conv_bwd(s["conv"], dya, s["mid_params"][0], n_seq)
        dsgu, df, dbf, dlg, dlb, dws, dbs = _sgu_gate_bwd(s["sgu"], s["f"], dyc, drow, dcol, *s["mid_params"][1:], n_seq)
        sg.update(conv_w=dcw[:3], fox_b_f=dbf[0, :N_HEADS], sgu_ln_g=dlg[0], sgu_ln_b=dlb[0], sgu_w_s=dws,
                  sgu_b_s=dbs[:, :N_SGU_GROUPS].T)
        dh, dp = _mix_in_bwd(dconv, dq, dk, dv, dsgu, df, dz, s["w_in"])
        g_in = _w_in_to_blocks(_matmul_tn(s["x1b"][None], dp[None], token, tk=1024)[0])
        first = [("ffn2_w_up", l, g_up2), ("ffn2_w_down", l, g_down2), ("mix_w_out", l, g_out), ("mix_w_in", l, g_in)]
        for name in ("ln2_g", "ln2_b", "ln3_g", "ln3_b"):
            sg[name] = sg[name][0]
        if l == 0:
            comm.push((l, "a"), first)
            token = comm.push_small(_pack_rest(sg))
            pending, first = (l, "a"), []
        dh, dy, a, dgu, dg1, db1 = _ffn_bwd(dh, s["z1"], s["gu1"], s["up1"], s["down1"], vec(small["ln1_g"][l]), token)
        if l == 0:
            token = comm.advance(pending, dh)
        g_up1 = _matmul_tn(dgu.reshape(N_DEV, -1, FFN_BLK), s["x0b"][None], token)
        ln1_rows = jnp.concatenate([dg1.reshape(8, 128), db1.reshape(8, 128)], axis=0)
        if l == 0:
            token = comm.push((l, "b"), [("ffn1_w_up", l, g_up1)])
            g_down1 = _matmul_tn(a, dy[None], token).reshape(N_DEV, FFN_BLK // 2, D_MODEL)
            token = comm.advance((l, "b"), g_down1)
            token = comm.push((l, "c"), [("ffn1_w_down", l, g_down1)])
            token = comm.advance((l, "c"), token)
            late_rows = ln1_rows
        else:
            g_down1 = _matmul_tn(a, dy[None], token).reshape(N_DEV, FFN_BLK // 2, D_MODEL)
            pending = (l, "b")
            comm.push(pending, first + [("ffn1_w_up", l, g_up1), ("ffn1_w_down", l, g_down1)])
            token = comm.push_small(jnp.concatenate([ln1_rows, _pack_rest(sg)], axis=0))
    return loss, dh, late_rows


def kernel(x, ln1_g, ln1_b, ffn1_w_up, ffn1_w_down, mix_w_in, fox_b_f, conv_w, sgu_ln_g, sgu_ln_b, sgu_w_s, sgu_b_s, mix_w_out, ln2_g, ln2_b, ffn2_w_up, ffn2_w_down, ln3_g, ln3_b, loss_target, m_ln1_g, m_ln1_b, m_ffn1_w_up, m_ffn1_w_down, m_mix_w_in, m_fox_b_f, m_conv_w, m_sgu_ln_g, m_sgu_ln_b, m_sgu_w_s, m_sgu_b_s, m_mix_w_out, m_ln2_g, m_ln2_b, m_ffn2_w_up, m_ffn2_w_down, m_ln3_g, m_ln3_b, v_ln1_g, v_ln1_b, v_ffn1_w_up, v_ffn1_w_down, v_mix_w_in, v_fox_b_f, v_conv_w, v_sgu_ln_g, v_sgu_ln_b, v_sgu_w_s, v_sgu_b_s, v_mix_w_out, v_ln2_g, v_ln2_b, v_ffn2_w_up, v_ffn2_w_down, v_ln3_g, v_ln3_b):
    w = dict(ln1_g=ln1_g, ln1_b=ln1_b, ffn1_w_up=ffn1_w_up, ffn1_w_down=ffn1_w_down, mix_w_in=mix_w_in, fox_b_f=fox_b_f,
             conv_w=conv_w, sgu_ln_g=sgu_ln_g, sgu_ln_b=sgu_ln_b, sgu_w_s=sgu_w_s, sgu_b_s=sgu_b_s, mix_w_out=mix_w_out,
             ln2_g=ln2_g, ln2_b=ln2_b, ffn2_w_up=ffn2_w_up, ffn2_w_down=ffn2_w_down, ln3_g=ln3_g, ln3_b=ln3_b)
    m = dict(ln1_g=m_ln1_g, ln1_b=m_ln1_b, ffn1_w_up=m_ffn1_w_up, ffn1_w_down=m_ffn1_w_down, mix_w_in=m_mix_w_in,
             fox_b_f=m_fox_b_f, conv_w=m_conv_w, sgu_ln_g=m_sgu_ln_g, sgu_ln_b=m_sgu_ln_b, sgu_w_s=m_sgu_w_s,
             sgu_b_s=m_sgu_b_s, mix_w_out=m_mix_w_out, ln2_g=m_ln2_g, ln2_b=m_ln2_b, ffn2_w_up=m_ffn2_w_up,
             ffn2_w_down=m_ffn2_w_down, ln3_g=m_ln3_g, ln3_b=m_ln3_b)
    v = dict(ln1_g=v_ln1_g, ln1_b=v_ln1_b, ffn1_w_up=v_ffn1_w_up, ffn1_w_down=v_ffn1_w_down, mix_w_in=v_mix_w_in,
             fox_b_f=v_fox_b_f, conv_w=v_conv_w, sgu_ln_g=v_sgu_ln_g, sgu_ln_b=v_sgu_ln_b, sgu_w_s=v_sgu_w_s,
             sgu_b_s=v_sgu_b_s, mix_w_out=v_mix_w_out, ln2_g=v_ln2_g, ln2_b=v_ln2_b, ffn2_w_up=v_ffn2_w_up,
             ffn2_w_down=v_ffn2_w_down, ln3_g=v_ln3_g, ln3_b=v_ln3_b)

    mx, my, mc = lax.axis_index("x"), lax.axis_index("y"), lax.axis_index("c")
    me = 4 * mx + 2 * my + mc
    n_seq, seq, _ = x.shape
    t_tok = n_seq * seq
    for name in UP_NAMES:
        for t in (w, m, v):
            t[name] = jnp.transpose(t[name], (0, 2, 1))

    comm = _Overlap(w, x, me, jnp.stack([mc, 2 * mx + my]).astype(jnp.int32))
    small = {name: w[name] for name in SMALL_NAMES}

    loss_dev, grad_x, late_rows = _local_step(
        x.reshape(t_tok, D_MODEL), loss_target.reshape(t_tok, D_MODEL), comm, small, n_seq)
    loss = lax.psum(loss_dev[0, 0], ("x", "y", "c"))
    out, pieces = comm.finish(w, m, v)
    for name in UP_NAMES:
        out[name] = [jnp.transpose(a, (0, 2, 1)) for a in out[name]]

    pieces.append(_allgather_small(late_rows))
    spans = [(l, 0, LN1_ROWS + REST_ROWS) for l in reversed(range(1, DEPTH))] + [(0, LN1_ROWS, LN1_ROWS + REST_ROWS), (0, 0, LN1_ROWS)]

    def widen(a):
        return lax.dynamic_update_slice(jnp.zeros((3, D_CONV), F32), a, (0, me * (D_CONV // N_DEV)))

    packed = [[_pack_layer({**{name: t[name][l] for name in SMALL_NAMES}, "conv_w": widen(t["conv_w"][l])}) for l in range(DEPTH)]
              for t in (w, m, v)]
    rows_out = {}
    for (l, lo, hi), gathered_piece in zip(spans, pieces):
        rows_out[(l, lo)] = _adamw_small(gathered_piece, *[packed[t][l][lo:hi] for t in range(3)])
    per_layer = []
    for l in range(DEPTH):
        parts = sorted(lo for (ll, lo) in rows_out if ll == l)
        per_layer.append([_unpack_layer(jnp.concatenate([rows_out[(l, lo)][k] for lo in parts], axis=0)) for k in range(4)])
    for name in SMALL_NAMES:
        out[name] = [jnp.stack([per_layer[l][k][name] for l in range(DEPTH)]) for k in range(4)]
    lo_col = me * (D_CONV // N_DEV)
    out["conv_w"] = [jnp.stack([lax.dynamic_slice(per_layer[l][k]["conv_w"], (0, lo_col), (3, D_CONV // N_DEV)) for l in range(DEPTH)])
                     for k in range(4)]

    return (loss, grad_x.reshape(x.shape), *[out[name][0] for name in WEIGHT_ORDER], *[out[name][1] for name in WEIGHT_ORDER],
            *[out[name][2] for name in WEIGHT_ORDER], *[out[name][3] for name in WEIGHT_ORDER])
```

```python
import functools

import jax
import jax.numpy as jnp
from jax import lax
from jax.experimental import pallas as pl
from jax.experimental.pallas import tpu as pltpu

F32 = jnp.float32
BF16 = jnp.bfloat16
MESH = pl.DeviceIdType.MESH

N_DEV = 8
DEPTH = 2
D_MODEL = 1024
D_FF = 2816
FFN_BLK = 2 * D_FF // N_DEV
MXU_TILE_V7X = 256
FFN_CHUNKS = tuple((lo, min(lo + 3 * MXU_TILE_V7X, D_FF)) for lo in range(0, D_FF, 3 * MXU_TILE_V7X))
DW_ROWS = D_FF // 2
D_CONV = 256
D_FOX = 512
N_HEADS = 8
D_SGU = 256
N_SGU_GROUPS = 4
SGU_CHUNK = 128
D_IN = 3 * D_CONV + 3 * D_FOX + N_HEADS + 2 * D_SGU
D_IN_SHARD = D_IN // N_DEV
COL_CONV, COL_QKV, COL_SGU, COL_F = 0, 768, 2304, 2816
D_IN_PAD = 2944
F_ORIG = 3 * D_CONV + 3 * D_FOX
ALPHA = (2 * DEPTH) ** 0.25
LN_EPS = 1e-5
ATT_SCALE = 0.125
ATT_BLK = 512
ATT_PAIRS = 2
NEG = -1e30

ADAM_LR, ADAM_B1, ADAM_B2, ADAM_EPS, ADAM_WD, ADAM_STEP = 0.001, 0.9, 0.999, 1e-08, 0.01, 10

VMEM_BYTES_V7X = 64 * 1024 * 1024
HIGHEST = lax.Precision.HIGHEST


def _params(vmem_mb, sem=None):
    assert vmem_mb * 1024 * 1024 < VMEM_BYTES_V7X
    kw = dict(vmem_limit_bytes=vmem_mb * 1024 * 1024)
    if sem is not None:
        kw["dimension_semantics"] = sem
    return pltpu.CompilerParams(**kw)


def _dot(a, b, precision=None):
    return lax.dot_general(a, b, (((1,), (0,)), ((), ())), preferred_element_type=F32, precision=precision)


def _dot_nt(a, b):
    return lax.dot_general(a, b, (((1,), (1,)), ((), ())), preferred_element_type=F32)


def _dot_tn(a, b):
    return lax.dot_general(a, b, (((0,), (0,)), ((), ())), preferred_element_type=F32)


def _ln_stats(z):
    mu = jnp.mean(z, axis=-1, keepdims=True)
    zc = z - mu
    var = jnp.mean(zc * zc, axis=-1, keepdims=True)
    rstd = lax.rsqrt(var + LN_EPS)
    return zc * rstd, rstd


def _ln_bwd(dy, xhat, rstd, g):
    dxh = dy * g
    m1 = jnp.mean(dxh, axis=-1, keepdims=True)
    m2 = jnp.mean(dxh * xhat, axis=-1, keepdims=True)
    return rstd * (dxh - m1 - xhat * m2)


_GELU_C = 0.7978845608028654


def _gelu(x):
    return 0.5 * x * (1.0 + jnp.tanh(_GELU_C * (x + 0.044715 * x * x * x)))


def _gelu_with_grad(x):
    t = jnp.tanh(_GELU_C * (x + 0.044715 * x * x * x))
    return 0.5 * x * (1.0 + t), 0.5 * (1.0 + t) + 0.5 * x * (1.0 - t * t) * _GELU_C * (1.0 + 3 * 0.044715 * x * x)


def _hbm(shape, dtype):
    n = 1
    for d in shape:
        n *= d
    if n * jnp.dtype(dtype).itemsize >= 1024 * 1024:
        return pltpu.HBM(tuple(shape), dtype)
    return jax.ShapeDtypeStruct(tuple(shape), dtype)


def _vspec():
    return pl.BlockSpec(memory_space=pltpu.VMEM)


def _anyspec():
    return pl.BlockSpec(memory_space=pl.ANY)


def _mesh_pos():
    return lax.axis_index("x"), lax.axis_index("y"), lax.axis_index("c")


def _other_chips(x, y):
    return [(1 - x, y), (x, 1 - y), (1 - x, 1 - y)]


_HBM_SPEC = pl.BlockSpec(memory_space=pltpu.HBM)
_SEM_SPEC = pl.BlockSpec(memory_space=pltpu.SEMAPHORE)
_DATAFLOW_EFFECT = pltpu.SideEffectType.DATAFLOW_SIDE_EFFECTING


def _remote_copies(plan, refs, send_sems, recv_sems):
    return [pltpu.make_async_remote_copy(src_ref=src, dst_ref=dst, send_sem=send_sems.at[k], recv_sem=recv_sems.at[k],
                                         device_id=to, device_id_type=MESH)
            for k, (src, dst, to) in enumerate(plan(refs, *_mesh_pos()))]


def _exchange_start(name, plan, n_copies, arrays, after):
    n = len(arrays)

    def body(*refs):
        send_sems, recv_sems, token = refs[n + 1], refs[n + 2], refs[-1]
        for cp in _remote_copies(plan, refs[:n], send_sems, recv_sems):
            cp.start()
        token[...] = jnp.zeros_like(token)

    out = pl.pallas_call(
        body, name=name,
        out_shape=(pltpu.SemaphoreType.DMA((n_copies,)), pltpu.SemaphoreType.DMA((n_copies,)),
                   *[pltpu.HBM(a.shape, a.dtype) for a in arrays], _hbm((8, 128), F32)),
        in_specs=[_HBM_SPEC] * n + [_anyspec()],
        out_specs=(_SEM_SPEC, _SEM_SPEC, *[_HBM_SPEC] * n, _vspec()),
        input_output_aliases={i: 2 + i for i in range(n)},
        compiler_params=pltpu.CompilerParams(has_side_effects=_DATAFLOW_EFFECT),
    )(*[pltpu.with_memory_space_constraint(a, pltpu.HBM) for a in arrays], after)
    return out[0], out[1], list(out[2:2 + n]), out[-1]


def _exchange_wait(name, plan, n_copies, started, after):
    send_sems, recv_sems, arrays, _ = started
    n = len(arrays)

    def body(*refs):
        for cp in _remote_copies(plan, refs[:n], refs[n], refs[n + 1]):
            cp.wait_send()
            cp.wait_recv()

    out = pl.pallas_call(
        body, name=name,
        out_shape=tuple(pltpu.HBM(a.shape, a.dtype) for a in arrays),
        in_specs=[_HBM_SPEC] * n + [_SEM_SPEC, _SEM_SPEC, _anyspec()], out_specs=tuple([_HBM_SPEC] * n),
        input_output_aliases={i: i for i in range(n)},
        compiler_params=pltpu.CompilerParams(has_side_effects=_DATAFLOW_EFFECT),
    )(*arrays, send_sems, recv_sems, after)
    return list(out)


def _gather_plan(m):
    def plan(refs, x, y, c):
        me = 4 * x + 2 * y + c
        return [(refs[i], refs[m + i].at[me], (*chip, c)) for i in range(m) for chip in _other_chips(x, y)]
    return plan


def _pass_on_plan(m):
    def plan(refs, x, y, c):
        out = []
        for i in range(m):
            out.append((refs[i], refs[m + i].at[4 * x + 2 * y + c], (x, y, 1 - c)))
            for cx, cy in _other_chips(x, y):
                block = refs[m + i].at[4 * cx + 2 * cy + c]
                out.append((block, block, (x, y, 1 - c)))
        return out
    return plan


def _peers_plan():
    def plan(refs, x, y, c):
        rel = [(dx, dy, dc) for dx in (0, 1) for dy in (0, 1) for dc in (0, 1)][1:]
        return [(refs[0], refs[1].at[4 * x + 2 * y + c], (x ^ dx, y ^ dy, c ^ dc)) for dx, dy, dc in rel]
    return plan


def _allgather_small(v):
    rows = v.shape[0]

    def body(v_ref, out_ref, send_sems, recv_sems):
        x, y, c = _mesh_pos()
        me = 4 * x + 2 * y + c
        out_ref[me] = v_ref[...]
        rel = [(dx, dy, dc) for dx in (0, 1) for dy in (0, 1) for dc in (0, 1)][1:]
        copies = []
        for k, (dx, dy, dc) in enumerate(rel):
            to = (x ^ dx, y ^ dy, c ^ dc)
            copies.append(pltpu.make_async_remote_copy(
                src_ref=v_ref, dst_ref=out_ref.at[me], send_sem=send_sems.at[k], recv_sem=recv_sems.at[k],
                device_id=to, device_id_type=MESH))
        for cp in copies:
            cp.start()
        for k, (dx, dy, dc) in enumerate(rel):
            src_blk = 4 * (x ^ dx) + 2 * (y ^ dy) + (c ^ dc)
            pltpu.make_async_remote_copy(
                src_ref=v_ref, dst_ref=out_ref.at[src_blk], send_sem=send_sems.at[k], recv_sem=recv_sems.at[k],
                device_id=(x, y, c), device_id_type=MESH).wait_recv()
        for cp in copies:
            cp.wait_send()

    return pl.pallas_call(
        body, name="allgather_small",
        out_shape=jax.ShapeDtypeStruct((N_DEV, rows, 128), v.dtype),
        in_specs=[_vspec()], out_specs=_vspec(),
        scratch_shapes=[pltpu.SemaphoreType.DMA((7,)), pltpu.SemaphoreType.DMA((7,))],
        compiler_params=_params(24),
    )(v)


def _sibling_plan(n):
    def plan(refs, x, y, c):
        return [(refs[a].at[2 * q + (1 - c)], refs[n + a].at[q], (x, y, 1 - c)) for a in range(n) for q in range(4)]
    return plan


def _chip_plan(n):
    def plan(refs, x, y, c):
        return [(refs[a].at[2 * cx + cy], refs[n + a].at[j], (cx, cy, c))
                for a in range(n) for j, (cx, cy) in enumerate(_other_chips(x, y))]
    return plan


def _row_tile(rows, cols, budget_bytes=2 * 1024 * 1024):
    best = 8
    for t in range(8, rows + 1, 8):
        if rows % t == 0 and t * cols * 4 <= budget_bytes:
            best = t
    return best


def _chip_partial(g, recv, where):
    _, rows, cols = g.shape
    tr = _row_tile(rows, cols)

    def body(where_ref, g_ref, r_ref, own_ref, o16_ref):
        s = g_ref[...] + r_ref[...]
        o16_ref[...] = s.astype(BF16)

        @pl.when(pl.program_id(1) == where_ref[1])
        def _():
            own_ref[...] = s

    blk = (None, tr, cols)
    return pl.pallas_call(
        body, name="rs_chip_partial",
        grid_spec=pltpu.PrefetchScalarGridSpec(
            num_scalar_prefetch=1, grid=(rows // tr, 4),
            in_specs=[pl.BlockSpec(blk, lambda i, q, w: (2 * q + w[0], i, 0)),
                      pl.BlockSpec(blk, lambda i, q, w: (q, i, 0))],
            out_specs=[pl.BlockSpec((tr, cols), lambda i, q, w: (i, 0)), pl.BlockSpec(blk, lambda i, q, w: (q, i, 0))]),
        out_shape=[_hbm((rows, cols), F32), _hbm((4, rows, cols), BF16)],
        compiler_params=_params(32),
    )(where, g, recv)


def _adam_math(w, g, m, v):
    m = ADAM_B1 * m + (1.0 - ADAM_B1) * g
    v = ADAM_B2 * v + (1.0 - ADAM_B2) * (g * g)
    m_hat = m / (1.0 - ADAM_B1 ** ADAM_STEP)
    v_hat = v / (1.0 - ADAM_B2 ** ADAM_STEP)
    delta = -ADAM_LR * (m_hat / (jnp.sqrt(v_hat) + ADAM_EPS) + ADAM_WD * w)
    return delta, m, v


def _adamw_shard(own32, recv16, w, m, v, layer, earlier):
    depth, rows, cols = w.shape
    tr = _row_tile(rows, cols, 1024 * 1024)
    n_prev = 0 if earlier is None else 4

    def body(p_ref, r_ref, w_ref, m_ref, v_ref, *rest):
        g_out, d_out, m_out, v_out = rest[n_prev:]
        g = p_ref[...] + r_ref[0].astype(F32) + r_ref[1].astype(F32) + r_ref[2].astype(F32)
        d, mn, vn = _adam_math(w_ref[...], g, m_ref[...], v_ref[...])
        g_out[...] = g
        d_out[...] = d
        m_out[...] = mn
        v_out[...] = vn

    mine = pl.BlockSpec((None, tr, cols), lambda i: (layer, i, 0))
    return pl.pallas_call(
        body, name="adamw_shard", grid=(rows // tr,),
        in_specs=[pl.BlockSpec((tr, cols), lambda i: (i, 0)), pl.BlockSpec((3, tr, cols), lambda i: (0, i, 0)),
                  mine, mine, mine] + [_anyspec()] * n_prev,
        out_specs=[mine] * 4,
        out_shape=[_hbm((depth, rows, cols), F32)] * 4,
        input_output_aliases={5 + k: k for k in range(n_prev)},
        compiler_params=_params(32),
    )(own32, recv16, *[pltpu.with_memory_space_constraint(t, pltpu.HBM) for t in (w, m, v)],
      *([] if earlier is None else earlier))


def _adamw_small(gathered, w, m, v):
    rows = w.shape[0]

    def body(a_ref, w_ref, m_ref, v_ref, g_out, d_out, m_out, v_out):
        g = a_ref[0]
        for d in range(1, N_DEV):
            g = g + a_ref[d]
        dl, mn, vn = _adam_math(w_ref[...], g, m_ref[...], v_ref[...])
        g_out[...] = g
        d_out[...] = dl
        m_out[...] = mn
        v_out[...] = vn

    return pl.pallas_call(
        body, name="adamw_small",
        in_specs=[_vspec()] * 4, out_specs=[_vspec()] * 4,
        out_shape=[_hbm((rows, 128), F32)] * 4,
        compiler_params=_params(32),
    )(gathered, w, m, v)


def _load_weights_once(pairs, sems):
    @pl.when(pl.program_id(0) == 0)
    def _():
        cps = [pltpu.make_async_copy(src, dst, sems.at[i]) for i, (src, dst) in enumerate(pairs)]
        for cp in cps:
            cp.start()
        for cp in cps:
            cp.wait()


def _ffn_fwd(x, wup, wd, ln_g, ln_b, after, target=None, tm=512):
    t_tok = x.shape[0]
    last = target is not None

    def body(x_ref, g_ref, b_ref, wup_hbm, wd_hbm, _after, *rest):
        if last:
            t_ref, dxn_ref, loss_ref, z_ref, gu_ref, wup_v, wd_v, sems = rest
        else:
            xn_ref, xnb_ref, z_ref, gu_ref, xb_ref, wup_v, wd_v, sems = rest
        _load_weights_once([(wup_hbm, wup_v), (wd_hbm, wd_v)], sems)
        xb = x_ref[...].astype(BF16)
        if not last:
            xb_ref[...] = xb
        y = None
        for lo, hi in FFN_CHUNKS:
            g = _dot_nt(xb, wup_v[0, lo:hi])
            u = _dot_nt(xb, wup_v[1, lo:hi])
            gu_ref[0, :, lo:hi] = g.astype(BF16)
            gu_ref[1, :, lo:hi] = u.astype(BF16)
            a = (g * jax.nn.sigmoid(g) * u).astype(BF16)
            part = _dot(a, wd_v[lo:hi])
            y = part if y is None else y + part
        z = ALPHA * x_ref[...] + 0.5 * y
        xhat, _ = _ln_stats(z)
        xn = xhat * g_ref[...] + b_ref[...]
        z_ref[...] = z
        if last:
            err = xn - t_ref[...]
            dxn_ref[...] = err * (1.0 / D_MODEL)
            part = jnp.sum(jnp.sum(err * err, axis=1, keepdims=True), axis=0, keepdims=True) * (0.5 / D_MODEL)

            @pl.when(pl.program_id(0) == 0)
            def _():
                loss_ref[...] = jnp.zeros_like(loss_ref)

            loss_ref[...] += part
        else:
            xn_ref[...] = xn
            xnb_ref[...] = xn.astype(BF16)

    tok = pl.BlockSpec((tm, D_MODEL), lambda i: (i, 0))
    vec = pl.BlockSpec((1, D_MODEL), lambda i: (0, 0))
    gu_spec = pl.BlockSpec((2, tm, D_FF), lambda i: (0, i, 0))
    gu_shape = _hbm((2, t_tok, D_FF), BF16)
    f32_tok, bf16_tok = _hbm((t_tok, D_MODEL), F32), _hbm((t_tok, D_MODEL), BF16)
    if last:
        extra_in, extra_spec = [target], [tok]
        out_specs = [tok, pl.BlockSpec((1, 128), lambda i: (0, 0)), tok, gu_spec]
        out_shape = [f32_tok, _hbm((1, 128), F32), f32_tok, gu_shape]
    else:
        extra_in, extra_spec = [], []
        out_specs = [tok, tok, tok, gu_spec, tok]
        out_shape = [f32_tok, bf16_tok, f32_tok, gu_shape, bf16_tok]
    return pl.pallas_call(
        body, name="ffn_fwd_loss" if last else "ffn_fwd", grid=(t_tok // tm,),
        in_specs=[tok, vec, vec, _anyspec(), _anyspec(), _anyspec()] + extra_spec,
        out_specs=out_specs, out_shape=out_shape,
        scratch_shapes=[pltpu.VMEM((2, D_FF, D_MODEL), BF16), pltpu.VMEM((D_FF, D_MODEL), BF16),
                        pltpu.SemaphoreType.DMA((2,))],
        compiler_params=_params(62, ("arbitrary",)),
    )(x, ln_g, ln_b, wup, wd, after, *extra_in)


def _ffn_bwd(dxn, z, gu, wup, wd, ln_g, after, tm=256):
    t_tok = dxn.shape[0]

    def body(dxn_ref, z_ref, gu_ref, g_ref, wup_hbm, wd_hbm, _after,
             dx_ref, dy_ref, a_ref, dgu_ref, dg_ref, db_ref, wup_v, wd_v, sems):
        i = pl.program_id(0)
        _load_weights_once([(wup_hbm, wup_v), (wd_hbm, wd_v)], sems)
        dxn_t = dxn_ref[...]
        xhat, rstd = _ln_stats(z_ref[...])
        pg = jnp.sum(dxn_t * xhat, axis=0, keepdims=True)
        pb = jnp.sum(dxn_t, axis=0, keepdims=True)

        @pl.when(i == 0)
        def _():
            dg_ref[...] = pg
            db_ref[...] = pb

        @pl.when(i > 0)
        def _():
            dg_ref[...] += pg
            db_ref[...] += pb

        dz = _ln_bwd(dxn_t, xhat, rstd, g_ref[...])
        dy = (0.5 * dz).astype(BF16)
        dy_ref[...] = dy
        dx = ALPHA * dz
        for lo, hi in FFN_CHUNKS:
            da = _dot_nt(dy, wd_v[lo:hi])
            g = gu_ref[0, :, lo:hi].astype(F32)
            u = gu_ref[1, :, lo:hi].astype(F32)
            sig = jax.nn.sigmoid(g)
            silu = g * sig
            a_ref[:, lo:hi] = (silu * u).astype(BF16)
            dg = (da * u * (sig * (1.0 + g * (1.0 - sig)))).astype(BF16)
            du = (da * silu).astype(BF16)
            dgu_ref[0, :, lo:hi] = dg
            dgu_ref[1, :, lo:hi] = du
            dx = dx + _dot(dg, wup_v[0, lo:hi]) + _dot(du, wup_v[1, lo:hi])
        dx_ref[...] = dx

    tok = pl.BlockSpec((tm, D_MODEL), lambda i: (i, 0))
    vec = pl.BlockSpec((1, D_MODEL), lambda i: (0, 0))
    gu_spec = pl.BlockSpec((2, tm, D_FF), lambda i: (0, i, 0))
    return pl.pallas_call(
        body, name="ffn_bwd", grid=(t_tok // tm,),
        in_specs=[tok, tok, gu_spec, vec, _anyspec(), _anyspec(), _anyspec()],
        out_specs=[tok, tok, pl.BlockSpec((tm, D_FF), lambda i: (i, 0)), gu_spec, vec, vec],
        out_shape=[_hbm((t_tok, D_MODEL), F32), _hbm((t_tok, D_MODEL), BF16),
                   _hbm((t_tok, D_FF), BF16), _hbm((2, t_tok, D_FF), BF16),
                   _hbm((1, D_MODEL), F32), _hbm((1, D_MODEL), F32)],
        scratch_shapes=[pltpu.VMEM((2, D_FF, D_MODEL), BF16), pltpu.VMEM((D_FF, D_MODEL), BF16),
                        pltpu.SemaphoreType.DMA((2,))],
        compiler_params=_params(60, ("arbitrary",)),
    )(dxn, z, gu, ln_g, wup, wd, after)


def _matmul_tn(a, b, after, tk=4096, bm=None):
    ga, t_tok, m = a.shape
    gb, _, n = b.shape
    groups = max(ga, gb)
    tk = min(tk, t_tok)
    bm = m if bm is None else bm

    def body(a_ref, b_ref, _after, o_ref):
        p = _dot_tn(a_ref[...].astype(BF16), b_ref[...].astype(BF16))

        @pl.when(pl.program_id(2) == 0)
        def _():
            o_ref[...] = p

        @pl.when(pl.program_id(2) > 0)
        def _():
            o_ref[...] += p

    return pl.pallas_call(
        body, name=f"matmul_tn_{m}x{n}", grid=(groups, m // bm, t_tok // tk),
        in_specs=[pl.BlockSpec((None, tk, bm), (lambda g, i, t: (g, t, i)) if ga > 1 else (lambda g, i, t: (0, t, i))),
                  pl.BlockSpec((None, tk, n), (lambda g, i, t: (g, t, 0)) if gb > 1 else (lambda g, i, t: (0, t, 0))),
                  _anyspec()],
        out_specs=pl.BlockSpec((None, bm, n), lambda g, i, t: (g, i, 0)),
        out_shape=_hbm((groups, m, n), F32),
        compiler_params=_params(56, ("arbitrary", "arbitrary", "arbitrary")),
    )(a, b, after)


def _in_proj(x, w_in, tm=512):
    t_tok = x.shape[0]

    def body(x_ref, w_ref, conv_ref, qkv_ref, sgu_ref, f_ref):
        xb = x_ref[...].astype(BF16)
        conv_ref[...] = _dot(xb, w_ref[:, COL_CONV:COL_QKV])
        qkv_ref[...] = _dot(xb, w_ref[:, COL_QKV:COL_SGU]).astype(BF16)
        sgu_ref[...] = _dot(xb, w_ref[:, COL_SGU:COL_F])
        f_ref[...] = _dot(xb, w_ref[:, COL_F:D_IN_PAD])

    def tok(n):
        return pl.BlockSpec((tm, n), lambda i: (i, 0))

    return pl.pallas_call(
        body, name="mix_in_proj", grid=(t_tok // tm,),
        in_specs=[tok(D_MODEL), pl.BlockSpec((D_MODEL, D_IN_PAD), lambda i: (0, 0))],
        out_specs=[tok(768), tok(1536), tok(512), tok(128)],
        out_shape=[_hbm((t_tok, 768), F32), _hbm((t_tok, 1536), BF16),
                   _hbm((t_tok, 512), F32), _hbm((t_tok, 128), F32)],
        compiler_params=_params(48, ("arbitrary",)),
    )(x, w_in)


def _shift_down(a, k):
    row = lax.broadcasted_iota(jnp.int32, a.shape, 0)
    return jnp.where(row >= k, pltpu.roll(a, k, 0), 0.0)


def _shift_up(a, k):
    rows = a.shape[0]
    row = lax.broadcasted_iota(jnp.int32, a.shape, 0)
    return jnp.where(row < rows - k, pltpu.roll(a, rows - k, 0), 0.0)


def _tril(n):
    return lax.broadcasted_iota(jnp.int32, (n, n), 0) >= lax.broadcasted_iota(jnp.int32, (n, n), 1)


def _sgu_group_of_lane():
    return lax.broadcasted_iota(jnp.int32, (1, D_SGU), 1) // (D_SGU // N_SGU_GROUPS)


def _log_sigmoid(x):
    return jnp.minimum(x, 0.0) - jnp.log1p(jnp.exp(-jnp.abs(x)))


def _mix_mid_fwd(conv, sgu, f, conv_w, b_f, sgu_g, sgu_b, w_s, b_mat, n_seq):
    t_tok = conv.shape[0]
    seq = t_tok // n_seq
    n_chunk = seq // SGU_CHUNK
    per_blk = ATT_BLK // SGU_CHUNK

    def body(conv_ref, sgu_ref, f_ref, cw_ref, bf_ref, lg_ref, lb_ref, ws_ref, bm_ref, cat_ref, cum_ref):
        z = conv_ref[:, 256:512] * conv_ref[:, 512:768]
        y = cw_ref[0:1, :] * _shift_down(z, 2) + cw_ref[1:2, :] * _shift_down(z, 1) + cw_ref[2:3, :] * z
        cat_ref[:, 0:D_CONV] = (conv_ref[:, 0:256] * y).astype(BF16)
        cat_ref[:, D_CONV:D_CONV + D_FOX] = jnp.zeros((seq, D_FOX), BF16)

        tril = _tril(SGU_CHUNK)
        grp = _sgu_group_of_lane()
        wc = [jnp.where(tril, ws_ref[g], 0.0).astype(BF16) for g in range(N_SGU_GROUPS)]
        tri_f = tril.astype(F32)
        carry = jnp.zeros((1, 128), F32)
        for n in range(n_chunk):
            rows = pl.ds(n * SGU_CHUNK, SGU_CHUNK)
            u = _gelu(sgu_ref[rows, 0:256])
            vhat, _ = _ln_stats(_gelu(sgu_ref[rows, 256:512]))
            vn = (vhat * lg_ref[...] + lb_ref[...]).astype(BF16)
            mixed = bm_ref[...]
            for g in range(N_SGU_GROUPS):
                mixed = mixed + jnp.where(grp == g, _dot(wc[g], vn), 0.0)
            cat_ref[rows, D_CONV + D_FOX:D_MODEL] = (u * mixed).astype(BF16)

            log_f = _log_sigmoid(f_ref[rows, :] + bf_ref[...])
            cs = _dot(tri_f, log_f, HIGHEST) + carry
            carry = cs[SGU_CHUNK - 1:SGU_CHUNK, :]
            cs_t = cs.T
            lanes = pl.ds((n % per_blk) * SGU_CHUNK, SGU_CHUNK)
            for h in range(N_HEADS):
                cum_ref[h, n // per_blk, :, lanes] = cs_t[h:h + 1, :]

    def seq_blk(n):
        return pl.BlockSpec((seq, n), lambda b: (b, 0))

    def full(shape):
        return pl.BlockSpec(shape, lambda b: (0,) * len(shape))

    return pl.pallas_call(
        body, name="mix_mid_fwd", grid=(n_seq,),
        in_specs=[seq_blk(768), seq_blk(512), seq_blk(128), full((8, 256)), full((1, 128)), full((1, 256)),
                  full((1, 256)), full((4, 128, 128)), full((128, 256))],
        out_specs=[seq_blk(D_MODEL), pl.BlockSpec((N_HEADS, seq // ATT_BLK, 1, ATT_BLK), lambda b: (b, 0, 0, 0))],
        out_shape=[_hbm((t_tok, D_MODEL), BF16), _hbm((n_seq * N_HEADS, seq // ATT_BLK, 1, ATT_BLK), F32)],
        compiler_params=_params(48, ("arbitrary",)),
    )(conv, sgu, f, conv_w, b_f, sgu_g, sgu_b, w_s, b_mat)


def _head_masks():
    lane = lax.broadcasted_iota(jnp.int32, (1, 128), 1)
    return lane < 64, lane


def _fox_fwd(qkv, cum_t, cat, n_seq):
    t_tok = qkv.shape[0]
    seq = t_tok // n_seq
    nq = seq // ATT_BLK
    blk = ATT_BLK

    def body(q_ref, k_ref, v_ref, c_ref, _cat, o_ref, lse_ref):
        qi = pl.program_id(2)
        first, _ = _head_masks()
        one = jnp.ones((1, 128), BF16)
        qh = []
        for hp in range(ATT_PAIRS):
            qs = q_ref[:, 128 * hp:128 * hp + 128] * ATT_SCALE
            zero = jnp.zeros_like(qs)
            qh += [jnp.where(first, qs, zero), jnp.where(first, zero, qs)]

        def step(kb, carry, masked):
            ms, accs = carry
            rows = pl.ds(pl.multiple_of(kb * blk, blk), blk)
            new_m, new_acc = [], []
            for hp in range(ATT_PAIRS):
                k = k_ref[rows, 128 * hp:128 * hp + 128]
                v = v_ref[rows, 128 * hp:128 * hp + 128]
                for h in range(2):
                    i = 2 * hp + h
                    s = _dot_nt(qh[i], k) - c_ref[i, kb]
                    if masked:
                        s = jnp.where(causal, s, NEG)
                    m_new = jnp.maximum(ms[i], jnp.max(s, axis=1, keepdims=True))
                    p = jnp.exp(s - m_new)
                    vh = jnp.where(first, v, one) if h == 0 else jnp.where(first, one, v)
                    new_acc.append(accs[i] * jnp.exp(ms[i] - m_new) + _dot(p.astype(BF16), vh))
                    new_m.append(m_new)
            return tuple(new_m), tuple(new_acc)

        causal = _tril(blk)
        n_heads = 2 * ATT_PAIRS
        col = jnp.full((blk, 1), NEG, F32)
        zacc = jnp.zeros((blk, 128), F32)
        carry = lax.fori_loop(0, qi, lambda kb, cr: step(kb, cr, False), ((col,) * n_heads, (zacc,) * n_heads))
        ms, accs = step(qi, carry, True)
        for hp in range(ATT_PAIRS):
            acc0, acc1 = accs[2 * hp], accs[2 * hp + 1]
            l0 = pltpu.roll(acc0, 64, 1)
            l1 = pltpu.roll(acc1, 64, 1)
            o_ref[:, 128 * hp:128 * hp + 128] = jnp.where(first, acc0 / l0, acc1 / l1).astype(BF16)
            lse_ref[:, 128 * hp:128 * hp + 128] = jnp.where(first, ms[2 * hp] + jnp.log(l0), ms[2 * hp + 1] + jnp.log(l1))

    wide = 128 * ATT_PAIRS
    n_grp = D_FOX // wide
    first_col = D_CONV // wide
    return pl.pallas_call(
        body, name="fox_fwd", grid=(n_seq, n_grp, nq),
        in_specs=[pl.BlockSpec((blk, wide), lambda b, g, qi: (b * nq + qi, g)),
                  pl.BlockSpec((seq, wide), lambda b, g, qi: (b, n_grp + g)),
                  pl.BlockSpec((seq, wide), lambda b, g, qi: (b, 2 * n_grp + g)),
                  pl.BlockSpec((2 * ATT_PAIRS, nq, 1, blk), lambda b, g, qi: (b * n_grp + g, 0, 0, 0)), _anyspec()],
        out_specs=[pl.BlockSpec((blk, wide), lambda b, g, qi: (b * nq + qi, first_col + g)),
                   pl.BlockSpec((blk, wide), lambda b, g, qi: (b * nq + qi, g))],
        out_shape=[_hbm(cat.shape, BF16), _hbm((t_tok, D_FOX), F32)],
        input_output_aliases={4: 0},
        compiler_params=_params(48, ("arbitrary", "arbitrary", "arbitrary")),
    )(qkv, qkv, qkv, cum_t, cat)


def _fox_bwd(qkv, cum_t, cat, lse, d_o, n_seq):
    t_tok = qkv.shape[0]
    seq = t_tok // n_seq
    nk = seq // ATT_BLK
    blk = ATT_BLK

    def body(q_ref, k_ref, v_ref, c_ref, o_ref, lse_ref, do_ref, dq_ref, dk_ref, dv_ref, drow_ref, dcol_ref):
        kb = pl.program_id(2)
        first, lane = _head_masks()
        second = jnp.logical_not(first)
        one = jnp.ones((1, 128), BF16)
        causal = _tril(blk)

        @pl.when(kb == 0)
        def _():
            dq_ref[...] = jnp.zeros_like(dq_ref)
            drow_ref[...] = jnp.zeros_like(drow_ref)

        def step(qi, carry, masked):
            rows = pl.ds(pl.multiple_of(qi * blk, blk), blk)
            dks, dvs = carry
            new_dk, new_dv = [], []
            for hp in range(ATT_PAIRS):
                cols = slice(128 * hp, 128 * hp + 128)
                k = k_ref[:, cols]
                v = v_ref[:, cols]
                ks = k * ATT_SCALE
                zero = jnp.zeros_like(k)
                qs = q_ref[rows, cols] * ATT_SCALE
                d_o = do_ref[rows, cols]
                dd = d_o.astype(F32) * o_ref[rows, cols].astype(F32)
                lse_t = lse_ref[rows, cols]
                dq = []
                for h, mine in enumerate((first, second)):
                    i = 2 * hp + h
                    qh = jnp.where(mine, qs, zero)
                    doh = jnp.where(mine, d_o, zero)
                    delta = jnp.sum(jnp.where(mine, dd, 0.0), axis=1, keepdims=True)
                    lse_h = jnp.sum(jnp.where(lane == 64 * h, lse_t, 0.0), axis=1, keepdims=True)
                    s = _dot_nt(qh, k) - c_ref[i]
                    if masked:
                        s = jnp.where(causal, s, NEG)
                    p = jnp.exp(s - lse_h)
                    ds = (p * (_dot_nt(doh, v) - delta)).astype(BF16)
                    new_dk.append(dks[i] + _dot_tn(ds, jnp.where(mine, qs, one)))
                    new_dv.append(dvs[i] + _dot_tn(p.astype(BF16), doh))
                    dq.append(_dot(ds, jnp.where(mine, ks, one)))
                dq_ref[rows, cols] += jnp.where(first, dq[0], dq[1])
                drow_ref[rows, cols] += jnp.where(first, dq[1], dq[0])
            return tuple(new_dk), tuple(new_dv)

        zt = (jnp.zeros((blk, 128), F32),) * (2 * ATT_PAIRS)
        carry = step(kb, (zt, zt), True)
        dks, dvs = lax.fori_loop(kb + 1, nk, lambda qi, cr: step(qi, cr, False), carry)
        for hp in range(ATT_PAIRS):
            cols = slice(128 * hp, 128 * hp + 128)
            dk_ref[:, cols] = jnp.where(first, dks[2 * hp], dks[2 * hp + 1]).astype(BF16)
            dcol_ref[:, cols] = jnp.where(first, dks[2 * hp + 1], dks[2 * hp])
            dv_ref[:, cols] = (dvs[2 * hp] + dvs[2 * hp + 1]).astype(BF16)

    wide = 128 * ATT_PAIRS
    n_grp = D_FOX // wide

    def seq_spec(col0):
        return pl.BlockSpec((seq, wide), lambda b, g, kb: (b, col0 + g))

    def key_spec(col0):
        return pl.BlockSpec((blk, wide), lambda b, g, kb: (b * nk + kb, col0 + g))

    return pl.pallas_call(
        body, name="fox_bwd", grid=(n_seq, n_grp, nk),
        in_specs=[seq_spec(0), key_spec(n_grp), key_spec(2 * n_grp),
                  pl.BlockSpec((2 * ATT_PAIRS, None, 1, blk), lambda b, g, kb: (b * n_grp + g, kb, 0, 0)),
                  seq_spec(D_CONV // wide), seq_spec(0), seq_spec(0)],
        out_specs=[seq_spec(0), key_spec(0), key_spec(0), seq_spec(0), key_spec(0)],
        out_shape=[_hbm((t_tok, D_FOX), F32), _hbm((t_tok, D_FOX), BF16),
                   _hbm((t_tok, D_FOX), BF16), _hbm((t_tok, D_FOX), F32),
                   _hbm((t_tok, D_FOX), F32)],
        compiler_params=_params(56, ("arbitrary", "arbitrary", "arbitrary")),
    )(qkv, qkv, qkv, cum_t, cat, lse, d_o)


def _mix_out_fwd(cat, x, w_out, ln_g, ln_b, tm=512):
    t_tok = x.shape[0]

    def body(cat_ref, x_ref, w_ref, g_ref, b_ref, xn_ref, xnb_ref, z_ref):
        z = ALPHA * x_ref[...] + _dot(cat_ref[...], w_ref[...])
        xhat, _ = _ln_stats(z)
        xn = xhat * g_ref[...] + b_ref[...]
        z_ref[...] = z
        xn_ref[...] = xn
        xnb_ref[...] = xn.astype(BF16)

    def tok(n):
        return pl.BlockSpec((tm, n), lambda i: (i, 0))

    vec = pl.BlockSpec((1, D_MODEL), lambda i: (0, 0))
    return pl.pallas_call(
        body, name="mix_out_fwd", grid=(t_tok // tm,),
        in_specs=[tok(D_MODEL), tok(D_MODEL), pl.BlockSpec((D_MODEL, D_MODEL), lambda i: (0, 0)), vec, vec],
        out_specs=[tok(D_MODEL)] * 3,
        out_shape=[_hbm((t_tok, D_MODEL), F32), _hbm((t_tok, D_MODEL), BF16),
                   _hbm((t_tok, D_MODEL), F32)],
        compiler_params=_params(40, ("arbitrary",)),
    )(cat, x, w_out, ln_g, ln_b)


def _mix_out_bwd(dxn, z, w_out, ln_g, tm=512):
    t_tok = dxn.shape[0]

    def body(dxn_ref, z_ref, w_ref, g_ref, dz_ref, dzb_ref, dya_ref, dyb_ref, dyc_ref, dg_ref, db_ref):
        i = pl.program_id(0)
        dxn_t = dxn_ref[...]
        xhat, rstd = _ln_stats(z_ref[...])
        pg = jnp.sum(dxn_t * xhat, axis=0, keepdims=True)
        pb = jnp.sum(dxn_t, axis=0, keepdims=True)

        @pl.when(i == 0)
        def _():
            dg_ref[...] = pg
            db_ref[...] = pb

        @pl.when(i > 0)
        def _():
            dg_ref[...] += pg
            db_ref[...] += pb

        dz = _ln_bwd(dxn_t, xhat, rstd, g_ref[...])
        dzb = dz.astype(BF16)
        dz_ref[...] = dz
        dzb_ref[...] = dzb
        dya_ref[...] = _dot_nt(dzb, w_ref[0:256, :])
        dyb_ref[...] = _dot_nt(dzb, w_ref[256:768, :]).astype(BF16)
        dyc_ref[...] = _dot_nt(dzb, w_ref[768:1024, :])

    def tok(n):
        return pl.BlockSpec((tm, n), lambda i: (i, 0))

    vec = pl.BlockSpec((1, D_MODEL), lambda i: (0, 0))
    return pl.pallas_call(
        body, name="mix_out_bwd", grid=(t_tok // tm,),
        in_specs=[tok(D_MODEL), tok(D_MODEL), pl.BlockSpec((D_MODEL, D_MODEL), lambda i: (0, 0)), vec],
        out_specs=[tok(D_MODEL), tok(D_MODEL), tok(256), tok(512), tok(256), vec, vec],
        out_shape=[_hbm((t_tok, D_MODEL), F32), _hbm((t_tok, D_MODEL), BF16),
                   _hbm((t_tok, 256), F32), _hbm((t_tok, 512), BF16),
                   _hbm((t_tok, 256), F32),
                   _hbm((1, D_MODEL), F32), _hbm((1, D_MODEL), F32)],
        compiler_params=_params(40, ("arbitrary",)),
    )(dxn, z, w_out, ln_g)


def _conv_bwd(conv, dya, conv_w, n_seq):
    t_tok = conv.shape[0]
    seq = t_tok // n_seq

    def body(conv_ref, dya_ref, cw_ref, dconv_ref, dcw_ref):
        @pl.when(pl.program_id(0) == 0)
        def _():
            dcw_ref[...] = jnp.zeros_like(dcw_ref)

        z = conv_ref[:, 256:512] * conv_ref[:, 512:768]
        z1 = _shift_down(z, 1)
        z2 = _shift_down(z, 2)
        y = cw_ref[0:1, :] * z2 + cw_ref[1:2, :] * z1 + cw_ref[2:3, :] * z
        dya_t = dya_ref[...]
        dconv_ref[:, 0:256] = (dya_t * y).astype(BF16)
        dy = dya_t * conv_ref[:, 0:256]
        dcw_ref[0:1, :] += jnp.sum(dy * z2, axis=0, keepdims=True)
        dcw_ref[1:2, :] += jnp.sum(dy * z1, axis=0, keepdims=True)
        dcw_ref[2:3, :] += jnp.sum(dy * z, axis=0, keepdims=True)
        dz = cw_ref[2:3, :] * dy + cw_ref[1:2, :] * _shift_up(dy, 1) + cw_ref[0:1, :] * _shift_up(dy, 2)
        dconv_ref[:, 256:512] = (dz * conv_ref[:, 512:768]).astype(BF16)
        dconv_ref[:, 512:768] = (dz * conv_ref[:, 256:512]).astype(BF16)

    def seq_blk(n):
        return pl.BlockSpec((seq, n), lambda b: (b, 0))

    par = pl.BlockSpec((8, 256), lambda b: (0, 0))
    return pl.pallas_call(
        body, name="conv_bwd", grid=(n_seq,),
        in_specs=[seq_blk(768), seq_blk(256), par], out_specs=[seq_blk(768), par],
        out_shape=[_hbm((t_tok, 768), BF16), _hbm((8, 256), F32)],
        compiler_params=_params(56, ("arbitrary",)),
    )(conv, dya, conv_w)


def _sgu_gate_bwd(sgu, f, dyc, drow, dcol, b_f, sgu_g, sgu_b, w_s, b_mat, n_seq):
    t_tok = sgu.shape[0]
    seq = t_tok // n_seq
    n_chunk = seq // SGU_CHUNK

    def body(sgu_ref, f_ref, dyc_ref, drow_ref, dcol_ref, bf_ref, lg_ref, lb_ref, ws_ref, bm_ref,
             dsgu_ref, df_ref, dbf_ref, dlg_ref, dlb_ref, dws_ref, dbs_ref, dbm_acc):
        b = pl.program_id(0)

        @pl.when(b == 0)
        def _():
            for r in (dbf_ref, dlg_ref, dlb_ref, dws_ref, dbm_acc):
                r[...] = jnp.zeros_like(r)

        tril = _tril(SGU_CHUNK)
        grp = _sgu_group_of_lane()
        wc = [jnp.where(tril, ws_ref[g], 0.0).astype(BF16) for g in range(N_SGU_GROUPS)]
        for n in range(n_chunk):
            rows = pl.ds(n * SGU_CHUNK, SGU_CHUNK)
            su = sgu_ref[rows, 0:256]
            sv = sgu_ref[rows, 256:512]
            u, du = _gelu_with_grad(su)
            gv, dgv = _gelu_with_grad(sv)
            vhat, rstd = _ln_stats(gv)
            vn = (vhat * lg_ref[...] + lb_ref[...]).astype(BF16)
            mixed = bm_ref[...]
            for g in range(N_SGU_GROUPS):
                mixed = mixed + jnp.where(grp == g, _dot(wc[g], vn), 0.0)
            dyc_t = dyc_ref[rows, :]
            dsgu_ref[rows, 0:256] = (dyc_t * mixed * du).astype(BF16)
            dmixed = dyc_t * u
            dbm_acc[...] += dmixed
            dvn = jnp.zeros((SGU_CHUNK, D_SGU), F32)
            for g in range(N_SGU_GROUPS):
                dm_g = jnp.where(grp == g, dmixed, 0.0).astype(BF16)
                dws_ref[g] += _dot_nt(dm_g, vn)
                dvn = dvn + _dot_tn(wc[g], dm_g)
            dlg_ref[...] += jnp.sum(dvn * vhat, axis=0, keepdims=True)
            dlb_ref[...] += jnp.sum(dvn, axis=0, keepdims=True)
            dsgu_ref[rows, 256:512] = (_ln_bwd(dvn, vhat, rstd, lg_ref[...]) * dgv).astype(BF16)

        later = (lax.broadcasted_iota(jnp.int32, (128, 128), 0) <= lax.broadcasted_iota(jnp.int32, (128, 128), 1)).astype(F32)
        head = lax.broadcasted_iota(jnp.int32, (D_FOX, 128), 1)
        pick = (lax.broadcasted_iota(jnp.int32, (D_FOX, 128), 0) == 128 * (head // 2) + 64 * (1 - head % 2)).astype(F32)
        carry = jnp.zeros((1, 128), F32)
        for n in reversed(range(n_chunk)):
            rows = pl.ds(n * SGU_CHUNK, SGU_CHUNK)
            dcum_n = _dot(drow_ref[rows, :] - dcol_ref[rows, :], pick, HIGHEST)
            dlf = _dot(later, dcum_n, HIGHEST) + carry
            carry = carry + jnp.sum(dcum_n, axis=0, keepdims=True)
            df = dlf * jax.nn.sigmoid(-(f_ref[rows, :] + bf_ref[...]))
            df_ref[rows, :] = df.astype(BF16)
            dbf_ref[...] += jnp.sum(df, axis=0, keepdims=True)

        @pl.when(b == n_seq - 1)
        def _():
            for g in range(N_SGU_GROUPS):
                dws_ref[g] = jnp.where(tril, dws_ref[g], 0.0)
            sel = (lax.broadcasted_iota(jnp.int32, (D_SGU, 128), 0) // (D_SGU // N_SGU_GROUPS)
                   == lax.broadcasted_iota(jnp.int32, (D_SGU, 128), 1)).astype(F32)
            dbs_ref[...] = _dot(dbm_acc[...], sel, HIGHEST)

    def seq_blk(n):
        return pl.BlockSpec((seq, n), lambda b: (b, 0))

    def full(shape):
        return pl.BlockSpec(shape, lambda b: (0,) * len(shape))

    param_shapes = [(1, 128), (1, 256), (1, 256), (4, 128, 128), (128, 128)]
    return pl.pallas_call(
        body, name="sgu_gate_bwd", grid=(n_seq,),
        in_specs=[seq_blk(512), seq_blk(128), seq_blk(256), seq_blk(D_FOX), seq_blk(D_FOX),
                  full((1, 128)), full((1, 256)), full((1, 256)), full((4, 128, 128)), full((128, 256))],
        out_specs=[seq_blk(512), seq_blk(128)] + [full(s) for s in param_shapes],
        out_shape=[_hbm((t_tok, 512), BF16), _hbm((t_tok, 128), BF16)]
        + [_hbm(s, F32) for s in param_shapes],
        scratch_shapes=[pltpu.VMEM((128, 256), F32)],
        compiler_params=_params(48, ("arbitrary",)),
    )(sgu, f, dyc, drow, dcol, b_f, sgu_g, sgu_b, w_s, b_mat)


def _mix_in_bwd(dconv, dq, dk, dv, dsgu, df, dz, w_in, tm=512):
    t_tok = dz.shape[0]

    def body(dconv_ref, dq_ref, dk_ref, dv_ref, dsgu_ref, df_ref, dz_ref, w_ref, dx_ref, dp_ref):
        dqb = dq_ref[...].astype(BF16)
        pieces = [(COL_CONV, dconv_ref[...]), (COL_QKV, dqb), (COL_QKV + 512, dk_ref[...]), (COL_QKV + 1024, dv_ref[...]),
                  (COL_SGU, dsgu_ref[...]), (COL_F, df_ref[...])]
        dx = ALPHA * dz_ref[...]
        for col, val in pieces:
            width = val.shape[1]
            dp_ref[:, col:col + width] = val
            dx = dx + _dot_nt(val, w_ref[:, col:col + width])
        dx_ref[...] = dx

    def tok(n):
        return pl.BlockSpec((tm, n), lambda i: (i, 0))

    return pl.pallas_call(
        body, name="mix_in_bwd", grid=(t_tok // tm,),
        in_specs=[tok(768), tok(512), tok(512), tok(512), tok(512), tok(128), tok(D_MODEL),
                  pl.BlockSpec((D_MODEL, D_IN_PAD), lambda i: (0, 0))],
        out_specs=[tok(D_MODEL), tok(D_IN_PAD)],
        out_shape=[_hbm((t_tok, D_MODEL), F32), _hbm((t_tok, D_IN_PAD), BF16)],
        compiler_params=_params(48, ("arbitrary",)),
    )(dconv, dq, dk, dv, dsgu, df, dz, w_in)


def _pad_rows(a, rows):
    return jnp.pad(a, ((0, rows - a.shape[0]), (0, 0)))


F_BLOCK = F_ORIG // D_IN_SHARD
F_AT = F_ORIG - F_BLOCK * D_IN_SHARD
assert (F_ORIG + N_HEADS) // D_IN_SHARD == F_BLOCK


def _w_in_from_blocks(g):
    fb = g[F_BLOCK]
    zeros = jnp.zeros((D_MODEL, D_IN_PAD - COL_F - N_HEADS), g.dtype)
    return jnp.concatenate([g[d] for d in range(F_BLOCK)] + [fb[:, :F_AT], fb[:, F_AT + N_HEADS:]]
                           + [g[d] for d in range(F_BLOCK + 1, N_DEV)] + [fb[:, F_AT:F_AT + N_HEADS], zeros], axis=1)


def _w_in_to_blocks(dw):
    def cols(lo, hi):
        shift = 0 if hi <= F_ORIG else N_HEADS
        return dw[:, lo - shift:hi - shift]

    blocks = []
    for d in range(N_DEV):
        lo, hi = d * D_IN_SHARD, (d + 1) * D_IN_SHARD
        if d == F_BLOCK:
            blocks.append(jnp.concatenate([cols(lo, F_ORIG), dw[:, COL_F:COL_F + N_HEADS], cols(F_ORIG + N_HEADS, hi)], axis=1))
        else:
            blocks.append(cols(lo, hi))
    return jnp.stack(blocks)


LN1_ROWS = 2 * 8
REST_ROWS = 4 * 8 + 2 * 8 + 512 + 8 + 8 + 8


def _pack_rest(p):
    rows = [p[name].reshape(8, 128) for name in ("ln2_g", "ln2_b", "ln3_g", "ln3_b")]
    rows += [_pad_rows(p[name].reshape(2, 128), 8) for name in ("sgu_ln_g", "sgu_ln_b")]
    rows += [p["sgu_w_s"].reshape(512, 128), _pad_rows(p["sgu_b_s"], 8),
             _pad_rows(jnp.pad(p["fox_b_f"], (0, 128 - N_HEADS)).reshape(1, 128), 8), _pad_rows(p["conv_w"].reshape(6, 128), 8)]
    return jnp.concatenate(rows, axis=0)


def _pack_layer(p):
    return jnp.concatenate([p["ln1_g"].reshape(8, 128), p["ln1_b"].reshape(8, 128), _pack_rest(p)], axis=0)


def _unpack_layer(a):
    r = 0

    def take(n, valid):
        nonlocal r
        piece = a[r:r + valid]
        r += n
        return piece

    d = {}
    for name in ("ln1_g", "ln1_b", "ln2_g", "ln2_b", "ln3_g", "ln3_b"):
        d[name] = take(8, 8).reshape(D_MODEL)
    for name in ("sgu_ln_g", "sgu_ln_b"):
        d[name] = take(8, 2).reshape(D_SGU)
    d["sgu_w_s"] = take(512, 512).reshape(N_SGU_GROUPS, SGU_CHUNK, SGU_CHUNK)
    d["sgu_b_s"] = take(8, 4).reshape(N_SGU_GROUPS, SGU_CHUNK)
    d["fox_b_f"] = take(8, 1).reshape(128)[:N_HEADS]
    d["conv_w"] = take(8, 6).reshape(3, D_CONV)
    return d


SMALL_NAMES = ("ln1_g", "ln1_b", "fox_b_f", "sgu_ln_g", "sgu_ln_b", "sgu_w_s", "sgu_b_s", "ln2_g", "ln2_b", "ln3_g", "ln3_b")
BIG_NAMES = ("ffn1_w_up", "ffn1_w_down", "mix_w_in", "mix_w_out", "ffn2_w_up", "ffn2_w_down")
UP_NAMES = ("ffn1_w_up", "ffn2_w_up")
WEIGHT_ORDER = ("ln1_g", "ln1_b", "ffn1_w_up", "ffn1_w_down", "mix_w_in", "fox_b_f", "conv_w", "sgu_ln_g", "sgu_ln_b",
                "sgu_w_s", "sgu_b_s", "mix_w_out", "ln2_g", "ln2_b", "ffn2_w_up", "ffn2_w_down", "ln3_g", "ln3_b")


class _Overlap:
    def __init__(self, w, after, me, where):
        self.me, self.where = me, where
        self.last = after
        width = D_CONV // N_DEV
        rows = jnp.pad(_pad_rows(w["conv_w"].reshape(DEPTH * 3, width), 8), ((0, 0), (0, 128 - width)))
        land = lax.dynamic_update_slice(lax.empty((N_DEV,) + rows.shape, F32), rows[None], (me, 0, 0))
        self.conv_started = self._start("conv_w_start", _peers_plan(), N_DEV - 1, [rows, land])
        self.conv_full = None
        groups = [[("ffn1_w_up", 0), ("ffn1_w_down", 0)],
                  [("mix_w_in", 0), ("mix_w_out", 0), ("ffn2_w_up", 0), ("ffn2_w_down", 0)]]
        groups += [[(name, l) for name in BIG_NAMES] for l in range(1, DEPTH)]
        self.gathers = []
        for gi, group in enumerate(groups):
            shards = [w[name][l].astype(BF16) for name, l in group]
            lands = [lax.dynamic_update_slice(lax.empty((N_DEV,) + s.shape, BF16), s[None], (me, 0, 0)) for s in shards]
            started = self._start(f"allgather_start_{gi}", _gather_plan(len(group)), 3 * len(group), shards + lands)
            self.gathers.append(dict(group=group, chips=started))
        self.all_started = self.last
        self.scatters = {}
        self.order = []
        self.small = []

    def conv_w(self, after):
        if self.conv_full is None:
            width = D_CONV // N_DEV
            gathered = _exchange_wait("conv_w_wait", _peers_plan(), N_DEV - 1, self.conv_started, after)[1]
            self.conv_full = jnp.transpose(gathered[:, :DEPTH * 3, :width], (1, 0, 2)).reshape(DEPTH, 3, D_CONV)
        return self.conv_full

    def _start(self, name, plan, n_copies, arrays):
        started = _exchange_start(name, plan, n_copies, arrays, self.last)
        self.last = started[3]
        return started

    def _group_of(self, layer, part):
        return layer + 1 if layer > 0 else (0 if part == "ffn1" else 1)

    def pass_on(self, layer, part, after):
        st = self.gathers[self._group_of(layer, part)]
        if "sibling" not in st:
            gi, m = self._group_of(layer, part), len(st["group"])
            arrays = _exchange_wait(f"allgather_wait_{gi}", _gather_plan(m), 3 * m, st["chips"], after)
            st["sibling"] = self._start(f"allgather_pass_start_{gi}", _pass_on_plan(m), 4 * m, arrays)
        return st["sibling"][3]

    def weights(self, layer, part, after):
        gi = self._group_of(layer, part)
        st = self.gathers[gi]
        if "full" not in st:
            after = self.all_started if after is None else after
            self.pass_on(layer, part, after)
            m = len(st["group"])
            arrays = _exchange_wait(f"allgather_pass_wait_{gi}", _pass_on_plan(m), 4 * m, st["sibling"], after)
            st["full"] = dict(zip(st["group"], arrays[m:]))
        g = st["full"]

        def ffn(n):
            return g[(f"ffn{n}_w_up", layer)].reshape(2, D_FF, D_MODEL), g[(f"ffn{n}_w_down", layer)].reshape(D_FF, D_MODEL)

        if part == "ffn1":
            return ffn(1)
        return (_w_in_from_blocks(g[("mix_w_in", layer)]), g[("mix_w_out", layer)].reshape(D_MODEL, D_MODEL), *ffn(2))

    def push(self, key, items):
        n = len(items)
        grads = [g for _, _, g in items]
        lands = [lax.empty((4,) + g.shape[1:], F32) for g in grads]
        started = self._start(f"rs_sibling_start_{key[0]}{key[1]}", _sibling_plan(n), 4 * n, grads + lands)
        self.scatters[key] = dict(items=items, sibling=started)
        self.order.append(key)
        return started[3]

    def advance(self, key, after):
        st = self.scatters[key]
        n = len(st["items"])
        arrays = _exchange_wait(f"rs_sibling_wait_{key[0]}{key[1]}", _sibling_plan(n), 4 * n, st["sibling"], after)
        partials = [_chip_partial(g, r, self.where) for g, r in zip(arrays[:n], arrays[n:])]
        p16 = [p for _, p in partials]
        lands = [lax.empty((3,) + p.shape[1:], BF16) for p in p16]
        started = self._start(f"rs_chip_start_{key[0]}{key[1]}", _chip_plan(n), 3 * n, p16 + lands)
        st.update(own32=[p for p, _ in partials], chip=started)
        return started[3]

    def push_small(self, rows):
        k = len(self.small)
        land = lax.dynamic_update_slice(lax.empty((N_DEV,) + rows.shape, F32), rows[None], (self.me, 0, 0))
        started = self._start(f"small_start_{k}", _peers_plan(), N_DEV - 1, [rows, land])
        self.small.append(started)
        return started[3]

    def finish(self, w, m, v):
        res = {}
        after = self.scatters[self.order[-1]]["chip"][3]
        for key in self.order:
            st = self.scatters[key]
            n = len(st["items"])
            arrays = _exchange_wait(f"rs_chip_wait_{key[0]}{key[1]}", _chip_plan(n), 3 * n, st["chip"], after)
            for (name, l, _), own32, r16 in zip(st["items"], st["own32"], arrays[n:]):
                res[name] = _adamw_shard(own32, r16, w[name], m[name], v[name], l, res.get(name))
                after = res[name][0]
        pieces = [_exchange_wait(f"small_wait_{k}", _peers_plan(), N_DEV - 1, started, after)[1]
                  for k, started in enumerate(self.small)]
        return res, pieces


def _dw_up(dgu, x, after):
    return _matmul_tn(dgu, x[None], after, tk=2048, bm=DW_ROWS).reshape(N_DEV, FFN_BLK, D_MODEL)


def _dw_down(a, dy, after):
    return _matmul_tn(a[None], dy[None], after, tk=2048, bm=DW_ROWS).reshape(N_DEV, FFN_BLK // 2, D_MODEL)


def _local_step(x, target, comm, small, n_seq):
    def vec(a):
        return a.reshape(1, -1)

    saved = []
    h = x
    for l in range(DEPTH):
        s = {}
        s["up1"], s["down1"] = comm.weights(l, "ffn1", None if l == 0 else h)
        h1, h1b, s["z1"], s["gu1"], s["x0b"] = _ffn_fwd(h, s["up1"], s["down1"], vec(small["ln1_g"][l]), vec(small["ln1_b"][l]), h)
        s["w_in"], s["w_out"], s["up2"], s["down2"] = comm.weights(l, "rest", s["z1"])
        s["x1b"] = h1b
        conv, qkv, sgu, f = _in_proj(h1, s["w_in"])
        cw = _pad_rows(comm.conv_w(h1)[l], 8)
        bf = jnp.pad(small["fox_b_f"][l], (0, 128 - N_HEADS)).reshape(1, 128)
        b_mat = jnp.repeat(small["sgu_b_s"][l].T, D_SGU // N_SGU_GROUPS, axis=1)
        mid_params = (cw, bf, vec(small["sgu_ln_g"][l]), vec(small["sgu_ln_b"][l]), small["sgu_w_s"][l], b_mat)
        cat, cum_t = _mix_mid_fwd(conv, sgu, f, *mid_params, n_seq)
        cat, lse = _fox_fwd(qkv, cum_t, cat, n_seq)
        h2, h2b, s["z2"] = _mix_out_fwd(cat, h1, s["w_out"], vec(small["ln2_g"][l]), vec(small["ln2_b"][l]))
        s.update(conv=conv, qkv=qkv, sgu=sgu, f=f, mid_params=mid_params, cat=cat, cum_t=cum_t, lse=lse, x2b=h2b)
        token = comm.pass_on(l + 1, "ffn1", s["z2"]) if l + 1 < DEPTH else h2
        ln3 = (vec(small["ln3_g"][l]), vec(small["ln3_b"][l]))
        if l + 1 < DEPTH:
            h, _, s["z3"], s["gu2"], _ = _ffn_fwd(h2, s["up2"], s["down2"], *ln3, token)
        else:
            dh, loss, s["z3"], s["gu2"] = _ffn_fwd(h2, s["up2"], s["down2"], *ln3, token, target)
        saved.append(s)

    late_rows = None
    token = loss
    pending = None
    for l in reversed(range(DEPTH)):
        s = saved[l]
        sg = {}
        dh, dy, a, dgu, sg["ln3_g"], sg["ln3_b"] = _ffn_bwd(dh, s["z3"], s["gu2"], s["up2"], s["down2"], vec(small["ln3_g"][l]), token)
        if pending is not None:
            token = comm.advance(pending, dh)
        g_up2 = _dw_up(dgu, s["x2b"], token)
        g_down2 = _dw_down(a, dy, token)
        dz, dzb, dya, dyb, dyc, sg["ln2_g"], sg["ln2_b"] = _mix_out_bwd(dh, s["z2"], s["w_out"], vec(small["ln2_g"][l]))
        g_out = _matmul_tn(s["cat"][None], dzb[None], token).reshape(N_DEV, D_MODEL // N_DEV, D_MODEL)
        dq, dk, dv, drow, dcol = _fox_bwd(s["qkv"], s["cum_t"], s["cat"], s["lse"], dyb, n_seq)
        dconv, dcw = _conv_bwd(s["conv"], dya, s["mid_params"][0], n_seq)
        dsgu, df, dbf, dlg, dlb, dws, dbs = _sgu_gate_bwd(s["sgu"], s["f"], dyc, drow, dcol, *s["mid_params"][1:], n_seq)
        sg.update(conv_w=dcw[:3], fox_b_f=dbf[0, :N_HEADS], sgu_ln_g=dlg[0], sgu_ln_b=dlb[0], sgu_w_s=dws,
                  sgu_b_s=dbs[:, :N_SGU_GROUPS].T)
        dh, dp = _mix_in_bwd(dconv, dq, dk, dv, dsgu, df, dz, s["w_in"])
        g_in = _w_in_to_blocks(_matmul_tn(s["x1b"][None], dp[None], token, tk=1024)[0])
        first = [("ffn2_w_up", l, g_up2), ("ffn2_w_down", l, g_down2), ("mix_w_out", l, g_out), ("mix_w_in", l, g_in)]
        for name in ("ln2_g", "ln2_b", "ln3_g", "ln3_b"):
            sg[name] = sg[name][0]
        if l == 0:
            comm.push((l, "a"), first)
            token = comm.push_small(_pack_rest(sg))
            pending, first = (l, "a"), []
        dh, dy, a, dgu, dg1, db1 = _ffn_bwd(dh, s["z1"], s["gu1"], s["up1"], s["down1"], vec(small["ln1_g"][l]), token)
        if l == 0:
            token = comm.advance(pending, dh)
        g_up1 = _dw_up(dgu, s["x0b"], token)
        ln1_rows = jnp.concatenate([dg1.reshape(8, 128), db1.reshape(8, 128)], axis=0)
        if l == 0:
            token = comm.push((l, "b"), [("ffn1_w_up", l, g_up1)])
            g_down1 = _dw_down(a, dy, token)
            token = comm.advance((l, "b"), g_down1)
            token = comm.push((l, "c"), [("ffn1_w_down", l, g_down1)])
            token = comm.advance((l, "c"), token)
            late_rows = ln1_rows
        else:
            g_down1 = _dw_down(a, dy, token)
            pending = (l, "b")
            comm.push(pending, first + [("ffn1_w_up", l, g_up1), ("ffn1_w_down", l, g_down1)])
            token = comm.push_small(jnp.concatenate([ln1_rows, _pack_rest(sg)], axis=0))
    return loss, dh, late_rows


def kernel(x, ln1_g, ln1_b, ffn1_w_up, ffn1_w_down, mix_w_in, fox_b_f, conv_w, sgu_ln_g, sgu_ln_b, sgu_w_s, sgu_b_s, mix_w_out, ln2_g, ln2_b, ffn2_w_up, ffn2_w_down, ln3_g, ln3_b, loss_target, m_ln1_g, m_ln1_b, m_ffn1_w_up, m_ffn1_w_down, m_mix_w_in, m_fox_b_f, m_conv_w, m_sgu_ln_g, m_sgu_ln_b, m_sgu_w_s, m_sgu_b_s, m_mix_w_out, m_ln2_g, m_ln2_b, m_ffn2_w_up, m_ffn2_w_down, m_ln3_g, m_ln3_b, v_ln1_g, v_ln1_b, v_ffn1_w_up, v_ffn1_w_down, v_mix_w_in, v_fox_b_f, v_conv_w, v_sgu_ln_g, v_sgu_ln_b, v_sgu_w_s, v_sgu_b_s, v_mix_w_out, v_ln2_g, v_ln2_b, v_ffn2_w_up, v_ffn2_w_down, v_ln3_g, v_ln3_b):
    w = dict(ln1_g=ln1_g, ln1_b=ln1_b, ffn1_w_up=ffn1_w_up, ffn1_w_down=ffn1_w_down, mix_w_in=mix_w_in, fox_b_f=fox_b_f,
             conv_w=conv_w, sgu_ln_g=sgu_ln_g, sgu_ln_b=sgu_ln_b, sgu_w_s=sgu_w_s, sgu_b_s=sgu_b_s, mix_w_out=mix_w_out,
             ln2_g=ln2_g, ln2_b=ln2_b, ffn2_w_up=ffn2_w_up, ffn2_w_down=ffn2_w_down, ln3_g=ln3_g, ln3_b=ln3_b)
    m = dict(ln1_g=m_ln1_g, ln1_b=m_ln1_b, ffn1_w_up=m_ffn1_w_up, ffn1_w_down=m_ffn1_w_down, mix_w_in=m_mix_w_in,
             fox_b_f=m_fox_b_f, conv_w=m_conv_w, sgu_ln_g=m_sgu_ln_g, sgu_ln_b=m_sgu_ln_b, sgu_w_s=m_sgu_w_s,
             sgu_b_s=m_sgu_b_s, mix_w_out=m_mix_w_out, ln2_g=m_ln2_g, ln2_b=m_ln2_b, ffn2_w_up=m_ffn2_w_up,
             ffn2_w_down=m_ffn2_w_down, ln3_g=m_ln3_g, ln3_b=m_ln3_b)
    v = dict(ln1_g=v_ln1_g, ln1_b=v_ln1_b, ffn1_w_up=v_ffn1_w_up, ffn1_w_down=v_ffn1_w_down, mix_w_in=v_mix_w_in,
             fox_b_f=v_fox_b_f, conv_w=v_conv_w, sgu_ln_g=v_sgu_ln_g, sgu_ln_b=v_sgu_ln_b, sgu_w_s=v_sgu_w_s,
             sgu_b_s=v_sgu_b_s, mix_w_out=v_mix_w_out, ln2_g=v_ln2_g, ln2_b=v_ln2_b, ffn2_w_up=v_ffn2_w_up,
             ffn2_w_down=v_ffn2_w_down, ln3_g=v_ln3_g, ln3_b=v_ln3_b)

    mx, my, mc = lax.axis_index("x"), lax.axis_index("y"), lax.axis_index("c")
    me = 4 * mx + 2 * my + mc
    n_seq, seq, _ = x.shape
    t_tok = n_seq * seq
    for name in UP_NAMES:
        for t in (w, m, v):
            t[name] = jnp.transpose(t[name], (0, 2, 1))

    comm = _Overlap(w, x, me, jnp.stack([mc, 2 * mx + my]).astype(jnp.int32))
    small = {name: w[name] for name in SMALL_NAMES}

    loss_dev, grad_x, late_rows = _local_step(
        x.reshape(t_tok, D_MODEL), loss_target.reshape(t_tok, D_MODEL), comm, small, n_seq)
    loss = lax.psum(loss_dev[0, 0], ("x", "y", "c"))
    out, pieces = comm.finish(w, m, v)
    for name in UP_NAMES:
        out[name] = [jnp.transpose(a, (0, 2, 1)) for a in out[name]]

    pieces.append(_allgather_small(late_rows))
    spans = [(l, 0, LN1_ROWS + REST_ROWS) for l in reversed(range(1, DEPTH))] + [(0, LN1_ROWS, LN1_ROWS + REST_ROWS), (0, 0, LN1_ROWS)]

    def widen(a):
        return lax.dynamic_update_slice(jnp.zeros((3, D_CONV), F32), a, (0, me * (D_CONV // N_DEV)))

    packed = [[_pack_layer({**{name: t[name][l] for name in SMALL_NAMES}, "conv_w": widen(t["conv_w"][l])}) for l in range(DEPTH)]
              for t in (w, m, v)]
    rows_out = {}
    for (l, lo, hi), gathered_piece in zip(spans, pieces):
        rows_out[(l, lo)] = _adamw_small(gathered_piece, *[packed[t][l][lo:hi] for t in range(3)])
    per_layer = []
    for l in range(DEPTH):
        parts = sorted(lo for (ll, lo) in rows_out if ll == l)
        per_layer.append([_unpack_layer(jnp.concatenate([rows_out[(l, lo)][k] for lo in parts], axis=0)) for k in range(4)])
    for name in SMALL_NAMES:
        out[name] = [jnp.stack([per_layer[l][k][name] for l in range(DEPTH)]) for k in range(4)]
    lo_col = me * (D_CONV // N_DEV)
    out["conv_w"] = [jnp.stack([lax.dynamic_slice(per_layer[l][k]["conv_w"], (0, lo_col), (3, D_CONV // N_DEV)) for l in range(DEPTH)])
                     for k in range(4)]

    return (loss, grad_x.reshape(x.shape), *[out[name][0] for name in WEIGHT_ORDER], *[out[name][1] for name in WEIGHT_ORDER],
            *[out[name][2] for name in WEIGHT_ORDER], *[out[name][3] for name in WEIGHT_ORDER])
```

```python
import functools

import jax
import jax.numpy as jnp
from jax import lax
from jax.experimental import pallas as pl
from jax.experimental.pallas import tpu as pltpu

F32 = jnp.float32
BF16 = jnp.bfloat16
MESH = pl.DeviceIdType.MESH

N_DEV = 8
DEPTH = 2
D_MODEL = 1024
D_FF = 2816
FFN_BLK = 2 * D_FF // N_DEV
MXU_TILE_V7X = 256
FFN_CHUNKS = tuple((lo, min(lo + 3 * MXU_TILE_V7X, D_FF)) for lo in range(0, D_FF, 3 * MXU_TILE_V7X))
FFN_BWD_CHUNKS = tuple((lo, min(lo + 4 * MXU_TILE_V7X, D_FF)) for lo in range(0, D_FF, 4 * MXU_TILE_V7X))
DW_ROWS = D_FF // 2
D_CONV = 256
D_FOX = 512
N_HEADS = 8
D_SGU = 256
N_SGU_GROUPS = 4
SGU_CHUNK = 128
D_IN = 3 * D_CONV + 3 * D_FOX + N_HEADS + 2 * D_SGU
D_IN_SHARD = D_IN // N_DEV
COL_CONV, COL_QKV, COL_SGU, COL_F = 0, 768, 2304, 2816
D_IN_PAD = 2944
F_ORIG = 3 * D_CONV + 3 * D_FOX
ALPHA = (2 * DEPTH) ** 0.25
LN_EPS = 1e-5
ATT_SCALE = 0.125
ATT_BLK = 512
ATT_PAIRS = 2
NEG = -1e30

ADAM_LR, ADAM_B1, ADAM_B2, ADAM_EPS, ADAM_WD, ADAM_STEP = 0.001, 0.9, 0.999, 1e-08, 0.01, 10

VMEM_BYTES_V7X = 64 * 1024 * 1024
HIGHEST = lax.Precision.HIGHEST


def _params(vmem_mb, sem=None):
    assert vmem_mb * 1024 * 1024 < VMEM_BYTES_V7X
    kw = dict(vmem_limit_bytes=vmem_mb * 1024 * 1024)
    if sem is not None:
        kw["dimension_semantics"] = sem
    return pltpu.CompilerParams(**kw)


def _dot(a, b, precision=None):
    return lax.dot_general(a, b, (((1,), (0,)), ((), ())), preferred_element_type=F32, precision=precision)


def _dot_nt(a, b):
    return lax.dot_general(a, b, (((1,), (1,)), ((), ())), preferred_element_type=F32)


def _dot_tn(a, b):
    return lax.dot_general(a, b, (((0,), (0,)), ((), ())), preferred_element_type=F32)


def _ln_stats(z):
    mu = jnp.mean(z, axis=-1, keepdims=True)
    zc = z - mu
    var = jnp.mean(zc * zc, axis=-1, keepdims=True)
    rstd = lax.rsqrt(var + LN_EPS)
    return zc * rstd, rstd


def _ln_bwd(dy, xhat, rstd, g):
    dxh = dy * g
    m1 = jnp.mean(dxh, axis=-1, keepdims=True)
    m2 = jnp.mean(dxh * xhat, axis=-1, keepdims=True)
    return rstd * (dxh - m1 - xhat * m2)


_GELU_C = 0.7978845608028654


def _gelu(x):
    return 0.5 * x * (1.0 + jnp.tanh(_GELU_C * (x + 0.044715 * x * x * x)))


def _gelu_with_grad(x):
    t = jnp.tanh(_GELU_C * (x + 0.044715 * x * x * x))
    return 0.5 * x * (1.0 + t), 0.5 * (1.0 + t) + 0.5 * x * (1.0 - t * t) * _GELU_C * (1.0 + 3 * 0.044715 * x * x)


def _hbm(shape, dtype):
    n = 1
    for d in shape:
        n *= d
    if n * jnp.dtype(dtype).itemsize >= 1024 * 1024:
        return pltpu.HBM(tuple(shape), dtype)
    return jax.ShapeDtypeStruct(tuple(shape), dtype)


def _vspec():
    return pl.BlockSpec(memory_space=pltpu.VMEM)


def _anyspec():
    return pl.BlockSpec(memory_space=pl.ANY)


def _mesh_pos():
    return lax.axis_index("x"), lax.axis_index("y"), lax.axis_index("c")


def _other_chips(x, y):
    return [(1 - x, y), (x, 1 - y), (1 - x, 1 - y)]


_HBM_SPEC = pl.BlockSpec(memory_space=pltpu.HBM)
_SEM_SPEC = pl.BlockSpec(memory_space=pltpu.SEMAPHORE)
_DATAFLOW_EFFECT = pltpu.SideEffectType.DATAFLOW_SIDE_EFFECTING


def _remote_copies(plan, refs, send_sems, recv_sems):
    return [pltpu.make_async_remote_copy(src_ref=src, dst_ref=dst, send_sem=send_sems.at[k], recv_sem=recv_sems.at[k],
                                         device_id=to, device_id_type=MESH)
            for k, (src, dst, to) in enumerate(plan(refs, *_mesh_pos()))]


def _exchange_start(name, plan, n_copies, arrays, after):
    n = len(arrays)

    def body(*refs):
        send_sems, recv_sems, token = refs[n + 1], refs[n + 2], refs[-1]
        for cp in _remote_copies(plan, refs[:n], send_sems, recv_sems):
            cp.start()
        token[...] = jnp.zeros_like(token)

    out = pl.pallas_call(
        body, name=name,
        out_shape=(pltpu.SemaphoreType.DMA((n_copies,)), pltpu.SemaphoreType.DMA((n_copies,)),
                   *[pltpu.HBM(a.shape, a.dtype) for a in arrays], _hbm((8, 128), F32)),
        in_specs=[_HBM_SPEC] * n + [_anyspec()],
        out_specs=(_SEM_SPEC, _SEM_SPEC, *[_HBM_SPEC] * n, _vspec()),
        input_output_aliases={i: 2 + i for i in range(n)},
        compiler_params=pltpu.CompilerParams(has_side_effects=_DATAFLOW_EFFECT),
    )(*[pltpu.with_memory_space_constraint(a, pltpu.HBM) for a in arrays], after)
    return out[0], out[1], list(out[2:2 + n]), out[-1]


def _exchange_wait(name, plan, n_copies, started, after):
    send_sems, recv_sems, arrays, _ = started
    n = len(arrays)

    def body(*refs):
        for cp in _remote_copies(plan, refs[:n], refs[n], refs[n + 1]):
            cp.wait_send()
            cp.wait_recv()

    out = pl.pallas_call(
        body, name=name,
        out_shape=tuple(pltpu.HBM(a.shape, a.dtype) for a in arrays),
        in_specs=[_HBM_SPEC] * n + [_SEM_SPEC, _SEM_SPEC, _anyspec()], out_specs=tuple([_HBM_SPEC] * n),
        input_output_aliases={i: i for i in range(n)},
        compiler_params=pltpu.CompilerParams(has_side_effects=_DATAFLOW_EFFECT),
    )(*arrays, send_sems, recv_sems, after)
    return list(out)


def _gather_plan(m):
    def plan(refs, x, y, c):
        me = 4 * x + 2 * y + c
        return [(refs[i], refs[m + i].at[me], (*chip, c)) for i in range(m) for chip in _other_chips(x, y)]
    return plan


def _pass_on_plan(m):
    def plan(refs, x, y, c):
        out = []
        for i in range(m):
            out.append((refs[i], refs[m + i].at[4 * x + 2 * y + c], (x, y, 1 - c)))
            for cx, cy in _other_chips(x, y):
                block = refs[m + i].at[4 * cx + 2 * cy + c]
                out.append((block, block, (x, y, 1 - c)))
        return out
    return plan


def _peers_plan():
    def plan(refs, x, y, c):
        rel = [(dx, dy, dc) for dx in (0, 1) for dy in (0, 1) for dc in (0, 1)][1:]
        return [(refs[0], refs[1].at[4 * x + 2 * y + c], (x ^ dx, y ^ dy, c ^ dc)) for dx, dy, dc in rel]
    return plan


def _allgather_small(v):
    rows = v.shape[0]

    def body(v_ref, out_ref, send_sems, recv_sems):
        x, y, c = _mesh_pos()
        me = 4 * x + 2 * y + c
        out_ref[me] = v_ref[...]
        rel = [(dx, dy, dc) for dx in (0, 1) for dy in (0, 1) for dc in (0, 1)][1:]
        copies = []
        for k, (dx, dy, dc) in enumerate(rel):
            to = (x ^ dx, y ^ dy, c ^ dc)
            copies.append(pltpu.make_async_remote_copy(
                src_ref=v_ref, dst_ref=out_ref.at[me], send_sem=send_sems.at[k], recv_sem=recv_sems.at[k],
                device_id=to, device_id_type=MESH))
        for cp in copies:
            cp.start()
        for k, (dx, dy, dc) in enumerate(rel):
            src_blk = 4 * (x ^ dx) + 2 * (y ^ dy) + (c ^ dc)
            pltpu.make_async_remote_copy(
                src_ref=v_ref, dst_ref=out_ref.at[src_blk], send_sem=send_sems.at[k], recv_sem=recv_sems.at[k],
                device_id=(x, y, c), device_id_type=MESH).wait_recv()
        for cp in copies:
            cp.wait_send()

    return pl.pallas_call(
        body, name="allgather_small",
        out_shape=jax.ShapeDtypeStruct((N_DEV, rows, 128), v.dtype),
        in_specs=[_vspec()], out_specs=_vspec(),
        scratch_shapes=[pltpu.SemaphoreType.DMA((7,)), pltpu.SemaphoreType.DMA((7,))],
        compiler_params=_params(24),
    )(v)


def _sibling_plan(n):
    def plan(refs, x, y, c):
        return [(refs[a].at[2 * q + (1 - c)], refs[n + a].at[q], (x, y, 1 - c)) for a in range(n) for q in range(4)]
    return plan


def _chip_plan(n):
    def plan(refs, x, y, c):
        return [(refs[a].at[2 * cx + cy], refs[n + a].at[j], (cx, cy, c))
                for a in range(n) for j, (cx, cy) in enumerate(_other_chips(x, y))]
    return plan


def _row_tile(rows, cols, budget_bytes=2 * 1024 * 1024):
    best = 8
    for t in range(8, rows + 1, 8):
        if rows % t == 0 and t * cols * 4 <= budget_bytes:
            best = t
    return best


def _chip_partial(g, recv, where):
    _, rows, cols = g.shape
    tr = _row_tile(rows, cols)

    def body(where_ref, g_ref, r_ref, own_ref, o16_ref):
        s = g_ref[...] + r_ref[...]
        o16_ref[...] = s.astype(BF16)

        @pl.when(pl.program_id(1) == where_ref[1])
        def _():
            own_ref[...] = s

    blk = (None, tr, cols)
    return pl.pallas_call(
        body, name="rs_chip_partial",
        grid_spec=pltpu.PrefetchScalarGridSpec(
            num_scalar_prefetch=1, grid=(rows // tr, 4),
            in_specs=[pl.BlockSpec(blk, lambda i, q, w: (2 * q + w[0], i, 0)),
                      pl.BlockSpec(blk, lambda i, q, w: (q, i, 0))],
            out_specs=[pl.BlockSpec((tr, cols), lambda i, q, w: (i, 0)), pl.BlockSpec(blk, lambda i, q, w: (q, i, 0))]),
        out_shape=[_hbm((rows, cols), F32), _hbm((4, rows, cols), BF16)],
        compiler_params=_params(32),
    )(where, g, recv)


def _adam_math(w, g, m, v):
    m = ADAM_B1 * m + (1.0 - ADAM_B1) * g
    v = ADAM_B2 * v + (1.0 - ADAM_B2) * (g * g)
    m_hat = m / (1.0 - ADAM_B1 ** ADAM_STEP)
    v_hat = v / (1.0 - ADAM_B2 ** ADAM_STEP)
    delta = -ADAM_LR * (m_hat / (jnp.sqrt(v_hat) + ADAM_EPS) + ADAM_WD * w)
    return delta, m, v


def _adamw_shard(own32, recv16, w, m, v, layer, earlier):
    depth, rows, cols = w.shape
    tr = _row_tile(rows, cols, 1024 * 1024)
    n_prev = 0 if earlier is None else 4

    def body(p_ref, r_ref, w_ref, m_ref, v_ref, *rest):
        g_out, d_out, m_out, v_out = rest[n_prev:]
        g = p_ref[...] + r_ref[0].astype(F32) + r_ref[1].astype(F32) + r_ref[2].astype(F32)
        d, mn, vn = _adam_math(w_ref[...], g, m_ref[...], v_ref[...])
        g_out[...] = g
        d_out[...] = d
        m_out[...] = mn
        v_out[...] = vn

    mine = pl.BlockSpec((None, tr, cols), lambda i: (layer, i, 0))
    return pl.pallas_call(
        body, name="adamw_shard", grid=(rows // tr,),
        in_specs=[pl.BlockSpec((tr, cols), lambda i: (i, 0)), pl.BlockSpec((3, tr, cols), lambda i: (0, i, 0)),
                  mine, mine, mine] + [_anyspec()] * n_prev,
        out_specs=[mine] * 4,
        out_shape=[_hbm((depth, rows, cols), F32)] * 4,
        input_output_aliases={5 + k: k for k in range(n_prev)},
        compiler_params=_params(32),
    )(own32, recv16, *[pltpu.with_memory_space_constraint(t, pltpu.HBM) for t in (w, m, v)],
      *([] if earlier is None else earlier))


def _adamw_small(gathered, w, m, v):
    rows = w.shape[0]

    def body(a_ref, w_ref, m_ref, v_ref, g_out, d_out, m_out, v_out):
        g = a_ref[0]
        for d in range(1, N_DEV):
            g = g + a_ref[d]
        dl, mn, vn = _adam_math(w_ref[...], g, m_ref[...], v_ref[...])
        g_out[...] = g
        d_out[...] = dl
        m_out[...] = mn
        v_out[...] = vn

    return pl.pallas_call(
        body, name="adamw_small",
        in_specs=[_vspec()] * 4, out_specs=[_vspec()] * 4,
        out_shape=[_hbm((rows, 128), F32)] * 4,
        compiler_params=_params(32),
    )(gathered, w, m, v)


def _load_weights_once(pairs, sems):
    @pl.when(pl.program_id(0) == 0)
    def _():
        cps = [pltpu.make_async_copy(src, dst, sems.at[i]) for i, (src, dst) in enumerate(pairs)]
        for cp in cps:
            cp.start()
        for cp in cps:
            cp.wait()


def _ffn_fwd(x, wup, wd, ln_g, ln_b, after, target=None, tm=512):
    t_tok = x.shape[0]
    last = target is not None

    def body(x_ref, g_ref, b_ref, wup_hbm, wd_hbm, _after, *rest):
        if last:
            t_ref, dxn_ref, loss_ref, z_ref, gu_ref, wup_v, wd_v, sems = rest
        else:
            xn_ref, xnb_ref, z_ref, gu_ref, xb_ref, wup_v, wd_v, sems = rest
        _load_weights_once([(wup_hbm, wup_v), (wd_hbm, wd_v)], sems)
        xb = x_ref[...].astype(BF16)
        if not last:
            xb_ref[...] = xb
        y = None
        for lo, hi in FFN_CHUNKS:
            g = _dot_nt(xb, wup_v[0, lo:hi])
            u = _dot_nt(xb, wup_v[1, lo:hi])
            gu_ref[0, :, lo:hi] = g.astype(BF16)
            gu_ref[1, :, lo:hi] = u.astype(BF16)
            a = (g * jax.nn.sigmoid(g) * u).astype(BF16)
            part = _dot(a, wd_v[lo:hi])
            y = part if y is None else y + part
        z = ALPHA * x_ref[...] + 0.5 * y
        xhat, _ = _ln_stats(z)
        xn = xhat * g_ref[...] + b_ref[...]
        z_ref[...] = z
        if last:
            err = xn - t_ref[...]
            dxn_ref[...] = err * (1.0 / D_MODEL)
            part = jnp.sum(jnp.sum(err * err, axis=1, keepdims=True), axis=0, keepdims=True) * (0.5 / D_MODEL)

            @pl.when(pl.program_id(0) == 0)
            def _():
                loss_ref[...] = jnp.zeros_like(loss_ref)

            loss_ref[...] += part
        else:
            xn_ref[...] = xn
            xnb_ref[...] = xn.astype(BF16)

    tok = pl.BlockSpec((tm, D_MODEL), lambda i: (i, 0))
    vec = pl.BlockSpec((1, D_MODEL), lambda i: (0, 0))
    gu_spec = pl.BlockSpec((2, tm, D_FF), lambda i: (0, i, 0))
    gu_shape = _hbm((2, t_tok, D_FF), BF16)
    f32_tok, bf16_tok = _hbm((t_tok, D_MODEL), F32), _hbm((t_tok, D_MODEL), BF16)
    if last:
        extra_in, extra_spec = [target], [tok]
        out_specs = [tok, pl.BlockSpec((1, 128), lambda i: (0, 0)), tok, gu_spec]
        out_shape = [f32_tok, _hbm((1, 128), F32), f32_tok, gu_shape]
    else:
        extra_in, extra_spec = [], []
        out_specs = [tok, tok, tok, gu_spec, tok]
        out_shape = [f32_tok, bf16_tok, f32_tok, gu_shape, bf16_tok]
    return pl.pallas_call(
        body, name="ffn_fwd_loss" if last else "ffn_fwd", grid=(t_tok // tm,),
        in_specs=[tok, vec, vec, _anyspec(), _anyspec(), _anyspec()] + extra_spec,
        out_specs=out_specs, out_shape=out_shape,
        scratch_shapes=[pltpu.VMEM((2, D_FF, D_MODEL), BF16), pltpu.VMEM((D_FF, D_MODEL), BF16),
                        pltpu.SemaphoreType.DMA((2,))],
        compiler_params=_params(62, ("arbitrary",)),
    )(x, ln_g, ln_b, wup, wd, after, *extra_in)


def _ffn_bwd(dxn, z, gu, wup, wd, ln_g, after, tm=256):
    t_tok = dxn.shape[0]

    def body(dxn_ref, z_ref, gu_ref, g_ref, wup_hbm, wd_hbm, _after,
             dx_ref, dy_ref, a_ref, dgu_ref, dg_ref, db_ref, wup_v, wd_v, sems):
        i = pl.program_id(0)
        _load_weights_once([(wup_hbm, wup_v), (wd_hbm, wd_v)], sems)
        dxn_t = dxn_ref[...]
        xhat, rstd = _ln_stats(z_ref[...])
        pg = jnp.sum(dxn_t * xhat, axis=0, keepdims=True)
        pb = jnp.sum(dxn_t, axis=0, keepdims=True)

        @pl.when(i == 0)
        def _():
            dg_ref[...] = pg
            db_ref[...] = pb

        @pl.when(i > 0)
        def _():
            dg_ref[...] += pg
            db_ref[...] += pb

        dz = _ln_bwd(dxn_t, xhat, rstd, g_ref[...])
        dy = (0.5 * dz).astype(BF16)
        dy_ref[...] = dy
        dx = ALPHA * dz
        for lo, hi in FFN_BWD_CHUNKS:
            da = _dot_nt(dy, wd_v[lo:hi])
            g = gu_ref[0, :, lo:hi].astype(F32)
            u = gu_ref[1, :, lo:hi].astype(F32)
            sig = jax.nn.sigmoid(g)
            silu = g * sig
            a_ref[:, lo:hi] = (silu * u).astype(BF16)
            dg = (da * u * (sig * (1.0 + g * (1.0 - sig)))).astype(BF16)
            du = (da * silu).astype(BF16)
            dgu_ref[0, :, lo:hi] = dg
            dgu_ref[1, :, lo:hi] = du
            dx = dx + _dot(dg, wup_v[0, lo:hi]) + _dot(du, wup_v[1, lo:hi])
        dx_ref[...] = dx

    tok = pl.BlockSpec((tm, D_MODEL), lambda i: (i, 0))
    vec = pl.BlockSpec((1, D_MODEL), lambda i: (0, 0))
    gu_spec = pl.BlockSpec((2, tm, D_FF), lambda i: (0, i, 0))
    return pl.pallas_call(
        body, name="ffn_bwd", grid=(t_tok // tm,),
        in_specs=[tok, tok, gu_spec, vec, _anyspec(), _anyspec(), _anyspec()],
        out_specs=[tok, tok, pl.BlockSpec((tm, D_FF), lambda i: (i, 0)), gu_spec, vec, vec],
        out_shape=[_hbm((t_tok, D_MODEL), F32), _hbm((t_tok, D_MODEL), BF16),
                   _hbm((t_tok, D_FF), BF16), _hbm((2, t_tok, D_FF), BF16),
                   _hbm((1, D_MODEL), F32), _hbm((1, D_MODEL), F32)],
        scratch_shapes=[pltpu.VMEM((2, D_FF, D_MODEL), BF16), pltpu.VMEM((D_FF, D_MODEL), BF16),
                        pltpu.SemaphoreType.DMA((2,))],
        compiler_params=_params(60, ("arbitrary",)),
    )(dxn, z, gu, ln_g, wup, wd, after)


def _matmul_tn(a, b, after, tk=4096, bm=None):
    ga, t_tok, m = a.shape
    gb, _, n = b.shape
    groups = max(ga, gb)
    tk = min(tk, t_tok)
    bm = m if bm is None else bm

    def body(a_ref, b_ref, _after, o_ref):
        p = _dot_tn(a_ref[...].astype(BF16), b_ref[...].astype(BF16))

        @pl.when(pl.program_id(2) == 0)
        def _():
            o_ref[...] = p

        @pl.when(pl.program_id(2) > 0)
        def _():
            o_ref[...] += p

    return pl.pallas_call(
        body, name=f"matmul_tn_{m}x{n}", grid=(groups, m // bm, t_tok // tk),
        in_specs=[pl.BlockSpec((None, tk, bm), (lambda g, i, t: (g, t, i)) if ga > 1 else (lambda g, i, t: (0, t, i))),
                  pl.BlockSpec((None, tk, n), (lambda g, i, t: (g, t, 0)) if gb > 1 else (lambda g, i, t: (0, t, 0))),
                  _anyspec()],
        out_specs=pl.BlockSpec((None, bm, n), lambda g, i, t: (g, i, 0)),
        out_shape=_hbm((groups, m, n), F32),
        compiler_params=_params(56, ("arbitrary", "arbitrary", "arbitrary")),
    )(a, b, after)


def _in_proj(x, w_in, tm=512):
    t_tok = x.shape[0]

    def body(x_ref, w_ref, conv_ref, qkv_ref, sgu_ref, f_ref):
        xb = x_ref[...].astype(BF16)
        conv_ref[...] = _dot(xb, w_ref[:, COL_CONV:COL_QKV])
        qkv_ref[...] = _dot(xb, w_ref[:, COL_QKV:COL_SGU]).astype(BF16)
        sgu_ref[...] = _dot(xb, w_ref[:, COL_SGU:COL_F])
        f_ref[...] = _dot(xb, w_ref[:, COL_F:D_IN_PAD])

    def tok(n):
        return pl.BlockSpec((tm, n), lambda i: (i, 0))

    return pl.pallas_call(
        body, name="mix_in_proj", grid=(t_tok // tm,),
        in_specs=[tok(D_MODEL), pl.BlockSpec((D_MODEL, D_IN_PAD), lambda i: (0, 0))],
        out_specs=[tok(768), tok(1536), tok(512), tok(128)],
        out_shape=[_hbm((t_tok, 768), F32), _hbm((t_tok, 1536), BF16),
                   _hbm((t_tok, 512), F32), _hbm((t_tok, 128), F32)],
        compiler_params=_params(48, ("arbitrary",)),
    )(x, w_in)


def _shift_down(a, k):
    row = lax.broadcasted_iota(jnp.int32, a.shape, 0)
    return jnp.where(row >= k, pltpu.roll(a, k, 0), 0.0)


def _shift_up(a, k):
    rows = a.shape[0]
    row = lax.broadcasted_iota(jnp.int32, a.shape, 0)
    return jnp.where(row < rows - k, pltpu.roll(a, rows - k, 0), 0.0)


def _tril(n):
    return lax.broadcasted_iota(jnp.int32, (n, n), 0) >= lax.broadcasted_iota(jnp.int32, (n, n), 1)


def _sgu_group_of_lane():
    return lax.broadcasted_iota(jnp.int32, (1, D_SGU), 1) // (D_SGU // N_SGU_GROUPS)


def _log_sigmoid(x):
    return jnp.minimum(x, 0.0) - jnp.log1p(jnp.exp(-jnp.abs(x)))


def _mix_mid_fwd(conv, sgu, f, conv_w, b_f, sgu_g, sgu_b, w_s, b_mat, n_seq):
    t_tok = conv.shape[0]
    seq = t_tok // n_seq
    n_chunk = seq // SGU_CHUNK
    per_blk = ATT_BLK // SGU_CHUNK

    def body(conv_ref, sgu_ref, f_ref, cw_ref, bf_ref, lg_ref, lb_ref, ws_ref, bm_ref, cat_ref, cum_ref):
        z = conv_ref[:, 256:512] * conv_ref[:, 512:768]
        y = cw_ref[0:1, :] * _shift_down(z, 2) + cw_ref[1:2, :] * _shift_down(z, 1) + cw_ref[2:3, :] * z
        cat_ref[:, 0:D_CONV] = (conv_ref[:, 0:256] * y).astype(BF16)
        cat_ref[:, D_CONV:D_CONV + D_FOX] = jnp.zeros((seq, D_FOX), BF16)

        tril = _tril(SGU_CHUNK)
        grp = _sgu_group_of_lane()
        wc = [jnp.where(tril, ws_ref[g], 0.0).astype(BF16) for g in range(N_SGU_GROUPS)]
        tri_f = tril.astype(F32)
        carry = jnp.zeros((1, 128), F32)
        for n in range(n_chunk):
            rows = pl.ds(n * SGU_CHUNK, SGU_CHUNK)
            u = _gelu(sgu_ref[rows, 0:256])
            vhat, _ = _ln_stats(_gelu(sgu_ref[rows, 256:512]))
            vn = (vhat * lg_ref[...] + lb_ref[...]).astype(BF16)
            mixed = bm_ref[...]
            for g in range(N_SGU_GROUPS):
                mixed = mixed + jnp.where(grp == g, _dot(wc[g], vn), 0.0)
            cat_ref[rows, D_CONV + D_FOX:D_MODEL] = (u * mixed).astype(BF16)

            log_f = _log_sigmoid(f_ref[rows, :] + bf_ref[...])
            cs = _dot(tri_f, log_f, HIGHEST) + carry
            carry = cs[SGU_CHUNK - 1:SGU_CHUNK, :]
            cs_t = cs.T
            lanes = pl.ds((n % per_blk) * SGU_CHUNK, SGU_CHUNK)
            for h in range(N_HEADS):
                cum_ref[h, n // per_blk, :, lanes] = cs_t[h:h + 1, :]

    def seq_blk(n):
        return pl.BlockSpec((seq, n), lambda b: (b, 0))

    def full(shape):
        return pl.BlockSpec(shape, lambda b: (0,) * len(shape))

    return pl.pallas_call(
        body, name="mix_mid_fwd", grid=(n_seq,),
        in_specs=[seq_blk(768), seq_blk(512), seq_blk(128), full((8, 256)), full((1, 128)), full((1, 256)),
                  full((1, 256)), full((4, 128, 128)), full((128, 256))],
        out_specs=[seq_blk(D_MODEL), pl.BlockSpec((N_HEADS, seq // ATT_BLK, 1, ATT_BLK), lambda b: (b, 0, 0, 0))],
        out_shape=[_hbm((t_tok, D_MODEL), BF16), _hbm((n_seq * N_HEADS, seq // ATT_BLK, 1, ATT_BLK), F32)],
        compiler_params=_params(48, ("arbitrary",)),
    )(conv, sgu, f, conv_w, b_f, sgu_g, sgu_b, w_s, b_mat)


def _head_masks():
    lane = lax.broadcasted_iota(jnp.int32, (1, 128), 1)
    return lane < 64, lane


def _fox_fwd(qkv, cum_t, cat, n_seq):
    t_tok = qkv.shape[0]
    seq = t_tok // n_seq
    nq = seq // ATT_BLK
    blk = ATT_BLK

    def body(q_ref, k_ref, v_ref, c_ref, _cat, o_ref, lse_ref):
        qi = pl.program_id(2)
        first, _ = _head_masks()
        one = jnp.ones((1, 128), BF16)
        qh = []
        for hp in range(ATT_PAIRS):
            qs = q_ref[:, 128 * hp:128 * hp + 128] * ATT_SCALE
            zero = jnp.zeros_like(qs)
            qh += [jnp.where(first, qs, zero), jnp.where(first, zero, qs)]

        def step(kb, carry, masked):
            ms, accs = carry
            rows = pl.ds(pl.multiple_of(kb * blk, blk), blk)
            new_m, new_acc = [], []
            for hp in range(ATT_PAIRS):
                k = k_ref[rows, 128 * hp:128 * hp + 128]
                v = v_ref[rows, 128 * hp:128 * hp + 128]
                for h in range(2):
                    i = 2 * hp + h
                    s = _dot_nt(qh[i], k) - c_ref[i, kb]
                    if masked:
                        s = jnp.where(causal, s, NEG)
                    m_new = jnp.maximum(ms[i], jnp.max(s, axis=1, keepdims=True))
                    p = jnp.exp(s - m_new)
                    vh = jnp.where(first, v, one) if h == 0 else jnp.where(first, one, v)
                    new_acc.append(accs[i] * jnp.exp(ms[i] - m_new) + _dot(p.astype(BF16), vh))
                    new_m.append(m_new)
            return tuple(new_m), tuple(new_acc)

        causal = _tril(blk)
        n_heads = 2 * ATT_PAIRS
        col = jnp.full((blk, 1), NEG, F32)
        zacc = jnp.zeros((blk, 128), F32)
        carry = lax.fori_loop(0, qi, lambda kb, cr: step(kb, cr, False), ((col,) * n_heads, (zacc,) * n_heads))
        ms, accs = step(qi, carry, True)
        for hp in range(ATT_PAIRS):
            acc0, acc1 = accs[2 * hp], accs[2 * hp + 1]
            l0 = pltpu.roll(acc0, 64, 1)
            l1 = pltpu.roll(acc1, 64, 1)
            o_ref[:, 128 * hp:128 * hp + 128] = jnp.where(first, acc0 / l0, acc1 / l1).astype(BF16)
            lse_ref[:, 128 * hp:128 * hp + 128] = jnp.where(first, ms[2 * hp] + jnp.log(l0), ms[2 * hp + 1] + jnp.log(l1))

    wide = 128 * ATT_PAIRS
    n_grp = D_FOX // wide
    first_col = D_CONV // wide
    return pl.pallas_call(
        body, name="fox_fwd", grid=(n_seq, n_grp, nq),
        in_specs=[pl.BlockSpec((blk, wide), lambda b, g, qi: (b * nq + qi, g)),
                  pl.BlockSpec((seq, wide), lambda b, g, qi: (b, n_grp + g)),
                  pl.BlockSpec((seq, wide), lambda b, g, qi: (b, 2 * n_grp + g)),
                  pl.BlockSpec((2 * ATT_PAIRS, nq, 1, blk), lambda b, g, qi: (b * n_grp + g, 0, 0, 0)), _anyspec()],
        out_specs=[pl.BlockSpec((blk, wide), lambda b, g, qi: (b * nq + qi, first_col + g)),
                   pl.BlockSpec((blk, wide), lambda b, g, qi: (b * nq + qi, g))],
        out_shape=[_hbm(cat.shape, BF16), _hbm((t_tok, D_FOX), F32)],
        input_output_aliases={4: 0},
        compiler_params=_params(48, ("arbitrary", "arbitrary", "arbitrary")),
    )(qkv, qkv, qkv, cum_t, cat)


def _fox_bwd(qkv, cum_t, cat, lse, d_o, n_seq):
    t_tok = qkv.shape[0]
    seq = t_tok // n_seq
    nk = seq // ATT_BLK
    blk = ATT_BLK

    def body(q_ref, k_ref, v_ref, c_ref, o_ref, lse_ref, do_ref, dq_ref, dk_ref, dv_ref, drow_ref, dcol_ref):
        kb = pl.program_id(2)
        first, lane = _head_masks()
        second = jnp.logical_not(first)
        one = jnp.ones((1, 128), BF16)
        causal = _tril(blk)

        @pl.when(kb == 0)
        def _():
            dq_ref[...] = jnp.zeros_like(dq_ref)
            drow_ref[...] = jnp.zeros_like(drow_ref)

        def step(qi, carry, masked):
            rows = pl.ds(pl.multiple_of(qi * blk, blk), blk)
            dks, dvs = carry
            new_dk, new_dv = [], []
            for hp in range(ATT_PAIRS):
                cols = slice(128 * hp, 128 * hp + 128)
                k = k_ref[:, cols]
                v = v_ref[:, cols]
                ks = k * ATT_SCALE
                zero = jnp.zeros_like(k)
                qs = q_ref[rows, cols] * ATT_SCALE
                d_o = do_ref[rows, cols]
                dd = d_o.astype(F32) * o_ref[rows, cols].astype(F32)
                lse_t = lse_ref[rows, cols]
                dq = []
                for h, mine in enumerate((first, second)):
                    i = 2 * hp + h
                    qh = jnp.where(mine, qs, zero)
                    doh = jnp.where(mine, d_o, zero)
                    delta = jnp.sum(jnp.where(mine, dd, 0.0), axis=1, keepdims=True)
                    lse_h = jnp.sum(jnp.where(lane == 64 * h, lse_t, 0.0), axis=1, keepdims=True)
                    s = _dot_nt(qh, k) - c_ref[i]
                    if masked:
                        s = jnp.where(causal, s, NEG)
                    p = jnp.exp(s - lse_h)
                    ds = (p * (_dot_nt(doh, v) - delta)).astype(BF16)
                    new_dk.append(dks[i] + _dot_tn(ds, jnp.where(mine, qs, one)))
                    new_dv.append(dvs[i] + _dot_tn(p.astype(BF16), doh))
                    dq.append(_dot(ds, jnp.where(mine, ks, one)))
                dq_ref[rows, cols] += jnp.where(first, dq[0], dq[1])
                drow_ref[rows, cols] += jnp.where(first, dq[1], dq[0])
            return tuple(new_dk), tuple(new_dv)

        zt = (jnp.zeros((blk, 128), F32),) * (2 * ATT_PAIRS)
        carry = step(kb, (zt, zt), True)
        dks, dvs = lax.fori_loop(kb + 1, nk, lambda qi, cr: step(qi, cr, False), carry)
        for hp in range(ATT_PAIRS):
            cols = slice(128 * hp, 128 * hp + 128)
            dk_ref[:, cols] = jnp.where(first, dks[2 * hp], dks[2 * hp + 1]).astype(BF16)
            dcol_ref[:, cols] = jnp.where(first, dks[2 * hp + 1], dks[2 * hp])
            dv_ref[:, cols] = (dvs[2 * hp] + dvs[2 * hp + 1]).astype(BF16)

    wide = 128 * ATT_PAIRS
    n_grp = D_FOX // wide

    def seq_spec(col0):
        return pl.BlockSpec((seq, wide), lambda b, g, kb: (b, col0 + g))

    def key_spec(col0):
        return pl.BlockSpec((blk, wide), lambda b, g, kb: (b * nk + kb, col0 + g))

    return pl.pallas_call(
        body, name="fox_bwd", grid=(n_seq, n_grp, nk),
        in_specs=[seq_spec(0), key_spec(n_grp), key_spec(2 * n_grp),
                  pl.BlockSpec((2 * ATT_PAIRS, None, 1, blk), lambda b, g, kb: (b * n_grp + g, kb, 0, 0)),
                  seq_spec(D_CONV // wide), seq_spec(0), seq_spec(0)],
        out_specs=[seq_spec(0), key_spec(0), key_spec(0), seq_spec(0), key_spec(0)],
        out_shape=[_hbm((t_tok, D_FOX), F32), _hbm((t_tok, D_FOX), BF16),
                   _hbm((t_tok, D_FOX), BF16), _hbm((t_tok, D_FOX), F32),
                   _hbm((t_tok, D_FOX), F32)],
        compiler_params=_params(56, ("arbitrary", "arbitrary", "arbitrary")),
    )(qkv, qkv, qkv, cum_t, cat, lse, d_o)


def _mix_out_fwd(cat, x, w_out, ln_g, ln_b, tm=512):
    t_tok = x.shape[0]

    def body(cat_ref, x_ref, w_ref, g_ref, b_ref, xn_ref, xnb_ref, z_ref):
        z = ALPHA * x_ref[...] + _dot(cat_ref[...], w_ref[...])
        xhat, _ = _ln_stats(z)
        xn = xhat * g_ref[...] + b_ref[...]
        z_ref[...] = z
        xn_ref[...] = xn
        xnb_ref[...] = xn.astype(BF16)

    def tok(n):
        return pl.BlockSpec((tm, n), lambda i: (i, 0))

    vec = pl.BlockSpec((1, D_MODEL), lambda i: (0, 0))
    return pl.pallas_call(
        body, name="mix_out_fwd", grid=(t_tok // tm,),
        in_specs=[tok(D_MODEL), tok(D_MODEL), pl.BlockSpec((D_MODEL, D_MODEL), lambda i: (0, 0)), vec, vec],
        out_specs=[tok(D_MODEL)] * 3,
        out_shape=[_hbm((t_tok, D_MODEL), F32), _hbm((t_tok, D_MODEL), BF16),
                   _hbm((t_tok, D_MODEL), F32)],
        compiler_params=_params(40, ("arbitrary",)),
    )(cat, x, w_out, ln_g, ln_b)


def _mix_out_bwd(dxn, z, w_out, ln_g, tm=512):
    t_tok = dxn.shape[0]

    def body(dxn_ref, z_ref, w_ref, g_ref, dz_ref, dzb_ref, dya_ref, dyb_ref, dyc_ref, dg_ref, db_ref):
        i = pl.program_id(0)
        dxn_t = dxn_ref[...]
        xhat, rstd = _ln_stats(z_ref[...])
        pg = jnp.sum(dxn_t * xhat, axis=0, keepdims=True)
        pb = jnp.sum(dxn_t, axis=0, keepdims=True)

        @pl.when(i == 0)
        def _():
            dg_ref[...] = pg
            db_ref[...] = pb

        @pl.when(i > 0)
        def _():
            dg_ref[...] += pg
            db_ref[...] += pb

        dz = _ln_bwd(dxn_t, xhat, rstd, g_ref[...])
        dzb = dz.astype(BF16)
        dz_ref[...] = dz
        dzb_ref[...] = dzb
        dya_ref[...] = _dot_nt(dzb, w_ref[0:256, :])
        dyb_ref[...] = _dot_nt(dzb, w_ref[256:768, :]).astype(BF16)
        dyc_ref[...] = _dot_nt(dzb, w_ref[768:1024, :])

    def tok(n):
        return pl.BlockSpec((tm, n), lambda i: (i, 0))

    vec = pl.BlockSpec((1, D_MODEL), lambda i: (0, 0))
    return pl.pallas_call(
        body, name="mix_out_bwd", grid=(t_tok // tm,),
        in_specs=[tok(D_MODEL), tok(D_MODEL), pl.BlockSpec((D_MODEL, D_MODEL), lambda i: (0, 0)), vec],
        out_specs=[tok(D_MODEL), tok(D_MODEL), tok(256), tok(512), tok(256), vec, vec],
        out_shape=[_hbm((t_tok, D_MODEL), F32), _hbm((t_tok, D_MODEL), BF16),
                   _hbm((t_tok, 256), F32), _hbm((t_tok, 512), BF16),
                   _hbm((t_tok, 256), F32),
                   _hbm((1, D_MODEL), F32), _hbm((1, D_MODEL), F32)],
        compiler_params=_params(40, ("arbitrary",)),
    )(dxn, z, w_out, ln_g)


def _conv_bwd(conv, dya, conv_w, n_seq):
    t_tok = conv.shape[0]
    seq = t_tok // n_seq

    def body(conv_ref, dya_ref, cw_ref, dconv_ref, dcw_ref):
        @pl.when(pl.program_id(0) == 0)
        def _():
            dcw_ref[...] = jnp.zeros_like(dcw_ref)

        z = conv_ref[:, 256:512] * conv_ref[:, 512:768]
        z1 = _shift_down(z, 1)
        z2 = _shift_down(z, 2)
        y = cw_ref[0:1, :] * z2 + cw_ref[1:2, :] * z1 + cw_ref[2:3, :] * z
        dya_t = dya_ref[...]
        dconv_ref[:, 0:256] = (dya_t * y).astype(BF16)
        dy = dya_t * conv_ref[:, 0:256]
        dcw_ref[0:1, :] += jnp.sum(dy * z2, axis=0, keepdims=True)
        dcw_ref[1:2, :] += jnp.sum(dy * z1, axis=0, keepdims=True)
        dcw_ref[2:3, :] += jnp.sum(dy * z, axis=0, keepdims=True)
        dz = cw_ref[2:3, :] * dy + cw_ref[1:2, :] * _shift_up(dy, 1) + cw_ref[0:1, :] * _shift_up(dy, 2)
        dconv_ref[:, 256:512] = (dz * conv_ref[:, 512:768]).astype(BF16)
        dconv_ref[:, 512:768] = (dz * conv_ref[:, 256:512]).astype(BF16)

    def seq_blk(n):
        return pl.BlockSpec((seq, n), lambda b: (b, 0))

    par = pl.BlockSpec((8, 256), lambda b: (0, 0))
    return pl.pallas_call(
        body, name="conv_bwd", grid=(n_seq,),
        in_specs=[seq_blk(768), seq_blk(256), par], out_specs=[seq_blk(768), par],
        out_shape=[_hbm((t_tok, 768), BF16), _hbm((8, 256), F32)],
        compiler_params=_params(56, ("arbitrary",)),
    )(conv, dya, conv_w)


def _sgu_gate_bwd(sgu, f, dyc, drow, dcol, b_f, sgu_g, sgu_b, w_s, b_mat, n_seq):
    t_tok = sgu.shape[0]
    seq = t_tok // n_seq
    n_chunk = seq // SGU_CHUNK

    def body(sgu_ref, f_ref, dyc_ref, drow_ref, dcol_ref, bf_ref, lg_ref, lb_ref, ws_ref, bm_ref,
             dsgu_ref, df_ref, dbf_ref, dlg_ref, dlb_ref, dws_ref, dbs_ref, dbm_acc):
        b = pl.program_id(0)

        @pl.when(b == 0)
        def _():
            for r in (dbf_ref, dlg_ref, dlb_ref, dws_ref, dbm_acc):
                r[...] = jnp.zeros_like(r)

        tril = _tril(SGU_CHUNK)
        grp = _sgu_group_of_lane()
        wc = [jnp.where(tril, ws_ref[g], 0.0).astype(BF16) for g in range(N_SGU_GROUPS)]
        for n in range(n_chunk):
            rows = pl.ds(n * SGU_CHUNK, SGU_CHUNK)
            su = sgu_ref[rows, 0:256]
            sv = sgu_ref[rows, 256:512]
            u, du = _gelu_with_grad(su)
            gv, dgv = _gelu_with_grad(sv)
            vhat, rstd = _ln_stats(gv)
            vn = (vhat * lg_ref[...] + lb_ref[...]).astype(BF16)
            mixed = bm_ref[...]
            for g in range(N_SGU_GROUPS):
                mixed = mixed + jnp.where(grp == g, _dot(wc[g], vn), 0.0)
            dyc_t = dyc_ref[rows, :]
            dsgu_ref[rows, 0:256] = (dyc_t * mixed * du).astype(BF16)
            dmixed = dyc_t * u
            dbm_acc[...] += dmixed
            dvn = jnp.zeros((SGU_CHUNK, D_SGU), F32)
            for g in range(N_SGU_GROUPS):
                dm_g = jnp.where(grp == g, dmixed, 0.0).astype(BF16)
                dws_ref[g] += _dot_nt(dm_g, vn)
                dvn = dvn + _dot_tn(wc[g], dm_g)
            dlg_ref[...] += jnp.sum(dvn * vhat, axis=0, keepdims=True)
            dlb_ref[...] += jnp.sum(dvn, axis=0, keepdims=True)
            dsgu_ref[rows, 256:512] = (_ln_bwd(dvn, vhat, rstd, lg_ref[...]) * dgv).astype(BF16)

        later = (lax.broadcasted_iota(jnp.int32, (128, 128), 0) <= lax.broadcasted_iota(jnp.int32, (128, 128), 1)).astype(F32)
        head = lax.broadcasted_iota(jnp.int32, (D_FOX, 128), 1)
        pick = (lax.broadcasted_iota(jnp.int32, (D_FOX, 128), 0) == 128 * (head // 2) + 64 * (1 - head % 2)).astype(F32)
        carry = jnp.zeros((1, 128), F32)
        for n in reversed(range(n_chunk)):
            rows = pl.ds(n * SGU_CHUNK, SGU_CHUNK)
            dcum_n = _dot(drow_ref[rows, :] - dcol_ref[rows, :], pick, HIGHEST)
            dlf = _dot(later, dcum_n, HIGHEST) + carry
            carry = carry + jnp.sum(dcum_n, axis=0, keepdims=True)
            df = dlf * jax.nn.sigmoid(-(f_ref[rows, :] + bf_ref[...]))
            df_ref[rows, :] = df.astype(BF16)
            dbf_ref[...] += jnp.sum(df, axis=0, keepdims=True)

        @pl.when(b == n_seq - 1)
        def _():
            for g in range(N_SGU_GROUPS):
                dws_ref[g] = jnp.where(tril, dws_ref[g], 0.0)
            sel = (lax.broadcasted_iota(jnp.int32, (D_SGU, 128), 0) // (D_SGU // N_SGU_GROUPS)
                   == lax.broadcasted_iota(jnp.int32, (D_SGU, 128), 1)).astype(F32)
            dbs_ref[...] = _dot(dbm_acc[...], sel, HIGHEST)

    def seq_blk(n):
        return pl.BlockSpec((seq, n), lambda b: (b, 0))

    def full(shape):
        return pl.BlockSpec(shape, lambda b: (0,) * len(shape))

    param_shapes = [(1, 128), (1, 256), (1, 256), (4, 128, 128), (128, 128)]
    return pl.pallas_call(
        body, name="sgu_gate_bwd", grid=(n_seq,),
        in_specs=[seq_blk(512), seq_blk(128), seq_blk(256), seq_blk(D_FOX), seq_blk(D_FOX),
                  full((1, 128)), full((1, 256)), full((1, 256)), full((4, 128, 128)), full((128, 256))],
        out_specs=[seq_blk(512), seq_blk(128)] + [full(s) for s in param_shapes],
        out_shape=[_hbm((t_tok, 512), BF16), _hbm((t_tok, 128), BF16)]
        + [_hbm(s, F32) for s in param_shapes],
        scratch_shapes=[pltpu.VMEM((128, 256), F32)],
        compiler_params=_params(48, ("arbitrary",)),
    )(sgu, f, dyc, drow, dcol, b_f, sgu_g, sgu_b, w_s, b_mat)


def _mix_in_bwd(dconv, dq, dk, dv, dsgu, df, dz, w_in, tm=512):
    t_tok = dz.shape[0]

    def body(dconv_ref, dq_ref, dk_ref, dv_ref, dsgu_ref, df_ref, dz_ref, w_ref, dx_ref, dp_ref):
        dqb = dq_ref[...].astype(BF16)
        pieces = [(COL_CONV, dconv_ref[...]), (COL_QKV, dqb), (COL_QKV + 512, dk_ref[...]), (COL_QKV + 1024, dv_ref[...]),
                  (COL_SGU, dsgu_ref[...]), (COL_F, df_ref[...])]
        dx = ALPHA * dz_ref[...]
        for col, val in pieces:
            width = val.shape[1]
            dp_ref[:, col:col + width] = val
            dx = dx + _dot_nt(val, w_ref[:, col:col + width])
        dx_ref[...] = dx

    def tok(n):
        return pl.BlockSpec((tm, n), lambda i: (i, 0))

    return pl.pallas_call(
        body, name="mix_in_bwd", grid=(t_tok // tm,),
        in_specs=[tok(768), tok(512), tok(512), tok(512), tok(512), tok(128), tok(D_MODEL),
                  pl.BlockSpec((D_MODEL, D_IN_PAD), lambda i: (0, 0))],
        out_specs=[tok(D_MODEL), tok(D_IN_PAD)],
        out_shape=[_hbm((t_tok, D_MODEL), F32), _hbm((t_tok, D_IN_PAD), BF16)],
        compiler_params=_params(48, ("arbitrary",)),
    )(dconv, dq, dk, dv, dsgu, df, dz, w_in)


def _pad_rows(a, rows):
    return jnp.pad(a, ((0, rows - a.shape[0]), (0, 0)))


F_BLOCK = F_ORIG // D_IN_SHARD
F_AT = F_ORIG - F_BLOCK * D_IN_SHARD
assert (F_ORIG + N_HEADS) // D_IN_SHARD == F_BLOCK


def _w_in_from_blocks(g):
    fb = g[F_BLOCK]
    zeros = jnp.zeros((D_MODEL, D_IN_PAD - COL_F - N_HEADS), g.dtype)
    return jnp.concatenate([g[d] for d in range(F_BLOCK)] + [fb[:, :F_AT], fb[:, F_AT + N_HEADS:]]
                           + [g[d] for d in range(F_BLOCK + 1, N_DEV)] + [fb[:, F_AT:F_AT + N_HEADS], zeros], axis=1)


def _w_in_to_blocks(dw):
    def cols(lo, hi):
        shift = 0 if hi <= F_ORIG else N_HEADS
        return dw[:, lo - shift:hi - shift]

    blocks = []
    for d in range(N_DEV):
        lo, hi = d * D_IN_SHARD, (d + 1) * D_IN_SHARD
        if d == F_BLOCK:
            blocks.append(jnp.concatenate([cols(lo, F_ORIG), dw[:, COL_F:COL_F + N_HEADS], cols(F_ORIG + N_HEADS, hi)], axis=1))
        else:
            blocks.append(cols(lo, hi))
    return jnp.stack(blocks)


LN1_ROWS = 2 * 8
REST_ROWS = 4 * 8 + 2 * 8 + 512 + 8 + 8 + 8


def _pack_rest(p):
    rows = [p[name].reshape(8, 128) for name in ("ln2_g", "ln2_b", "ln3_g", "ln3_b")]
    rows += [_pad_rows(p[name].reshape(2, 128), 8) for name in ("sgu_ln_g", "sgu_ln_b")]
    rows += [p["sgu_w_s"].reshape(512, 128), _pad_rows(p["sgu_b_s"], 8),
             _pad_rows(jnp.pad(p["fox_b_f"], (0, 128 - N_HEADS)).reshape(1, 128), 8), _pad_rows(p["conv_w"].reshape(6, 128), 8)]
    return jnp.concatenate(rows, axis=0)


def _pack_layer(p):
    return jnp.concatenate([p["ln1_g"].reshape(8, 128), p["ln1_b"].reshape(8, 128), _pack_rest(p)], axis=0)


def _unpack_layer(a):
    r = 0

    def take(n, valid):
        nonlocal r
        piece = a[r:r + valid]
        r += n
        return piece

    d = {}
    for name in ("ln1_g", "ln1_b", "ln2_g", "ln2_b", "ln3_g", "ln3_b"):
        d[name] = take(8, 8).reshape(D_MODEL)
    for name in ("sgu_ln_g", "sgu_ln_b"):
        d[name] = take(8, 2).reshape(D_SGU)
    d["sgu_w_s"] = take(512, 512).reshape(N_SGU_GROUPS, SGU_CHUNK, SGU_CHUNK)
    d["sgu_b_s"] = take(8, 4).reshape(N_SGU_GROUPS, SGU_CHUNK)
    d["fox_b_f"] = take(8, 1).reshape(128)[:N_HEADS]
    d["conv_w"] = take(8, 6).reshape(3, D_CONV)
    return d


SMALL_NAMES = ("ln1_g", "ln1_b", "fox_b_f", "sgu_ln_g", "sgu_ln_b", "sgu_w_s", "sgu_b_s", "ln2_g", "ln2_b", "ln3_g", "ln3_b")
BIG_NAMES = ("ffn1_w_up", "ffn1_w_down", "mix_w_in", "mix_w_out", "ffn2_w_up", "ffn2_w_down")
UP_NAMES = ("ffn1_w_up", "ffn2_w_up")
WEIGHT_ORDER = ("ln1_g", "ln1_b", "ffn1_w_up", "ffn1_w_down", "mix_w_in", "fox_b_f", "conv_w", "sgu_ln_g", "sgu_ln_b",
                "sgu_w_s", "sgu_b_s", "mix_w_out", "ln2_g", "ln2_b", "ffn2_w_up", "ffn2_w_down", "ln3_g", "ln3_b")


class _Overlap:
    def __init__(self, w, after, me, where):
        self.me, self.where = me, where
        self.last = after
        width = D_CONV // N_DEV
        rows = jnp.pad(_pad_rows(w["conv_w"].reshape(DEPTH * 3, width), 8), ((0, 0), (0, 128 - width)))
        land = lax.dynamic_update_slice(lax.empty((N_DEV,) + rows.shape, F32), rows[None], (me, 0, 0))
        self.conv_started = self._start("conv_w_start", _peers_plan(), N_DEV - 1, [rows, land])
        self.conv_full = None
        groups = [[("ffn1_w_up", 0), ("ffn1_w_down", 0)],
                  [("mix_w_in", 0), ("mix_w_out", 0), ("ffn2_w_up", 0), ("ffn2_w_down", 0)]]
        groups += [[(name, l) for name in BIG_NAMES] for l in range(1, DEPTH)]
        self.gathers = []
        for gi, group in enumerate(groups):
            shards = [w[name][l].astype(BF16) for name, l in group]
            lands = [lax.dynamic_update_slice(lax.empty((N_DEV,) + s.shape, BF16), s[None], (me, 0, 0)) for s in shards]
            started = self._start(f"allgather_start_{gi}", _gather_plan(len(group)), 3 * len(group), shards + lands)
            self.gathers.append(dict(group=group, chips=started))
        self.all_started = self.last
        self.scatters = {}
        self.order = []
        self.small = []

    def conv_w(self, after):
        if self.conv_full is None:
            width = D_CONV // N_DEV
            gathered = _exchange_wait("conv_w_wait", _peers_plan(), N_DEV - 1, self.conv_started, after)[1]
            self.conv_full = jnp.transpose(gathered[:, :DEPTH * 3, :width], (1, 0, 2)).reshape(DEPTH, 3, D_CONV)
        return self.conv_full

    def _start(self, name, plan, n_copies, arrays):
        started = _exchange_start(name, plan, n_copies, arrays, self.last)
        self.last = started[3]
        return started

    def _group_of(self, layer, part):
        return layer + 1 if layer > 0 else (0 if part == "ffn1" else 1)

    def pass_on(self, layer, part, after):
        st = self.gathers[self._group_of(layer, part)]
        if "sibling" not in st:
            gi, m = self._group_of(layer, part), len(st["group"])
            arrays = _exchange_wait(f"allgather_wait_{gi}", _gather_plan(m), 3 * m, st["chips"], after)
            st["sibling"] = self._start(f"allgather_pass_start_{gi}", _pass_on_plan(m), 4 * m, arrays)
        return st["sibling"][3]

    def weights(self, layer, part, after):
        gi = self._group_of(layer, part)
        st = self.gathers[gi]
        if "full" not in st:
            after = self.all_started if after is None else after
            self.pass_on(layer, part, after)
            m = len(st["group"])
            arrays = _exchange_wait(f"allgather_pass_wait_{gi}", _pass_on_plan(m), 4 * m, st["sibling"], after)
            st["full"] = dict(zip(st["group"], arrays[m:]))
        g = st["full"]

        def ffn(n):
            return g[(f"ffn{n}_w_up", layer)].reshape(2, D_FF, D_MODEL), g[(f"ffn{n}_w_down", layer)].reshape(D_FF, D_MODEL)

        if part == "ffn1":
            return ffn(1)
        return (_w_in_from_blocks(g[("mix_w_in", layer)]), g[("mix_w_out", layer)].reshape(D_MODEL, D_MODEL), *ffn(2))

    def push(self, key, items):
        n = len(items)
        grads = [g for _, _, g in items]
        lands = [lax.empty((4,) + g.shape[1:], F32) for g in grads]
        started = self._start(f"rs_sibling_start_{key[0]}{key[1]}", _sibling_plan(n), 4 * n, grads + lands)
        self.scatters[key] = dict(items=items, sibling=started)
        self.order.append(key)
        return started[3]

    def advance(self, key, after):
        st = self.scatters[key]
        n = len(st["items"])
        arrays = _exchange_wait(f"rs_sibling_wait_{key[0]}{key[1]}", _sibling_plan(n), 4 * n, st["sibling"], after)
        partials = [_chip_partial(g, r, self.where) for g, r in zip(arrays[:n], arrays[n:])]
        p16 = [p for _, p in partials]
        lands = [lax.empty((3,) + p.shape[1:], BF16) for p in p16]
        started = self._start(f"rs_chip_start_{key[0]}{key[1]}", _chip_plan(n), 3 * n, p16 + lands)
        st.update(own32=[p for p, _ in partials], chip=started)
        return started[3]

    def push_small(self, rows):
        k = len(self.small)
        land = lax.dynamic_update_slice(lax.empty((N_DEV,) + rows.shape, F32), rows[None], (self.me, 0, 0))
        started = self._start(f"small_start_{k}", _peers_plan(), N_DEV - 1, [rows, land])
        self.small.append(started)
        return started[3]

    def finish(self, w, m, v):
        res = {}
        after = self.scatters[self.order[-1]]["chip"][3]
        for key in self.order:
            st = self.scatters[key]
            n = len(st["items"])
            arrays = _exchange_wait(f"rs_chip_wait_{key[0]}{key[1]}", _chip_plan(n), 3 * n, st["chip"], after)
            for (name, l, _), own32, r16 in zip(st["items"], st["own32"], arrays[n:]):
                res[name] = _adamw_shard(own32, r16, w[name], m[name], v[name], l, res.get(name))
                after = res[name][0]
        pieces = [_exchange_wait(f"small_wait_{k}", _peers_plan(), N_DEV - 1, started, after)[1]
                  for k, started in enumerate(self.small)]
        return res, pieces


def _dw_up(dgu, x, after):
    return _matmul_tn(dgu, x[None], after, tk=2048, bm=DW_ROWS).reshape(N_DEV, FFN_BLK, D_MODEL)


def _dw_down(a, dy, after):
    return _matmul_tn(a[None], dy[None], after, tk=2048, bm=DW_ROWS).reshape(N_DEV, FFN_BLK // 2, D_MODEL)


def _local_step(x, target, comm, small, n_seq):
    def vec(a):
        return a.reshape(1, -1)

    saved = []
    h = x
    for l in range(DEPTH):
        s = {}
        s["up1"], s["down1"] = comm.weights(l, "ffn1", None if l == 0 else h)
        h1, h1b, s["z1"], s["gu1"], s["x0b"] = _ffn_fwd(h, s["up1"], s["down1"], vec(small["ln1_g"][l]), vec(small["ln1_b"][l]), h)
        s["w_in"], s["w_out"], s["up2"], s["down2"] = comm.weights(l, "rest", s["z1"])
        s["x1b"] = h1b
        conv, qkv, sgu, f = _in_proj(h1, s["w_in"])
        cw = _pad_rows(comm.conv_w(h1)[l], 8)
        bf = jnp.pad(small["fox_b_f"][l], (0, 128 - N_HEADS)).reshape(1, 128)
        b_mat = jnp.repeat(small["sgu_b_s"][l].T, D_SGU // N_SGU_GROUPS, axis=1)
        mid_params = (cw, bf, vec(small["sgu_ln_g"][l]), vec(small["sgu_ln_b"][l]), small["sgu_w_s"][l], b_mat)
        cat, cum_t = _mix_mid_fwd(conv, sgu, f, *mid_params, n_seq)
        cat, lse = _fox_fwd(qkv, cum_t, cat, n_seq)
        h2, h2b, s["z2"] = _mix_out_fwd(cat, h1, s["w_out"], vec(small["ln2_g"][l]), vec(small["ln2_b"][l]))
        s.update(conv=conv, qkv=qkv, sgu=sgu, f=f, mid_params=mid_params, cat=cat, cum_t=cum_t, lse=lse, x2b=h2b)
        token = comm.pass_on(l + 1, "ffn1", s["z2"]) if l + 1 < DEPTH else h2
        ln3 = (vec(small["ln3_g"][l]), vec(small["ln3_b"][l]))
        if l + 1 < DEPTH:
            h, _, s["z3"], s["gu2"], _ = _ffn_fwd(h2, s["up2"], s["down2"], *ln3, token)
        else:
            dh, loss, s["z3"], s["gu2"] = _ffn_fwd(h2, s["up2"], s["down2"], *ln3, token, target)
        saved.append(s)

    late_rows = None
    token = loss
    pending = None
    for l in reversed(range(DEPTH)):
        s = saved[l]
        sg = {}
        dh, dy, a, dgu, sg["ln3_g"], sg["ln3_b"] = _ffn_bwd(dh, s["z3"], s["gu2"], s["up2"], s["down2"], vec(small["ln3_g"][l]), token)
        if pending is not None:
            token = comm.advance(pending, dh)
        g_up2 = _dw_up(dgu, s["x2b"], token)
        g_down2 = _dw_down(a, dy, token)
        dz, dzb, dya, dyb, dyc, sg["ln2_g"], sg["ln2_b"] = _mix_out_bwd(dh, s["z2"], s["w_out"], vec(small["ln2_g"][l]))
        g_out = _matmul_tn(s["cat"][None], dzb[None], token).reshape(N_DEV, D_MODEL // N_DEV, D_MODEL)
        dq, dk, dv, drow, dcol = _fox_bwd(s["qkv"], s["cum_t"], s["cat"], s["lse"], dyb, n_seq)
        dconv, dcw = _conv_bwd(s["conv"], dya, s["mid_params"][0], n_seq)
        dsgu, df, dbf, dlg, dlb, dws, dbs = _sgu_gate_bwd(s["sgu"], s["f"], dyc, drow, dcol, *s["mid_params"][1:], n_seq)
        sg.update(conv_w=dcw[:3], fox_b_f=dbf[0, :N_HEADS], sgu_ln_g=dlg[0], sgu_ln_b=dlb[0], sgu_w_s=dws,
                  sgu_b_s=dbs[:, :N_SGU_GROUPS].T)
        dh, dp = _mix_in_bwd(dconv, dq, dk, dv, dsgu, df, dz, s["w_in"])
        g_in = _w_in_to_blocks(_matmul_tn(s["x1b"][None], dp[None], token, tk=1024)[0])
        first = [("ffn2_w_up", l, g_up2), ("ffn2_w_down", l, g_down2), ("mix_w_out", l, g_out), ("mix_w_in", l, g_in)]
        for name in ("ln2_g", "ln2_b", "ln3_g", "ln3_b"):
            sg[name] = sg[name][0]
        if l == 0:
            comm.push((l, "a"), first)
            token = comm.push_small(_pack_rest(sg))
            pending, first = (l, "a"), []
        dh, dy, a, dgu, dg1, db1 = _ffn_bwd(dh, s["z1"], s["gu1"], s["up1"], s["down1"], vec(small["ln1_g"][l]), token)
        if l == 0:
            token = comm.advance(pending, dh)
        g_up1 = _dw_up(dgu, s["x0b"], token)
        ln1_rows = jnp.concatenate([dg1.reshape(8, 128), db1.reshape(8, 128)], axis=0)
        if l == 0:
            token = comm.push((l, "b"), [("ffn1_w_up", l, g_up1)])
            g_down1 = _dw_down(a, dy, token)
            token = comm.advance((l, "b"), g_down1)
            token = comm.push((l, "c"), [("ffn1_w_down", l, g_down1)])
            token = comm.advance((l, "c"), token)
            late_rows = ln1_rows
        else:
            g_down1 = _dw_down(a, dy, token)
            pending = (l, "b")
            comm.push(pending, first + [("ffn1_w_up", l, g_up1), ("ffn1_w_down", l, g_down1)])
            token = comm.push_small(jnp.concatenate([ln1_rows, _pack_rest(sg)], axis=0))
    return loss, dh, late_rows


def kernel(x, ln1_g, ln1_b, ffn1_w_up, ffn1_w_down, mix_w_in, fox_b_f, conv_w, sgu_ln_g, sgu_ln_b, sgu_w_s, sgu_b_s, mix_w_out, ln2_g, ln2_b, ffn2_w_up, ffn2_w_down, ln3_g, ln3_b, loss_target, m_ln1_g, m_ln1_b, m_ffn1_w_up, m_ffn1_w_down, m_mix_w_in, m_fox_b_f, m_conv_w, m_sgu_ln_g, m_sgu_ln_b, m_sgu_w_s, m_sgu_b_s, m_mix_w_out, m_ln2_g, m_ln2_b, m_ffn2_w_up, m_ffn2_w_down, m_ln3_g, m_ln3_b, v_ln1_g, v_ln1_b, v_ffn1_w_up, v_ffn1_w_down, v_mix_w_in, v_fox_b_f, v_conv_w, v_sgu_ln_g, v_sgu_ln_b, v_sgu_w_s, v_sgu_b_s, v_mix_w_out, v_ln2_g, v_ln2_b, v_ffn2_w_up, v_ffn2_w_down, v_ln3_g, v_ln3_b):
    w = dict(ln1_g=ln1_g, ln1_b=ln1_b, ffn1_w_up=ffn1_w_up, ffn1_w_down=ffn1_w_down, mix_w_in=mix_w_in, fox_b_f=fox_b_f,
             conv_w=conv_w, sgu_ln_g=sgu_ln_g, sgu_ln_b=sgu_ln_b, sgu_w_s=sgu_w_s, sgu_b_s=sgu_b_s, mix_w_out=mix_w_out,
             ln2_g=ln2_g, ln2_b=ln2_b, ffn2_w_up=ffn2_w_up, ffn2_w_down=ffn2_w_down, ln3_g=ln3_g, ln3_b=ln3_b)
    m = dict(ln1_g=m_ln1_g, ln1_b=m_ln1_b, ffn1_w_up=m_ffn1_w_up, ffn1_w_down=m_ffn1_w_down, mix_w_in=m_mix_w_in,
             fox_b_f=m_fox_b_f, conv_w=m_conv_w, sgu_ln_g=m_sgu_ln_g, sgu_ln_b=m_sgu_ln_b, sgu_w_s=m_sgu_w_s,
             sgu_b_s=m_sgu_b_s, mix_w_out=m_mix_w_out, ln2_g=m_ln2_g, ln2_b=m_ln2_b, ffn2_w_up=m_ffn2_w_up,
             ffn2_w_down=m_ffn2_w_down, ln3_g=m_ln3_g, ln3_b=m_ln3_b)
    v = dict(ln1_g=v_ln1_g, ln1_b=v_ln1_b, ffn1_w_up=v_ffn1_w_up, ffn1_w_down=v_ffn1_w_down, mix_w_in=v_mix_w_in,
             fox_b_f=v_fox_b_f, conv_w=v_conv_w, sgu_ln_g=v_sgu_ln_g, sgu_ln_b=v_sgu_ln_b, sgu_w_s=v_sgu_w_s,
             sgu_b_s=v_sgu_b_s, mix_w_out=v_mix_w_out, ln2_g=v_ln2_g, ln2_b=v_ln2_b, ffn2_w_up=v_ffn2_w_up,
             ffn2_w_down=v_ffn2_w_down, ln3_g=v_ln3_g, ln3_b=v_ln3_b)

    mx, my, mc = lax.axis_index("x"), lax.axis_index("y"), lax.axis_index("c")
    me = 4 * mx + 2 * my + mc
    n_seq, seq, _ = x.shape
    t_tok = n_seq * seq
    for name in UP_NAMES:
        for t in (w, m, v):
            t[name] = jnp.transpose(t[name], (0, 2, 1))

    comm = _Overlap(w, x, me, jnp.stack([mc, 2 * mx + my]).astype(jnp.int32))
    small = {name: w[name] for name in SMALL_NAMES}

    loss_dev, grad_x, late_rows = _local_step(
        x.reshape(t_tok, D_MODEL), loss_target.reshape(t_tok, D_MODEL), comm, small, n_seq)
    loss = lax.psum(loss_dev[0, 0], ("x", "y", "c"))
    out, pieces = comm.finish(w, m, v)
    for name in UP_NAMES:
        out[name] = [jnp.transpose(a, (0, 2, 1)) for a in out[name]]

    pieces.append(_allgather_small(late_rows))
    spans = [(l, 0, LN1_ROWS + REST_ROWS) for l in reversed(range(1, DEPTH))] + [(0, LN1_ROWS, LN1_ROWS + REST_ROWS), (0, 0, LN1_ROWS)]

    def widen(a):
        return lax.dynamic_update_slice(jnp.zeros((3, D_CONV), F32), a, (0, me * (D_CONV // N_DEV)))

    packed = [[_pack_layer({**{name: t[name][l] for name in SMALL_NAMES}, "conv_w": widen(t["conv_w"][l])}) for l in range(DEPTH)]
              for t in (w, m, v)]
    rows_out = {}
    for (l, lo, hi), gathered_piece in zip(spans, pieces):
        rows_out[(l, lo)] = _adamw_small(gathered_piece, *[packed[t][l][lo:hi] for t in range(3)])
    per_layer = []
    for l in range(DEPTH):
        parts = sorted(lo for (ll, lo) in rows_out if ll == l)
        per_layer.append([_unpack_layer(jnp.concatenate([rows_out[(l, lo)][k] for lo in parts], axis=0)) for k in range(4)])
    for name in SMALL_NAMES:
        out[name] = [jnp.stack([per_layer[l][k][name] for l in range(DEPTH)]) for k in range(4)]
    lo_col = me * (D_CONV // N_DEV)
    out["conv_w"] = [jnp.stack([lax.dynamic_slice(per_layer[l][k]["conv_w"], (0, lo_col), (3, D_CONV // N_DEV)) for l in range(DEPTH)])
                     for k in range(4)]

    return (loss, grad_x.reshape(x.shape), *[out[name][0] for name in WEIGHT_ORDER], *[out[name][1] for name in WEIGHT_ORDER],
            *[out[name][2] for name in WEIGHT_ORDER], *[out[name][3] for name in WEIGHT_ORDER])
```

```python
import functools

import jax
import jax.numpy as jnp
from jax import lax
from jax.experimental import pallas as pl
from jax.experimental.pallas import tpu as pltpu

F32 = jnp.float32
BF16 = jnp.bfloat16
MESH = pl.DeviceIdType.MESH

N_DEV = 8
DEPTH = 2
D_MODEL = 1024
D_FF = 2816
FFN_BLK = 2 * D_FF // N_DEV
MXU_TILE_V7X = 256
FFN_CHUNKS = tuple((lo, min(lo + 3 * MXU_TILE_V7X, D_FF)) for lo in range(0, D_FF, 3 * MXU_TILE_V7X))
FFN_BWD_CHUNKS = tuple((lo, min(lo + 4 * MXU_TILE_V7X, D_FF)) for lo in range(0, D_FF, 4 * MXU_TILE_V7X))
DW_ROWS = D_FF // 2
DW_COLS = 2 * MXU_TILE_V7X
D_CONV = 256
D_FOX = 512
N_HEADS = 8
D_SGU = 256
N_SGU_GROUPS = 4
SGU_CHUNK = 128
D_IN = 3 * D_CONV + 3 * D_FOX + N_HEADS + 2 * D_SGU
D_IN_SHARD = D_IN // N_DEV
COL_CONV, COL_QKV, COL_SGU, COL_F = 0, 768, 2304, 2816
D_IN_PAD = 2944
F_ORIG = 3 * D_CONV + 3 * D_FOX
ALPHA = (2 * DEPTH) ** 0.25
LN_EPS = 1e-5
ATT_SCALE = 0.125
ATT_BLK = 512
ATT_PAIRS = 2
NEG = -1e30

ADAM_LR, ADAM_B1, ADAM_B2, ADAM_EPS, ADAM_WD, ADAM_STEP = 0.001, 0.9, 0.999, 1e-08, 0.01, 10

VMEM_BYTES_V7X = 64 * 1024 * 1024
HIGHEST = lax.Precision.HIGHEST


def _params(vmem_mb, sem=None):
    assert vmem_mb * 1024 * 1024 < VMEM_BYTES_V7X
    kw = dict(vmem_limit_bytes=vmem_mb * 1024 * 1024)
    if sem is not None:
        kw["dimension_semantics"] = sem
    return pltpu.CompilerParams(**kw)


def _dot(a, b, precision=None):
    return lax.dot_general(a, b, (((1,), (0,)), ((), ())), preferred_element_type=F32, precision=precision)


def _dot_nt(a, b):
    return lax.dot_general(a, b, (((1,), (1,)), ((), ())), preferred_element_type=F32)


def _dot_tn(a, b):
    return lax.dot_general(a, b, (((0,), (0,)), ((), ())), preferred_element_type=F32)


def _ln_stats(z):
    mu = jnp.mean(z, axis=-1, keepdims=True)
    zc = z - mu
    var = jnp.mean(zc * zc, axis=-1, keepdims=True)
    rstd = lax.rsqrt(var + LN_EPS)
    return zc * rstd, rstd


def _ln_bwd(dy, xhat, rstd, g):
    dxh = dy * g
    m1 = jnp.mean(dxh, axis=-1, keepdims=True)
    m2 = jnp.mean(dxh * xhat, axis=-1, keepdims=True)
    return rstd * (dxh - m1 - xhat * m2)


_GELU_C = 0.7978845608028654


def _gelu(x):
    return 0.5 * x * (1.0 + jnp.tanh(_GELU_C * (x + 0.044715 * x * x * x)))


def _gelu_with_grad(x):
    t = jnp.tanh(_GELU_C * (x + 0.044715 * x * x * x))
    return 0.5 * x * (1.0 + t), 0.5 * (1.0 + t) + 0.5 * x * (1.0 - t * t) * _GELU_C * (1.0 + 3 * 0.044715 * x * x)


def _hbm(shape, dtype):
    n = 1
    for d in shape:
        n *= d
    if n * jnp.dtype(dtype).itemsize >= 1024 * 1024:
        return pltpu.HBM(tuple(shape), dtype)
    return jax.ShapeDtypeStruct(tuple(shape), dtype)


def _vspec():
    return pl.BlockSpec(memory_space=pltpu.VMEM)


def _anyspec():
    return pl.BlockSpec(memory_space=pl.ANY)


def _mesh_pos():
    return lax.axis_index("x"), lax.axis_index("y"), lax.axis_index("c")


def _other_chips(x, y):
    return [(1 - x, y), (x, 1 - y), (1 - x, 1 - y)]


_HBM_SPEC = pl.BlockSpec(memory_space=pltpu.HBM)
_SEM_SPEC = pl.BlockSpec(memory_space=pltpu.SEMAPHORE)
_DATAFLOW_EFFECT = pltpu.SideEffectType.DATAFLOW_SIDE_EFFECTING


def _remote_copies(plan, refs, send_sems, recv_sems):
    return [pltpu.make_async_remote_copy(src_ref=src, dst_ref=dst, send_sem=send_sems.at[k], recv_sem=recv_sems.at[k],
                                         device_id=to, device_id_type=MESH)
            for k, (src, dst, to) in enumerate(plan(refs, *_mesh_pos()))]


def _exchange_start(name, plan, n_copies, arrays, after):
    n = len(arrays)

    def body(*refs):
        send_sems, recv_sems, token = refs[n + 1], refs[n + 2], refs[-1]
        for cp in _remote_copies(plan, refs[:n], send_sems, recv_sems):
            cp.start()
        token[...] = jnp.zeros_like(token)

    out = pl.pallas_call(
        body, name=name,
        out_shape=(pltpu.SemaphoreType.DMA((n_copies,)), pltpu.SemaphoreType.DMA((n_copies,)),
                   *[pltpu.HBM(a.shape, a.dtype) for a in arrays], _hbm((8, 128), F32)),
        in_specs=[_HBM_SPEC] * n + [_anyspec()],
        out_specs=(_SEM_SPEC, _SEM_SPEC, *[_HBM_SPEC] * n, _vspec()),
        input_output_aliases={i: 2 + i for i in range(n)},
        compiler_params=pltpu.CompilerParams(has_side_effects=_DATAFLOW_EFFECT),
    )(*[pltpu.with_memory_space_constraint(a, pltpu.HBM) for a in arrays], after)
    return out[0], out[1], list(out[2:2 + n]), out[-1]


def _exchange_wait(name, plan, n_copies, started, after):
    send_sems, recv_sems, arrays, _ = started
    n = len(arrays)

    def body(*refs):
        for cp in _remote_copies(plan, refs[:n], refs[n], refs[n + 1]):
            cp.wait_send()
            cp.wait_recv()

    out = pl.pallas_call(
        body, name=name,
        out_shape=tuple(pltpu.HBM(a.shape, a.dtype) for a in arrays),
        in_specs=[_HBM_SPEC] * n + [_SEM_SPEC, _SEM_SPEC, _anyspec()], out_specs=tuple([_HBM_SPEC] * n),
        input_output_aliases={i: i for i in range(n)},
        compiler_params=pltpu.CompilerParams(has_side_effects=_DATAFLOW_EFFECT),
    )(*arrays, send_sems, recv_sems, after)
    return list(out)


def _gather_plan(m):
    def plan(refs, x, y, c):
        me = 4 * x + 2 * y + c
        return [(refs[i], refs[m + i].at[me], (*chip, c)) for i in range(m) for chip in _other_chips(x, y)]
    return plan


def _pass_on_plan(m):
    def plan(refs, x, y, c):
        out = []
        for i in range(m):
            out.append((refs[i], refs[m + i].at[4 * x + 2 * y + c], (x, y, 1 - c)))
            for cx, cy in _other_chips(x, y):
                block = refs[m + i].at[4 * cx + 2 * cy + c]
                out.append((block, block, (x, y, 1 - c)))
        return out
    return plan


def _peers_plan():
    def plan(refs, x, y, c):
        rel = [(dx, dy, dc) for dx in (0, 1) for dy in (0, 1) for dc in (0, 1)][1:]
        return [(refs[0], refs[1].at[4 * x + 2 * y + c], (x ^ dx, y ^ dy, c ^ dc)) for dx, dy, dc in rel]
    return plan


def _allgather_small(v):
    rows = v.shape[0]

    def body(v_ref, out_ref, send_sems, recv_sems):
        x, y, c = _mesh_pos()
        me = 4 * x + 2 * y + c
        out_ref[me] = v_ref[...]
        rel = [(dx, dy, dc) for dx in (0, 1) for dy in (0, 1) for dc in (0, 1)][1:]
        copies = []
        for k, (dx, dy, dc) in enumerate(rel):
            to = (x ^ dx, y ^ dy, c ^ dc)
            copies.append(pltpu.make_async_remote_copy(
                src_ref=v_ref, dst_ref=out_ref.at[me], send_sem=send_sems.at[k], recv_sem=recv_sems.at[k],
                device_id=to, device_id_type=MESH))
        for cp in copies:
            cp.start()
        for k, (dx, dy, dc) in enumerate(rel):
            src_blk = 4 * (x ^ dx) + 2 * (y ^ dy) + (c ^ dc)
            pltpu.make_async_remote_copy(
                src_ref=v_ref, dst_ref=out_ref.at[src_blk], send_sem=send_sems.at[k], recv_sem=recv_sems.at[k],
                device_id=(x, y, c), device_id_type=MESH).wait_recv()
        for cp in copies:
            cp.wait_send()

    return pl.pallas_call(
        body, name="allgather_small",
        out_shape=jax.ShapeDtypeStruct((N_DEV, rows, 128), v.dtype),
        in_specs=[_vspec()], out_specs=_vspec(),
        scratch_shapes=[pltpu.SemaphoreType.DMA((7,)), pltpu.SemaphoreType.DMA((7,))],
        compiler_params=_params(24),
    )(v)


def _sibling_plan(n):
    def plan(refs, x, y, c):
        return [(refs[a].at[2 * q + (1 - c)], refs[n + a].at[q], (x, y, 1 - c)) for a in range(n) for q in range(4)]
    return plan


def _chip_plan(n):
    def plan(refs, x, y, c):
        return [(refs[a].at[2 * cx + cy], refs[n + a].at[j], (cx, cy, c))
                for a in range(n) for j, (cx, cy) in enumerate(_other_chips(x, y))]
    return plan


def _row_tile(rows, cols, budget_bytes=2 * 1024 * 1024):
    best = 8
    for t in range(8, rows + 1, 8):
        if rows % t == 0 and t * cols * 4 <= budget_bytes:
            best = t
    return best


def _chip_partial(g, recv, where):
    _, rows, cols = g.shape
    tr = _row_tile(rows, cols)

    def body(where_ref, g_ref, r_ref, own_ref, o16_ref):
        s = g_ref[...] + r_ref[...]
        o16_ref[...] = s.astype(BF16)

        @pl.when(pl.program_id(1) == where_ref[1])
        def _():
            own_ref[...] = s

    blk = (None, tr, cols)
    return pl.pallas_call(
        body, name="rs_chip_partial",
        grid_spec=pltpu.PrefetchScalarGridSpec(
            num_scalar_prefetch=1, grid=(rows // tr, 4),
            in_specs=[pl.BlockSpec(blk, lambda i, q, w: (2 * q + w[0], i, 0)),
                      pl.BlockSpec(blk, lambda i, q, w: (q, i, 0))],
            out_specs=[pl.BlockSpec((tr, cols), lambda i, q, w: (i, 0)), pl.BlockSpec(blk, lambda i, q, w: (q, i, 0))]),
        out_shape=[_hbm((rows, cols), F32), _hbm((4, rows, cols), BF16)],
        compiler_params=_params(32),
    )(where, g, recv)


def _adam_math(w, g, m, v):
    m = ADAM_B1 * m + (1.0 - ADAM_B1) * g
    v = ADAM_B2 * v + (1.0 - ADAM_B2) * (g * g)
    m_hat = m / (1.0 - ADAM_B1 ** ADAM_STEP)
    v_hat = v / (1.0 - ADAM_B2 ** ADAM_STEP)
    delta = -ADAM_LR * (m_hat / (jnp.sqrt(v_hat) + ADAM_EPS) + ADAM_WD * w)
    return delta, m, v


def _adamw_shard(own32, recv16, w, m, v, layer, earlier):
    depth, rows, cols = w.shape
    tr = _row_tile(rows, cols, 1024 * 1024)
    n_prev = 0 if earlier is None else 4

    def body(p_ref, r_ref, w_ref, m_ref, v_ref, *rest):
        g_out, d_out, m_out, v_out = rest[n_prev:]
        g = p_ref[...] + r_ref[0].astype(F32) + r_ref[1].astype(F32) + r_ref[2].astype(F32)
        d, mn, vn = _adam_math(w_ref[...], g, m_ref[...], v_ref[...])
        g_out[...] = g
        d_out[...] = d
        m_out[...] = mn
        v_out[...] = vn

    mine = pl.BlockSpec((None, tr, cols), lambda i: (layer, i, 0))
    return pl.pallas_call(
        body, name="adamw_shard", grid=(rows // tr,),
        in_specs=[pl.BlockSpec((tr, cols), lambda i: (i, 0)), pl.BlockSpec((3, tr, cols), lambda i: (0, i, 0)),
                  mine, mine, mine] + [_anyspec()] * n_prev,
        out_specs=[mine] * 4,
        out_shape=[_hbm((depth, rows, cols), F32)] * 4,
        input_output_aliases={5 + k: k for k in range(n_prev)},
        compiler_params=_params(32),
    )(own32, recv16, *[pltpu.with_memory_space_constraint(t, pltpu.HBM) for t in (w, m, v)],
      *([] if earlier is None else earlier))


def _adamw_small(gathered, w, m, v):
    rows = w.shape[0]

    def body(a_ref, w_ref, m_ref, v_ref, g_out, d_out, m_out, v_out):
        g = a_ref[0]
        for d in range(1, N_DEV):
            g = g + a_ref[d]
        dl, mn, vn = _adam_math(w_ref[...], g, m_ref[...], v_ref[...])
        g_out[...] = g
        d_out[...] = dl
        m_out[...] = mn
        v_out[...] = vn

    return pl.pallas_call(
        body, name="adamw_small",
        in_specs=[_vspec()] * 4, out_specs=[_vspec()] * 4,
        out_shape=[_hbm((rows, 128), F32)] * 4,
        compiler_params=_params(32),
    )(gathered, w, m, v)


def _load_weights_once(pairs, sems):
    @pl.when(pl.program_id(0) == 0)
    def _():
        cps = [pltpu.make_async_copy(src, dst, sems.at[i]) for i, (src, dst) in enumerate(pairs)]
        for cp in cps:
            cp.start()
        for cp in cps:
            cp.wait()


def _ffn_fwd(x, wup, wd, ln_g, ln_b, after, target=None, tm=512):
    t_tok = x.shape[0]
    last = target is not None

    def body(x_ref, g_ref, b_ref, wup_hbm, wd_hbm, _after, *rest):
        if last:
            t_ref, dxn_ref, loss_ref, z_ref, gu_ref, wup_v, wd_v, sems = rest
        else:
            xn_ref, xnb_ref, z_ref, gu_ref, xb_ref, wup_v, wd_v, sems = rest
        _load_weights_once([(wup_hbm, wup_v), (wd_hbm, wd_v)], sems)
        xb = x_ref[...].astype(BF16)
        if not last:
            xb_ref[...] = xb
        y = None
        for lo, hi in FFN_CHUNKS:
            g = _dot_nt(xb, wup_v[0, lo:hi])
            u = _dot_nt(xb, wup_v[1, lo:hi])
            gu_ref[0, :, lo:hi] = g.astype(BF16)
            gu_ref[1, :, lo:hi] = u.astype(BF16)
            a = (g * jax.nn.sigmoid(g) * u).astype(BF16)
            part = _dot(a, wd_v[lo:hi])
            y = part if y is None else y + part
        z = ALPHA * x_ref[...] + 0.5 * y
        xhat, _ = _ln_stats(z)
        xn = xhat * g_ref[...] + b_ref[...]
        z_ref[...] = z
        if last:
            err = xn - t_ref[...]
            dxn_ref[...] = err * (1.0 / D_MODEL)
            part = jnp.sum(jnp.sum(err * err, axis=1, keepdims=True), axis=0, keepdims=True) * (0.5 / D_MODEL)

            @pl.when(pl.program_id(0) == 0)
            def _():
                loss_ref[...] = jnp.zeros_like(loss_ref)

            loss_ref[...] += part
        else:
            xn_ref[...] = xn
            xnb_ref[...] = xn.astype(BF16)

    tok = pl.BlockSpec((tm, D_MODEL), lambda i: (i, 0))
    vec = pl.BlockSpec((1, D_MODEL), lambda i: (0, 0))
    gu_spec = pl.BlockSpec((2, tm, D_FF), lambda i: (0, i, 0))
    gu_shape = _hbm((2, t_tok, D_FF), BF16)
    f32_tok, bf16_tok = _hbm((t_tok, D_MODEL), F32), _hbm((t_tok, D_MODEL), BF16)
    if last:
        extra_in, extra_spec = [target], [tok]
        out_specs = [tok, pl.BlockSpec((1, 128), lambda i: (0, 0)), tok, gu_spec]
        out_shape = [f32_tok, _hbm((1, 128), F32), f32_tok, gu_shape]
    else:
        extra_in, extra_spec = [], []
        out_specs = [tok, tok, tok, gu_spec, tok]
        out_shape = [f32_tok, bf16_tok, f32_tok, gu_shape, bf16_tok]
    return pl.pallas_call(
        body, name="ffn_fwd_loss" if last else "ffn_fwd", grid=(t_tok // tm,),
        in_specs=[tok, vec, vec, _anyspec(), _anyspec(), _anyspec()] + extra_spec,
        out_specs=out_specs, out_shape=out_shape,
        scratch_shapes=[pltpu.VMEM((2, D_FF, D_MODEL), BF16), pltpu.VMEM((D_FF, D_MODEL), BF16),
                        pltpu.SemaphoreType.DMA((2,))],
        compiler_params=_params(62, ("arbitrary",)),
    )(x, ln_g, ln_b, wup, wd, after, *extra_in)


def _ffn_bwd(dxn, z, gu, wup, wd, ln_g, after, tm=256):
    t_tok = dxn.shape[0]

    def body(dxn_ref, z_ref, gu_ref, g_ref, wup_hbm, wd_hbm, _after,
             dx_ref, dy_ref, a_ref, dgu_ref, dg_ref, db_ref, wup_v, wd_v, sems):
        i = pl.program_id(0)
        _load_weights_once([(wup_hbm, wup_v), (wd_hbm, wd_v)], sems)
        dxn_t = dxn_ref[...]
        xhat, rstd = _ln_stats(z_ref[...])
        pg = jnp.sum(dxn_t * xhat, axis=0, keepdims=True)
        pb = jnp.sum(dxn_t, axis=0, keepdims=True)

        @pl.when(i == 0)
        def _():
            dg_ref[...] = pg
            db_ref[...] = pb

        @pl.when(i > 0)
        def _():
            dg_ref[...] += pg
            db_ref[...] += pb

        dz = _ln_bwd(dxn_t, xhat, rstd, g_ref[...])
        dy = (0.5 * dz).astype(BF16)
        dy_ref[...] = dy
        dx = ALPHA * dz
        for lo, hi in FFN_BWD_CHUNKS:
            da = _dot_nt(dy, wd_v[lo:hi])
            g = gu_ref[0, :, lo:hi].astype(F32)
            u = gu_ref[1, :, lo:hi].astype(F32)
            sig = jax.nn.sigmoid(g)
            silu = g * sig
            a_ref[:, lo:hi] = (silu * u).astype(BF16)
            dg = (da * u * (sig * (1.0 + g * (1.0 - sig)))).astype(BF16)
            du = (da * silu).astype(BF16)
            dgu_ref[0, :, lo:hi] = dg
            dgu_ref[1, :, lo:hi] = du
            dx = dx + _dot(dg, wup_v[0, lo:hi]) + _dot(du, wup_v[1, lo:hi])
        dx_ref[...] = dx

    tok = pl.BlockSpec((tm, D_MODEL), lambda i: (i, 0))
    vec = pl.BlockSpec((1, D_MODEL), lambda i: (0, 0))
    gu_spec = pl.BlockSpec((2, tm, D_FF), lambda i: (0, i, 0))
    return pl.pallas_call(
        body, name="ffn_bwd", grid=(t_tok // tm,),
        in_specs=[tok, tok, gu_spec, vec, _anyspec(), _anyspec(), _anyspec()],
        out_specs=[tok, tok, pl.BlockSpec((tm, D_FF), lambda i: (i, 0)), gu_spec, vec, vec],
        out_shape=[_hbm((t_tok, D_MODEL), F32), _hbm((t_tok, D_MODEL), BF16),
                   _hbm((t_tok, D_FF), BF16), _hbm((2, t_tok, D_FF), BF16),
                   _hbm((1, D_MODEL), F32), _hbm((1, D_MODEL), F32)],
        scratch_shapes=[pltpu.VMEM((2, D_FF, D_MODEL), BF16), pltpu.VMEM((D_FF, D_MODEL), BF16),
                        pltpu.SemaphoreType.DMA((2,))],
        compiler_params=_params(60, ("arbitrary",)),
    )(dxn, z, gu, ln_g, wup, wd, after)


def _matmul_tn(a, b, after, tk=4096, bm=None, bn=None):
    ga, t_tok, m = a.shape
    gb, _, n = b.shape
    groups = max(ga, gb)
    tk = min(tk, t_tok)
    bm = m if bm is None else bm
    bn = n if bn is None else bn

    def body(a_ref, b_ref, _after, o_ref):
        p = _dot_tn(a_ref[...].astype(BF16), b_ref[...].astype(BF16))

        @pl.when(pl.program_id(3) == 0)
        def _():
            o_ref[...] = p

        @pl.when(pl.program_id(3) > 0)
        def _():
            o_ref[...] += p

    return pl.pallas_call(
        body, name=f"matmul_tn_{m}x{n}", grid=(groups, m // bm, n // bn, t_tok // tk),
        in_specs=[pl.BlockSpec((None, tk, bm), (lambda g, i, j, t: (g, t, i)) if ga > 1 else (lambda g, i, j, t: (0, t, i))),
                  pl.BlockSpec((None, tk, bn), (lambda g, i, j, t: (g, t, j)) if gb > 1 else (lambda g, i, j, t: (0, t, j))),
                  _anyspec()],
        out_specs=pl.BlockSpec((None, bm, bn), lambda g, i, j, t: (g, i, j)),
        out_shape=_hbm((groups, m, n), F32),
        compiler_params=_params(56, ("arbitrary",) * 4),
    )(a, b, after)


def _in_proj(x, w_in, tm=512):
    t_tok = x.shape[0]

    def body(x_ref, w_ref, conv_ref, qkv_ref, sgu_ref, f_ref):
        xb = x_ref[...].astype(BF16)
        conv_ref[...] = _dot(xb, w_ref[:, COL_CONV:COL_QKV])
        qkv_ref[...] = _dot(xb, w_ref[:, COL_QKV:COL_SGU]).astype(BF16)
        sgu_ref[...] = _dot(xb, w_ref[:, COL_SGU:COL_F])
        f_ref[...] = _dot(xb, w_ref[:, COL_F:D_IN_PAD])

    def tok(n):
        return pl.BlockSpec((tm, n), lambda i: (i, 0))

    return pl.pallas_call(
        body, name="mix_in_proj", grid=(t_tok // tm,),
        in_specs=[tok(D_MODEL), pl.BlockSpec((D_MODEL, D_IN_PAD), lambda i: (0, 0))],
        out_specs=[tok(768), tok(1536), tok(512), tok(128)],
        out_shape=[_hbm((t_tok, 768), F32), _hbm((t_tok, 1536), BF16),
                   _hbm((t_tok, 512), F32), _hbm((t_tok, 128), F32)],
        compiler_params=_params(48, ("arbitrary",)),
    )(x, w_in)


def _shift_down(a, k):
    row = lax.broadcasted_iota(jnp.int32, a.shape, 0)
    return jnp.where(row >= k, pltpu.roll(a, k, 0), 0.0)


def _shift_up(a, k):
    rows = a.shape[0]
    row = lax.broadcasted_iota(jnp.int32, a.shape, 0)
    return jnp.where(row < rows - k, pltpu.roll(a, rows - k, 0), 0.0)


def _tril(n):
    return lax.broadcasted_iota(jnp.int32, (n, n), 0) >= lax.broadcasted_iota(jnp.int32, (n, n), 1)


def _sgu_group_of_lane():
    return lax.broadcasted_iota(jnp.int32, (1, D_SGU), 1) // (D_SGU // N_SGU_GROUPS)


def _log_sigmoid(x):
    return jnp.minimum(x, 0.0) - jnp.log1p(jnp.exp(-jnp.abs(x)))


def _mix_mid_fwd(conv, sgu, f, conv_w, b_f, sgu_g, sgu_b, w_s, b_mat, n_seq):
    t_tok = conv.shape[0]
    seq = t_tok // n_seq
    n_chunk = seq // SGU_CHUNK
    per_blk = ATT_BLK // SGU_CHUNK

    def body(conv_ref, sgu_ref, f_ref, cw_ref, bf_ref, lg_ref, lb_ref, ws_ref, bm_ref, cat_ref, cum_ref):
        z = conv_ref[:, 256:512] * conv_ref[:, 512:768]
        y = cw_ref[0:1, :] * _shift_down(z, 2) + cw_ref[1:2, :] * _shift_down(z, 1) + cw_ref[2:3, :] * z
        cat_ref[:, 0:D_CONV] = (conv_ref[:, 0:256] * y).astype(BF16)
        cat_ref[:, D_CONV:D_CONV + D_FOX] = jnp.zeros((seq, D_FOX), BF16)

        tril = _tril(SGU_CHUNK)
        grp = _sgu_group_of_lane()
        wc = [jnp.where(tril, ws_ref[g], 0.0).astype(BF16) for g in range(N_SGU_GROUPS)]
        tri_f = tril.astype(F32)
        carry = jnp.zeros((1, 128), F32)
        for n in range(n_chunk):
            rows = pl.ds(n * SGU_CHUNK, SGU_CHUNK)
            u = _gelu(sgu_ref[rows, 0:256])
            vhat, _ = _ln_stats(_gelu(sgu_ref[rows, 256:512]))
            vn = (vhat * lg_ref[...] + lb_ref[...]).astype(BF16)
            mixed = bm_ref[...]
            for g in range(N_SGU_GROUPS):
                mixed = mixed + jnp.where(grp == g, _dot(wc[g], vn), 0.0)
            cat_ref[rows, D_CONV + D_FOX:D_MODEL] = (u * mixed).astype(BF16)

            log_f = _log_sigmoid(f_ref[rows, :] + bf_ref[...])
            cs = _dot(tri_f, log_f, HIGHEST) + carry
            carry = cs[SGU_CHUNK - 1:SGU_CHUNK, :]
            cs_t = cs.T
            lanes = pl.ds((n % per_blk) * SGU_CHUNK, SGU_CHUNK)
            for h in range(N_HEADS):
                cum_ref[h, n // per_blk, :, lanes] = cs_t[h:h + 1, :]

    def seq_blk(n):
        return pl.BlockSpec((seq, n), lambda b: (b, 0))

    def full(shape):
        return pl.BlockSpec(shape, lambda b: (0,) * len(shape))

    return pl.pallas_call(
        body, name="mix_mid_fwd", grid=(n_seq,),
        in_specs=[seq_blk(768), seq_blk(512), seq_blk(128), full((8, 256)), full((1, 128)), full((1, 256)),
                  full((1, 256)), full((4, 128, 128)), full((128, 256))],
        out_specs=[seq_blk(D_MODEL), pl.BlockSpec((N_HEADS, seq // ATT_BLK, 1, ATT_BLK), lambda b: (b, 0, 0, 0))],
        out_shape=[_hbm((t_tok, D_MODEL), BF16), _hbm((n_seq * N_HEADS, seq // ATT_BLK, 1, ATT_BLK), F32)],
        compiler_params=_params(48, ("arbitrary",)),
    )(conv, sgu, f, conv_w, b_f, sgu_g, sgu_b, w_s, b_mat)


def _head_masks():
    lane = lax.broadcasted_iota(jnp.int32, (1, 128), 1)
    return lane < 64, lane


def _fox_fwd(qkv, cum_t, cat, n_seq):
    t_tok = qkv.shape[0]
    seq = t_tok // n_seq
    nq = seq // ATT_BLK
    blk = ATT_BLK

    def body(q_ref, k_ref, v_ref, c_ref, _cat, o_ref, lse_ref):
        qi = pl.program_id(2)
        first, _ = _head_masks()
        one = jnp.ones((1, 128), BF16)
        qh = []
        for hp in range(ATT_PAIRS):
            qs = q_ref[:, 128 * hp:128 * hp + 128] * ATT_SCALE
            zero = jnp.zeros_like(qs)
            qh += [jnp.where(first, qs, zero), jnp.where(first, zero, qs)]

        def step(kb, carry, masked):
            ms, accs = carry
            rows = pl.ds(pl.multiple_of(kb * blk, blk), blk)
            new_m, new_acc = [], []
            for hp in range(ATT_PAIRS):
                k = k_ref[rows, 128 * hp:128 * hp + 128]
                v = v_ref[rows, 128 * hp:128 * hp + 128]
                for h in range(2):
                    i = 2 * hp + h
                    s = _dot_nt(qh[i], k) - c_ref[i, kb]
                    if masked:
                        s = jnp.where(causal, s, NEG)
                    m_new = jnp.maximum(ms[i], jnp.max(s, axis=1, keepdims=True))
                    p = jnp.exp(s - m_new)
                    vh = jnp.where(first, v, one) if h == 0 else jnp.where(first, one, v)
                    new_acc.append(accs[i] * jnp.exp(ms[i] - m_new) + _dot(p.astype(BF16), vh))
                    new_m.append(m_new)
            return tuple(new_m), tuple(new_acc)

        causal = _tril(blk)
        n_heads = 2 * ATT_PAIRS
        col = jnp.full((blk, 1), NEG, F32)
        zacc = jnp.zeros((blk, 128), F32)
        carry = lax.fori_loop(0, qi, lambda kb, cr: step(kb, cr, False), ((col,) * n_heads, (zacc,) * n_heads))
        ms, accs = step(qi, carry, True)
        for hp in range(ATT_PAIRS):
            acc0, acc1 = accs[2 * hp], accs[2 * hp + 1]
            l0 = pltpu.roll(acc0, 64, 1)
            l1 = pltpu.roll(acc1, 64, 1)
            o_ref[:, 128 * hp:128 * hp + 128] = jnp.where(first, acc0 / l0, acc1 / l1).astype(BF16)
            lse_ref[:, 128 * hp:128 * hp + 128] = jnp.where(first, ms[2 * hp] + jnp.log(l0), ms[2 * hp + 1] + jnp.log(l1))

    wide = 128 * ATT_PAIRS
    n_grp = D_FOX // wide
    first_col = D_CONV // wide
    return pl.pallas_call(
        body, name="fox_fwd", grid=(n_seq, n_grp, nq),
        in_specs=[pl.BlockSpec((blk, wide), lambda b, g, qi: (b * nq + qi, g)),
                  pl.BlockSpec((seq, wide), lambda b, g, qi: (b, n_grp + g)),
                  pl.BlockSpec((seq, wide), lambda b, g, qi: (b, 2 * n_grp + g)),
                  pl.BlockSpec((2 * ATT_PAIRS, nq, 1, blk), lambda b, g, qi: (b * n_grp + g, 0, 0, 0)), _anyspec()],
        out_specs=[pl.BlockSpec((blk, wide), lambda b, g, qi: (b * nq + qi, first_col + g)),
                   pl.BlockSpec((blk, wide), lambda b, g, qi: (b * nq + qi, g))],
        out_shape=[_hbm(cat.shape, BF16), _hbm((t_tok, D_FOX), F32)],
        input_output_aliases={4: 0},
        compiler_params=_params(48, ("arbitrary", "arbitrary", "arbitrary")),
    )(qkv, qkv, qkv, cum_t, cat)


def _fox_bwd(qkv, cum_t, cat, lse, d_o, n_seq):
    t_tok = qkv.shape[0]
    seq = t_tok // n_seq
    nk = seq // ATT_BLK
    blk = ATT_BLK

    def body(q_ref, k_ref, v_ref, c_ref, o_ref, lse_ref, do_ref, dq_ref, dk_ref, dv_ref, drow_ref, dcol_ref):
        kb = pl.program_id(2)
        first, lane = _head_masks()
        second = jnp.logical_not(first)
        one = jnp.ones((1, 128), BF16)
        causal = _tril(blk)

        @pl.when(kb == 0)
        def _():
            dq_ref[...] = jnp.zeros_like(dq_ref)
            drow_ref[...] = jnp.zeros_like(drow_ref)

        def step(qi, carry, masked):
            rows = pl.ds(pl.multiple_of(qi * blk, blk), blk)
            dks, dvs = carry
            new_dk, new_dv = [], []
            for hp in range(ATT_PAIRS):
                cols = slice(128 * hp, 128 * hp + 128)
                k = k_ref[:, cols]
                v = v_ref[:, cols]
                ks = k * ATT_SCALE
                zero = jnp.zeros_like(k)
                qs = q_ref[rows, cols] * ATT_SCALE
                d_o = do_ref[rows, cols]
                dd = d_o.astype(F32) * o_ref[rows, cols].astype(F32)
                lse_t = lse_ref[rows, cols]
                dq = []
                for h, mine in enumerate((first, second)):
                    i = 2 * hp + h
                    qh = jnp.where(mine, qs, zero)
                    doh = jnp.where(mine, d_o, zero)
                    delta = jnp.sum(jnp.where(mine, dd, 0.0), axis=1, keepdims=True)
                    lse_h = jnp.sum(jnp.where(lane == 64 * h, lse_t, 0.0), axis=1, keepdims=True)
                    s = _dot_nt(qh, k) - c_ref[i]
                    if masked:
                        s = jnp.where(causal, s, NEG)
                    p = jnp.exp(s - lse_h)
                    ds = (p * (_dot_nt(doh, v) - delta)).astype(BF16)
                    new_dk.append(dks[i] + _dot_tn(ds, jnp.where(mine, qs, one)))
                    new_dv.append(dvs[i] + _dot_tn(p.astype(BF16), doh))
                    dq.append(_dot(ds, jnp.where(mine, ks, one)))
                dq_ref[rows, cols] += jnp.where(first, dq[0], dq[1])
                drow_ref[rows, cols] += jnp.where(first, dq[1], dq[0])
            return tuple(new_dk), tuple(new_dv)

        zt = (jnp.zeros((blk, 128), F32),) * (2 * ATT_PAIRS)
        carry = step(kb, (zt, zt), True)
        dks, dvs = lax.fori_loop(kb + 1, nk, lambda qi, cr: step(qi, cr, False), carry)
        for hp in range(ATT_PAIRS):
            cols = slice(128 * hp, 128 * hp + 128)
            dk_ref[:, cols] = jnp.where(first, dks[2 * hp], dks[2 * hp + 1]).astype(BF16)
            dcol_ref[:, cols] = jnp.where(first, dks[2 * hp + 1], dks[2 * hp])
            dv_ref[:, cols] = (dvs[2 * hp] + dvs[2 * hp + 1]).astype(BF16)

    wide = 128 * ATT_PAIRS
    n_grp = D_FOX // wide

    def seq_spec(col0):
        return pl.BlockSpec((seq, wide), lambda b, g, kb: (b, col0 + g))

    def key_spec(col0):
        return pl.BlockSpec((blk, wide), lambda b, g, kb: (b * nk + kb, col0 + g))

    return pl.pallas_call(
        body, name="fox_bwd", grid=(n_seq, n_grp, nk),
        in_specs=[seq_spec(0), key_spec(n_grp), key_spec(2 * n_grp),
                  pl.BlockSpec((2 * ATT_PAIRS, None, 1, blk), lambda b, g, kb: (b * n_grp + g, kb, 0, 0)),
                  seq_spec(D_CONV // wide), seq_spec(0), seq_spec(0)],
        out_specs=[seq_spec(0), key_spec(0), key_spec(0), seq_spec(0), key_spec(0)],
        out_shape=[_hbm((t_tok, D_FOX), F32), _hbm((t_tok, D_FOX), BF16),
                   _hbm((t_tok, D_FOX), BF16), _hbm((t_tok, D_FOX), F32),
                   _hbm((t_tok, D_FOX), F32)],
        compiler_params=_params(56, ("arbitrary", "arbitrary", "arbitrary")),
    )(qkv, qkv, qkv, cum_t, cat, lse, d_o)


def _mix_out_fwd(cat, x, w_out, ln_g, ln_b, tm=512):
    t_tok = x.shape[0]

    def body(cat_ref, x_ref, w_ref, g_ref, b_ref, xn_ref, xnb_ref, z_ref):
        z = ALPHA * x_ref[...] + _dot(cat_ref[...], w_ref[...])
        xhat, _ = _ln_stats(z)
        xn = xhat * g_ref[...] + b_ref[...]
        z_ref[...] = z
        xn_ref[...] = xn
        xnb_ref[...] = xn.astype(BF16)

    def tok(n):
        return pl.BlockSpec((tm, n), lambda i: (i, 0))

    vec = pl.BlockSpec((1, D_MODEL), lambda i: (0, 0))
    return pl.pallas_call(
        body, name="mix_out_fwd", grid=(t_tok // tm,),
        in_specs=[tok(D_MODEL), tok(D_MODEL), pl.BlockSpec((D_MODEL, D_MODEL), lambda i: (0, 0)), vec, vec],
        out_specs=[tok(D_MODEL)] * 3,
        out_shape=[_hbm((t_tok, D_MODEL), F32), _hbm((t_tok, D_MODEL), BF16),
                   _hbm((t_tok, D_MODEL), F32)],
        compiler_params=_params(40, ("arbitrary",)),
    )(cat, x, w_out, ln_g, ln_b)


def _mix_out_bwd(dxn, z, w_out, ln_g, tm=512):
    t_tok = dxn.shape[0]

    def body(dxn_ref, z_ref, w_ref, g_ref, dz_ref, dzb_ref, dya_ref, dyb_ref, dyc_ref, dg_ref, db_ref):
        i = pl.program_id(0)
        dxn_t = dxn_ref[...]
        xhat, rstd = _ln_stats(z_ref[...])
        pg = jnp.sum(dxn_t * xhat, axis=0, keepdims=True)
        pb = jnp.sum(dxn_t, axis=0, keepdims=True)

        @pl.when(i == 0)
        def _():
            dg_ref[...] = pg
            db_ref[...] = pb

        @pl.when(i > 0)
        def _():
            dg_ref[...] += pg
            db_ref[...] += pb

        dz = _ln_bwd(dxn_t, xhat, rstd, g_ref[...])
        dzb = dz.astype(BF16)
        dz_ref[...] = dz
        dzb_ref[...] = dzb
        dya_ref[...] = _dot_nt(dzb, w_ref[0:256, :])
        dyb_ref[...] = _dot_nt(dzb, w_ref[256:768, :]).astype(BF16)
        dyc_ref[...] = _dot_nt(dzb, w_ref[768:1024, :])

    def tok(n):
        return pl.BlockSpec((tm, n), lambda i: (i, 0))

    vec = pl.BlockSpec((1, D_MODEL), lambda i: (0, 0))
    return pl.pallas_call(
        body, name="mix_out_bwd", grid=(t_tok // tm,),
        in_specs=[tok(D_MODEL), tok(D_MODEL), pl.BlockSpec((D_MODEL, D_MODEL), lambda i: (0, 0)), vec],
        out_specs=[tok(D_MODEL), tok(D_MODEL), tok(256), tok(512), tok(256), vec, vec],
        out_shape=[_hbm((t_tok, D_MODEL), F32), _hbm((t_tok, D_MODEL), BF16),
                   _hbm((t_tok, 256), F32), _hbm((t_tok, 512), BF16),
                   _hbm((t_tok, 256), F32),
                   _hbm((1, D_MODEL), F32), _hbm((1, D_MODEL), F32)],
        compiler_params=_params(40, ("arbitrary",)),
    )(dxn, z, w_out, ln_g)


def _conv_bwd(conv, dya, conv_w, n_seq):
    t_tok = conv.shape[0]
    seq = t_tok // n_seq

    def body(conv_ref, dya_ref, cw_ref, dconv_ref, dcw_ref):
        @pl.when(pl.program_id(0) == 0)
        def _():
            dcw_ref[...] = jnp.zeros_like(dcw_ref)

        z = conv_ref[:, 256:512] * conv_ref[:, 512:768]
        z1 = _shift_down(z, 1)
        z2 = _shift_down(z, 2)
        y = cw_ref[0:1, :] * z2 + cw_ref[1:2, :] * z1 + cw_ref[2:3, :] * z
        dya_t = dya_ref[...]
        dconv_ref[:, 0:256] = (dya_t * y).astype(BF16)
        dy = dya_t * conv_ref[:, 0:256]
        dcw_ref[0:1, :] += jnp.sum(dy * z2, axis=0, keepdims=True)
        dcw_ref[1:2, :] += jnp.sum(dy * z1, axis=0, keepdims=True)
        dcw_ref[2:3, :] += jnp.sum(dy * z, axis=0, keepdims=True)
        dz = cw_ref[2:3, :] * dy + cw_ref[1:2, :] * _shift_up(dy, 1) + cw_ref[0:1, :] * _shift_up(dy, 2)
        dconv_ref[:, 256:512] = (dz * conv_ref[:, 512:768]).astype(BF16)
        dconv_ref[:, 512:768] = (dz * conv_ref[:, 256:512]).astype(BF16)

    def seq_blk(n):
        return pl.BlockSpec((seq, n), lambda b: (b, 0))

    par = pl.BlockSpec((8, 256), lambda b: (0, 0))
    return pl.pallas_call(
        body, name="conv_bwd", grid=(n_seq,),
        in_specs=[seq_blk(768), seq_blk(256), par], out_specs=[seq_blk(768), par],
        out_shape=[_hbm((t_tok, 768), BF16), _hbm((8, 256), F32)],
        compiler_params=_params(56, ("arbitrary",)),
    )(conv, dya, conv_w)


def _sgu_gate_bwd(sgu, f, dyc, drow, dcol, b_f, sgu_g, sgu_b, w_s, b_mat, n_seq):
    t_tok = sgu.shape[0]
    seq = t_tok // n_seq
    n_chunk = seq // SGU_CHUNK

    def body(sgu_ref, f_ref, dyc_ref, drow_ref, dcol_ref, bf_ref, lg_ref, lb_ref, ws_ref, bm_ref,
             dsgu_ref, df_ref, dbf_ref, dlg_ref, dlb_ref, dws_ref, dbs_ref, dbm_acc):
        b = pl.program_id(0)

        @pl.when(b == 0)
        def _():
            for r in (dbf_ref, dlg_ref, dlb_ref, dws_ref, dbm_acc):
                r[...] = jnp.zeros_like(r)

        tril = _tril(SGU_CHUNK)
        grp = _sgu_group_of_lane()
        wc = [jnp.where(tril, ws_ref[g], 0.0).astype(BF16) for g in range(N_SGU_GROUPS)]
        for n in range(n_chunk):
            rows = pl.ds(n * SGU_CHUNK, SGU_CHUNK)
            su = sgu_ref[rows, 0:256]
            sv = sgu_ref[rows, 256:512]
            u, du = _gelu_with_grad(su)
            gv, dgv = _gelu_with_grad(sv)
            vhat, rstd = _ln_stats(gv)
            vn = (vhat * lg_ref[...] + lb_ref[...]).astype(BF16)
            mixed = bm_ref[...]
            for g in range(N_SGU_GROUPS):
                mixed = mixed + jnp.where(grp == g, _dot(wc[g], vn), 0.0)
            dyc_t = dyc_ref[rows, :]
            dsgu_ref[rows, 0:256] = (dyc_t * mixed * du).astype(BF16)
            dmixed = dyc_t * u
            dbm_acc[...] += dmixed
            dvn = jnp.zeros((SGU_CHUNK, D_SGU), F32)
            for g in range(N_SGU_GROUPS):
                dm_g = jnp.where(grp == g, dmixed, 0.0).astype(BF16)
                dws_ref[g] += _dot_nt(dm_g, vn)
                dvn = dvn + _dot_tn(wc[g], dm_g)
            dlg_ref[...] += jnp.sum(dvn * vhat, axis=0, keepdims=True)
            dlb_ref[...] += jnp.sum(dvn, axis=0, keepdims=True)
            dsgu_ref[rows, 256:512] = (_ln_bwd(dvn, vhat, rstd, lg_ref[...]) * dgv).astype(BF16)

        later = (lax.broadcasted_iota(jnp.int32, (128, 128), 0) <= lax.broadcasted_iota(jnp.int32, (128, 128), 1)).astype(F32)
        head = lax.broadcasted_iota(jnp.int32, (D_FOX, 128), 1)
        pick = (lax.broadcasted_iota(jnp.int32, (D_FOX, 128), 0) == 128 * (head // 2) + 64 * (1 - head % 2)).astype(F32)
        carry = jnp.zeros((1, 128), F32)
        for n in reversed(range(n_chunk)):
            rows = pl.ds(n * SGU_CHUNK, SGU_CHUNK)
            dcum_n = _dot(drow_ref[rows, :] - dcol_ref[rows, :], pick, HIGHEST)
            dlf = _dot(later, dcum_n, HIGHEST) + carry
            carry = carry + jnp.sum(dcum_n, axis=0, keepdims=True)
            df = dlf * jax.nn.sigmoid(-(f_ref[rows, :] + bf_ref[...]))
            df_ref[rows, :] = df.astype(BF16)
            dbf_ref[...] += jnp.sum(df, axis=0, keepdims=True)

        @pl.when(b == n_seq - 1)
        def _():
            for g in range(N_SGU_GROUPS):
                dws_ref[g] = jnp.where(tril, dws_ref[g], 0.0)
            sel = (lax.broadcasted_iota(jnp.int32, (D_SGU, 128), 0) // (D_SGU // N_SGU_GROUPS)
                   == lax.broadcasted_iota(jnp.int32, (D_SGU, 128), 1)).astype(F32)
            dbs_ref[...] = _dot(dbm_acc[...], sel, HIGHEST)

    def seq_blk(n):
        return pl.BlockSpec((seq, n), lambda b: (b, 0))

    def full(shape):
        return pl.BlockSpec(shape, lambda b: (0,) * len(shape))

    param_shapes = [(1, 128), (1, 256), (1, 256), (4, 128, 128), (128, 128)]
    return pl.pallas_call(
        body, name="sgu_gate_bwd", grid=(n_seq,),
        in_specs=[seq_blk(512), seq_blk(128), seq_blk(256), seq_blk(D_FOX), seq_blk(D_FOX),
                  full((1, 128)), full((1, 256)), full((1, 256)), full((4, 128, 128)), full((128, 256))],
        out_specs=[seq_blk(512), seq_blk(128)] + [full(s) for s in param_shapes],
        out_shape=[_hbm((t_tok, 512), BF16), _hbm((t_tok, 128), BF16)]
        + [_hbm(s, F32) for s in param_shapes],
        scratch_shapes=[pltpu.VMEM((128, 256), F32)],
        compiler_params=_params(48, ("arbitrary",)),
    )(sgu, f, dyc, drow, dcol, b_f, sgu_g, sgu_b, w_s, b_mat)


def _mix_in_bwd(dconv, dq, dk, dv, dsgu, df, dz, w_in, tm=512):
    t_tok = dz.shape[0]

    def body(dconv_ref, dq_ref, dk_ref, dv_ref, dsgu_ref, df_ref, dz_ref, w_ref, dx_ref, dp_ref):
        dqb = dq_ref[...].astype(BF16)
        pieces = [(COL_CONV, dconv_ref[...]), (COL_QKV, dqb), (COL_QKV + 512, dk_ref[...]), (COL_QKV + 1024, dv_ref[...]),
                  (COL_SGU, dsgu_ref[...]), (COL_F, df_ref[...])]
        dx = ALPHA * dz_ref[...]
        for col, val in pieces:
            width = val.shape[1]
            dp_ref[:, col:col + width] = val
            dx = dx + _dot_nt(val, w_ref[:, col:col + width])
        dx_ref[...] = dx

    def tok(n):
        return pl.BlockSpec((tm, n), lambda i: (i, 0))

    return pl.pallas_call(
        body, name="mix_in_bwd", grid=(t_tok // tm,),
        in_specs=[tok(768), tok(512), tok(512), tok(512), tok(512), tok(128), tok(D_MODEL),
                  pl.BlockSpec((D_MODEL, D_IN_PAD), lambda i: (0, 0))],
        out_specs=[tok(D_MODEL), tok(D_IN_PAD)],
        out_shape=[_hbm((t_tok, D_MODEL), F32), _hbm((t_tok, D_IN_PAD), BF16)],
        compiler_params=_params(48, ("arbitrary",)),
    )(dconv, dq, dk, dv, dsgu, df, dz, w_in)


def _pad_rows(a, rows):
    return jnp.pad(a, ((0, rows - a.shape[0]), (0, 0)))


F_BLOCK = F_ORIG // D_IN_SHARD
F_AT = F_ORIG - F_BLOCK * D_IN_SHARD
assert (F_ORIG + N_HEADS) // D_IN_SHARD == F_BLOCK


def _w_in_from_blocks(g):
    fb = g[F_BLOCK]
    zeros = jnp.zeros((D_MODEL, D_IN_PAD - COL_F - N_HEADS), g.dtype)
    return jnp.concatenate([g[d] for d in range(F_BLOCK)] + [fb[:, :F_AT], fb[:, F_AT + N_HEADS:]]
                           + [g[d] for d in range(F_BLOCK + 1, N_DEV)] + [fb[:, F_AT:F_AT + N_HEADS], zeros], axis=1)


def _w_in_to_blocks(dw):
    def cols(lo, hi):
        shift = 0 if hi <= F_ORIG else N_HEADS
        return dw[:, lo - shift:hi - shift]

    blocks = []
    for d in range(N_DEV):
        lo, hi = d * D_IN_SHARD, (d + 1) * D_IN_SHARD
        if d == F_BLOCK:
            blocks.append(jnp.concatenate([cols(lo, F_ORIG), dw[:, COL_F:COL_F + N_HEADS], cols(F_ORIG + N_HEADS, hi)], axis=1))
        else:
            blocks.append(cols(lo, hi))
    return jnp.stack(blocks)


LN1_ROWS = 2 * 8
REST_ROWS = 4 * 8 + 2 * 8 + 512 + 8 + 8 + 8


def _pack_rest(p):
    rows = [p[name].reshape(8, 128) for name in ("ln2_g", "ln2_b", "ln3_g", "ln3_b")]
    rows += [_pad_rows(p[name].reshape(2, 128), 8) for name in ("sgu_ln_g", "sgu_ln_b")]
    rows += [p["sgu_w_s"].reshape(512, 128), _pad_rows(p["sgu_b_s"], 8),
             _pad_rows(jnp.pad(p["fox_b_f"], (0, 128 - N_HEADS)).reshape(1, 128), 8), _pad_rows(p["conv_w"].reshape(6, 128), 8)]
    return jnp.concatenate(rows, axis=0)


def _pack_layer(p):
    return jnp.concatenate([p["ln1_g"].reshape(8, 128), p["ln1_b"].reshape(8, 128), _pack_rest(p)], axis=0)


def _unpack_layer(a):
    r = 0

    def take(n, valid):
        nonlocal r
        piece = a[r:r + valid]
        r += n
        return piece

    d = {}
    for name in ("ln1_g", "ln1_b", "ln2_g", "ln2_b", "ln3_g", "ln3_b"):
        d[name] = take(8, 8).reshape(D_MODEL)
    for name in ("sgu_ln_g", "sgu_ln_b"):
        d[name] = take(8, 2).reshape(D_SGU)
    d["sgu_w_s"] = take(512, 512).reshape(N_SGU_GROUPS, SGU_CHUNK, SGU_CHUNK)
    d["sgu_b_s"] = take(8, 4).reshape(N_SGU_GROUPS, SGU_CHUNK)
    d["fox_b_f"] = take(8, 1).reshape(128)[:N_HEADS]
    d["conv_w"] = take(8, 6).reshape(3, D_CONV)
    return d


SMALL_NAMES = ("ln1_g", "ln1_b", "fox_b_f", "sgu_ln_g", "sgu_ln_b", "sgu_w_s", "sgu_b_s", "ln2_g", "ln2_b", "ln3_g", "ln3_b")
BIG_NAMES = ("ffn1_w_up", "ffn1_w_down", "mix_w_in", "mix_w_out", "ffn2_w_up", "ffn2_w_down")
UP_NAMES = ("ffn1_w_up", "ffn2_w_up")
WEIGHT_ORDER = ("ln1_g", "ln1_b", "ffn1_w_up", "ffn1_w_down", "mix_w_in", "fox_b_f", "conv_w", "sgu_ln_g", "sgu_ln_b",
                "sgu_w_s", "sgu_b_s", "mix_w_out", "ln2_g", "ln2_b", "ffn2_w_up", "ffn2_w_down", "ln3_g", "ln3_b")


class _Overlap:
    def __init__(self, w, after, me, where):
        self.me, self.where = me, where
        self.last = after
        width = D_CONV // N_DEV
        rows = jnp.pad(_pad_rows(w["conv_w"].reshape(DEPTH * 3, width), 8), ((0, 0), (0, 128 - width)))
        land = lax.dynamic_update_slice(lax.empty((N_DEV,) + rows.shape, F32), rows[None], (me, 0, 0))
        self.conv_started = self._start("conv_w_start", _peers_plan(), N_DEV - 1, [rows, land])
        self.conv_full = None
        groups = [[("ffn1_w_up", 0), ("ffn1_w_down", 0)],
                  [("mix_w_in", 0), ("mix_w_out", 0), ("ffn2_w_up", 0), ("ffn2_w_down", 0)]]
        groups += [[(name, l) for name in BIG_NAMES] for l in range(1, DEPTH)]
        self.gathers = []
        for gi, group in enumerate(groups):
            shards = [w[name][l].astype(BF16) for name, l in group]
            lands = [lax.dynamic_update_slice(lax.empty((N_DEV,) + s.shape, BF16), s[None], (me, 0, 0)) for s in shards]
            started = self._start(f"allgather_start_{gi}", _gather_plan(len(group)), 3 * len(group), shards + lands)
            self.gathers.append(dict(group=group, chips=started))
        self.all_started = self.last
        self.scatters = {}
        self.order = []
        self.small = []

    def conv_w(self, after):
        if self.conv_full is None:
            width = D_CONV // N_DEV
            gathered = _exchange_wait("conv_w_wait", _peers_plan(), N_DEV - 1, self.conv_started, after)[1]
            self.conv_full = jnp.transpose(gathered[:, :DEPTH * 3, :width], (1, 0, 2)).reshape(DEPTH, 3, D_CONV)
        return self.conv_full

    def _start(self, name, plan, n_copies, arrays):
        started = _exchange_start(name, plan, n_copies, arrays, self.last)
        self.last = started[3]
        return started

    def _group_of(self, layer, part):
        return layer + 1 if layer > 0 else (0 if part == "ffn1" else 1)

    def pass_on(self, layer, part, after):
        st = self.gathers[self._group_of(layer, part)]
        if "sibling" not in st:
            gi, m = self._group_of(layer, part), len(st["group"])
            arrays = _exchange_wait(f"allgather_wait_{gi}", _gather_plan(m), 3 * m, st["chips"], after)
            st["sibling"] = self._start(f"allgather_pass_start_{gi}", _pass_on_plan(m), 4 * m, arrays)
        return st["sibling"][3]

    def weights(self, layer, part, after):
        gi = self._group_of(layer, part)
        st = self.gathers[gi]
        if "full" not in st:
            after = self.all_started if after is None else after
            self.pass_on(layer, part, after)
            m = len(st["group"])
            arrays = _exchange_wait(f"allgather_pass_wait_{gi}", _pass_on_plan(m), 4 * m, st["sibling"], after)
            st["full"] = dict(zip(st["group"], arrays[m:]))
        g = st["full"]

        def ffn(n):
            return g[(f"ffn{n}_w_up", layer)].reshape(2, D_FF, D_MODEL), g[(f"ffn{n}_w_down", layer)].reshape(D_FF, D_MODEL)

        if part == "ffn1":
            return ffn(1)
        return (_w_in_from_blocks(g[("mix_w_in", layer)]), g[("mix_w_out", layer)].reshape(D_MODEL, D_MODEL), *ffn(2))

    def push(self, key, items):
        n = len(items)
        grads = [g for _, _, g in items]
        lands = [lax.empty((4,) + g.shape[1:], F32) for g in grads]
        started = self._start(f"rs_sibling_start_{key[0]}{key[1]}", _sibling_plan(n), 4 * n, grads + lands)
        self.scatters[key] = dict(items=items, sibling=started)
        self.order.append(key)
        return started[3]

    def advance(self, key, after):
        st = self.scatters[key]
        n = len(st["items"])
        arrays = _exchange_wait(f"rs_sibling_wait_{key[0]}{key[1]}", _sibling_plan(n), 4 * n, st["sibling"], after)
        partials = [_chip_partial(g, r, self.where) for g, r in zip(arrays[:n], arrays[n:])]
        p16 = [p for _, p in partials]
        lands = [lax.empty((3,) + p.shape[1:], BF16) for p in p16]
        started = self._start(f"rs_chip_start_{key[0]}{key[1]}", _chip_plan(n), 3 * n, p16 + lands)
        st.update(own32=[p for p, _ in partials], chip=started)
        return started[3]

    def push_small(self, rows):
        k = len(self.small)
        land = lax.dynamic_update_slice(lax.empty((N_DEV,) + rows.shape, F32), rows[None], (self.me, 0, 0))
        started = self._start(f"small_start_{k}", _peers_plan(), N_DEV - 1, [rows, land])
        self.small.append(started)
        return started[3]

    def finish(self, w, m, v):
        res = {}
        after = self.scatters[self.order[-1]]["chip"][3]
        for key in self.order:
            st = self.scatters[key]
            n = len(st["items"])
            arrays = _exchange_wait(f"rs_chip_wait_{key[0]}{key[1]}", _chip_plan(n), 3 * n, st["chip"], after)
            for (name, l, _), own32, r16 in zip(st["items"], st["own32"], arrays[n:]):
                res[name] = _adamw_shard(own32, r16, w[name], m[name], v[name], l, res.get(name))
                after = res[name][0]
        pieces = [_exchange_wait(f"small_wait_{k}", _peers_plan(), N_DEV - 1, started, after)[1]
                  for k, started in enumerate(self.small)]
        return res, pieces


def _dw_up(dgu, x, after):
    return _matmul_tn(dgu, x[None], after, bm=DW_ROWS, bn=DW_COLS).reshape(N_DEV, FFN_BLK, D_MODEL)


def _dw_down(a, dy, after):
    return _matmul_tn(a[None], dy[None], after, bm=DW_ROWS, bn=DW_COLS).reshape(N_DEV, FFN_BLK // 2, D_MODEL)


def _local_step(x, target, comm, small, n_seq):
    def vec(a):
        return a.reshape(1, -1)

    saved = []
    h = x
    for l in range(DEPTH):
        s = {}
        s["up1"], s["down1"] = comm.weights(l, "ffn1", None if l == 0 else h)
        h1, h1b, s["z1"], s["gu1"], s["x0b"] = _ffn_fwd(h, s["up1"], s["down1"], vec(small["ln1_g"][l]), vec(small["ln1_b"][l]), h)
        s["w_in"], s["w_out"], s["up2"], s["down2"] = comm.weights(l, "rest", s["z1"])
        s["x1b"] = h1b
        conv, qkv, sgu, f = _in_proj(h1, s["w_in"])
        cw = _pad_rows(comm.conv_w(h1)[l], 8)
        bf = jnp.pad(small["fox_b_f"][l], (0, 128 - N_HEADS)).reshape(1, 128)
        b_mat = jnp.repeat(small["sgu_b_s"][l].T, D_SGU // N_SGU_GROUPS, axis=1)
        mid_params = (cw, bf, vec(small["sgu_ln_g"][l]), vec(small["sgu_ln_b"][l]), small["sgu_w_s"][l], b_mat)
        cat, cum_t = _mix_mid_fwd(conv, sgu, f, *mid_params, n_seq)
        cat, lse = _fox_fwd(qkv, cum_t, cat, n_seq)
        h2, h2b, s["z2"] = _mix_out_fwd(cat, h1, s["w_out"], vec(small["ln2_g"][l]), vec(small["ln2_b"][l]))
        s.update(conv=conv, qkv=qkv, sgu=sgu, f=f, mid_params=mid_params, cat=cat, cum_t=cum_t, lse=lse, x2b=h2b)
        token = comm.pass_on(l + 1, "ffn1", s["z2"]) if l + 1 < DEPTH else h2
        ln3 = (vec(small["ln3_g"][l]), vec(small["ln3_b"][l]))
        if l + 1 < DEPTH:
            h, _, s["z3"], s["gu2"], _ = _ffn_fwd(h2, s["up2"], s["down2"], *ln3, token)
        else:
            dh, loss, s["z3"], s["gu2"] = _ffn_fwd(h2, s["up2"], s["down2"], *ln3, token, target)
        saved.append(s)

    late_rows = None
    token = loss
    pending = None
    for l in reversed(range(DEPTH)):
        s = saved[l]
        sg = {}
        dh, dy, a, dgu, sg["ln3_g"], sg["ln3_b"] = _ffn_bwd(dh, s["z3"], s["gu2"], s["up2"], s["down2"], vec(small["ln3_g"][l]), token)
        if pending is not None:
            token = comm.advance(pending, dh)
        g_up2 = _dw_up(dgu, s["x2b"], token)
        g_down2 = _dw_down(a, dy, token)
        dz, dzb, dya, dyb, dyc, sg["ln2_g"], sg["ln2_b"] = _mix_out_bwd(dh, s["z2"], s["w_out"], vec(small["ln2_g"][l]))
        g_out = _matmul_tn(s["cat"][None], dzb[None], token).reshape(N_DEV, D_MODEL // N_DEV, D_MODEL)
        dq, dk, dv, drow, dcol = _fox_bwd(s["qkv"], s["cum_t"], s["cat"], s["lse"], dyb, n_seq)
        dconv, dcw = _conv_bwd(s["conv"], dya, s["mid_params"][0], n_seq)
        dsgu, df, dbf, dlg, dlb, dws, dbs = _sgu_gate_bwd(s["sgu"], s["f"], dyc, drow, dcol, *s["mid_params"][1:], n_seq)
        sg.update(conv_w=dcw[:3], fox_b_f=dbf[0, :N_HEADS], sgu_ln_g=dlg[0], sgu_ln_b=dlb[0], sgu_w_s=dws,
                  sgu_b_s=dbs[:, :N_SGU_GROUPS].T)
        dh, dp = _mix_in_bwd(dconv, dq, dk, dv, dsgu, df, dz, s["w_in"])
        g_in = _w_in_to_blocks(_matmul_tn(s["x1b"][None], dp[None], token, tk=1024)[0])
        first = [("ffn2_w_up", l, g_up2), ("ffn2_w_down", l, g_down2), ("mix_w_out", l, g_out), ("mix_w_in", l, g_in)]
        for name in ("ln2_g", "ln2_b", "ln3_g", "ln3_b"):
            sg[name] = sg[name][0]
        if l == 0:
            comm.push((l, "a"), first)
            token = comm.push_small(_pack_rest(sg))
            pending, first = (l, "a"), []
        dh, dy, a, dgu, dg1, db1 = _ffn_bwd(dh, s["z1"], s["gu1"], s["up1"], s["down1"], vec(small["ln1_g"][l]), token)
        if l == 0:
            token = comm.advance(pending, dh)
        g_up1 = _dw_up(dgu, s["x0b"], token)
        ln1_rows = jnp.concatenate([dg1.reshape(8, 128), db1.reshape(8, 128)], axis=0)
        if l == 0:
            token = comm.push((l, "b"), [("ffn1_w_up", l, g_up1)])
            g_down1 = _dw_down(a, dy, token)
            token = comm.advance((l, "b"), g_down1)
            token = comm.push((l, "c"), [("ffn1_w_down", l, g_down1)])
            token = comm.advance((l, "c"), token)
            late_rows = ln1_rows
        else:
            g_down1 = _dw_down(a, dy, token)
            pending = (l, "b")
            comm.push(pending, first + [("ffn1_w_up", l, g_up1), ("ffn1_w_down", l, g_down1)])
            token = comm.push_small(jnp.concatenate([ln1_rows, _pack_rest(sg)], axis=0))
    return loss, dh, late_rows


def kernel(x, ln1_g, ln1_b, ffn1_w_up, ffn1_w_down, mix_w_in, fox_b_f, conv_w, sgu_ln_g, sgu_ln_b, sgu_w_s, sgu_b_s, mix_w_out, ln2_g, ln2_b, ffn2_w_up, ffn2_w_down, ln3_g, ln3_b, loss_target, m_ln1_g, m_ln1_b, m_ffn1_w_up, m_ffn1_w_down, m_mix_w_in, m_fox_b_f, m_conv_w, m_sgu_ln_g, m_sgu_ln_b, m_sgu_w_s, m_sgu_b_s, m_mix_w_out, m_ln2_g, m_ln2_b, m_ffn2_w_up, m_ffn2_w_down, m_ln3_g, m_ln3_b, v_ln1_g, v_ln1_b, v_ffn1_w_up, v_ffn1_w_down, v_mix_w_in, v_fox_b_f, v_conv_w, v_sgu_ln_g, v_sgu_ln_b, v_sgu_w_s, v_sgu_b_s, v_mix_w_out, v_ln2_g, v_ln2_b, v_ffn2_w_up, v_ffn2_w_down, v_ln3_g, v_ln3_b):
    w = dict(ln1_g=ln1_g, ln1_b=ln1_b, ffn1_w_up=ffn1_w_up, ffn1_w_down=ffn1_w_down, mix_w_in=mix_w_in, fox_b_f=fox_b_f,
             conv_w=conv_w, sgu_ln_g=sgu_ln_g, sgu_ln_b=sgu_ln_b, sgu_w_s=sgu_w_s, sgu_b_s=sgu_b_s, mix_w_out=mix_w_out,
             ln2_g=ln2_g, ln2_b=ln2_b, ffn2_w_up=ffn2_w_up, ffn2_w_down=ffn2_w_down, ln3_g=ln3_g, ln3_b=ln3_b)
    m = dict(ln1_g=m_ln1_g, ln1_b=m_ln1_b, ffn1_w_up=m_ffn1_w_up, ffn1_w_down=m_ffn1_w_down, mix_w_in=m_mix_w_in,
             fox_b_f=m_fox_b_f, conv_w=m_conv_w, sgu_ln_g=m_sgu_ln_g, sgu_ln_b=m_sgu_ln_b, sgu_w_s=m_sgu_w_s,
             sgu_b_s=m_sgu_b_s, mix_w_out=m_mix_w_out, ln2_g=m_ln2_g, ln2_b=m_ln2_b, ffn2_w_up=m_ffn2_w_up,
             ffn2_w_down=m_ffn2_w_down, ln3_g=m_ln3_g, ln3_b=m_ln3_b)
    v = dict(ln1_g=v_ln1_g, ln1_b=v_ln1_b, ffn1_w_up=v_ffn1_w_up, ffn1_w_down=v_ffn1_w_down, mix_w_in=v_mix_w_in,
             fox_b_f=v_fox_b_f, conv_w=v_conv_w, sgu_ln_g=v_sgu_ln_g, sgu_ln_b=v_sgu_ln_b, sgu_w_s=v_sgu_w_s,
             sgu_b_s=v_sgu_b_s, mix_w_out=v_mix_w_out, ln2_g=v_ln2_g, ln2_b=v_ln2_b, ffn2_w_up=v_ffn2_w_up,
             ffn2_w_down=v_ffn2_w_down, ln3_g=v_ln3_g, ln3_b=v_ln3_b)

    mx, my, mc = lax.axis_index("x"), lax.axis_index("y"), lax.axis_index("c")
    me = 4 * mx + 2 * my + mc
    n_seq, seq, _ = x.shape
    t_tok = n_seq * seq
    for name in UP_NAMES:
        for t in (w, m, v):
            t[name] = jnp.transpose(t[name], (0, 2, 1))

    comm = _Overlap(w, x, me, jnp.stack([mc, 2 * mx + my]).astype(jnp.int32))
    small = {name: w[name] for name in SMALL_NAMES}

    loss_dev, grad_x, late_rows = _local_step(
        x.reshape(t_tok, D_MODEL), loss_target.reshape(t_tok, D_MODEL), comm, small, n_seq)
    loss = lax.psum(loss_dev[0, 0], ("x", "y", "c"))
    out, pieces = comm.finish(w, m, v)
    for name in UP_NAMES:
        out[name] = [jnp.transpose(a, (0, 2, 1)) for a in out[name]]

    pieces.append(_allgather_small(late_rows))
    spans = [(l, 0, LN1_ROWS + REST_ROWS) for l in reversed(range(1, DEPTH))] + [(0, LN1_ROWS, LN1_ROWS + REST_ROWS), (0, 0, LN1_ROWS)]

    def widen(a):
        return lax.dynamic_update_slice(jnp.zeros((3, D_CONV), F32), a, (0, me * (D_CONV // N_DEV)))

    packed = [[_pack_layer({**{name: t[name][l] for name in SMALL_NAMES}, "conv_w": widen(t["conv_w"][l])}) for l in range(DEPTH)]
              for t in (w, m, v)]
    rows_out = {}
    for (l, lo, hi), gathered_piece in zip(spans, pieces):
        rows_out[(l, lo)] = _adamw_small(gathered_piece, *[packed[t][l][lo:hi] for t in range(3)])
    per_layer = []
    for l in range(DEPTH):
        parts = sorted(lo for (ll, lo) in rows_out if ll == l)
        per_layer.append([_unpack_layer(jnp.concatenate([rows_out[(l, lo)][k] for lo in parts], axis=0)) for k in range(4)])
    for name in SMALL_NAMES:
        out[name] = [jnp.stack([per_layer[l][k][name] for l in range(DEPTH)]) for k in range(4)]
    lo_col = me * (D_CONV // N_DEV)
    out["conv_w"] = [jnp.stack([lax.dynamic_slice(per_layer[l][k]["conv_w"], (0, lo_col), (3, D_CONV // N_DEV)) for l in range(DEPTH)])
                     for k in range(4)]

    return (loss, grad_x.reshape(x.shape), *[out[name][0] for name in WEIGHT_ORDER], *[out[name][1] for name in WEIGHT_ORDER],
            *[out[name][2] for name in WEIGHT_ORDER], *[out[name][3] for name in WEIGHT_ORDER])
```

```python
import functools

import jax
import jax.numpy as jnp
from jax import lax
from jax.experimental import pallas as pl
from jax.experimental.pallas import tpu as pltpu

F32 = jnp.float32
BF16 = jnp.bfloat16
MESH = pl.DeviceIdType.MESH

N_DEV = 8
DEPTH = 2
D_MODEL = 1024
D_FF = 2816
FFN_BLK = 2 * D_FF // N_DEV
MXU_TILE_V7X = 256
FFN_CHUNKS = tuple((lo, min(lo + 3 * MXU_TILE_V7X, D_FF)) for lo in range(0, D_FF, 3 * MXU_TILE_V7X))
FFN_BWD_CHUNKS = tuple((lo, min(lo + 4 * MXU_TILE_V7X, D_FF)) for lo in range(0, D_FF, 4 * MXU_TILE_V7X))
DW_ROWS = D_FF // 2
D_CONV = 256
D_FOX = 512
N_HEADS = 8
D_SGU = 256
N_SGU_GROUPS = 4
SGU_CHUNK = 128
D_IN = 3 * D_CONV + 3 * D_FOX + N_HEADS + 2 * D_SGU
D_IN_SHARD = D_IN // N_DEV
COL_CONV, COL_QKV, COL_SGU, COL_F = 0, 768, 2304, 2816
D_IN_PAD = 2944
F_ORIG = 3 * D_CONV + 3 * D_FOX
ALPHA = (2 * DEPTH) ** 0.25
LN_EPS = 1e-5
ATT_SCALE = 0.125
ATT_BLK = 512
ATT_PAIRS = 2
NEG = -1e30

ADAM_LR, ADAM_B1, ADAM_B2, ADAM_EPS, ADAM_WD, ADAM_STEP = 0.001, 0.9, 0.999, 1e-08, 0.01, 10

VMEM_BYTES_V7X = 64 * 1024 * 1024
HIGHEST = lax.Precision.HIGHEST


def _params(vmem_mb, sem=None):
    assert vmem_mb * 1024 * 1024 < VMEM_BYTES_V7X
    kw = dict(vmem_limit_bytes=vmem_mb * 1024 * 1024)
    if sem is not None:
        kw["dimension_semantics"] = sem
    return pltpu.CompilerParams(**kw)


def _dot(a, b, precision=None):
    return lax.dot_general(a, b, (((1,), (0,)), ((), ())), preferred_element_type=F32, precision=precision)


def _dot_nt(a, b):
    return lax.dot_general(a, b, (((1,), (1,)), ((), ())), preferred_element_type=F32)


def _dot_tn(a, b):
    return lax.dot_general(a, b, (((0,), (0,)), ((), ())), preferred_element_type=F32)


def _ln_stats(z):
    mu = jnp.mean(z, axis=-1, keepdims=True)
    zc = z - mu
    var = jnp.mean(zc * zc, axis=-1, keepdims=True)
    rstd = lax.rsqrt(var + LN_EPS)
    return zc * rstd, rstd


def _ln_bwd(dy, xhat, rstd, g):
    dxh = dy * g
    m1 = jnp.mean(dxh, axis=-1, keepdims=True)
    m2 = jnp.mean(dxh * xhat, axis=-1, keepdims=True)
    return rstd * (dxh - m1 - xhat * m2)


_GELU_C = 0.7978845608028654


def _gelu(x):
    return 0.5 * x * (1.0 + jnp.tanh(_GELU_C * (x + 0.044715 * x * x * x)))


def _gelu_with_grad(x):
    t = jnp.tanh(_GELU_C * (x + 0.044715 * x * x * x))
    return 0.5 * x * (1.0 + t), 0.5 * (1.0 + t) + 0.5 * x * (1.0 - t * t) * _GELU_C * (1.0 + 3 * 0.044715 * x * x)


def _hbm(shape, dtype):
    n = 1
    for d in shape:
        n *= d
    if n * jnp.dtype(dtype).itemsize >= 1024 * 1024:
        return pltpu.HBM(tuple(shape), dtype)
    return jax.ShapeDtypeStruct(tuple(shape), dtype)


def _vspec():
    return pl.BlockSpec(memory_space=pltpu.VMEM)


def _anyspec():
    return pl.BlockSpec(memory_space=pl.ANY)


def _mesh_pos():
    return lax.axis_index("x"), lax.axis_index("y"), lax.axis_index("c")


def _other_chips(x, y):
    return [(1 - x, y), (x, 1 - y), (1 - x, 1 - y)]


_HBM_SPEC = pl.BlockSpec(memory_space=pltpu.HBM)
_SEM_SPEC = pl.BlockSpec(memory_space=pltpu.SEMAPHORE)
_DATAFLOW_EFFECT = pltpu.SideEffectType.DATAFLOW_SIDE_EFFECTING


def _remote_copies(plan, refs, send_sems, recv_sems):
    return [pltpu.make_async_remote_copy(src_ref=src, dst_ref=dst, send_sem=send_sems.at[k], recv_sem=recv_sems.at[k],
                                         device_id=to, device_id_type=MESH)
            for k, (src, dst, to) in enumerate(plan(refs, *_mesh_pos()))]


def _exchange_start(name, plan, n_copies, arrays, after):
    n = len(arrays)

    def body(*refs):
        send_sems, recv_sems, token = refs[n + 1], refs[n + 2], refs[-1]
        for cp in _remote_copies(plan, refs[:n], send_sems, recv_sems):
            cp.start()
        token[...] = jnp.zeros_like(token)

    out = pl.pallas_call(
        body, name=name,
        out_shape=(pltpu.SemaphoreType.DMA((n_copies,)), pltpu.SemaphoreType.DMA((n_copies,)),
                   *[pltpu.HBM(a.shape, a.dtype) for a in arrays], _hbm((8, 128), F32)),
        in_specs=[_HBM_SPEC] * n + [_anyspec()],
        out_specs=(_SEM_SPEC, _SEM_SPEC, *[_HBM_SPEC] * n, _vspec()),
        input_output_aliases={i: 2 + i for i in range(n)},
        compiler_params=pltpu.CompilerParams(has_side_effects=_DATAFLOW_EFFECT),
    )(*[pltpu.with_memory_space_constraint(a, pltpu.HBM) for a in arrays], after)
    return out[0], out[1], list(out[2:2 + n]), out[-1]


def _exchange_wait(name, plan, n_copies, started, after):
    send_sems, recv_sems, arrays, _ = started
    n = len(arrays)

    def body(*refs):
        for cp in _remote_copies(plan, refs[:n], refs[n], refs[n + 1]):
            cp.wait_send()
            cp.wait_recv()

    out = pl.pallas_call(
        body, name=name,
        out_shape=tuple(pltpu.HBM(a.shape, a.dtype) for a in arrays),
        in_specs=[_HBM_SPEC] * n + [_SEM_SPEC, _SEM_SPEC, _anyspec()], out_specs=tuple([_HBM_SPEC] * n),
        input_output_aliases={i: i for i in range(n)},
        compiler_params=pltpu.CompilerParams(has_side_effects=_DATAFLOW_EFFECT),
    )(*arrays, send_sems, recv_sems, after)
    return list(out)


def _gather_plan(m):
    def plan(refs, x, y, c):
        me = 4 * x + 2 * y + c
        return [(refs[i], refs[m + i].at[me], (*chip, c)) for i in range(m) for chip in _other_chips(x, y)]
    return plan


def _pass_on_plan(m):
    def plan(refs, x, y, c):
        out = []
        for i in range(m):
            out.append((refs[i], refs[m + i].at[4 * x + 2 * y + c], (x, y, 1 - c)))
            for cx, cy in _other_chips(x, y):
                block = refs[m + i].at[4 * cx + 2 * cy + c]
                out.append((block, block, (x, y, 1 - c)))
        return out
    return plan


def _peers_plan():
    def plan(refs, x, y, c):
        rel = [(dx, dy, dc) for dx in (0, 1) for dy in (0, 1) for dc in (0, 1)][1:]
        return [(refs[0], refs[1].at[4 * x + 2 * y + c], (x ^ dx, y ^ dy, c ^ dc)) for dx, dy, dc in rel]
    return plan


def _allgather_small(v):
    rows = v.shape[0]

    def body(v_ref, out_ref, send_sems, recv_sems):
        x, y, c = _mesh_pos()
        me = 4 * x + 2 * y + c
        out_ref[me] = v_ref[...]
        rel = [(dx, dy, dc) for dx in (0, 1) for dy in (0, 1) for dc in (0, 1)][1:]
        copies = []
        for k, (dx, dy, dc) in enumerate(rel):
            to = (x ^ dx, y ^ dy, c ^ dc)
            copies.append(pltpu.make_async_remote_copy(
                src_ref=v_ref, dst_ref=out_ref.at[me], send_sem=send_sems.at[k], recv_sem=recv_sems.at[k],
                device_id=to, device_id_type=MESH))
        for cp in copies:
            cp.start()
        for k, (dx, dy, dc) in enumerate(rel):
            src_blk = 4 * (x ^ dx) + 2 * (y ^ dy) + (c ^ dc)
            pltpu.make_async_remote_copy(
                src_ref=v_ref, dst_ref=out_ref.at[src_blk], send_sem=send_sems.at[k], recv_sem=recv_sems.at[k],
                device_id=(x, y, c), device_id_type=MESH).wait_recv()
        for cp in copies:
            cp.wait_send()

    return pl.pallas_call(
        body, name="allgather_small",
        out_shape=jax.ShapeDtypeStruct((N_DEV, rows, 128), v.dtype),
        in_specs=[_vspec()], out_specs=_vspec(),
        scratch_shapes=[pltpu.SemaphoreType.DMA((7,)), pltpu.SemaphoreType.DMA((7,))],
        compiler_params=_params(24),
    )(v)


def _sibling_plan(n):
    def plan(refs, x, y, c):
        return [(refs[a].at[2 * q + (1 - c)], refs[n + a].at[q], (x, y, 1 - c)) for a in range(n) for q in range(4)]
    return plan


def _chip_plan(n):
    def plan(refs, x, y, c):
        return [(refs[a].at[2 * cx + cy], refs[n + a].at[j], (cx, cy, c))
                for a in range(n) for j, (cx, cy) in enumerate(_other_chips(x, y))]
    return plan


def _row_tile(rows, cols, budget_bytes=2 * 1024 * 1024):
    best = 8
    for t in range(8, rows + 1, 8):
        if rows % t == 0 and t * cols * 4 <= budget_bytes:
            best = t
    return best


def _chip_partial(g, recv, where):
    _, rows, cols = g.shape
    tr = _row_tile(rows, cols)

    def body(where_ref, g_ref, r_ref, own_ref, o16_ref):
        s = g_ref[...] + r_ref[...]
        o16_ref[...] = s.astype(BF16)

        @pl.when(pl.program_id(1) == where_ref[1])
        def _():
            own_ref[...] = s

    blk = (None, tr, cols)
    return pl.pallas_call(
        body, name="rs_chip_partial",
        grid_spec=pltpu.PrefetchScalarGridSpec(
            num_scalar_prefetch=1, grid=(rows // tr, 4),
            in_specs=[pl.BlockSpec(blk, lambda i, q, w: (2 * q + w[0], i, 0)),
                      pl.BlockSpec(blk, lambda i, q, w: (q, i, 0))],
            out_specs=[pl.BlockSpec((tr, cols), lambda i, q, w: (i, 0)), pl.BlockSpec(blk, lambda i, q, w: (q, i, 0))]),
        out_shape=[_hbm((rows, cols), F32), _hbm((4, rows, cols), BF16)],
        compiler_params=_params(32),
    )(where, g, recv)


def _adam_math(w, g, m, v):
    m = ADAM_B1 * m + (1.0 - ADAM_B1) * g
    v = ADAM_B2 * v + (1.0 - ADAM_B2) * (g * g)
    m_hat = m / (1.0 - ADAM_B1 ** ADAM_STEP)
    v_hat = v / (1.0 - ADAM_B2 ** ADAM_STEP)
    delta = -ADAM_LR * (m_hat / (jnp.sqrt(v_hat) + ADAM_EPS) + ADAM_WD * w)
    return delta, m, v


def _adamw_shard(own32, recv16, w, m, v, layer, earlier):
    depth, rows, cols = w.shape
    tr = _row_tile(rows, cols, 1024 * 1024)
    n_prev = 0 if earlier is None else 4

    def body(p_ref, r_ref, w_ref, m_ref, v_ref, *rest):
        g_out, d_out, m_out, v_out = rest[n_prev:]
        g = p_ref[...] + r_ref[0].astype(F32) + r_ref[1].astype(F32) + r_ref[2].astype(F32)
        d, mn, vn = _adam_math(w_ref[...], g, m_ref[...], v_ref[...])
        g_out[...] = g
        d_out[...] = d
        m_out[...] = mn
        v_out[...] = vn

    mine = pl.BlockSpec((None, tr, cols), lambda i: (layer, i, 0))
    return pl.pallas_call(
        body, name="adamw_shard", grid=(rows // tr,),
        in_specs=[pl.BlockSpec((tr, cols), lambda i: (i, 0)), pl.BlockSpec((3, tr, cols), lambda i: (0, i, 0)),
                  mine, mine, mine] + [_anyspec()] * n_prev,
        out_specs=[mine] * 4,
        out_shape=[_hbm((depth, rows, cols), F32)] * 4,
        input_output_aliases={5 + k: k for k in range(n_prev)},
        compiler_params=_params(32),
    )(own32, recv16, *[pltpu.with_memory_space_constraint(t, pltpu.HBM) for t in (w, m, v)],
      *([] if earlier is None else earlier))


def _adamw_small(gathered, w, m, v):
    rows = w.shape[0]

    def body(a_ref, w_ref, m_ref, v_ref, g_out, d_out, m_out, v_out):
        g = a_ref[0]
        for d in range(1, N_DEV):
            g = g + a_ref[d]
        dl, mn, vn = _adam_math(w_ref[...], g, m_ref[...], v_ref[...])
        g_out[...] = g
        d_out[...] = dl
        m_out[...] = mn
        v_out[...] = vn

    return pl.pallas_call(
        body, name="adamw_small",
        in_specs=[_vspec()] * 4, out_specs=[_vspec()] * 4,
        out_shape=[_hbm((rows, 128), F32)] * 4,
        compiler_params=_params(32),
    )(gathered, w, m, v)


def _load_weights_once(pairs, sems):
    @pl.when(pl.program_id(0) == 0)
    def _():
        cps = [pltpu.make_async_copy(src, dst, sems.at[i]) for i, (src, dst) in enumerate(pairs)]
        for cp in cps:
            cp.start()
        for cp in cps:
            cp.wait()


def _ffn_fwd(x, wup, wd, ln_g, ln_b, after, target=None, tm=512):
    t_tok = x.shape[0]
    last = target is not None

    def body(x_ref, g_ref, b_ref, wup_hbm, wd_hbm, _after, *rest):
        if last:
            t_ref, dxn_ref, loss_ref, z_ref, gu_ref, wup_v, wd_v, sems = rest
        else:
            xn_ref, xnb_ref, z_ref, gu_ref, xb_ref, wup_v, wd_v, sems = rest
        _load_weights_once([(wup_hbm, wup_v), (wd_hbm, wd_v)], sems)
        xb = x_ref[...].astype(BF16)
        if not last:
            xb_ref[...] = xb
        y = None
        for lo, hi in FFN_CHUNKS:
            g = _dot_nt(xb, wup_v[0, lo:hi])
            u = _dot_nt(xb, wup_v[1, lo:hi])
            gu_ref[0, :, lo:hi] = g.astype(BF16)
            gu_ref[1, :, lo:hi] = u.astype(BF16)
            a = (g * jax.nn.sigmoid(g) * u).astype(BF16)
            part = _dot(a, wd_v[lo:hi])
            y = part if y is None else y + part
        z = ALPHA * x_ref[...] + 0.5 * y
        xhat, _ = _ln_stats(z)
        xn = xhat * g_ref[...] + b_ref[...]
        z_ref[...] = z
        if last:
            err = xn - t_ref[...]
            dxn_ref[...] = err * (1.0 / D_MODEL)
            part = jnp.sum(jnp.sum(err * err, axis=1, keepdims=True), axis=0, keepdims=True) * (0.5 / D_MODEL)

            @pl.when(pl.program_id(0) == 0)
            def _():
                loss_ref[...] = jnp.zeros_like(loss_ref)

            loss_ref[...] += part
        else:
            xn_ref[...] = xn
            xnb_ref[...] = xn.astype(BF16)

    tok = pl.BlockSpec((tm, D_MODEL), lambda i: (i, 0))
    vec = pl.BlockSpec((1, D_MODEL), lambda i: (0, 0))
    gu_spec = pl.BlockSpec((2, tm, D_FF), lambda i: (0, i, 0))
    gu_shape = _hbm((2, t_tok, D_FF), BF16)
    f32_tok, bf16_tok = _hbm((t_tok, D_MODEL), F32), _hbm((t_tok, D_MODEL), BF16)
    if last:
        extra_in, extra_spec = [target], [tok]
        out_specs = [tok, pl.BlockSpec((1, 128), lambda i: (0, 0)), tok, gu_spec]
        out_shape = [f32_tok, _hbm((1, 128), F32), f32_tok, gu_shape]
    else:
        extra_in, extra_spec = [], []
        out_specs = [tok, tok, tok, gu_spec, tok]
        out_shape = [f32_tok, bf16_tok, f32_tok, gu_shape, bf16_tok]
    return pl.pallas_call(
        body, name="ffn_fwd_loss" if last else "ffn_fwd", grid=(t_tok // tm,),
        in_specs=[tok, vec, vec, _anyspec(), _anyspec(), _anyspec()] + extra_spec,
        out_specs=out_specs, out_shape=out_shape,
        scratch_shapes=[pltpu.VMEM((2, D_FF, D_MODEL), BF16), pltpu.VMEM((D_FF, D_MODEL), BF16),
                        pltpu.SemaphoreType.DMA((2,))],
        compiler_params=_params(62, ("arbitrary",)),
    )(x, ln_g, ln_b, wup, wd, after, *extra_in)


def _ffn_bwd(dxn, z, gu, wup, wd, ln_g, after, tm=256):
    t_tok = dxn.shape[0]

    def body(dxn_ref, z_ref, gu_ref, g_ref, wup_hbm, wd_hbm, _after,
             dx_ref, dy_ref, a_ref, dgu_ref, dg_ref, db_ref, wup_v, wd_v, sems):
        i = pl.program_id(0)
        _load_weights_once([(wup_hbm, wup_v), (wd_hbm, wd_v)], sems)
        dxn_t = dxn_ref[...]
        xhat, rstd = _ln_stats(z_ref[...])
        pg = jnp.sum(dxn_t * xhat, axis=0, keepdims=True)
        pb = jnp.sum(dxn_t, axis=0, keepdims=True)

        @pl.when(i == 0)
        def _():
            dg_ref[...] = pg
            db_ref[...] = pb

        @pl.when(i > 0)
        def _():
            dg_ref[...] += pg
            db_ref[...] += pb

        dz = _ln_bwd(dxn_t, xhat, rstd, g_ref[...])
        dy = (0.5 * dz).astype(BF16)
        dy_ref[...] = dy
        dx = ALPHA * dz
        for lo, hi in FFN_BWD_CHUNKS:
            da = _dot_nt(dy, wd_v[lo:hi])
            g = gu_ref[0, :, lo:hi].astype(F32)
            u = gu_ref[1, :, lo:hi].astype(F32)
            sig = jax.nn.sigmoid(g)
            silu = g * sig
            a_ref[:, lo:hi] = (silu * u).astype(BF16)
            dg = (da * u * (sig * (1.0 + g * (1.0 - sig)))).astype(BF16)
            du = (da * silu).astype(BF16)
            dgu_ref[0, :, lo:hi] = dg
            dgu_ref[1, :, lo:hi] = du
            dx = dx + _dot(dg, wup_v[0, lo:hi]) + _dot(du, wup_v[1, lo:hi])
        dx_ref[...] = dx

    tok = pl.BlockSpec((tm, D_MODEL), lambda i: (i, 0))
    vec = pl.BlockSpec((1, D_MODEL), lambda i: (0, 0))
    gu_spec = pl.BlockSpec((2, tm, D_FF), lambda i: (0, i, 0))
    return pl.pallas_call(
        body, name="ffn_bwd", grid=(t_tok // tm,),
        in_specs=[tok, tok, gu_spec, vec, _anyspec(), _anyspec(), _anyspec()],
        out_specs=[tok, tok, pl.BlockSpec((tm, D_FF), lambda i: (i, 0)), gu_spec, vec, vec],
        out_shape=[_hbm((t_tok, D_MODEL), F32), _hbm((t_tok, D_MODEL), BF16),
                   _hbm((t_tok, D_FF), BF16), _hbm((2, t_tok, D_FF), BF16),
                   _hbm((1, D_MODEL), F32), _hbm((1, D_MODEL), F32)],
        scratch_shapes=[pltpu.VMEM((2, D_FF, D_MODEL), BF16), pltpu.VMEM((D_FF, D_MODEL), BF16),
                        pltpu.SemaphoreType.DMA((2,))],
        compiler_params=_params(60, ("arbitrary",)),
    )(dxn, z, gu, ln_g, wup, wd, after)


def _matmul_tn(a, b, after, tk=4096, bm=None):
    ga, t_tok, m = a.shape
    gb, _, n = b.shape
    groups = max(ga, gb)
    tk = min(tk, t_tok)
    bm = m if bm is None else bm

    def body(a_ref, b_ref, _after, o_ref):
        p = _dot_tn(a_ref[...].astype(BF16), b_ref[...].astype(BF16))

        @pl.when(pl.program_id(2) == 0)
        def _():
            o_ref[...] = p

        @pl.when(pl.program_id(2) > 0)
        def _():
            o_ref[...] += p

    return pl.pallas_call(
        body, name=f"matmul_tn_{m}x{n}", grid=(groups, m // bm, t_tok // tk),
        in_specs=[pl.BlockSpec((None, tk, bm), (lambda g, i, t: (g, t, i)) if ga > 1 else (lambda g, i, t: (0, t, i))),
                  pl.BlockSpec((None, tk, n), (lambda g, i, t: (g, t, 0)) if gb > 1 else (lambda g, i, t: (0, t, 0))),
                  _anyspec()],
        out_specs=pl.BlockSpec((None, bm, n), lambda g, i, t: (g, i, 0)),
        out_shape=_hbm((groups, m, n), F32),
        compiler_params=_params(56, ("arbitrary", "arbitrary", "arbitrary")),
    )(a, b, after)


def _in_proj(x, w_in, tm=512):
    t_tok = x.shape[0]

    def body(x_ref, w_ref, conv_ref, qkv_ref, sgu_ref, f_ref):
        xb = x_ref[...].astype(BF16)
        conv_ref[...] = _dot(xb, w_ref[:, COL_CONV:COL_QKV])
        qkv_ref[...] = _dot(xb, w_ref[:, COL_QKV:COL_SGU]).astype(BF16)
        sgu_ref[...] = _dot(xb, w_ref[:, COL_SGU:COL_F])
        f_ref[...] = _dot(xb, w_ref[:, COL_F:D_IN_PAD])

    def tok(n):
        return pl.BlockSpec((tm, n), lambda i: (i, 0))

    return pl.pallas_call(
        body, name="mix_in_proj", grid=(t_tok // tm,),
        in_specs=[tok(D_MODEL), pl.BlockSpec((D_MODEL, D_IN_PAD), lambda i: (0, 0))],
        out_specs=[tok(768), tok(1536), tok(512), tok(128)],
        out_shape=[_hbm((t_tok, 768), F32), _hbm((t_tok, 1536), BF16),
                   _hbm((t_tok, 512), F32), _hbm((t_tok, 128), F32)],
        compiler_params=_params(48, ("arbitrary",)),
    )(x, w_in)


def _shift_down(a, k):
    row = lax.broadcasted_iota(jnp.int32, a.shape, 0)
    return jnp.where(row >= k, pltpu.roll(a, k, 0), 0.0)


def _shift_up(a, k):
    rows = a.shape[0]
    row = lax.broadcasted_iota(jnp.int32, a.shape, 0)
    return jnp.where(row < rows - k, pltpu.roll(a, rows - k, 0), 0.0)


def _tril(n):
    return lax.broadcasted_iota(jnp.int32, (n, n), 0) >= lax.broadcasted_iota(jnp.int32, (n, n), 1)


def _sgu_group_of_lane():
    return lax.broadcasted_iota(jnp.int32, (1, D_SGU), 1) // (D_SGU // N_SGU_GROUPS)


def _log_sigmoid(x):
    return jnp.minimum(x, 0.0) - jnp.log1p(jnp.exp(-jnp.abs(x)))


def _mix_mid_fwd(conv, sgu, f, conv_w, b_f, sgu_g, sgu_b, w_s, b_mat, n_seq):
    t_tok = conv.shape[0]
    seq = t_tok // n_seq
    n_chunk = seq // SGU_CHUNK
    per_blk = ATT_BLK // SGU_CHUNK

    def body(conv_ref, sgu_ref, f_ref, cw_ref, bf_ref, lg_ref, lb_ref, ws_ref, bm_ref, cat_ref, cum_ref):
        z = conv_ref[:, 256:512] * conv_ref[:, 512:768]
        y = cw_ref[0:1, :] * _shift_down(z, 2) + cw_ref[1:2, :] * _shift_down(z, 1) + cw_ref[2:3, :] * z
        cat_ref[:, 0:D_CONV] = (conv_ref[:, 0:256] * y).astype(BF16)
        cat_ref[:, D_CONV:D_CONV + D_FOX] = jnp.zeros((seq, D_FOX), BF16)

        tril = _tril(SGU_CHUNK)
        grp = _sgu_group_of_lane()
        wc = [jnp.where(tril, ws_ref[g], 0.0).astype(BF16) for g in range(N_SGU_GROUPS)]
        tri_f = tril.astype(F32)
        carry = jnp.zeros((1, 128), F32)
        for n in range(n_chunk):
            rows = pl.ds(n * SGU_CHUNK, SGU_CHUNK)
            u = _gelu(sgu_ref[rows, 0:256])
            vhat, _ = _ln_stats(_gelu(sgu_ref[rows, 256:512]))
            vn = (vhat * lg_ref[...] + lb_ref[...]).astype(BF16)
            mixed = bm_ref[...]
            for g in range(N_SGU_GROUPS):
                mixed = mixed + jnp.where(grp == g, _dot(wc[g], vn), 0.0)
            cat_ref[rows, D_CONV + D_FOX:D_MODEL] = (u * mixed).astype(BF16)

            log_f = _log_sigmoid(f_ref[rows, :] + bf_ref[...])
            cs = _dot(tri_f, log_f, HIGHEST) + carry
            carry = cs[SGU_CHUNK - 1:SGU_CHUNK, :]
            cs_t = cs.T
            lanes = pl.ds((n % per_blk) * SGU_CHUNK, SGU_CHUNK)
            for h in range(N_HEADS):
                cum_ref[h, n // per_blk, :, lanes] = cs_t[h:h + 1, :]

    def seq_blk(n):
        return pl.BlockSpec((seq, n), lambda b: (b, 0))

    def full(shape):
        return pl.BlockSpec(shape, lambda b: (0,) * len(shape))

    return pl.pallas_call(
        body, name="mix_mid_fwd", grid=(n_seq,),
        in_specs=[seq_blk(768), seq_blk(512), seq_blk(128), full((8, 256)), full((1, 128)), full((1, 256)),
                  full((1, 256)), full((4, 128, 128)), full((128, 256))],
        out_specs=[seq_blk(D_MODEL), pl.BlockSpec((N_HEADS, seq // ATT_BLK, 1, ATT_BLK), lambda b: (b, 0, 0, 0))],
        out_shape=[_hbm((t_tok, D_MODEL), BF16), _hbm((n_seq * N_HEADS, seq // ATT_BLK, 1, ATT_BLK), F32)],
        compiler_params=_params(48, ("arbitrary",)),
    )(conv, sgu, f, conv_w, b_f, sgu_g, sgu_b, w_s, b_mat)


def _head_masks():
    lane = lax.broadcasted_iota(jnp.int32, (1, 128), 1)
    return lane < 64, lane


def _fox_fwd(qkv, cum_t, cat, n_seq):
    t_tok = qkv.shape[0]
    seq = t_tok // n_seq
    nq = seq // ATT_BLK
    blk = ATT_BLK

    def body(q_ref, k_ref, v_ref, c_ref, _cat, o_ref, lse_ref):
        qi = pl.program_id(2)
        first, _ = _head_masks()
        one = jnp.ones((1, 128), BF16)
        qh = []
        for hp in range(ATT_PAIRS):
            qs = q_ref[:, 128 * hp:128 * hp + 128] * ATT_SCALE
            zero = jnp.zeros_like(qs)
            qh += [jnp.where(first, qs, zero), jnp.where(first, zero, qs)]

        def step(kb, carry, masked):
            ms, accs = carry
            rows = pl.ds(pl.multiple_of(kb * blk, blk), blk)
            new_m, new_acc = [], []
            for hp in range(ATT_PAIRS):
                k = k_ref[rows, 128 * hp:128 * hp + 128]
                v = v_ref[rows, 128 * hp:128 * hp + 128]
                for h in range(2):
                    i = 2 * hp + h
                    s = _dot_nt(qh[i], k) - c_ref[i, kb]
                    if masked:
                        s = jnp.where(causal, s, NEG)
                    m_new = jnp.maximum(ms[i], jnp.max(s, axis=1, keepdims=True))
                    p = jnp.exp(s - m_new)
                    vh = jnp.where(first, v, one) if h == 0 else jnp.where(first, one, v)
                    new_acc.append(accs[i] * jnp.exp(ms[i] - m_new) + _dot(p.astype(BF16), vh))
                    new_m.append(m_new)
            return tuple(new_m), tuple(new_acc)

        causal = _tril(blk)
        n_heads = 2 * ATT_PAIRS
        col = jnp.full((blk, 1), NEG, F32)
        zacc = jnp.zeros((blk, 128), F32)
        carry = lax.fori_loop(0, qi, lambda kb, cr: step(kb, cr, False), ((col,) * n_heads, (zacc,) * n_heads))
        ms, accs = step(qi, carry, True)
        for hp in range(ATT_PAIRS):
            acc0, acc1 = accs[2 * hp], accs[2 * hp + 1]
            l0 = pltpu.roll(acc0, 64, 1)
            l1 = pltpu.roll(acc1, 64, 1)
            o_ref[:, 128 * hp:128 * hp + 128] = jnp.where(first, acc0 / l0, acc1 / l1).astype(BF16)
            lse_ref[:, 128 * hp:128 * hp + 128] = jnp.where(first, ms[2 * hp] + jnp.log(l0), ms[2 * hp + 1] + jnp.log(l1))

    wide = 128 * ATT_PAIRS
    n_grp = D_FOX // wide
    first_col = D_CONV // wide
    return pl.pallas_call(
        body, name="fox_fwd", grid=(n_seq, n_grp, nq),
        in_specs=[pl.BlockSpec((blk, wide), lambda b, g, qi: (b * nq + qi, g)),
                  pl.BlockSpec((seq, wide), lambda b, g, qi: (b, n_grp + g)),
                  pl.BlockSpec((seq, wide), lambda b, g, qi: (b, 2 * n_grp + g)),
                  pl.BlockSpec((2 * ATT_PAIRS, nq, 1, blk), lambda b, g, qi: (b * n_grp + g, 0, 0, 0)), _anyspec()],
        out_specs=[pl.BlockSpec((blk, wide), lambda b, g, qi: (b * nq + qi, first_col + g)),
                   pl.BlockSpec((blk, wide), lambda b, g, qi: (b * nq + qi, g))],
        out_shape=[_hbm(cat.shape, BF16), _hbm((t_tok, D_FOX), F32)],
        input_output_aliases={4: 0},
        compiler_params=_params(48, ("arbitrary", "arbitrary", "arbitrary")),
    )(qkv, qkv, qkv, cum_t, cat)


def _fox_bwd(qkv, cum_t, cat, lse, d_o, n_seq):
    t_tok = qkv.shape[0]
    seq = t_tok // n_seq
    nk = seq // ATT_BLK
    blk = ATT_BLK

    def body(q_ref, k_ref, v_ref, c_ref, o_ref, lse_ref, do_ref, dq_ref, dk_ref, dv_ref, drow_ref, dcol_ref):
        kb = pl.program_id(2)
        first, lane = _head_masks()
        second = jnp.logical_not(first)
        one = jnp.ones((1, 128), BF16)
        causal = _tril(blk)

        @pl.when(kb == 0)
        def _():
            dq_ref[...] = jnp.zeros_like(dq_ref)
            drow_ref[...] = jnp.zeros_like(drow_ref)

        def step(qi, carry, masked):
            rows = pl.ds(pl.multiple_of(qi * blk, blk), blk)
            dks, dvs = carry
            new_dk, new_dv = [], []
            for hp in range(ATT_PAIRS):
                cols = slice(128 * hp, 128 * hp + 128)
                k = k_ref[:, cols]
                v = v_ref[:, cols]
                ks = k * ATT_SCALE
                zero = jnp.zeros_like(k)
                qs = q_ref[rows, cols] * ATT_SCALE
                d_o = do_ref[rows, cols]
                dd = d_o.astype(F32) * o_ref[rows, cols].astype(F32)
                lse_t = lse_ref[rows, cols]
                dq = []
                for h, mine in enumerate((first, second)):
                    i = 2 * hp + h
                    qh = jnp.where(mine, qs, zero)
                    doh = jnp.where(mine, d_o, zero)
                    delta = jnp.sum(jnp.where(mine, dd, 0.0), axis=1, keepdims=True)
                    lse_h = jnp.sum(jnp.where(lane == 64 * h, lse_t, 0.0), axis=1, keepdims=True)
                    s = _dot_nt(qh, k) - c_ref[i]
                    if masked:
                        s = jnp.where(causal, s, NEG)
                    p = jnp.exp(s - lse_h)
                    ds = (p * (_dot_nt(doh, v) - delta)).astype(BF16)
                    new_dk.append(dks[i] + _dot_tn(ds, jnp.where(mine, qs, one)))
                    new_dv.append(dvs[i] + _dot_tn(p.astype(BF16), doh))
                    dq.append(_dot(ds, jnp.where(mine, ks, one)))
                dq_ref[rows, cols] += jnp.where(first, dq[0], dq[1])
                drow_ref[rows, cols] += jnp.where(first, dq[1], dq[0])
            return tuple(new_dk), tuple(new_dv)

        zt = (jnp.zeros((blk, 128), F32),) * (2 * ATT_PAIRS)
        carry = step(kb, (zt, zt), True)
        dks, dvs = lax.fori_loop(kb + 1, nk, lambda qi, cr: step(qi, cr, False), carry)
        for hp in range(ATT_PAIRS):
            cols = slice(128 * hp, 128 * hp + 128)
            dk_ref[:, cols] = jnp.where(first, dks[2 * hp], dks[2 * hp + 1]).astype(BF16)
            dcol_ref[:, cols] = jnp.where(first, dks[2 * hp + 1], dks[2 * hp])
            dv_ref[:, cols] = (dvs[2 * hp] + dvs[2 * hp + 1]).astype(BF16)

    wide = 128 * ATT_PAIRS
    n_grp = D_FOX // wide

    def seq_spec(col0):
        return pl.BlockSpec((seq, wide), lambda b, g, kb: (b, col0 + g))

    def key_spec(col0):
        return pl.BlockSpec((blk, wide), lambda b, g, kb: (b * nk + kb, col0 + g))

    return pl.pallas_call(
        body, name="fox_bwd", grid=(n_seq, n_grp, nk),
        in_specs=[seq_spec(0), key_spec(n_grp), key_spec(2 * n_grp),
                  pl.BlockSpec((2 * ATT_PAIRS, None, 1, blk), lambda b, g, kb: (b * n_grp + g, kb, 0, 0)),
                  seq_spec(D_CONV // wide), seq_spec(0), seq_spec(0)],
        out_specs=[seq_spec(0), key_spec(0), key_spec(0), seq_spec(0), key_spec(0)],
        out_shape=[_hbm((t_tok, D_FOX), F32), _hbm((t_tok, D_FOX), BF16),
                   _hbm((t_tok, D_FOX), BF16), _hbm((t_tok, D_FOX), F32),
                   _hbm((t_tok, D_FOX), F32)],
        compiler_params=_params(56, ("arbitrary", "arbitrary", "arbitrary")),
    )(qkv, qkv, qkv, cum_t, cat, lse, d_o)


def _mix_out_fwd(cat, x, w_out, ln_g, ln_b, tm=512):
    t_tok = x.shape[0]

    def body(cat_ref, x_ref, w_ref, g_ref, b_ref, xn_ref, xnb_ref, z_ref):
        z = ALPHA * x_ref[...] + _dot(cat_ref[...], w_ref[...])
        xhat, _ = _ln_stats(z)
        xn = xhat * g_ref[...] + b_ref[...]
        z_ref[...] = z
        xn_ref[...] = xn
        xnb_ref[...] = xn.astype(BF16)

    def tok(n):
        return pl.BlockSpec((tm, n), lambda i: (i, 0))

    vec = pl.BlockSpec((1, D_MODEL), lambda i: (0, 0))
    return pl.pallas_call(
        body, name="mix_out_fwd", grid=(t_tok // tm,),
        in_specs=[tok(D_MODEL), tok(D_MODEL), pl.BlockSpec((D_MODEL, D_MODEL), lambda i: (0, 0)), vec, vec],
        out_specs=[tok(D_MODEL)] * 3,
        out_shape=[_hbm((t_tok, D_MODEL), F32), _hbm((t_tok, D_MODEL), BF16),
                   _hbm((t_tok, D_MODEL), F32)],
        compiler_params=_params(40, ("arbitrary",)),
    )(cat, x, w_out, ln_g, ln_b)


def _mix_out_bwd(dxn, z, w_out, ln_g, tm=512):
    t_tok = dxn.shape[0]

    def body(dxn_ref, z_ref, w_ref, g_ref, dz_ref, dzb_ref, dya_ref, dyb_ref, dyc_ref, dg_ref, db_ref):
        i = pl.program_id(0)
        dxn_t = dxn_ref[...]
        xhat, rstd = _ln_stats(z_ref[...])
        pg = jnp.sum(dxn_t * xhat, axis=0, keepdims=True)
        pb = jnp.sum(dxn_t, axis=0, keepdims=True)

        @pl.when(i == 0)
        def _():
            dg_ref[...] = pg
            db_ref[...] = pb

        @pl.when(i > 0)
        def _():
            dg_ref[...] += pg
            db_ref[...] += pb

        dz = _ln_bwd(dxn_t, xhat, rstd, g_ref[...])
        dzb = dz.astype(BF16)
        dz_ref[...] = dz
        dzb_ref[...] = dzb
        dya_ref[...] = _dot_nt(dzb, w_ref[0:256, :])
        dyb_ref[...] = _dot_nt(dzb, w_ref[256:768, :]).astype(BF16)
        dyc_ref[...] = _dot_nt(dzb, w_ref[768:1024, :])

    def tok(n):
        return pl.BlockSpec((tm, n), lambda i: (i, 0))

    vec = pl.BlockSpec((1, D_MODEL), lambda i: (0, 0))
    return pl.pallas_call(
        body, name="mix_out_bwd", grid=(t_tok // tm,),
        in_specs=[tok(D_MODEL), tok(D_MODEL), pl.BlockSpec((D_MODEL, D_MODEL), lambda i: (0, 0)), vec],
        out_specs=[tok(D_MODEL), tok(D_MODEL), tok(256), tok(512), tok(256), vec, vec],
        out_shape=[_hbm((t_tok, D_MODEL), F32), _hbm((t_tok, D_MODEL), BF16),
                   _hbm((t_tok, 256), F32), _hbm((t_tok, 512), BF16),
                   _hbm((t_tok, 256), F32),
                   _hbm((1, D_MODEL), F32), _hbm((1, D_MODEL), F32)],
        compiler_params=_params(40, ("arbitrary",)),
    )(dxn, z, w_out, ln_g)


def _conv_bwd(conv, dya, conv_w, n_seq):
    t_tok = conv.shape[0]
    seq = t_tok // n_seq

    def body(conv_ref, dya_ref, cw_ref, dconv_ref, dcw_ref):
        @pl.when(pl.program_id(0) == 0)
        def _():
            dcw_ref[...] = jnp.zeros_like(dcw_ref)

        z = conv_ref[:, 256:512] * conv_ref[:, 512:768]
        z1 = _shift_down(z, 1)
        z2 = _shift_down(z, 2)
        y = cw_ref[0:1, :] * z2 + cw_ref[1:2, :] * z1 + cw_ref[2:3, :] * z
        dya_t = dya_ref[...]
        dconv_ref[:, 0:256] = (dya_t * y).astype(BF16)
        dy = dya_t * conv_ref[:, 0:256]
        dcw_ref[0:1, :] += jnp.sum(dy * z2, axis=0, keepdims=True)
        dcw_ref[1:2, :] += jnp.sum(dy * z1, axis=0, keepdims=True)
        dcw_ref[2:3, :] += jnp.sum(dy * z, axis=0, keepdims=True)
        dz = cw_ref[2:3, :] * dy + cw_ref[1:2, :] * _shift_up(dy, 1) + cw_ref[0:1, :] * _shift_up(dy, 2)
        dconv_ref[:, 256:512] = (dz * conv_ref[:, 512:768]).astype(BF16)
        dconv_ref[:, 512:768] = (dz * conv_ref[:, 256:512]).astype(BF16)

    def seq_blk(n):
        return pl.BlockSpec((seq, n), lambda b: (b, 0))

    par = pl.BlockSpec((8, 256), lambda b: (0, 0))
    return pl.pallas_call(
        body, name="conv_bwd", grid=(n_seq,),
        in_specs=[seq_blk(768), seq_blk(256), par], out_specs=[seq_blk(768), par],
        out_shape=[_hbm((t_tok, 768), BF16), _hbm((8, 256), F32)],
        compiler_params=_params(56, ("arbitrary",)),
    )(conv, dya, conv_w)


def _sgu_gate_bwd(sgu, f, dyc, drow, dcol, b_f, sgu_g, sgu_b, w_s, b_mat, n_seq):
    t_tok = sgu.shape[0]
    seq = t_tok // n_seq
    n_chunk = seq // SGU_CHUNK

    def body(sgu_ref, f_ref, dyc_ref, drow_ref, dcol_ref, bf_ref, lg_ref, lb_ref, ws_ref, bm_ref,
             dsgu_ref, df_ref, dbf_ref, dlg_ref, dlb_ref, dws_ref, dbs_ref, dbm_acc):
        b = pl.program_id(0)

        @pl.when(b == 0)
        def _():
            for r in (dbf_ref, dlg_ref, dlb_ref, dws_ref, dbm_acc):
                r[...] = jnp.zeros_like(r)

        tril = _tril(SGU_CHUNK)
        grp = _sgu_group_of_lane()
        wc = [jnp.where(tril, ws_ref[g], 0.0).astype(BF16) for g in range(N_SGU_GROUPS)]
        for n in range(n_chunk):
            rows = pl.ds(n * SGU_CHUNK, SGU_CHUNK)
            su = sgu_ref[rows, 0:256]
            sv = sgu_ref[rows, 256:512]
            u, du = _gelu_with_grad(su)
            gv, dgv = _gelu_with_grad(sv)
            vhat, rstd = _ln_stats(gv)
            vn = (vhat * lg_ref[...] + lb_ref[...]).astype(BF16)
            mixed = bm_ref[...]
            for g in range(N_SGU_GROUPS):
                mixed = mixed + jnp.where(grp == g, _dot(wc[g], vn), 0.0)
            dyc_t = dyc_ref[rows, :]
            dsgu_ref[rows, 0:256] = (dyc_t * mixed * du).astype(BF16)
            dmixed = dyc_t * u
            dbm_acc[...] += dmixed
            dvn = jnp.zeros((SGU_CHUNK, D_SGU), F32)
            for g in range(N_SGU_GROUPS):
                dm_g = jnp.where(grp == g, dmixed, 0.0).astype(BF16)
                dws_ref[g] += _dot_nt(dm_g, vn)
                dvn = dvn + _dot_tn(wc[g], dm_g)
            dlg_ref[...] += jnp.sum(dvn * vhat, axis=0, keepdims=True)
            dlb_ref[...] += jnp.sum(dvn, axis=0, keepdims=True)
            dsgu_ref[rows, 256:512] = (_ln_bwd(dvn, vhat, rstd, lg_ref[...]) * dgv).astype(BF16)

        later = (lax.broadcasted_iota(jnp.int32, (128, 128), 0) <= lax.broadcasted_iota(jnp.int32, (128, 128), 1)).astype(F32)
        head = lax.broadcasted_iota(jnp.int32, (D_FOX, 128), 1)
        pick = (lax.broadcasted_iota(jnp.int32, (D_FOX, 128), 0) == 128 * (head // 2) + 64 * (1 - head % 2)).astype(F32)
        carry = jnp.zeros((1, 128), F32)
        for n in reversed(range(n_chunk)):
            rows = pl.ds(n * SGU_CHUNK, SGU_CHUNK)
            dcum_n = _dot(drow_ref[rows, :] - dcol_ref[rows, :], pick, HIGHEST)
            dlf = _dot(later, dcum_n, HIGHEST) + carry
            carry = carry + jnp.sum(dcum_n, axis=0, keepdims=True)
            df = dlf * jax.nn.sigmoid(-(f_ref[rows, :] + bf_ref[...]))
            df_ref[rows, :] = df.astype(BF16)
            dbf_ref[...] += jnp.sum(df, axis=0, keepdims=True)

        @pl.when(b == n_seq - 1)
        def _():
            for g in range(N_SGU_GROUPS):
                dws_ref[g] = jnp.where(tril, dws_ref[g], 0.0)
            sel = (lax.broadcasted_iota(jnp.int32, (D_SGU, 128), 0) // (D_SGU // N_SGU_GROUPS)
                   == lax.broadcasted_iota(jnp.int32, (D_SGU, 128), 1)).astype(F32)
            dbs_ref[...] = _dot(dbm_acc[...], sel, HIGHEST)

    def seq_blk(n):
        return pl.BlockSpec((seq, n), lambda b: (b, 0))

    def full(shape):
        return pl.BlockSpec(shape, lambda b: (0,) * len(shape))

    param_shapes = [(1, 128), (1, 256), (1, 256), (4, 128, 128), (128, 128)]
    return pl.pallas_call(
        body, name="sgu_gate_bwd", grid=(n_seq,),
        in_specs=[seq_blk(512), seq_blk(128), seq_blk(256), seq_blk(D_FOX), seq_blk(D_FOX),
                  full((1, 128)), full((1, 256)), full((1, 256)), full((4, 128, 128)), full((128, 256))],
        out_specs=[seq_blk(512), seq_blk(128)] + [full(s) for s in param_shapes],
        out_shape=[_hbm((t_tok, 512), BF16), _hbm((t_tok, 128), BF16)]
        + [_hbm(s, F32) for s in param_shapes],
        scratch_shapes=[pltpu.VMEM((128, 256), F32)],
        compiler_params=_params(48, ("arbitrary",)),
    )(sgu, f, dyc, drow, dcol, b_f, sgu_g, sgu_b, w_s, b_mat)


def _mix_in_bwd(dconv, dq, dk, dv, dsgu, df, dz, w_in, tm=512):
    t_tok = dz.shape[0]

    def body(dconv_ref, dq_ref, dk_ref, dv_ref, dsgu_ref, df_ref, dz_ref, w_ref, dx_ref, dp_ref):
        dqb = dq_ref[...].astype(BF16)
        pieces = [(COL_CONV, dconv_ref[...]), (COL_QKV, dqb), (COL_QKV + 512, dk_ref[...]), (COL_QKV + 1024, dv_ref[...]),
                  (COL_SGU, dsgu_ref[...]), (COL_F, df_ref[...])]
        dx = ALPHA * dz_ref[...]
        for col, val in pieces:
            width = val.shape[1]
            dp_ref[:, col:col + width] = val
            dx = dx + _dot_nt(val, w_ref[:, col:col + width])
        dx_ref[...] = dx

    def tok(n):
        return pl.BlockSpec((tm, n), lambda i: (i, 0))

    return pl.pallas_call(
        body, name="mix_in_bwd", grid=(t_tok // tm,),
        in_specs=[tok(768), tok(512), tok(512), tok(512), tok(512), tok(128), tok(D_MODEL),
                  pl.BlockSpec((D_MODEL, D_IN_PAD), lambda i: (0, 0))],
        out_specs=[tok(D_MODEL), tok(D_IN_PAD)],
        out_shape=[_hbm((t_tok, D_MODEL), F32), _hbm((t_tok, D_IN_PAD), BF16)],
        compiler_params=_params(48, ("arbitrary",)),
    )(dconv, dq, dk, dv, dsgu, df, dz, w_in)


def _pad_rows(a, rows):
    return jnp.pad(a, ((0, rows - a.shape[0]), (0, 0)))


F_BLOCK = F_ORIG // D_IN_SHARD
F_AT = F_ORIG - F_BLOCK * D_IN_SHARD
assert (F_ORIG + N_HEADS) // D_IN_SHARD == F_BLOCK


def _w_in_from_blocks(g):
    fb = g[F_BLOCK]
    zeros = jnp.zeros((D_MODEL, D_IN_PAD - COL_F - N_HEADS), g.dtype)
    return jnp.concatenate([g[d] for d in range(F_BLOCK)] + [fb[:, :F_AT], fb[:, F_AT + N_HEADS:]]
                           + [g[d] for d in range(F_BLOCK + 1, N_DEV)] + [fb[:, F_AT:F_AT + N_HEADS], zeros], axis=1)


def _w_in_to_blocks(dw):
    def cols(lo, hi):
        shift = 0 if hi <= F_ORIG else N_HEADS
        return dw[:, lo - shift:hi - shift]

    blocks = []
    for d in range(N_DEV):
        lo, hi = d * D_IN_SHARD, (d + 1) * D_IN_SHARD
        if d == F_BLOCK:
            blocks.append(jnp.concatenate([cols(lo, F_ORIG), dw[:, COL_F:COL_F + N_HEADS], cols(F_ORIG + N_HEADS, hi)], axis=1))
        else:
            blocks.append(cols(lo, hi))
    return jnp.stack(blocks)


LN1_ROWS = 2 * 8
REST_ROWS = 4 * 8 + 2 * 8 + 512 + 8 + 8 + 8


def _pack_rest(p):
    rows = [p[name].reshape(8, 128) for name in ("ln2_g", "ln2_b", "ln3_g", "ln3_b")]
    rows += [_pad_rows(p[name].reshape(2, 128), 8) for name in ("sgu_ln_g", "sgu_ln_b")]
    rows += [p["sgu_w_s"].reshape(512, 128), _pad_rows(p["sgu_b_s"], 8),
             _pad_rows(jnp.pad(p["fox_b_f"], (0, 128 - N_HEADS)).reshape(1, 128), 8), _pad_rows(p["conv_w"].reshape(6, 128), 8)]
    return jnp.concatenate(rows, axis=0)


def _pack_layer(p):
    return jnp.concatenate([p["ln1_g"].reshape(8, 128), p["ln1_b"].reshape(8, 128), _pack_rest(p)], axis=0)


def _unpack_layer(a):
    r = 0

    def take(n, valid):
        nonlocal r
        piece = a[r:r + valid]
        r += n
        return piece

    d = {}
    for name in ("ln1_g", "ln1_b", "ln2_g", "ln2_b", "ln3_g", "ln3_b"):
        d[name] = take(8, 8).reshape(D_MODEL)
    for name in ("sgu_ln_g", "sgu_ln_b"):
        d[name] = take(8, 2).reshape(D_SGU)
    d["sgu_w_s"] = take(512, 512).reshape(N_SGU_GROUPS, SGU_CHUNK, SGU_CHUNK)
    d["sgu_b_s"] = take(8, 4).reshape(N_SGU_GROUPS, SGU_CHUNK)
    d["fox_b_f"] = take(8, 1).reshape(128)[:N_HEADS]
    d["conv_w"] = take(8, 6).reshape(3, D_CONV)
    return d


SMALL_NAMES = ("ln1_g", "ln1_b", "fox_b_f", "sgu_ln_g", "sgu_ln_b", "sgu_w_s", "sgu_b_s", "ln2_g", "ln2_b", "ln3_g", "ln3_b")
BIG_NAMES = ("ffn1_w_up", "ffn1_w_down", "mix_w_in", "mix_w_out", "ffn2_w_up", "ffn2_w_down")
UP_NAMES = ("ffn1_w_up", "ffn2_w_up")
WEIGHT_ORDER = ("ln1_g", "ln1_b", "ffn1_w_up", "ffn1_w_down", "mix_w_in", "fox_b_f", "conv_w", "sgu_ln_g", "sgu_ln_b",
                "sgu_w_s", "sgu_b_s", "mix_w_out", "ln2_g", "ln2_b", "ffn2_w_up", "ffn2_w_down", "ln3_g", "ln3_b")


class _Overlap:
    def __init__(self, w, after, me, where):
        self.me, self.where = me, where
        self.last = after
        groups = [[("ffn1_w_up", 0), ("ffn1_w_down", 0)],
                  [("mix_w_in", 0), ("mix_w_out", 0), ("ffn2_w_up", 0), ("ffn2_w_down", 0)]]
        groups += [[(name, l) for name in BIG_NAMES] for l in range(1, DEPTH)]
        self.gathers = []
        for gi, group in enumerate(groups):
            shards = [w[name][l].astype(BF16) for name, l in group]
            lands = [lax.dynamic_update_slice(lax.empty((N_DEV,) + s.shape, BF16), s[None], (me, 0, 0)) for s in shards]
            started = self._start(f"allgather_start_{gi}", _gather_plan(len(group)), 3 * len(group), shards + lands)
            self.gathers.append(dict(group=group, chips=started))
            if gi == 0:
                width = D_CONV // N_DEV
                rows = jnp.pad(_pad_rows(w["conv_w"].reshape(DEPTH * 3, width), 8), ((0, 0), (0, 128 - width)))
                land = lax.dynamic_update_slice(lax.empty((N_DEV,) + rows.shape, F32), rows[None], (me, 0, 0))
                self.conv_started = self._start("conv_w_start", _peers_plan(), N_DEV - 1, [rows, land])
                self.conv_full = None
        self.all_started = self.last
        self.scatters = {}
        self.order = []
        self.small = []

    def conv_w(self, after):
        if self.conv_full is None:
            width = D_CONV // N_DEV
            gathered = _exchange_wait("conv_w_wait", _peers_plan(), N_DEV - 1, self.conv_started, after)[1]
            self.conv_full = jnp.transpose(gathered[:, :DEPTH * 3, :width], (1, 0, 2)).reshape(DEPTH, 3, D_CONV)
        return self.conv_full

    def _start(self, name, plan, n_copies, arrays):
        started = _exchange_start(name, plan, n_copies, arrays, self.last)
        self.last = started[3]
        return started

    def _group_of(self, layer, part):
        return layer + 1 if layer > 0 else (0 if part == "ffn1" else 1)

    def pass_on(self, layer, part, after):
        st = self.gathers[self._group_of(layer, part)]
        if "sibling" not in st:
            gi, m = self._group_of(layer, part), len(st["group"])
            arrays = _exchange_wait(f"allgather_wait_{gi}", _gather_plan(m), 3 * m, st["chips"], after)
            st["sibling"] = self._start(f"allgather_pass_start_{gi}", _pass_on_plan(m), 4 * m, arrays)
        return st["sibling"][3]

    def weights(self, layer, part, after):
        gi = self._group_of(layer, part)
        st = self.gathers[gi]
        if "full" not in st:
            after = self.all_started if after is None else after
            self.pass_on(layer, part, after)
            m = len(st["group"])
            arrays = _exchange_wait(f"allgather_pass_wait_{gi}", _pass_on_plan(m), 4 * m, st["sibling"], after)
            st["full"] = dict(zip(st["group"], arrays[m:]))
        g = st["full"]

        def ffn(n):
            return g[(f"ffn{n}_w_up", layer)].reshape(2, D_FF, D_MODEL), g[(f"ffn{n}_w_down", layer)].reshape(D_FF, D_MODEL)

        if part == "ffn1":
            return ffn(1)
        return (_w_in_from_blocks(g[("mix_w_in", layer)]), g[("mix_w_out", layer)].reshape(D_MODEL, D_MODEL), *ffn(2))

    def push(self, key, items):
        n = len(items)
        grads = [g for _, _, g in items]
        lands = [lax.empty((4,) + g.shape[1:], F32) for g in grads]
        started = self._start(f"rs_sibling_start_{key[0]}{key[1]}", _sibling_plan(n), 4 * n, grads + lands)
        self.scatters[key] = dict(items=items, sibling=started)
        self.order.append(key)
        return started[3]

    def advance(self, key, after):
        st = self.scatters[key]
        n = len(st["items"])
        arrays = _exchange_wait(f"rs_sibling_wait_{key[0]}{key[1]}", _sibling_plan(n), 4 * n, st["sibling"], after)
        partials = [_chip_partial(g, r, self.where) for g, r in zip(arrays[:n], arrays[n:])]
        p16 = [p for _, p in partials]
        lands = [lax.empty((3,) + p.shape[1:], BF16) for p in p16]
        started = self._start(f"rs_chip_start_{key[0]}{key[1]}", _chip_plan(n), 3 * n, p16 + lands)
        st.update(own32=[p for p, _ in partials], chip=started)
        return started[3]

    def push_small(self, rows):
        k = len(self.small)
        land = lax.dynamic_update_slice(lax.empty((N_DEV,) + rows.shape, F32), rows[None], (self.me, 0, 0))
        started = self._start(f"small_start_{k}", _peers_plan(), N_DEV - 1, [rows, land])
        self.small.append(started)
        return started[3]

    def finish(self, w, m, v):
        res = {}
        after = self.scatters[self.order[-1]]["chip"][3]
        for key in self.order:
            st = self.scatters[key]
            n = len(st["items"])
            arrays = _exchange_wait(f"rs_chip_wait_{key[0]}{key[1]}", _chip_plan(n), 3 * n, st["chip"], after)
            for (name, l, _), own32, r16 in zip(st["items"], st["own32"], arrays[n:]):
                res[name] = _adamw_shard(own32, r16, w[name], m[name], v[name], l, res.get(name))
                after = res[name][0]
        pieces = [_exchange_wait(f"small_wait_{k}", _peers_plan(), N_DEV - 1, started, after)[1]
                  for k, started in enumerate(self.small)]
        return res, pieces


def _dw_up(dgu, x, after):
    return _matmul_tn(dgu, x[None], after, tk=2048, bm=DW_ROWS).reshape(N_DEV, FFN_BLK, D_MODEL)


def _dw_down(a, dy, after):
    return _matmul_tn(a[None], dy[None], after, tk=2048, bm=DW_ROWS).reshape(N_DEV, FFN_BLK // 2, D_MODEL)


def _local_step(x, target, comm, small, n_seq):
    def vec(a):
        return a.reshape(1, -1)

    saved = []
    h = x
    for l in range(DEPTH):
        s = {}
        s["up1"], s["down1"] = comm.weights(l, "ffn1", None if l == 0 else h)
        h1, h1b, s["z1"], s["gu1"], s["x0b"] = _ffn_fwd(h, s["up1"], s["down1"], vec(small["ln1_g"][l]), vec(small["ln1_b"][l]), h)
        s["w_in"], s["w_out"], s["up2"], s["down2"] = comm.weights(l, "rest", s["z1"])
        s["x1b"] = h1b
        conv, qkv, sgu, f = _in_proj(h1, s["w_in"])
        cw = _pad_rows(comm.conv_w(h1)[l], 8)
        bf = jnp.pad(small["fox_b_f"][l], (0, 128 - N_HEADS)).reshape(1, 128)
        b_mat = jnp.repeat(small["sgu_b_s"][l].T, D_SGU // N_SGU_GROUPS, axis=1)
        mid_params = (cw, bf, vec(small["sgu_ln_g"][l]), vec(small["sgu_ln_b"][l]), small["sgu_w_s"][l], b_mat)
        cat, cum_t = _mix_mid_fwd(conv, sgu, f, *mid_params, n_seq)
        cat, lse = _fox_fwd(qkv, cum_t, cat, n_seq)
        h2, h2b, s["z2"] = _mix_out_fwd(cat, h1, s["w_out"], vec(small["ln2_g"][l]), vec(small["ln2_b"][l]))
        s.update(conv=conv, qkv=qkv, sgu=sgu, f=f, mid_params=mid_params, cat=cat, cum_t=cum_t, lse=lse, x2b=h2b)
        token = comm.pass_on(l + 1, "ffn1", s["z2"]) if l + 1 < DEPTH else h2
        ln3 = (vec(small["ln3_g"][l]), vec(small["ln3_b"][l]))
        if l + 1 < DEPTH:
            h, _, s["z3"], s["gu2"], _ = _ffn_fwd(h2, s["up2"], s["down2"], *ln3, token)
        else:
            dh, loss, s["z3"], s["gu2"] = _ffn_fwd(h2, s["up2"], s["down2"], *ln3, token, target)
        saved.append(s)

    late_rows = None
    token = loss
    pending = None
    for l in reversed(range(DEPTH)):
        s = saved[l]
        sg = {}
        dh, dy, a, dgu, sg["ln3_g"], sg["ln3_b"] = _ffn_bwd(dh, s["z3"], s["gu2"], s["up2"], s["down2"], vec(small["ln3_g"][l]), token)
        if pending is not None:
            token = comm.advance(pending, dh)
        g_up2 = _dw_up(dgu, s["x2b"], token)
        g_down2 = _dw_down(a, dy, token)
        dz, dzb, dya, dyb, dyc, sg["ln2_g"], sg["ln2_b"] = _mix_out_bwd(dh, s["z2"], s["w_out"], vec(small["ln2_g"][l]))
        g_out = _matmul_tn(s["cat"][None], dzb[None], token).reshape(N_DEV, D_MODEL // N_DEV, D_MODEL)
        dq, dk, dv, drow, dcol = _fox_bwd(s["qkv"], s["cum_t"], s["cat"], s["lse"], dyb, n_seq)
        dconv, dcw = _conv_bwd(s["conv"], dya, s["mid_params"][0], n_seq)
        dsgu, df, dbf, dlg, dlb, dws, dbs = _sgu_gate_bwd(s["sgu"], s["f"], dyc, drow, dcol, *s["mid_params"][1:], n_seq)
        sg.update(conv_w=dcw[:3], fox_b_f=dbf[0, :N_HEADS], sgu_ln_g=dlg[0], sgu_ln_b=dlb[0], sgu_w_s=dws,
                  sgu_b_s=dbs[:, :N_SGU_GROUPS].T)
        dh, dp = _mix_in_bwd(dconv, dq, dk, dv, dsgu, df, dz, s["w_in"])
        g_in = _w_in_to_blocks(_matmul_tn(s["x1b"][None], dp[None], token, tk=2048, bm=D_MODEL // 2)[0])
        first = [("ffn2_w_up", l, g_up2), ("ffn2_w_down", l, g_down2), ("mix_w_out", l, g_out), ("mix_w_in", l, g_in)]
        for name in ("ln2_g", "ln2_b", "ln3_g", "ln3_b"):
            sg[name] = sg[name][0]
        if l == 0:
            comm.push((l, "a"), first)
            token = comm.push_small(_pack_rest(sg))
            pending, first = (l, "a"), []
        dh, dy, a, dgu, dg1, db1 = _ffn_bwd(dh, s["z1"], s["gu1"], s["up1"], s["down1"], vec(small["ln1_g"][l]), token)
        if l == 0:
            token = comm.advance(pending, dh)
        g_up1 = _dw_up(dgu, s["x0b"], token)
        ln1_rows = jnp.concatenate([dg1.reshape(8, 128), db1.reshape(8, 128)], axis=0)
        if l == 0:
            token = comm.push((l, "b"), [("ffn1_w_up", l, g_up1)])
            g_down1 = _dw_down(a, dy, token)
            token = comm.advance((l, "b"), g_down1)
            token = comm.push((l, "c"), [("ffn1_w_down", l, g_down1)])
            token = comm.advance((l, "c"), token)
            late_rows = ln1_rows
        else:
            g_down1 = _dw_down(a, dy, token)
            pending = (l, "b")
            comm.push(pending, first + [("ffn1_w_up", l, g_up1), ("ffn1_w_down", l, g_down1)])
            token = comm.push_small(jnp.concatenate([ln1_rows, _pack_rest(sg)], axis=0))
    return loss, dh, late_rows


def kernel(x, ln1_g, ln1_b, ffn1_w_up, ffn1_w_down, mix_w_in, fox_b_f, conv_w, sgu_ln_g, sgu_ln_b, sgu_w_s, sgu_b_s, mix_w_out, ln2_g, ln2_b, ffn2_w_up, ffn2_w_down, ln3_g, ln3_b, loss_target, m_ln1_g, m_ln1_b, m_ffn1_w_up, m_ffn1_w_down, m_mix_w_in, m_fox_b_f, m_conv_w, m_sgu_ln_g, m_sgu_ln_b, m_sgu_w_s, m_sgu_b_s, m_mix_w_out, m_ln2_g, m_ln2_b, m_ffn2_w_up, m_ffn2_w_down, m_ln3_g, m_ln3_b, v_ln1_g, v_ln1_b, v_ffn1_w_up, v_ffn1_w_down, v_mix_w_in, v_fox_b_f, v_conv_w, v_sgu_ln_g, v_sgu_ln_b, v_sgu_w_s, v_sgu_b_s, v_mix_w_out, v_ln2_g, v_ln2_b, v_ffn2_w_up, v_ffn2_w_down, v_ln3_g, v_ln3_b):
    w = dict(ln1_g=ln1_g, ln1_b=ln1_b, ffn1_w_up=ffn1_w_up, ffn1_w_down=ffn1_w_down, mix_w_in=mix_w_in, fox_b_f=fox_b_f,
             conv_w=conv_w, sgu_ln_g=sgu_ln_g, sgu_ln_b=sgu_ln_b, sgu_w_s=sgu_w_s, sgu_b_s=sgu_b_s, mix_w_out=mix_w_out,
             ln2_g=ln2_g, ln2_b=ln2_b, ffn2_w_up=ffn2_w_up, ffn2_w_down=ffn2_w_down, ln3_g=ln3_g, ln3_b=ln3_b)
    m = dict(ln1_g=m_ln1_g, ln1_b=m_ln1_b, ffn1_w_up=m_ffn1_w_up, ffn1_w_down=m_ffn1_w_down, mix_w_in=m_mix_w_in,
             fox_b_f=m_fox_b_f, conv_w=m_conv_w, sgu_ln_g=m_sgu_ln_g, sgu_ln_b=m_sgu_ln_b, sgu_w_s=m_sgu_w_s,
             sgu_b_s=m_sgu_b_s, mix_w_out=m_mix_w_out, ln2_g=m_ln2_g, ln2_b=m_ln2_b, ffn2_w_up=m_ffn2_w_up,
             ffn2_w_down=m_ffn2_w_down, ln3_g=m_ln3_g, ln3_b=m_ln3_b)
    v = dict(ln1_g=v_ln1_g, ln1_b=v_ln1_b, ffn1_w_up=v_ffn1_w_up, ffn1_w_down=v_ffn1_w_down, mix_w_in=v_mix_w_in,
             fox_b_f=v_fox_b_f, conv_w=v_conv_w, sgu_ln_g=v_sgu_ln_g, sgu_ln_b=v_sgu_ln_b, sgu_w_s=v_sgu_w_s,
             sgu_b_s=v_sgu_b_s, mix_w_out=v_mix_w_out, ln2_g=v_ln2_g, ln2_b=v_ln2_b, ffn2_w_up=v_ffn2_w_up,
             ffn2_w_down=v_ffn2_w_down, ln3_g=v_ln3_g, ln3_b=v_ln3_b)

    mx, my, mc = lax.axis_index("x"), lax.axis_index("y"), lax.axis_index("c")
    me = 4 * mx + 2 * my + mc
    n_seq, seq, _ = x.shape
    t_tok = n_seq * seq
    for name in UP_NAMES:
        for t in (w, m, v):
            t[name] = jnp.transpose(t[name], (0, 2, 1))

    comm = _Overlap(w, x, me, jnp.stack([mc, 2 * mx + my]).astype(jnp.int32))
    small = {name: w[name] for name in SMALL_NAMES}

    loss_dev, grad_x, late_rows = _local_step(
        x.reshape(t_tok, D_MODEL), loss_target.reshape(t_tok, D_MODEL), comm, small, n_seq)
    loss = lax.psum(loss_dev[0, 0], ("x", "y", "c"))
    out, pieces = comm.finish(w, m, v)
    for name in UP_NAMES:
        out[name] = [jnp.transpose(a, (0, 2, 1)) for a in out[name]]

    pieces.append(_allgather_small(late_rows))
    spans = [(l, 0, LN1_ROWS + REST_ROWS) for l in reversed(range(1, DEPTH))] + [(0, LN1_ROWS, LN1_ROWS + REST_ROWS), (0, 0, LN1_ROWS)]

    def widen(a):
        return lax.dynamic_update_slice(jnp.zeros((3, D_CONV), F32), a, (0, me * (D_CONV // N_DEV)))

    packed = [[_pack_layer({**{name: t[name][l] for name in SMALL_NAMES}, "conv_w": widen(t["conv_w"][l])}) for l in range(DEPTH)]
              for t in (w, m, v)]
    rows_out = {}
    for (l, lo, hi), gathered_piece in zip(spans, pieces):
        rows_out[(l, lo)] = _adamw_small(gathered_piece, *[packed[t][l][lo:hi] for t in range(3)])
    per_layer = []
    for l in range(DEPTH):
        parts = sorted(lo for (ll, lo) in rows_out if ll == l)
        per_layer.append([_unpack_layer(jnp.concatenate([rows_out[(l, lo)][k] for lo in parts], axis=0)) for k in range(4)])
    for name in SMALL_NAMES:
        out[name] = [jnp.stack([per_layer[l][k][name] for l in range(DEPTH)]) for k in range(4)]
    lo_col = me * (D_CONV // N_DEV)
    out["conv_w"] = [jnp.stack([lax.dynamic_slice(per_layer[l][k]["conv_w"], (0, lo_col), (3, D_CONV // N_DEV)) for l in range(DEPTH)])
                     for k in range(4)]

    return (loss, grad_x.reshape(x.shape), *[out[name][0] for name in WEIGHT_ORDER], *[out[name][1] for name in WEIGHT_ORDER],
            *[out[name][2] for name in WEIGHT_ORDER], *[out[name][3] for name in WEIGHT_ORDER])
```

```python
import functools

import jax
import jax.numpy as jnp
from jax import lax
from jax.experimental import pallas as pl
from jax.experimental.pallas import tpu as pltpu

F32 = jnp.float32
BF16 = jnp.bfloat16
MESH = pl.DeviceIdType.MESH

N_DEV = 8
DEPTH = 2
D_MODEL = 1024
D_FF = 2816
FFN_BLK = 2 * D_FF // N_DEV
MXU_TILE_V7X = 256
FFN_CHUNKS = tuple((lo, min(lo + 3 * MXU_TILE_V7X, D_FF)) for lo in range(0, D_FF, 3 * MXU_TILE_V7X))
FFN_BWD_CHUNKS = tuple((lo, min(lo + 4 * MXU_TILE_V7X, D_FF)) for lo in range(0, D_FF, 4 * MXU_TILE_V7X))
DW_ROWS = D_FF // 2
D_CONV = 256
D_FOX = 512
N_HEADS = 8
D_SGU = 256
N_SGU_GROUPS = 4
SGU_CHUNK = 128
D_IN = 3 * D_CONV + 3 * D_FOX + N_HEADS + 2 * D_SGU
D_IN_SHARD = D_IN // N_DEV
COL_CONV, COL_QKV, COL_SGU, COL_F = 0, 768, 2304, 2816
D_IN_PAD = 2944
F_ORIG = 3 * D_CONV + 3 * D_FOX
ALPHA = (2 * DEPTH) ** 0.25
LN_EPS = 1e-5
ATT_SCALE = 0.125
ATT_BLK = 512
ATT_PAIRS = 2
NEG = -1e30

ADAM_LR, ADAM_B1, ADAM_B2, ADAM_EPS, ADAM_WD, ADAM_STEP = 0.001, 0.9, 0.999, 1e-08, 0.01, 10

VMEM_BYTES_V7X = 64 * 1024 * 1024
HIGHEST = lax.Precision.HIGHEST


def _params(vmem_mb, sem=None):
    assert vmem_mb * 1024 * 1024 < VMEM_BYTES_V7X
    kw = dict(vmem_limit_bytes=vmem_mb * 1024 * 1024)
    if sem is not None:
        kw["dimension_semantics"] = sem
    return pltpu.CompilerParams(**kw)


def _dot(a, b, precision=None):
    return lax.dot_general(a, b, (((1,), (0,)), ((), ())), preferred_element_type=F32, precision=precision)


def _dot_nt(a, b):
    return lax.dot_general(a, b, (((1,), (1,)), ((), ())), preferred_element_type=F32)


def _dot_tn(a, b):
    return lax.dot_general(a, b, (((0,), (0,)), ((), ())), preferred_element_type=F32)


def _ln_stats(z):
    mu = jnp.mean(z, axis=-1, keepdims=True)
    zc = z - mu
    var = jnp.mean(zc * zc, axis=-1, keepdims=True)
    rstd = lax.rsqrt(var + LN_EPS)
    return zc * rstd, rstd


def _ln_bwd(dy, xhat, rstd, g):
    dxh = dy * g
    m1 = jnp.mean(dxh, axis=-1, keepdims=True)
    m2 = jnp.mean(dxh * xhat, axis=-1, keepdims=True)
    return rstd * (dxh - m1 - xhat * m2)


_GELU_C = 0.7978845608028654


def _gelu(x):
    return 0.5 * x * (1.0 + jnp.tanh(_GELU_C * (x + 0.044715 * x * x * x)))


def _gelu_with_grad(x):
    t = jnp.tanh(_GELU_C * (x + 0.044715 * x * x * x))
    return 0.5 * x * (1.0 + t), 0.5 * (1.0 + t) + 0.5 * x * (1.0 - t * t) * _GELU_C * (1.0 + 3 * 0.044715 * x * x)


def _hbm(shape, dtype):
    n = 1
    for d in shape:
        n *= d
    if n * jnp.dtype(dtype).itemsize >= 1024 * 1024:
        return pltpu.HBM(tuple(shape), dtype)
    return jax.ShapeDtypeStruct(tuple(shape), dtype)


def _vspec():
    return pl.BlockSpec(memory_space=pltpu.VMEM)


def _anyspec():
    return pl.BlockSpec(memory_space=pl.ANY)


def _mesh_pos():
    return lax.axis_index("x"), lax.axis_index("y"), lax.axis_index("c")


def _other_chips(x, y):
    return [(1 - x, y), (x, 1 - y), (1 - x, 1 - y)]


_HBM_SPEC = pl.BlockSpec(memory_space=pltpu.HBM)
_SEM_SPEC = pl.BlockSpec(memory_space=pltpu.SEMAPHORE)
_DATAFLOW_EFFECT = pltpu.SideEffectType.DATAFLOW_SIDE_EFFECTING


def _remote_copies(plan, refs, send_sems, recv_sems):
    return [pltpu.make_async_remote_copy(src_ref=src, dst_ref=dst, send_sem=send_sems.at[k], recv_sem=recv_sems.at[k],
                                         device_id=to, device_id_type=MESH)
            for k, (src, dst, to) in enumerate(plan(refs, *_mesh_pos()))]


def _exchange_start(name, plan, n_copies, arrays, after):
    n = len(arrays)

    def body(*refs):
        send_sems, recv_sems, token = refs[n + 1], refs[n + 2], refs[-1]
        for cp in _remote_copies(plan, refs[:n], send_sems, recv_sems):
            cp.start()
        token[...] = jnp.zeros_like(token)

    out = pl.pallas_call(
        body, name=name,
        out_shape=(pltpu.SemaphoreType.DMA((n_copies,)), pltpu.SemaphoreType.DMA((n_copies,)),
                   *[pltpu.HBM(a.shape, a.dtype) for a in arrays], _hbm((8, 128), F32)),
        in_specs=[_HBM_SPEC] * n + [_anyspec()],
        out_specs=(_SEM_SPEC, _SEM_SPEC, *[_HBM_SPEC] * n, _vspec()),
        input_output_aliases={i: 2 + i for i in range(n)},
        compiler_params=pltpu.CompilerParams(has_side_effects=_DATAFLOW_EFFECT),
    )(*[pltpu.with_memory_space_constraint(a, pltpu.HBM) for a in arrays], after)
    return out[0], out[1], list(out[2:2 + n]), out[-1]


def _exchange_wait(name, plan, n_copies, started, after):
    send_sems, recv_sems, arrays, _ = started
    n = len(arrays)

    def body(*refs):
        for cp in _remote_copies(plan, refs[:n], refs[n], refs[n + 1]):
            cp.wait_send()
            cp.wait_recv()

    out = pl.pallas_call(
        body, name=name,
        out_shape=tuple(pltpu.HBM(a.shape, a.dtype) for a in arrays),
        in_specs=[_HBM_SPEC] * n + [_SEM_SPEC, _SEM_SPEC, _anyspec()], out_specs=tuple([_HBM_SPEC] * n),
        input_output_aliases={i: i for i in range(n)},
        compiler_params=pltpu.CompilerParams(has_side_effects=_DATAFLOW_EFFECT),
    )(*arrays, send_sems, recv_sems, after)
    return list(out)


def _gather_plan(m):
    def plan(refs, x, y, c):
        me = 4 * x + 2 * y + c
        return [(refs[i], refs[m + i].at[me], (*chip, c)) for i in range(m) for chip in _other_chips(x, y)]
    return plan


def _pass_on_plan(m):
    def plan(refs, x, y, c):
        out = []
        for i in range(m):
            out.append((refs[i], refs[m + i].at[4 * x + 2 * y + c], (x, y, 1 - c)))
            for cx, cy in _other_chips(x, y):
                block = refs[m + i].at[4 * cx + 2 * cy + c]
                out.append((block, block, (x, y, 1 - c)))
        return out
    return plan


def _peers_plan():
    def plan(refs, x, y, c):
        rel = [(dx, dy, dc) for dx in (0, 1) for dy in (0, 1) for dc in (0, 1)][1:]
        return [(refs[0], refs[1].at[4 * x + 2 * y + c], (x ^ dx, y ^ dy, c ^ dc)) for dx, dy, dc in rel]
    return plan


def _allgather_small(v):
    rows = v.shape[0]

    def body(v_ref, out_ref, send_sems, recv_sems):
        x, y, c = _mesh_pos()
        me = 4 * x + 2 * y + c
        out_ref[me] = v_ref[...]
        rel = [(dx, dy, dc) for dx in (0, 1) for dy in (0, 1) for dc in (0, 1)][1:]
        copies = []
        for k, (dx, dy, dc) in enumerate(rel):
            to = (x ^ dx, y ^ dy, c ^ dc)
            copies.append(pltpu.make_async_remote_copy(
                src_ref=v_ref, dst_ref=out_ref.at[me], send_sem=send_sems.at[k], recv_sem=recv_sems.at[k],
                device_id=to, device_id_type=MESH))
        for cp in copies:
            cp.start()
        for k, (dx, dy, dc) in enumerate(rel):
            src_blk = 4 * (x ^ dx) + 2 * (y ^ dy) + (c ^ dc)
            pltpu.make_async_remote_copy(
                src_ref=v_ref, dst_ref=out_ref.at[src_blk], send_sem=send_sems.at[k], recv_sem=recv_sems.at[k],
                device_id=(x, y, c), device_id_type=MESH).wait_recv()
        for cp in copies:
            cp.wait_send()

    return pl.pallas_call(
        body, name="allgather_small",
        out_shape=jax.ShapeDtypeStruct((N_DEV, rows, 128), v.dtype),
        in_specs=[_vspec()], out_specs=_vspec(),
        scratch_shapes=[pltpu.SemaphoreType.DMA((7,)), pltpu.SemaphoreType.DMA((7,))],
        compiler_params=_params(24),
    )(v)


def _sibling_plan(n):
    def plan(refs, x, y, c):
        return [(refs[a].at[2 * q + (1 - c)], refs[n + a].at[q], (x, y, 1 - c)) for a in range(n) for q in range(4)]
    return plan


def _chip_plan(n):
    def plan(refs, x, y, c):
        return [(refs[a].at[2 * cx + cy], refs[n + a].at[j], (cx, cy, c))
                for a in range(n) for j, (cx, cy) in enumerate(_other_chips(x, y))]
    return plan


def _row_tile(rows, cols, budget_bytes=2 * 1024 * 1024):
    best = 8
    for t in range(8, rows + 1, 8):
        if rows % t == 0 and t * cols * 4 <= budget_bytes:
            best = t
    return best


def _chip_partial(g, recv, where):
    _, rows, cols = g.shape
    tr = _row_tile(rows, cols)

    def body(where_ref, g_ref, r_ref, own_ref, o16_ref):
        s = g_ref[...] + r_ref[...]
        o16_ref[...] = s.astype(BF16)

        @pl.when(pl.program_id(1) == where_ref[1])
        def _():
            own_ref[...] = s

    blk = (None, tr, cols)
    return pl.pallas_call(
        body, name="rs_chip_partial",
        grid_spec=pltpu.PrefetchScalarGridSpec(
            num_scalar_prefetch=1, grid=(rows // tr, 4),
            in_specs=[pl.BlockSpec(blk, lambda i, q, w: (2 * q + w[0], i, 0)),
                      pl.BlockSpec(blk, lambda i, q, w: (q, i, 0))],
            out_specs=[pl.BlockSpec((tr, cols), lambda i, q, w: (i, 0)), pl.BlockSpec(blk, lambda i, q, w: (q, i, 0))]),
        out_shape=[_hbm((rows, cols), F32), _hbm((4, rows, cols), BF16)],
        compiler_params=_params(32),
    )(where, g, recv)


def _adam_math(w, g, m, v):
    m = ADAM_B1 * m + (1.0 - ADAM_B1) * g
    v = ADAM_B2 * v + (1.0 - ADAM_B2) * (g * g)
    m_hat = m / (1.0 - ADAM_B1 ** ADAM_STEP)
    v_hat = v / (1.0 - ADAM_B2 ** ADAM_STEP)
    delta = -ADAM_LR * (m_hat / (jnp.sqrt(v_hat) + ADAM_EPS) + ADAM_WD * w)
    return delta, m, v


def _adamw_shard(own32, recv16, w, m, v, layer, earlier):
    depth, rows, cols = w.shape
    tr = _row_tile(rows, cols, 1024 * 1024)
    n_prev = 0 if earlier is None else 4

    def body(p_ref, r_ref, w_ref, m_ref, v_ref, *rest):
        g_out, d_out, m_out, v_out = rest[n_prev:]
        g = p_ref[...] + r_ref[0].astype(F32) + r_ref[1].astype(F32) + r_ref[2].astype(F32)
        d, mn, vn = _adam_math(w_ref[...], g, m_ref[...], v_ref[...])
        g_out[...] = g
        d_out[...] = d
        m_out[...] = mn
        v_out[...] = vn

    mine = pl.BlockSpec((None, tr, cols), lambda i: (layer, i, 0))
    return pl.pallas_call(
        body, name="adamw_shard", grid=(rows // tr,),
        in_specs=[pl.BlockSpec((tr, cols), lambda i: (i, 0)), pl.BlockSpec((3, tr, cols), lambda i: (0, i, 0)),
                  mine, mine, mine] + [_anyspec()] * n_prev,
        out_specs=[mine] * 4,
        out_shape=[_hbm((depth, rows, cols), F32)] * 4,
        input_output_aliases={5 + k: k for k in range(n_prev)},
        compiler_params=_params(32),
    )(own32, recv16, *[pltpu.with_memory_space_constraint(t, pltpu.HBM) for t in (w, m, v)],
      *([] if earlier is None else earlier))


def _adamw_small(gathered, w, m, v):
    rows = w.shape[0]

    def body(a_ref, w_ref, m_ref, v_ref, g_out, d_out, m_out, v_out):
        g = a_ref[0]
        for d in range(1, N_DEV):
            g = g + a_ref[d]
        dl, mn, vn = _adam_math(w_ref[...], g, m_ref[...], v_ref[...])
        g_out[...] = g
        d_out[...] = dl
        m_out[...] = mn
        v_out[...] = vn

    return pl.pallas_call(
        body, name="adamw_small",
        in_specs=[_vspec()] * 4, out_specs=[_vspec()] * 4,
        out_shape=[_hbm((rows, 128), F32)] * 4,
        compiler_params=_params(32),
    )(gathered, w, m, v)


def _load_weights_once(pairs, sems):
    @pl.when(pl.program_id(0) == 0)
    def _():
        cps = [pltpu.make_async_copy(src, dst, sems.at[i]) for i, (src, dst) in enumerate(pairs)]
        for cp in cps:
            cp.start()
        for cp in cps:
            cp.wait()


def _ffn_fwd(x, wup, wd, ln_g, ln_b, after, target=None, tm=512):
    t_tok = x.shape[0]
    last = target is not None

    def body(x_ref, g_ref, b_ref, wup_hbm, wd_hbm, _after, *rest):
        if last:
            t_ref, dxn_ref, loss_ref, z_ref, gu_ref, wup_v, wd_v, sems = rest
        else:
            xn_ref, xnb_ref, z_ref, gu_ref, xb_ref, wup_v, wd_v, sems = rest
        _load_weights_once([(wup_hbm, wup_v), (wd_hbm, wd_v)], sems)
        xb = x_ref[...].astype(BF16)
        if not last:
            xb_ref[...] = xb
        y = None
        for lo, hi in FFN_CHUNKS:
            g = _dot_nt(xb, wup_v[0, lo:hi])
            u = _dot_nt(xb, wup_v[1, lo:hi])
            gu_ref[0, :, lo:hi] = g.astype(BF16)
            gu_ref[1, :, lo:hi] = u.astype(BF16)
            a = (g * jax.nn.sigmoid(g) * u).astype(BF16)
            part = _dot(a, wd_v[lo:hi])
            y = part if y is None else y + part
        z = ALPHA * x_ref[...] + 0.5 * y
        xhat, _ = _ln_stats(z)
        xn = xhat * g_ref[...] + b_ref[...]
        z_ref[...] = z
        if last:
            err = xn - t_ref[...]
            dxn_ref[...] = err * (1.0 / D_MODEL)
            part = jnp.sum(jnp.sum(err * err, axis=1, keepdims=True), axis=0, keepdims=True) * (0.5 / D_MODEL)

            @pl.when(pl.program_id(0) == 0)
            def _():
                loss_ref[...] = jnp.zeros_like(loss_ref)

            loss_ref[...] += part
        else:
            xn_ref[...] = xn
            xnb_ref[...] = xn.astype(BF16)

    tok = pl.BlockSpec((tm, D_MODEL), lambda i: (i, 0))
    vec = pl.BlockSpec((1, D_MODEL), lambda i: (0, 0))
    gu_spec = pl.BlockSpec((2, tm, D_FF), lambda i: (0, i, 0))
    gu_shape = _hbm((2, t_tok, D_FF), BF16)
    f32_tok, bf16_tok = _hbm((t_tok, D_MODEL), F32), _hbm((t_tok, D_MODEL), BF16)
    if last:
        extra_in, extra_spec = [target], [tok]
        out_specs = [tok, pl.BlockSpec((1, 128), lambda i: (0, 0)), tok, gu_spec]
        out_shape = [f32_tok, _hbm((1, 128), F32), f32_tok, gu_shape]
    else:
        extra_in, extra_spec = [], []
        out_specs = [tok, tok, tok, gu_spec, tok]
        out_shape = [f32_tok, bf16_tok, f32_tok, gu_shape, bf16_tok]
    return pl.pallas_call(
        body, name="ffn_fwd_loss" if last else "ffn_fwd", grid=(t_tok // tm,),
        in_specs=[tok, vec, vec, _anyspec(), _anyspec(), _anyspec()] + extra_spec,
        out_specs=out_specs, out_shape=out_shape,
        scratch_shapes=[pltpu.VMEM((2, D_FF, D_MODEL), BF16), pltpu.VMEM((D_FF, D_MODEL), BF16),
                        pltpu.SemaphoreType.DMA((2,))],
        compiler_params=_params(62, ("arbitrary",)),
    )(x, ln_g, ln_b, wup, wd, after, *extra_in)


def _ffn_bwd(dxn, z, gu, wup, wd, ln_g, after, tm=256):
    t_tok = dxn.shape[0]

    def body(dxn_ref, z_ref, gu_ref, g_ref, wup_hbm, wd_hbm, _after,
             dx_ref, dy_ref, a_ref, dgu_ref, dg_ref, db_ref, wup_v, wd_v, sems):
        i = pl.program_id(0)
        _load_weights_once([(wup_hbm, wup_v), (wd_hbm, wd_v)], sems)
        dxn_t = dxn_ref[...]
        xhat, rstd = _ln_stats(z_ref[...])
        pg = jnp.sum(dxn_t * xhat, axis=0, keepdims=True)
        pb = jnp.sum(dxn_t, axis=0, keepdims=True)

        @pl.when(i == 0)
        def _():
            dg_ref[...] = pg
            db_ref[...] = pb

        @pl.when(i > 0)
        def _():
            dg_ref[...] += pg
            db_ref[...] += pb

        dz = _ln_bwd(dxn_t, xhat, rstd, g_ref[...])
        dy = (0.5 * dz).astype(BF16)
        dy_ref[...] = dy
        dx = ALPHA * dz
        for lo, hi in FFN_BWD_CHUNKS:
            da = _dot_nt(dy, wd_v[lo:hi])
            g = gu_ref[0, :, lo:hi].astype(F32)
            u = gu_ref[1, :, lo:hi].astype(F32)
            sig = jax.nn.sigmoid(g)
            silu = g * sig
            a_ref[:, lo:hi] = (silu * u).astype(BF16)
            dg = (da * u * (sig * (1.0 + g * (1.0 - sig)))).astype(BF16)
            du = (da * silu).astype(BF16)
            dgu_ref[0, :, lo:hi] = dg
            dgu_ref[1, :, lo:hi] = du
            dx = dx + _dot(dg, wup_v[0, lo:hi]) + _dot(du, wup_v[1, lo:hi])
        dx_ref[...] = dx

    tok = pl.BlockSpec((tm, D_MODEL), lambda i: (i, 0))
    vec = pl.BlockSpec((1, D_MODEL), lambda i: (0, 0))
    gu_spec = pl.BlockSpec((2, tm, D_FF), lambda i: (0, i, 0))
    return pl.pallas_call(
        body, name="ffn_bwd", grid=(t_tok // tm,),
        in_specs=[tok, tok, gu_spec, vec, _anyspec(), _anyspec(), _anyspec()],
        out_specs=[tok, tok, pl.BlockSpec((tm, D_FF), lambda i: (i, 0)), gu_spec, vec, vec],
        out_shape=[_hbm((t_tok, D_MODEL), F32), _hbm((t_tok, D_MODEL), BF16),
                   _hbm((t_tok, D_FF), BF16), _hbm((2, t_tok, D_FF), BF16),
                   _hbm((1, D_MODEL), F32), _hbm((1, D_MODEL), F32)],
        scratch_shapes=[pltpu.VMEM((2, D_FF, D_MODEL), BF16), pltpu.VMEM((D_FF, D_MODEL), BF16),
                        pltpu.SemaphoreType.DMA((2,))],
        compiler_params=_params(60, ("arbitrary",)),
    )(dxn, z, gu, ln_g, wup, wd, after)


def _matmul_tn(a, b, after, tk=4096, bm=None):
    ga, t_tok, m = a.shape
    gb, _, n = b.shape
    groups = max(ga, gb)
    tk = min(tk, t_tok)
    bm = m if bm is None else bm

    def body(a_ref, b_ref, _after, o_ref):
        p = _dot_tn(a_ref[...].astype(BF16), b_ref[...].astype(BF16))

        @pl.when(pl.program_id(2) == 0)
        def _():
            o_ref[...] = p

        @pl.when(pl.program_id(2) > 0)
        def _():
            o_ref[...] += p

    return pl.pallas_call(
        body, name=f"matmul_tn_{m}x{n}", grid=(groups, m // bm, t_tok // tk),
        in_specs=[pl.BlockSpec((None, tk, bm), (lambda g, i, t: (g, t, i)) if ga > 1 else (lambda g, i, t: (0, t, i))),
                  pl.BlockSpec((None, tk, n), (lambda g, i, t: (g, t, 0)) if gb > 1 else (lambda g, i, t: (0, t, 0))),
                  _anyspec()],
        out_specs=pl.BlockSpec((None, bm, n), lambda g, i, t: (g, i, 0)),
        out_shape=_hbm((groups, m, n), F32),
        compiler_params=_params(56, ("arbitrary", "arbitrary", "arbitrary")),
    )(a, b, after)


def _in_proj(x, w_in, tm=512):
    t_tok = x.shape[0]

    def body(x_ref, w_ref, conv_ref, qkv_ref, sgu_ref, f_ref):
        xb = x_ref[...].astype(BF16)
        conv_ref[...] = _dot(xb, w_ref[:, COL_CONV:COL_QKV])
        qkv_ref[...] = _dot(xb, w_ref[:, COL_QKV:COL_SGU]).astype(BF16)
        sgu_ref[...] = _dot(xb, w_ref[:, COL_SGU:COL_F])
        f_ref[...] = _dot(xb, w_ref[:, COL_F:D_IN_PAD])

    def tok(n):
        return pl.BlockSpec((tm, n), lambda i: (i, 0))

    return pl.pallas_call(
        body, name="mix_in_proj", grid=(t_tok // tm,),
        in_specs=[tok(D_MODEL), pl.BlockSpec((D_MODEL, D_IN_PAD), lambda i: (0, 0))],
        out_specs=[tok(768), tok(1536), tok(512), tok(128)],
        out_shape=[_hbm((t_tok, 768), F32), _hbm((t_tok, 1536), BF16),
                   _hbm((t_tok, 512), F32), _hbm((t_tok, 128), F32)],
        compiler_params=_params(48, ("arbitrary",)),
    )(x, w_in)


def _shift_down(a, k):
    row = lax.broadcasted_iota(jnp.int32, a.shape, 0)
    return jnp.where(row >= k, pltpu.roll(a, k, 0), 0.0)


def _shift_up(a, k):
    rows = a.shape[0]
    row = lax.broadcasted_iota(jnp.int32, a.shape, 0)
    return jnp.where(row < rows - k, pltpu.roll(a, rows - k, 0), 0.0)


def _tril(n):
    return lax.broadcasted_iota(jnp.int32, (n, n), 0) >= lax.broadcasted_iota(jnp.int32, (n, n), 1)


def _sgu_group_of_lane():
    return lax.broadcasted_iota(jnp.int32, (1, D_SGU), 1) // (D_SGU // N_SGU_GROUPS)


def _log_sigmoid(x):
    return jnp.minimum(x, 0.0) - jnp.log1p(jnp.exp(-jnp.abs(x)))


def _mix_mid_fwd(conv, sgu, f, conv_w, b_f, sgu_g, sgu_b, w_s, b_mat, n_seq):
    t_tok = conv.shape[0]
    seq = t_tok // n_seq
    n_chunk = seq // SGU_CHUNK
    per_blk = ATT_BLK // SGU_CHUNK

    def body(conv_ref, sgu_ref, f_ref, cw_ref, bf_ref, lg_ref, lb_ref, ws_ref, bm_ref, cat_ref, cum_ref):
        z = conv_ref[:, 256:512] * conv_ref[:, 512:768]
        y = cw_ref[0:1, :] * _shift_down(z, 2) + cw_ref[1:2, :] * _shift_down(z, 1) + cw_ref[2:3, :] * z
        cat_ref[:, 0:D_CONV] = (conv_ref[:, 0:256] * y).astype(BF16)
        cat_ref[:, D_CONV:D_CONV + D_FOX] = jnp.zeros((seq, D_FOX), BF16)

        tril = _tril(SGU_CHUNK)
        grp = _sgu_group_of_lane()
        wc = [jnp.where(tril, ws_ref[g], 0.0).astype(BF16) for g in range(N_SGU_GROUPS)]
        tri_f = tril.astype(F32)
        carry = jnp.zeros((1, 128), F32)
        for n in range(n_chunk):
            rows = pl.ds(n * SGU_CHUNK, SGU_CHUNK)
            u = _gelu(sgu_ref[rows, 0:256])
            vhat, _ = _ln_stats(_gelu(sgu_ref[rows, 256:512]))
            vn = (vhat * lg_ref[...] + lb_ref[...]).astype(BF16)
            mixed = bm_ref[...]
            for g in range(N_SGU_GROUPS):
                mixed = mixed + jnp.where(grp == g, _dot(wc[g], vn), 0.0)
            cat_ref[rows, D_CONV + D_FOX:D_MODEL] = (u * mixed).astype(BF16)

            log_f = _log_sigmoid(f_ref[rows, :] + bf_ref[...])
            cs = _dot(tri_f, log_f, HIGHEST) + carry
            carry = cs[SGU_CHUNK - 1:SGU_CHUNK, :]
            cs_t = cs.T
            lanes = pl.ds((n % per_blk) * SGU_CHUNK, SGU_CHUNK)
            for h in range(N_HEADS):
                cum_ref[h, n // per_blk, :, lanes] = cs_t[h:h + 1, :]

    def seq_blk(n):
        return pl.BlockSpec((seq, n), lambda b: (b, 0))

    def full(shape):
        return pl.BlockSpec(shape, lambda b: (0,) * len(shape))

    return pl.pallas_call(
        body, name="mix_mid_fwd", grid=(n_seq,),
        in_specs=[seq_blk(768), seq_blk(512), seq_blk(128), full((8, 256)), full((1, 128)), full((1, 256)),
                  full((1, 256)), full((4, 128, 128)), full((128, 256))],
        out_specs=[seq_blk(D_MODEL), pl.BlockSpec((N_HEADS, seq // ATT_BLK, 1, ATT_BLK), lambda b: (b, 0, 0, 0))],
        out_shape=[_hbm((t_tok, D_MODEL), BF16), _hbm((n_seq * N_HEADS, seq // ATT_BLK, 1, ATT_BLK), F32)],
        compiler_params=_params(48, ("arbitrary",)),
    )(conv, sgu, f, conv_w, b_f, sgu_g, sgu_b, w_s, b_mat)


def _head_masks():
    lane = lax.broadcasted_iota(jnp.int32, (1, 128), 1)
    return lane < 64, lane


def _fox_fwd(qkv, cum_t, cat, n_seq):
    t_tok = qkv.shape[0]
    seq = t_tok // n_seq
    nq = seq // ATT_BLK
    blk = ATT_BLK

    def body(q_ref, k_ref, v_ref, c_ref, _cat, o_ref, lse_ref):
        qi = pl.program_id(2)
        first, _ = _head_masks()
        one = jnp.ones((1, 128), BF16)
        qh = []
        for hp in range(ATT_PAIRS):
            qs = q_ref[:, 128 * hp:128 * hp + 128] * ATT_SCALE
            zero = jnp.zeros_like(qs)
            qh += [jnp.where(first, qs, zero), jnp.where(first, zero, qs)]

        def step(kb, carry, masked):
            ms, accs = carry
            rows = pl.ds(pl.multiple_of(kb * blk, blk), blk)
            new_m, new_acc = [], []
            for hp in range(ATT_PAIRS):
                k = k_ref[rows, 128 * hp:128 * hp + 128]
                v = v_ref[rows, 128 * hp:128 * hp + 128]
                for h in range(2):
                    i = 2 * hp + h
                    s = _dot_nt(qh[i], k) - c_ref[i, kb]
                    if masked:
                        s = jnp.where(causal, s, NEG)
                    m_new = jnp.maximum(ms[i], jnp.max(s, axis=1, keepdims=True))
                    p = jnp.exp(s - m_new)
                    vh = jnp.where(first, v, one) if h == 0 else jnp.where(first, one, v)
                    new_acc.append(accs[i] * jnp.exp(ms[i] - m_new) + _dot(p.astype(BF16), vh))
                    new_m.append(m_new)
            return tuple(new_m), tuple(new_acc)

        causal = _tril(blk)
        n_heads = 2 * ATT_PAIRS
        col = jnp.full((blk, 1), NEG, F32)
        zacc = jnp.zeros((blk, 128), F32)
        carry = lax.fori_loop(0, qi, lambda kb, cr: step(kb, cr, False), ((col,) * n_heads, (zacc,) * n_heads))
        ms, accs = step(qi, carry, True)
        for hp in range(ATT_PAIRS):
            acc0, acc1 = accs[2 * hp], accs[2 * hp + 1]
            l0 = pltpu.roll(acc0, 64, 1)
            l1 = pltpu.roll(acc1, 64, 1)
            o_ref[:, 128 * hp:128 * hp + 128] = jnp.where(first, acc0 / l0, acc1 / l1).astype(BF16)
            lse_ref[:, 128 * hp:128 * hp + 128] = jnp.where(first, ms[2 * hp] + jnp.log(l0), ms[2 * hp + 1] + jnp.log(l1))

    wide = 128 * ATT_PAIRS
    n_grp = D_FOX // wide
    first_col = D_CONV // wide
    return pl.pallas_call(
        body, name="fox_fwd", grid=(n_seq, n_grp, nq),
        in_specs=[pl.BlockSpec((blk, wide), lambda b, g, qi: (b * nq + qi, g)),
                  pl.BlockSpec((seq, wide), lambda b, g, qi: (b, n_grp + g)),
                  pl.BlockSpec((seq, wide), lambda b, g, qi: (b, 2 * n_grp + g)),
                  pl.BlockSpec((2 * ATT_PAIRS, nq, 1, blk), lambda b, g, qi: (b * n_grp + g, 0, 0, 0)), _anyspec()],
        out_specs=[pl.BlockSpec((blk, wide), lambda b, g, qi: (b * nq + qi, first_col + g)),
                   pl.BlockSpec((blk, wide), lambda b, g, qi: (b * nq + qi, g))],
        out_shape=[_hbm(cat.shape, BF16), _hbm((t_tok, D_FOX), F32)],
        input_output_aliases={4: 0},
        compiler_params=_params(48, ("arbitrary", "arbitrary", "arbitrary")),
    )(qkv, qkv, qkv, cum_t, cat)


def _fox_bwd(qkv, cum_t, cat, lse, d_o, n_seq):
    t_tok = qkv.shape[0]
    seq = t_tok // n_seq
    nk = seq // ATT_BLK
    blk = ATT_BLK

    def body(q_ref, k_ref, v_ref, c_ref, o_ref, lse_ref, do_ref, dq_ref, dk_ref, dv_ref, drow_ref, dcol_ref):
        kb = pl.program_id(2)
        first, lane = _head_masks()
        second = jnp.logical_not(first)
        one = jnp.ones((1, 128), BF16)
        causal = _tril(blk)

        @pl.when(kb == 0)
        def _():
            dq_ref[...] = jnp.zeros_like(dq_ref)
            drow_ref[...] = jnp.zeros_like(drow_ref)

        def step(qi, carry, masked):
            rows = pl.ds(pl.multiple_of(qi * blk, blk), blk)
            dks, dvs = carry
            new_dk, new_dv = [], []
            for hp in range(ATT_PAIRS):
                cols = slice(128 * hp, 128 * hp + 128)
                k = k_ref[:, cols]
                v = v_ref[:, cols]
                ks = k * ATT_SCALE
                zero = jnp.zeros_like(k)
                qs = q_ref[rows, cols] * ATT_SCALE
                d_o = do_ref[rows, cols]
                dd = d_o.astype(F32) * o_ref[rows, cols].astype(F32)
                lse_t = lse_ref[rows, cols]
                dq = []
                for h, mine in enumerate((first, second)):
                    i = 2 * hp + h
                    qh = jnp.where(mine, qs, zero)
                    doh = jnp.where(mine, d_o, zero)
                    delta = jnp.sum(jnp.where(mine, dd, 0.0), axis=1, keepdims=True)
                    lse_h = jnp.sum(jnp.where(lane == 64 * h, lse_t, 0.0), axis=1, keepdims=True)
                    s = _dot_nt(qh, k) - c_ref[i]
                    if masked:
                        s = jnp.where(causal, s, NEG)
                    p = jnp.exp(s - lse_h)
                    ds = (p * (_dot_nt(doh, v) - delta)).astype(BF16)
                    new_dk.append(dks[i] + _dot_tn(ds, jnp.where(mine, qs, one)))
                    new_dv.append(dvs[i] + _dot_tn(p.astype(BF16), doh))
                    dq.append(_dot(ds, jnp.where(mine, ks, one)))
                dq_ref[rows, cols] += jnp.where(first, dq[0], dq[1])
                drow_ref[rows, cols] += jnp.where(first, dq[1], dq[0])
            return tuple(new_dk), tuple(new_dv)

        zt = (jnp.zeros((blk, 128), F32),) * (2 * ATT_PAIRS)
        carry = step(kb, (zt, zt), True)
        dks, dvs = lax.fori_loop(kb + 1, nk, lambda qi, cr: step(qi, cr, False), carry)
        for hp in range(ATT_PAIRS):
            cols = slice(128 * hp, 128 * hp + 128)
            dk_ref[:, cols] = jnp.where(first, dks[2 * hp], dks[2 * hp + 1]).astype(BF16)
            dcol_ref[:, cols] = jnp.where(first, dks[2 * hp + 1], dks[2 * hp])
            dv_ref[:, cols] = (dvs[2 * hp] + dvs[2 * hp + 1]).astype(BF16)

    wide = 128 * ATT_PAIRS
    n_grp = D_FOX // wide

    def seq_spec(col0):
        return pl.BlockSpec((seq, wide), lambda b, g, kb: (b, col0 + g))

    def key_spec(col0):
        return pl.BlockSpec((blk, wide), lambda b, g, kb: (b * nk + kb, col0 + g))

    return pl.pallas_call(
        body, name="fox_bwd", grid=(n_seq, n_grp, nk),
        in_specs=[seq_spec(0), key_spec(n_grp), key_spec(2 * n_grp),
                  pl.BlockSpec((2 * ATT_PAIRS, None, 1, blk), lambda b, g, kb: (b * n_grp + g, kb, 0, 0)),
                  seq_spec(D_CONV // wide), seq_spec(0), seq_spec(0)],
        out_specs=[seq_spec(0), key_spec(0), key_spec(0), seq_spec(0), key_spec(0)],
        out_shape=[_hbm((t_tok, D_FOX), F32), _hbm((t_tok, D_FOX), BF16),
                   _hbm((t_tok, D_FOX), BF16), _hbm((t_tok, D_FOX), F32),
                   _hbm((t_tok, D_FOX), F32)],
        compiler_params=_params(56, ("arbitrary", "arbitrary", "arbitrary")),
    )(qkv, qkv, qkv, cum_t, cat, lse, d_o)


def _mix_out_fwd(cat, x, w_out, ln_g, ln_b, tm=512):
    t_tok = x.shape[0]

    def body(cat_ref, x_ref, w_ref, g_ref, b_ref, xn_ref, xnb_ref, z_ref):
        z = ALPHA * x_ref[...] + _dot(cat_ref[...], w_ref[...])
        xhat, _ = _ln_stats(z)
        xn = xhat * g_ref[...] + b_ref[...]
        z_ref[...] = z
        xn_ref[...] = xn
        xnb_ref[...] = xn.astype(BF16)

    def tok(n):
        return pl.BlockSpec((tm, n), lambda i: (i, 0))

    vec = pl.BlockSpec((1, D_MODEL), lambda i: (0, 0))
    return pl.pallas_call(
        body, name="mix_out_fwd", grid=(t_tok // tm,),
        in_specs=[tok(D_MODEL), tok(D_MODEL), pl.BlockSpec((D_MODEL, D_MODEL), lambda i: (0, 0)), vec, vec],
        out_specs=[tok(D_MODEL)] * 3,
        out_shape=[_hbm((t_tok, D_MODEL), F32), _hbm((t_tok, D_MODEL), BF16),
                   _hbm((t_tok, D_MODEL), F32)],
        compiler_params=_params(40, ("arbitrary",)),
    )(cat, x, w_out, ln_g, ln_b)


def _mix_out_bwd(dxn, z, w_out, ln_g, tm=512):
    t_tok = dxn.shape[0]

    def body(dxn_ref, z_ref, w_ref, g_ref, dz_ref, dzb_ref, dya_ref, dyb_ref, dyc_ref, dg_ref, db_ref):
        i = pl.program_id(0)
        dxn_t = dxn_ref[...]
        xhat, rstd = _ln_stats(z_ref[...])
        pg = jnp.sum(dxn_t * xhat, axis=0, keepdims=True)
        pb = jnp.sum(dxn_t, axis=0, keepdims=True)

        @pl.when(i == 0)
        def _():
            dg_ref[...] = pg
            db_ref[...] = pb

        @pl.when(i > 0)
        def _():
            dg_ref[...] += pg
            db_ref[...] += pb

        dz = _ln_bwd(dxn_t, xhat, rstd, g_ref[...])
        dzb = dz.astype(BF16)
        dz_ref[...] = dz
        dzb_ref[...] = dzb
        dya_ref[...] = _dot_nt(dzb, w_ref[0:256, :])
        dyb_ref[...] = _dot_nt(dzb, w_ref[256:768, :]).astype(BF16)
        dyc_ref[...] = _dot_nt(dzb, w_ref[768:1024, :])

    def tok(n):
        return pl.BlockSpec((tm, n), lambda i: (i, 0))

    vec = pl.BlockSpec((1, D_MODEL), lambda i: (0, 0))
    return pl.pallas_call(
        body, name="mix_out_bwd", grid=(t_tok // tm,),
        in_specs=[tok(D_MODEL), tok(D_MODEL), pl.BlockSpec((D_MODEL, D_MODEL), lambda i: (0, 0)), vec],
        out_specs=[tok(D_MODEL), tok(D_MODEL), tok(256), tok(512), tok(256), vec, vec],
        out_shape=[_hbm((t_tok, D_MODEL), F32), _hbm((t_tok, D_MODEL), BF16),
                   _hbm((t_tok, 256), F32), _hbm((t_tok, 512), BF16),
                   _hbm((t_tok, 256), F32),
                   _hbm((1, D_MODEL), F32), _hbm((1, D_MODEL), F32)],
        compiler_params=_params(40, ("arbitrary",)),
    )(dxn, z, w_out, ln_g)


def _conv_bwd(conv, dya, conv_w, n_seq):
    t_tok = conv.shape[0]
    seq = t_tok // n_seq

    def body(conv_ref, dya_ref, cw_ref, dconv_ref, dcw_ref):
        @pl.when(pl.program_id(0) == 0)
        def _():
            dcw_ref[...] = jnp.zeros_like(dcw_ref)

        z = conv_ref[:, 256:512] * conv_ref[:, 512:768]
        z1 = _shift_down(z, 1)
        z2 = _shift_down(z, 2)
        y = cw_ref[0:1, :] * z2 + cw_ref[1:2, :] * z1 + cw_ref[2:3, :] * z
        dya_t = dya_ref[...]
        dconv_ref[:, 0:256] = (dya_t * y).astype(BF16)
        dy = dya_t * conv_ref[:, 0:256]
        dcw_ref[0:1, :] += jnp.sum(dy * z2, axis=0, keepdims=True)
        dcw_ref[1:2, :] += jnp.sum(dy * z1, axis=0, keepdims=True)
        dcw_ref[2:3, :] += jnp.sum(dy * z, axis=0, keepdims=True)
        dz = cw_ref[2:3, :] * dy + cw_ref[1:2, :] * _shift_up(dy, 1) + cw_ref[0:1, :] * _shift_up(dy, 2)
        dconv_ref[:, 256:512] = (dz * conv_ref[:, 512:768]).astype(BF16)
        dconv_ref[:, 512:768] = (dz * conv_ref[:, 256:512]).astype(BF16)

    def seq_blk(n):
        return pl.BlockSpec((seq, n), lambda b: (b, 0))

    par = pl.BlockSpec((8, 256), lambda b: (0, 0))
    return pl.pallas_call(
        body, name="conv_bwd", grid=(n_seq,),
        in_specs=[seq_blk(768), seq_blk(256), par], out_specs=[seq_blk(768), par],
        out_shape=[_hbm((t_tok, 768), BF16), _hbm((8, 256), F32)],
        compiler_params=_params(56, ("arbitrary",)),
    )(conv, dya, conv_w)


def _sgu_gate_bwd(sgu, f, dyc, drow, dcol, b_f, sgu_g, sgu_b, w_s, b_mat, n_seq):
    t_tok = sgu.shape[0]
    seq = t_tok // n_seq
    n_chunk = seq // SGU_CHUNK

    def body(sgu_ref, f_ref, dyc_ref, drow_ref, dcol_ref, bf_ref, lg_ref, lb_ref, ws_ref, bm_ref,
             dsgu_ref, df_ref, dbf_ref, dlg_ref, dlb_ref, dws_ref, dbs_ref, dbm_acc):
        b = pl.program_id(0)

        @pl.when(b == 0)
        def _():
            for r in (dbf_ref, dlg_ref, dlb_ref, dws_ref, dbm_acc):
                r[...] = jnp.zeros_like(r)

        tril = _tril(SGU_CHUNK)
        grp = _sgu_group_of_lane()
        wc = [jnp.where(tril, ws_ref[g], 0.0).astype(BF16) for g in range(N_SGU_GROUPS)]
        for n in range(n_chunk):
            rows = pl.ds(n * SGU_CHUNK, SGU_CHUNK)
            su = sgu_ref[rows, 0:256]
            sv = sgu_ref[rows, 256:512]
            u, du = _gelu_with_grad(su)
            gv, dgv = _gelu_with_grad(sv)
            vhat, rstd = _ln_stats(gv)
            vn = (vhat * lg_ref[...] + lb_ref[...]).astype(BF16)
            mixed = bm_ref[...]
            for g in range(N_SGU_GROUPS):
                mixed = mixed + jnp.where(grp == g, _dot(wc[g], vn), 0.0)
            dyc_t = dyc_ref[rows, :]
            dsgu_ref[rows, 0:256] = (dyc_t * mixed * du).astype(BF16)
            dmixed = dyc_t * u
            dbm_acc[...] += dmixed
            dvn = jnp.zeros((SGU_CHUNK, D_SGU), F32)
            for g in range(N_SGU_GROUPS):
                dm_g = jnp.where(grp == g, dmixed, 0.0).astype(BF16)
                dws_ref[g] += _dot_nt(dm_g, vn)
                dvn = dvn + _dot_tn(wc[g], dm_g)
            dlg_ref[...] += jnp.sum(dvn * vhat, axis=0, keepdims=True)
            dlb_ref[...] += jnp.sum(dvn, axis=0, keepdims=True)
            dsgu_ref[rows, 256:512] = (_ln_bwd(dvn, vhat, rstd, lg_ref[...]) * dgv).astype(BF16)

        later = (lax.broadcasted_iota(jnp.int32, (128, 128), 0) <= lax.broadcasted_iota(jnp.int32, (128, 128), 1)).astype(F32)
        head = lax.broadcasted_iota(jnp.int32, (D_FOX, 128), 1)
        pick = (lax.broadcasted_iota(jnp.int32, (D_FOX, 128), 0) == 128 * (head // 2) + 64 * (1 - head % 2)).astype(F32)
        carry = jnp.zeros((1, 128), F32)
        for n in reversed(range(n_chunk)):
            rows = pl.ds(n * SGU_CHUNK, SGU_CHUNK)
            dcum_n = _dot(drow_ref[rows, :] - dcol_ref[rows, :], pick, HIGHEST)
            dlf = _dot(later, dcum_n, HIGHEST) + carry
            carry = carry + jnp.sum(dcum_n, axis=0, keepdims=True)
            df = dlf * jax.nn.sigmoid(-(f_ref[rows, :] + bf_ref[...]))
            df_ref[rows, :] = df.astype(BF16)
            dbf_ref[...] += jnp.sum(df, axis=0, keepdims=True)

        @pl.when(b == n_seq - 1)
        def _():
            for g in range(N_SGU_GROUPS):
                dws_ref[g] = jnp.where(tril, dws_ref[g], 0.0)
            sel = (lax.broadcasted_iota(jnp.int32, (D_SGU, 128), 0) // (D_SGU // N_SGU_GROUPS)
                   == lax.broadcasted_iota(jnp.int32, (D_SGU, 128), 1)).astype(F32)
            dbs_ref[...] = _dot(dbm_acc[...], sel, HIGHEST)

    def seq_blk(n):
        return pl.BlockSpec((seq, n), lambda b: (b, 0))

    def full(shape):
        return pl.BlockSpec(shape, lambda b: (0,) * len(shape))

    param_shapes = [(1, 128), (1, 256), (1, 256), (4, 128, 128), (128, 128)]
    return pl.pallas_call(
        body, name="sgu_gate_bwd", grid=(n_seq,),
        in_specs=[seq_blk(512), seq_blk(128), seq_blk(256), seq_blk(D_FOX), seq_blk(D_FOX),
                  full((1, 128)), full((1, 256)), full((1, 256)), full((4, 128, 128)), full((128, 256))],
        out_specs=[seq_blk(512), seq_blk(128)] + [full(s) for s in param_shapes],
        out_shape=[_hbm((t_tok, 512), BF16), _hbm((t_tok, 128), BF16)]
        + [_hbm(s, F32) for s in param_shapes],
        scratch_shapes=[pltpu.VMEM((128, 256), F32)],
        compiler_params=_params(48, ("arbitrary",)),
    )(sgu, f, dyc, drow, dcol, b_f, sgu_g, sgu_b, w_s, b_mat)


def _mix_in_bwd(dconv, dq, dk, dv, dsgu, df, dz, w_in, tm=512):
    t_tok = dz.shape[0]

    def body(dconv_ref, dq_ref, dk_ref, dv_ref, dsgu_ref, df_ref, dz_ref, w_ref, dx_ref, dp_ref):
        dqb = dq_ref[...].astype(BF16)
        pieces = [(COL_CONV, dconv_ref[...]), (COL_QKV, dqb), (COL_QKV + 512, dk_ref[...]), (COL_QKV + 1024, dv_ref[...]),
                  (COL_SGU, dsgu_ref[...]), (COL_F, df_ref[...])]
        dx = ALPHA * dz_ref[...]
        for col, val in pieces:
            width = val.shape[1]
            dp_ref[:, col:col + width] = val
            dx = dx + _dot_nt(val, w_ref[:, col:col + width])
        dx_ref[...] = dx

    def tok(n):
        return pl.BlockSpec((tm, n), lambda i: (i, 0))

    return pl.pallas_call(
        body, name="mix_in_bwd", grid=(t_tok // tm,),
        in_specs=[tok(768), tok(512), tok(512), tok(512), tok(512), tok(128), tok(D_MODEL),
                  pl.BlockSpec((D_MODEL, D_IN_PAD), lambda i: (0, 0))],
        out_specs=[tok(D_MODEL), tok(D_IN_PAD)],
        out_shape=[_hbm((t_tok, D_MODEL), F32), _hbm((t_tok, D_IN_PAD), BF16)],
        compiler_params=_params(48, ("arbitrary",)),
    )(dconv, dq, dk, dv, dsgu, df, dz, w_in)


def _pad_rows(a, rows):
    return jnp.pad(a, ((0, rows - a.shape[0]), (0, 0)))


F_BLOCK = F_ORIG // D_IN_SHARD
F_AT = F_ORIG - F_BLOCK * D_IN_SHARD
assert (F_ORIG + N_HEADS) // D_IN_SHARD == F_BLOCK


def _w_in_from_blocks(g):
    fb = g[F_BLOCK]
    zeros = jnp.zeros((D_MODEL, D_IN_PAD - COL_F - N_HEADS), g.dtype)
    return jnp.concatenate([g[d] for d in range(F_BLOCK)] + [fb[:, :F_AT], fb[:, F_AT + N_HEADS:]]
                           + [g[d] for d in range(F_BLOCK + 1, N_DEV)] + [fb[:, F_AT:F_AT + N_HEADS], zeros], axis=1)


def _w_in_to_blocks(dw):
    def cols(lo, hi):
        shift = 0 if hi <= F_ORIG else N_HEADS
        return dw[:, lo - shift:hi - shift]

    blocks = []
    for d in range(N_DEV):
        lo, hi = d * D_IN_SHARD, (d + 1) * D_IN_SHARD
        if d == F_BLOCK:
            blocks.append(jnp.concatenate([cols(lo, F_ORIG), dw[:, COL_F:COL_F + N_HEADS], cols(F_ORIG + N_HEADS, hi)], axis=1))
        else:
            blocks.append(cols(lo, hi))
    return jnp.stack(blocks)


LN1_ROWS = 2 * 8
REST_ROWS = 4 * 8 + 2 * 8 + 512 + 8 + 8 + 8


def _pack_rest(p):
    rows = [p[name].reshape(8, 128) for name in ("ln2_g", "ln2_b", "ln3_g", "ln3_b")]
    rows += [_pad_rows(p[name].reshape(2, 128), 8) for name in ("sgu_ln_g", "sgu_ln_b")]
    rows += [p["sgu_w_s"].reshape(512, 128), _pad_rows(p["sgu_b_s"], 8),
             _pad_rows(jnp.pad(p["fox_b_f"], (0, 128 - N_HEADS)).reshape(1, 128), 8), _pad_rows(p["conv_w"].reshape(6, 128), 8)]
    return jnp.concatenate(rows, axis=0)


def _pack_layer(p):
    return jnp.concatenate([p["ln1_g"].reshape(8, 128), p["ln1_b"].reshape(8, 128), _pack_rest(p)], axis=0)


def _unpack_layer(a):
    r = 0

    def take(n, valid):
        nonlocal r
        piece = a[r:r + valid]
        r += n
        return piece

    d = {}
    for name in ("ln1_g", "ln1_b", "ln2_g", "ln2_b", "ln3_g", "ln3_b"):
        d[name] = take(8, 8).reshape(D_MODEL)
    for name in ("sgu_ln_g", "sgu_ln_b"):
        d[name] = take(8, 2).reshape(D_SGU)
    d["sgu_w_s"] = take(512, 512).reshape(N_SGU_GROUPS, SGU_CHUNK, SGU_CHUNK)
    d["sgu_b_s"] = take(8, 4).reshape(N_SGU_GROUPS, SGU_CHUNK)
    d["fox_b_f"] = take(8, 1).reshape(128)[:N_HEADS]
    d["conv_w"] = take(8, 6).reshape(3, D_CONV)
    return d


SMALL_NAMES = ("ln1_g", "ln1_b", "fox_b_f", "sgu_ln_g", "sgu_ln_b", "sgu_w_s", "sgu_b_s", "ln2_g", "ln2_b", "ln3_g", "ln3_b")
BIG_NAMES = ("ffn1_w_up", "ffn1_w_down", "mix_w_in", "mix_w_out", "ffn2_w_up", "ffn2_w_down")
UP_NAMES = ("ffn1_w_up", "ffn2_w_up")
WEIGHT_ORDER = ("ln1_g", "ln1_b", "ffn1_w_up", "ffn1_w_down", "mix_w_in", "fox_b_f", "conv_w", "sgu_ln_g", "sgu_ln_b",
                "sgu_w_s", "sgu_b_s", "mix_w_out", "ln2_g", "ln2_b", "ffn2_w_up", "ffn2_w_down", "ln3_g", "ln3_b")


class _Overlap:
    def __init__(self, w, after, me, where):
        self.me, self.where = me, where
        self.last = after
        groups = [[("ffn1_w_up", 0), ("ffn1_w_down", 0)],
                  [("mix_w_in", 0), ("mix_w_out", 0), ("ffn2_w_up", 0), ("ffn2_w_down", 0)]]
        groups += [[(name, l) for name in BIG_NAMES] for l in range(1, DEPTH)]
        self.gathers = []
        for gi, group in enumerate(groups):
            shards = [w[name][l].astype(BF16) for name, l in group]
            lands = [lax.dynamic_update_slice(lax.empty((N_DEV,) + s.shape, BF16), s[None], (me, 0, 0)) for s in shards]
            started = self._start(f"allgather_start_{gi}", _gather_plan(len(group)), 3 * len(group), shards + lands)
            self.gathers.append(dict(group=group, chips=started))
            if gi == 0:
                width = D_CONV // N_DEV
                rows = jnp.pad(_pad_rows(w["conv_w"].reshape(DEPTH * 3, width), 8), ((0, 0), (0, 128 - width)))
                land = lax.dynamic_update_slice(lax.empty((N_DEV,) + rows.shape, F32), rows[None], (me, 0, 0))
                self.conv_started = self._start("conv_w_start", _peers_plan(), N_DEV - 1, [rows, land])
                self.conv_full = None
        self.all_started = self.last
        self.scatters = {}
        self.order = []
        self.small = []

    def conv_w(self, after):
        if self.conv_full is None:
            width = D_CONV // N_DEV
            gathered = _exchange_wait("conv_w_wait", _peers_plan(), N_DEV - 1, self.conv_started, after)[1]
            self.conv_full = jnp.transpose(gathered[:, :DEPTH * 3, :width], (1, 0, 2)).reshape(DEPTH, 3, D_CONV)
        return self.conv_full

    def _start(self, name, plan, n_copies, arrays):
        started = _exchange_start(name, plan, n_copies, arrays, self.last)
        self.last = started[3]
        return started

    def _group_of(self, layer, part):
        return layer + 1 if layer > 0 else (0 if part == "ffn1" else 1)

    def pass_on(self, layer, part, after):
        gi = self._group_of(layer, part)
        st = self.gathers[gi]
        if "sibling" not in st:
            m = len(st["group"])
            arrays = _exchange_wait(f"allgather_wait_{gi}", _gather_plan(m), 3 * m, st["chips"], after)
            first = [i for i, (name, _) in enumerate(st["group"]) if name == "mix_w_in"] if gi == 1 else []
            st["passes"] = [idx for idx in (first, [i for i in range(m) if i not in first]) if idx]
            st["sibling"] = [self._start(f"allgather_pass_start_{gi}_{k}", _pass_on_plan(len(idx)), 4 * len(idx),
                                         [arrays[i] for i in idx] + [arrays[m + i] for i in idx])
                             for k, idx in enumerate(st["passes"])]
            st["full"] = {}
        return st["sibling"][-1][3]

    def weights(self, layer, part, after):
        names = {"ffn1": ("ffn1_w_up", "ffn1_w_down"), "in": ("mix_w_in",),
                 "rest": ("mix_w_out", "ffn2_w_up", "ffn2_w_down")}[part]
        gi = self._group_of(layer, part)
        st = self.gathers[gi]
        after = self.all_started if after is None else after
        self.pass_on(layer, part, after)
        g = st["full"]
        if (names[0], layer) not in g:
            for k, idx in enumerate(st["passes"]):
                if st["group"].index((names[0], layer)) in idx:
                    arrays = _exchange_wait(f"allgather_pass_wait_{gi}_{k}", _pass_on_plan(len(idx)), 4 * len(idx),
                                            st["sibling"][k], after)
                    g.update(zip([st["group"][i] for i in idx], arrays[len(idx):]))

        def ffn(n):
            return g[(f"ffn{n}_w_up", layer)].reshape(2, D_FF, D_MODEL), g[(f"ffn{n}_w_down", layer)].reshape(D_FF, D_MODEL)

        if part == "ffn1":
            return ffn(1)
        if part == "in":
            return _w_in_from_blocks(g[("mix_w_in", layer)])
        return (g[("mix_w_out", layer)].reshape(D_MODEL, D_MODEL), *ffn(2))

    def push(self, key, items):
        n = len(items)
        grads = [g for _, _, g in items]
        lands = [lax.empty((4,) + g.shape[1:], F32) for g in grads]
        started = self._start(f"rs_sibling_start_{key[0]}{key[1]}", _sibling_plan(n), 4 * n, grads + lands)
        self.scatters[key] = dict(items=items, sibling=started)
        self.order.append(key)
        return started[3]

    def advance(self, key, after):
        st = self.scatters[key]
        n = len(st["items"])
        arrays = _exchange_wait(f"rs_sibling_wait_{key[0]}{key[1]}", _sibling_plan(n), 4 * n, st["sibling"], after)
        partials = [_chip_partial(g, r, self.where) for g, r in zip(arrays[:n], arrays[n:])]
        p16 = [p for _, p in partials]
        lands = [lax.empty((3,) + p.shape[1:], BF16) for p in p16]
        started = self._start(f"rs_chip_start_{key[0]}{key[1]}", _chip_plan(n), 3 * n, p16 + lands)
        st.update(own32=[p for p, _ in partials], chip=started)
        return started[3]

    def push_small(self, rows):
        k = len(self.small)
        land = lax.dynamic_update_slice(lax.empty((N_DEV,) + rows.shape, F32), rows[None], (self.me, 0, 0))
        started = self._start(f"small_start_{k}", _peers_plan(), N_DEV - 1, [rows, land])
        self.small.append(started)
        return started[3]

    def finish(self, w, m, v):
        res = {}
        after = self.scatters[self.order[-1]]["chip"][3]
        for key in self.order:
            st = self.scatters[key]
            n = len(st["items"])
            arrays = _exchange_wait(f"rs_chip_wait_{key[0]}{key[1]}", _chip_plan(n), 3 * n, st["chip"], after)
            for (name, l, _), own32, r16 in zip(st["items"], st["own32"], arrays[n:]):
                res[name] = _adamw_shard(own32, r16, w[name], m[name], v[name], l, res.get(name))
                after = res[name][0]
        pieces = [_exchange_wait(f"small_wait_{k}", _peers_plan(), N_DEV - 1, started, after)[1]
                  for k, started in enumerate(self.small)]
        return res, pieces


def _dw_up(dgu, x, after):
    return _matmul_tn(dgu, x[None], after, tk=2048, bm=DW_ROWS).reshape(N_DEV, FFN_BLK, D_MODEL)


def _dw_down(a, dy, after):
    return _matmul_tn(a[None], dy[None], after, tk=2048, bm=DW_ROWS).reshape(N_DEV, FFN_BLK // 2, D_MODEL)


def _local_step(x, target, comm, small, n_seq):
    def vec(a):
        return a.reshape(1, -1)

    saved = []
    h = x
    for l in range(DEPTH):
        s = {}
        s["up1"], s["down1"] = comm.weights(l, "ffn1", None if l == 0 else h)
        h1, h1b, s["z1"], s["gu1"], s["x0b"] = _ffn_fwd(h, s["up1"], s["down1"], vec(small["ln1_g"][l]), vec(small["ln1_b"][l]), h)
        s["w_in"] = comm.weights(l, "in", s["z1"])
        s["x1b"] = h1b
        conv, qkv, sgu, f = _in_proj(h1, s["w_in"])
        s["w_out"], s["up2"], s["down2"] = comm.weights(l, "rest", f)
        cw = _pad_rows(comm.conv_w(h1)[l], 8)
        bf = jnp.pad(small["fox_b_f"][l], (0, 128 - N_HEADS)).reshape(1, 128)
        b_mat = jnp.repeat(small["sgu_b_s"][l].T, D_SGU // N_SGU_GROUPS, axis=1)
        mid_params = (cw, bf, vec(small["sgu_ln_g"][l]), vec(small["sgu_ln_b"][l]), small["sgu_w_s"][l], b_mat)
        cat, cum_t = _mix_mid_fwd(conv, sgu, f, *mid_params, n_seq)
        cat, lse = _fox_fwd(qkv, cum_t, cat, n_seq)
        h2, h2b, s["z2"] = _mix_out_fwd(cat, h1, s["w_out"], vec(small["ln2_g"][l]), vec(small["ln2_b"][l]))
        s.update(conv=conv, qkv=qkv, sgu=sgu, f=f, mid_params=mid_params, cat=cat, cum_t=cum_t, lse=lse, x2b=h2b)
        token = comm.pass_on(l + 1, "ffn1", s["z2"]) if l + 1 < DEPTH else h2
        ln3 = (vec(small["ln3_g"][l]), vec(small["ln3_b"][l]))
        if l + 1 < DEPTH:
            h, _, s["z3"], s["gu2"], _ = _ffn_fwd(h2, s["up2"], s["down2"], *ln3, token)
        else:
            dh, loss, s["z3"], s["gu2"] = _ffn_fwd(h2, s["up2"], s["down2"], *ln3, token, target)
        saved.append(s)

    late_rows = None
    token = loss
    pending = None
    for l in reversed(range(DEPTH)):
        s = saved[l]
        sg = {}
        dh, dy, a, dgu, sg["ln3_g"], sg["ln3_b"] = _ffn_bwd(dh, s["z3"], s["gu2"], s["up2"], s["down2"], vec(small["ln3_g"][l]), token)
        if pending is not None:
            token = comm.advance(pending, dh)
        g_up2 = _dw_up(dgu, s["x2b"], token)
        g_down2 = _dw_down(a, dy, token)
        dz, dzb, dya, dyb, dyc, sg["ln2_g"], sg["ln2_b"] = _mix_out_bwd(dh, s["z2"], s["w_out"], vec(small["ln2_g"][l]))
        g_out = _matmul_tn(s["cat"][None], dzb[None], token).reshape(N_DEV, D_MODEL // N_DEV, D_MODEL)
        dq, dk, dv, drow, dcol = _fox_bwd(s["qkv"], s["cum_t"], s["cat"], s["lse"], dyb, n_seq)
        dconv, dcw = _conv_bwd(s["conv"], dya, s["mid_params"][0], n_seq)
        dsgu, df, dbf, dlg, dlb, dws, dbs = _sgu_gate_bwd(s["sgu"], s["f"], dyc, drow, dcol, *s["mid_params"][1:], n_seq)
        sg.update(conv_w=dcw[:3], fox_b_f=dbf[0, :N_HEADS], sgu_ln_g=dlg[0], sgu_ln_b=dlb[0], sgu_w_s=dws,
                  sgu_b_s=dbs[:, :N_SGU_GROUPS].T)
        dh, dp = _mix_in_bwd(dconv, dq, dk, dv, dsgu, df, dz, s["w_in"])
        g_in = _w_in_to_blocks(_matmul_tn(s["x1b"][None], dp[None], token, tk=2048, bm=D_MODEL // 2)[0])
        first = [("ffn2_w_up", l, g_up2), ("ffn2_w_down", l, g_down2), ("mix_w_out", l, g_out), ("mix_w_in", l, g_in)]
        for name in ("ln2_g", "ln2_b", "ln3_g", "ln3_b"):
            sg[name] = sg[name][0]
        if l == 0:
            comm.push((l, "a"), first)
            token = comm.push_small(_pack_rest(sg))
            pending, first = (l, "a"), []
        dh, dy, a, dgu, dg1, db1 = _ffn_bwd(dh, s["z1"], s["gu1"], s["up1"], s["down1"], vec(small["ln1_g"][l]), token)
        if l == 0:
            token = comm.advance(pending, dh)
        g_up1 = _dw_up(dgu, s["x0b"], token)
        ln1_rows = jnp.concatenate([dg1.reshape(8, 128), db1.reshape(8, 128)], axis=0)
        if l == 0:
            token = comm.push((l, "b"), [("ffn1_w_up", l, g_up1)])
            g_down1 = _dw_down(a, dy, token)
            token = comm.advance((l, "b"), g_down1)
            token = comm.push((l, "c"), [("ffn1_w_down", l, g_down1)])
            token = comm.advance((l, "c"), token)
            late_rows = ln1_rows
        else:
            g_down1 = _dw_down(a, dy, token)
            pending = (l, "b")
            comm.push(pending, first + [("ffn1_w_up", l, g_up1), ("ffn1_w_down", l, g_down1)])
            token = comm.push_small(jnp.concatenate([ln1_rows, _pack_rest(sg)], axis=0))
    return loss, dh, late_rows


def kernel(x, ln1_g, ln1_b, ffn1_w_up, ffn1_w_down, mix_w_in, fox_b_f, conv_w, sgu_ln_g, sgu_ln_b, sgu_w_s, sgu_b_s, mix_w_out, ln2_g, ln2_b, ffn2_w_up, ffn2_w_down, ln3_g, ln3_b, loss_target, m_ln1_g, m_ln1_b, m_ffn1_w_up, m_ffn1_w_down, m_mix_w_in, m_fox_b_f, m_conv_w, m_sgu_ln_g, m_sgu_ln_b, m_sgu_w_s, m_sgu_b_s, m_mix_w_out, m_ln2_g, m_ln2_b, m_ffn2_w_up, m_ffn2_w_down, m_ln3_g, m_ln3_b, v_ln1_g, v_ln1_b, v_ffn1_w_up, v_ffn1_w_down, v_mix_w_in, v_fox_b_f, v_conv_w, v_sgu_ln_g, v_sgu_ln_b, v_sgu_w_s, v_sgu_b_s, v_mix_w_out, v_ln2_g, v_ln2_b, v_ffn2_w_up, v_ffn2_w_down, v_ln3_g, v_ln3_b):
    w = dict(ln1_g=ln1_g, ln1_b=ln1_b, ffn1_w_up=ffn1_w_up, ffn1_w_down=ffn1_w_down, mix_w_in=mix_w_in, fox_b_f=fox_b_f,
             conv_w=conv_w, sgu_ln_g=sgu_ln_g, sgu_ln_b=sgu_ln_b, sgu_w_s=sgu_w_s, sgu_b_s=sgu_b_s, mix_w_out=mix_w_out,
             ln2_g=ln2_g, ln2_b=ln2_b, ffn2_w_up=ffn2_w_up, ffn2_w_down=ffn2_w_down, ln3_g=ln3_g, ln3_b=ln3_b)
    m = dict(ln1_g=m_ln1_g, ln1_b=m_ln1_b, ffn1_w_up=m_ffn1_w_up, ffn1_w_down=m_ffn1_w_down, mix_w_in=m_mix_w_in,
             fox_b_f=m_fox_b_f, conv_w=m_conv_w, sgu_ln_g=m_sgu_ln_g, sgu_ln_b=m_sgu_ln_b, sgu_w_s=m_sgu_w_s,
             sgu_b_s=m_sgu_b_s, mix_w_out=m_mix_w_out, ln2_g=m_ln2_g, ln2_b=m_ln2_b, ffn2_w_up=m_ffn2_w_up,
             ffn2_w_down=m_ffn2_w_down, ln3_g=m_ln3_g, ln3_b=m_ln3_b)
    v = dict(ln1_g=v_ln1_g, ln1_b=v_ln1_b, ffn1_w_up=v_ffn1_w_up, ffn1_w_down=v_ffn1_w_down, mix_w_in=v_mix_w_in,
             fox_b_f=v_fox_b_f, conv_w=v_conv_w, sgu_ln_g=v_sgu_ln_g, sgu_ln_b=v_sgu_ln_b, sgu_w_s=v_sgu_w_s,
             sgu_b_s=v_sgu_b_s, mix_w_out=v_mix_w_out, ln2_g=v_ln2_g, ln2_b=v_ln2_b, ffn2_w_up=v_ffn2_w_up,
             ffn2_w_down=v_ffn2_w_down, ln3_g=v_ln3_g, ln3_b=v_ln3_b)

    mx, my, mc = lax.axis_index("x"), lax.axis_index("y"), lax.axis_index("c")
    me = 4 * mx + 2 * my + mc
    n_seq, seq, _ = x.shape
    t_tok = n_seq * seq
    for name in UP_NAMES:
        for t in (w, m, v):
            t[name] = jnp.transpose(t[name], (0, 2, 1))

    comm = _Overlap(w, x, me, jnp.stack([mc, 2 * mx + my]).astype(jnp.int32))
    small = {name: w[name] for name in SMALL_NAMES}

    loss_dev, grad_x, late_rows = _local_step(
        x.reshape(t_tok, D_MODEL), loss_target.reshape(t_tok, D_MODEL), comm, small, n_seq)
    loss = lax.psum(loss_dev[0, 0], ("x", "y", "c"))
    out, pieces = comm.finish(w, m, v)
    for name in UP_NAMES:
        out[name] = [jnp.transpose(a, (0, 2, 1)) for a in out[name]]

    pieces.append(_allgather_small(late_rows))
    spans = [(l, 0, LN1_ROWS + REST_ROWS) for l in reversed(range(1, DEPTH))] + [(0, LN1_ROWS, LN1_ROWS + REST_ROWS), (0, 0, LN1_ROWS)]

    def widen(a):
        return lax.dynamic_update_slice(jnp.zeros((3, D_CONV), F32), a, (0, me * (D_CONV // N_DEV)))

    packed = [[_pack_layer({**{name: t[name][l] for name in SMALL_NAMES}, "conv_w": widen(t["conv_w"][l])}) for l in range(DEPTH)]
              for t in (w, m, v)]
    rows_out = {}
    for (l, lo, hi), gathered_piece in zip(spans, pieces):
        rows_out[(l, lo)] = _adamw_small(gathered_piece, *[packed[t][l][lo:hi] for t in range(3)])
    per_layer = []
    for l in range(DEPTH):
        parts = sorted(lo for (ll, lo) in rows_out if ll == l)
        per_layer.append([_unpack_layer(jnp.concatenate([rows_out[(l, lo)][k] for lo in parts], axis=0)) for k in range(4)])
    for name in SMALL_NAMES:
        out[name] = [jnp.stack([per_layer[l][k][name] for l in range(DEPTH)]) for k in range(4)]
    lo_col = me * (D_CONV // N_DEV)
    out["conv_w"] = [jnp.stack([lax.dynamic_slice(per_layer[l][k]["conv_w"], (0, lo_col), (3, D_CONV // N_DEV)) for l in range(DEPTH)])
                     for k in range(4)]

    return (loss, grad_x.reshape(x.shape), *[out[name][0] for name in WEIGHT_ORDER], *[out[name][1] for name in WEIGHT_ORDER],
            *[out[name][2] for name in WEIGHT_ORDER], *[out[name][3] for name in WEIGHT_ORDER])
```

```python
import functools

import jax
import jax.numpy as jnp
from jax import lax
from jax.experimental import pallas as pl
from jax.experimental.pallas import tpu as pltpu

F32 = jnp.float32
BF16 = jnp.bfloat16
MESH = pl.DeviceIdType.MESH

N_DEV = 8
DEPTH = 2
D_MODEL = 1024
D_FF = 2816
FFN_BLK = 2 * D_FF // N_DEV
MXU_TILE_V7X = 256
FFN_CHUNKS = tuple((lo, min(lo + 3 * MXU_TILE_V7X, D_FF)) for lo in range(0, D_FF, 3 * MXU_TILE_V7X))
FFN_BWD_CHUNKS = tuple((lo, min(lo + 4 * MXU_TILE_V7X, D_FF)) for lo in range(0, D_FF, 4 * MXU_TILE_V7X))
DW_ROWS = D_FF // 2
D_CONV = 256
D_FOX = 512
N_HEADS = 8
D_SGU = 256
N_SGU_GROUPS = 4
SGU_CHUNK = 128
D_IN = 3 * D_CONV + 3 * D_FOX + N_HEADS + 2 * D_SGU
D_IN_SHARD = D_IN // N_DEV
COL_CONV, COL_QKV, COL_SGU, COL_F = 0, 768, 2304, 2816
D_IN_PAD = 2944
F_ORIG = 3 * D_CONV + 3 * D_FOX
ALPHA = (2 * DEPTH) ** 0.25
LN_EPS = 1e-5
ATT_SCALE = 0.125
ATT_BLK = 512
ATT_PAIRS = 2
NEG = -1e30

ADAM_LR, ADAM_B1, ADAM_B2, ADAM_EPS, ADAM_WD, ADAM_STEP = 0.001, 0.9, 0.999, 1e-08, 0.01, 10

VMEM_BYTES_V7X = 64 * 1024 * 1024
HIGHEST = lax.Precision.HIGHEST


def _params(vmem_mb, sem=None):
    assert vmem_mb * 1024 * 1024 < VMEM_BYTES_V7X
    kw = dict(vmem_limit_bytes=vmem_mb * 1024 * 1024)
    if sem is not None:
        kw["dimension_semantics"] = sem
    return pltpu.CompilerParams(**kw)


def _dot(a, b, precision=None):
    return lax.dot_general(a, b, (((1,), (0,)), ((), ())), preferred_element_type=F32, precision=precision)


def _dot_nt(a, b):
    return lax.dot_general(a, b, (((1,), (1,)), ((), ())), preferred_element_type=F32)


def _dot_tn(a, b):
    return lax.dot_general(a, b, (((0,), (0,)), ((), ())), preferred_element_type=F32)


def _ln_stats(z):
    mu = jnp.mean(z, axis=-1, keepdims=True)
    zc = z - mu
    var = jnp.mean(zc * zc, axis=-1, keepdims=True)
    rstd = lax.rsqrt(var + LN_EPS)
    return zc * rstd, rstd


def _ln_bwd(dy, xhat, rstd, g):
    dxh = dy * g
    m1 = jnp.mean(dxh, axis=-1, keepdims=True)
    m2 = jnp.mean(dxh * xhat, axis=-1, keepdims=True)
    return rstd * (dxh - m1 - xhat * m2)


_GELU_C = 0.7978845608028654


def _gelu(x):
    return 0.5 * x * (1.0 + jnp.tanh(_GELU_C * (x + 0.044715 * x * x * x)))


def _gelu_with_grad(x):
    t = jnp.tanh(_GELU_C * (x + 0.044715 * x * x * x))
    return 0.5 * x * (1.0 + t), 0.5 * (1.0 + t) + 0.5 * x * (1.0 - t * t) * _GELU_C * (1.0 + 3 * 0.044715 * x * x)


def _hbm(shape, dtype):
    n = 1
    for d in shape:
        n *= d
    if n * jnp.dtype(dtype).itemsize >= 1024 * 1024:
        return pltpu.HBM(tuple(shape), dtype)
    return jax.ShapeDtypeStruct(tuple(shape), dtype)


def _vspec():
    return pl.BlockSpec(memory_space=pltpu.VMEM)


def _anyspec():
    return pl.BlockSpec(memory_space=pl.ANY)


def _mesh_pos():
    return lax.axis_index("x"), lax.axis_index("y"), lax.axis_index("c")


def _other_chips(x, y):
    return [(1 - x, y), (x, 1 - y), (1 - x, 1 - y)]


_HBM_SPEC = pl.BlockSpec(memory_space=pltpu.HBM)
_SEM_SPEC = pl.BlockSpec(memory_space=pltpu.SEMAPHORE)
_DATAFLOW_EFFECT = pltpu.SideEffectType.DATAFLOW_SIDE_EFFECTING


def _remote_copies(plan, refs, send_sems, recv_sems):
    return [pltpu.make_async_remote_copy(src_ref=src, dst_ref=dst, send_sem=send_sems.at[k], recv_sem=recv_sems.at[k],
                                         device_id=to, device_id_type=MESH)
            for k, (src, dst, to) in enumerate(plan(refs, *_mesh_pos()))]


def _exchange_start(name, plan, n_copies, arrays, after):
    n = len(arrays)

    def body(*refs):
        send_sems, recv_sems, token = refs[n + 1], refs[n + 2], refs[-1]
        for cp in _remote_copies(plan, refs[:n], send_sems, recv_sems):
            cp.start()
        token[...] = jnp.zeros_like(token)

    out = pl.pallas_call(
        body, name=name,
        out_shape=(pltpu.SemaphoreType.DMA((n_copies,)), pltpu.SemaphoreType.DMA((n_copies,)),
                   *[pltpu.HBM(a.shape, a.dtype) for a in arrays], _hbm((8, 128), F32)),
        in_specs=[_HBM_SPEC] * n + [_anyspec()],
        out_specs=(_SEM_SPEC, _SEM_SPEC, *[_HBM_SPEC] * n, _vspec()),
        input_output_aliases={i: 2 + i for i in range(n)},
        compiler_params=pltpu.CompilerParams(has_side_effects=_DATAFLOW_EFFECT),
    )(*[pltpu.with_memory_space_constraint(a, pltpu.HBM) for a in arrays], after)
    return out[0], out[1], list(out[2:2 + n]), out[-1]


def _exchange_wait(name, plan, n_copies, started, after):
    send_sems, recv_sems, arrays, _ = started
    n = len(arrays)

    def body(*refs):
        for cp in _remote_copies(plan, refs[:n], refs[n], refs[n + 1]):
            cp.wait_send()
            cp.wait_recv()

    out = pl.pallas_call(
        body, name=name,
        out_shape=tuple(pltpu.HBM(a.shape, a.dtype) for a in arrays),
        in_specs=[_HBM_SPEC] * n + [_SEM_SPEC, _SEM_SPEC, _anyspec()], out_specs=tuple([_HBM_SPEC] * n),
        input_output_aliases={i: i for i in range(n)},
        compiler_params=pltpu.CompilerParams(has_side_effects=_DATAFLOW_EFFECT),
    )(*arrays, send_sems, recv_sems, after)
    return list(out)


def _gather_plan(m):
    def plan(refs, x, y, c):
        me = 4 * x + 2 * y + c
        return [(refs[i], refs[m + i].at[me], (*chip, c)) for i in range(m) for chip in _other_chips(x, y)]
    return plan


def _pass_on_plan(m):
    def plan(refs, x, y, c):
        out = []
        for i in range(m):
            out.append((refs[i], refs[m + i].at[4 * x + 2 * y + c], (x, y, 1 - c)))
            for cx, cy in _other_chips(x, y):
                block = refs[m + i].at[4 * cx + 2 * cy + c]
                out.append((block, block, (x, y, 1 - c)))
        return out
    return plan


def _peers_plan():
    def plan(refs, x, y, c):
        rel = [(dx, dy, dc) for dx in (0, 1) for dy in (0, 1) for dc in (0, 1)][1:]
        return [(refs[0], refs[1].at[4 * x + 2 * y + c], (x ^ dx, y ^ dy, c ^ dc)) for dx, dy, dc in rel]
    return plan


def _allgather_small(v):
    rows = v.shape[0]

    def body(v_ref, out_ref, send_sems, recv_sems):
        x, y, c = _mesh_pos()
        me = 4 * x + 2 * y + c
        out_ref[me] = v_ref[...]
        rel = [(dx, dy, dc) for dx in (0, 1) for dy in (0, 1) for dc in (0, 1)][1:]
        copies = []
        for k, (dx, dy, dc) in enumerate(rel):
            to = (x ^ dx, y ^ dy, c ^ dc)
            copies.append(pltpu.make_async_remote_copy(
                src_ref=v_ref, dst_ref=out_ref.at[me], send_sem=send_sems.at[k], recv_sem=recv_sems.at[k],
                device_id=to, device_id_type=MESH))
        for cp in copies:
            cp.start()
        for k, (dx, dy, dc) in enumerate(rel):
            src_blk = 4 * (x ^ dx) + 2 * (y ^ dy) + (c ^ dc)
            pltpu.make_async_remote_copy(
                src_ref=v_ref, dst_ref=out_ref.at[src_blk], send_sem=send_sems.at[k], recv_sem=recv_sems.at[k],
                device_id=(x, y, c), device_id_type=MESH).wait_recv()
        for cp in copies:
            cp.wait_send()

    return pl.pallas_call(
        body, name="allgather_small",
        out_shape=jax.ShapeDtypeStruct((N_DEV, rows, 128), v.dtype),
        in_specs=[_vspec()], out_specs=_vspec(),
        scratch_shapes=[pltpu.SemaphoreType.DMA((7,)), pltpu.SemaphoreType.DMA((7,))],
        compiler_params=_params(24),
    )(v)


def _sibling_plan(n):
    def plan(refs, x, y, c):
        return [(refs[a].at[2 * q + (1 - c)], refs[n + a].at[q], (x, y, 1 - c)) for a in range(n) for q in range(4)]
    return plan


def _chip_plan(n):
    def plan(refs, x, y, c):
        return [(refs[a].at[2 * cx + cy], refs[n + a].at[j], (cx, cy, c))
                for a in range(n) for j, (cx, cy) in enumerate(_other_chips(x, y))]
    return plan


def _row_tile(rows, cols, budget_bytes=2 * 1024 * 1024):
    best = 8
    for t in range(8, rows + 1, 8):
        if rows % t == 0 and t * cols * 4 <= budget_bytes:
            best = t
    return best


def _chip_partial(g, recv, where):
    _, rows, cols = g.shape
    tr = _row_tile(rows, cols)

    def body(where_ref, g_ref, r_ref, own_ref, o16_ref):
        s = g_ref[...] + r_ref[...]
        o16_ref[...] = s.astype(BF16)

        @pl.when(pl.program_id(1) == where_ref[1])
        def _():
            own_ref[...] = s

    blk = (None, tr, cols)
    return pl.pallas_call(
        body, name="rs_chip_partial",
        grid_spec=pltpu.PrefetchScalarGridSpec(
            num_scalar_prefetch=1, grid=(rows // tr, 4),
            in_specs=[pl.BlockSpec(blk, lambda i, q, w: (2 * q + w[0], i, 0)),
                      pl.BlockSpec(blk, lambda i, q, w: (q, i, 0))],
            out_specs=[pl.BlockSpec((tr, cols), lambda i, q, w: (i, 0)), pl.BlockSpec(blk, lambda i, q, w: (q, i, 0))]),
        out_shape=[_hbm((rows, cols), F32), _hbm((4, rows, cols), BF16)],
        compiler_params=_params(32),
    )(where, g, recv)


def _adam_math(w, g, m, v):
    m = ADAM_B1 * m + (1.0 - ADAM_B1) * g
    v = ADAM_B2 * v + (1.0 - ADAM_B2) * (g * g)
    m_hat = m / (1.0 - ADAM_B1 ** ADAM_STEP)
    v_hat = v / (1.0 - ADAM_B2 ** ADAM_STEP)
    delta = -ADAM_LR * (m_hat / (jnp.sqrt(v_hat) + ADAM_EPS) + ADAM_WD * w)
    return delta, m, v


def _adamw_shard(own32, recv16, w, m, v, layer, earlier):
    depth, rows, cols = w.shape
    tr = _row_tile(rows, cols, 1024 * 1024)
    n_prev = 0 if earlier is None else 4

    def body(p_ref, r_ref, w_ref, m_ref, v_ref, *rest):
        g_out, d_out, m_out, v_out = rest[n_prev:]
        g = p_ref[...] + r_ref[0].astype(F32) + r_ref[1].astype(F32) + r_ref[2].astype(F32)
        d, mn, vn = _adam_math(w_ref[...], g, m_ref[...], v_ref[...])
        g_out[...] = g
        d_out[...] = d
        m_out[...] = mn
        v_out[...] = vn

    mine = pl.BlockSpec((None, tr, cols), lambda i: (layer, i, 0))
    return pl.pallas_call(
        body, name="adamw_shard", grid=(rows // tr,),
        in_specs=[pl.BlockSpec((tr, cols), lambda i: (i, 0)), pl.BlockSpec((3, tr, cols), lambda i: (0, i, 0)),
                  mine, mine, mine] + [_anyspec()] * n_prev,
        out_specs=[mine] * 4,
        out_shape=[_hbm((depth, rows, cols), F32)] * 4,
        input_output_aliases={5 + k: k for k in range(n_prev)},
        compiler_params=_params(32),
    )(own32, recv16, *[pltpu.with_memory_space_constraint(t, pltpu.HBM) for t in (w, m, v)],
      *([] if earlier is None else earlier))


def _adamw_small(gathered, w, m, v):
    rows = w.shape[0]

    def body(a_ref, w_ref, m_ref, v_ref, g_out, d_out, m_out, v_out):
        g = a_ref[0]
        for d in range(1, N_DEV):
            g = g + a_ref[d]
        dl, mn, vn = _adam_math(w_ref[...], g, m_ref[...], v_ref[...])
        g_out[...] = g
        d_out[...] = dl
        m_out[...] = mn
        v_out[...] = vn

    return pl.pallas_call(
        body, name="adamw_small",
        in_specs=[_vspec()] * 4, out_specs=[_vspec()] * 4,
        out_shape=[_hbm((rows, 128), F32)] * 4,
        compiler_params=_params(32),
    )(gathered, w, m, v)


def _load_weights_once(pairs, sems):
    @pl.when(pl.program_id(0) == 0)
    def _():
        cps = [pltpu.make_async_copy(src, dst, sems.at[i]) for i, (src, dst) in enumerate(pairs)]
        for cp in cps:
            cp.start()
        for cp in cps:
            cp.wait()


def _ffn_fwd(x, wup, wd, ln_g, ln_b, after, target=None, tm=512):
    t_tok = x.shape[0]
    last = target is not None

    def body(x_ref, g_ref, b_ref, wup_hbm, wd_hbm, _after, *rest):
        if last:
            t_ref, dxn_ref, loss_ref, z_ref, gu_ref, wup_v, wd_v, sems = rest
        else:
            xn_ref, xnb_ref, z_ref, gu_ref, xb_ref, wup_v, wd_v, sems = rest
        _load_weights_once([(wup_hbm, wup_v), (wd_hbm, wd_v)], sems)
        xb = x_ref[...].astype(BF16)
        if not last:
            xb_ref[...] = xb
        y = None
        for lo, hi in FFN_CHUNKS:
            g = _dot_nt(xb, wup_v[0, lo:hi])
            u = _dot_nt(xb, wup_v[1, lo:hi])
            gu_ref[0, :, lo:hi] = g.astype(BF16)
            gu_ref[1, :, lo:hi] = u.astype(BF16)
            a = (g * jax.nn.sigmoid(g) * u).astype(BF16)
            part = _dot(a, wd_v[lo:hi])
            y = part if y is None else y + part
        z = ALPHA * x_ref[...] + 0.5 * y
        xhat, _ = _ln_stats(z)
        xn = xhat * g_ref[...] + b_ref[...]
        z_ref[...] = z
        if last:
            err = xn - t_ref[...]
            dxn_ref[...] = err * (1.0 / D_MODEL)
            part = jnp.sum(jnp.sum(err * err, axis=1, keepdims=True), axis=0, keepdims=True) * (0.5 / D_MODEL)

            @pl.when(pl.program_id(0) == 0)
            def _():
                loss_ref[...] = jnp.zeros_like(loss_ref)

            loss_ref[...] += part
        else:
            xn_ref[...] = xn
            xnb_ref[...] = xn.astype(BF16)

    tok = pl.BlockSpec((tm, D_MODEL), lambda i: (i, 0))
    vec = pl.BlockSpec((1, D_MODEL), lambda i: (0, 0))
    gu_spec = pl.BlockSpec((2, tm, D_FF), lambda i: (0, i, 0))
    gu_shape = _hbm((2, t_tok, D_FF), BF16)
    f32_tok, bf16_tok = _hbm((t_tok, D_MODEL), F32), _hbm((t_tok, D_MODEL), BF16)
    if last:
        extra_in, extra_spec = [target], [tok]
        out_specs = [tok, pl.BlockSpec((1, 128), lambda i: (0, 0)), tok, gu_spec]
        out_shape = [f32_tok, _hbm((1, 128), F32), f32_tok, gu_shape]
    else:
        extra_in, extra_spec = [], []
        out_specs = [tok, tok, tok, gu_spec, tok]
        out_shape = [f32_tok, bf16_tok, f32_tok, gu_shape, bf16_tok]
    return pl.pallas_call(
        body, name="ffn_fwd_loss" if last else "ffn_fwd", grid=(t_tok // tm,),
        in_specs=[tok, vec, vec, _anyspec(), _anyspec(), _anyspec()] + extra_spec,
        out_specs=out_specs, out_shape=out_shape,
        scratch_shapes=[pltpu.VMEM((2, D_FF, D_MODEL), BF16), pltpu.VMEM((D_FF, D_MODEL), BF16),
                        pltpu.SemaphoreType.DMA((2,))],
        compiler_params=_params(62, ("arbitrary",)),
    )(x, ln_g, ln_b, wup, wd, after, *extra_in)


def _ffn_bwd(dxn, z, gu, wup, wd, ln_g, after, tm=256):
    t_tok = dxn.shape[0]

    def body(dxn_ref, z_ref, gu_ref, g_ref, wup_hbm, wd_hbm, _after,
             dx_ref, dy_ref, a_ref, dgu_ref, dg_ref, db_ref, wup_v, wd_v, sems):
        i = pl.program_id(0)
        _load_weights_once([(wup_hbm, wup_v), (wd_hbm, wd_v)], sems)
        dxn_t = dxn_ref[...]
        xhat, rstd = _ln_stats(z_ref[...])
        pg = jnp.sum(dxn_t * xhat, axis=0, keepdims=True)
        pb = jnp.sum(dxn_t, axis=0, keepdims=True)

        @pl.when(i == 0)
        def _():
            dg_ref[...] = pg
            db_ref[...] = pb

        @pl.when(i > 0)
        def _():
            dg_ref[...] += pg
            db_ref[...] += pb

        dz = _ln_bwd(dxn_t, xhat, rstd, g_ref[...])
        dy = (0.5 * dz).astype(BF16)
        dy_ref[...] = dy
        dx = ALPHA * dz
        for lo, hi in FFN_BWD_CHUNKS:
            da = _dot_nt(dy, wd_v[lo:hi])
            g = gu_ref[0, :, lo:hi].astype(F32)
            u = gu_ref[1, :, lo:hi].astype(F32)
            sig = jax.nn.sigmoid(g)
            silu = g * sig
            a_ref[:, lo:hi] = (silu * u).astype(BF16)
            dg = (da * u * (sig * (1.0 + g * (1.0 - sig)))).astype(BF16)
            du = (da * silu).astype(BF16)
            dgu_ref[0, :, lo:hi] = dg
            dgu_ref[1, :, lo:hi] = du
            dx = dx + _dot(dg, wup_v[0, lo:hi]) + _dot(du, wup_v[1, lo:hi])
        dx_ref[...] = dx

    tok = pl.BlockSpec((tm, D_MODEL), lambda i: (i, 0))
    vec = pl.BlockSpec((1, D_MODEL), lambda i: (0, 0))
    gu_spec = pl.BlockSpec((2, tm, D_FF), lambda i: (0, i, 0))
    return pl.pallas_call(
        body, name="ffn_bwd", grid=(t_tok // tm,),
        in_specs=[tok, tok, gu_spec, vec, _anyspec(), _anyspec(), _anyspec()],
        out_specs=[tok, tok, pl.BlockSpec((tm, D_FF), lambda i: (i, 0)), gu_spec, vec, vec],
        out_shape=[_hbm((t_tok, D_MODEL), F32), _hbm((t_tok, D_MODEL), BF16),
                   _hbm((t_tok, D_FF), BF16), _hbm((2, t_tok, D_FF), BF16),
                   _hbm((1, D_MODEL), F32), _hbm((1, D_MODEL), F32)],
        scratch_shapes=[pltpu.VMEM((2, D_FF, D_MODEL), BF16), pltpu.VMEM((D_FF, D_MODEL), BF16),
                        pltpu.SemaphoreType.DMA((2,))],
        compiler_params=_params(60, ("arbitrary",)),
    )(dxn, z, gu, ln_g, wup, wd, after)


def _matmul_tn(a, b, after, tk=4096, bm=None):
    ga, t_tok, m = a.shape
    gb, _, n = b.shape
    groups = max(ga, gb)
    tk = min(tk, t_tok)
    bm = m if bm is None else bm

    def body(a_ref, b_ref, _after, o_ref):
        p = _dot_tn(a_ref[...].astype(BF16), b_ref[...].astype(BF16))

        @pl.when(pl.program_id(2) == 0)
        def _():
            o_ref[...] = p

        @pl.when(pl.program_id(2) > 0)
        def _():
            o_ref[...] += p

    return pl.pallas_call(
        body, name=f"matmul_tn_{m}x{n}", grid=(groups, m // bm, t_tok // tk),
        in_specs=[pl.BlockSpec((None, tk, bm), (lambda g, i, t: (g, t, i)) if ga > 1 else (lambda g, i, t: (0, t, i))),
                  pl.BlockSpec((None, tk, n), (lambda g, i, t: (g, t, 0)) if gb > 1 else (lambda g, i, t: (0, t, 0))),
                  _anyspec()],
        out_specs=pl.BlockSpec((None, bm, n), lambda g, i, t: (g, i, 0)),
        out_shape=_hbm((groups, m, n), F32),
        compiler_params=_params(56, ("arbitrary", "arbitrary", "arbitrary")),
    )(a, b, after)


def _in_proj(x, w_in, tm=512):
    t_tok = x.shape[0]

    def body(x_ref, w_ref, conv_ref, qkv_ref, sgu_ref, f_ref):
        xb = x_ref[...].astype(BF16)
        conv_ref[...] = _dot(xb, w_ref[:, COL_CONV:COL_QKV])
        qkv_ref[...] = _dot(xb, w_ref[:, COL_QKV:COL_SGU]).astype(BF16)
        sgu_ref[...] = _dot(xb, w_ref[:, COL_SGU:COL_F])
        f_ref[...] = _dot(xb, w_ref[:, COL_F:D_IN_PAD])

    def tok(n):
        return pl.BlockSpec((tm, n), lambda i: (i, 0))

    return pl.pallas_call(
        body, name="mix_in_proj", grid=(t_tok // tm,),
        in_specs=[tok(D_MODEL), pl.BlockSpec((D_MODEL, D_IN_PAD), lambda i: (0, 0))],
        out_specs=[tok(768), tok(1536), tok(512), tok(128)],
        out_shape=[_hbm((t_tok, 768), F32), _hbm((t_tok, 1536), BF16),
                   _hbm((t_tok, 512), F32), _hbm((t_tok, 128), F32)],
        compiler_params=_params(48, ("arbitrary",)),
    )(x, w_in)


def _shift_down(a, k):
    row = lax.broadcasted_iota(jnp.int32, a.shape, 0)
    return jnp.where(row >= k, pltpu.roll(a, k, 0), 0.0)


def _shift_up(a, k):
    rows = a.shape[0]
    row = lax.broadcasted_iota(jnp.int32, a.shape, 0)
    return jnp.where(row < rows - k, pltpu.roll(a, rows - k, 0), 0.0)


def _tril(n):
    return lax.broadcasted_iota(jnp.int32, (n, n), 0) >= lax.broadcasted_iota(jnp.int32, (n, n), 1)


def _sgu_group_of_lane():
    return lax.broadcasted_iota(jnp.int32, (1, D_SGU), 1) // (D_SGU // N_SGU_GROUPS)


def _log_sigmoid(x):
    return jnp.minimum(x, 0.0) - jnp.log1p(jnp.exp(-jnp.abs(x)))


def _mix_mid_fwd(conv, sgu, f, conv_w, b_f, sgu_g, sgu_b, w_s, b_mat, n_seq):
    t_tok = conv.shape[0]
    seq = t_tok // n_seq
    n_chunk = seq // SGU_CHUNK
    per_blk = ATT_BLK // SGU_CHUNK

    def body(conv_ref, sgu_ref, f_ref, cw_ref, bf_ref, lg_ref, lb_ref, ws_ref, bm_ref, cat_ref, cum_ref):
        z = conv_ref[:, 256:512] * conv_ref[:, 512:768]
        y = cw_ref[0:1, :] * _shift_down(z, 2) + cw_ref[1:2, :] * _shift_down(z, 1) + cw_ref[2:3, :] * z
        cat_ref[:, 0:D_CONV] = (conv_ref[:, 0:256] * y).astype(BF16)
        cat_ref[:, D_CONV:D_CONV + D_FOX] = jnp.zeros((seq, D_FOX), BF16)

        tril = _tril(SGU_CHUNK)
        grp = _sgu_group_of_lane()
        wc = [jnp.where(tril, ws_ref[g], 0.0).astype(BF16) for g in range(N_SGU_GROUPS)]
        tri_f = tril.astype(F32)
        carry = jnp.zeros((1, 128), F32)
        for n in range(n_chunk):
            rows = pl.ds(n * SGU_CHUNK, SGU_CHUNK)
            u = _gelu(sgu_ref[rows, 0:256])
            vhat, _ = _ln_stats(_gelu(sgu_ref[rows, 256:512]))
            vn = (vhat * lg_ref[...] + lb_ref[...]).astype(BF16)
            mixed = bm_ref[...]
            for g in range(N_SGU_GROUPS):
                mixed = mixed + jnp.where(grp == g, _dot(wc[g], vn), 0.0)
            cat_ref[rows, D_CONV + D_FOX:D_MODEL] = (u * mixed).astype(BF16)

            log_f = _log_sigmoid(f_ref[rows, :] + bf_ref[...])
            cs = _dot(tri_f, log_f, HIGHEST) + carry
            carry = cs[SGU_CHUNK - 1:SGU_CHUNK, :]
            cs_t = cs.T
            lanes = pl.ds((n % per_blk) * SGU_CHUNK, SGU_CHUNK)
            for h in range(N_HEADS):
                cum_ref[h, n // per_blk, :, lanes] = cs_t[h:h + 1, :]

    def seq_blk(n):
        return pl.BlockSpec((seq, n), lambda b: (b, 0))

    def full(shape):
        return pl.BlockSpec(shape, lambda b: (0,) * len(shape))

    return pl.pallas_call(
        body, name="mix_mid_fwd", grid=(n_seq,),
        in_specs=[seq_blk(768), seq_blk(512), seq_blk(128), full((8, 256)), full((1, 128)), full((1, 256)),
                  full((1, 256)), full((4, 128, 128)), full((128, 256))],
        out_specs=[seq_blk(D_MODEL), pl.BlockSpec((N_HEADS, seq // ATT_BLK, 1, ATT_BLK), lambda b: (b, 0, 0, 0))],
        out_shape=[_hbm((t_tok, D_MODEL), BF16), _hbm((n_seq * N_HEADS, seq // ATT_BLK, 1, ATT_BLK), F32)],
        compiler_params=_params(48, ("arbitrary",)),
    )(conv, sgu, f, conv_w, b_f, sgu_g, sgu_b, w_s, b_mat)


def _head_masks():
    lane = lax.broadcasted_iota(jnp.int32, (1, 128), 1)
    return lane < 64, lane


def _fox_fwd(qkv, cum_t, cat, n_seq):
    t_tok = qkv.shape[0]
    seq = t_tok // n_seq
    nq = seq // ATT_BLK
    blk = ATT_BLK

    def body(q_ref, k_ref, v_ref, c_ref, _cat, o_ref, lse_ref):
        qi = pl.program_id(2)
        first, _ = _head_masks()
        one = jnp.ones((1, 128), BF16)
        qh = []
        for hp in range(ATT_PAIRS):
            qs = q_ref[:, 128 * hp:128 * hp + 128] * ATT_SCALE
            zero = jnp.zeros_like(qs)
            qh += [jnp.where(first, qs, zero), jnp.where(first, zero, qs)]

        def step(kb, carry, masked):
            ms, accs = carry
            rows = pl.ds(pl.multiple_of(kb * blk, blk), blk)
            new_m, new_acc = [], []
            for hp in range(ATT_PAIRS):
                k = k_ref[rows, 128 * hp:128 * hp + 128]
                v = v_ref[rows, 128 * hp:128 * hp + 128]
                for h in range(2):
                    i = 2 * hp + h
                    s = _dot_nt(qh[i], k) - c_ref[i, kb]
                    if masked:
                        s = jnp.where(causal, s, NEG)
                    m_new = jnp.maximum(ms[i], jnp.max(s, axis=1, keepdims=True))
                    p = jnp.exp(s - m_new)
                    vh = jnp.where(first, v, one) if h == 0 else jnp.where(first, one, v)
                    new_acc.append(accs[i] * jnp.exp(ms[i] - m_new) + _dot(p.astype(BF16), vh))
                    new_m.append(m_new)
            return tuple(new_m), tuple(new_acc)

        causal = _tril(blk)
        n_heads = 2 * ATT_PAIRS
        col = jnp.full((blk, 1), NEG, F32)
        zacc = jnp.zeros((blk, 128), F32)
        carry = lax.fori_loop(0, qi, lambda kb, cr: step(kb, cr, False), ((col,) * n_heads, (zacc,) * n_heads))
        ms, accs = step(qi, carry, True)
        for hp in range(ATT_PAIRS):
            acc0, acc1 = accs[2 * hp], accs[2 * hp + 1]
            l0 = pltpu.roll(acc0, 64, 1)
            l1 = pltpu.roll(acc1, 64, 1)
            o_ref[:, 128 * hp:128 * hp + 128] = jnp.where(first, acc0 / l0, acc1 / l1).astype(BF16)
            lse_ref[:, 128 * hp:128 * hp + 128] = jnp.where(first, ms[2 * hp] + jnp.log(l0), ms[2 * hp + 1] + jnp.log(l1))

    wide = 128 * ATT_PAIRS
    n_grp = D_FOX // wide
    first_col = D_CONV // wide
    return pl.pallas_call(
        body, name="fox_fwd", grid=(n_seq, n_grp, nq),
        in_specs=[pl.BlockSpec((blk, wide), lambda b, g, qi: (b * nq + qi, g)),
                  pl.BlockSpec((seq, wide), lambda b, g, qi: (b, n_grp + g)),
                  pl.BlockSpec((seq, wide), lambda b, g, qi: (b, 2 * n_grp + g)),
                  pl.BlockSpec((2 * ATT_PAIRS, nq, 1, blk), lambda b, g, qi: (b * n_grp + g, 0, 0, 0)), _anyspec()],
        out_specs=[pl.BlockSpec((blk, wide), lambda b, g, qi: (b * nq + qi, first_col + g)),
                   pl.BlockSpec((blk, wide), lambda b, g, qi: (b * nq + qi, g))],
        out_shape=[_hbm(cat.shape, BF16), _hbm((t_tok, D_FOX), F32)],
        input_output_aliases={4: 0},
        compiler_params=_params(48, ("arbitrary", "arbitrary", "arbitrary")),
    )(qkv, qkv, qkv, cum_t, cat)


def _fox_bwd(qkv, cum_t, cat, lse, d_o, n_seq):
    t_tok = qkv.shape[0]
    seq = t_tok // n_seq
    nk = seq // ATT_BLK
    blk = ATT_BLK

    def body(q_ref, k_ref, v_ref, c_ref, o_ref, lse_ref, do_ref, dq_ref, dk_ref, dv_ref, drow_ref, dcol_ref):
        kb = pl.program_id(2)
        first, lane = _head_masks()
        second = jnp.logical_not(first)
        one = jnp.ones((1, 128), BF16)
        causal = _tril(blk)

        @pl.when(kb == 0)
        def _():
            dq_ref[...] = jnp.zeros_like(dq_ref)
            drow_ref[...] = jnp.zeros_like(drow_ref)

        def step(qi, carry, masked):
            rows = pl.ds(pl.multiple_of(qi * blk, blk), blk)
            dks, dvs = carry
            new_dk, new_dv = [], []
            for hp in range(ATT_PAIRS):
                cols = slice(128 * hp, 128 * hp + 128)
                k = k_ref[:, cols]
                v = v_ref[:, cols]
                ks = k * ATT_SCALE
                zero = jnp.zeros_like(k)
                qs = q_ref[rows, cols] * ATT_SCALE
                d_o = do_ref[rows, cols]
                dd = d_o.astype(F32) * o_ref[rows, cols].astype(F32)
                lse_t = lse_ref[rows, cols]
                dq = []
                for h, mine in enumerate((first, second)):
                    i = 2 * hp + h
                    qh = jnp.where(mine, qs, zero)
                    doh = jnp.where(mine, d_o, zero)
                    delta = jnp.sum(jnp.where(mine, dd, 0.0), axis=1, keepdims=True)
                    lse_h = jnp.sum(jnp.where(lane == 64 * h, lse_t, 0.0), axis=1, keepdims=True)
                    s = _dot_nt(qh, k) - c_ref[i]
                    if masked:
                        s = jnp.where(causal, s, NEG)
                    p = jnp.exp(s - lse_h)
                    ds = (p * (_dot_nt(doh, v) - delta)).astype(BF16)
                    new_dk.append(dks[i] + _dot_tn(ds, jnp.where(mine, qs, one)))
                    new_dv.append(dvs[i] + _dot_tn(p.astype(BF16), doh))
                    dq.append(_dot(ds, jnp.where(mine, ks, one)))
                dq_ref[rows, cols] += jnp.where(first, dq[0], dq[1])
                drow_ref[rows, cols] += jnp.where(first, dq[1], dq[0])
            return tuple(new_dk), tuple(new_dv)

        zt = (jnp.zeros((blk, 128), F32),) * (2 * ATT_PAIRS)
        carry = step(kb, (zt, zt), True)
        dks, dvs = lax.fori_loop(kb + 1, nk, lambda qi, cr: step(qi, cr, False), carry)
        for hp in range(ATT_PAIRS):
            cols = slice(128 * hp, 128 * hp + 128)
            dk_ref[:, cols] = jnp.where(first, dks[2 * hp], dks[2 * hp + 1]).astype(BF16)
            dcol_ref[:, cols] = jnp.where(first, dks[2 * hp + 1], dks[2 * hp])
            dv_ref[:, cols] = (dvs[2 * hp] + dvs[2 * hp + 1]).astype(BF16)

    wide = 128 * ATT_PAIRS
    n_grp = D_FOX // wide

    def seq_spec(col0):
        return pl.BlockSpec((seq, wide), lambda b, g, kb: (b, col0 + g))

    def key_spec(col0):
        return pl.BlockSpec((blk, wide), lambda b, g, kb: (b * nk + kb, col0 + g))

    return pl.pallas_call(
        body, name="fox_bwd", grid=(n_seq, n_grp, nk),
        in_specs=[seq_spec(0), key_spec(n_grp), key_spec(2 * n_grp),
                  pl.BlockSpec((2 * ATT_PAIRS, None, 1, blk), lambda b, g, kb: (b * n_grp + g, kb, 0, 0)),
                  seq_spec(D_CONV // wide), seq_spec(0), seq_spec(0)],
        out_specs=[seq_spec(0), key_spec(0), key_spec(0), seq_spec(0), key_spec(0)],
        out_shape=[_hbm((t_tok, D_FOX), F32), _hbm((t_tok, D_FOX), BF16),
                   _hbm((t_tok, D_FOX), BF16), _hbm((t_tok, D_FOX), F32),
                   _hbm((t_tok, D_FOX), F32)],
        compiler_params=_params(56, ("arbitrary", "arbitrary", "arbitrary")),
    )(qkv, qkv, qkv, cum_t, cat, lse, d_o)


def _mix_out_fwd(cat, x, w_out, ln_g, ln_b, tm=512):
    t_tok = x.shape[0]

    def body(cat_ref, x_ref, w_ref, g_ref, b_ref, xn_ref, xnb_ref, z_ref):
        z = ALPHA * x_ref[...] + _dot(cat_ref[...], w_ref[...])
        xhat, _ = _ln_stats(z)
        xn = xhat * g_ref[...] + b_ref[...]
        z_ref[...] = z
        xn_ref[...] = xn
        xnb_ref[...] = xn.astype(BF16)

    def tok(n):
        return pl.BlockSpec((tm, n), lambda i: (i, 0))

    vec = pl.BlockSpec((1, D_MODEL), lambda i: (0, 0))
    return pl.pallas_call(
        body, name="mix_out_fwd", grid=(t_tok // tm,),
        in_specs=[tok(D_MODEL), tok(D_MODEL), pl.BlockSpec((D_MODEL, D_MODEL), lambda i: (0, 0)), vec, vec],
        out_specs=[tok(D_MODEL)] * 3,
        out_shape=[_hbm((t_tok, D_MODEL), F32), _hbm((t_tok, D_MODEL), BF16),
                   _hbm((t_tok, D_MODEL), F32)],
        compiler_params=_params(40, ("arbitrary",)),
    )(cat, x, w_out, ln_g, ln_b)


def _mix_out_bwd(dxn, z, w_out, ln_g, tm=512):
    t_tok = dxn.shape[0]

    def body(dxn_ref, z_ref, w_ref, g_ref, dz_ref, dzb_ref, dya_ref, dyb_ref, dyc_ref, dg_ref, db_ref):
        i = pl.program_id(0)
        dxn_t = dxn_ref[...]
        xhat, rstd = _ln_stats(z_ref[...])
        pg = jnp.sum(dxn_t * xhat, axis=0, keepdims=True)
        pb = jnp.sum(dxn_t, axis=0, keepdims=True)

        @pl.when(i == 0)
        def _():
            dg_ref[...] = pg
            db_ref[...] = pb

        @pl.when(i > 0)
        def _():
            dg_ref[...] += pg
            db_ref[...] += pb

        dz = _ln_bwd(dxn_t, xhat, rstd, g_ref[...])
        dzb = dz.astype(BF16)
        dz_ref[...] = dz
        dzb_ref[...] = dzb
        dya_ref[...] = _dot_nt(dzb, w_ref[0:256, :])
        dyb_ref[...] = _dot_nt(dzb, w_ref[256:768, :]).astype(BF16)
        dyc_ref[...] = _dot_nt(dzb, w_ref[768:1024, :])

    def tok(n):
        return pl.BlockSpec((tm, n), lambda i: (i, 0))

    vec = pl.BlockSpec((1, D_MODEL), lambda i: (0, 0))
    return pl.pallas_call(
        body, name="mix_out_bwd", grid=(t_tok // tm,),
        in_specs=[tok(D_MODEL), tok(D_MODEL), pl.BlockSpec((D_MODEL, D_MODEL), lambda i: (0, 0)), vec],
        out_specs=[tok(D_MODEL), tok(D_MODEL), tok(256), tok(512), tok(256), vec, vec],
        out_shape=[_hbm((t_tok, D_MODEL), F32), _hbm((t_tok, D_MODEL), BF16),
                   _hbm((t_tok, 256), F32), _hbm((t_tok, 512), BF16),
                   _hbm((t_tok, 256), F32),
                   _hbm((1, D_MODEL), F32), _hbm((1, D_MODEL), F32)],
        compiler_params=_params(40, ("arbitrary",)),
    )(dxn, z, w_out, ln_g)


def _conv_bwd(conv, dya, conv_w, n_seq):
    t_tok = conv.shape[0]
    seq = t_tok // n_seq

    def body(conv_ref, dya_ref, cw_ref, dconv_ref, dcw_ref):
        @pl.when(pl.program_id(0) == 0)
        def _():
            dcw_ref[...] = jnp.zeros_like(dcw_ref)

        z = conv_ref[:, 256:512] * conv_ref[:, 512:768]
        z1 = _shift_down(z, 1)
        z2 = _shift_down(z, 2)
        y = cw_ref[0:1, :] * z2 + cw_ref[1:2, :] * z1 + cw_ref[2:3, :] * z
        dya_t = dya_ref[...]
        dconv_ref[:, 0:256] = (dya_t * y).astype(BF16)
        dy = dya_t * conv_ref[:, 0:256]
        dcw_ref[0:1, :] += jnp.sum(dy * z2, axis=0, keepdims=True)
        dcw_ref[1:2, :] += jnp.sum(dy * z1, axis=0, keepdims=True)
        dcw_ref[2:3, :] += jnp.sum(dy * z, axis=0, keepdims=True)
        dz = cw_ref[2:3, :] * dy + cw_ref[1:2, :] * _shift_up(dy, 1) + cw_ref[0:1, :] * _shift_up(dy, 2)
        dconv_ref[:, 256:512] = (dz * conv_ref[:, 512:768]).astype(BF16)
        dconv_ref[:, 512:768] = (dz * conv_ref[:, 256:512]).astype(BF16)

    def seq_blk(n):
        return pl.BlockSpec((seq, n), lambda b: (b, 0))

    par = pl.BlockSpec((8, 256), lambda b: (0, 0))
    return pl.pallas_call(
        body, name="conv_bwd", grid=(n_seq,),
        in_specs=[seq_blk(768), seq_blk(256), par], out_specs=[seq_blk(768), par],
        out_shape=[_hbm((t_tok, 768), BF16), _hbm((8, 256), F32)],
        compiler_params=_params(56, ("arbitrary",)),
    )(conv, dya, conv_w)


def _sgu_gate_bwd(sgu, f, dyc, drow, dcol, b_f, sgu_g, sgu_b, w_s, b_mat, n_seq):
    t_tok = sgu.shape[0]
    seq = t_tok // n_seq
    n_chunk = seq // SGU_CHUNK

    def body(sgu_ref, f_ref, dyc_ref, drow_ref, dcol_ref, bf_ref, lg_ref, lb_ref, ws_ref, bm_ref,
             dsgu_ref, df_ref, dbf_ref, dlg_ref, dlb_ref, dws_ref, dbs_ref, dbm_acc):
        b = pl.program_id(0)

        @pl.when(b == 0)
        def _():
            for r in (dbf_ref, dlg_ref, dlb_ref, dws_ref, dbm_acc):
                r[...] = jnp.zeros_like(r)

        tril = _tril(SGU_CHUNK)
        grp = _sgu_group_of_lane()
        wc = [jnp.where(tril, ws_ref[g], 0.0).astype(BF16) for g in range(N_SGU_GROUPS)]
        for n in range(n_chunk):
            rows = pl.ds(n * SGU_CHUNK, SGU_CHUNK)
            su = sgu_ref[rows, 0:256]
            sv = sgu_ref[rows, 256:512]
            u, du = _gelu_with_grad(su)
            gv, dgv = _gelu_with_grad(sv)
            vhat, rstd = _ln_stats(gv)
            vn = (vhat * lg_ref[...] + lb_ref[...]).astype(BF16)
            mixed = bm_ref[...]
            for g in range(N_SGU_GROUPS):
                mixed = mixed + jnp.where(grp == g, _dot(wc[g], vn), 0.0)
            dyc_t = dyc_ref[rows, :]
            dsgu_ref[rows, 0:256] = (dyc_t * mixed * du).astype(BF16)
            dmixed = dyc_t * u
            dbm_acc[...] += dmixed
            dvn = jnp.zeros((SGU_CHUNK, D_SGU), F32)
            for g in range(N_SGU_GROUPS):
                dm_g = jnp.where(grp == g, dmixed, 0.0).astype(BF16)
                dws_ref[g] += _dot_nt(dm_g, vn)
                dvn = dvn + _dot_tn(wc[g], dm_g)
            dlg_ref[...] += jnp.sum(dvn * vhat, axis=0, keepdims=True)
            dlb_ref[...] += jnp.sum(dvn, axis=0, keepdims=True)
            dsgu_ref[rows, 256:512] = (_ln_bwd(dvn, vhat, rstd, lg_ref[...]) * dgv).astype(BF16)

        later = (lax.broadcasted_iota(jnp.int32, (128, 128), 0) <= lax.broadcasted_iota(jnp.int32, (128, 128), 1)).astype(F32)
        head = lax.broadcasted_iota(jnp.int32, (D_FOX, 128), 1)
        pick = (lax.broadcasted_iota(jnp.int32, (D_FOX, 128), 0) == 128 * (head // 2) + 64 * (1 - head % 2)).astype(F32)
        carry = jnp.zeros((1, 128), F32)
        for n in reversed(range(n_chunk)):
            rows = pl.ds(n * SGU_CHUNK, SGU_CHUNK)
            dcum_n = _dot(drow_ref[rows, :] - dcol_ref[rows, :], pick, HIGHEST)
            dlf = _dot(later, dcum_n, HIGHEST) + carry
            carry = carry + jnp.sum(dcum_n, axis=0, keepdims=True)
            df = dlf * jax.nn.sigmoid(-(f_ref[rows, :] + bf_ref[...]))
            df_ref[rows, :] = df.astype(BF16)
            dbf_ref[...] += jnp.sum(df, axis=0, keepdims=True)

        @pl.when(b == n_seq - 1)
        def _():
            for g in range(N_SGU_GROUPS):
                dws_ref[g] = jnp.where(tril, dws_ref[g], 0.0)
            sel = (lax.broadcasted_iota(jnp.int32, (D_SGU, 128), 0) // (D_SGU // N_SGU_GROUPS)
                   == lax.broadcasted_iota(jnp.int32, (D_SGU, 128), 1)).astype(F32)
            dbs_ref[...] = _dot(dbm_acc[...], sel, HIGHEST)

    def seq_blk(n):
        return pl.BlockSpec((seq, n), lambda b: (b, 0))

    def full(shape):
        return pl.BlockSpec(shape, lambda b: (0,) * len(shape))

    param_shapes = [(1, 128), (1, 256), (1, 256), (4, 128, 128), (128, 128)]
    return pl.pallas_call(
        body, name="sgu_gate_bwd", grid=(n_seq,),
        in_specs=[seq_blk(512), seq_blk(128), seq_blk(256), seq_blk(D_FOX), seq_blk(D_FOX),
                  full((1, 128)), full((1, 256)), full((1, 256)), full((4, 128, 128)), full((128, 256))],
        out_specs=[seq_blk(512), seq_blk(128)] + [full(s) for s in param_shapes],
        out_shape=[_hbm((t_tok, 512), BF16), _hbm((t_tok, 128), BF16)]
        + [_hbm(s, F32) for s in param_shapes],
        scratch_shapes=[pltpu.VMEM((128, 256), F32)],
        compiler_params=_params(48, ("arbitrary",)),
    )(sgu, f, dyc, drow, dcol, b_f, sgu_g, sgu_b, w_s, b_mat)


def _mix_in_bwd(dconv, dq, dk, dv, dsgu, df, dz, w_in, tm=512):
    t_tok = dz.shape[0]

    def body(dconv_ref, dq_ref, dk_ref, dv_ref, dsgu_ref, df_ref, dz_ref, w_ref, dx_ref, dp_ref):
        dqb = dq_ref[...].astype(BF16)
        pieces = [(COL_CONV, dconv_ref[...]), (COL_QKV, dqb), (COL_QKV + 512, dk_ref[...]), (COL_QKV + 1024, dv_ref[...]),
                  (COL_SGU, dsgu_ref[...]), (COL_F, df_ref[...])]
        dx = ALPHA * dz_ref[...]
        for col, val in pieces:
            width = val.shape[1]
            dp_ref[:, col:col + width] = val
            dx = dx + _dot_nt(val, w_ref[:, col:col + width])
        dx_ref[...] = dx

    def tok(n):
        return pl.BlockSpec((tm, n), lambda i: (i, 0))

    return pl.pallas_call(
        body, name="mix_in_bwd", grid=(t_tok // tm,),
        in_specs=[tok(768), tok(512), tok(512), tok(512), tok(512), tok(128), tok(D_MODEL),
                  pl.BlockSpec((D_MODEL, D_IN_PAD), lambda i: (0, 0))],
        out_specs=[tok(D_MODEL), tok(D_IN_PAD)],
        out_shape=[_hbm((t_tok, D_MODEL), F32), _hbm((t_tok, D_IN_PAD), BF16)],
        compiler_params=_params(48, ("arbitrary",)),
    )(dconv, dq, dk, dv, dsgu, df, dz, w_in)


def _pad_rows(a, rows):
    return jnp.pad(a, ((0, rows - a.shape[0]), (0, 0)))


F_BLOCK = F_ORIG // D_IN_SHARD
F_AT = F_ORIG - F_BLOCK * D_IN_SHARD
assert (F_ORIG + N_HEADS) // D_IN_SHARD == F_BLOCK


def _w_in_from_blocks(g):
    fb = g[F_BLOCK]
    zeros = jnp.zeros((D_MODEL, D_IN_PAD - COL_F - N_HEADS), g.dtype)
    return jnp.concatenate([g[d] for d in range(F_BLOCK)] + [fb[:, :F_AT], fb[:, F_AT + N_HEADS:]]
                           + [g[d] for d in range(F_BLOCK + 1, N_DEV)] + [fb[:, F_AT:F_AT + N_HEADS], zeros], axis=1)


def _w_in_to_blocks(dw):
    def cols(lo, hi):
        shift = 0 if hi <= F_ORIG else N_HEADS
        return dw[:, lo - shift:hi - shift]

    blocks = []
    for d in range(N_DEV):
        lo, hi = d * D_IN_SHARD, (d + 1) * D_IN_SHARD
        if d == F_BLOCK:
            blocks.append(jnp.concatenate([cols(lo, F_ORIG), dw[:, COL_F:COL_F + N_HEADS], cols(F_ORIG + N_HEADS, hi)], axis=1))
        else:
            blocks.append(cols(lo, hi))
    return jnp.stack(blocks)


LN1_ROWS = 2 * 8
REST_ROWS = 4 * 8 + 2 * 8 + 512 + 8 + 8 + 8


def _pack_rest(p):
    rows = [p[name].reshape(8, 128) for name in ("ln2_g", "ln2_b", "ln3_g", "ln3_b")]
    rows += [_pad_rows(p[name].reshape(2, 128), 8) for name in ("sgu_ln_g", "sgu_ln_b")]
    rows += [p["sgu_w_s"].reshape(512, 128), _pad_rows(p["sgu_b_s"], 8),
             _pad_rows(jnp.pad(p["fox_b_f"], (0, 128 - N_HEADS)).reshape(1, 128), 8), _pad_rows(p["conv_w"].reshape(6, 128), 8)]
    return jnp.concatenate(rows, axis=0)


def _pack_layer(p):
    return jnp.concatenate([p["ln1_g"].reshape(8, 128), p["ln1_b"].reshape(8, 128), _pack_rest(p)], axis=0)


def _unpack_layer(a):
    r = 0

    def take(n, valid):
        nonlocal r
        piece = a[r:r + valid]
        r += n
        return piece

    d = {}
    for name in ("ln1_g", "ln1_b", "ln2_g", "ln2_b", "ln3_g", "ln3_b"):
        d[name] = take(8, 8).reshape(D_MODEL)
    for name in ("sgu_ln_g", "sgu_ln_b"):
        d[name] = take(8, 2).reshape(D_SGU)
    d["sgu_w_s"] = take(512, 512).reshape(N_SGU_GROUPS, SGU_CHUNK, SGU_CHUNK)
    d["sgu_b_s"] = take(8, 4).reshape(N_SGU_GROUPS, SGU_CHUNK)
    d["fox_b_f"] = take(8, 1).reshape(128)[:N_HEADS]
    d["conv_w"] = take(8, 6).reshape(3, D_CONV)
    return d


SMALL_NAMES = ("ln1_g", "ln1_b", "fox_b_f", "sgu_ln_g", "sgu_ln_b", "sgu_w_s", "sgu_b_s", "ln2_g", "ln2_b", "ln3_g", "ln3_b")
BIG_NAMES = ("ffn1_w_up", "ffn1_w_down", "mix_w_in", "mix_w_out", "ffn2_w_up", "ffn2_w_down")
UP_NAMES = ("ffn1_w_up", "ffn2_w_up")
WEIGHT_ORDER = ("ln1_g", "ln1_b", "ffn1_w_up", "ffn1_w_down", "mix_w_in", "fox_b_f", "conv_w", "sgu_ln_g", "sgu_ln_b",
                "sgu_w_s", "sgu_b_s", "mix_w_out", "ln2_g", "ln2_b", "ffn2_w_up", "ffn2_w_down", "ln3_g", "ln3_b")


class _Overlap:
    def __init__(self, w, after, me, where):
        self.me, self.where = me, where
        self.last = after
        groups = [[("ffn1_w_up", 0), ("ffn1_w_down", 0)],
                  [("mix_w_in", 0), ("mix_w_out", 0), ("ffn2_w_up", 0), ("ffn2_w_down", 0)]]
        groups += [[(name, l) for name in BIG_NAMES] for l in range(1, DEPTH)]
        self.gathers = []
        for gi, group in enumerate(groups):
            shards = [w[name][l].astype(BF16) for name, l in group]
            lands = [lax.dynamic_update_slice(lax.empty((N_DEV,) + s.shape, BF16), s[None], (me, 0, 0)) for s in shards]
            started = self._start(f"allgather_start_{gi}", _gather_plan(len(group)), 3 * len(group), shards + lands)
            self.gathers.append(dict(group=group, chips=started))
            if gi == 0:
                width = D_CONV // N_DEV
                rows = jnp.pad(_pad_rows(w["conv_w"].reshape(DEPTH * 3, width), 8), ((0, 0), (0, 128 - width)))
                land = lax.dynamic_update_slice(lax.empty((N_DEV,) + rows.shape, F32), rows[None], (me, 0, 0))
                self.conv_started = self._start("conv_w_start", _peers_plan(), N_DEV - 1, [rows, land])
                self.conv_full = None
        self.all_started = self.last
        self.scatters = {}
        self.order = []
        self.small = []

    def conv_w(self, after):
        if self.conv_full is None:
            width = D_CONV // N_DEV
            gathered = _exchange_wait("conv_w_wait", _peers_plan(), N_DEV - 1, self.conv_started, after)[1]
            self.conv_full = jnp.transpose(gathered[:, :DEPTH * 3, :width], (1, 0, 2)).reshape(DEPTH, 3, D_CONV)
        return self.conv_full

    def _start(self, name, plan, n_copies, arrays):
        started = _exchange_start(name, plan, n_copies, arrays, self.last)
        self.last = started[3]
        return started

    def _group_of(self, layer, part):
        return layer + 1 if layer > 0 else (0 if part == "ffn1" else 1)

    def pass_on(self, layer, part, after):
        gi = self._group_of(layer, part)
        st = self.gathers[gi]
        if "sibling" not in st:
            m = len(st["group"])
            arrays = _exchange_wait(f"allgather_wait_{gi}", _gather_plan(m), 3 * m, st["chips"], after)
            first = [i for i, (name, _) in enumerate(st["group"]) if name == "mix_w_in"] if gi == 1 else []
            st["passes"] = [idx for idx in (first, [i for i in range(m) if i not in first]) if idx]
            st["sibling"] = [self._start(f"allgather_pass_start_{gi}_{k}", _pass_on_plan(len(idx)), 4 * len(idx),
                                         [arrays[i] for i in idx] + [arrays[m + i] for i in idx])
                             for k, idx in enumerate(st["passes"])]
            st["full"] = {}
        return st["sibling"][-1][3]

    def weights(self, layer, part, after):
        names = {"ffn1": ("ffn1_w_up", "ffn1_w_down"), "in": ("mix_w_in",),
                 "rest": ("mix_w_out", "ffn2_w_up", "ffn2_w_down")}[part]
        gi = self._group_of(layer, part)
        st = self.gathers[gi]
        after = self.all_started if after is None else after
        self.pass_on(layer, part, after)
        g = st["full"]
        if (names[0], layer) not in g:
            for k, idx in enumerate(st["passes"]):
                if st["group"].index((names[0], layer)) in idx:
                    arrays = _exchange_wait(f"allgather_pass_wait_{gi}_{k}", _pass_on_plan(len(idx)), 4 * len(idx),
                                            st["sibling"][k], after)
                    g.update(zip([st["group"][i] for i in idx], arrays[len(idx):]))

        def ffn(n):
            return g[(f"ffn{n}_w_up", layer)].reshape(2, D_FF, D_MODEL), g[(f"ffn{n}_w_down", layer)].reshape(D_FF, D_MODEL)

        if part == "ffn1":
            return ffn(1)
        if part == "in":
            return _w_in_from_blocks(g[("mix_w_in", layer)])
        return (g[("mix_w_out", layer)].reshape(D_MODEL, D_MODEL), *ffn(2))

    def push(self, key, items):
        n = len(items)
        grads = [g for _, _, g in items]
        lands = [lax.empty((4,) + g.shape[1:], F32) for g in grads]
        started = self._start(f"rs_sibling_start_{key[0]}{key[1]}", _sibling_plan(n), 4 * n, grads + lands)
        self.scatters[key] = dict(items=items, sibling=started)
        self.order.append(key)
        return started[3]

    def advance(self, key, after):
        st = self.scatters[key]
        n = len(st["items"])
        arrays = _exchange_wait(f"rs_sibling_wait_{key[0]}{key[1]}", _sibling_plan(n), 4 * n, st["sibling"], after)
        partials = [_chip_partial(g, r, self.where) for g, r in zip(arrays[:n], arrays[n:])]
        p16 = [p for _, p in partials]
        lands = [lax.empty((3,) + p.shape[1:], BF16) for p in p16]
        started = self._start(f"rs_chip_start_{key[0]}{key[1]}", _chip_plan(n), 3 * n, p16 + lands)
        st.update(own32=[p for p, _ in partials], chip=started)
        return started[3]

    def push_small(self, rows):
        k = len(self.small)
        land = lax.dynamic_update_slice(lax.empty((N_DEV,) + rows.shape, F32), rows[None], (self.me, 0, 0))
        started = self._start(f"small_start_{k}", _peers_plan(), N_DEV - 1, [rows, land])
        self.small.append(started)
        return started[3]

    def finish(self, w, m, v):
        res = {}
        after = self.scatters[self.order[-1]]["chip"][3]
        for key in self.order:
            st = self.scatters[key]
            n = len(st["items"])
            arrays = _exchange_wait(f"rs_chip_wait_{key[0]}{key[1]}", _chip_plan(n), 3 * n, st["chip"], after)
            for (name, l, _), own32, r16 in zip(st["items"], st["own32"], arrays[n:]):
                res[name] = _adamw_shard(own32, r16, w[name], m[name], v[name], l, res.get(name))
                after = res[name][0]
        pieces = [_exchange_wait(f"small_wait_{k}", _peers_plan(), N_DEV - 1, started, after)[1]
                  for k, started in enumerate(self.small)]
        return res, pieces


def _dw_up(dgu, x, after):
    return _matmul_tn(dgu, x[None], after, tk=2048, bm=DW_ROWS).reshape(N_DEV, FFN_BLK, D_MODEL)


def _dw_down(a, dy, after):
    return _matmul_tn(a[None], dy[None], after, tk=2048, bm=DW_ROWS).reshape(N_DEV, FFN_BLK // 2, D_MODEL)


def _local_step(x, target, comm, small, n_seq):
    def vec(a):
        return a.reshape(1, -1)

    saved = []
    h = x
    for l in range(DEPTH):
        s = {}
        s["up1"], s["down1"] = comm.weights(l, "ffn1", None if l == 0 else h)
        h1, h1b, s["z1"], s["gu1"], s["x0b"] = _ffn_fwd(h, s["up1"], s["down1"], vec(small["ln1_g"][l]), vec(small["ln1_b"][l]), h)
        s["w_in"] = comm.weights(l, "in", s["z1"])
        s["x1b"] = h1b
        conv, qkv, sgu, f = _in_proj(h1, s["w_in"])
        s["w_out"], s["up2"], s["down2"] = comm.weights(l, "rest", f)
        cw = _pad_rows(comm.conv_w(h1)[l], 8)
        bf = jnp.pad(small["fox_b_f"][l], (0, 128 - N_HEADS)).reshape(1, 128)
        b_mat = jnp.repeat(small["sgu_b_s"][l].T, D_SGU // N_SGU_GROUPS, axis=1)
        mid_params = (cw, bf, vec(small["sgu_ln_g"][l]), vec(small["sgu_ln_b"][l]), small["sgu_w_s"][l], b_mat)
        cat, cum_t = _mix_mid_fwd(conv, sgu, f, *mid_params, n_seq)
        cat, lse = _fox_fwd(qkv, cum_t, cat, n_seq)
        h2, h2b, s["z2"] = _mix_out_fwd(cat, h1, s["w_out"], vec(small["ln2_g"][l]), vec(small["ln2_b"][l]))
        s.update(conv=conv, qkv=qkv, sgu=sgu, f=f, mid_params=mid_params, cat=cat, cum_t=cum_t, lse=lse, x2b=h2b)
        token = comm.pass_on(l + 1, "ffn1", s["z2"]) if l + 1 < DEPTH else h2
        ln3 = (vec(small["ln3_g"][l]), vec(small["ln3_b"][l]))
        if l + 1 < DEPTH:
            h, _, s["z3"], s["gu2"], _ = _ffn_fwd(h2, s["up2"], s["down2"], *ln3, token)
        else:
            dh, loss, s["z3"], s["gu2"] = _ffn_fwd(h2, s["up2"], s["down2"], *ln3, token, target)
        saved.append(s)

    late_rows = None
    token = loss
    pending = None
    for l in reversed(range(DEPTH)):
        s = saved[l]
        sg = {}
        dh, dy, a, dgu, sg["ln3_g"], sg["ln3_b"] = _ffn_bwd(dh, s["z3"], s["gu2"], s["up2"], s["down2"], vec(small["ln3_g"][l]), token)
        if pending is not None:
            token = comm.advance(pending, dh)
        g_up2 = _dw_up(dgu, s["x2b"], token)
        g_down2 = _dw_down(a, dy, token)
        dz, dzb, dya, dyb, dyc, sg["ln2_g"], sg["ln2_b"] = _mix_out_bwd(dh, s["z2"], s["w_out"], vec(small["ln2_g"][l]))
        g_out = _matmul_tn(s["cat"][None], dzb[None], token, tk=2048, bm=D_MODEL // 2).reshape(N_DEV, D_MODEL // N_DEV, D_MODEL)
        dq, dk, dv, drow, dcol = _fox_bwd(s["qkv"], s["cum_t"], s["cat"], s["lse"], dyb, n_seq)
        dconv, dcw = _conv_bwd(s["conv"], dya, s["mid_params"][0], n_seq)
        dsgu, df, dbf, dlg, dlb, dws, dbs = _sgu_gate_bwd(s["sgu"], s["f"], dyc, drow, dcol, *s["mid_params"][1:], n_seq)
        sg.update(conv_w=dcw[:3], fox_b_f=dbf[0, :N_HEADS], sgu_ln_g=dlg[0], sgu_ln_b=dlb[0], sgu_w_s=dws,
                  sgu_b_s=dbs[:, :N_SGU_GROUPS].T)
        dh, dp = _mix_in_bwd(dconv, dq, dk, dv, dsgu, df, dz, s["w_in"])
        g_in = _w_in_to_blocks(_matmul_tn(s["x1b"][None], dp[None], token, tk=2048, bm=D_MODEL // 2)[0])
        first = [("ffn2_w_up", l, g_up2), ("ffn2_w_down", l, g_down2), ("mix_w_out", l, g_out), ("mix_w_in", l, g_in)]
        for name in ("ln2_g", "ln2_b", "ln3_g", "ln3_b"):
            sg[name] = sg[name][0]
        if l == 0:
            comm.push((l, "a"), first)
            token = comm.push_small(_pack_rest(sg))
            pending, first = (l, "a"), []
        dh, dy, a, dgu, dg1, db1 = _ffn_bwd(dh, s["z1"], s["gu1"], s["up1"], s["down1"], vec(small["ln1_g"][l]), token)
        if l == 0:
            token = comm.advance(pending, dh)
        g_up1 = _dw_up(dgu, s["x0b"], token)
        ln1_rows = jnp.concatenate([dg1.reshape(8, 128), db1.reshape(8, 128)], axis=0)
        if l == 0:
            token = comm.push((l, "b"), [("ffn1_w_up", l, g_up1)])
            g_down1 = _dw_down(a, dy, token)
            token = comm.advance((l, "b"), g_down1)
            token = comm.push((l, "c"), [("ffn1_w_down", l, g_down1)])
            token = comm.advance((l, "c"), token)
            late_rows = ln1_rows
        else:
            g_down1 = _dw_down(a, dy, token)
            pending = (l, "b")
            comm.push(pending, first + [("ffn1_w_up", l, g_up1), ("ffn1_w_down", l, g_down1)])
            token = comm.push_small(jnp.concatenate([ln1_rows, _pack_rest(sg)], axis=0))
    return loss, dh, late_rows


def kernel(x, ln1_g, ln1_b, ffn1_w_up, ffn1_w_down, mix_w_in, fox_b_f, conv_w, sgu_ln_g, sgu_ln_b, sgu_w_s, sgu_b_s, mix_w_out, ln2_g, ln2_b, ffn2_w_up, ffn2_w_down, ln3_g, ln3_b, loss_target, m_ln1_g, m_ln1_b, m_ffn1_w_up, m_ffn1_w_down, m_mix_w_in, m_fox_b_f, m_conv_w, m_sgu_ln_g, m_sgu_ln_b, m_sgu_w_s, m_sgu_b_s, m_mix_w_out, m_ln2_g, m_ln2_b, m_ffn2_w_up, m_ffn2_w_down, m_ln3_g, m_ln3_b, v_ln1_g, v_ln1_b, v_ffn1_w_up, v_ffn1_w_down, v_mix_w_in, v_fox_b_f, v_conv_w, v_sgu_ln_g, v_sgu_ln_b, v_sgu_w_s, v_sgu_b_s, v_mix_w_out, v_ln2_g, v_ln2_b, v_ffn2_w_up, v_ffn2_w_down, v_ln3_g, v_ln3_b):
    w = dict(ln1_g=ln1_g, ln1_b=ln1_b, ffn1_w_up=ffn1_w_up, ffn1_w_down=ffn1_w_down, mix_w_in=mix_w_in, fox_b_f=fox_b_f,
             conv_w=conv_w, sgu_ln_g=sgu_ln_g, sgu_ln_b=sgu_ln_b, sgu_w_s=sgu_w_s, sgu_b_s=sgu_b_s, mix_w_out=mix_w_out,
             ln2_g=ln2_g, ln2_b=ln2_b, ffn2_w_up=ffn2_w_up, ffn2_w_down=ffn2_w_down, ln3_g=ln3_g, ln3_b=ln3_b)
    m = dict(ln1_g=m_ln1_g, ln1_b=m_ln1_b, ffn1_w_up=m_ffn1_w_up, ffn1_w_down=m_ffn1_w_down, mix_w_in=m_mix_w_in,
             fox_b_f=m_fox_b_f, conv_w=m_conv_w, sgu_ln_g=m_sgu_ln_g, sgu_ln_b=m_sgu_ln_b, sgu_w_s=m_sgu_w_s,
             sgu_b_s=m_sgu_b_s, mix_w_out=m_mix_w_out, ln2_g=m_ln2_g, ln2_b=m_ln2_b, ffn2_w_up=m_ffn2_w_up,
             ffn2_w_down=m_ffn2_w_down, ln3_g=m_ln3_g, ln3_b=m_ln3_b)
    v = dict(ln1_g=v_ln1_g, ln1_b=v_ln1_b, ffn1_w_up=v_ffn1_w_up, ffn1_w_down=v_ffn1_w_down, mix_w_in=v_mix_w_in,
             fox_b_f=v_fox_b_f, conv_w=v_conv_w, sgu_ln_g=v_sgu_ln_g, sgu_ln_b=v_sgu_ln_b, sgu_w_s=v_sgu_w_s,
             sgu_b_s=v_sgu_b_s, mix_w_out=v_mix_w_out, ln2_g=v_ln2_g, ln2_b=v_ln2_b, ffn2_w_up=v_ffn2_w_up,
             ffn2_w_down=v_ffn2_w_down, ln3_g=v_ln3_g, ln3_b=v_ln3_b)

    mx, my, mc = lax.axis_index("x"), lax.axis_index("y"), lax.axis_index("c")
    me = 4 * mx + 2 * my + mc
    n_seq, seq, _ = x.shape
    t_tok = n_seq * seq
    for name in UP_NAMES:
        for t in (w, m, v):
            t[name] = jnp.transpose(t[name], (0, 2, 1))

    comm = _Overlap(w, x, me, jnp.stack([mc, 2 * mx + my]).astype(jnp.int32))
    small = {name: w[name] for name in SMALL_NAMES}

    loss_dev, grad_x, late_rows = _local_step(
        x.reshape(t_tok, D_MODEL), loss_target.reshape(t_tok, D_MODEL), comm, small, n_seq)
    loss = lax.psum(loss_dev[0, 0], ("x", "y", "c"))
    out, pieces = comm.finish(w, m, v)
    for name in UP_NAMES:
        out[name] = [jnp.transpose(a, (0, 2, 1)) for a in out[name]]

    pieces.append(_allgather_small(late_rows))
    spans = [(l, 0, LN1_ROWS + REST_ROWS) for l in reversed(range(1, DEPTH))] + [(0, LN1_ROWS, LN1_ROWS + REST_ROWS), (0, 0, LN1_ROWS)]

    def widen(a):
        return lax.dynamic_update_slice(jnp.zeros((3, D_CONV), F32), a, (0, me * (D_CONV // N_DEV)))

    packed = [[_pack_layer({**{name: t[name][l] for name in SMALL_NAMES}, "conv_w": widen(t["conv_w"][l])}) for l in range(DEPTH)]
              for t in (w, m, v)]
    rows_out = {}
    for (l, lo, hi), gathered_piece in zip(spans, pieces):
        rows_out[(l, lo)] = _adamw_small(gathered_piece, *[packed[t][l][lo:hi] for t in range(3)])
    per_layer = []
    for l in range(DEPTH):
        parts = sorted(lo for (ll, lo) in rows_out if ll == l)
        per_layer.append([_unpack_layer(jnp.concatenate([rows_out[(l, lo)][k] for lo in parts], axis=0)) for k in range(4)])
    for name in SMALL_NAMES:
        out[name] = [jnp.stack([per_layer[l][k][name] for l in range(DEPTH)]) for k in range(4)]
    lo_col = me * (D_CONV // N_DEV)
    out["conv_w"] = [jnp.stack([lax.dynamic_slice(per_layer[l][k]["conv_w"], (0, lo_col), (3, D_CONV // N_DEV)) for l in range(DEPTH)])
                     for k in range(4)]

    return (loss, grad_x.reshape(x.shape), *[out[name][0] for name in WEIGHT_ORDER], *[out[name][1] for name in WEIGHT_ORDER],
            *[out[name][2] for name in WEIGHT_ORDER], *[out[name][3] for name in WEIGHT_ORDER])
```

```python
import functools

import jax
import jax.numpy as jnp
from jax import lax
from jax.experimental import pallas as pl
from jax.experimental.pallas import tpu as pltpu

F32 = jnp.float32
BF16 = jnp.bfloat16
MESH = pl.DeviceIdType.MESH

N_DEV = 8
DEPTH = 2
D_MODEL = 1024
D_FF = 2816
FFN_BLK = 2 * D_FF // N_DEV
MXU_TILE_V7X = 256
FFN_CHUNKS = tuple((lo, min(lo + 3 * MXU_TILE_V7X, D_FF)) for lo in range(0, D_FF, 3 * MXU_TILE_V7X))
FFN_BWD_CHUNKS = tuple((lo, min(lo + 4 * MXU_TILE_V7X, D_FF)) for lo in range(0, D_FF, 4 * MXU_TILE_V7X))
DW_ROWS = D_FF // 2
D_CONV = 256
D_FOX = 512
N_HEADS = 8
D_SGU = 256
N_SGU_GROUPS = 4
SGU_CHUNK = 128
D_IN = 3 * D_CONV + 3 * D_FOX + N_HEADS + 2 * D_SGU
D_IN_SHARD = D_IN // N_DEV
COL_CONV, COL_QKV, COL_SGU, COL_F = 0, 768, 2304, 2816
D_IN_PAD = 2944
F_ORIG = 3 * D_CONV + 3 * D_FOX
ALPHA = (2 * DEPTH) ** 0.25
LN_EPS = 1e-5
ATT_SCALE = 0.125
ATT_BLK = 512
ATT_PAIRS = 2
NEG = -1e30

ADAM_LR, ADAM_B1, ADAM_B2, ADAM_EPS, ADAM_WD, ADAM_STEP = 0.001, 0.9, 0.999, 1e-08, 0.01, 10

VMEM_BYTES_V7X = 64 * 1024 * 1024
HIGHEST = lax.Precision.HIGHEST


def _params(vmem_mb, sem=None):
    assert vmem_mb * 1024 * 1024 < VMEM_BYTES_V7X
    kw = dict(vmem_limit_bytes=vmem_mb * 1024 * 1024)
    if sem is not None:
        kw["dimension_semantics"] = sem
    return pltpu.CompilerParams(**kw)


def _dot(a, b, precision=None):
    return lax.dot_general(a, b, (((1,), (0,)), ((), ())), preferred_element_type=F32, precision=precision)


def _dot_nt(a, b):
    return lax.dot_general(a, b, (((1,), (1,)), ((), ())), preferred_element_type=F32)


def _dot_tn(a, b):
    return lax.dot_general(a, b, (((0,), (0,)), ((), ())), preferred_element_type=F32)


def _ln_stats(z):
    mu = jnp.mean(z, axis=-1, keepdims=True)
    zc = z - mu
    var = jnp.mean(zc * zc, axis=-1, keepdims=True)
    rstd = lax.rsqrt(var + LN_EPS)
    return zc * rstd, rstd


def _ln_bwd(dy, xhat, rstd, g):
    dxh = dy * g
    m1 = jnp.mean(dxh, axis=-1, keepdims=True)
    m2 = jnp.mean(dxh * xhat, axis=-1, keepdims=True)
    return rstd * (dxh - m1 - xhat * m2)


_GELU_C = 0.7978845608028654


def _gelu(x):
    return 0.5 * x * (1.0 + jnp.tanh(_GELU_C * (x + 0.044715 * x * x * x)))


def _gelu_with_grad(x):
    t = jnp.tanh(_GELU_C * (x + 0.044715 * x * x * x))
    return 0.5 * x * (1.0 + t), 0.5 * (1.0 + t) + 0.5 * x * (1.0 - t * t) * _GELU_C * (1.0 + 3 * 0.044715 * x * x)


def _hbm(shape, dtype):
    n = 1
    for d in shape:
        n *= d
    if n * jnp.dtype(dtype).itemsize >= 1024 * 1024:
        return pltpu.HBM(tuple(shape), dtype)
    return jax.ShapeDtypeStruct(tuple(shape), dtype)


def _vspec():
    return pl.BlockSpec(memory_space=pltpu.VMEM)


def _anyspec():
    return pl.BlockSpec(memory_space=pl.ANY)


def _mesh_pos():
    return lax.axis_index("x"), lax.axis_index("y"), lax.axis_index("c")


def _other_chips(x, y):
    return [(1 - x, y), (x, 1 - y), (1 - x, 1 - y)]


_HBM_SPEC = pl.BlockSpec(memory_space=pltpu.HBM)
_SEM_SPEC = pl.BlockSpec(memory_space=pltpu.SEMAPHORE)
_DATAFLOW_EFFECT = pltpu.SideEffectType.DATAFLOW_SIDE_EFFECTING


def _remote_copies(plan, refs, send_sems, recv_sems):
    return [pltpu.make_async_remote_copy(src_ref=src, dst_ref=dst, send_sem=send_sems.at[k], recv_sem=recv_sems.at[k],
                                         device_id=to, device_id_type=MESH)
            for k, (src, dst, to) in enumerate(plan(refs, *_mesh_pos()))]


def _exchange_start(name, plan, n_copies, arrays, after):
    n = len(arrays)

    def body(*refs):
        send_sems, recv_sems, token = refs[n + 1], refs[n + 2], refs[-1]
        for cp in _remote_copies(plan, refs[:n], send_sems, recv_sems):
            cp.start()
        token[...] = jnp.zeros_like(token)

    out = pl.pallas_call(
        body, name=name,
        out_shape=(pltpu.SemaphoreType.DMA((n_copies,)), pltpu.SemaphoreType.DMA((n_copies,)),
                   *[pltpu.HBM(a.shape, a.dtype) for a in arrays], _hbm((8, 128), F32)),
        in_specs=[_HBM_SPEC] * n + [_anyspec()],
        out_specs=(_SEM_SPEC, _SEM_SPEC, *[_HBM_SPEC] * n, _vspec()),
        input_output_aliases={i: 2 + i for i in range(n)},
        compiler_params=pltpu.CompilerParams(has_side_effects=_DATAFLOW_EFFECT),
    )(*[pltpu.with_memory_space_constraint(a, pltpu.HBM) for a in arrays], after)
    return out[0], out[1], list(out[2:2 + n]), out[-1]


def _exchange_wait(name, plan, n_copies, started, after):
    send_sems, recv_sems, arrays, _ = started
    n = len(arrays)

    def body(*refs):
        for cp in _remote_copies(plan, refs[:n], refs[n], refs[n + 1]):
            cp.wait_send()
            cp.wait_recv()

    out = pl.pallas_call(
        body, name=name,
        out_shape=tuple(pltpu.HBM(a.shape, a.dtype) for a in arrays),
        in_specs=[_HBM_SPEC] * n + [_SEM_SPEC, _SEM_SPEC, _anyspec()], out_specs=tuple([_HBM_SPEC] * n),
        input_output_aliases={i: i for i in range(n)},
        compiler_params=pltpu.CompilerParams(has_side_effects=_DATAFLOW_EFFECT),
    )(*arrays, send_sems, recv_sems, after)
    return list(out)


def _gather_plan(m):
    def plan(refs, x, y, c):
        me = 4 * x + 2 * y + c
        return [(refs[i], refs[m + i].at[me], (*chip, c)) for i in range(m) for chip in _other_chips(x, y)]
    return plan


def _pass_on_plan(m):
    def plan(refs, x, y, c):
        out = []
        for i in range(m):
            out.append((refs[i], refs[m + i].at[4 * x + 2 * y + c], (x, y, 1 - c)))
            for cx, cy in _other_chips(x, y):
                block = refs[m + i].at[4 * cx + 2 * cy + c]
                out.append((block, block, (x, y, 1 - c)))
        return out
    return plan


def _peers_plan():
    def plan(refs, x, y, c):
        rel = [(dx, dy, dc) for dx in (0, 1) for dy in (0, 1) for dc in (0, 1)][1:]
        return [(refs[0], refs[1].at[4 * x + 2 * y + c], (x ^ dx, y ^ dy, c ^ dc)) for dx, dy, dc in rel]
    return plan


def _allgather_small(v):
    rows = v.shape[0]

    def body(v_ref, out_ref, send_sems, recv_sems):
        x, y, c = _mesh_pos()
        me = 4 * x + 2 * y + c
        out_ref[me] = v_ref[...]
        rel = [(dx, dy, dc) for dx in (0, 1) for dy in (0, 1) for dc in (0, 1)][1:]
        copies = []
        for k, (dx, dy, dc) in enumerate(rel):
            to = (x ^ dx, y ^ dy, c ^ dc)
            copies.append(pltpu.make_async_remote_copy(
                src_ref=v_ref, dst_ref=out_ref.at[me], send_sem=send_sems.at[k], recv_sem=recv_sems.at[k],
                device_id=to, device_id_type=MESH))
        for cp in copies:
            cp.start()
        for k, (dx, dy, dc) in enumerate(rel):
            src_blk = 4 * (x ^ dx) + 2 * (y ^ dy) + (c ^ dc)
            pltpu.make_async_remote_copy(
                src_ref=v_ref, dst_ref=out_ref.at[src_blk], send_sem=send_sems.at[k], recv_sem=recv_sems.at[k],
                device_id=(x, y, c), device_id_type=MESH).wait_recv()
        for cp in copies:
            cp.wait_send()

    return pl.pallas_call(
        body, name="allgather_small",
        out_shape=jax.ShapeDtypeStruct((N_DEV, rows, 128), v.dtype),
        in_specs=[_vspec()], out_specs=_vspec(),
        scratch_shapes=[pltpu.SemaphoreType.DMA((7,)), pltpu.SemaphoreType.DMA((7,))],
        compiler_params=_params(24),
    )(v)


def _sibling_plan(n):
    def plan(refs, x, y, c):
        return [(refs[a].at[2 * q + (1 - c)], refs[n + a].at[q], (x, y, 1 - c)) for a in range(n) for q in range(4)]
    return plan


def _chip_plan(n):
    def plan(refs, x, y, c):
        return [(refs[a].at[2 * cx + cy], refs[n + a].at[j], (cx, cy, c))
                for a in range(n) for j, (cx, cy) in enumerate(_other_chips(x, y))]
    return plan


def _row_tile(rows, cols, budget_bytes=2 * 1024 * 1024):
    best = 8
    for t in range(8, rows + 1, 8):
        if rows % t == 0 and t * cols * 4 <= budget_bytes:
            best = t
    return best


def _chip_partial(g, recv, where):
    _, rows, cols = g.shape
    tr = _row_tile(rows, cols)

    def body(where_ref, g_ref, r_ref, own_ref, o16_ref):
        s = g_ref[...] + r_ref[...]
        o16_ref[...] = s.astype(BF16)

        @pl.when(pl.program_id(1) == where_ref[1])
        def _():
            own_ref[...] = s

    blk = (None, tr, cols)
    return pl.pallas_call(
        body, name="rs_chip_partial",
        grid_spec=pltpu.PrefetchScalarGridSpec(
            num_scalar_prefetch=1, grid=(rows // tr, 4),
            in_specs=[pl.BlockSpec(blk, lambda i, q, w: (2 * q + w[0], i, 0)),
                      pl.BlockSpec(blk, lambda i, q, w: (q, i, 0))],
            out_specs=[pl.BlockSpec((tr, cols), lambda i, q, w: (i, 0)), pl.BlockSpec(blk, lambda i, q, w: (q, i, 0))]),
        out_shape=[_hbm((rows, cols), F32), _hbm((4, rows, cols), BF16)],
        compiler_params=_params(32),
    )(where, g, recv)


def _adam_math(w, g, m, v):
    m = ADAM_B1 * m + (1.0 - ADAM_B1) * g
    v = ADAM_B2 * v + (1.0 - ADAM_B2) * (g * g)
    m_hat = m / (1.0 - ADAM_B1 ** ADAM_STEP)
    v_hat = v / (1.0 - ADAM_B2 ** ADAM_STEP)
    delta = -ADAM_LR * (m_hat / (jnp.sqrt(v_hat) + ADAM_EPS) + ADAM_WD * w)
    return delta, m, v


def _adamw_shard(own32, recv16, w, m, v, layer, earlier):
    depth, rows, cols = w.shape
    tr = _row_tile(rows, cols, 1024 * 1024)
    n_prev = 0 if earlier is None else 4

    def body(p_ref, r_ref, w_ref, m_ref, v_ref, *rest):
        g_out, d_out, m_out, v_out = rest[n_prev:]
        g = p_ref[...] + r_ref[0].astype(F32) + r_ref[1].astype(F32) + r_ref[2].astype(F32)
        d, mn, vn = _adam_math(w_ref[...], g, m_ref[...], v_ref[...])
        g_out[...] = g
        d_out[...] = d
        m_out[...] = mn
        v_out[...] = vn

    mine = pl.BlockSpec((None, tr, cols), lambda i: (layer, i, 0))
    return pl.pallas_call(
        body, name="adamw_shard", grid=(rows // tr,),
        in_specs=[pl.BlockSpec((tr, cols), lambda i: (i, 0)), pl.BlockSpec((3, tr, cols), lambda i: (0, i, 0)),
                  mine, mine, mine] + [_anyspec()] * n_prev,
        out_specs=[mine] * 4,
        out_shape=[_hbm((depth, rows, cols), F32)] * 4,
        input_output_aliases={5 + k: k for k in range(n_prev)},
        compiler_params=_params(32),
    )(own32, recv16, *[pltpu.with_memory_space_constraint(t, pltpu.HBM) for t in (w, m, v)],
      *([] if earlier is None else earlier))


def _adamw_small(gathered, w, m, v):
    rows = w.shape[0]

    def body(a_ref, w_ref, m_ref, v_ref, g_out, d_out, m_out, v_out):
        g = a_ref[0]
        for d in range(1, N_DEV):
            g = g + a_ref[d]
        dl, mn, vn = _adam_math(w_ref[...], g, m_ref[...], v_ref[...])
        g_out[...] = g
        d_out[...] = dl
        m_out[...] = mn
        v_out[...] = vn

    return pl.pallas_call(
        body, name="adamw_small",
        in_specs=[_vspec()] * 4, out_specs=[_vspec()] * 4,
        out_shape=[_hbm((rows, 128), F32)] * 4,
        compiler_params=_params(32),
    )(gathered, w, m, v)


def _load_weights_once(pairs, sems):
    @pl.when(pl.program_id(0) == 0)
    def _():
        cps = [pltpu.make_async_copy(src, dst, sems.at[i]) for i, (src, dst) in enumerate(pairs)]
        for cp in cps:
            cp.start()
        for cp in cps:
            cp.wait()


def _ffn_fwd(x, wup, wd, ln_g, ln_b, after, target=None, tm=512):
    t_tok = x.shape[0]
    last = target is not None

    def body(x_ref, g_ref, b_ref, wup_hbm, wd_hbm, _after, *rest):
        pl.when(pl.program_id(0) == 0)(functools.partial(step, True, x_ref, g_ref, b_ref, wup_hbm, wd_hbm, *rest))
        pl.when(pl.program_id(0) > 0)(functools.partial(step, False, x_ref, g_ref, b_ref, wup_hbm, wd_hbm, *rest))

    def step(first, x_ref, g_ref, b_ref, wup_hbm, wd_hbm, *rest):
        if last:
            t_ref, dxn_ref, loss_ref, z_ref, gu_ref, wup_v, wd_v, sems = rest
        else:
            xn_ref, xnb_ref, z_ref, gu_ref, xb_ref, wup_v, wd_v, sems = rest
        if first:
            copies = [(pltpu.make_async_copy(wup_hbm.at[:, lo:hi], wup_v.at[:, lo:hi], sems.at[2 * k]),
                       pltpu.make_async_copy(wd_hbm.at[lo:hi], wd_v.at[lo:hi], sems.at[2 * k + 1]))
                      for k, (lo, hi) in enumerate(FFN_CHUNKS)]
            for pair in copies:
                for cp in pair:
                    cp.start()
        xb = x_ref[...].astype(BF16)
        if not last:
            xb_ref[...] = xb
        y = None
        for k, (lo, hi) in enumerate(FFN_CHUNKS):
            if first:
                for cp in copies[k]:
                    cp.wait()
            g = _dot_nt(xb, wup_v[0, lo:hi])
            u = _dot_nt(xb, wup_v[1, lo:hi])
            gu_ref[0, :, lo:hi] = g.astype(BF16)
            gu_ref[1, :, lo:hi] = u.astype(BF16)
            a = (g * jax.nn.sigmoid(g) * u).astype(BF16)
            part = _dot(a, wd_v[lo:hi])
            y = part if y is None else y + part
        z = ALPHA * x_ref[...] + 0.5 * y
        xhat, _ = _ln_stats(z)
        xn = xhat * g_ref[...] + b_ref[...]
        z_ref[...] = z
        if last:
            err = xn - t_ref[...]
            dxn_ref[...] = err * (1.0 / D_MODEL)
            part = jnp.sum(jnp.sum(err * err, axis=1, keepdims=True), axis=0, keepdims=True) * (0.5 / D_MODEL)

            if first:
                loss_ref[...] = jnp.broadcast_to(part, loss_ref.shape)
            else:
                loss_ref[...] += part
        else:
            xn_ref[...] = xn
            xnb_ref[...] = xn.astype(BF16)

    tok = pl.BlockSpec((tm, D_MODEL), lambda i: (i, 0))
    vec = pl.BlockSpec((1, D_MODEL), lambda i: (0, 0))
    gu_spec = pl.BlockSpec((2, tm, D_FF), lambda i: (0, i, 0))
    gu_shape = _hbm((2, t_tok, D_FF), BF16)
    f32_tok, bf16_tok = _hbm((t_tok, D_MODEL), F32), _hbm((t_tok, D_MODEL), BF16)
    if last:
        extra_in, extra_spec = [target], [tok]
        out_specs = [tok, pl.BlockSpec((1, 128), lambda i: (0, 0)), tok, gu_spec]
        out_shape = [f32_tok, _hbm((1, 128), F32), f32_tok, gu_shape]
    else:
        extra_in, extra_spec = [], []
        out_specs = [tok, tok, tok, gu_spec, tok]
        out_shape = [f32_tok, bf16_tok, f32_tok, gu_shape, bf16_tok]
    return pl.pallas_call(
        body, name="ffn_fwd_loss" if last else "ffn_fwd", grid=(t_tok // tm,),
        in_specs=[tok, vec, vec, _anyspec(), _anyspec(), _anyspec()] + extra_spec,
        out_specs=out_specs, out_shape=out_shape,
        scratch_shapes=[pltpu.VMEM((2, D_FF, D_MODEL), BF16), pltpu.VMEM((D_FF, D_MODEL), BF16),
                        pltpu.SemaphoreType.DMA((2 * len(FFN_CHUNKS),))],
        compiler_params=_params(62, ("arbitrary",)),
    )(x, ln_g, ln_b, wup, wd, after, *extra_in)


def _ffn_bwd(dxn, z, gu, wup, wd, ln_g, after, tm=256):
    t_tok = dxn.shape[0]

    def body(dxn_ref, z_ref, gu_ref, g_ref, wup_hbm, wd_hbm, _after,
             dx_ref, dy_ref, a_ref, dgu_ref, dg_ref, db_ref, wup_v, wd_v, sems):
        i = pl.program_id(0)
        _load_weights_once([(wup_hbm, wup_v), (wd_hbm, wd_v)], sems)
        dxn_t = dxn_ref[...]
        xhat, rstd = _ln_stats(z_ref[...])
        pg = jnp.sum(dxn_t * xhat, axis=0, keepdims=True)
        pb = jnp.sum(dxn_t, axis=0, keepdims=True)

        @pl.when(i == 0)
        def _():
            dg_ref[...] = pg
            db_ref[...] = pb

        @pl.when(i > 0)
        def _():
            dg_ref[...] += pg
            db_ref[...] += pb

        dz = _ln_bwd(dxn_t, xhat, rstd, g_ref[...])
        dy = (0.5 * dz).astype(BF16)
        dy_ref[...] = dy
        dx = ALPHA * dz
        for lo, hi in FFN_BWD_CHUNKS:
            da = _dot_nt(dy, wd_v[lo:hi])
            g = gu_ref[0, :, lo:hi].astype(F32)
            u = gu_ref[1, :, lo:hi].astype(F32)
            sig = jax.nn.sigmoid(g)
            silu = g * sig
            a_ref[:, lo:hi] = (silu * u).astype(BF16)
            dg = (da * u * (sig * (1.0 + g * (1.0 - sig)))).astype(BF16)
            du = (da * silu).astype(BF16)
            dgu_ref[0, :, lo:hi] = dg
            dgu_ref[1, :, lo:hi] = du
            dx = dx + _dot(dg, wup_v[0, lo:hi]) + _dot(du, wup_v[1, lo:hi])
        dx_ref[...] = dx

    tok = pl.BlockSpec((tm, D_MODEL), lambda i: (i, 0))
    vec = pl.BlockSpec((1, D_MODEL), lambda i: (0, 0))
    gu_spec = pl.BlockSpec((2, tm, D_FF), lambda i: (0, i, 0))
    return pl.pallas_call(
        body, name="ffn_bwd", grid=(t_tok // tm,),
        in_specs=[tok, tok, gu_spec, vec, _anyspec(), _anyspec(), _anyspec()],
        out_specs=[tok, tok, pl.BlockSpec((tm, D_FF), lambda i: (i, 0)), gu_spec, vec, vec],
        out_shape=[_hbm((t_tok, D_MODEL), F32), _hbm((t_tok, D_MODEL), BF16),
                   _hbm((t_tok, D_FF), BF16), _hbm((2, t_tok, D_FF), BF16),
                   _hbm((1, D_MODEL), F32), _hbm((1, D_MODEL), F32)],
        scratch_shapes=[pltpu.VMEM((2, D_FF, D_MODEL), BF16), pltpu.VMEM((D_FF, D_MODEL), BF16),
                        pltpu.SemaphoreType.DMA((2,))],
        compiler_params=_params(60, ("arbitrary",)),
    )(dxn, z, gu, ln_g, wup, wd, after)


def _matmul_tn(a, b, after, tk=4096, bm=None):
    ga, t_tok, m = a.shape
    gb, _, n = b.shape
    groups = max(ga, gb)
    tk = min(tk, t_tok)
    bm = m if bm is None else bm

    def body(a_ref, b_ref, _after, o_ref):
        p = _dot_tn(a_ref[...].astype(BF16), b_ref[...].astype(BF16))

        @pl.when(pl.program_id(2) == 0)
        def _():
            o_ref[...] = p

        @pl.when(pl.program_id(2) > 0)
        def _():
            o_ref[...] += p

    return pl.pallas_call(
        body, name=f"matmul_tn_{m}x{n}", grid=(groups, m // bm, t_tok // tk),
        in_specs=[pl.BlockSpec((None, tk, bm), (lambda g, i, t: (g, t, i)) if ga > 1 else (lambda g, i, t: (0, t, i))),
                  pl.BlockSpec((None, tk, n), (lambda g, i, t: (g, t, 0)) if gb > 1 else (lambda g, i, t: (0, t, 0))),
                  _anyspec()],
        out_specs=pl.BlockSpec((None, bm, n), lambda g, i, t: (g, i, 0)),
        out_shape=_hbm((groups, m, n), F32),
        compiler_params=_params(56, ("arbitrary", "arbitrary", "arbitrary")),
    )(a, b, after)


def _in_proj(x, w_in, tm=512):
    t_tok = x.shape[0]

    def body(x_ref, w_ref, conv_ref, qkv_ref, sgu_ref, f_ref):
        xb = x_ref[...].astype(BF16)
        conv_ref[...] = _dot(xb, w_ref[:, COL_CONV:COL_QKV])
        qkv_ref[...] = _dot(xb, w_ref[:, COL_QKV:COL_SGU]).astype(BF16)
        sgu_ref[...] = _dot(xb, w_ref[:, COL_SGU:COL_F])
        f_ref[...] = _dot(xb, w_ref[:, COL_F:D_IN_PAD])

    def tok(n):
        return pl.BlockSpec((tm, n), lambda i: (i, 0))

    return pl.pallas_call(
        body, name="mix_in_proj", grid=(t_tok // tm,),
        in_specs=[tok(D_MODEL), pl.BlockSpec((D_MODEL, D_IN_PAD), lambda i: (0, 0))],
        out_specs=[tok(768), tok(1536), tok(512), tok(128)],
        out_shape=[_hbm((t_tok, 768), F32), _hbm((t_tok, 1536), BF16),
                   _hbm((t_tok, 512), F32), _hbm((t_tok, 128), F32)],
        compiler_params=_params(48, ("arbitrary",)),
    )(x, w_in)


def _shift_down(a, k):
    row = lax.broadcasted_iota(jnp.int32, a.shape, 0)
    return jnp.where(row >= k, pltpu.roll(a, k, 0), 0.0)


def _shift_up(a, k):
    rows = a.shape[0]
    row = lax.broadcasted_iota(jnp.int32, a.shape, 0)
    return jnp.where(row < rows - k, pltpu.roll(a, rows - k, 0), 0.0)


def _tril(n):
    return lax.broadcasted_iota(jnp.int32, (n, n), 0) >= lax.broadcasted_iota(jnp.int32, (n, n), 1)


def _sgu_group_of_lane():
    return lax.broadcasted_iota(jnp.int32, (1, D_SGU), 1) // (D_SGU // N_SGU_GROUPS)


def _log_sigmoid(x):
    return jnp.minimum(x, 0.0) - jnp.log1p(jnp.exp(-jnp.abs(x)))


def _mix_mid_fwd(conv, sgu, f, conv_w, b_f, sgu_g, sgu_b, w_s, b_mat, n_seq):
    t_tok = conv.shape[0]
    seq = t_tok // n_seq
    n_chunk = seq // SGU_CHUNK
    per_blk = ATT_BLK // SGU_CHUNK

    def body(conv_ref, sgu_ref, f_ref, cw_ref, bf_ref, lg_ref, lb_ref, ws_ref, bm_ref, cat_ref, cum_ref):
        z = conv_ref[:, 256:512] * conv_ref[:, 512:768]
        y = cw_ref[0:1, :] * _shift_down(z, 2) + cw_ref[1:2, :] * _shift_down(z, 1) + cw_ref[2:3, :] * z
        cat_ref[:, 0:D_CONV] = (conv_ref[:, 0:256] * y).astype(BF16)
        cat_ref[:, D_CONV:D_CONV + D_FOX] = jnp.zeros((seq, D_FOX), BF16)

        tril = _tril(SGU_CHUNK)
        grp = _sgu_group_of_lane()
        wc = [jnp.where(tril, ws_ref[g], 0.0).astype(BF16) for g in range(N_SGU_GROUPS)]
        tri_f = tril.astype(F32)
        carry = jnp.zeros((1, 128), F32)
        for n in range(n_chunk):
            rows = pl.ds(n * SGU_CHUNK, SGU_CHUNK)
            u = _gelu(sgu_ref[rows, 0:256])
            vhat, _ = _ln_stats(_gelu(sgu_ref[rows, 256:512]))
            vn = (vhat * lg_ref[...] + lb_ref[...]).astype(BF16)
            mixed = bm_ref[...]
            for g in range(N_SGU_GROUPS):
                mixed = mixed + jnp.where(grp == g, _dot(wc[g], vn), 0.0)
            cat_ref[rows, D_CONV + D_FOX:D_MODEL] = (u * mixed).astype(BF16)

            log_f = _log_sigmoid(f_ref[rows, :] + bf_ref[...])
            cs = _dot(tri_f, log_f, HIGHEST) + carry
            carry = cs[SGU_CHUNK - 1:SGU_CHUNK, :]
            cs_t = cs.T
            lanes = pl.ds((n % per_blk) * SGU_CHUNK, SGU_CHUNK)
            for h in range(N_HEADS):
                cum_ref[h, n // per_blk, :, lanes] = cs_t[h:h + 1, :]

    def seq_blk(n):
        return pl.BlockSpec((seq, n), lambda b: (b, 0))

    def full(shape):
        return pl.BlockSpec(shape, lambda b: (0,) * len(shape))

    return pl.pallas_call(
        body, name="mix_mid_fwd", grid=(n_seq,),
        in_specs=[seq_blk(768), seq_blk(512), seq_blk(128), full((8, 256)), full((1, 128)), full((1, 256)),
                  full((1, 256)), full((4, 128, 128)), full((128, 256))],
        out_specs=[seq_blk(D_MODEL), pl.BlockSpec((N_HEADS, seq // ATT_BLK, 1, ATT_BLK), lambda b: (b, 0, 0, 0))],
        out_shape=[_hbm((t_tok, D_MODEL), BF16), _hbm((n_seq * N_HEADS, seq // ATT_BLK, 1, ATT_BLK), F32)],
        compiler_params=_params(48, ("arbitrary",)),
    )(conv, sgu, f, conv_w, b_f, sgu_g, sgu_b, w_s, b_mat)


def _head_masks():
    lane = lax.broadcasted_iota(jnp.int32, (1, 128), 1)
    return lane < 64, lane


def _fox_fwd(qkv, cum_t, cat, n_seq):
    t_tok = qkv.shape[0]
    seq = t_tok // n_seq
    nq = seq // ATT_BLK
    blk = ATT_BLK

    def body(q_ref, k_ref, v_ref, c_ref, _cat, o_ref, lse_ref):
        qi = pl.program_id(2)
        first, _ = _head_masks()
        one = jnp.ones((1, 128), BF16)
        qh = []
        for hp in range(ATT_PAIRS):
            qs = q_ref[:, 128 * hp:128 * hp + 128] * ATT_SCALE
            zero = jnp.zeros_like(qs)
            qh += [jnp.where(first, qs, zero), jnp.where(first, zero, qs)]

        def step(kb, carry, masked):
            ms, accs = carry
            rows = pl.ds(pl.multiple_of(kb * blk, blk), blk)
            new_m, new_acc = [], []
            for hp in range(ATT_PAIRS):
                k = k_ref[rows, 128 * hp:128 * hp + 128]
                v = v_ref[rows, 128 * hp:128 * hp + 128]
                for h in range(2):
                    i = 2 * hp + h
                    s = _dot_nt(qh[i], k) - c_ref[i, kb]
                    if masked:
                        s = jnp.where(causal, s, NEG)
                    m_new = jnp.maximum(ms[i], jnp.max(s, axis=1, keepdims=True))
                    p = jnp.exp(s - m_new)
                    vh = jnp.where(first, v, one) if h == 0 else jnp.where(first, one, v)
                    new_acc.append(accs[i] * jnp.exp(ms[i] - m_new) + _dot(p.astype(BF16), vh))
                    new_m.append(m_new)
            return tuple(new_m), tuple(new_acc)

        causal = _tril(blk)
        n_heads = 2 * ATT_PAIRS
        col = jnp.full((blk, 1), NEG, F32)
        zacc = jnp.zeros((blk, 128), F32)
        carry = lax.fori_loop(0, qi, lambda kb, cr: step(kb, cr, False), ((col,) * n_heads, (zacc,) * n_heads))
        ms, accs = step(qi, carry, True)
        for hp in range(ATT_PAIRS):
            acc0, acc1 = accs[2 * hp], accs[2 * hp + 1]
            l0 = pltpu.roll(acc0, 64, 1)
            l1 = pltpu.roll(acc1, 64, 1)
            o_ref[:, 128 * hp:128 * hp + 128] = jnp.where(first, acc0 / l0, acc1 / l1).astype(BF16)
            lse_ref[:, 128 * hp:128 * hp + 128] = jnp.where(first, ms[2 * hp] + jnp.log(l0), ms[2 * hp + 1] + jnp.log(l1))

    wide = 128 * ATT_PAIRS
    n_grp = D_FOX // wide
    first_col = D_CONV // wide
    return pl.pallas_call(
        body, name="fox_fwd", grid=(n_seq, n_grp, nq),
        in_specs=[pl.BlockSpec((blk, wide), lambda b, g, qi: (b * nq + qi, g)),
                  pl.BlockSpec((seq, wide), lambda b, g, qi: (b, n_grp + g)),
                  pl.BlockSpec((seq, wide), lambda b, g, qi: (b, 2 * n_grp + g)),
                  pl.BlockSpec((2 * ATT_PAIRS, nq, 1, blk), lambda b, g, qi: (b * n_grp + g, 0, 0, 0)), _anyspec()],
        out_specs=[pl.BlockSpec((blk, wide), lambda b, g, qi: (b * nq + qi, first_col + g)),
                   pl.BlockSpec((blk, wide), lambda b, g, qi: (b * nq + qi, g))],
        out_shape=[_hbm(cat.shape, BF16), _hbm((t_tok, D_FOX), F32)],
        input_output_aliases={4: 0},
        compiler_params=_params(48, ("arbitrary", "arbitrary", "arbitrary")),
    )(qkv, qkv, qkv, cum_t, cat)


def _fox_bwd(qkv, cum_t, cat, lse, d_o, n_seq):
    t_tok = qkv.shape[0]
    seq = t_tok // n_seq
    nk = seq // ATT_BLK
    blk = ATT_BLK

    def body(q_ref, k_ref, v_ref, c_ref, o_ref, lse_ref, do_ref, dq_ref, dk_ref, dv_ref, drow_ref, dcol_ref):
        kb = pl.program_id(2)
        first, lane = _head_masks()
        second = jnp.logical_not(first)
        one = jnp.ones((1, 128), BF16)
        causal = _tril(blk)

        @pl.when(kb == 0)
        def _():
            dq_ref[...] = jnp.zeros_like(dq_ref)
            drow_ref[...] = jnp.zeros_like(drow_ref)

        def step(qi, carry, masked):
            rows = pl.ds(pl.multiple_of(qi * blk, blk), blk)
            dks, dvs = carry
            new_dk, new_dv = [], []
            for hp in range(ATT_PAIRS):
                cols = slice(128 * hp, 128 * hp + 128)
                k = k_ref[:, cols]
                v = v_ref[:, cols]
                ks = k * ATT_SCALE
                zero = jnp.zeros_like(k)
                qs = q_ref[rows, cols] * ATT_SCALE
                d_o = do_ref[rows, cols]
                dd = d_o.astype(F32) * o_ref[rows, cols].astype(F32)
                lse_t = lse_ref[rows, cols]
                dq = []
                for h, mine in enumerate((first, second)):
                    i = 2 * hp + h
                    qh = jnp.where(mine, qs, zero)
                    doh = jnp.where(mine, d_o, zero)
                    delta = jnp.sum(jnp.where(mine, dd, 0.0), axis=1, keepdims=True)
                    lse_h = jnp.sum(jnp.where(lane == 64 * h, lse_t, 0.0), axis=1, keepdims=True)
                    s = _dot_nt(qh, k) - c_ref[i]
                    if masked:
                        s = jnp.where(causal, s, NEG)
                    p = jnp.exp(s - lse_h)
                    ds = (p * (_dot_nt(doh, v) - delta)).astype(BF16)
                    new_dk.append(dks[i] + _dot_tn(ds, jnp.where(mine, qs, one)))
                    new_dv.append(dvs[i] + _dot_tn(p.astype(BF16), doh))
                    dq.append(_dot(ds, jnp.where(mine, ks, one)))
                dq_ref[rows, cols] += jnp.where(first, dq[0], dq[1])
                drow_ref[rows, cols] += jnp.where(first, dq[1], dq[0])
            return tuple(new_dk), tuple(new_dv)

        zt = (jnp.zeros((blk, 128), F32),) * (2 * ATT_PAIRS)
        carry = step(kb, (zt, zt), True)
        dks, dvs = lax.fori_loop(kb + 1, nk, lambda qi, cr: step(qi, cr, False), carry)
        for hp in range(ATT_PAIRS):
            cols = slice(128 * hp, 128 * hp + 128)
            dk_ref[:, cols] = jnp.where(first, dks[2 * hp], dks[2 * hp + 1]).astype(BF16)
            dcol_ref[:, cols] = jnp.where(first, dks[2 * hp + 1], dks[2 * hp])
            dv_ref[:, cols] = (dvs[2 * hp] + dvs[2 * hp + 1]).astype(BF16)

    wide = 128 * ATT_PAIRS
    n_grp = D_FOX // wide

    def seq_spec(col0):
        return pl.BlockSpec((seq, wide), lambda b, g, kb: (b, col0 + g))

    def key_spec(col0):
        return pl.BlockSpec((blk, wide), lambda b, g, kb: (b * nk + kb, col0 + g))

    return pl.pallas_call(
        body, name="fox_bwd", grid=(n_seq, n_grp, nk),
        in_specs=[seq_spec(0), key_spec(n_grp), key_spec(2 * n_grp),
                  pl.BlockSpec((2 * ATT_PAIRS, None, 1, blk), lambda b, g, kb: (b * n_grp + g, kb, 0, 0)),
                  seq_spec(D_CONV // wide), seq_spec(0), seq_spec(0)],
        out_specs=[seq_spec(0), key_spec(0), key_spec(0), seq_spec(0), key_spec(0)],
        out_shape=[_hbm((t_tok, D_FOX), F32), _hbm((t_tok, D_FOX), BF16),
                   _hbm((t_tok, D_FOX), BF16), _hbm((t_tok, D_FOX), F32),
                   _hbm((t_tok, D_FOX), F32)],
        compiler_params=_params(56, ("arbitrary", "arbitrary", "arbitrary")),
    )(qkv, qkv, qkv, cum_t, cat, lse, d_o)


def _mix_out_fwd(cat, x, w_out, ln_g, ln_b, tm=512):
    t_tok = x.shape[0]

    def body(cat_ref, x_ref, w_ref, g_ref, b_ref, xn_ref, xnb_ref, z_ref):
        z = ALPHA * x_ref[...] + _dot(cat_ref[...], w_ref[...])
        xhat, _ = _ln_stats(z)
        xn = xhat * g_ref[...] + b_ref[...]
        z_ref[...] = z
        xn_ref[...] = xn
        xnb_ref[...] = xn.astype(BF16)

    def tok(n):
        return pl.BlockSpec((tm, n), lambda i: (i, 0))

    vec = pl.BlockSpec((1, D_MODEL), lambda i: (0, 0))
    return pl.pallas_call(
        body, name="mix_out_fwd", grid=(t_tok // tm,),
        in_specs=[tok(D_MODEL), tok(D_MODEL), pl.BlockSpec((D_MODEL, D_MODEL), lambda i: (0, 0)), vec, vec],
        out_specs=[tok(D_MODEL)] * 3,
        out_shape=[_hbm((t_tok, D_MODEL), F32), _hbm((t_tok, D_MODEL), BF16),
                   _hbm((t_tok, D_MODEL), F32)],
        compiler_params=_params(40, ("arbitrary",)),
    )(cat, x, w_out, ln_g, ln_b)


def _mix_out_bwd(dxn, z, w_out, ln_g, tm=512):
    t_tok = dxn.shape[0]

    def body(dxn_ref, z_ref, w_ref, g_ref, dz_ref, dzb_ref, dya_ref, dyb_ref, dyc_ref, dg_ref, db_ref):
        i = pl.program_id(0)
        dxn_t = dxn_ref[...]
        xhat, rstd = _ln_stats(z_ref[...])
        pg = jnp.sum(dxn_t * xhat, axis=0, keepdims=True)
        pb = jnp.sum(dxn_t, axis=0, keepdims=True)

        @pl.when(i == 0)
        def _():
            dg_ref[...] = pg
            db_ref[...] = pb

        @pl.when(i > 0)
        def _():
            dg_ref[...] += pg
            db_ref[...] += pb

        dz = _ln_bwd(dxn_t, xhat, rstd, g_ref[...])
        dzb = dz.astype(BF16)
        dz_ref[...] = dz
        dzb_ref[...] = dzb
        dya_ref[...] = _dot_nt(dzb, w_ref[0:256, :])
        dyb_ref[...] = _dot_nt(dzb, w_ref[256:768, :]).astype(BF16)
        dyc_ref[...] = _dot_nt(dzb, w_ref[768:1024, :])

    def tok(n):
        return pl.BlockSpec((tm, n), lambda i: (i, 0))

    vec = pl.BlockSpec((1, D_MODEL), lambda i: (0, 0))
    return pl.pallas_call(
        body, name="mix_out_bwd", grid=(t_tok // tm,),
        in_specs=[tok(D_MODEL), tok(D_MODEL), pl.BlockSpec((D_MODEL, D_MODEL), lambda i: (0, 0)), vec],
        out_specs=[tok(D_MODEL), tok(D_MODEL), tok(256), tok(512), tok(256), vec, vec],
        out_shape=[_hbm((t_tok, D_MODEL), F32), _hbm((t_tok, D_MODEL), BF16),
                   _hbm((t_tok, 256), F32), _hbm((t_tok, 512), BF16),
                   _hbm((t_tok, 256), F32),
                   _hbm((1, D_MODEL), F32), _hbm((1, D_MODEL), F32)],
        compiler_params=_params(40, ("arbitrary",)),
    )(dxn, z, w_out, ln_g)


def _conv_bwd(conv, dya, conv_w, n_seq):
    t_tok = conv.shape[0]
    seq = t_tok // n_seq

    def body(conv_ref, dya_ref, cw_ref, dconv_ref, dcw_ref):
        @pl.when(pl.program_id(0) == 0)
        def _():
            dcw_ref[...] = jnp.zeros_like(dcw_ref)

        z = conv_ref[:, 256:512] * conv_ref[:, 512:768]
        z1 = _shift_down(z, 1)
        z2 = _shift_down(z, 2)
        y = cw_ref[0:1, :] * z2 + cw_ref[1:2, :] * z1 + cw_ref[2:3, :] * z
        dya_t = dya_ref[...]
        dconv_ref[:, 0:256] = (dya_t * y).astype(BF16)
        dy = dya_t * conv_ref[:, 0:256]
        dcw_ref[0:1, :] += jnp.sum(dy * z2, axis=0, keepdims=True)
        dcw_ref[1:2, :] += jnp.sum(dy * z1, axis=0, keepdims=True)
        dcw_ref[2:3, :] += jnp.sum(dy * z, axis=0, keepdims=True)
        dz = cw_ref[2:3, :] * dy + cw_ref[1:2, :] * _shift_up(dy, 1) + cw_ref[0:1, :] * _shift_up(dy, 2)
        dconv_ref[:, 256:512] = (dz * conv_ref[:, 512:768]).astype(BF16)
        dconv_ref[:, 512:768] = (dz * conv_ref[:, 256:512]).astype(BF16)

    def seq_blk(n):
        return pl.BlockSpec((seq, n), lambda b: (b, 0))

    par = pl.BlockSpec((8, 256), lambda b: (0, 0))
    return pl.pallas_call(
        body, name="conv_bwd", grid=(n_seq,),
        in_specs=[seq_blk(768), seq_blk(256), par], out_specs=[seq_blk(768), par],
        out_shape=[_hbm((t_tok, 768), BF16), _hbm((8, 256), F32)],
        compiler_params=_params(56, ("arbitrary",)),
    )(conv, dya, conv_w)


def _sgu_gate_bwd(sgu, f, dyc, drow, dcol, b_f, sgu_g, sgu_b, w_s, b_mat, n_seq):
    t_tok = sgu.shape[0]
    seq = t_tok // n_seq
    n_chunk = seq // SGU_CHUNK

    def body(sgu_ref, f_ref, dyc_ref, drow_ref, dcol_ref, bf_ref, lg_ref, lb_ref, ws_ref, bm_ref,
             dsgu_ref, df_ref, dbf_ref, dlg_ref, dlb_ref, dws_ref, dbs_ref, dbm_acc):
        b = pl.program_id(0)

        @pl.when(b == 0)
        def _():
            for r in (dbf_ref, dlg_ref, dlb_ref, dws_ref, dbm_acc):
                r[...] = jnp.zeros_like(r)

        tril = _tril(SGU_CHUNK)
        grp = _sgu_group_of_lane()
        wc = [jnp.where(tril, ws_ref[g], 0.0).astype(BF16) for g in range(N_SGU_GROUPS)]
        for n in range(n_chunk):
            rows = pl.ds(n * SGU_CHUNK, SGU_CHUNK)
            su = sgu_ref[rows, 0:256]
            sv = sgu_ref[rows, 256:512]
            u, du = _gelu_with_grad(su)
            gv, dgv = _gelu_with_grad(sv)
            vhat, rstd = _ln_stats(gv)
            vn = (vhat * lg_ref[...] + lb_ref[...]).astype(BF16)
            mixed = bm_ref[...]
            for g in range(N_SGU_GROUPS):
                mixed = mixed + jnp.where(grp == g, _dot(wc[g], vn), 0.0)
            dyc_t = dyc_ref[rows, :]
            dsgu_ref[rows, 0:256] = (dyc_t * mixed * du).astype(BF16)
            dmixed = dyc_t * u
            dbm_acc[...] += dmixed
            dvn = jnp.zeros((SGU_CHUNK, D_SGU), F32)
            for g in range(N_SGU_GROUPS):
                dm_g = jnp.where(grp == g, dmixed, 0.0).astype(BF16)
                dws_ref[g] += _dot_nt(dm_g, vn)
                dvn = dvn + _dot_tn(wc[g], dm_g)
            dlg_ref[...] += jnp.sum(dvn * vhat, axis=0, keepdims=True)
            dlb_ref[...] += jnp.sum(dvn, axis=0, keepdims=True)
            dsgu_ref[rows, 256:512] = (_ln_bwd(dvn, vhat, rstd, lg_ref[...]) * dgv).astype(BF16)

        later = (lax.broadcasted_iota(jnp.int32, (128, 128), 0) <= lax.broadcasted_iota(jnp.int32, (128, 128), 1)).astype(F32)
        head = lax.broadcasted_iota(jnp.int32, (D_FOX, 128), 1)
        pick = (lax.broadcasted_iota(jnp.int32, (D_FOX, 128), 0) == 128 * (head // 2) + 64 * (1 - head % 2)).astype(F32)
        carry = jnp.zeros((1, 128), F32)
        for n in reversed(range(n_chunk)):
            rows = pl.ds(n * SGU_CHUNK, SGU_CHUNK)
            dcum_n = _dot(drow_ref[rows, :] - dcol_ref[rows, :], pick, HIGHEST)
            dlf = _dot(later, dcum_n, HIGHEST) + carry
            carry = carry + jnp.sum(dcum_n, axis=0, keepdims=True)
            df = dlf * jax.nn.sigmoid(-(f_ref[rows, :] + bf_ref[...]))
            df_ref[rows, :] = df.astype(BF16)
            dbf_ref[...] += jnp.sum(df, axis=0, keepdims=True)

        @pl.when(b == n_seq - 1)
        def _():
            for g in range(N_SGU_GROUPS):
                dws_ref[g] = jnp.where(tril, dws_ref[g], 0.0)
            sel = (lax.broadcasted_iota(jnp.int32, (D_SGU, 128), 0) // (D_SGU // N_SGU_GROUPS)
                   == lax.broadcasted_iota(jnp.int32, (D_SGU, 128), 1)).astype(F32)
            dbs_ref[...] = _dot(dbm_acc[...], sel, HIGHEST)

    def seq_blk(n):
        return pl.BlockSpec((seq, n), lambda b: (b, 0))

    def full(shape):
        return pl.BlockSpec(shape, lambda b: (0,) * len(shape))

    param_shapes = [(1, 128), (1, 256), (1, 256), (4, 128, 128), (128, 128)]
    return pl.pallas_call(
        body, name="sgu_gate_bwd", grid=(n_seq,),
        in_specs=[seq_blk(512), seq_blk(128), seq_blk(256), seq_blk(D_FOX), seq_blk(D_FOX),
                  full((1, 128)), full((1, 256)), full((1, 256)), full((4, 128, 128)), full((128, 256))],
        out_specs=[seq_blk(512), seq_blk(128)] + [full(s) for s in param_shapes],
        out_shape=[_hbm((t_tok, 512), BF16), _hbm((t_tok, 128), BF16)]
        + [_hbm(s, F32) for s in param_shapes],
        scratch_shapes=[pltpu.VMEM((128, 256), F32)],
        compiler_params=_params(48, ("arbitrary",)),
    )(sgu, f, dyc, drow, dcol, b_f, sgu_g, sgu_b, w_s, b_mat)


def _mix_in_bwd(dconv, dq, dk, dv, dsgu, df, dz, w_in, tm=512):
    t_tok = dz.shape[0]

    def body(dconv_ref, dq_ref, dk_ref, dv_ref, dsgu_ref, df_ref, dz_ref, w_ref, dx_ref, dp_ref):
        dqb = dq_ref[...].astype(BF16)
        pieces = [(COL_CONV, dconv_ref[...]), (COL_QKV, dqb), (COL_QKV + 512, dk_ref[...]), (COL_QKV + 1024, dv_ref[...]),
                  (COL_SGU, dsgu_ref[...]), (COL_F, df_ref[...])]
        dx = ALPHA * dz_ref[...]
        for col, val in pieces:
            width = val.shape[1]
            dp_ref[:, col:col + width] = val
            dx = dx + _dot_nt(val, w_ref[:, col:col + width])
        dx_ref[...] = dx

    def tok(n):
        return pl.BlockSpec((tm, n), lambda i: (i, 0))

    return pl.pallas_call(
        body, name="mix_in_bwd", grid=(t_tok // tm,),
        in_specs=[tok(768), tok(512), tok(512), tok(512), tok(512), tok(128), tok(D_MODEL),
                  pl.BlockSpec((D_MODEL, D_IN_PAD), lambda i: (0, 0))],
        out_specs=[tok(D_MODEL), tok(D_IN_PAD)],
        out_shape=[_hbm((t_tok, D_MODEL), F32), _hbm((t_tok, D_IN_PAD), BF16)],
        compiler_params=_params(48, ("arbitrary",)),
    )(dconv, dq, dk, dv, dsgu, df, dz, w_in)


def _pad_rows(a, rows):
    return jnp.pad(a, ((0, rows - a.shape[0]), (0, 0)))


F_BLOCK = F_ORIG // D_IN_SHARD
F_AT = F_ORIG - F_BLOCK * D_IN_SHARD
assert (F_ORIG + N_HEADS) // D_IN_SHARD == F_BLOCK


def _w_in_from_blocks(g):
    fb = g[F_BLOCK]
    zeros = jnp.zeros((D_MODEL, D_IN_PAD - COL_F - N_HEADS), g.dtype)
    return jnp.concatenate([g[d] for d in range(F_BLOCK)] + [fb[:, :F_AT], fb[:, F_AT + N_HEADS:]]
                           + [g[d] for d in range(F_BLOCK + 1, N_DEV)] + [fb[:, F_AT:F_AT + N_HEADS], zeros], axis=1)


def _w_in_to_blocks(dw):
    def cols(lo, hi):
        shift = 0 if hi <= F_ORIG else N_HEADS
        return dw[:, lo - shift:hi - shift]

    blocks = []
    for d in range(N_DEV):
        lo, hi = d * D_IN_SHARD, (d + 1) * D_IN_SHARD
        if d == F_BLOCK:
            blocks.append(jnp.concatenate([cols(lo, F_ORIG), dw[:, COL_F:COL_F + N_HEADS], cols(F_ORIG + N_HEADS, hi)], axis=1))
        else:
            blocks.append(cols(lo, hi))
    return jnp.stack(blocks)


LN1_ROWS = 2 * 8
REST_ROWS = 4 * 8 + 2 * 8 + 512 + 8 + 8 + 8


def _pack_rest(p):
    rows = [p[name].reshape(8, 128) for name in ("ln2_g", "ln2_b", "ln3_g", "ln3_b")]
    rows += [_pad_rows(p[name].reshape(2, 128), 8) for name in ("sgu_ln_g", "sgu_ln_b")]
    rows += [p["sgu_w_s"].reshape(512, 128), _pad_rows(p["sgu_b_s"], 8),
             _pad_rows(jnp.pad(p["fox_b_f"], (0, 128 - N_HEADS)).reshape(1, 128), 8), _pad_rows(p["conv_w"].reshape(6, 128), 8)]
    return jnp.concatenate(rows, axis=0)


def _pack_layer(p):
    return jnp.concatenate([p["ln1_g"].reshape(8, 128), p["ln1_b"].reshape(8, 128), _pack_rest(p)], axis=0)


def _unpack_layer(a):
    r = 0

    def take(n, valid):
        nonlocal r
        piece = a[r:r + valid]
        r += n
        return piece

    d = {}
    for name in ("ln1_g", "ln1_b", "ln2_g", "ln2_b", "ln3_g", "ln3_b"):
        d[name] = take(8, 8).reshape(D_MODEL)
    for name in ("sgu_ln_g", "sgu_ln_b"):
        d[name] = take(8, 2).reshape(D_SGU)
    d["sgu_w_s"] = take(512, 512).reshape(N_SGU_GROUPS, SGU_CHUNK, SGU_CHUNK)
    d["sgu_b_s"] = take(8, 4).reshape(N_SGU_GROUPS, SGU_CHUNK)
    d["fox_b_f"] = take(8, 1).reshape(128)[:N_HEADS]
    d["conv_w"] = take(8, 6).reshape(3, D_CONV)
    return d


SMALL_NAMES = ("ln1_g", "ln1_b", "fox_b_f", "sgu_ln_g", "sgu_ln_b", "sgu_w_s", "sgu_b_s", "ln2_g", "ln2_b", "ln3_g", "ln3_b")
BIG_NAMES = ("ffn1_w_up", "ffn1_w_down", "mix_w_in", "mix_w_out", "ffn2_w_up", "ffn2_w_down")
UP_NAMES = ("ffn1_w_up", "ffn2_w_up")
WEIGHT_ORDER = ("ln1_g", "ln1_b", "ffn1_w_up", "ffn1_w_down", "mix_w_in", "fox_b_f", "conv_w", "sgu_ln_g", "sgu_ln_b",
                "sgu_w_s", "sgu_b_s", "mix_w_out", "ln2_g", "ln2_b", "ffn2_w_up", "ffn2_w_down", "ln3_g", "ln3_b")


class _Overlap:
    def __init__(self, w, after, me, where):
        self.me, self.where = me, where
        self.last = after
        groups = [[("ffn1_w_up", 0), ("ffn1_w_down", 0)],
                  [("mix_w_in", 0), ("mix_w_out", 0), ("ffn2_w_up", 0), ("ffn2_w_down", 0)]]
        groups += [[(name, l) for name in BIG_NAMES] for l in range(1, DEPTH)]
        self.gathers = []
        for gi, group in enumerate(groups):
            shards = [w[name][l].astype(BF16) for name, l in group]
            lands = [lax.dynamic_update_slice(lax.empty((N_DEV,) + s.shape, BF16), s[None], (me, 0, 0)) for s in shards]
            started = self._start(f"allgather_start_{gi}", _gather_plan(len(group)), 3 * len(group), shards + lands)
            self.gathers.append(dict(group=group, chips=started))
            if gi == 0:
                width = D_CONV // N_DEV
                rows = jnp.pad(_pad_rows(w["conv_w"].reshape(DEPTH * 3, width), 8), ((0, 0), (0, 128 - width)))
                land = lax.dynamic_update_slice(lax.empty((N_DEV,) + rows.shape, F32), rows[None], (me, 0, 0))
                self.conv_started = self._start("conv_w_start", _peers_plan(), N_DEV - 1, [rows, land])
                self.conv_full = None
        self.all_started = self.last
        self.scatters = {}
        self.order = []
        self.small = []

    def conv_w(self, after):
        if self.conv_full is None:
            width = D_CONV // N_DEV
            gathered = _exchange_wait("conv_w_wait", _peers_plan(), N_DEV - 1, self.conv_started, after)[1]
            self.conv_full = jnp.transpose(gathered[:, :DEPTH * 3, :width], (1, 0, 2)).reshape(DEPTH, 3, D_CONV)
        return self.conv_full

    def _start(self, name, plan, n_copies, arrays):
        started = _exchange_start(name, plan, n_copies, arrays, self.last)
        self.last = started[3]
        return started

    def _group_of(self, layer, part):
        return layer + 1 if layer > 0 else (0 if part == "ffn1" else 1)

    def pass_on(self, layer, part, after):
        gi = self._group_of(layer, part)
        st = self.gathers[gi]
        if "sibling" not in st:
            m = len(st["group"])
            arrays = _exchange_wait(f"allgather_wait_{gi}", _gather_plan(m), 3 * m, st["chips"], after)
            first = [i for i, (name, _) in enumerate(st["group"]) if name == "mix_w_in"] if gi == 1 else []
            st["passes"] = [idx for idx in (first, [i for i in range(m) if i not in first]) if idx]
            st["sibling"] = [self._start(f"allgather_pass_start_{gi}_{k}", _pass_on_plan(len(idx)), 4 * len(idx),
                                         [arrays[i] for i in idx] + [arrays[m + i] for i in idx])
                             for k, idx in enumerate(st["passes"])]
            st["full"] = {}
        return st["sibling"][-1][3]

    def weights(self, layer, part, after):
        names = {"ffn1": ("ffn1_w_up", "ffn1_w_down"), "in": ("mix_w_in",),
                 "rest": ("mix_w_out", "ffn2_w_up", "ffn2_w_down")}[part]
        gi = self._group_of(layer, part)
        st = self.gathers[gi]
        after = self.all_started if after is None else after
        self.pass_on(layer, part, after)
        g = st["full"]
        if (names[0], layer) not in g:
            for k, idx in enumerate(st["passes"]):
                if st["group"].index((names[0], layer)) in idx:
                    arrays = _exchange_wait(f"allgather_pass_wait_{gi}_{k}", _pass_on_plan(len(idx)), 4 * len(idx),
                                            st["sibling"][k], after)
                    g.update(zip([st["group"][i] for i in idx], arrays[len(idx):]))

        def ffn(n):
            return g[(f"ffn{n}_w_up", layer)].reshape(2, D_FF, D_MODEL), g[(f"ffn{n}_w_down", layer)].reshape(D_FF, D_MODEL)

        if part == "ffn1":
            return ffn(1)
        if part == "in":
            return _w_in_from_blocks(g[("mix_w_in", layer)])
        return (g[("mix_w_out", layer)].reshape(D_MODEL, D_MODEL), *ffn(2))

    def push(self, key, items):
        n = len(items)
        grads = [g for _, _, g in items]
        lands = [lax.empty((4,) + g.shape[1:], F32) for g in grads]
        started = self._start(f"rs_sibling_start_{key[0]}{key[1]}", _sibling_plan(n), 4 * n, grads + lands)
        self.scatters[key] = dict(items=items, sibling=started)
        self.order.append(key)
        return started[3]

    def advance(self, key, after):
        st = self.scatters[key]
        n = len(st["items"])
        arrays = _exchange_wait(f"rs_sibling_wait_{key[0]}{key[1]}", _sibling_plan(n), 4 * n, st["sibling"], after)
        partials = [_chip_partial(g, r, self.where) for g, r in zip(arrays[:n], arrays[n:])]
        p16 = [p for _, p in partials]
        lands = [lax.empty((3,) + p.shape[1:], BF16) for p in p16]
        started = self._start(f"rs_chip_start_{key[0]}{key[1]}", _chip_plan(n), 3 * n, p16 + lands)
        st.update(own32=[p for p, _ in partials], chip=started)
        return started[3]

    def push_small(self, rows):
        k = len(self.small)
        land = lax.dynamic_update_slice(lax.empty((N_DEV,) + rows.shape, F32), rows[None], (self.me, 0, 0))
        started = self._start(f"small_start_{k}", _peers_plan(), N_DEV - 1, [rows, land])
        self.small.append(started)
        return started[3]

    def finish(self, w, m, v):
        res = {}
        after = self.scatters[self.order[-1]]["chip"][3]
        for key in self.order:
            st = self.scatters[key]
            n = len(st["items"])
            arrays = _exchange_wait(f"rs_chip_wait_{key[0]}{key[1]}", _chip_plan(n), 3 * n, st["chip"], after)
            for (name, l, _), own32, r16 in zip(st["items"], st["own32"], arrays[n:]):
                res[name] = _adamw_shard(own32, r16, w[name], m[name], v[name], l, res.get(name))
                after = res[name][0]
        pieces = [_exchange_wait(f"small_wait_{k}", _peers_plan(), N_DEV - 1, started, after)[1]
                  for k, started in enumerate(self.small)]
        return res, pieces


def _dw_up(dgu, x, after):
    return _matmul_tn(dgu, x[None], after, tk=2048, bm=DW_ROWS).reshape(N_DEV, FFN_BLK, D_MODEL)


def _dw_down(a, dy, after):
    return _matmul_tn(a[None], dy[None], after, tk=2048, bm=DW_ROWS).reshape(N_DEV, FFN_BLK // 2, D_MODEL)


def _local_step(x, target, comm, small, n_seq):
    def vec(a):
        return a.reshape(1, -1)

    saved = []
    h = x
    for l in range(DEPTH):
        s = {}
        s["up1"], s["down1"] = comm.weights(l, "ffn1", None if l == 0 else h)
        h1, h1b, s["z1"], s["gu1"], s["x0b"] = _ffn_fwd(h, s["up1"], s["down1"], vec(small["ln1_g"][l]), vec(small["ln1_b"][l]), h)
        s["w_in"] = comm.weights(l, "in", s["z1"])
        s["x1b"] = h1b
        conv, qkv, sgu, f = _in_proj(h1, s["w_in"])
        s["w_out"], s["up2"], s["down2"] = comm.weights(l, "rest", f)
        cw = _pad_rows(comm.conv_w(h1)[l], 8)
        bf = jnp.pad(small["fox_b_f"][l], (0, 128 - N_HEADS)).reshape(1, 128)
        b_mat = jnp.repeat(small["sgu_b_s"][l].T, D_SGU // N_SGU_GROUPS, axis=1)
        mid_params = (cw, bf, vec(small["sgu_ln_g"][l]), vec(small["sgu_ln_b"][l]), small["sgu_w_s"][l], b_mat)
        cat, cum_t = _mix_mid_fwd(conv, sgu, f, *mid_params, n_seq)
        cat, lse = _fox_fwd(qkv, cum_t, cat, n_seq)
        h2, h2b, s["z2"] = _mix_out_fwd(cat, h1, s["w_out"], vec(small["ln2_g"][l]), vec(small["ln2_b"][l]))
        s.update(conv=conv, qkv=qkv, sgu=sgu, f=f, mid_params=mid_params, cat=cat, cum_t=cum_t, lse=lse, x2b=h2b)
        token = comm.pass_on(l + 1, "ffn1", s["z2"]) if l + 1 < DEPTH else h2
        ln3 = (vec(small["ln3_g"][l]), vec(small["ln3_b"][l]))
        if l + 1 < DEPTH:
            h, _, s["z3"], s["gu2"], _ = _ffn_fwd(h2, s["up2"], s["down2"], *ln3, token)
        else:
            dh, loss, s["z3"], s["gu2"] = _ffn_fwd(h2, s["up2"], s["down2"], *ln3, token, target)
        saved.append(s)

    late_rows = None
    token = loss
    pending = None
    for l in reversed(range(DEPTH)):
        s = saved[l]
        sg = {}
        dh, dy, a, dgu, sg["ln3_g"], sg["ln3_b"] = _ffn_bwd(dh, s["z3"], s["gu2"], s["up2"], s["down2"], vec(small["ln3_g"][l]), token)
        if pending is not None:
            token = comm.advance(pending, dh)
        g_up2 = _dw_up(dgu, s["x2b"], token)
        g_down2 = _dw_down(a, dy, token)
        dz, dzb, dya, dyb, dyc, sg["ln2_g"], sg["ln2_b"] = _mix_out_bwd(dh, s["z2"], s["w_out"], vec(small["ln2_g"][l]))
        g_out = _matmul_tn(s["cat"][None], dzb[None], token, tk=2048, bm=D_MODEL // 2).reshape(N_DEV, D_MODEL // N_DEV, D_MODEL)
        dq, dk, dv, drow, dcol = _fox_bwd(s["qkv"], s["cum_t"], s["cat"], s["lse"], dyb, n_seq)
        dconv, dcw = _conv_bwd(s["conv"], dya, s["mid_params"][0], n_seq)
        dsgu, df, dbf, dlg, dlb, dws, dbs = _sgu_gate_bwd(s["sgu"], s["f"], dyc, drow, dcol, *s["mid_params"][1:], n_seq)
        sg.update(conv_w=dcw[:3], fox_b_f=dbf[0, :N_HEADS], sgu_ln_g=dlg[0], sgu_ln_b=dlb[0], sgu_w_s=dws,
                  sgu_b_s=dbs[:, :N_SGU_GROUPS].T)
        dh, dp = _mix_in_bwd(dconv, dq, dk, dv, dsgu, df, dz, s["w_in"])
        g_in = _w_in_to_blocks(_matmul_tn(s["x1b"][None], dp[None], token, tk=2048, bm=D_MODEL // 2)[0])
        first = [("ffn2_w_up", l, g_up2), ("ffn2_w_down", l, g_down2), ("mix_w_out", l, g_out), ("mix_w_in", l, g_in)]
        for name in ("ln2_g", "ln2_b", "ln3_g", "ln3_b"):
            sg[name] = sg[name][0]
        if l == 0:
            comm.push((l, "a"), first)
            token = comm.push_small(_pack_rest(sg))
            pending, first = (l, "a"), []
        dh, dy, a, dgu, dg1, db1 = _ffn_bwd(dh, s["z1"], s["gu1"], s["up1"], s["down1"], vec(small["ln1_g"][l]), token)
        if l == 0:
            token = comm.advance(pending, dh)
        g_up1 = _dw_up(dgu, s["x0b"], token)
        ln1_rows = jnp.concatenate([dg1.reshape(8, 128), db1.reshape(8, 128)], axis=0)
        if l == 0:
            token = comm.push((l, "b"), [("ffn1_w_up", l, g_up1)])
            g_down1 = _dw_down(a, dy, token)
            token = comm.advance((l, "b"), g_down1)
            token = comm.push((l, "c"), [("ffn1_w_down", l, g_down1)])
            token = comm.advance((l, "c"), token)
            late_rows = ln1_rows
        else:
            g_down1 = _dw_down(a, dy, token)
            pending = (l, "b")
            comm.push(pending, first + [("ffn1_w_up", l, g_up1), ("ffn1_w_down", l, g_down1)])
            token = comm.push_small(jnp.concatenate([ln1_rows, _pack_rest(sg)], axis=0))
    return loss, dh, late_rows


def kernel(x, ln1_g, ln1_b, ffn1_w_up, ffn1_w_down, mix_w_in, fox_b_f, conv_w, sgu_ln_g, sgu_ln_b, sgu_w_s, sgu_b_s, mix_w_out, ln2_g, ln2_b, ffn2_w_up, ffn2_w_down, ln3_g, ln3_b, loss_target, m_ln1_g, m_ln1_b, m_ffn1_w_up, m_ffn1_w_down, m_mix_w_in, m_fox_b_f, m_conv_w, m_sgu_ln_g, m_sgu_ln_b, m_sgu_w_s, m_sgu_b_s, m_mix_w_out, m_ln2_g, m_ln2_b, m_ffn2_w_up, m_ffn2_w_down, m_ln3_g, m_ln3_b, v_ln1_g, v_ln1_b, v_ffn1_w_up, v_ffn1_w_down, v_mix_w_in, v_fox_b_f, v_conv_w, v_sgu_ln_g, v_sgu_ln_b, v_sgu_w_s, v_sgu_b_s, v_mix_w_out, v_ln2_g, v_ln2_b, v_ffn2_w_up, v_ffn2_w_down, v_ln3_g, v_ln3_b):
    w = dict(ln1_g=ln1_g, ln1_b=ln1_b, ffn1_w_up=ffn1_w_up, ffn1_w_down=ffn1_w_down, mix_w_in=mix_w_in, fox_b_f=fox_b_f,
             conv_w=conv_w, sgu_ln_g=sgu_ln_g, sgu_ln_b=sgu_ln_b, sgu_w_s=sgu_w_s, sgu_b_s=sgu_b_s, mix_w_out=mix_w_out,
             ln2_g=ln2_g, ln2_b=ln2_b, ffn2_w_up=ffn2_w_up, ffn2_w_down=ffn2_w_down, ln3_g=ln3_g, ln3_b=ln3_b)
    m = dict(ln1_g=m_ln1_g, ln1_b=m_ln1_b, ffn1_w_up=m_ffn1_w_up, ffn1_w_down=m_ffn1_w_down, mix_w_in=m_mix_w_in,
             fox_b_f=m_fox_b_f, conv_w=m_conv_w, sgu_ln_g=m_sgu_ln_g, sgu_ln_b=m_sgu_ln_b, sgu_w_s=m_sgu_w_s,
             sgu_b_s=m_sgu_b_s, mix_w_out=m_mix_w_out, ln2_g=m_ln2_g, ln2_b=m_ln2_b, ffn2_w_up=m_ffn2_w_up,
             ffn2_w_down=m_ffn2_w_down, ln3_g=m_ln3_g, ln3_b=m_ln3_b)
    v = dict(ln1_g=v_ln1_g, ln1_b=v_ln1_b, ffn1_w_up=v_ffn1_w_up, ffn1_w_down=v_ffn1_w_down, mix_w_in=v_mix_w_in,
             fox_b_f=v_fox_b_f, conv_w=v_conv_w, sgu_ln_g=v_sgu_ln_g, sgu_ln_b=v_sgu_ln_b, sgu_w_s=v_sgu_w_s,
             sgu_b_s=v_sgu_b_s, mix_w_out=v_mix_w_out, ln2_g=v_ln2_g, ln2_b=v_ln2_b, ffn2_w_up=v_ffn2_w_up,
             ffn2_w_down=v_ffn2_w_down, ln3_g=v_ln3_g, ln3_b=v_ln3_b)

    mx, my, mc = lax.axis_index("x"), lax.axis_index("y"), lax.axis_index("c")
    me = 4 * mx + 2 * my + mc
    n_seq, seq, _ = x.shape
    t_tok = n_seq * seq
    for name in UP_NAMES:
        for t in (w, m, v):
            t[name] = jnp.transpose(t[name], (0, 2, 1))

    comm = _Overlap(w, x, me, jnp.stack([mc, 2 * mx + my]).astype(jnp.int32))
    small = {name: w[name] for name in SMALL_NAMES}

    loss_dev, grad_x, late_rows = _local_step(
        x.reshape(t_tok, D_MODEL), loss_target.reshape(t_tok, D_MODEL), comm, small, n_seq)
    loss = lax.psum(loss_dev[0, 0], ("x", "y", "c"))
    out, pieces = comm.finish(w, m, v)
    for name in UP_NAMES:
        out[name] = [jnp.transpose(a, (0, 2, 1)) for a in out[name]]

    pieces.append(_allgather_small(late_rows))
    spans = [(l, 0, LN1_ROWS + REST_ROWS) for l in reversed(range(1, DEPTH))] + [(0, LN1_ROWS, LN1_ROWS + REST_ROWS), (0, 0, LN1_ROWS)]

    def widen(a):
        return lax.dynamic_update_slice(jnp.zeros((3, D_CONV), F32), a, (0, me * (D_CONV // N_DEV)))

    packed = [[_pack_layer({**{name: t[name][l] for name in SMALL_NAMES}, "conv_w": widen(t["conv_w"][l])}) for l in range(DEPTH)]
              for t in (w, m, v)]
    rows_out = {}
    for (l, lo, hi), gathered_piece in zip(spans, pieces):
        rows_out[(l, lo)] = _adamw_small(gathered_piece, *[packed[t][l][lo:hi] for t in range(3)])
    per_layer = []
    for l in range(DEPTH):
        parts = sorted(lo for (ll, lo) in rows_out if ll == l)
        per_layer.append([_unpack_layer(jnp.concatenate([rows_out[(l, lo)][k] for lo in parts], axis=0)) for k in range(4)])
    for name in SMALL_NAMES:
        out[name] = [jnp.stack([per_layer[l][k][name] for l in range(DEPTH)]) for k in range(4)]
    lo_col = me * (D_CONV // N_DEV)
    out["conv_w"] = [jnp.stack([lax.dynamic_slice(per_layer[l][k]["conv_w"], (0, lo_col), (3, D_CONV // N_DEV)) for l in range(DEPTH)])
                     for k in range(4)]

    return (loss, grad_x.reshape(x.shape), *[out[name][0] for name in WEIGHT_ORDER], *[out[name][1] for name in WEIGHT_ORDER],
            *[out[name][2] for name in WEIGHT_ORDER], *[out[name][3] for name in WEIGHT_ORDER])
```

```python
import functools

import jax
import jax.numpy as jnp
from jax import lax
from jax.experimental import pallas as pl
from jax.experimental.pallas import tpu as pltpu

F32 = jnp.float32
BF16 = jnp.bfloat16
MESH = pl.DeviceIdType.MESH

N_DEV = 8
DEPTH = 2
D_MODEL = 1024
D_FF = 2816
FFN_BLK = 2 * D_FF // N_DEV
MXU_TILE_V7X = 256
FFN_CHUNKS = tuple((lo, min(lo + 3 * MXU_TILE_V7X, D_FF)) for lo in range(0, D_FF, 3 * MXU_TILE_V7X))
FFN_BWD_CHUNKS = tuple((lo, min(lo + 4 * MXU_TILE_V7X, D_FF)) for lo in range(0, D_FF, 4 * MXU_TILE_V7X))
DW_ROWS = D_FF // 2
D_CONV = 256
D_FOX = 512
N_HEADS = 8
D_SGU = 256
N_SGU_GROUPS = 4
SGU_CHUNK = 128
D_IN = 3 * D_CONV + 3 * D_FOX + N_HEADS + 2 * D_SGU
D_IN_SHARD = D_IN // N_DEV
COL_CONV, COL_QKV, COL_SGU, COL_F = 0, 768, 2304, 2816
D_IN_PAD = 2944
F_ORIG = 3 * D_CONV + 3 * D_FOX
ALPHA = (2 * DEPTH) ** 0.25
LN_EPS = 1e-5
ATT_SCALE = 0.125
ATT_BLK = 512
ATT_PAIRS = 2
NEG = -1e30

ADAM_LR, ADAM_B1, ADAM_B2, ADAM_EPS, ADAM_WD, ADAM_STEP = 0.001, 0.9, 0.999, 1e-08, 0.01, 10

VMEM_BYTES_V7X = 64 * 1024 * 1024
HIGHEST = lax.Precision.HIGHEST


def _params(vmem_mb, sem=None):
    assert vmem_mb * 1024 * 1024 < VMEM_BYTES_V7X
    kw = dict(vmem_limit_bytes=vmem_mb * 1024 * 1024)
    if sem is not None:
        kw["dimension_semantics"] = sem
    return pltpu.CompilerParams(**kw)


def _dot(a, b, precision=None):
    return lax.dot_general(a, b, (((1,), (0,)), ((), ())), preferred_element_type=F32, precision=precision)


def _dot_nt(a, b):
    return lax.dot_general(a, b, (((1,), (1,)), ((), ())), preferred_element_type=F32)


def _dot_tn(a, b):
    return lax.dot_general(a, b, (((0,), (0,)), ((), ())), preferred_element_type=F32)


def _ln_stats(z):
    mu = jnp.mean(z, axis=-1, keepdims=True)
    zc = z - mu
    var = jnp.mean(zc * zc, axis=-1, keepdims=True)
    rstd = lax.rsqrt(var + LN_EPS)
    return zc * rstd, rstd


def _ln_bwd(dy, xhat, rstd, g):
    dxh = dy * g
    m1 = jnp.mean(dxh, axis=-1, keepdims=True)
    m2 = jnp.mean(dxh * xhat, axis=-1, keepdims=True)
    return rstd * (dxh - m1 - xhat * m2)


_GELU_C = 0.7978845608028654


def _gelu(x):
    return 0.5 * x * (1.0 + jnp.tanh(_GELU_C * (x + 0.044715 * x * x * x)))


def _gelu_with_grad(x):
    t = jnp.tanh(_GELU_C * (x + 0.044715 * x * x * x))
    return 0.5 * x * (1.0 + t), 0.5 * (1.0 + t) + 0.5 * x * (1.0 - t * t) * _GELU_C * (1.0 + 3 * 0.044715 * x * x)


def _hbm(shape, dtype):
    n = 1
    for d in shape:
        n *= d
    if n * jnp.dtype(dtype).itemsize >= 1024 * 1024:
        return pltpu.HBM(tuple(shape), dtype)
    return jax.ShapeDtypeStruct(tuple(shape), dtype)


def _vspec():
    return pl.BlockSpec(memory_space=pltpu.VMEM)


def _anyspec():
    return pl.BlockSpec(memory_space=pl.ANY)


def _mesh_pos():
    return lax.axis_index("x"), lax.axis_index("y"), lax.axis_index("c")


def _other_chips(x, y):
    return [(1 - x, y), (x, 1 - y), (1 - x, 1 - y)]


_HBM_SPEC = pl.BlockSpec(memory_space=pltpu.HBM)
_SEM_SPEC = pl.BlockSpec(memory_space=pltpu.SEMAPHORE)
_DATAFLOW_EFFECT = pltpu.SideEffectType.DATAFLOW_SIDE_EFFECTING


def _remote_copies(plan, refs, send_sems, recv_sems):
    return [pltpu.make_async_remote_copy(src_ref=src, dst_ref=dst, send_sem=send_sems.at[k], recv_sem=recv_sems.at[k],
                                         device_id=to, device_id_type=MESH)
            for k, (src, dst, to) in enumerate(plan(refs, *_mesh_pos()))]


def _exchange_start(name, plan, n_copies, arrays, after):
    n = len(arrays)

    def body(*refs):
        send_sems, recv_sems, token = refs[n + 1], refs[n + 2], refs[-1]
        for cp in _remote_copies(plan, refs[:n], send_sems, recv_sems):
            cp.start()
        token[...] = jnp.zeros_like(token)

    out = pl.pallas_call(
        body, name=name,
        out_shape=(pltpu.SemaphoreType.DMA((n_copies,)), pltpu.SemaphoreType.DMA((n_copies,)),
                   *[pltpu.HBM(a.shape, a.dtype) for a in arrays], _hbm((8, 128), F32)),
        in_specs=[_HBM_SPEC] * n + [_anyspec()],
        out_specs=(_SEM_SPEC, _SEM_SPEC, *[_HBM_SPEC] * n, _vspec()),
        input_output_aliases={i: 2 + i for i in range(n)},
        compiler_params=pltpu.CompilerParams(has_side_effects=_DATAFLOW_EFFECT),
    )(*[pltpu.with_memory_space_constraint(a, pltpu.HBM) for a in arrays], after)
    return out[0], out[1], list(out[2:2 + n]), out[-1]


def _exchange_wait(name, plan, n_copies, started, after):
    send_sems, recv_sems, arrays, _ = started
    n = len(arrays)

    def body(*refs):
        for cp in _remote_copies(plan, refs[:n], refs[n], refs[n + 1]):
            cp.wait_send()
            cp.wait_recv()

    out = pl.pallas_call(
        body, name=name,
        out_shape=tuple(pltpu.HBM(a.shape, a.dtype) for a in arrays),
        in_specs=[_HBM_SPEC] * n + [_SEM_SPEC, _SEM_SPEC, _anyspec()], out_specs=tuple([_HBM_SPEC] * n),
        input_output_aliases={i: i for i in range(n)},
        compiler_params=pltpu.CompilerParams(has_side_effects=_DATAFLOW_EFFECT),
    )(*arrays, send_sems, recv_sems, after)
    return list(out)


def _gather_plan(m):
    def plan(refs, x, y, c):
        me = 4 * x + 2 * y + c
        return [(refs[i], refs[m + i].at[me], (*chip, c)) for i in range(m) for chip in _other_chips(x, y)]
    return plan


def _pass_on_plan(m):
    def plan(refs, x, y, c):
        out = []
        for i in range(m):
            out.append((refs[i], refs[m + i].at[4 * x + 2 * y + c], (x, y, 1 - c)))
            for cx, cy in _other_chips(x, y):
                block = refs[m + i].at[4 * cx + 2 * cy + c]
                out.append((block, block, (x, y, 1 - c)))
        return out
    return plan


def _peers_plan():
    def plan(refs, x, y, c):
        rel = [(dx, dy, dc) for dx in (0, 1) for dy in (0, 1) for dc in (0, 1)][1:]
        return [(refs[0], refs[1].at[4 * x + 2 * y + c], (x ^ dx, y ^ dy, c ^ dc)) for dx, dy, dc in rel]
    return plan


def _allgather_small(v):
    rows = v.shape[0]

    def body(v_ref, out_ref, send_sems, recv_sems):
        x, y, c = _mesh_pos()
        me = 4 * x + 2 * y + c
        out_ref[me] = v_ref[...]
        rel = [(dx, dy, dc) for dx in (0, 1) for dy in (0, 1) for dc in (0, 1)][1:]
        copies = []
        for k, (dx, dy, dc) in enumerate(rel):
            to = (x ^ dx, y ^ dy, c ^ dc)
            copies.append(pltpu.make_async_remote_copy(
                src_ref=v_ref, dst_ref=out_ref.at[me], send_sem=send_sems.at[k], recv_sem=recv_sems.at[k],
                device_id=to, device_id_type=MESH))
        for cp in copies:
            cp.start()
        for k, (dx, dy, dc) in enumerate(rel):
            src_blk = 4 * (x ^ dx) + 2 * (y ^ dy) + (c ^ dc)
            pltpu.make_async_remote_copy(
                src_ref=v_ref, dst_ref=out_ref.at[src_blk], send_sem=send_sems.at[k], recv_sem=recv_sems.at[k],
                device_id=(x, y, c), device_id_type=MESH).wait_recv()
        for cp in copies:
            cp.wait_send()

    return pl.pallas_call(
        body, name="allgather_small",
        out_shape=jax.ShapeDtypeStruct((N_DEV, rows, 128), v.dtype),
        in_specs=[_vspec()], out_specs=_vspec(),
        scratch_shapes=[pltpu.SemaphoreType.DMA((7,)), pltpu.SemaphoreType.DMA((7,))],
        compiler_params=_params(24),
    )(v)


def _sibling_plan(n):
    def plan(refs, x, y, c):
        return [(refs[a].at[2 * q + (1 - c)], refs[n + a].at[q], (x, y, 1 - c)) for a in range(n) for q in range(4)]
    return plan


def _chip_plan(n):
    def plan(refs, x, y, c):
        return [(refs[a].at[2 * cx + cy], refs[n + a].at[j], (cx, cy, c))
                for a in range(n) for j, (cx, cy) in enumerate(_other_chips(x, y))]
    return plan


def _row_tile(rows, cols, budget_bytes=2 * 1024 * 1024):
    best = 8
    for t in range(8, rows + 1, 8):
        if rows % t == 0 and t * cols * 4 <= budget_bytes:
            best = t
    return best


def _chip_partial(g, recv, where):
    _, rows, cols = g.shape
    tr = _row_tile(rows, cols)

    def body(where_ref, g_ref, r_ref, own_ref, o16_ref):
        s = g_ref[...] + r_ref[...]
        o16_ref[...] = s.astype(BF16)

        @pl.when(pl.program_id(1) == where_ref[1])
        def _():
            own_ref[...] = s

    blk = (None, tr, cols)
    return pl.pallas_call(
        body, name="rs_chip_partial",
        grid_spec=pltpu.PrefetchScalarGridSpec(
            num_scalar_prefetch=1, grid=(rows // tr, 4),
            in_specs=[pl.BlockSpec(blk, lambda i, q, w: (2 * q + w[0], i, 0)),
                      pl.BlockSpec(blk, lambda i, q, w: (q, i, 0))],
            out_specs=[pl.BlockSpec((tr, cols), lambda i, q, w: (i, 0)), pl.BlockSpec(blk, lambda i, q, w: (q, i, 0))]),
        out_shape=[_hbm((rows, cols), F32), _hbm((4, rows, cols), BF16)],
        compiler_params=_params(32),
    )(where, g, recv)


def _adam_math(w, g, m, v):
    m = ADAM_B1 * m + (1.0 - ADAM_B1) * g
    v = ADAM_B2 * v + (1.0 - ADAM_B2) * (g * g)
    m_hat = m / (1.0 - ADAM_B1 ** ADAM_STEP)
    v_hat = v / (1.0 - ADAM_B2 ** ADAM_STEP)
    delta = -ADAM_LR * (m_hat / (jnp.sqrt(v_hat) + ADAM_EPS) + ADAM_WD * w)
    return delta, m, v


def _adamw_shard(own32, recv16, w, m, v, layer, earlier):
    depth, rows, cols = w.shape
    tr = _row_tile(rows, cols, 1024 * 1024)
    n_prev = 0 if earlier is None else 4

    def body(p_ref, r_ref, w_ref, m_ref, v_ref, *rest):
        g_out, d_out, m_out, v_out = rest[n_prev:]
        g = p_ref[...] + r_ref[0].astype(F32) + r_ref[1].astype(F32) + r_ref[2].astype(F32)
        d, mn, vn = _adam_math(w_ref[...], g, m_ref[...], v_ref[...])
        g_out[...] = g
        d_out[...] = d
        m_out[...] = mn
        v_out[...] = vn

    mine = pl.BlockSpec((None, tr, cols), lambda i: (layer, i, 0))
    return pl.pallas_call(
        body, name="adamw_shard", grid=(rows // tr,),
        in_specs=[pl.BlockSpec((tr, cols), lambda i: (i, 0)), pl.BlockSpec((3, tr, cols), lambda i: (0, i, 0)),
                  mine, mine, mine] + [_anyspec()] * n_prev,
        out_specs=[mine] * 4,
        out_shape=[_hbm((depth, rows, cols), F32)] * 4,
        input_output_aliases={5 + k: k for k in range(n_prev)},
        compiler_params=_params(32),
    )(own32, recv16, *[pltpu.with_memory_space_constraint(t, pltpu.HBM) for t in (w, m, v)],
      *([] if earlier is None else earlier))


def _adamw_small(gathered, w, m, v):
    rows = w.shape[0]

    def body(a_ref, w_ref, m_ref, v_ref, g_out, d_out, m_out, v_out):
        g = a_ref[0]
        for d in range(1, N_DEV):
            g = g + a_ref[d]
        dl, mn, vn = _adam_math(w_ref[...], g, m_ref[...], v_ref[...])
        g_out[...] = g
        d_out[...] = dl
        m_out[...] = mn
        v_out[...] = vn

    return pl.pallas_call(
        body, name="adamw_small",
        in_specs=[_vspec()] * 4, out_specs=[_vspec()] * 4,
        out_shape=[_hbm((rows, 128), F32)] * 4,
        compiler_params=_params(32),
    )(gathered, w, m, v)


def _load_weights_once(pairs, sems):
    @pl.when(pl.program_id(0) == 0)
    def _():
        cps = [pltpu.make_async_copy(src, dst, sems.at[i]) for i, (src, dst) in enumerate(pairs)]
        for cp in cps:
            cp.start()
        for cp in cps:
            cp.wait()


def _ffn_fwd(x, wup, wd, ln_g, ln_b, after, target=None, tm=512):
    t_tok = x.shape[0]
    last = target is not None

    def body(x_ref, g_ref, b_ref, wup_hbm, wd_hbm, _after, *rest):
        if last:
            t_ref, dxn_ref, loss_ref, z_ref, gu_ref, wup_v, wd_v, sems = rest
        else:
            xn_ref, xnb_ref, z_ref, gu_ref, xb_ref, wup_v, wd_v, sems = rest
        _load_weights_once([(wup_hbm, wup_v), (wd_hbm, wd_v)], sems)
        xb = x_ref[...].astype(BF16)
        if not last:
            xb_ref[...] = xb
        y = None
        for lo, hi in FFN_CHUNKS:
            g = _dot_nt(xb, wup_v[0, lo:hi])
            u = _dot_nt(xb, wup_v[1, lo:hi])
            gu_ref[0, :, lo:hi] = g.astype(BF16)
            gu_ref[1, :, lo:hi] = u.astype(BF16)
            a = (g * jax.nn.sigmoid(g) * u).astype(BF16)
            part = _dot(a, wd_v[lo:hi])
            y = part if y is None else y + part
        z = ALPHA * x_ref[...] + 0.5 * y
        xhat, _ = _ln_stats(z)
        xn = xhat * g_ref[...] + b_ref[...]
        z_ref[...] = z
        if last:
            err = xn - t_ref[...]
            dxn_ref[...] = err * (1.0 / D_MODEL)
            part = jnp.sum(jnp.sum(err * err, axis=1, keepdims=True), axis=0, keepdims=True) * (0.5 / D_MODEL)

            @pl.when(pl.program_id(0) == 0)
            def _():
                loss_ref[...] = jnp.zeros_like(loss_ref)

            loss_ref[...] += part
        else:
            xn_ref[...] = xn
            xnb_ref[...] = xn.astype(BF16)

    tok = pl.BlockSpec((tm, D_MODEL), lambda i: (i, 0))
    vec = pl.BlockSpec((1, D_MODEL), lambda i: (0, 0))
    gu_spec = pl.BlockSpec((2, tm, D_FF), lambda i: (0, i, 0))
    gu_shape = _hbm((2, t_tok, D_FF), BF16)
    f32_tok, bf16_tok = _hbm((t_tok, D_MODEL), F32), _hbm((t_tok, D_MODEL), BF16)
    if last:
        extra_in, extra_spec = [target], [tok]
        out_specs = [tok, pl.BlockSpec((1, 128), lambda i: (0, 0)), tok, gu_spec]
        out_shape = [f32_tok, _hbm((1, 128), F32), f32_tok, gu_shape]
    else:
        extra_in, extra_spec = [], []
        out_specs = [tok, tok, tok, gu_spec, tok]
        out_shape = [f32_tok, bf16_tok, f32_tok, gu_shape, bf16_tok]
    return pl.pallas_call(
        body, name="ffn_fwd_loss" if last else "ffn_fwd", grid=(t_tok // tm,),
        in_specs=[tok, vec, vec, _anyspec(), _anyspec(), _anyspec()] + extra_spec,
        out_specs=out_specs, out_shape=out_shape,
        scratch_shapes=[pltpu.VMEM((2, D_FF, D_MODEL), BF16), pltpu.VMEM((D_FF, D_MODEL), BF16),
                        pltpu.SemaphoreType.DMA((2,))],
        compiler_params=_params(62, ("arbitrary",)),
    )(x, ln_g, ln_b, wup, wd, after, *extra_in)


def _ffn_bwd(dxn, z, gu, wup, wd, ln_g, after, tm=256):
    t_tok = dxn.shape[0]

    def body(dxn_ref, z_ref, gu_ref, g_ref, wup_hbm, wd_hbm, _after,
             dx_ref, dy_ref, a_ref, dgu_ref, dg_ref, db_ref, wup_v, wd_v, sems):
        refs = (dxn_ref, z_ref, gu_ref, g_ref, wup_hbm, wd_hbm, dx_ref, dy_ref, a_ref, dgu_ref, dg_ref, db_ref, wup_v, wd_v, sems)
        pl.when(pl.program_id(0) == 0)(functools.partial(step, True, *refs))
        pl.when(pl.program_id(0) > 0)(functools.partial(step, False, *refs))

    def step(first, dxn_ref, z_ref, gu_ref, g_ref, wup_hbm, wd_hbm,
             dx_ref, dy_ref, a_ref, dgu_ref, dg_ref, db_ref, wup_v, wd_v, sems):
        i = pl.program_id(0)
        if first:
            copies = [(pltpu.make_async_copy(wd_hbm.at[lo:hi], wd_v.at[lo:hi], sems.at[2 * k]),
                       pltpu.make_async_copy(wup_hbm.at[:, lo:hi], wup_v.at[:, lo:hi], sems.at[2 * k + 1]))
                      for k, (lo, hi) in enumerate(FFN_BWD_CHUNKS)]
            for pair in copies:
                for cp in pair:
                    cp.start()
        dxn_t = dxn_ref[...]
        xhat, rstd = _ln_stats(z_ref[...])
        pg = jnp.sum(dxn_t * xhat, axis=0, keepdims=True)
        pb = jnp.sum(dxn_t, axis=0, keepdims=True)

        @pl.when(i == 0)
        def _():
            dg_ref[...] = pg
            db_ref[...] = pb

        @pl.when(i > 0)
        def _():
            dg_ref[...] += pg
            db_ref[...] += pb

        dz = _ln_bwd(dxn_t, xhat, rstd, g_ref[...])
        dy = (0.5 * dz).astype(BF16)
        dy_ref[...] = dy
        dx = ALPHA * dz
        for k, (lo, hi) in enumerate(FFN_BWD_CHUNKS):
            if first:
                for cp in copies[k]:
                    cp.wait()
            da = _dot_nt(dy, wd_v[lo:hi])
            g = gu_ref[0, :, lo:hi].astype(F32)
            u = gu_ref[1, :, lo:hi].astype(F32)
            sig = jax.nn.sigmoid(g)
            silu = g * sig
            a_ref[:, lo:hi] = (silu * u).astype(BF16)
            dg = (da * u * (sig * (1.0 + g * (1.0 - sig)))).astype(BF16)
            du = (da * silu).astype(BF16)
            dgu_ref[0, :, lo:hi] = dg
            dgu_ref[1, :, lo:hi] = du
            dx = dx + _dot(dg, wup_v[0, lo:hi]) + _dot(du, wup_v[1, lo:hi])
        dx_ref[...] = dx

    tok = pl.BlockSpec((tm, D_MODEL), lambda i: (i, 0))
    vec = pl.BlockSpec((1, D_MODEL), lambda i: (0, 0))
    gu_spec = pl.BlockSpec((2, tm, D_FF), lambda i: (0, i, 0))
    return pl.pallas_call(
        body, name="ffn_bwd", grid=(t_tok // tm,),
        in_specs=[tok, tok, gu_spec, vec, _anyspec(), _anyspec(), _anyspec()],
        out_specs=[tok, tok, pl.BlockSpec((tm, D_FF), lambda i: (i, 0)), gu_spec, vec, vec],
        out_shape=[_hbm((t_tok, D_MODEL), F32), _hbm((t_tok, D_MODEL), BF16),
                   _hbm((t_tok, D_FF), BF16), _hbm((2, t_tok, D_FF), BF16),
                   _hbm((1, D_MODEL), F32), _hbm((1, D_MODEL), F32)],
        scratch_shapes=[pltpu.VMEM((2, D_FF, D_MODEL), BF16), pltpu.VMEM((D_FF, D_MODEL), BF16),
                        pltpu.SemaphoreType.DMA((2 * len(FFN_BWD_CHUNKS),))],
        compiler_params=_params(60, ("arbitrary",)),
    )(dxn, z, gu, ln_g, wup, wd, after)


def _matmul_tn(a, b, after, tk=4096, bm=None):
    ga, t_tok, m = a.shape
    gb, _, n = b.shape
    groups = max(ga, gb)
    tk = min(tk, t_tok)
    bm = m if bm is None else bm

    def body(a_ref, b_ref, _after, o_ref):
        p = _dot_tn(a_ref[...].astype(BF16), b_ref[...].astype(BF16))

        @pl.when(pl.program_id(2) == 0)
        def _():
            o_ref[...] = p

        @pl.when(pl.program_id(2) > 0)
        def _():
            o_ref[...] += p

    return pl.pallas_call(
        body, name=f"matmul_tn_{m}x{n}", grid=(groups, m // bm, t_tok // tk),
        in_specs=[pl.BlockSpec((None, tk, bm), (lambda g, i, t: (g, t, i)) if ga > 1 else (lambda g, i, t: (0, t, i))),
                  pl.BlockSpec((None, tk, n), (lambda g, i, t: (g, t, 0)) if gb > 1 else (lambda g, i, t: (0, t, 0))),
                  _anyspec()],
        out_specs=pl.BlockSpec((None, bm, n), lambda g, i, t: (g, i, 0)),
        out_shape=_hbm((groups, m, n), F32),
        compiler_params=_params(56, ("arbitrary", "arbitrary", "arbitrary")),
    )(a, b, after)


def _in_proj(x, w_in, tm=512):
    t_tok = x.shape[0]

    def body(x_ref, w_ref, conv_ref, qkv_ref, sgu_ref, f_ref):
        xb = x_ref[...].astype(BF16)
        conv_ref[...] = _dot(xb, w_ref[:, COL_CONV:COL_QKV])
        qkv_ref[...] = _dot(xb, w_ref[:, COL_QKV:COL_SGU]).astype(BF16)
        sgu_ref[...] = _dot(xb, w_ref[:, COL_SGU:COL_F])
        f_ref[...] = _dot(xb, w_ref[:, COL_F:D_IN_PAD])

    def tok(n):
        return pl.BlockSpec((tm, n), lambda i: (i, 0))

    return pl.pallas_call(
        body, name="mix_in_proj", grid=(t_tok // tm,),
        in_specs=[tok(D_MODEL), pl.BlockSpec((D_MODEL, D_IN_PAD), lambda i: (0, 0))],
        out_specs=[tok(768), tok(1536), tok(512), tok(128)],
        out_shape=[_hbm((t_tok, 768), F32), _hbm((t_tok, 1536), BF16),
                   _hbm((t_tok, 512), F32), _hbm((t_tok, 128), F32)],
        compiler_params=_params(48, ("arbitrary",)),
    )(x, w_in)


def _shift_down(a, k):
    row = lax.broadcasted_iota(jnp.int32, a.shape, 0)
    return jnp.where(row >= k, pltpu.roll(a, k, 0), 0.0)


def _shift_up(a, k):
    rows = a.shape[0]
    row = lax.broadcasted_iota(jnp.int32, a.shape, 0)
    return jnp.where(row < rows - k, pltpu.roll(a, rows - k, 0), 0.0)


def _tril(n):
    return lax.broadcasted_iota(jnp.int32, (n, n), 0) >= lax.broadcasted_iota(jnp.int32, (n, n), 1)


def _sgu_group_of_lane():
    return lax.broadcasted_iota(jnp.int32, (1, D_SGU), 1) // (D_SGU // N_SGU_GROUPS)


def _log_sigmoid(x):
    return jnp.minimum(x, 0.0) - jnp.log1p(jnp.exp(-jnp.abs(x)))


def _mix_mid_fwd(conv, sgu, f, conv_w, b_f, sgu_g, sgu_b, w_s, b_mat, n_seq):
    t_tok = conv.shape[0]
    seq = t_tok // n_seq
    n_chunk = seq // SGU_CHUNK
    per_blk = ATT_BLK // SGU_CHUNK

    def body(conv_ref, sgu_ref, f_ref, cw_ref, bf_ref, lg_ref, lb_ref, ws_ref, bm_ref, cat_ref, cum_ref):
        z = conv_ref[:, 256:512] * conv_ref[:, 512:768]
        y = cw_ref[0:1, :] * _shift_down(z, 2) + cw_ref[1:2, :] * _shift_down(z, 1) + cw_ref[2:3, :] * z
        cat_ref[:, 0:D_CONV] = (conv_ref[:, 0:256] * y).astype(BF16)
        cat_ref[:, D_CONV:D_CONV + D_FOX] = jnp.zeros((seq, D_FOX), BF16)

        tril = _tril(SGU_CHUNK)
        grp = _sgu_group_of_lane()
        wc = [jnp.where(tril, ws_ref[g], 0.0).astype(BF16) for g in range(N_SGU_GROUPS)]
        tri_f = tril.astype(F32)
        carry = jnp.zeros((1, 128), F32)
        for n in range(n_chunk):
            rows = pl.ds(n * SGU_CHUNK, SGU_CHUNK)
            u = _gelu(sgu_ref[rows, 0:256])
            vhat, _ = _ln_stats(_gelu(sgu_ref[rows, 256:512]))
            vn = (vhat * lg_ref[...] + lb_ref[...]).astype(BF16)
            mixed = bm_ref[...]
            for g in range(N_SGU_GROUPS):
                mixed = mixed + jnp.where(grp == g, _dot(wc[g], vn), 0.0)
            cat_ref[rows, D_CONV + D_FOX:D_MODEL] = (u * mixed).astype(BF16)

            log_f = _log_sigmoid(f_ref[rows, :] + bf_ref[...])
            cs = _dot(tri_f, log_f, HIGHEST) + carry
            carry = cs[SGU_CHUNK - 1:SGU_CHUNK, :]
            cs_t = cs.T
            lanes = pl.ds((n % per_blk) * SGU_CHUNK, SGU_CHUNK)
            for h in range(N_HEADS):
                cum_ref[h, n // per_blk, :, lanes] = cs_t[h:h + 1, :]

    def seq_blk(n):
        return pl.BlockSpec((seq, n), lambda b: (b, 0))

    def full(shape):
        return pl.BlockSpec(shape, lambda b: (0,) * len(shape))

    return pl.pallas_call(
        body, name="mix_mid_fwd", grid=(n_seq,),
        in_specs=[seq_blk(768), seq_blk(512), seq_blk(128), full((8, 256)), full((1, 128)), full((1, 256)),
                  full((1, 256)), full((4, 128, 128)), full((128, 256))],
        out_specs=[seq_blk(D_MODEL), pl.BlockSpec((N_HEADS, seq // ATT_BLK, 1, ATT_BLK), lambda b: (b, 0, 0, 0))],
        out_shape=[_hbm((t_tok, D_MODEL), BF16), _hbm((n_seq * N_HEADS, seq // ATT_BLK, 1, ATT_BLK), F32)],
        compiler_params=_params(48, ("arbitrary",)),
    )(conv, sgu, f, conv_w, b_f, sgu_g, sgu_b, w_s, b_mat)


def _head_masks():
    lane = lax.broadcasted_iota(jnp.int32, (1, 128), 1)
    return lane < 64, lane


def _fox_fwd(qkv, cum_t, cat, n_seq):
    t_tok = qkv.shape[0]
    seq = t_tok // n_seq
    nq = seq // ATT_BLK
    blk = ATT_BLK

    def body(q_ref, k_ref, v_ref, c_ref, _cat, o_ref, lse_ref):
        qi = pl.program_id(2)
        first, _ = _head_masks()
        one = jnp.ones((1, 128), BF16)
        qh = []
        for hp in range(ATT_PAIRS):
            qs = q_ref[:, 128 * hp:128 * hp + 128] * ATT_SCALE
            zero = jnp.zeros_like(qs)
            qh += [jnp.where(first, qs, zero), jnp.where(first, zero, qs)]

        def step(kb, carry, masked):
            ms, accs = carry
            rows = pl.ds(pl.multiple_of(kb * blk, blk), blk)
            new_m, new_acc = [], []
            for hp in range(ATT_PAIRS):
                k = k_ref[rows, 128 * hp:128 * hp + 128]
                v = v_ref[rows, 128 * hp:128 * hp + 128]
                for h in range(2):
                    i = 2 * hp + h
                    s = _dot_nt(qh[i], k) - c_ref[i, kb]
                    if masked:
                        s = jnp.where(causal, s, NEG)
                    m_new = jnp.maximum(ms[i], jnp.max(s, axis=1, keepdims=True))
                    p = jnp.exp(s - m_new)
                    vh = jnp.where(first, v, one) if h == 0 else jnp.where(first, one, v)
                    new_acc.append(accs[i] * jnp.exp(ms[i] - m_new) + _dot(p.astype(BF16), vh))
                    new_m.append(m_new)
            return tuple(new_m), tuple(new_acc)

        causal = _tril(blk)
        n_heads = 2 * ATT_PAIRS
        col = jnp.full((blk, 1), NEG, F32)
        zacc = jnp.zeros((blk, 128), F32)
        carry = lax.fori_loop(0, qi, lambda kb, cr: step(kb, cr, False), ((col,) * n_heads, (zacc,) * n_heads))
        ms, accs = step(qi, carry, True)
        for hp in range(ATT_PAIRS):
            acc0, acc1 = accs[2 * hp], accs[2 * hp + 1]
            l0 = pltpu.roll(acc0, 64, 1)
            l1 = pltpu.roll(acc1, 64, 1)
            o_ref[:, 128 * hp:128 * hp + 128] = jnp.where(first, acc0 / l0, acc1 / l1).astype(BF16)
            lse_ref[:, 128 * hp:128 * hp + 128] = jnp.where(first, ms[2 * hp] + jnp.log(l0), ms[2 * hp + 1] + jnp.log(l1))

    wide = 128 * ATT_PAIRS
    n_grp = D_FOX // wide
    first_col = D_CONV // wide
    return pl.pallas_call(
        body, name="fox_fwd", grid=(n_seq, n_grp, nq),
        in_specs=[pl.BlockSpec((blk, wide), lambda b, g, qi: (b * nq + qi, g)),
                  pl.BlockSpec((seq, wide), lambda b, g, qi: (b, n_grp + g)),
                  pl.BlockSpec((seq, wide), lambda b, g, qi: (b, 2 * n_grp + g)),
                  pl.BlockSpec((2 * ATT_PAIRS, nq, 1, blk), lambda b, g, qi: (b * n_grp + g, 0, 0, 0)), _anyspec()],
        out_specs=[pl.BlockSpec((blk, wide), lambda b, g, qi: (b * nq + qi, first_col + g)),
                   pl.BlockSpec((blk, wide), lambda b, g, qi: (b * nq + qi, g))],
        out_shape=[_hbm(cat.shape, BF16), _hbm((t_tok, D_FOX), F32)],
        input_output_aliases={4: 0},
        compiler_params=_params(48, ("arbitrary", "arbitrary", "arbitrary")),
    )(qkv, qkv, qkv, cum_t, cat)


def _fox_bwd(qkv, cum_t, cat, lse, d_o, n_seq):
    t_tok = qkv.shape[0]
    seq = t_tok // n_seq
    nk = seq // ATT_BLK
    blk = ATT_BLK

    def body(q_ref, k_ref, v_ref, c_ref, o_ref, lse_ref, do_ref, dq_ref, dk_ref, dv_ref, drow_ref, dcol_ref):
        kb = pl.program_id(2)
        first, lane = _head_masks()
        second = jnp.logical_not(first)
        one = jnp.ones((1, 128), BF16)
        causal = _tril(blk)

        @pl.when(kb == 0)
        def _():
            dq_ref[...] = jnp.zeros_like(dq_ref)
            drow_ref[...] = jnp.zeros_like(drow_ref)

        def step(qi, carry, masked):
            rows = pl.ds(pl.multiple_of(qi * blk, blk), blk)
            dks, dvs = carry
            new_dk, new_dv = [], []
            for hp in range(ATT_PAIRS):
                cols = slice(128 * hp, 128 * hp + 128)
                k = k_ref[:, cols]
                v = v_ref[:, cols]
                ks = k * ATT_SCALE
                zero = jnp.zeros_like(k)
                qs = q_ref[rows, cols] * ATT_SCALE
                d_o = do_ref[rows, cols]
                dd = d_o.astype(F32) * o_ref[rows, cols].astype(F32)
                lse_t = lse_ref[rows, cols]
                dq = []
                for h, mine in enumerate((first, second)):
                    i = 2 * hp + h
                    qh = jnp.where(mine, qs, zero)
                    doh = jnp.where(mine, d_o, zero)
                    delta = jnp.sum(jnp.where(mine, dd, 0.0), axis=1, keepdims=True)
                    lse_h = jnp.sum(jnp.where(lane == 64 * h, lse_t, 0.0), axis=1, keepdims=True)
                    s = _dot_nt(qh, k) - c_ref[i]
                    if masked:
                        s = jnp.where(causal, s, NEG)
                    p = jnp.exp(s - lse_h)
                    ds = (p * (_dot_nt(doh, v) - delta)).astype(BF16)
                    new_dk.append(dks[i] + _dot_tn(ds, jnp.where(mine, qs, one)))
                    new_dv.append(dvs[i] + _dot_tn(p.astype(BF16), doh))
                    dq.append(_dot(ds, jnp.where(mine, ks, one)))
                dq_ref[rows, cols] += jnp.where(first, dq[0], dq[1])
                drow_ref[rows, cols] += jnp.where(first, dq[1], dq[0])
            return tuple(new_dk), tuple(new_dv)

        zt = (jnp.zeros((blk, 128), F32),) * (2 * ATT_PAIRS)
        carry = step(kb, (zt, zt), True)
        dks, dvs = lax.fori_loop(kb + 1, nk, lambda qi, cr: step(qi, cr, False), carry)
        for hp in range(ATT_PAIRS):
            cols = slice(128 * hp, 128 * hp + 128)
            dk_ref[:, cols] = jnp.where(first, dks[2 * hp], dks[2 * hp + 1]).astype(BF16)
            dcol_ref[:, cols] = jnp.where(first, dks[2 * hp + 1], dks[2 * hp])
            dv_ref[:, cols] = (dvs[2 * hp] + dvs[2 * hp + 1]).astype(BF16)

    wide = 128 * ATT_PAIRS
    n_grp = D_FOX // wide

    def seq_spec(col0):
        return pl.BlockSpec((seq, wide), lambda b, g, kb: (b, col0 + g))

    def key_spec(col0):
        return pl.BlockSpec((blk, wide), lambda b, g, kb: (b * nk + kb, col0 + g))

    return pl.pallas_call(
        body, name="fox_bwd", grid=(n_seq, n_grp, nk),
        in_specs=[seq_spec(0), key_spec(n_grp), key_spec(2 * n_grp),
                  pl.BlockSpec((2 * ATT_PAIRS, None, 1, blk), lambda b, g, kb: (b * n_grp + g, kb, 0, 0)),
                  seq_spec(D_CONV // wide), seq_spec(0), seq_spec(0)],
        out_specs=[seq_spec(0), key_spec(0), key_spec(0), seq_spec(0), key_spec(0)],
        out_shape=[_hbm((t_tok, D_FOX), F32), _hbm((t_tok, D_FOX), BF16),
                   _hbm((t_tok, D_FOX), BF16), _hbm((t_tok, D_FOX), F32),
                   _hbm((t_tok, D_FOX), F32)],
        compiler_params=_params(56, ("arbitrary", "arbitrary", "arbitrary")),
    )(qkv, qkv, qkv, cum_t, cat, lse, d_o)


def _mix_out_fwd(cat, x, w_out, ln_g, ln_b, tm=512):
    t_tok = x.shape[0]

    def body(cat_ref, x_ref, w_ref, g_ref, b_ref, xn_ref, xnb_ref, z_ref):
        z = ALPHA * x_ref[...] + _dot(cat_ref[...], w_ref[...])
        xhat, _ = _ln_stats(z)
        xn = xhat * g_ref[...] + b_ref[...]
        z_ref[...] = z
        xn_ref[...] = xn
        xnb_ref[...] = xn.astype(BF16)

    def tok(n):
        return pl.BlockSpec((tm, n), lambda i: (i, 0))

    vec = pl.BlockSpec((1, D_MODEL), lambda i: (0, 0))
    return pl.pallas_call(
        body, name="mix_out_fwd", grid=(t_tok // tm,),
        in_specs=[tok(D_MODEL), tok(D_MODEL), pl.BlockSpec((D_MODEL, D_MODEL), lambda i: (0, 0)), vec, vec],
        out_specs=[tok(D_MODEL)] * 3,
        out_shape=[_hbm((t_tok, D_MODEL), F32), _hbm((t_tok, D_MODEL), BF16),
                   _hbm((t_tok, D_MODEL), F32)],
        compiler_params=_params(40, ("arbitrary",)),
    )(cat, x, w_out, ln_g, ln_b)


def _mix_out_bwd(dxn, z, w_out, ln_g, tm=512):
    t_tok = dxn.shape[0]

    def body(dxn_ref, z_ref, w_ref, g_ref, dz_ref, dzb_ref, dya_ref, dyb_ref, dyc_ref, dg_ref, db_ref):
        i = pl.program_id(0)
        dxn_t = dxn_ref[...]
        xhat, rstd = _ln_stats(z_ref[...])
        pg = jnp.sum(dxn_t * xhat, axis=0, keepdims=True)
        pb = jnp.sum(dxn_t, axis=0, keepdims=True)

        @pl.when(i == 0)
        def _():
            dg_ref[...] = pg
            db_ref[...] = pb

        @pl.when(i > 0)
        def _():
            dg_ref[...] += pg
            db_ref[...] += pb

        dz = _ln_bwd(dxn_t, xhat, rstd, g_ref[...])
        dzb = dz.astype(BF16)
        dz_ref[...] = dz
        dzb_ref[...] = dzb
        dya_ref[...] = _dot_nt(dzb, w_ref[0:256, :])
        dyb_ref[...] = _dot_nt(dzb, w_ref[256:768, :]).astype(BF16)
        dyc_ref[...] = _dot_nt(dzb, w_ref[768:1024, :])

    def tok(n):
        return pl.BlockSpec((tm, n), lambda i: (i, 0))

    vec = pl.BlockSpec((1, D_MODEL), lambda i: (0, 0))
    return pl.pallas_call(
        body, name="mix_out_bwd", grid=(t_tok // tm,),
        in_specs=[tok(D_MODEL), tok(D_MODEL), pl.BlockSpec((D_MODEL, D_MODEL), lambda i: (0, 0)), vec],
        out_specs=[tok(D_MODEL), tok(D_MODEL), tok(256), tok(512), tok(256), vec, vec],
        out_shape=[_hbm((t_tok, D_MODEL), F32), _hbm((t_tok, D_MODEL), BF16),
                   _hbm((t_tok, 256), F32), _hbm((t_tok, 512), BF16),
                   _hbm((t_tok, 256), F32),
                   _hbm((1, D_MODEL), F32), _hbm((1, D_MODEL), F32)],
        compiler_params=_params(40, ("arbitrary",)),
    )(dxn, z, w_out, ln_g)


def _conv_bwd(conv, dya, conv_w, n_seq):
    t_tok = conv.shape[0]
    seq = t_tok // n_seq

    def body(conv_ref, dya_ref, cw_ref, dconv_ref, dcw_ref):
        @pl.when(pl.program_id(0) == 0)
        def _():
            dcw_ref[...] = jnp.zeros_like(dcw_ref)

        z = conv_ref[:, 256:512] * conv_ref[:, 512:768]
        z1 = _shift_down(z, 1)
        z2 = _shift_down(z, 2)
        y = cw_ref[0:1, :] * z2 + cw_ref[1:2, :] * z1 + cw_ref[2:3, :] * z
        dya_t = dya_ref[...]
        dconv_ref[:, 0:256] = (dya_t * y).astype(BF16)
        dy = dya_t * conv_ref[:, 0:256]
        dcw_ref[0:1, :] += jnp.sum(dy * z2, axis=0, keepdims=True)
        dcw_ref[1:2, :] += jnp.sum(dy * z1, axis=0, keepdims=True)
        dcw_ref[2:3, :] += jnp.sum(dy * z, axis=0, keepdims=True)
        dz = cw_ref[2:3, :] * dy + cw_ref[1:2, :] * _shift_up(dy, 1) + cw_ref[0:1, :] * _shift_up(dy, 2)
        dconv_ref[:, 256:512] = (dz * conv_ref[:, 512:768]).astype(BF16)
        dconv_ref[:, 512:768] = (dz * conv_ref[:, 256:512]).astype(BF16)

    def seq_blk(n):
        return pl.BlockSpec((seq, n), lambda b: (b, 0))

    par = pl.BlockSpec((8, 256), lambda b: (0, 0))
    return pl.pallas_call(
        body, name="conv_bwd", grid=(n_seq,),
        in_specs=[seq_blk(768), seq_blk(256), par], out_specs=[seq_blk(768), par],
        out_shape=[_hbm((t_tok, 768), BF16), _hbm((8, 256), F32)],
        compiler_params=_params(56, ("arbitrary",)),
    )(conv, dya, conv_w)


def _sgu_gate_bwd(sgu, f, dyc, drow, dcol, b_f, sgu_g, sgu_b, w_s, b_mat, n_seq):
    t_tok = sgu.shape[0]
    seq = t_tok // n_seq
    n_chunk = seq // SGU_CHUNK

    def body(sgu_ref, f_ref, dyc_ref, drow_ref, dcol_ref, bf_ref, lg_ref, lb_ref, ws_ref, bm_ref,
             dsgu_ref, df_ref, dbf_ref, dlg_ref, dlb_ref, dws_ref, dbs_ref, dbm_acc):
        b = pl.program_id(0)

        @pl.when(b == 0)
        def _():
            for r in (dbf_ref, dlg_ref, dlb_ref, dws_ref, dbm_acc):
                r[...] = jnp.zeros_like(r)

        tril = _tril(SGU_CHUNK)
        grp = _sgu_group_of_lane()
        wc = [jnp.where(tril, ws_ref[g], 0.0).astype(BF16) for g in range(N_SGU_GROUPS)]
        for n in range(n_chunk):
            rows = pl.ds(n * SGU_CHUNK, SGU_CHUNK)
            su = sgu_ref[rows, 0:256]
            sv = sgu_ref[rows, 256:512]
            u, du = _gelu_with_grad(su)
            gv, dgv = _gelu_with_grad(sv)
            vhat, rstd = _ln_stats(gv)
            vn = (vhat * lg_ref[...] + lb_ref[...]).astype(BF16)
            mixed = bm_ref[...]
            for g in range(N_SGU_GROUPS):
                mixed = mixed + jnp.where(grp == g, _dot(wc[g], vn), 0.0)
            dyc_t = dyc_ref[rows, :]
            dsgu_ref[rows, 0:256] = (dyc_t * mixed * du).astype(BF16)
            dmixed = dyc_t * u
            dbm_acc[...] += dmixed
            dvn = jnp.zeros((SGU_CHUNK, D_SGU), F32)
            for g in range(N_SGU_GROUPS):
                dm_g = jnp.where(grp == g, dmixed, 0.0).astype(BF16)
                dws_ref[g] += _dot_nt(dm_g, vn)
                dvn = dvn + _dot_tn(wc[g], dm_g)
            dlg_ref[...] += jnp.sum(dvn * vhat, axis=0, keepdims=True)
            dlb_ref[...] += jnp.sum(dvn, axis=0, keepdims=True)
            dsgu_ref[rows, 256:512] = (_ln_bwd(dvn, vhat, rstd, lg_ref[...]) * dgv).astype(BF16)

        later = (lax.broadcasted_iota(jnp.int32, (128, 128), 0) <= lax.broadcasted_iota(jnp.int32, (128, 128), 1)).astype(F32)
        head = lax.broadcasted_iota(jnp.int32, (D_FOX, 128), 1)
        pick = (lax.broadcasted_iota(jnp.int32, (D_FOX, 128), 0) == 128 * (head // 2) + 64 * (1 - head % 2)).astype(F32)
        carry = jnp.zeros((1, 128), F32)
        for n in reversed(range(n_chunk)):
            rows = pl.ds(n * SGU_CHUNK, SGU_CHUNK)
            dcum_n = _dot(drow_ref[rows, :] - dcol_ref[rows, :], pick, HIGHEST)
            dlf = _dot(later, dcum_n, HIGHEST) + carry
            carry = carry + jnp.sum(dcum_n, axis=0, keepdims=True)
            df = dlf * jax.nn.sigmoid(-(f_ref[rows, :] + bf_ref[...]))
            df_ref[rows, :] = df.astype(BF16)
            dbf_ref[...] += jnp.sum(df, axis=0, keepdims=True)

        @pl.when(b == n_seq - 1)
        def _():
            for g in range(N_SGU_GROUPS):
                dws_ref[g] = jnp.where(tril, dws_ref[g], 0.0)
            sel = (lax.broadcasted_iota(jnp.int32, (D_SGU, 128), 0) // (D_SGU // N_SGU_GROUPS)
                   == lax.broadcasted_iota(jnp.int32, (D_SGU, 128), 1)).astype(F32)
            dbs_ref[...] = _dot(dbm_acc[...], sel, HIGHEST)

    def seq_blk(n):
        return pl.BlockSpec((seq, n), lambda b: (b, 0))

    def full(shape):
        return pl.BlockSpec(shape, lambda b: (0,) * len(shape))

    param_shapes = [(1, 128), (1, 256), (1, 256), (4, 128, 128), (128, 128)]
    return pl.pallas_call(
        body, name="sgu_gate_bwd", grid=(n_seq,),
        in_specs=[seq_blk(512), seq_blk(128), seq_blk(256), seq_blk(D_FOX), seq_blk(D_FOX),
                  full((1, 128)), full((1, 256)), full((1, 256)), full((4, 128, 128)), full((128, 256))],
        out_specs=[seq_blk(512), seq_blk(128)] + [full(s) for s in param_shapes],
        out_shape=[_hbm((t_tok, 512), BF16), _hbm((t_tok, 128), BF16)]
        + [_hbm(s, F32) for s in param_shapes],
        scratch_shapes=[pltpu.VMEM((128, 256), F32)],
        compiler_params=_params(48, ("arbitrary",)),
    )(sgu, f, dyc, drow, dcol, b_f, sgu_g, sgu_b, w_s, b_mat)


def _mix_in_bwd(dconv, dq, dk, dv, dsgu, df, dz, w_in, tm=512):
    t_tok = dz.shape[0]

    def body(dconv_ref, dq_ref, dk_ref, dv_ref, dsgu_ref, df_ref, dz_ref, w_ref, dx_ref, dp_ref):
        dqb = dq_ref[...].astype(BF16)
        pieces = [(COL_CONV, dconv_ref[...]), (COL_QKV, dqb), (COL_QKV + 512, dk_ref[...]), (COL_QKV + 1024, dv_ref[...]),
                  (COL_SGU, dsgu_ref[...]), (COL_F, df_ref[...])]
        dx = ALPHA * dz_ref[...]
        for col, val in pieces:
            width = val.shape[1]
            dp_ref[:, col:col + width] = val
            dx = dx + _dot_nt(val, w_ref[:, col:col + width])
        dx_ref[...] = dx

    def tok(n):
        return pl.BlockSpec((tm, n), lambda i: (i, 0))

    return pl.pallas_call(
        body, name="mix_in_bwd", grid=(t_tok // tm,),
        in_specs=[tok(768), tok(512), tok(512), tok(512), tok(512), tok(128), tok(D_MODEL),
                  pl.BlockSpec((D_MODEL, D_IN_PAD), lambda i: (0, 0))],
        out_specs=[tok(D_MODEL), tok(D_IN_PAD)],
        out_shape=[_hbm((t_tok, D_MODEL), F32), _hbm((t_tok, D_IN_PAD), BF16)],
        compiler_params=_params(48, ("arbitrary",)),
    )(dconv, dq, dk, dv, dsgu, df, dz, w_in)


def _pad_rows(a, rows):
    return jnp.pad(a, ((0, rows - a.shape[0]), (0, 0)))


F_BLOCK = F_ORIG // D_IN_SHARD
F_AT = F_ORIG - F_BLOCK * D_IN_SHARD
assert (F_ORIG + N_HEADS) // D_IN_SHARD == F_BLOCK


def _w_in_from_blocks(g):
    fb = g[F_BLOCK]
    zeros = jnp.zeros((D_MODEL, D_IN_PAD - COL_F - N_HEADS), g.dtype)
    return jnp.concatenate([g[d] for d in range(F_BLOCK)] + [fb[:, :F_AT], fb[:, F_AT + N_HEADS:]]
                           + [g[d] for d in range(F_BLOCK + 1, N_DEV)] + [fb[:, F_AT:F_AT + N_HEADS], zeros], axis=1)


def _w_in_to_blocks(dw):
    def cols(lo, hi):
        shift = 0 if hi <= F_ORIG else N_HEADS
        return dw[:, lo - shift:hi - shift]

    blocks = []
    for d in range(N_DEV):
        lo, hi = d * D_IN_SHARD, (d + 1) * D_IN_SHARD
        if d == F_BLOCK:
            blocks.append(jnp.concatenate([cols(lo, F_ORIG), dw[:, COL_F:COL_F + N_HEADS], cols(F_ORIG + N_HEADS, hi)], axis=1))
        else:
            blocks.append(cols(lo, hi))
    return jnp.stack(blocks)


LN1_ROWS = 2 * 8
REST_ROWS = 4 * 8 + 2 * 8 + 512 + 8 + 8 + 8


def _pack_rest(p):
    rows = [p[name].reshape(8, 128) for name in ("ln2_g", "ln2_b", "ln3_g", "ln3_b")]
    rows += [_pad_rows(p[name].reshape(2, 128), 8) for name in ("sgu_ln_g", "sgu_ln_b")]
    rows += [p["sgu_w_s"].reshape(512, 128), _pad_rows(p["sgu_b_s"], 8),
             _pad_rows(jnp.pad(p["fox_b_f"], (0, 128 - N_HEADS)).reshape(1, 128), 8), _pad_rows(p["conv_w"].reshape(6, 128), 8)]
    return jnp.concatenate(rows, axis=0)


def _pack_layer(p):
    return jnp.concatenate([p["ln1_g"].reshape(8, 128), p["ln1_b"].reshape(8, 128), _pack_rest(p)], axis=0)


def _unpack_layer(a):
    r = 0

    def take(n, valid):
        nonlocal r
        piece = a[r:r + valid]
        r += n
        return piece

    d = {}
    for name in ("ln1_g", "ln1_b", "ln2_g", "ln2_b", "ln3_g", "ln3_b"):
        d[name] = take(8, 8).reshape(D_MODEL)
    for name in ("sgu_ln_g", "sgu_ln_b"):
        d[name] = take(8, 2).reshape(D_SGU)
    d["sgu_w_s"] = take(512, 512).reshape(N_SGU_GROUPS, SGU_CHUNK, SGU_CHUNK)
    d["sgu_b_s"] = take(8, 4).reshape(N_SGU_GROUPS, SGU_CHUNK)
    d["fox_b_f"] = take(8, 1).reshape(128)[:N_HEADS]
    d["conv_w"] = take(8, 6).reshape(3, D_CONV)
    return d


SMALL_NAMES = ("ln1_g", "ln1_b", "fox_b_f", "sgu_ln_g", "sgu_ln_b", "sgu_w_s", "sgu_b_s", "ln2_g", "ln2_b", "ln3_g", "ln3_b")
BIG_NAMES = ("ffn1_w_up", "ffn1_w_down", "mix_w_in", "mix_w_out", "ffn2_w_up", "ffn2_w_down")
UP_NAMES = ("ffn1_w_up", "ffn2_w_up")
WEIGHT_ORDER = ("ln1_g", "ln1_b", "ffn1_w_up", "ffn1_w_down", "mix_w_in", "fox_b_f", "conv_w", "sgu_ln_g", "sgu_ln_b",
                "sgu_w_s", "sgu_b_s", "mix_w_out", "ln2_g", "ln2_b", "ffn2_w_up", "ffn2_w_down", "ln3_g", "ln3_b")


class _Overlap:
    def __init__(self, w, after, me, where):
        self.me, self.where = me, where
        self.last = after
        groups = [[("ffn1_w_up", 0), ("ffn1_w_down", 0)],
                  [("mix_w_in", 0), ("mix_w_out", 0), ("ffn2_w_up", 0), ("ffn2_w_down", 0)]]
        groups += [[(name, l) for name in BIG_NAMES] for l in range(1, DEPTH)]
        self.gathers = []
        for gi, group in enumerate(groups):
            shards = [w[name][l].astype(BF16) for name, l in group]
            lands = [lax.dynamic_update_slice(lax.empty((N_DEV,) + s.shape, BF16), s[None], (me, 0, 0)) for s in shards]
            started = self._start(f"allgather_start_{gi}", _gather_plan(len(group)), 3 * len(group), shards + lands)
            self.gathers.append(dict(group=group, chips=started))
            if gi == 0:
                width = D_CONV // N_DEV
                rows = jnp.pad(_pad_rows(w["conv_w"].reshape(DEPTH * 3, width), 8), ((0, 0), (0, 128 - width)))
                land = lax.dynamic_update_slice(lax.empty((N_DEV,) + rows.shape, F32), rows[None], (me, 0, 0))
                self.conv_started = self._start("conv_w_start", _peers_plan(), N_DEV - 1, [rows, land])
                self.conv_full = None
        self.all_started = self.last
        self.scatters = {}
        self.order = []
        self.small = []

    def conv_w(self, after):
        if self.conv_full is None:
            width = D_CONV // N_DEV
            gathered = _exchange_wait("conv_w_wait", _peers_plan(), N_DEV - 1, self.conv_started, after)[1]
            self.conv_full = jnp.transpose(gathered[:, :DEPTH * 3, :width], (1, 0, 2)).reshape(DEPTH, 3, D_CONV)
        return self.conv_full

    def _start(self, name, plan, n_copies, arrays):
        started = _exchange_start(name, plan, n_copies, arrays, self.last)
        self.last = started[3]
        return started

    def _group_of(self, layer, part):
        return layer + 1 if layer > 0 else (0 if part == "ffn1" else 1)

    def pass_on(self, layer, part, after):
        gi = self._group_of(layer, part)
        st = self.gathers[gi]
        if "sibling" not in st:
            m = len(st["group"])
            arrays = _exchange_wait(f"allgather_wait_{gi}", _gather_plan(m), 3 * m, st["chips"], after)
            first = [i for i, (name, _) in enumerate(st["group"]) if name == "mix_w_in"] if gi == 1 else []
            st["passes"] = [idx for idx in (first, [i for i in range(m) if i not in first]) if idx]
            st["sibling"] = [self._start(f"allgather_pass_start_{gi}_{k}", _pass_on_plan(len(idx)), 4 * len(idx),
                                         [arrays[i] for i in idx] + [arrays[m + i] for i in idx])
                             for k, idx in enumerate(st["passes"])]
            st["full"] = {}
        return st["sibling"][-1][3]

    def weights(self, layer, part, after):
        names = {"ffn1": ("ffn1_w_up", "ffn1_w_down"), "in": ("mix_w_in",),
                 "rest": ("mix_w_out", "ffn2_w_up", "ffn2_w_down")}[part]
        gi = self._group_of(layer, part)
        st = self.gathers[gi]
        after = self.all_started if after is None else after
        self.pass_on(layer, part, after)
        g = st["full"]
        if (names[0], layer) not in g:
            for k, idx in enumerate(st["passes"]):
                if st["group"].index((names[0], layer)) in idx:
                    arrays = _exchange_wait(f"allgather_pass_wait_{gi}_{k}", _pass_on_plan(len(idx)), 4 * len(idx),
                                            st["sibling"][k], after)
                    g.update(zip([st["group"][i] for i in idx], arrays[len(idx):]))

        def ffn(n):
            return g[(f"ffn{n}_w_up", layer)].reshape(2, D_FF, D_MODEL), g[(f"ffn{n}_w_down", layer)].reshape(D_FF, D_MODEL)

        if part == "ffn1":
            return ffn(1)
        if part == "in":
            return _w_in_from_blocks(g[("mix_w_in", layer)])
        return (g[("mix_w_out", layer)].reshape(D_MODEL, D_MODEL), *ffn(2))

    def push(self, key, items):
        n = len(items)
        grads = [g for _, _, g in items]
        lands = [lax.empty((4,) + g.shape[1:], F32) for g in grads]
        started = self._start(f"rs_sibling_start_{key[0]}{key[1]}", _sibling_plan(n), 4 * n, grads + lands)
        self.scatters[key] = dict(items=items, sibling=started)
        self.order.append(key)
        return started[3]

    def advance(self, key, after):
        st = self.scatters[key]
        n = len(st["items"])
        arrays = _exchange_wait(f"rs_sibling_wait_{key[0]}{key[1]}", _sibling_plan(n), 4 * n, st["sibling"], after)
        partials = [_chip_partial(g, r, self.where) for g, r in zip(arrays[:n], arrays[n:])]
        p16 = [p for _, p in partials]
        lands = [lax.empty((3,) + p.shape[1:], BF16) for p in p16]
        started = self._start(f"rs_chip_start_{key[0]}{key[1]}", _chip_plan(n), 3 * n, p16 + lands)
        st.update(own32=[p for p, _ in partials], chip=started)
        return started[3]

    def push_small(self, rows):
        k = len(self.small)
        land = lax.dynamic_update_slice(lax.empty((N_DEV,) + rows.shape, F32), rows[None], (self.me, 0, 0))
        started = self._start(f"small_start_{k}", _peers_plan(), N_DEV - 1, [rows, land])
        self.small.append(started)
        return started[3]

    def finish(self, w, m, v):
        res = {}
        after = self.scatters[self.order[-1]]["chip"][3]
        for key in self.order:
            st = self.scatters[key]
            n = len(st["items"])
            arrays = _exchange_wait(f"rs_chip_wait_{key[0]}{key[1]}", _chip_plan(n), 3 * n, st["chip"], after)
            for (name, l, _), own32, r16 in zip(st["items"], st["own32"], arrays[n:]):
                res[name] = _adamw_shard(own32, r16, w[name], m[name], v[name], l, res.get(name))
                after = res[name][0]
        pieces = [_exchange_wait(f"small_wait_{k}", _peers_plan(), N_DEV - 1, started, after)[1]
                  for k, started in enumerate(self.small)]
        return res, pieces


def _dw_up(dgu, x, after):
    return _matmul_tn(dgu, x[None], after, tk=2048, bm=DW_ROWS).reshape(N_DEV, FFN_BLK, D_MODEL)


def _dw_down(a, dy, after):
    return _matmul_tn(a[None], dy[None], after, tk=2048, bm=DW_ROWS).reshape(N_DEV, FFN_BLK // 2, D_MODEL)


def _local_step(x, target, comm, small, n_seq):
    def vec(a):
        return a.reshape(1, -1)

    saved = []
    h = x
    for l in range(DEPTH):
        s = {}
        s["up1"], s["down1"] = comm.weights(l, "ffn1", None if l == 0 else h)
        h1, h1b, s["z1"], s["gu1"], s["x0b"] = _ffn_fwd(h, s["up1"], s["down1"], vec(small["ln1_g"][l]), vec(small["ln1_b"][l]), h)
        s["w_in"] = comm.weights(l, "in", s["z1"])
        s["x1b"] = h1b
        conv, qkv, sgu, f = _in_proj(h1, s["w_in"])
        s["w_out"], s["up2"], s["down2"] = comm.weights(l, "rest", f)
        cw = _pad_rows(comm.conv_w(h1)[l], 8)
        bf = jnp.pad(small["fox_b_f"][l], (0, 128 - N_HEADS)).reshape(1, 128)
        b_mat = jnp.repeat(small["sgu_b_s"][l].T, D_SGU // N_SGU_GROUPS, axis=1)
        mid_params = (cw, bf, vec(small["sgu_ln_g"][l]), vec(small["sgu_ln_b"][l]), small["sgu_w_s"][l], b_mat)
        cat, cum_t = _mix_mid_fwd(conv, sgu, f, *mid_params, n_seq)
        cat, lse = _fox_fwd(qkv, cum_t, cat, n_seq)
        h2, h2b, s["z2"] = _mix_out_fwd(cat, h1, s["w_out"], vec(small["ln2_g"][l]), vec(small["ln2_b"][l]))
        s.update(conv=conv, qkv=qkv, sgu=sgu, f=f, mid_params=mid_params, cat=cat, cum_t=cum_t, lse=lse, x2b=h2b)
        token = comm.pass_on(l + 1, "ffn1", s["z2"]) if l + 1 < DEPTH else h2
        ln3 = (vec(small["ln3_g"][l]), vec(small["ln3_b"][l]))
        if l + 1 < DEPTH:
            h, _, s["z3"], s["gu2"], _ = _ffn_fwd(h2, s["up2"], s["down2"], *ln3, token)
        else:
            dh, loss, s["z3"], s["gu2"] = _ffn_fwd(h2, s["up2"], s["down2"], *ln3, token, target)
        saved.append(s)

    late_rows = None
    token = loss
    pending = None
    for l in reversed(range(DEPTH)):
        s = saved[l]
        sg = {}
        dh, dy, a, dgu, sg["ln3_g"], sg["ln3_b"] = _ffn_bwd(dh, s["z3"], s["gu2"], s["up2"], s["down2"], vec(small["ln3_g"][l]), token)
        if pending is not None:
            token = comm.advance(pending, dh)
        g_up2 = _dw_up(dgu, s["x2b"], token)
        g_down2 = _dw_down(a, dy, token)
        dz, dzb, dya, dyb, dyc, sg["ln2_g"], sg["ln2_b"] = _mix_out_bwd(dh, s["z2"], s["w_out"], vec(small["ln2_g"][l]))
        g_out = _matmul_tn(s["cat"][None], dzb[None], token, tk=2048, bm=D_MODEL // 2).reshape(N_DEV, D_MODEL // N_DEV, D_MODEL)
        dq, dk, dv, drow, dcol = _fox_bwd(s["qkv"], s["cum_t"], s["cat"], s["lse"], dyb, n_seq)
        dconv, dcw = _conv_bwd(s["conv"], dya, s["mid_params"][0], n_seq)
        dsgu, df, dbf, dlg, dlb, dws, dbs = _sgu_gate_bwd(s["sgu"], s["f"], dyc, drow, dcol, *s["mid_params"][1:], n_seq)
        sg.update(conv_w=dcw[:3], fox_b_f=dbf[0, :N_HEADS], sgu_ln_g=dlg[0], sgu_ln_b=dlb[0], sgu_w_s=dws,
                  sgu_b_s=dbs[:, :N_SGU_GROUPS].T)
        dh, dp = _mix_in_bwd(dconv, dq, dk, dv, dsgu, df, dz, s["w_in"])
        g_in = _w_in_to_blocks(_matmul_tn(s["x1b"][None], dp[None], token, tk=2048, bm=D_MODEL // 2)[0])
        first = [("ffn2_w_up", l, g_up2), ("ffn2_w_down", l, g_down2), ("mix_w_out", l, g_out), ("mix_w_in", l, g_in)]
        for name in ("ln2_g", "ln2_b", "ln3_g", "ln3_b"):
            sg[name] = sg[name][0]
        if l == 0:
            comm.push((l, "a"), first)
            token = comm.push_small(_pack_rest(sg))
            pending, first = (l, "a"), []
        dh, dy, a, dgu, dg1, db1 = _ffn_bwd(dh, s["z1"], s["gu1"], s["up1"], s["down1"], vec(small["ln1_g"][l]), token)
        if l == 0:
            token = comm.advance(pending, dh)
        g_up1 = _dw_up(dgu, s["x0b"], token)
        ln1_rows = jnp.concatenate([dg1.reshape(8, 128), db1.reshape(8, 128)], axis=0)
        if l == 0:
            token = comm.push((l, "b"), [("ffn1_w_up", l, g_up1)])
            g_down1 = _dw_down(a, dy, token)
            token = comm.advance((l, "b"), g_down1)
            token = comm.push((l, "c"), [("ffn1_w_down", l, g_down1)])
            token = comm.advance((l, "c"), token)
            late_rows = ln1_rows
        else:
            g_down1 = _dw_down(a, dy, token)
            pending = (l, "b")
            comm.push(pending, first + [("ffn1_w_up", l, g_up1), ("ffn1_w_down", l, g_down1)])
            token = comm.push_small(jnp.concatenate([ln1_rows, _pack_rest(sg)], axis=0))
    return loss, dh, late_rows


def kernel(x, ln1_g, ln1_b, ffn1_w_up, ffn1_w_down, mix_w_in, fox_b_f, conv_w, sgu_ln_g, sgu_ln_b, sgu_w_s, sgu_b_s, mix_w_out, ln2_g, ln2_b, ffn2_w_up, ffn2_w_down, ln3_g, ln3_b, loss_target, m_ln1_g, m_ln1_b, m_ffn1_w_up, m_ffn1_w_down, m_mix_w_in, m_fox_b_f, m_conv_w, m_sgu_ln_g, m_sgu_ln_b, m_sgu_w_s, m_sgu_b_s, m_mix_w_out, m_ln2_g, m_ln2_b, m_ffn2_w_up, m_ffn2_w_down, m_ln3_g, m_ln3_b, v_ln1_g, v_ln1_b, v_ffn1_w_up, v_ffn1_w_down, v_mix_w_in, v_fox_b_f, v_conv_w, v_sgu_ln_g, v_sgu_ln_b, v_sgu_w_s, v_sgu_b_s, v_mix_w_out, v_ln2_g, v_ln2_b, v_ffn2_w_up, v_ffn2_w_down, v_ln3_g, v_ln3_b):
    w = dict(ln1_g=ln1_g, ln1_b=ln1_b, ffn1_w_up=ffn1_w_up, ffn1_w_down=ffn1_w_down, mix_w_in=mix_w_in, fox_b_f=fox_b_f,
             conv_w=conv_w, sgu_ln_g=sgu_ln_g, sgu_ln_b=sgu_ln_b, sgu_w_s=sgu_w_s, sgu_b_s=sgu_b_s, mix_w_out=mix_w_out,
             ln2_g=ln2_g, ln2_b=ln2_b, ffn2_w_up=ffn2_w_up, ffn2_w_down=ffn2_w_down, ln3_g=ln3_g, ln3_b=ln3_b)
    m = dict(ln1_g=m_ln1_g, ln1_b=m_ln1_b, ffn1_w_up=m_ffn1_w_up, ffn1_w_down=m_ffn1_w_down, mix_w_in=m_mix_w_in,
             fox_b_f=m_fox_b_f, conv_w=m_conv_w, sgu_ln_g=m_sgu_ln_g, sgu_ln_b=m_sgu_ln_b, sgu_w_s=m_sgu_w_s,
             sgu_b_s=m_sgu_b_s, mix_w_out=m_mix_w_out, ln2_g=m_ln2_g, ln2_b=m_ln2_b, ffn2_w_up=m_ffn2_w_up,
             ffn2_w_down=m_ffn2_w_down, ln3_g=m_ln3_g, ln3_b=m_ln3_b)
    v = dict(ln1_g=v_ln1_g, ln1_b=v_ln1_b, ffn1_w_up=v_ffn1_w_up, ffn1_w_down=v_ffn1_w_down, mix_w_in=v_mix_w_in,
             fox_b_f=v_fox_b_f, conv_w=v_conv_w, sgu_ln_g=v_sgu_ln_g, sgu_ln_b=v_sgu_ln_b, sgu_w_s=v_sgu_w_s,
             sgu_b_s=v_sgu_b_s, mix_w_out=v_mix_w_out, ln2_g=v_ln2_g, ln2_b=v_ln2_b, ffn2_w_up=v_ffn2_w_up,
             ffn2_w_down=v_ffn2_w_down, ln3_g=v_ln3_g, ln3_b=v_ln3_b)

    mx, my, mc = lax.axis_index("x"), lax.axis_index("y"), lax.axis_index("c")
    me = 4 * mx + 2 * my + mc
    n_seq, seq, _ = x.shape
    t_tok = n_seq * seq
    for name in UP_NAMES:
        for t in (w, m, v):
            t[name] = jnp.transpose(t[name], (0, 2, 1))

    comm = _Overlap(w, x, me, jnp.stack([mc, 2 * mx + my]).astype(jnp.int32))
    small = {name: w[name] for name in SMALL_NAMES}

    loss_dev, grad_x, late_rows = _local_step(
        x.reshape(t_tok, D_MODEL), loss_target.reshape(t_tok, D_MODEL), comm, small, n_seq)
    loss = lax.psum(loss_dev[0, 0], ("x", "y", "c"))
    out, pieces = comm.finish(w, m, v)
    for name in UP_NAMES:
        out[name] = [jnp.transpose(a, (0, 2, 1)) for a in out[name]]

    pieces.append(_allgather_small(late_rows))
    spans = [(l, 0, LN1_ROWS + REST_ROWS) for l in reversed(range(1, DEPTH))] + [(0, LN1_ROWS, LN1_ROWS + REST_ROWS), (0, 0, LN1_ROWS)]

    def widen(a):
        return lax.dynamic_update_slice(jnp.zeros((3, D_CONV), F32), a, (0, me * (D_CONV // N_DEV)))

    packed = [[_pack_layer({**{name: t[name][l] for name in SMALL_NAMES}, "conv_w": widen(t["conv_w"][l])}) for l in range(DEPTH)]
              for t in (w, m, v)]
    rows_out = {}
    for (l, lo, hi), gathered_piece in zip(spans, pieces):
        rows_out[(l, lo)] = _adamw_small(gathered_piece, *[packed[t][l][lo:hi] for t in range(3)])
    per_layer = []
    for l in range(DEPTH):
        parts = sorted(lo for (ll, lo) in rows_out if ll == l)
        per_layer.append([_unpack_layer(jnp.concatenate([rows_out[(l, lo)][k] for lo in parts], axis=0)) for k in range(4)])
    for name in SMALL_NAMES:
        out[name] = [jnp.stack([per_layer[l][k][name] for l in range(DEPTH)]) for k in range(4)]
    lo_col = me * (D_CONV // N_DEV)
    out["conv_w"] = [jnp.stack([lax.dynamic_slice(per_layer[l][k]["conv_w"], (0, lo_col), (3, D_CONV // N_DEV)) for l in range(DEPTH)])
                     for k in range(4)]

    return (loss, grad_x.reshape(x.shape), *[out[name][0] for name in WEIGHT_ORDER], *[out[name][1] for name in WEIGHT_ORDER],
            *[out[name][2] for name in WEIGHT_ORDER], *[out[name][3] for name in WEIGHT_ORDER])
```
